```python
import jax, jax.numpy as jnp
from jax import lax
import numpy as np

D_MODEL = 2048
BATCH = 8
SEQ = 4096
DEPTH = 1

CHUNK = 64
Q_BLOCK = 128
RET_HEADS = D_MODEL // 256
RET_DK = 128
RET_DV = 128
RET_QK = RET_HEADS * RET_DK
RET_WIDTH = RET_HEADS * RET_DV
MLA_HEADS = D_MODEL // 256
MLA_NOPE = 128
MLA_ROPE = 64
MLA_DV = 128
MLA_Q_RANK = 768
MLA_KV_RANK = 512
MLA_WIDTH = MLA_HEADS * MLA_DV
MIX_WIDTH = RET_WIDTH + MLA_WIDTH
IN_SIZES = (RET_QK, RET_QK, RET_WIDTH, RET_WIDTH, MLA_Q_RANK, MLA_KV_RANK, MLA_ROPE)
IN_WIDTH = RET_QK * 2 + RET_WIDTH * 2 + MLA_Q_RANK + MLA_KV_RANK + MLA_ROPE
D_FF = 4 * D_MODEL
ROPE_BASE = 10000.0
EPS = 1e-5
DEEPNORM_ALPHA = (2.0 * DEPTH) ** 0.25
DEEPNORM_BETA = (8.0 * DEPTH) ** -0.25

kernel_name = 'hymba_retention_mla_deepnorm_block'


def _layer_norm(x, g, b):
    xf = x.astype(jnp.float32)
    mu = jnp.mean(xf, axis=-1, keepdims=True)
    var = jnp.mean(jnp.square(xf - mu), axis=-1, keepdims=True)
    return ((xf - mu) * lax.rsqrt(var + EPS) * g + b).astype(x.dtype)


def _rms_norm(x, g):
    xf = x.astype(jnp.float32)
    return (xf * lax.rsqrt(jnp.mean(jnp.square(xf), axis=-1, keepdims=True) + EPS) * g).astype(x.dtype)


def _rope(x, positions):
    d = x.shape[-1]
    inv = ROPE_BASE ** (-jnp.arange(0, d, 2, dtype=jnp.float32) / d)
    ang = positions.astype(jnp.float32)[..., None] * inv
    ang = ang.reshape(ang.shape[:2] + (1,) * (x.ndim - 3) + ang.shape[-1:])
    cos, sin = jnp.cos(ang), jnp.sin(ang)
    xf = x.astype(jnp.float32)
    x1, x2 = xf[..., : d // 2], xf[..., d // 2:]
    return jnp.concatenate([x1 * cos - x2 * sin, x1 * sin + x2 * cos], axis=-1).astype(x.dtype)


def _retention(q, k, v):
    B, S, H, dk = q.shape
    dv = v.shape[-1]
    nc = S // CHUNK
    dt = q.dtype
    q = q.reshape(B, nc, CHUNK, H, dk).transpose(0, 3, 1, 2, 4)
    k = k.reshape(B, nc, CHUNK, H, dk).transpose(0, 3, 1, 2, 4) * (dk ** -0.5)
    v = v.reshape(B, nc, CHUNK, H, dv).transpose(0, 3, 1, 2, 4)
    log_g = jnp.log1p(-jnp.exp2(-5.0 - jnp.arange(H, dtype=jnp.float32)))
    idx = jnp.arange(CHUNK, dtype=jnp.float32)
    intra_decay = jnp.exp(log_g[:, None, None] * jnp.abs(idx[:, None] - idx[None, :])).astype(dt)
    k_decay = jnp.exp(log_g[:, None] * (CHUNK - 1 - idx)).astype(dt)
    q_decay = jnp.exp(log_g[:, None] * (idx + 1.0)).astype(dt)
    chunk_decay = jnp.exp(log_g * CHUNK).astype(dt)[:, None, None]
    scores = jnp.einsum('bhnid,bhnjd->bhnij', q, k) * intra_decay[:, None]
    intra = jnp.einsum('bhnij,bhnje->bhnie', scores, v)
    kv = jnp.einsum('bhnjd,bhnje->nbhde', k * k_decay[:, None, :, None], v)

    def step(state, kv_c):
        return chunk_decay * state + kv_c, state

    _, s_prev = lax.scan(step, jnp.zeros((B, H, dk, dv), kv.dtype), kv)
    cross = jnp.einsum('bhnid,nbhde->bhnie', q * q_decay[:, None, :, None], s_prev)
    return (intra + cross).transpose(0, 2, 3, 1, 4).reshape(B, S, H, dv)


def _mla(c_q, c_kv, k_r, positions, q_norm_g, w_uq, kv_norm_g, w_uk, w_uv):
    B, S, _ = c_q.shape
    H = MLA_HEADS
    q = (_rms_norm(c_q, q_norm_g) @ w_uq).reshape(B, S, H, MLA_NOPE + MLA_ROPE)
    q_nope = q[..., :MLA_NOPE]
    q_rope = _rope(q[..., MLA_NOPE:], positions)
    ckv = _rms_norm(c_kv, kv_norm_g)
    k_nope = (ckv @ w_uk).reshape(B, S, H, MLA_NOPE)
    v = (ckv @ w_uv).reshape(B, S, H, MLA_DV)
    k_rope = _rope(k_r, positions)
    scale = (MLA_NOPE + MLA_ROPE) ** -0.5
    nqb = S // Q_BLOCK
    qn = q_nope.reshape(B, nqb, Q_BLOCK, H, MLA_NOPE).transpose(1, 0, 3, 2, 4)
    qr = q_rope.reshape(B, nqb, Q_BLOCK, H, MLA_ROPE).transpose(1, 0, 3, 2, 4)
    k_chunk = jnp.arange(S) // CHUNK

    def block(args):
        qn_b, qr_b, start = args
        s = (jnp.einsum('bhqd,bkhd->bhqk', qn_b, k_nope)
             + jnp.einsum('bhqd,bkd->bhqk', qr_b, k_rope)).astype(jnp.float32) * scale
        q_chunk = (start + jnp.arange(Q_BLOCK)) // CHUNK
        mask = k_chunk[None, :] <= q_chunk[:, None]
        p = jax.nn.softmax(jnp.where(mask, s, -jnp.inf), axis=-1).astype(v.dtype)
        return jnp.einsum('bhqk,bkhd->bqhd', p, v)

    out = lax.map(block, (qn, qr, jnp.arange(nqb, dtype=jnp.int32) * Q_BLOCK))
    return out.transpose(1, 0, 2, 3, 4).reshape(B, S, H * MLA_DV)


def _fwd_setup_inputs(seed: int = 0) -> dict:
    key = jax.random.key(seed)
    ks = jax.random.split(key, 16)
    f32 = jnp.float32
    x = jax.random.normal(ks[0], (BATCH, SEQ, D_MODEL), f32)
    offset = jax.random.randint(ks[1], (BATCH, 1), 0, 1024, dtype=jnp.int32)
    positions = offset + jnp.arange(SEQ, dtype=jnp.int32)[None, :]
    col_scale = jnp.concatenate([
        jnp.ones((2 * RET_QK,), f32),
        jnp.full((RET_WIDTH,), DEEPNORM_BETA, f32),
        jnp.ones((IN_WIDTH - 2 * RET_QK - RET_WIDTH,), f32)])
    w_in = jax.random.normal(ks[2], (DEPTH, D_MODEL, IN_WIDTH), f32) * (D_MODEL ** -0.5) * col_scale
    q_norm_g = 1.0 + 0.02 * jax.random.normal(ks[3], (DEPTH, MLA_Q_RANK), f32)
    w_uq = jax.random.normal(ks[4], (DEPTH, MLA_Q_RANK, MLA_HEADS * (MLA_NOPE + MLA_ROPE)), f32) * (MLA_Q_RANK ** -0.5)
    kv_norm_g = 1.0 + 0.02 * jax.random.normal(ks[5], (DEPTH, MLA_KV_RANK), f32)
    w_uk = jax.random.normal(ks[6], (DEPTH, MLA_KV_RANK, MLA_HEADS * MLA_NOPE), f32) * (MLA_KV_RANK ** -0.5)
    w_uv = jax.random.normal(ks[7], (DEPTH, MLA_KV_RANK, MLA_HEADS * MLA_DV), f32) * (MLA_KV_RANK ** -0.5) * DEEPNORM_BETA
    ret_gn_g = 1.0 + 0.02 * jax.random.normal(ks[8], (DEPTH, RET_WIDTH), f32)
    w_out = jax.random.normal(ks[9], (DEPTH, MIX_WIDTH, D_MODEL), f32) * (MIX_WIDTH ** -0.5) * DEEPNORM_BETA
    ln1_g = 1.0 + 0.02 * jax.random.normal(ks[10], (DEPTH, D_MODEL), f32)
    ln1_b = 0.02 * jax.random.normal(ks[11], (DEPTH, D_MODEL), f32)
    w_up = jax.random.normal(ks[12], (DEPTH, D_MODEL, D_FF), f32) * (D_MODEL ** -0.5) * DEEPNORM_BETA
    w_down = jax.random.normal(ks[13], (DEPTH, D_FF, D_MODEL), f32) * (D_FF ** -0.5) * DEEPNORM_BETA
    ln2_g = 1.0 + 0.02 * jax.random.normal(ks[14], (DEPTH, D_MODEL), f32)
    ln2_b = 0.02 * jax.random.normal(ks[15], (DEPTH, D_MODEL), f32)
    return {'x': x, 'positions': positions, 'w_in': w_in, 'q_norm_g': q_norm_g, 'w_uq': w_uq,
            'kv_norm_g': kv_norm_g, 'w_uk': w_uk, 'w_uv': w_uv, 'ret_gn_g': ret_gn_g,
            'w_out': w_out, 'ln1_g': ln1_g, 'ln1_b': ln1_b, 'w_up': w_up, 'w_down': w_down,
            'ln2_g': ln2_g, 'ln2_b': ln2_b}


def _fwd_reference(x, positions, w_in, q_norm_g, w_uq, kv_norm_g, w_uk, w_uv, ret_gn_g,
              w_out, ln1_g, ln1_b, w_up, w_down, ln2_g, ln2_b):
    B, S, _ = x.shape
    split_idx = [int(i) for i in np.cumsum(IN_SIZES)[:-1]]
    for l in range(DEPTH):
        proj = x @ w_in[l]
        rq, rk, rv, rg, cq, ckv, kr = jnp.split(proj, split_idx, axis=-1)
        rq = _rope(rq.reshape(B, S, RET_HEADS, RET_DK), positions)
        rk = _rope(rk.reshape(B, S, RET_HEADS, RET_DK), positions)
        ret = _retention(rq, rk, rv.reshape(B, S, RET_HEADS, RET_DV)).astype(jnp.float32)
        mu = jnp.mean(ret, axis=-1, keepdims=True)
        var = jnp.mean(jnp.square(ret - mu), axis=-1, keepdims=True)
        ret = ((ret - mu) * lax.rsqrt(var + EPS)).reshape(B, S, RET_WIDTH) * ret_gn_g[l]
        ret_out = (jax.nn.silu(rg.astype(jnp.float32)) * ret).astype(x.dtype)
        mla_out = _mla(cq, ckv, kr, positions, q_norm_g[l], w_uq[l], kv_norm_g[l], w_uk[l], w_uv[l])
        mix = jnp.concatenate([ret_out, mla_out], axis=-1) @ w_out[l]
        x = _layer_norm(DEEPNORM_ALPHA * x + mix, ln1_g[l], ln1_b[l])
        h = jnp.square(jax.nn.relu(x @ w_up[l])) @ w_down[l]
        x = _layer_norm(DEEPNORM_ALPHA * x + h, ln2_g[l], ln2_b[l])
    return x


import jax as _jax
import jax.numpy as _jnp

TWIN_FORMAT = 'train_step'
FWD_PARAMS = ['x', 'positions', 'w_in', 'q_norm_g', 'w_uq', 'kv_norm_g', 'w_uk', 'w_uv', 'ret_gn_g', 'w_out', 'ln1_g', 'ln1_b', 'w_up', 'w_down', 'ln2_g', 'ln2_b']
TWIN_WEIGHTS = ['w_in', 'q_norm_g', 'w_uq', 'kv_norm_g', 'w_uk', 'w_uv', 'ret_gn_g', 'w_out', 'ln1_g', 'ln1_b', 'w_up', 'w_down', 'ln2_g', 'ln2_b']
TWIN_DIFF_INPUT = 'x'
TWIN_INPUTS = ['x', 'positions', 'w_in', 'q_norm_g', 'w_uq', 'kv_norm_g', 'w_uk', 'w_uv', 'ret_gn_g', 'w_out', 'ln1_g', 'ln1_b', 'w_up', 'w_down', 'ln2_g', 'ln2_b', 'loss_target', 'm_w_in', 'm_q_norm_g', 'm_w_uq', 'm_kv_norm_g', 'm_w_uk', 'm_w_uv', 'm_ret_gn_g', 'm_w_out', 'm_ln1_g', 'm_ln1_b', 'm_w_up', 'm_w_down', 'm_ln2_g', 'm_ln2_b', 'v_w_in', 'v_q_norm_g', 'v_w_uq', 'v_kv_norm_g', 'v_w_uk', 'v_w_uv', 'v_ret_gn_g', 'v_w_out', 'v_ln1_g', 'v_ln1_b', 'v_w_up', 'v_w_down', 'v_ln2_g', 'v_ln2_b']
TWIN_OUTPUTS = ['loss', 'grad_x', 'grad_w_in', 'grad_q_norm_g', 'grad_w_uq', 'grad_kv_norm_g', 'grad_w_uk', 'grad_w_uv', 'grad_ret_gn_g', 'grad_w_out', 'grad_ln1_g', 'grad_ln1_b', 'grad_w_up', 'grad_w_down', 'grad_ln2_g', 'grad_ln2_b', 'delta_w_in', 'delta_q_norm_g', 'delta_w_uq', 'delta_kv_norm_g', 'delta_w_uk', 'delta_w_uv', 'delta_ret_gn_g', 'delta_w_out', 'delta_ln1_g', 'delta_ln1_b', 'delta_w_up', 'delta_w_down', 'delta_ln2_g', 'delta_ln2_b', 'new_m_w_in', 'new_m_q_norm_g', 'new_m_w_uq', 'new_m_kv_norm_g', 'new_m_w_uk', 'new_m_w_uv', 'new_m_ret_gn_g', 'new_m_w_out', 'new_m_ln1_g', 'new_m_ln1_b', 'new_m_w_up', 'new_m_w_down', 'new_m_ln2_g', 'new_m_ln2_b', 'new_v_w_in', 'new_v_q_norm_g', 'new_v_w_uq', 'new_v_kv_norm_g', 'new_v_w_uk', 'new_v_w_uv', 'new_v_ret_gn_g', 'new_v_w_out', 'new_v_ln1_g', 'new_v_ln1_b', 'new_v_w_up', 'new_v_w_down', 'new_v_ln2_g', 'new_v_ln2_b']
TWIN_LEAF_KINDS = {'loss': 'loss', 'grad_x': 'grad_x', 'grad_w_in': 'grad_w', 'grad_q_norm_g': 'grad_w', 'grad_w_uq': 'grad_w', 'grad_kv_norm_g': 'grad_w', 'grad_w_uk': 'grad_w', 'grad_w_uv': 'grad_w', 'grad_ret_gn_g': 'grad_w', 'grad_w_out': 'grad_w', 'grad_ln1_g': 'grad_w', 'grad_ln1_b': 'grad_w', 'grad_w_up': 'grad_w', 'grad_w_down': 'grad_w', 'grad_ln2_g': 'grad_w', 'grad_ln2_b': 'grad_w', 'delta_w_in': 'delta_w', 'delta_q_norm_g': 'delta_w', 'delta_w_uq': 'delta_w', 'delta_kv_norm_g': 'delta_w', 'delta_w_uk': 'delta_w', 'delta_w_uv': 'delta_w', 'delta_ret_gn_g': 'delta_w', 'delta_w_out': 'delta_w', 'delta_ln1_g': 'delta_w', 'delta_ln1_b': 'delta_w', 'delta_w_up': 'delta_w', 'delta_w_down': 'delta_w', 'delta_ln2_g': 'delta_w', 'delta_ln2_b': 'delta_w', 'new_m_w_in': 'new_m', 'new_m_q_norm_g': 'new_m', 'new_m_w_uq': 'new_m', 'new_m_kv_norm_g': 'new_m', 'new_m_w_uk': 'new_m', 'new_m_w_uv': 'new_m', 'new_m_ret_gn_g': 'new_m', 'new_m_w_out': 'new_m', 'new_m_ln1_g': 'new_m', 'new_m_ln1_b': 'new_m', 'new_m_w_up': 'new_m', 'new_m_w_down': 'new_m', 'new_m_ln2_g': 'new_m', 'new_m_ln2_b': 'new_m', 'new_v_w_in': 'new_v', 'new_v_q_norm_g': 'new_v', 'new_v_w_uq': 'new_v', 'new_v_kv_norm_g': 'new_v', 'new_v_w_uk': 'new_v', 'new_v_w_uv': 'new_v', 'new_v_ret_gn_g': 'new_v', 'new_v_w_out': 'new_v', 'new_v_ln1_g': 'new_v', 'new_v_ln1_b': 'new_v', 'new_v_w_up': 'new_v', 'new_v_w_down': 'new_v', 'new_v_ln2_g': 'new_v', 'new_v_ln2_b': 'new_v'}


def _forward(args):
    return _fwd_reference(*[args[k] for k in FWD_PARAMS])


def _output_shape():
    def fwd():
        inp = _fwd_setup_inputs(0)
        return _fwd_reference(*[inp[k] for k in FWD_PARAMS])
    out = _jax.eval_shape(fwd)
    return out.shape, out.dtype

N_MICROBATCH = 1
ADAM_LR = 0.001
ADAM_B1 = 0.9
ADAM_B2 = 0.999
ADAM_EPS = 1e-08
ADAM_WD = 0.01
ADAM_STEP = 10
PER_EXAMPLE_BATCH_AXIS = {'x': 0, 'positions': 0, 'loss_target': 0}
SHARED_INPUTS = []
_WEIGHT_DTYPES = {'w_in': _jnp.float32, 'q_norm_g': _jnp.float32, 'w_uq': _jnp.float32, 'kv_norm_g': _jnp.float32, 'w_uk': _jnp.float32, 'w_uv': _jnp.float32, 'ret_gn_g': _jnp.float32, 'w_out': _jnp.float32, 'ln1_g': _jnp.float32, 'ln1_b': _jnp.float32, 'w_up': _jnp.float32, 'w_down': _jnp.float32, 'ln2_g': _jnp.float32, 'ln2_b': _jnp.float32}
MOMENT_SCALE = {'w_in': 2.767683e-02, 'q_norm_g': 4.375125e-03, 'w_uq': 3.019518e-03, 'kv_norm_g': 7.049435e-03, 'w_uk': 3.070205e-03, 'w_uv': 6.543937e-03, 'ret_gn_g': 2.573656e-02, 'w_out': 3.129436e-02, 'ln1_g': 5.710901e-01, 'ln1_b': 2.862351e-01, 'w_up': 1.823433e-02, 'w_down': 4.618256e-02, 'ln2_g': 1.599082e+01, 'ln2_b': 1.415933e+00}


def _to_microbatches(a, axis):
    t = _jnp.moveaxis(a, axis, 0)
    t = t.reshape((N_MICROBATCH, t.shape[0] // N_MICROBATCH) + t.shape[1:])
    return _jnp.moveaxis(t, 1, axis + 1)


def setup_inputs(seed: int = 0) -> dict:
    inp = _fwd_setup_inputs(seed)
    key = _jax.random.fold_in(_jax.random.key(seed), 7919)
    shape, _ = _output_shape()
    out = dict(inp)
    out["loss_target"] = _jax.random.normal(_jax.random.fold_in(key, 0), shape, _jnp.float32)
    for i, name in enumerate(TWIN_WEIGHTS):
        w = inp[name].astype(_jnp.float32)
        if MOMENT_SCALE is None:
            s = _jnp.sqrt(_jnp.mean(_jnp.square(w)) + 1e-30)
        else:
            s = MOMENT_SCALE[name]
        km, kv = _jax.random.split(_jax.random.fold_in(key, i + 1))
        out[name] = w
        out["m_" + name] = s * _jax.random.normal(km, w.shape, _jnp.float32)
        out["v_" + name] = (s * s) * _jax.random.uniform(kv, w.shape, _jnp.float32, 0.5, 1.5)
    if N_MICROBATCH > 1:
        for name, axis in PER_EXAMPLE_BATCH_AXIS.items():
            out[name] = _to_microbatches(out[name], axis)
    return {'x': out['x'], 'positions': out['positions'], 'w_in': out['w_in'], 'q_norm_g': out['q_norm_g'], 'w_uq': out['w_uq'], 'kv_norm_g': out['kv_norm_g'], 'w_uk': out['w_uk'], 'w_uv': out['w_uv'], 'ret_gn_g': out['ret_gn_g'], 'w_out': out['w_out'], 'ln1_g': out['ln1_g'], 'ln1_b': out['ln1_b'], 'w_up': out['w_up'], 'w_down': out['w_down'], 'ln2_g': out['ln2_g'], 'ln2_b': out['ln2_b'], 'loss_target': out['loss_target'], 'm_w_in': out['m_w_in'], 'm_q_norm_g': out['m_q_norm_g'], 'm_w_uq': out['m_w_uq'], 'm_kv_norm_g': out['m_kv_norm_g'], 'm_w_uk': out['m_w_uk'], 'm_w_uv': out['m_w_uv'], 'm_ret_gn_g': out['m_ret_gn_g'], 'm_w_out': out['m_w_out'], 'm_ln1_g': out['m_ln1_g'], 'm_ln1_b': out['m_ln1_b'], 'm_w_up': out['m_w_up'], 'm_w_down': out['m_w_down'], 'm_ln2_g': out['m_ln2_g'], 'm_ln2_b': out['m_ln2_b'], 'v_w_in': out['v_w_in'], 'v_q_norm_g': out['v_q_norm_g'], 'v_w_uq': out['v_w_uq'], 'v_kv_norm_g': out['v_kv_norm_g'], 'v_w_uk': out['v_w_uk'], 'v_w_uv': out['v_w_uv'], 'v_ret_gn_g': out['v_ret_gn_g'], 'v_w_out': out['v_w_out'], 'v_ln1_g': out['v_ln1_g'], 'v_ln1_b': out['v_ln1_b'], 'v_w_up': out['v_w_up'], 'v_w_down': out['v_w_down'], 'v_ln2_g': out['v_ln2_g'], 'v_ln2_b': out['v_ln2_b']}


def _loss(weights, diff, rest, loss_target):
    with _jax.named_scope("forward"):
        args = {**rest, TWIN_DIFF_INPUT: diff, **{k: w.astype(_WEIGHT_DTYPES[k]) for k, w in weights.items()}}
        y = _forward(args)
    with _jax.named_scope("loss_head"):
        err = _jnp.square(y.astype(_jnp.float32) - loss_target)
        return 0.5 * _jnp.sum(_jnp.mean(err, axis=-1)) if err.ndim else 0.5 * err


def _adamw(w, g, m, v):
    m = ADAM_B1 * m + (1.0 - ADAM_B1) * g
    v = ADAM_B2 * v + (1.0 - ADAM_B2) * _jnp.square(g)
    m_hat = m / (1.0 - ADAM_B1 ** ADAM_STEP)
    v_hat = v / (1.0 - ADAM_B2 ** ADAM_STEP)
    delta = -ADAM_LR * (m_hat / (_jnp.sqrt(v_hat) + ADAM_EPS) + ADAM_WD * w)
    return delta, m, v


def reference(x, positions, w_in, q_norm_g, w_uq, kv_norm_g, w_uk, w_uv, ret_gn_g, w_out, ln1_g, ln1_b, w_up, w_down, ln2_g, ln2_b, loss_target, m_w_in, m_q_norm_g, m_w_uq, m_kv_norm_g, m_w_uk, m_w_uv, m_ret_gn_g, m_w_out, m_ln1_g, m_ln1_b, m_w_up, m_w_down, m_ln2_g, m_ln2_b, v_w_in, v_q_norm_g, v_w_uq, v_kv_norm_g, v_w_uk, v_w_uv, v_ret_gn_g, v_w_out, v_ln1_g, v_ln1_b, v_w_up, v_w_down, v_ln2_g, v_ln2_b):
    given = dict(x=x, positions=positions, w_in=w_in, q_norm_g=q_norm_g, w_uq=w_uq, kv_norm_g=kv_norm_g, w_uk=w_uk, w_uv=w_uv, ret_gn_g=ret_gn_g, w_out=w_out, ln1_g=ln1_g, ln1_b=ln1_b, w_up=w_up, w_down=w_down, ln2_g=ln2_g, ln2_b=ln2_b, loss_target=loss_target, m_w_in=m_w_in, m_q_norm_g=m_q_norm_g, m_w_uq=m_w_uq, m_kv_norm_g=m_kv_norm_g, m_w_uk=m_w_uk, m_w_uv=m_w_uv, m_ret_gn_g=m_ret_gn_g, m_w_out=m_w_out, m_ln1_g=m_ln1_g, m_ln1_b=m_ln1_b, m_w_up=m_w_up, m_w_down=m_w_down, m_ln2_g=m_ln2_g, m_ln2_b=m_ln2_b, v_w_in=v_w_in, v_q_norm_g=v_q_norm_g, v_w_uq=v_w_uq, v_kv_norm_g=v_kv_norm_g, v_w_uk=v_w_uk, v_w_uv=v_w_uv, v_ret_gn_g=v_ret_gn_g, v_w_out=v_w_out, v_ln1_g=v_ln1_g, v_ln1_b=v_ln1_b, v_w_up=v_w_up, v_w_down=v_w_down, v_ln2_g=v_ln2_g, v_ln2_b=v_ln2_b)
    weights = {n: given[n] for n in TWIN_WEIGHTS}
    shared = {n: given[n] for n in SHARED_INPUTS}
    per_example = {n: given[n] for n in ['x', 'positions']}
    grad_fn = _jax.value_and_grad(_loss, argnums=(0, 1))

    def one_microbatch(ex, loss_target):
        ex = dict(ex)
        diff = ex.pop(TWIN_DIFF_INPUT)
        return grad_fn(weights, diff, {**shared, **ex}, loss_target)

    if N_MICROBATCH == 1:
        loss, (grad_w, grad_x) = one_microbatch(per_example, given["loss_target"])
    else:
        def body(carry, xs):
            loss_sum, grad_sum = carry
            l_k, (gw_k, gx_k) = one_microbatch(xs[0], xs[1])
            with _jax.named_scope("update"):
                return (loss_sum + l_k, _jax.tree.map(_jnp.add, grad_sum, gw_k)), gx_k

        init = (_jnp.zeros((), _jnp.float32), _jax.tree.map(_jnp.zeros_like, weights))
        (loss, grad_w), grad_x = _jax.lax.scan(body, init, (per_example, given["loss_target"]))
    with _jax.named_scope("update"):
        delta_w, new_m, new_v = {}, {}, {}
        for n in TWIN_WEIGHTS:
            delta_w[n], new_m[n], new_v[n] = _adamw(weights[n], grad_w[n], given["m_" + n], given["v_" + n])
    return (loss, grad_x, *[grad_w[n] for n in TWIN_WEIGHTS], *[delta_w[n] for n in TWIN_WEIGHTS],
            *[new_m[n] for n in TWIN_WEIGHTS], *[new_v[n] for n in TWIN_WEIGHTS])
```

```python
import functools

import numpy as np

import jax
import jax.numpy as jnp
from jax import lax
from jax.experimental import pallas as pl
from jax.experimental.pallas import tpu as pltpu

F32 = jnp.float32
BF16 = jnp.bfloat16

CHUNK = 64
RET_DK = 128
RET_DV = 128
MLA_NOPE = 128
MLA_ROPE = 64
MLA_DV = 128
MLA_Q_RANK = 768
MLA_KV_RANK = 512
MLA_HEAD_PAD = 256
ROPE_BASE = 10000.0
EPS = 1e-5
DEPTH = 1
DEEPNORM_ALPHA = (2.0 * DEPTH) ** 0.25
ADAM_LR = 0.001
ADAM_B1 = 0.9
ADAM_B2 = 0.999
ADAM_EPS = 1e-08
ADAM_WD = 0.01
ADAM_STEP = 10

N_DEV = 8
LANE = 128
SUBLANE = 8
VMEM_LIMIT = 48 * 1024 * 1024
ADAMW_TILE_ELEMS = 128 * 1024

MESH = pl.DeviceIdType.MESH
ANY = pl.BlockSpec(memory_space=pl.ANY)

NN = (((1,), (0,)), ((), ()))
NT = (((1,), (1,)), ((), ()))
TN = (((0,), (0,)), ((), ()))


def _dot(a, b, dims=NN):
    return lax.dot_general(a.astype(BF16), b.astype(BF16), dims, preferred_element_type=F32)


def _params(n_axes):
    return pltpu.CompilerParams(dimension_semantics=("arbitrary",) * n_axes, vmem_limit_bytes=VMEM_LIMIT)


def _tile(n, want):
    if n <= want:
        return n
    t = want
    while t >= LANE:
        if n % t == 0:
            return t
        t -= LANE
    return n


def _matmul(name, a, b, *, ta=False, tb=False, out_dtypes=(F32,), epilogue=None, extras=(), tm=1024, tn=1024, tk=512):
    if ta:
        K, M = a.shape
    else:
        M, K = a.shape
    N = b.shape[0] if tb else b.shape[1]
    tm, tn, tk = _tile(M, tm), _tile(N, tn), _tile(K, tk)
    nk = K // tk
    dims = TN if ta else (NT if tb else NN)
    n_ex, n_out = len(extras), len(out_dtypes)

    def body(*refs):
        a_ref, b_ref = refs[0], refs[1]
        ex_refs = refs[2:2 + n_ex]
        out_refs = refs[2 + n_ex:2 + n_ex + n_out]
        acc_ref = refs[-1]
        k = pl.program_id(2)

        @pl.when(k == 0)
        def _():
            acc_ref[...] = jnp.zeros_like(acc_ref)

        acc_ref[...] += _dot(a_ref[...], b_ref[...], dims)

        @pl.when(k == nk - 1)
        def _():
            acc = acc_ref[...]
            res = (acc,) if epilogue is None else epilogue(acc, *[e[...] for e in ex_refs])
            for o_ref, r in zip(out_refs, res):
                o_ref[...] = r.astype(o_ref.dtype)

    a_spec = pl.BlockSpec((tk, tm), lambda i, j, k: (k, i)) if ta else pl.BlockSpec((tm, tk), lambda i, j, k: (i, k))
    b_spec = pl.BlockSpec((tn, tk), lambda i, j, k: (j, k)) if tb else pl.BlockSpec((tk, tn), lambda i, j, k: (k, j))
    tile_spec = pl.BlockSpec((tm, tn), lambda i, j, k: (i, j))
    outs = pl.pallas_call(
        body,
        name=name,
        grid=(M // tm, N // tn, nk),
        in_specs=[a_spec, b_spec] + [tile_spec] * n_ex,
        out_specs=[tile_spec] * n_out,
        out_shape=[jax.ShapeDtypeStruct((M, N), d) for d in out_dtypes],
        scratch_shapes=[pltpu.VMEM((tm, tn), F32)],
        compiler_params=_params(3),
    )(a, b, *extras)
    return outs[0] if n_out == 1 else outs


def _rowwise(name, fn, row_in, const_in, row_out, acc_out, tm):
    T = row_in[0][0].shape[0]
    n_r, n_c, n_o, n_a = len(row_in), len(const_in), len(row_out), len(acc_out)

    def body(*refs):
        ins = [r[...] for r in refs[:n_r + n_c]]
        out_refs = refs[n_r + n_c:n_r + n_c + n_o]
        acc_refs = refs[n_r + n_c + n_o:]
        res = fn(*ins)
        for o_ref, r in zip(out_refs, res[:n_o]):
            o_ref[...] = r.astype(o_ref.dtype)

        @pl.when(pl.program_id(0) == 0)
        def _():
            for a_ref in acc_refs:
                a_ref[...] = jnp.zeros_like(a_ref)

        for a_ref, r in zip(acc_refs, res[n_o:]):
            a_ref[...] += jnp.sum(r.reshape(tm // SUBLANE, SUBLANE, r.shape[-1]), axis=0)

    in_specs = [pl.BlockSpec((tm, w), functools.partial(lambda i, c: (i, c), c=cb)) for (_, w, cb) in row_in]
    in_specs += [pl.BlockSpec(c.shape, functools.partial(lambda i, nd: (0,) * nd, nd=c.ndim)) for c in const_in]
    out_specs = [pl.BlockSpec((tm, c), lambda i: (i, 0)) for (c, _) in row_out]
    out_specs += [pl.BlockSpec((SUBLANE, c), lambda i: (0, 0)) for c in acc_out]
    out_shape = [jax.ShapeDtypeStruct((T, c), d) for (c, d) in row_out]
    out_shape += [jax.ShapeDtypeStruct((SUBLANE, c), F32) for c in acc_out]
    return pl.pallas_call(
        body, name=name, grid=(T // tm,), in_specs=in_specs, out_specs=out_specs, out_shape=out_shape,
        compiler_params=_params(1),
    )(*[r[0] for r in row_in], *const_in)


def _ln_stats(z):
    mu = jnp.mean(z, axis=-1, keepdims=True)
    var = jnp.mean(jnp.square(z - mu), axis=-1, keepdims=True)
    rstd = lax.rsqrt(var + EPS)
    return (z - mu) * rstd, rstd


def _ln_bwd(dy, xhat, rstd, g):
    dxh = dy * g
    return rstd * (dxh - jnp.mean(dxh, axis=-1, keepdims=True) - xhat * jnp.mean(dxh * xhat, axis=-1, keepdims=True))


def _rms(x, g):
    return x * lax.rsqrt(jnp.mean(jnp.square(x), axis=-1, keepdims=True) + EPS) * g


def _rope128(x, cos2, sin2):
    return x * cos2 + pltpu.roll(x, RET_DK // 2, 1) * sin2


def _unrope128(dy, cos2, sin2):
    return dy * cos2 + pltpu.roll(dy * sin2, RET_DK // 2, 1)


def _rope64(x, cosp, sina, sinb):
    h = MLA_ROPE // 2
    return x * cosp + pltpu.roll(x, h, 1) * sina + pltpu.roll(x, LANE - h, 1) * sinb


def _unrope64(dy, cosp, sina, sinb):
    h = MLA_ROPE // 2
    return dy * cosp + pltpu.roll(dy * sina, LANE - h, 1) + pltpu.roll(dy * sinb, h, 1)


RET_BLOCK = 256


def _ret_tables(H, B):
    log_g = np.log1p(-np.exp2(-5.0 - np.arange(H, dtype=np.float32))).astype(np.float32)
    idx = np.arange(B, dtype=np.float32)
    dist = np.abs(idx[:, None] - idx[None, :])
    visible = (np.arange(B)[None, :] // CHUNK) <= (np.arange(B)[:, None] // CHUNK)
    dmat = np.exp(log_g[:, None, None] * dist).astype(np.float32) * visible[None].astype(np.float32)
    qd = np.exp(log_g[:, None] * (idx + 1.0)).astype(np.float32)
    kd = np.exp(log_g[:, None] * (B - 1 - idx)).astype(np.float32)
    sd = np.exp(log_g * B).astype(np.float32)
    bc = lambda v: np.ascontiguousarray(np.broadcast_to(v[:, :, None], (H, v.shape[1], LANE)))
    sdb = np.ascontiguousarray(np.broadcast_to(sd[:, None, None], (H, SUBLANE, LANE)))
    return jnp.asarray(dmat), jnp.asarray(bc(qd)), jnp.asarray(bc(kd)), jnp.asarray(sdb)


def _ret_specs(H, B, reverse, nb):
    blk = (lambda b: nb - 1 - b) if reverse else (lambda b: b)
    col = lambda off: pl.BlockSpec((B, LANE), functools.partial(lambda h, b, off: (blk(b), off + h), off=off))
    row = pl.BlockSpec((B, LANE), lambda h, b: (blk(b), 0))
    per_head = lambda r: pl.BlockSpec((1, r, LANE), lambda h, b: (h, 0, 0))
    dmat = pl.BlockSpec((1, B, B), lambda h, b: (h, 0, 0))
    return col, row, per_head, dmat


def _retention_fwd(p1, cos2, sin2, gn_g, H):
    T = p1.shape[0]
    B = min(RET_BLOCK, T)
    nb = T // B
    scale = RET_DK ** -0.5
    dmat, qd, kd, sd = _ret_tables(H, B)
    col, row, per_head, dspec = _ret_specs(H, B, False, nb)

    def body(q_ref, k_ref, v_ref, g_ref, cos_ref, sin_ref, d_ref, qd_ref, kd_ref, sd_ref, gn_ref, pre_ref, out_ref, s_ref):
        @pl.when(pl.program_id(1) == 0)
        def _():
            s_ref[...] = jnp.zeros_like(s_ref)

        cos, sin = cos_ref[...], sin_ref[...]
        q = _rope128(q_ref[...], cos, sin)
        k = _rope128(k_ref[...], cos, sin) * scale
        v = v_ref[...]
        s = s_ref[...]
        a = _dot(q, k, NT) * d_ref[0]
        ret = _dot(a, v) + _dot(q * qd_ref[0], s)
        s_ref[...] = sd_ref[0, 0:1, :] * s + _dot(k * kd_ref[0], v, TN)
        pre_ref[...] = ret
        nrm, _ = _ln_stats(ret)
        out_ref[...] = (jax.nn.silu(g_ref[...]) * (nrm * gn_ref[...])).astype(out_ref.dtype)

    out_spec = pl.BlockSpec((B, LANE), lambda h, b: (b, h))
    return pl.pallas_call(
        body, name="retention_fwd", grid=(H, nb),
        in_specs=[col(0), col(H), col(2 * H), col(3 * H), row, row, dspec, per_head(B), per_head(B), per_head(SUBLANE),
                  pl.BlockSpec((1, LANE), lambda h, b: (0, h))],
        out_specs=[out_spec, out_spec],
        out_shape=[jax.ShapeDtypeStruct((T, H * RET_DV), F32), jax.ShapeDtypeStruct((T, H * RET_DV), BF16)],
        scratch_shapes=[pltpu.VMEM((RET_DK, RET_DV), F32)],
        compiler_params=_params(2),
    )(p1, p1, p1, p1, cos2, sin2, dmat, qd, kd, sd, gn_g)


def _retention_bwd(p1, dpre, cos2, sin2, H):
    T = p1.shape[0]
    B = min(RET_BLOCK, T)
    nb = T // B
    scale = RET_DK ** -0.5
    dmat, qd, kd, sd = _ret_tables(H, B)

    def dq_body(q_ref, k_ref, v_ref, do_ref, cos_ref, sin_ref, d_ref, qd_ref, kd_ref, sd_ref, dq_ref, s_ref):
        @pl.when(pl.program_id(1) == 0)
        def _():
            s_ref[...] = jnp.zeros_like(s_ref)

        cos, sin = cos_ref[...], sin_ref[...]
        k = _rope128(k_ref[...], cos, sin) * scale
        v, do, s = v_ref[...], do_ref[...], s_ref[...]
        da = _dot(do, v, NT) * d_ref[0]
        dq = _dot(da, k) + _dot(do, s, NT) * qd_ref[0]
        dq_ref[...] = _unrope128(dq, cos, sin).astype(dq_ref.dtype)
        s_ref[...] = sd_ref[0, 0:1, :] * s + _dot(k * kd_ref[0], v, TN)

    col, row, per_head, dspec = _ret_specs(H, B, False, nb)
    dq = pl.pallas_call(
        dq_body, name="retention_bwd_dq", grid=(H, nb),
        in_specs=[col(0), col(H), col(2 * H), col(0), row, row, dspec, per_head(B), per_head(B), per_head(SUBLANE)],
        out_specs=pl.BlockSpec((B, LANE), lambda h, b: (b, h)),
        out_shape=jax.ShapeDtypeStruct((T, H * RET_DK), BF16),
        scratch_shapes=[pltpu.VMEM((RET_DK, RET_DV), F32)],
        compiler_params=_params(2),
    )(p1, p1, p1, dpre, cos2, sin2, dmat, qd, kd, sd)

    def dkv_body(q_ref, k_ref, v_ref, do_ref, cos_ref, sin_ref, d_ref, qd_ref, kd_ref, sd_ref, dk_ref, dv_ref, g_ref):
        @pl.when(pl.program_id(1) == 0)
        def _():
            g_ref[...] = jnp.zeros_like(g_ref)

        cos, sin = cos_ref[...], sin_ref[...]
        q = _rope128(q_ref[...], cos, sin)
        k = _rope128(k_ref[...], cos, sin) * scale
        v, do, g = v_ref[...], do_ref[...], g_ref[...]
        a = _dot(q, k, NT) * d_ref[0]
        da = _dot(do, v, NT) * d_ref[0]
        dv = _dot(a, do, TN) + _dot(k * kd_ref[0], g)
        dk = _dot(da, q, TN) + _dot(v, g, NT) * kd_ref[0]
        dv_ref[...] = dv.astype(dv_ref.dtype)
        dk_ref[...] = _unrope128(dk * scale, cos, sin).astype(dk_ref.dtype)
        g_ref[...] = sd_ref[0, 0:1, :] * g + _dot(q * qd_ref[0], do, TN)

    col, row, per_head, dspec = _ret_specs(H, B, True, nb)
    out_spec = pl.BlockSpec((B, LANE), lambda h, b: (nb - 1 - b, h))
    dk, dv = pl.pallas_call(
        dkv_body, name="retention_bwd_dkv", grid=(H, nb),
        in_specs=[col(0), col(H), col(2 * H), col(0), row, row, dspec, per_head(B), per_head(B), per_head(SUBLANE)],
        out_specs=[out_spec, out_spec],
        out_shape=[jax.ShapeDtypeStruct((T, H * RET_DK), BF16), jax.ShapeDtypeStruct((T, H * RET_DV), BF16)],
        scratch_shapes=[pltpu.VMEM((RET_DK, RET_DV), F32)],
        compiler_params=_params(2),
    )(p1, p1, p1, dpre, cos2, sin2, dmat, qd, kd, sd)
    return dq, dk, dv


ATT_BLOCK = 512


def _chunk_mask(t):
    qi = lax.broadcasted_iota(jnp.int32, (t, t), 0) // CHUNK
    kj = lax.broadcasted_iota(jnp.int32, (t, t), 1) // CHUNK
    return kj <= qi


def _attention_fwd(qh, kh, kv, H):
    T = qh.shape[0]
    t = min(ATT_BLOCK, T)
    n = T // t
    scale = (MLA_NOPE + MLA_ROPE) ** -0.5

    def body(q_ref, k_ref, v_ref, o_ref, ob_ref, lse_ref, m_ref, l_ref, acc_ref):
        i, j = pl.program_id(1), pl.program_id(2)

        @pl.when(j == 0)
        def _():
            m_ref[...] = jnp.full_like(m_ref, -jnp.inf)
            l_ref[...] = jnp.zeros_like(l_ref)
            acc_ref[...] = jnp.zeros_like(acc_ref)

        @pl.when(j <= i)
        def _():
            s = _dot(q_ref[...], k_ref[...], NT) * scale
            s = jnp.where(jnp.logical_or(j < i, _chunk_mask(t)), s, -jnp.inf)
            m_prev = m_ref[...]
            m_new = jnp.maximum(m_prev, jnp.max(s, axis=-1, keepdims=True))
            alpha = jnp.exp(m_prev - m_new)
            p = jnp.exp(s - m_new)
            l_ref[...] = alpha * l_ref[...] + jnp.sum(p, axis=-1, keepdims=True)
            acc_ref[...] = alpha * acc_ref[...] + _dot(p, v_ref[...])
            m_ref[...] = m_new

        @pl.when(j == n - 1)
        def _():
            o = acc_ref[...] / l_ref[...]
            o_ref[...] = o
            ob_ref[...] = o.astype(ob_ref.dtype)
            lse_ref[0] = m_ref[...] + jnp.log(l_ref[...])

    kclamp = lambda i, j: jnp.minimum(j, i)
    o_spec = pl.BlockSpec((t, MLA_DV), lambda h, i, j: (i, h))
    return pl.pallas_call(
        body, name="attention_fwd", grid=(H, n, n),
        in_specs=[pl.BlockSpec((t, MLA_HEAD_PAD), lambda h, i, j: (i, h)),
                  pl.BlockSpec((t, MLA_HEAD_PAD), lambda h, i, j: (kclamp(i, j), h)),
                  pl.BlockSpec((t, MLA_DV), lambda h, i, j: (kclamp(i, j), H + h))],
        out_specs=[o_spec, o_spec, pl.BlockSpec((1, t, 1), lambda h, i, j: (h, i, 0))],
        out_shape=[jax.ShapeDtypeStruct((T, H * MLA_DV), F32), jax.ShapeDtypeStruct((T, H * MLA_DV), BF16),
                   jax.ShapeDtypeStruct((H, T, 1), F32)],
        scratch_shapes=[pltpu.VMEM((t, 1), F32), pltpu.VMEM((t, 1), F32), pltpu.VMEM((t, MLA_DV), F32)],
        compiler_params=_params(3),
    )(qh, kh, kv)


def _attention_bwd(qh, kh, kv, o, lse, dmix, H):
    T = qh.shape[0]
    t = min(ATT_BLOCK, T)
    n = T // t
    scale = (MLA_NOPE + MLA_ROPE) ** -0.5

    def body(q_ref, k_ref, v_ref, do_ref, o_ref, lse_ref, dq_ref, dkn_ref, dkr_ref, dv_ref, dk_acc, dv_acc):
        j, i = pl.program_id(1), pl.program_id(2)

        @pl.when(jnp.logical_and(j == 0, i == 0))
        def _():
            dq_ref[...] = jnp.zeros_like(dq_ref)

        @pl.when(i == j)
        def _():
            dk_acc[...] = jnp.zeros_like(dk_acc)
            dv_acc[...] = jnp.zeros_like(dv_acc)

        @pl.when(i >= j)
        def _():
            q, k, v, do = q_ref[...], k_ref[...], v_ref[...], do_ref[...]
            s = _dot(q, k, NT) * scale
            s = jnp.where(jnp.logical_or(j < i, _chunk_mask(t)), s, -jnp.inf)
            p = jnp.exp(s - lse_ref[0])
            dv_acc[...] += _dot(p, do, TN)
            dp = _dot(do, v, NT)
            delta = jnp.sum(do * o_ref[...], axis=-1, keepdims=True)
            ds = p * (dp - delta) * scale
            rows = pl.ds(pl.multiple_of(i * t, t), t)
            dq_ref[rows, :] += _dot(ds, k)
            dk_acc[...] += _dot(ds, q, TN)

        @pl.when(i == n - 1)
        def _():
            dk = dk_acc[...]
            dkn_ref[...] = dk[:, :MLA_NOPE].astype(dkn_ref.dtype)
            dkr_ref[...] = dk[:, MLA_NOPE:]
            dv_ref[...] = dv_acc[...].astype(dv_ref.dtype)

    qclamp = lambda j, i: jnp.maximum(i, j)
    kv_out = pl.BlockSpec((t, LANE), lambda h, j, i: (j, h))
    return pl.pallas_call(
        body, name="attention_bwd", grid=(H, n, n),
        in_specs=[pl.BlockSpec((t, MLA_HEAD_PAD), lambda h, j, i: (qclamp(j, i), h)),
                  pl.BlockSpec((t, MLA_HEAD_PAD), lambda h, j, i: (j, h)),
                  pl.BlockSpec((t, MLA_DV), lambda h, j, i: (j, H + h)),
                  pl.BlockSpec((t, MLA_DV), lambda h, j, i: (qclamp(j, i), H + h)),
                  pl.BlockSpec((t, MLA_DV), lambda h, j, i: (qclamp(j, i), h)),
                  pl.BlockSpec((1, t, 1), lambda h, j, i: (h, qclamp(j, i), 0))],
        out_specs=[pl.BlockSpec((T, MLA_HEAD_PAD), lambda h, j, i: (0, h)), kv_out, kv_out, kv_out],
        out_shape=[jax.ShapeDtypeStruct((T, H * MLA_HEAD_PAD), F32), jax.ShapeDtypeStruct((T, H * LANE), BF16),
                   jax.ShapeDtypeStruct((T, H * LANE), F32), jax.ShapeDtypeStruct((T, H * LANE), BF16)],
        scratch_shapes=[pltpu.VMEM((t, MLA_HEAD_PAD), F32), pltpu.VMEM((t, MLA_DV), F32)],
        compiler_params=_params(3),
    )(qh, kh, kv, dmix, o, lse)


def _my_place():
    return lax.axis_index("x"), lax.axis_index("y"), lax.axis_index("c")


def _flat(px, py, pc):
    return 4 * px + 2 * py + pc


def _block_of(ref, kind, idx, rows, cols):
    if kind == "blk":
        return ref.at[idx]
    if kind == "rows":
        return ref.at[pl.ds(pl.multiple_of(idx * rows, rows), rows), :]
    return ref.at[:, pl.ds(pl.multiple_of(idx * cols, cols), cols)]


def _all_gather(shards, kinds):
    nw = len(shards)
    shapes = []
    for s, kind in zip(shards, kinds):
        r, c = s.shape
        shapes.append({"blk": (N_DEV, r, c), "rows": (N_DEV * r, c), "cols": (r, N_DEV * c)}[kind])

    def body(*refs):
        src, dst = refs[:nw], refs[nw:2 * nw]
        send_sems, recv_sems, local_sems = refs[2 * nw:]
        x, y, c = _my_place()
        me, sibling = (x, y, c), (x, y, 1 - c)
        chips = [(1 - x, y), (x, 1 - y), (1 - x, 1 - y)]

        def place(w, dev):
            r, cc = shards[w].shape
            return _block_of(dst[w], kinds[w], _flat(*dev), r, cc)

        def copy(w, k, block, to, from_src=False):
            return pltpu.make_async_remote_copy(
                src_ref=src[w] if from_src else place(w, block), dst_ref=place(w, block),
                send_sem=send_sems.at[7 * w + k], recv_sem=recv_sems.at[7 * w + k],
                device_id=to, device_id_type=MESH)

        started = []
        for w in range(nw):
            mine = pltpu.make_async_copy(src[w], place(w, me), local_sems.at[w])
            mine.start()
            started.append(mine)
        first = []
        for w in range(nw):
            first.append(copy(w, 0, me, sibling, from_src=True))
            first += [copy(w, 1 + j, me, (*chip, c), from_src=True) for j, chip in enumerate(chips)]
        for cp in first:
            cp.start()
        passed = []
        for w in range(nw):
            for j, chip in enumerate(chips):
                copy(w, 1 + j, (*chip, c), me).wait_recv()
                fwd = copy(w, 4 + j, (*chip, c), sibling)
                fwd.start()
                passed.append(fwd)
        for w in range(nw):
            copy(w, 0, sibling, me).wait_recv()
            for j, chip in enumerate(chips):
                copy(w, 4 + j, (*chip, 1 - c), me).wait_recv()
        for cp in first + passed:
            cp.wait_send()
        for mine in started:
            mine.wait()

    return pl.pallas_call(
        body, name="weights_all_gather",
        in_specs=[ANY] * nw, out_specs=[ANY] * nw,
        out_shape=[jax.ShapeDtypeStruct(s, BF16) for s in shapes],
        scratch_shapes=[pltpu.SemaphoreType.DMA((7 * nw,)), pltpu.SemaphoreType.DMA((7 * nw,)),
                        pltpu.SemaphoreType.DMA((nw,))],
    )(*shards)


def _exchange_partials(grads, kinds, shard_shapes, small):
    nw = len(grads)

    def body(*refs):
        src, small_ref = refs[:nw], refs[nw]
        land, small_land = refs[nw + 1:2 * nw + 1], refs[2 * nw + 1]
        send_sems, recv_sems, local_sems = refs[2 * nw + 2:]
        x, y, c = _my_place()
        my_idx = _flat(x, y, c)

        def part(w, dev_idx):
            r, cc = shard_shapes[w]
            return _block_of(src[w], kinds[w], dev_idx, r, cc)

        local = []
        for w in range(nw):
            cp = pltpu.make_async_copy(part(w, my_idx), land[w].at[my_idx], local_sems.at[w])
            cp.start()
            local.append(cp)
        cp = pltpu.make_async_copy(small_ref, small_land.at[my_idx], local_sems.at[nw])
        cp.start()
        local.append(cp)

        remote = []
        for k in range(1, N_DEV):
            fx, fy, fc = (k >> 2) & 1, (k >> 1) & 1, k & 1
            peer = (x ^ fx, y ^ fy, c ^ fc)
            for w in range(nw + 1):
                s = (nw + 1) * (k - 1) + w
                remote.append(pltpu.make_async_remote_copy(
                    src_ref=part(w, _flat(*peer)) if w < nw else small_ref,
                    dst_ref=land[w].at[my_idx] if w < nw else small_land.at[my_idx],
                    send_sem=send_sems.at[s], recv_sem=recv_sems.at[s], device_id=peer, device_id_type=MESH))
        for cp in remote:
            cp.start()
        for cp in remote:
            cp.wait()
        for cp in local:
            cp.wait()

    ns = (N_DEV - 1) * (nw + 1)
    out_shape = [jax.ShapeDtypeStruct((N_DEV,) + tuple(s), BF16) for s in shard_shapes]
    out_shape.append(jax.ShapeDtypeStruct((N_DEV,) + small.shape, F32))
    return pl.pallas_call(
        body, name="gradient_exchange",
        in_specs=[ANY] * (nw + 1), out_specs=[ANY] * (nw + 1), out_shape=out_shape,
        scratch_shapes=[pltpu.SemaphoreType.DMA((ns,)), pltpu.SemaphoreType.DMA((ns,)),
                        pltpu.SemaphoreType.DMA((nw + 1,))],
    )(*grads, small)


def _adamw_math(w, g, m, v):
    m = ADAM_B1 * m + (1.0 - ADAM_B1) * g
    v = ADAM_B2 * v + (1.0 - ADAM_B2) * jnp.square(g)
    m_hat = m / (1.0 - ADAM_B1 ** ADAM_STEP)
    v_hat = v / (1.0 - ADAM_B2 ** ADAM_STEP)
    delta = -ADAM_LR * (m_hat / (jnp.sqrt(v_hat) + ADAM_EPS) + ADAM_WD * w)
    return delta, m, v


def _adamw(name, landed, w, m, v):
    R, C = w.shape
    tr = 16
    while tr * 2 * C <= ADAMW_TILE_ELEMS and R % (tr * 2) == 0:
        tr *= 2
    tr = tr if R % tr == 0 else R

    def body(l_ref, w_ref, m_ref, v_ref, g_out, d_out, m_out, v_out):
        g = l_ref[0].astype(F32)
        for s in range(1, N_DEV):
            g = g + l_ref[s].astype(F32)
        delta, mn, vn = _adamw_math(w_ref[...], g, m_ref[...], v_ref[...])
        g_out[...] = g
        d_out[...] = delta
        m_out[...] = mn
        v_out[...] = vn

    spec = pl.BlockSpec((tr, C), lambda i: (i, 0))
    l_spec = pl.BlockSpec((N_DEV, tr, C), lambda i: (0, i, 0))
    return pl.pallas_call(
        body, name=name, grid=(R // tr,), in_specs=[l_spec, spec, spec, spec], out_specs=[spec] * 4,
        out_shape=[jax.ShapeDtypeStruct((R, C), F32)] * 4, compiler_params=_params(1),
    )(landed, w, m, v)


def _adamw_small(landed, w, m, v):
    P = w.shape[1]

    def body(l_ref, w_ref, m_ref, v_ref, g_out, d_out, m_out, v_out):
        acc = l_ref[0]
        for s in range(1, N_DEV):
            acc = acc + l_ref[s]
        g = jnp.sum(acc, axis=0, keepdims=True)
        delta, mn, vn = _adamw_math(w_ref[...], g, m_ref[...], v_ref[...])
        g_out[...] = g
        d_out[...] = delta
        m_out[...] = mn
        v_out[...] = vn

    return pl.pallas_call(
        body, name="adamw_replicated", out_shape=[jax.ShapeDtypeStruct((1, P), F32)] * 4,
    )(landed, w, m, v)


def _rope_tables(pos, T):
    p = pos.astype(F32)[:, None]
    inv_r = ROPE_BASE ** (-jnp.arange(0, RET_DK, 2, dtype=F32) / RET_DK)
    ang = p * inv_r
    cos, sin = jnp.cos(ang), jnp.sin(ang)
    cos2 = jnp.concatenate([cos, cos], axis=-1)
    sin2 = jnp.concatenate([-sin, sin], axis=-1)
    inv_m = ROPE_BASE ** (-jnp.arange(0, MLA_ROPE, 2, dtype=F32) / MLA_ROPE)
    ang = p * inv_m
    cos, sin = jnp.cos(ang), jnp.sin(ang)
    zh = jnp.zeros_like(cos)
    z2 = jnp.zeros((T, LANE - MLA_ROPE), F32)
    cosp = jnp.concatenate([cos, cos, z2], axis=-1)
    sina = jnp.concatenate([zh, sin, z2], axis=-1)
    sinb = jnp.concatenate([-sin, zh, z2], axis=-1)
    return cos2, sin2, cosp, sina, sinb


def kernel(x, positions, w_in, q_norm_g, w_uq, kv_norm_g, w_uk, w_uv, ret_gn_g, w_out, ln1_g, ln1_b, w_up, w_down, ln2_g, ln2_b, loss_target, m_w_in, m_q_norm_g, m_w_uq, m_kv_norm_g, m_w_uk, m_w_uv, m_ret_gn_g, m_w_out, m_ln1_g, m_ln1_b, m_w_up, m_w_down, m_ln2_g, m_ln2_b, v_w_in, v_q_norm_g, v_w_uq, v_kv_norm_g, v_w_uk, v_w_uv, v_ret_gn_g, v_w_out, v_ln1_g, v_ln1_b, v_w_up, v_w_down, v_ln2_g, v_ln2_b):
    T, D = x.shape[1], x.shape[2]
    H = D // 256
    RW = H * RET_DV
    x = x[0]
    tgt = loss_target[0]
    alpha = DEEPNORM_ALPHA
    in_width = w_in.shape[2] * N_DEV
    mla_in = in_width - 4 * RW
    mla_in_pad = -(-mla_in // 512) * 512
    f_uq = MLA_NOPE + MLA_ROPE

    big = [w_in[0], w_uq[0], w_uk[0], w_uv[0], w_out[0], w_up[0], w_down[0]]
    kinds = ["blk", "blk", "cols", "cols", "rows", "cols", "rows"]
    g_in, g_uq, wuk, wuv, wout, wup, wdown = _all_gather([w.astype(BF16) for w in big], kinds)
    win = jnp.transpose(g_in, (1, 0, 2)).reshape(D, in_width)
    w1 = win[:, :4 * RW]
    w2 = jnp.pad(win[:, 4 * RW:], ((0, 0), (0, mla_in_pad - mla_in)))
    wuq = jnp.transpose(g_uq, (1, 0, 2)).reshape(MLA_Q_RANK, H, f_uq)
    wuq = jnp.pad(wuq, ((0, 0), (0, 0), (0, MLA_HEAD_PAD - f_uq))).reshape(MLA_Q_RANK, H * MLA_HEAD_PAD)
    wukv = jnp.concatenate([wuk, wuv], axis=1)

    cos2, sin2, cosp, sina, sinb = _rope_tables(positions[0], T)
    xb = x.astype(BF16)
    tr = min(256, T)

    p1 = _matmul("proj_ret", xb, w1)
    p2 = _matmul("proj_mla", xb, w2)
    c_kv, c_kr = MLA_Q_RANK, MLA_Q_RANK + MLA_KV_RANK

    def mla_prep(p, cp, sa, sb, gq, gkv):
        cqn = _rms(p[:, :c_kv], gq)
        ckvn = _rms(p[:, c_kv:c_kr], gkv)
        return cqn, ckvn, _rope64(p[:, c_kr:c_kr + LANE], cp, sa, sb)

    cqn, ckvn, krope = _rowwise(
        "mla_prep", mla_prep, [(p2, mla_in_pad, 0), (cosp, LANE, 0), (sina, LANE, 0), (sinb, LANE, 0)],
        [q_norm_g, kv_norm_g], [(MLA_Q_RANK, BF16), (MLA_KV_RANK, BF16), (LANE, BF16)], [], tr)
    qf = _matmul("mla_q", cqn, wuq)
    kv = _matmul("mla_kv", ckvn, wukv, out_dtypes=(BF16,))

    def mla_heads(q, kn, kr, cp, sa, sb):
        qs, ks = [], []
        for h in range(H):
            o = h * MLA_HEAD_PAD
            qs += [q[:, o:o + MLA_NOPE], _rope64(q[:, o + MLA_NOPE:o + MLA_HEAD_PAD], cp, sa, sb)]
            ks += [kn[:, h * MLA_NOPE:(h + 1) * MLA_NOPE], kr]
        return jnp.concatenate(qs, axis=1), jnp.concatenate(ks, axis=1)

    qh, kh = _rowwise(
        "mla_heads", mla_heads,
        [(qf, H * MLA_HEAD_PAD, 0), (kv, RW, 0), (krope, LANE, 0), (cosp, LANE, 0), (sina, LANE, 0), (sinb, LANE, 0)],
        [], [(H * MLA_HEAD_PAD, BF16), (H * MLA_HEAD_PAD, BF16)], [], tr)
    att_o, att_ob, lse = _attention_fwd(qh, kh, kv, H)
    ret_pre, ret_ob = _retention_fwd(p1, cos2, sin2, ret_gn_g, H)
    mixin = jnp.concatenate([ret_ob, att_ob], axis=1)
    mix = _matmul("mix_out", mixin, wout)

    def ln1_fwd(xt, mt, g, b):
        z = alpha * xt + mt
        xhat, _ = _ln_stats(z)
        y = xhat * g + b
        return z, y, y

    z1, x1, x1b = _rowwise("ln1", ln1_fwd, [(x, D, 0), (mix, D, 0)], [ln1_g, ln1_b],
                           [(D, F32), (D, F32), (D, BF16)], [], tr)

    def relu2(acc):
        r = jnp.maximum(acc, 0.0)
        return r * r, r

    a2b, rb = _matmul("mlp_up", x1b, wup, out_dtypes=(BF16, BF16), epilogue=relu2)
    hmlp = _matmul("mlp_down", a2b, wdown)

    def ln2_loss(x1t, ht, tt, g, b):
        xhat, rstd = _ln_stats(alpha * x1t + ht)
        err = xhat * g + b - tt
        dy = err / D
        dz = _ln_bwd(dy, xhat, rstd, g)
        lrow = 0.5 * jnp.mean(jnp.square(err), axis=-1, keepdims=True)
        return dz, dz, dy * xhat, dy, jnp.broadcast_to(lrow, (lrow.shape[0], LANE))

    dz2, dz2b, dg2, db2, lpart = _rowwise("ln2_loss", ln2_loss, [(x1, D, 0), (hmlp, D, 0), (tgt, D, 0)], [ln2_g, ln2_b],
                                          [(D, F32), (D, BF16)], [D, D, LANE], tr)
    loss = lax.psum(jnp.sum(lpart[:, 0]), ("x", "y", "c"))

    da = _matmul("mlp_down_dx", dz2b, wdown, tb=True, out_dtypes=(BF16,), extras=(rb,),
                 epilogue=lambda acc, r: (acc * (2.0 * r.astype(F32)),))
    g_wdown = _matmul("mlp_down_dw", a2b, dz2b, ta=True, out_dtypes=(BF16,))
    g_wup = _matmul("mlp_up_dw", x1b, da, ta=True, out_dtypes=(BF16,))
    dx1m = _matmul("mlp_up_dx", da, wup, tb=True)

    def ln1_bwd(dm, dzz, z, g):
        xhat, rstd = _ln_stats(z)
        dy = dm + alpha * dzz
        dz = _ln_bwd(dy, xhat, rstd, g)
        return dz, dz, dy * xhat, dy

    dz1, dz1b, dg1, db1 = _rowwise("ln1_bwd", ln1_bwd, [(dx1m, D, 0), (dz2, D, 0), (z1, D, 0)], [ln1_g],
                                   [(D, F32), (D, BF16)], [D, D], tr)
    g_wout = _matmul("mix_out_dw", mixin, dz1b, ta=True, out_dtypes=(BF16,))
    dmix = _matmul("mix_out_dx", dz1b, wout, tb=True)

    dqh, dkn, dkr, dvv = _attention_bwd(qh, kh, kv, att_o, lse, dmix, H)

    def mla_heads_bwd(dq, dkrh, cp, sa, sb):
        parts, acc = [], dkrh[:, :LANE]
        for h in range(H):
            o = h * MLA_HEAD_PAD
            parts += [dq[:, o:o + MLA_NOPE], _unrope64(dq[:, o + MLA_NOPE:o + MLA_HEAD_PAD], cp, sa, sb)]
            if h:
                acc = acc + dkrh[:, h * LANE:(h + 1) * LANE]
        return jnp.concatenate(parts, axis=1), _unrope64(acc, cp, sa, sb)

    dqb, dkr128 = _rowwise(
        "mla_heads_bwd", mla_heads_bwd,
        [(dqh, H * MLA_HEAD_PAD, 0), (dkr, H * LANE, 0), (cosp, LANE, 0), (sina, LANE, 0), (sinb, LANE, 0)],
        [], [(H * MLA_HEAD_PAD, BF16), (LANE, F32)], [], tr)
    dcqn = _matmul("mla_q_dx", dqb, wuq, tb=True)
    g_wuq = _matmul("mla_q_dw", cqn, dqb, ta=True, out_dtypes=(BF16,))
    dkvb = jnp.concatenate([dkn, dvv], axis=1)
    dckvn = _matmul("mla_kv_dx", dkvb, wukv, tb=True)
    g_wukv = _matmul("mla_kv_dw", ckvn, dkvb, ta=True, out_dtypes=(BF16,))

    def rms_bwd(c, dy, g):
        rstd = lax.rsqrt(jnp.mean(jnp.square(c), axis=-1, keepdims=True) + EPS)
        dyg = dy * g
        dc = rstd * (dyg - c * (rstd * rstd) * jnp.mean(dyg * c, axis=-1, keepdims=True))
        return dc, dy * c * rstd

    def mla_prep_bwd(p, dq_, dkv_, dkr_, gq, gkv):
        dcq, dgq_ = rms_bwd(p[:, :c_kv], dq_, gq)
        dckv, dgkv_ = rms_bwd(p[:, c_kv:c_kr], dkv_, gkv)
        pad = jnp.zeros((p.shape[0], mla_in_pad - c_kr - LANE), F32)
        return jnp.concatenate([dcq, dckv, dkr_, pad], axis=1), dgq_, dgkv_

    d_p2, dgq, dgkv = _rowwise(
        "mla_prep_bwd", mla_prep_bwd, [(p2, mla_in_pad, 0), (dcqn, MLA_Q_RANK, 0), (dckvn, MLA_KV_RANK, 0), (dkr128, LANE, 0)],
        [q_norm_g, kv_norm_g], [(mla_in_pad, BF16)], [MLA_Q_RANK, MLA_KV_RANK], tr)

    def gate_bwd(pre, gate, dout, gn):
        dpre, dgate, dgn = [], [], []
        for h in range(H):
            sl = slice(h * RET_DV, (h + 1) * RET_DV)
            nrm, rstd = _ln_stats(pre[:, sl])
            gate_h, do_h, gn_h = gate[:, sl], dout[:, sl], gn[:, sl]
            sg = jax.nn.sigmoid(gate_h)
            silu = gate_h * sg
            dgate.append(do_h * (nrm * gn_h) * (sg * (1.0 + gate_h * (1.0 - sg))))
            dn = do_h * silu
            dgn.append(dn * nrm)
            dpre.append(_ln_bwd(dn, nrm, rstd, gn_h))
        return jnp.concatenate(dpre, axis=1), jnp.concatenate(dgate, axis=1), jnp.concatenate(dgn, axis=1)

    dpre, drg, dgn = _rowwise("ret_gate_bwd", gate_bwd, [(ret_pre, RW, 0), (p1, RW, 3), (dmix, RW, 0)], [ret_gn_g],
                              [(RW, F32), (RW, BF16)], [RW], tr)
    drq, drk, drv = _retention_bwd(p1, dpre, cos2, sin2, H)
    d_p1 = jnp.concatenate([drq, drk, drv, drg], axis=1)

    g_w1 = _matmul("proj_ret_dw", xb, d_p1, ta=True, out_dtypes=(BF16,))
    g_w2 = _matmul("proj_mla_dw", xb, d_p2, ta=True, out_dtypes=(BF16,))
    dx_a = _matmul("proj_mla_dx", d_p2, w2, tb=True, extras=(dz1,), epilogue=lambda acc, d: (acc + alpha * d,))
    grad_x = _matmul("proj_ret_dx", d_p1, w1, tb=True, extras=(dx_a,), epilogue=lambda acc, d: (acc + d,))

    g_win = jnp.concatenate([g_w1, g_w2[:, :mla_in]], axis=1)
    g_win = jnp.transpose(g_win.reshape(D, N_DEV, in_width // N_DEV), (1, 0, 2))
    g_uq = g_wuq.reshape(MLA_Q_RANK, H, MLA_HEAD_PAD)[:, :, :f_uq].reshape(MLA_Q_RANK, N_DEV, H * f_uq // N_DEV)
    g_uq = jnp.transpose(g_uq, (1, 0, 2))
    g_wuk, g_wuv = g_wukv[:, :RW], g_wukv[:, RW:]
    small = jnp.concatenate([dgq, dgkv, dgn, dg1, db1, dg2, db2], axis=1)
    grads = [g_win, g_uq, g_wuk, g_wuv, g_wout, g_wup, g_wdown]
    shard_shapes = [w.shape for w in big]
    landed = _exchange_partials(grads, kinds, shard_shapes, small)

    moments_m = [m_w_in, m_w_uq, m_w_uk, m_w_uv, m_w_out, m_w_up, m_w_down]
    moments_v = [v_w_in, v_w_uq, v_w_uk, v_w_uv, v_w_out, v_w_up, v_w_down]
    names = ["w_in", "w_uq", "w_uk", "w_uv", "w_out", "w_up", "w_down"]
    res = {}
    for n, l, w, m, v in zip(names, landed[:-1], big, moments_m, moments_v):
        res[n] = [r[None] for r in _adamw("adamw_" + n, l, w, m[0], v[0])]

    small_names = ["q_norm_g", "kv_norm_g", "ret_gn_g", "ln1_g", "ln1_b", "ln2_g", "ln2_b"]
    small_w = [q_norm_g, kv_norm_g, ret_gn_g, ln1_g, ln1_b, ln2_g, ln2_b]
    small_m = [m_q_norm_g, m_kv_norm_g, m_ret_gn_g, m_ln1_g, m_ln1_b, m_ln2_g, m_ln2_b]
    small_v = [v_q_norm_g, v_kv_norm_g, v_ret_gn_g, v_ln1_g, v_ln1_b, v_ln2_g, v_ln2_b]
    cat = lambda arrs: jnp.concatenate(arrs, axis=1)
    sres = _adamw_small(landed[-1], cat(small_w), cat(small_m), cat(small_v))
    off = 0
    for n, w in zip(small_names, small_w):
        res[n] = [r[:, off:off + w.shape[1]] for r in sres]
        off += w.shape[1]

    order = ["w_in", "q_norm_g", "w_uq", "kv_norm_g", "w_uk", "w_uv", "ret_gn_g", "w_out", "ln1_g", "ln1_b",
             "w_up", "w_down", "ln2_g", "ln2_b"]
    outs = [loss, grad_x[None]]
    for k in range(4):
        outs += [res[n][k] for n in order]
    return tuple(outs)
```

```python
import functools

import numpy as np

import jax
import jax.numpy as jnp
from jax import lax
from jax.experimental import pallas as pl
from jax.experimental.pallas import tpu as pltpu

F32 = jnp.float32
BF16 = jnp.bfloat16

CHUNK = 64
RET_DK = 128
RET_DV = 128
MLA_NOPE = 128
MLA_ROPE = 64
MLA_DV = 128
MLA_Q_RANK = 768
MLA_KV_RANK = 512
MLA_HEAD_PAD = 256
ROPE_BASE = 10000.0
EPS = 1e-5
DEPTH = 1
DEEPNORM_ALPHA = (2.0 * DEPTH) ** 0.25
ADAM_LR = 0.001
ADAM_B1 = 0.9
ADAM_B2 = 0.999
ADAM_EPS = 1e-08
ADAM_WD = 0.01
ADAM_STEP = 10

N_DEV = 8
LANE = 128
SUBLANE = 8
VMEM_LIMIT = 48 * 1024 * 1024
ADAMW_TILE_ELEMS = 128 * 1024

MESH = pl.DeviceIdType.MESH
ANY = pl.BlockSpec(memory_space=pl.ANY)

NN = (((1,), (0,)), ((), ()))
NT = (((1,), (1,)), ((), ()))
TN = (((0,), (0,)), ((), ()))


def _dot(a, b, dims=NN):
    return lax.dot_general(a.astype(BF16), b.astype(BF16), dims, preferred_element_type=F32)


def _params(n_axes):
    return pltpu.CompilerParams(dimension_semantics=("arbitrary",) * n_axes, vmem_limit_bytes=VMEM_LIMIT)


def _tile(n, want):
    if n <= want:
        return n
    t = want
    while t >= LANE:
        if n % t == 0:
            return t
        t -= LANE
    return n


def _my_place():
    return lax.axis_index("x"), lax.axis_index("y"), lax.axis_index("c")


def _flat(px, py, pc):
    return 4 * px + 2 * py + pc


def _block_of(ref, kind, idx, rows, cols):
    if kind == "blk":
        return ref.at[idx]
    if kind == "rows":
        return ref.at[pl.ds(pl.multiple_of(idx * rows, rows), rows), :]
    return ref.at[:, pl.ds(pl.multiple_of(idx * cols, cols), cols)]


class _GatherJob:
    def __init__(self, shards, kinds, mid):
        self.ins = list(shards)
        self.kinds = list(kinds)
        nw = len(shards)
        self.out_shape = []
        for s, kind in zip(shards, kinds):
            r, c = s.shape
            shape = {"blk": (N_DEV, r, c), "rows": (N_DEV * r, c), "cols": (r, N_DEV * c)}[kind]
            self.out_shape.append(jax.ShapeDtypeStruct(shape, s.dtype))
        self.sems = [pltpu.SemaphoreType.DMA((7 * nw,)), pltpu.SemaphoreType.DMA((7 * nw,)), pltpu.SemaphoreType.DMA((nw,))]
        self.phases = [(0.0, self._start), (mid, self._forward), (1.0, self._finish)]

    def _ctx(self, src, dst, send_sems, recv_sems):
        x, y, c = _my_place()
        chips = [(1 - x, y), (x, 1 - y), (1 - x, 1 - y)]

        def place(w, dev):
            r, cc = self.ins[w].shape
            return _block_of(dst[w], self.kinds[w], _flat(*dev), r, cc)

        def copy(w, k, block, to, from_src=False):
            return pltpu.make_async_remote_copy(
                src_ref=src[w] if from_src else place(w, block), dst_ref=place(w, block),
                send_sem=send_sems.at[7 * w + k], recv_sem=recv_sems.at[7 * w + k],
                device_id=to, device_id_type=MESH)

        return (x, y, c), (x, y, 1 - c), chips, place, copy

    def _start(self, src, dst, send_sems, recv_sems, local_sems):
        me, sibling, chips, place, copy = self._ctx(src, dst, send_sems, recv_sems)
        for w in range(len(src)):
            pltpu.make_async_copy(src[w], place(w, me), local_sems.at[w]).start()
        for w in range(len(src)):
            copy(w, 0, me, sibling, from_src=True).start()
            for j, chip in enumerate(chips):
                copy(w, 1 + j, me, (*chip, me[2]), from_src=True).start()

    def _forward(self, src, dst, send_sems, recv_sems, local_sems):
        me, sibling, chips, place, copy = self._ctx(src, dst, send_sems, recv_sems)
        for w in range(len(src)):
            for j, chip in enumerate(chips):
                copy(w, 1 + j, (*chip, me[2]), me).wait_recv()
                copy(w, 4 + j, (*chip, me[2]), sibling).start()

    def _finish(self, src, dst, send_sems, recv_sems, local_sems):
        me, sibling, chips, place, copy = self._ctx(src, dst, send_sems, recv_sems)
        for w in range(len(src)):
            copy(w, 0, sibling, me).wait_recv()
            for j, chip in enumerate(chips):
                copy(w, 4 + j, (*chip, 1 - me[2]), me).wait_recv()
        for w in range(len(src)):
            copy(w, 0, me, sibling, from_src=True).wait_send()
            for j, chip in enumerate(chips):
                copy(w, 1 + j, me, (*chip, me[2]), from_src=True).wait_send()
                copy(w, 4 + j, (*chip, me[2]), sibling).wait_send()
            pltpu.make_async_copy(src[w], place(w, me), local_sems.at[w]).wait()


REL_NEIGHBOURS = [(1, 0, 0), (1, 0, 1), (0, 1, 0), (0, 1, 1)]
REL_REST = [(1, 1, 0), (1, 1, 1), (0, 0, 1)]
REL_ALL = REL_NEIGHBOURS + REL_REST


class _ExchangeJob:
    def __init__(self, grads, kinds, shard_shapes, rels, with_own, by_source=None):
        self.ins = list(grads) + ([by_source] if by_source is not None else [])
        self.kinds, self.shard_shapes, self.rels, self.with_own = list(kinds), list(shard_shapes), list(rels), with_own
        self.nw, self.has_src = len(grads), by_source is not None
        slots = len(rels) + int(with_own)
        self.out_shape = [jax.ShapeDtypeStruct((slots,) + tuple(s), g.dtype) for g, s in zip(grads, shard_shapes)]
        if self.has_src:
            self.out_shape.append(jax.ShapeDtypeStruct((N_DEV,) + by_source.shape, by_source.dtype))
        n = len(rels) * self.nw + len(REL_ALL) * int(self.has_src)
        self.sems = [pltpu.SemaphoreType.DMA((n,)), pltpu.SemaphoreType.DMA((n,)), pltpu.SemaphoreType.DMA((len(self.ins),))]
        self.phases = [(0.0, self._start), (1.0, self._finish)]

    def _copies(self, src, land, send_sems, recv_sems, local_sems):
        x, y, c = _my_place()
        my_idx = _flat(x, y, c)
        nw = self.nw

        def part(w, dev_idx):
            r, cc = self.shard_shapes[w]
            return _block_of(src[w], self.kinds[w], dev_idx, r, cc)

        def remote_copy(s, src_ref, dst_ref, rel):
            peer = (x ^ rel[0], y ^ rel[1], c ^ rel[2])
            return pltpu.make_async_remote_copy(src_ref=src_ref, dst_ref=dst_ref, send_sem=send_sems.at[s],
                                                recv_sem=recv_sems.at[s], device_id=peer, device_id_type=MESH)

        local, remote = [], []
        if self.with_own:
            local += [pltpu.make_async_copy(part(w, my_idx), land[w].at[0], local_sems.at[w]) for w in range(nw)]
        for k, rel in enumerate(self.rels):
            peer_idx = _flat(x ^ rel[0], y ^ rel[1], c ^ rel[2])
            remote += [remote_copy(nw * k + w, part(w, peer_idx), land[w].at[k + int(self.with_own)], rel) for w in range(nw)]
        if self.has_src:
            local.append(pltpu.make_async_copy(src[nw], land[nw].at[my_idx], local_sems.at[nw]))
            remote += [remote_copy(nw * len(self.rels) + k, src[nw], land[nw].at[my_idx], rel) for k, rel in enumerate(REL_ALL)]
        return local, remote

    def _start(self, *refs):
        local, remote = self._copies(*refs)
        for cp in local + remote:
            cp.start()

    def _finish(self, *refs):
        local, remote = self._copies(*refs)
        for cp in remote + local:
            cp.wait()


class _Carried:
    def __init__(self, jobs, grid):
        self.jobs = list(jobs)
        self.steps = int(np.prod(grid))
        self.grid = tuple(grid)
        self.operands = [a for j in self.jobs for a in j.ins]
        self.out_shape = [o for j in self.jobs for o in j.out_shape]
        self.sems = [s for j in self.jobs for s in j.sems]
        self.in_specs = [ANY] * len(self.operands)
        self.out_specs = [ANY] * len(self.out_shape)

    def _step(self):
        lin = pl.program_id(0)
        for a in range(1, len(self.grid)):
            lin = lin * self.grid[a] + pl.program_id(a)
        return lin

    def _run(self, in_refs, out_refs, sem_refs, last):
        lin = self._step()
        for job in self.jobs:
            ins, in_refs = in_refs[:len(job.ins)], in_refs[len(job.ins):]
            outs, out_refs = out_refs[:len(job.out_shape)], out_refs[len(job.out_shape):]
            sems, sem_refs = sem_refs[:len(job.sems)], sem_refs[len(job.sems):]
            for frac, fn in job.phases:
                if (frac >= 1.0) != last:
                    continue
                at = min(self.steps - 1, int(frac * (self.steps - 1)))
                pl.when(lin == at)(functools.partial(fn, ins, outs, *sems))

    def before(self, in_refs, out_refs, sem_refs):
        self._run(in_refs, out_refs, sem_refs, last=False)

    def after(self, in_refs, out_refs, sem_refs):
        self._run(in_refs, out_refs, sem_refs, last=True)


def _standalone(name, job):
    def body(*refs):
        ni, no = len(job.ins), len(job.out_shape)
        for _, fn in job.phases:
            fn(refs[:ni], refs[ni:ni + no], *refs[ni + no:])

    return pl.pallas_call(body, name=name, in_specs=[ANY] * len(job.ins), out_specs=[ANY] * len(job.out_shape),
                          out_shape=job.out_shape, scratch_shapes=job.sems)(*job.ins)


def _split_refs(refs, n_in, n_job_in, n_out, n_job_out, n_scratch):
    cuts = np.cumsum([0, n_in, n_job_in, n_out, n_job_out, n_scratch])
    return [refs[cuts[i]:cuts[i + 1]] for i in range(5)] + [refs[cuts[5]:]]


def _matmul(name, a, b, *, ta=False, tb=False, out_dtypes=(F32,), epilogue=None, extras=(), tm=1024, tn=1024, tk=512, jobs=()):
    if ta:
        K, M = a.shape
    else:
        M, K = a.shape
    N = b.shape[0] if tb else b.shape[1]
    tm, tn, tk = _tile(M, tm), _tile(N, tn), _tile(K, tk)
    nk = K // tk
    grid = (M // tm, N // tn, nk)
    dims = TN if ta else (NT if tb else NN)
    n_ex, n_out = len(extras), len(out_dtypes)
    carried = _Carried(jobs, grid)

    def body(*refs):
        ins, job_ins, out_refs, job_outs, (acc_ref,), job_sems = _split_refs(
            refs, 2 + n_ex, len(carried.operands), n_out, len(carried.out_shape), 1)
        a_ref, b_ref, ex_refs = ins[0], ins[1], ins[2:]
        k = pl.program_id(2)
        carried.before(job_ins, job_outs, job_sems)

        @pl.when(k == 0)
        def _():
            acc_ref[...] = jnp.zeros_like(acc_ref)

        acc_ref[...] += _dot(a_ref[...], b_ref[...], dims)

        @pl.when(k == nk - 1)
        def _():
            acc = acc_ref[...]
            res = (acc,) if epilogue is None else epilogue(acc, *[e[...] for e in ex_refs])
            for o_ref, r in zip(out_refs, res):
                o_ref[...] = r.astype(o_ref.dtype)

        carried.after(job_ins, job_outs, job_sems)

    a_spec = pl.BlockSpec((tk, tm), lambda i, j, k: (k, i)) if ta else pl.BlockSpec((tm, tk), lambda i, j, k: (i, k))
    b_spec = pl.BlockSpec((tn, tk), lambda i, j, k: (j, k)) if tb else pl.BlockSpec((tk, tn), lambda i, j, k: (k, j))
    tile_spec = pl.BlockSpec((tm, tn), lambda i, j, k: (i, j))
    outs = pl.pallas_call(
        body,
        name=name,
        grid=grid,
        in_specs=[a_spec, b_spec] + [tile_spec] * n_ex + carried.in_specs,
        out_specs=[tile_spec] * n_out + carried.out_specs,
        out_shape=[jax.ShapeDtypeStruct((M, N), d) for d in out_dtypes] + carried.out_shape,
        scratch_shapes=[pltpu.VMEM((tm, tn), F32)] + carried.sems,
        compiler_params=_params(3),
    )(a, b, *extras, *carried.operands)
    return outs[0] if len(outs) == 1 else outs


def _rowwise(name, fn, row_in, const_in, row_out, acc_out, tm):
    T = row_in[0][0].shape[0]
    n_r, n_c, n_o, n_a = len(row_in), len(const_in), len(row_out), len(acc_out)

    def body(*refs):
        ins = [r[...] for r in refs[:n_r + n_c]]
        out_refs = refs[n_r + n_c:n_r + n_c + n_o]
        acc_refs = refs[n_r + n_c + n_o:]
        res = fn(*ins)
        for o_ref, r in zip(out_refs, res[:n_o]):
            o_ref[...] = r.astype(o_ref.dtype)

        @pl.when(pl.program_id(0) == 0)
        def _():
            for a_ref in acc_refs:
                a_ref[...] = jnp.zeros_like(a_ref)

        for a_ref, r in zip(acc_refs, res[n_o:]):
            a_ref[...] += jnp.sum(r.reshape(tm // SUBLANE, SUBLANE, r.shape[-1]), axis=0)

    in_specs = [pl.BlockSpec((tm, w), functools.partial(lambda i, c: (i, c), c=cb)) for (_, w, cb) in row_in]
    in_specs += [pl.BlockSpec(c.shape, functools.partial(lambda i, nd: (0,) * nd, nd=c.ndim)) for c in const_in]
    out_specs = [pl.BlockSpec((tm, c), lambda i: (i, 0)) for (c, _) in row_out]
    out_specs += [pl.BlockSpec((SUBLANE, c), lambda i: (0, 0)) for c in acc_out]
    out_shape = [jax.ShapeDtypeStruct((T, c), d) for (c, d) in row_out]
    out_shape += [jax.ShapeDtypeStruct((SUBLANE, c), F32) for c in acc_out]
    return pl.pallas_call(
        body, name=name, grid=(T // tm,), in_specs=in_specs, out_specs=out_specs, out_shape=out_shape,
        compiler_params=_params(1),
    )(*[r[0] for r in row_in], *const_in)


def _ln_stats(z):
    mu = jnp.mean(z, axis=-1, keepdims=True)
    var = jnp.mean(jnp.square(z - mu), axis=-1, keepdims=True)
    rstd = lax.rsqrt(var + EPS)
    return (z - mu) * rstd, rstd


def _ln_bwd(dy, xhat, rstd, g):
    dxh = dy * g
    return rstd * (dxh - jnp.mean(dxh, axis=-1, keepdims=True) - xhat * jnp.mean(dxh * xhat, axis=-1, keepdims=True))


def _rms(x, g):
    return x * lax.rsqrt(jnp.mean(jnp.square(x), axis=-1, keepdims=True) + EPS) * g


def _rope128(x, cos2, sin2):
    return x * cos2 + pltpu.roll(x, RET_DK // 2, 1) * sin2


def _unrope128(dy, cos2, sin2):
    return dy * cos2 + pltpu.roll(dy * sin2, RET_DK // 2, 1)


def _rope64(x, cosp, sina, sinb):
    h = MLA_ROPE // 2
    return x * cosp + pltpu.roll(x, h, 1) * sina + pltpu.roll(x, LANE - h, 1) * sinb


def _unrope64(dy, cosp, sina, sinb):
    h = MLA_ROPE // 2
    return dy * cosp + pltpu.roll(dy * sina, LANE - h, 1) + pltpu.roll(dy * sinb, h, 1)


RET_BLOCK = 256


def _ret_tables(H, B):
    log_g = np.log1p(-np.exp2(-5.0 - np.arange(H, dtype=np.float32))).astype(np.float32)
    idx = np.arange(B, dtype=np.float32)
    dist = np.abs(idx[:, None] - idx[None, :])
    visible = (np.arange(B)[None, :] // CHUNK) <= (np.arange(B)[:, None] // CHUNK)
    dmat = np.exp(log_g[:, None, None] * dist).astype(np.float32) * visible[None].astype(np.float32)
    qd = np.exp(log_g[:, None] * (idx + 1.0)).astype(np.float32)
    kd = np.exp(log_g[:, None] * (B - 1 - idx)).astype(np.float32)
    sd = np.exp(log_g * B).astype(np.float32)
    bc = lambda v: np.ascontiguousarray(np.broadcast_to(v[:, :, None], (H, v.shape[1], LANE)))
    sdb = np.ascontiguousarray(np.broadcast_to(sd[:, None, None], (H, SUBLANE, LANE)))
    return jnp.asarray(dmat), jnp.asarray(bc(qd)), jnp.asarray(bc(kd)), jnp.asarray(sdb)


def _ret_specs(H, B, reverse, nb):
    blk = (lambda b: nb - 1 - b) if reverse else (lambda b: b)
    col = lambda off: pl.BlockSpec((B, LANE), functools.partial(lambda h, b, off: (blk(b), off + h), off=off))
    row = pl.BlockSpec((B, LANE), lambda h, b: (blk(b), 0))
    per_head = lambda r: pl.BlockSpec((1, r, LANE), lambda h, b: (h, 0, 0))
    dmat = pl.BlockSpec((1, B, B), lambda h, b: (h, 0, 0))
    return col, row, per_head, dmat


def _retention_fwd(p1, cos2, sin2, gn_g, H):
    T = p1.shape[0]
    B = min(RET_BLOCK, T)
    nb = T // B
    scale = RET_DK ** -0.5
    dmat, qd, kd, sd = _ret_tables(H, B)
    col, row, per_head, dspec = _ret_specs(H, B, False, nb)

    def body(q_ref, k_ref, v_ref, g_ref, cos_ref, sin_ref, d_ref, qd_ref, kd_ref, sd_ref, gn_ref, pre_ref, out_ref, s_ref):
        @pl.when(pl.program_id(1) == 0)
        def _():
            s_ref[...] = jnp.zeros_like(s_ref)

        cos, sin = cos_ref[...], sin_ref[...]
        q = _rope128(q_ref[...], cos, sin)
        k = _rope128(k_ref[...], cos, sin) * scale
        v = v_ref[...]
        s = s_ref[...]
        a = _dot(q, k, NT) * d_ref[0]
        ret = _dot(a, v) + _dot(q * qd_ref[0], s)
        s_ref[...] = sd_ref[0, 0:1, :] * s + _dot(k * kd_ref[0], v, TN)
        pre_ref[...] = ret
        nrm, _ = _ln_stats(ret)
        out_ref[...] = (jax.nn.silu(g_ref[...]) * (nrm * gn_ref[...])).astype(out_ref.dtype)

    out_spec = pl.BlockSpec((B, LANE), lambda h, b: (b, h))
    return pl.pallas_call(
        body, name="retention_fwd", grid=(H, nb),
        in_specs=[col(0), col(H), col(2 * H), col(3 * H), row, row, dspec, per_head(B), per_head(B), per_head(SUBLANE),
                  pl.BlockSpec((1, LANE), lambda h, b: (0, h))],
        out_specs=[out_spec, out_spec],
        out_shape=[jax.ShapeDtypeStruct((T, H * RET_DV), F32), jax.ShapeDtypeStruct((T, H * RET_DV), BF16)],
        scratch_shapes=[pltpu.VMEM((RET_DK, RET_DV), F32)],
        compiler_params=_params(2),
    )(p1, p1, p1, p1, cos2, sin2, dmat, qd, kd, sd, gn_g)


def _retention_bwd(p1, dpre, cos2, sin2, H):
    T = p1.shape[0]
    B = min(RET_BLOCK, T)
    nb = T // B
    scale = RET_DK ** -0.5
    dmat, qd, kd, sd = _ret_tables(H, B)

    def dq_body(q_ref, k_ref, v_ref, do_ref, cos_ref, sin_ref, d_ref, qd_ref, kd_ref, sd_ref, dq_ref, s_ref):
        @pl.when(pl.program_id(1) == 0)
        def _():
            s_ref[...] = jnp.zeros_like(s_ref)

        cos, sin = cos_ref[...], sin_ref[...]
        k = _rope128(k_ref[...], cos, sin) * scale
        v, do, s = v_ref[...], do_ref[...], s_ref[...]
        da = _dot(do, v, NT) * d_ref[0]
        dq = _dot(da, k) + _dot(do, s, NT) * qd_ref[0]
        dq_ref[...] = _unrope128(dq, cos, sin).astype(dq_ref.dtype)
        s_ref[...] = sd_ref[0, 0:1, :] * s + _dot(k * kd_ref[0], v, TN)

    col, row, per_head, dspec = _ret_specs(H, B, False, nb)
    dq = pl.pallas_call(
        dq_body, name="retention_bwd_dq", grid=(H, nb),
        in_specs=[col(0), col(H), col(2 * H), col(0), row, row, dspec, per_head(B), per_head(B), per_head(SUBLANE)],
        out_specs=pl.BlockSpec((B, LANE), lambda h, b: (b, h)),
        out_shape=jax.ShapeDtypeStruct((T, H * RET_DK), BF16),
        scratch_shapes=[pltpu.VMEM((RET_DK, RET_DV), F32)],
        compiler_params=_params(2),
    )(p1, p1, p1, dpre, cos2, sin2, dmat, qd, kd, sd)

    def dkv_body(q_ref, k_ref, v_ref, do_ref, cos_ref, sin_ref, d_ref, qd_ref, kd_ref, sd_ref, dk_ref, dv_ref, g_ref):
        @pl.when(pl.program_id(1) == 0)
        def _():
            g_ref[...] = jnp.zeros_like(g_ref)

        cos, sin = cos_ref[...], sin_ref[...]
        q = _rope128(q_ref[...], cos, sin)
        k = _rope128(k_ref[...], cos, sin) * scale
        v, do, g = v_ref[...], do_ref[...], g_ref[...]
        a = _dot(q, k, NT) * d_ref[0]
        da = _dot(do, v, NT) * d_ref[0]
        dv = _dot(a, do, TN) + _dot(k * kd_ref[0], g)
        dk = _dot(da, q, TN) + _dot(v, g, NT) * kd_ref[0]
        dv_ref[...] = dv.astype(dv_ref.dtype)
        dk_ref[...] = _unrope128(dk * scale, cos, sin).astype(dk_ref.dtype)
        g_ref[...] = sd_ref[0, 0:1, :] * g + _dot(q * qd_ref[0], do, TN)

    col, row, per_head, dspec = _ret_specs(H, B, True, nb)
    out_spec = pl.BlockSpec((B, LANE), lambda h, b: (nb - 1 - b, h))
    dk, dv = pl.pallas_call(
        dkv_body, name="retention_bwd_dkv", grid=(H, nb),
        in_specs=[col(0), col(H), col(2 * H), col(0), row, row, dspec, per_head(B), per_head(B), per_head(SUBLANE)],
        out_specs=[out_spec, out_spec],
        out_shape=[jax.ShapeDtypeStruct((T, H * RET_DK), BF16), jax.ShapeDtypeStruct((T, H * RET_DV), BF16)],
        scratch_shapes=[pltpu.VMEM((RET_DK, RET_DV), F32)],
        compiler_params=_params(2),
    )(p1, p1, p1, dpre, cos2, sin2, dmat, qd, kd, sd)
    return dq, dk, dv


ATT_BLOCK = 512


def _chunk_mask(t):
    qi = lax.broadcasted_iota(jnp.int32, (t, t), 0) // CHUNK
    kj = lax.broadcasted_iota(jnp.int32, (t, t), 1) // CHUNK
    return kj <= qi


def _attention_fwd(qh, kh, kv, H, jobs=()):
    T = qh.shape[0]
    t = min(ATT_BLOCK, T)
    n = T // t
    scale = (MLA_NOPE + MLA_ROPE) ** -0.5
    carried = _Carried(jobs, (H, n, n))

    def body(*refs):
        (q_ref, k_ref, v_ref), job_ins, (o_ref, ob_ref, lse_ref), job_outs, (m_ref, l_ref, acc_ref), job_sems = _split_refs(
            refs, 3, len(carried.operands), 3, len(carried.out_shape), 3)
        i, j = pl.program_id(1), pl.program_id(2)
        carried.before(job_ins, job_outs, job_sems)

        @pl.when(j == 0)
        def _():
            m_ref[...] = jnp.full_like(m_ref, -jnp.inf)
            l_ref[...] = jnp.zeros_like(l_ref)
            acc_ref[...] = jnp.zeros_like(acc_ref)

        @pl.when(j <= i)
        def _():
            s = _dot(q_ref[...], k_ref[...], NT) * scale
            s = jnp.where(jnp.logical_or(j < i, _chunk_mask(t)), s, -jnp.inf)
            m_prev = m_ref[...]
            m_new = jnp.maximum(m_prev, jnp.max(s, axis=-1, keepdims=True))
            alpha = jnp.exp(m_prev - m_new)
            p = jnp.exp(s - m_new)
            l_ref[...] = alpha * l_ref[...] + jnp.sum(p, axis=-1, keepdims=True)
            acc_ref[...] = alpha * acc_ref[...] + _dot(p, v_ref[...])
            m_ref[...] = m_new

        @pl.when(j == n - 1)
        def _():
            o = acc_ref[...] / l_ref[...]
            o_ref[...] = o
            ob_ref[...] = o.astype(ob_ref.dtype)
            lse_ref[0] = m_ref[...] + jnp.log(l_ref[...])

        carried.after(job_ins, job_outs, job_sems)

    kclamp = lambda i, j: jnp.minimum(j, i)
    o_spec = pl.BlockSpec((t, MLA_DV), lambda h, i, j: (i, h))
    return pl.pallas_call(
        body, name="attention_fwd", grid=(H, n, n),
        in_specs=[pl.BlockSpec((t, MLA_HEAD_PAD), lambda h, i, j: (i, h)),
                  pl.BlockSpec((t, MLA_HEAD_PAD), lambda h, i, j: (kclamp(i, j), h)),
                  pl.BlockSpec((t, MLA_DV), lambda h, i, j: (kclamp(i, j), H + h))] + carried.in_specs,
        out_specs=[o_spec, o_spec, pl.BlockSpec((1, t, 1), lambda h, i, j: (h, i, 0))] + carried.out_specs,
        out_shape=[jax.ShapeDtypeStruct((T, H * MLA_DV), F32), jax.ShapeDtypeStruct((T, H * MLA_DV), BF16),
                   jax.ShapeDtypeStruct((H, T, 1), F32)] + carried.out_shape,
        scratch_shapes=[pltpu.VMEM((t, 1), F32), pltpu.VMEM((t, 1), F32), pltpu.VMEM((t, MLA_DV), F32)] + carried.sems,
        compiler_params=_params(3),
    )(qh, kh, kv, *carried.operands)


def _attention_bwd(qh, kh, kv, o, lse, dmix, H, jobs=()):
    T = qh.shape[0]
    t = min(ATT_BLOCK, T)
    n = T // t
    scale = (MLA_NOPE + MLA_ROPE) ** -0.5
    carried = _Carried(jobs, (H, n, n))

    def body(*refs):
        ((q_ref, k_ref, v_ref, do_ref, o_ref, lse_ref), job_ins, (dq_ref, dkn_ref, dkr_ref, dv_ref), job_outs,
         (dk_acc, dv_acc), job_sems) = _split_refs(refs, 6, len(carried.operands), 4, len(carried.out_shape), 2)
        j, i = pl.program_id(1), pl.program_id(2)
        carried.before(job_ins, job_outs, job_sems)

        @pl.when(jnp.logical_and(j == 0, i == 0))
        def _():
            dq_ref[...] = jnp.zeros_like(dq_ref)

        @pl.when(i == j)
        def _():
            dk_acc[...] = jnp.zeros_like(dk_acc)
            dv_acc[...] = jnp.zeros_like(dv_acc)

        @pl.when(i >= j)
        def _():
            q, k, v, do = q_ref[...], k_ref[...], v_ref[...], do_ref[...]
            s = _dot(q, k, NT) * scale
            s = jnp.where(jnp.logical_or(j < i, _chunk_mask(t)), s, -jnp.inf)
            p = jnp.exp(s - lse_ref[0])
            dv_acc[...] += _dot(p, do, TN)
            dp = _dot(do, v, NT)
            delta = jnp.sum(do * o_ref[...], axis=-1, keepdims=True)
            ds = p * (dp - delta) * scale
            rows = pl.ds(pl.multiple_of(i * t, t), t)
            dq_ref[rows, :] += _dot(ds, k)
            dk_acc[...] += _dot(ds, q, TN)

        @pl.when(i == n - 1)
        def _():
            dk = dk_acc[...]
            dkn_ref[...] = dk[:, :MLA_NOPE].astype(dkn_ref.dtype)
            dkr_ref[...] = dk[:, MLA_NOPE:]
            dv_ref[...] = dv_acc[...].astype(dv_ref.dtype)

        carried.after(job_ins, job_outs, job_sems)

    qclamp = lambda j, i: jnp.maximum(i, j)
    kv_out = pl.BlockSpec((t, LANE), lambda h, j, i: (j, h))
    return pl.pallas_call(
        body, name="attention_bwd", grid=(H, n, n),
        in_specs=[pl.BlockSpec((t, MLA_HEAD_PAD), lambda h, j, i: (qclamp(j, i), h)),
                  pl.BlockSpec((t, MLA_HEAD_PAD), lambda h, j, i: (j, h)),
                  pl.BlockSpec((t, MLA_DV), lambda h, j, i: (j, H + h)),
                  pl.BlockSpec((t, MLA_DV), lambda h, j, i: (qclamp(j, i), H + h)),
                  pl.BlockSpec((t, MLA_DV), lambda h, j, i: (qclamp(j, i), h)),
                  pl.BlockSpec((1, t, 1), lambda h, j, i: (h, qclamp(j, i), 0))] + carried.in_specs,
        out_specs=[pl.BlockSpec((T, MLA_HEAD_PAD), lambda h, j, i: (0, h)), kv_out, kv_out, kv_out] + carried.out_specs,
        out_shape=[jax.ShapeDtypeStruct((T, H * MLA_HEAD_PAD), F32), jax.ShapeDtypeStruct((T, H * LANE), BF16),
                   jax.ShapeDtypeStruct((T, H * LANE), F32), jax.ShapeDtypeStruct((T, H * LANE), BF16)] + carried.out_shape,
        scratch_shapes=[pltpu.VMEM((t, MLA_HEAD_PAD), F32), pltpu.VMEM((t, MLA_DV), F32)] + carried.sems,
        compiler_params=_params(3),
    )(qh, kh, kv, dmix, o, lse, *carried.operands)


def _adamw_math(w, g, m, v):
    m = ADAM_B1 * m + (1.0 - ADAM_B1) * g
    v = ADAM_B2 * v + (1.0 - ADAM_B2) * jnp.square(g)
    m_hat = m / (1.0 - ADAM_B1 ** ADAM_STEP)
    v_hat = v / (1.0 - ADAM_B2 ** ADAM_STEP)
    delta = -ADAM_LR * (m_hat / (jnp.sqrt(v_hat) + ADAM_EPS) + ADAM_WD * w)
    return delta, m, v


def _adamw(name, landed, w, m, v):
    R, C = w.shape
    tr = 16
    while tr * 2 * C <= ADAMW_TILE_ELEMS and R % (tr * 2) == 0:
        tr *= 2
    tr = tr if R % tr == 0 else R
    nl = len(landed)
    assert sum(l.shape[0] for l in landed) == N_DEV

    def body(*refs):
        l_refs, (w_ref, m_ref, v_ref, g_out, d_out, m_out, v_out) = refs[:nl], refs[nl:]
        g = None
        for l_ref in l_refs:
            for s in range(l_ref.shape[0]):
                part = l_ref[s].astype(F32)
                g = part if g is None else g + part
        delta, mn, vn = _adamw_math(w_ref[...], g, m_ref[...], v_ref[...])
        g_out[...] = g
        d_out[...] = delta
        m_out[...] = mn
        v_out[...] = vn

    spec = pl.BlockSpec((tr, C), lambda i: (i, 0))
    l_specs = [pl.BlockSpec((l.shape[0], tr, C), lambda i: (0, i, 0)) for l in landed]
    return pl.pallas_call(
        body, name=name, grid=(R // tr,), in_specs=l_specs + [spec, spec, spec], out_specs=[spec] * 4,
        out_shape=[jax.ShapeDtypeStruct((R, C), F32)] * 4, compiler_params=_params(1),
    )(*landed, w, m, v)


def _adamw_small(landed, w, m, v):
    P = w.shape[1]

    def body(l_ref, w_ref, m_ref, v_ref, g_out, d_out, m_out, v_out):
        acc = l_ref[0]
        for s in range(1, N_DEV):
            acc = acc + l_ref[s]
        g = jnp.sum(acc, axis=0, keepdims=True)
        delta, mn, vn = _adamw_math(w_ref[...], g, m_ref[...], v_ref[...])
        g_out[...] = g
        d_out[...] = delta
        m_out[...] = mn
        v_out[...] = vn

    return pl.pallas_call(
        body, name="adamw_replicated", out_shape=[jax.ShapeDtypeStruct((1, P), F32)] * 4,
    )(landed, w, m, v)


def _rope_tables(pos, T):
    p = pos.astype(F32)[:, None]
    inv_r = ROPE_BASE ** (-jnp.arange(0, RET_DK, 2, dtype=F32) / RET_DK)
    ang = p * inv_r
    cos, sin = jnp.cos(ang), jnp.sin(ang)
    cos2 = jnp.concatenate([cos, cos], axis=-1)
    sin2 = jnp.concatenate([-sin, sin], axis=-1)
    inv_m = ROPE_BASE ** (-jnp.arange(0, MLA_ROPE, 2, dtype=F32) / MLA_ROPE)
    ang = p * inv_m
    cos, sin = jnp.cos(ang), jnp.sin(ang)
    zh = jnp.zeros_like(cos)
    z2 = jnp.zeros((T, LANE - MLA_ROPE), F32)
    cosp = jnp.concatenate([cos, cos, z2], axis=-1)
    sina = jnp.concatenate([zh, sin, z2], axis=-1)
    sinb = jnp.concatenate([-sin, zh, z2], axis=-1)
    return cos2, sin2, cosp, sina, sinb


def kernel(x, positions, w_in, q_norm_g, w_uq, kv_norm_g, w_uk, w_uv, ret_gn_g, w_out, ln1_g, ln1_b, w_up, w_down, ln2_g, ln2_b, loss_target, m_w_in, m_q_norm_g, m_w_uq, m_kv_norm_g, m_w_uk, m_w_uv, m_ret_gn_g, m_w_out, m_ln1_g, m_ln1_b, m_w_up, m_w_down, m_ln2_g, m_ln2_b, v_w_in, v_q_norm_g, v_w_uq, v_kv_norm_g, v_w_uk, v_w_uv, v_ret_gn_g, v_w_out, v_ln1_g, v_ln1_b, v_w_up, v_w_down, v_ln2_g, v_ln2_b):
    T, D = x.shape[1], x.shape[2]
    H = D // 256
    RW = H * RET_DV
    x = x[0]
    tgt = loss_target[0]
    alpha = DEEPNORM_ALPHA
    in_width = w_in.shape[2] * N_DEV
    mla_in = in_width - 4 * RW
    mla_in_pad = -(-mla_in // 512) * 512
    f_uq = MLA_NOPE + MLA_ROPE

    names = ["w_in", "w_uq", "w_uk", "w_uv", "w_out", "w_up", "w_down"]
    big = dict(zip(names, [w_in[0], w_uq[0], w_uk[0], w_uv[0], w_out[0], w_up[0], w_down[0]]))
    kind = dict(zip(names, ["blk", "blk", "cols", "cols", "rows", "cols", "rows"]))
    half = {n: w.astype(BF16) for n, w in big.items()}

    def gather(ns, mid):
        return _GatherJob([half[n] for n in ns], [kind[n] for n in ns], mid)

    (g_in,) = _standalone("gather_w_in", gather(["w_in"], 0.0))
    win = jnp.transpose(g_in, (1, 0, 2)).reshape(D, in_width)
    w1 = win[:, :4 * RW]
    w2 = jnp.pad(win[:, 4 * RW:], ((0, 0), (0, mla_in_pad - mla_in)))

    cos2, sin2, cosp, sina, sinb = _rope_tables(positions[0], T)
    xb = x.astype(BF16)
    tr = min(256, T)

    p1, g_uq, wuk, wuv, wout = _matmul("proj_ret", xb, w1,
                                       jobs=[gather(["w_uq", "w_uk", "w_uv", "w_out"], 0.65)])
    wuq = jnp.transpose(g_uq, (1, 0, 2)).reshape(MLA_Q_RANK, H, f_uq)
    wuq = jnp.pad(wuq, ((0, 0), (0, 0), (0, MLA_HEAD_PAD - f_uq))).reshape(MLA_Q_RANK, H * MLA_HEAD_PAD)
    wukv = jnp.concatenate([wuk, wuv], axis=1)
    p2 = _matmul("proj_mla", xb, w2)
    c_kv, c_kr = MLA_Q_RANK, MLA_Q_RANK + MLA_KV_RANK

    def mla_prep(p, cp, sa, sb, gq, gkv):
        cqn = _rms(p[:, :c_kv], gq)
        ckvn = _rms(p[:, c_kv:c_kr], gkv)
        return cqn, ckvn, _rope64(p[:, c_kr:c_kr + LANE], cp, sa, sb)

    cqn, ckvn, krope = _rowwise(
        "mla_prep", mla_prep, [(p2, mla_in_pad, 0), (cosp, LANE, 0), (sina, LANE, 0), (sinb, LANE, 0)],
        [q_norm_g, kv_norm_g], [(MLA_Q_RANK, BF16), (MLA_KV_RANK, BF16), (LANE, BF16)], [], tr)
    qf = _matmul("mla_q", cqn, wuq)
    kv = _matmul("mla_kv", ckvn, wukv, out_dtypes=(BF16,))

    def mla_heads(q, kn, kr, cp, sa, sb):
        qs, ks = [], []
        for h in range(H):
            o = h * MLA_HEAD_PAD
            qs += [q[:, o:o + MLA_NOPE], _rope64(q[:, o + MLA_NOPE:o + MLA_HEAD_PAD], cp, sa, sb)]
            ks += [kn[:, h * MLA_NOPE:(h + 1) * MLA_NOPE], kr]
        return jnp.concatenate(qs, axis=1), jnp.concatenate(ks, axis=1)

    qh, kh = _rowwise(
        "mla_heads", mla_heads,
        [(qf, H * MLA_HEAD_PAD, 0), (kv, RW, 0), (krope, LANE, 0), (cosp, LANE, 0), (sina, LANE, 0), (sinb, LANE, 0)],
        [], [(H * MLA_HEAD_PAD, BF16), (H * MLA_HEAD_PAD, BF16)], [], tr)
    att_o, att_ob, lse, wup = _attention_fwd(qh, kh, kv, H, jobs=[gather(["w_up"], 0.5)])
    ret_pre, ret_ob = _retention_fwd(p1, cos2, sin2, ret_gn_g, H)
    mixin = jnp.concatenate([ret_ob, att_ob], axis=1)
    mix = _matmul("mix_out", mixin, wout)

    def ln1_fwd(xt, mt, g, b):
        z = alpha * xt + mt
        xhat, _ = _ln_stats(z)
        y = xhat * g + b
        return z, y, y

    z1, x1, x1b = _rowwise("ln1", ln1_fwd, [(x, D, 0), (mix, D, 0)], [ln1_g, ln1_b],
                           [(D, F32), (D, F32), (D, BF16)], [], tr)

    def relu2(acc):
        r = jnp.maximum(acc, 0.0)
        return r * r, r

    a2b, rb, wdown = _matmul("mlp_up", x1b, wup, out_dtypes=(BF16, BF16), epilogue=relu2, jobs=[gather(["w_down"], 0.8)])
    hmlp = _matmul("mlp_down", a2b, wdown)

    def ln2_loss(x1t, ht, tt, g, b):
        xhat, rstd = _ln_stats(alpha * x1t + ht)
        err = xhat * g + b - tt
        dy = err / D
        dz = _ln_bwd(dy, xhat, rstd, g)
        lrow = 0.5 * jnp.mean(jnp.square(err), axis=-1, keepdims=True)
        return dz, dz, dy * xhat, dy, jnp.broadcast_to(lrow, (lrow.shape[0], LANE))

    dz2, dz2b, dg2, db2, lpart = _rowwise("ln2_loss", ln2_loss, [(x1, D, 0), (hmlp, D, 0), (tgt, D, 0)], [ln2_g, ln2_b],
                                          [(D, F32), (D, BF16)], [D, D, LANE], tr)
    loss = lax.psum(jnp.sum(lpart[:, 0]), ("x", "y", "c"))

    def exchange(grads, rels, with_own, by_source=None):
        ns = list(grads)
        return _ExchangeJob([grads[n] for n in ns], [kind[n] for n in ns], [big[n].shape for n in ns], rels, with_own, by_source)

    landed = {}
    da = _matmul("mlp_down_dx", dz2b, wdown, tb=True, out_dtypes=(BF16,), extras=(rb,),
                 epilogue=lambda acc, r: (acc * (2.0 * r.astype(F32)),))
    g_wdown = _matmul("mlp_down_dw", a2b, dz2b, ta=True, out_dtypes=(BF16,))
    g_wup, l_a = _matmul("mlp_up_dw", x1b, da, ta=True, out_dtypes=(BF16,),
                         jobs=[exchange({"w_down": g_wdown}, REL_NEIGHBOURS, True)])
    dx1m, l_b = _matmul("mlp_up_dx", da, wup, tb=True, jobs=[exchange({"w_down": g_wdown}, REL_REST, False)])
    landed["w_down"] = [l_a, l_b]

    def ln1_bwd(dm, dzz, z, g):
        xhat, rstd = _ln_stats(z)
        dy = dm + alpha * dzz
        dz = _ln_bwd(dy, xhat, rstd, g)
        return dz, dz, dy * xhat, dy

    dz1, dz1b, dg1, db1 = _rowwise("ln1_bwd", ln1_bwd, [(dx1m, D, 0), (dz2, D, 0), (z1, D, 0)], [ln1_g],
                                   [(D, F32), (D, BF16)], [D, D], tr)
    g_wout = _matmul("mix_out_dw", mixin, dz1b, ta=True, out_dtypes=(BF16,))
    dmix = _matmul("mix_out_dx", dz1b, wout, tb=True)

    dqh, dkn, dkr, dvv, l_a = _attention_bwd(qh, kh, kv, att_o, lse, dmix, H, jobs=[exchange({"w_up": g_wup}, REL_ALL, True)])
    landed["w_up"] = [l_a]

    def mla_heads_bwd(dq, dkrh, cp, sa, sb):
        parts, acc = [], dkrh[:, :LANE]
        for h in range(H):
            o = h * MLA_HEAD_PAD
            parts += [dq[:, o:o + MLA_NOPE], _unrope64(dq[:, o + MLA_NOPE:o + MLA_HEAD_PAD], cp, sa, sb)]
            if h:
                acc = acc + dkrh[:, h * LANE:(h + 1) * LANE]
        return jnp.concatenate(parts, axis=1), _unrope64(acc, cp, sa, sb)

    dqb, dkr128 = _rowwise(
        "mla_heads_bwd", mla_heads_bwd,
        [(dqh, H * MLA_HEAD_PAD, 0), (dkr, H * LANE, 0), (cosp, LANE, 0), (sina, LANE, 0), (sinb, LANE, 0)],
        [], [(H * MLA_HEAD_PAD, BF16), (LANE, F32)], [], tr)
    dcqn = _matmul("mla_q_dx", dqb, wuq, tb=True)
    g_wuq = _matmul("mla_q_dw", cqn, dqb, ta=True, out_dtypes=(BF16,))
    dkvb = jnp.concatenate([dkn, dvv], axis=1)
    dckvn = _matmul("mla_kv_dx", dkvb, wukv, tb=True)
    g_wukv = _matmul("mla_kv_dw", ckvn, dkvb, ta=True, out_dtypes=(BF16,))

    def rms_bwd(c, dy, g):
        rstd = lax.rsqrt(jnp.mean(jnp.square(c), axis=-1, keepdims=True) + EPS)
        dyg = dy * g
        dc = rstd * (dyg - c * (rstd * rstd) * jnp.mean(dyg * c, axis=-1, keepdims=True))
        return dc, dy * c * rstd

    def mla_prep_bwd(p, dq_, dkv_, dkr_, gq, gkv):
        dcq, dgq_ = rms_bwd(p[:, :c_kv], dq_, gq)
        dckv, dgkv_ = rms_bwd(p[:, c_kv:c_kr], dkv_, gkv)
        pad = jnp.zeros((p.shape[0], mla_in_pad - c_kr - LANE), F32)
        return jnp.concatenate([dcq, dckv, dkr_, pad], axis=1), dgq_, dgkv_

    d_p2, dgq, dgkv = _rowwise(
        "mla_prep_bwd", mla_prep_bwd, [(p2, mla_in_pad, 0), (dcqn, MLA_Q_RANK, 0), (dckvn, MLA_KV_RANK, 0), (dkr128, LANE, 0)],
        [q_norm_g, kv_norm_g], [(mla_in_pad, BF16)], [MLA_Q_RANK, MLA_KV_RANK], tr)

    def gate_bwd(pre, gate, dout, gn):
        dpre, dgate, dgn = [], [], []
        for h in range(H):
            sl = slice(h * RET_DV, (h + 1) * RET_DV)
            nrm, rstd = _ln_stats(pre[:, sl])
            gate_h, do_h, gn_h = gate[:, sl], dout[:, sl], gn[:, sl]
            sg = jax.nn.sigmoid(gate_h)
            silu = gate_h * sg
            dgate.append(do_h * (nrm * gn_h) * (sg * (1.0 + gate_h * (1.0 - sg))))
            dn = do_h * silu
            dgn.append(dn * nrm)
            dpre.append(_ln_bwd(dn, nrm, rstd, gn_h))
        return jnp.concatenate(dpre, axis=1), jnp.concatenate(dgate, axis=1), jnp.concatenate(dgn, axis=1)

    dpre, drg, dgn = _rowwise("ret_gate_bwd", gate_bwd, [(ret_pre, RW, 0), (p1, RW, 3), (dmix, RW, 0)], [ret_gn_g],
                              [(RW, F32), (RW, BF16)], [RW], tr)
    drq, drk, drv = _retention_bwd(p1, dpre, cos2, sin2, H)
    d_p1 = jnp.concatenate([drq, drk, drv, drg], axis=1)

    g_uq = g_wuq.reshape(MLA_Q_RANK, H, MLA_HEAD_PAD)[:, :, :f_uq].reshape(MLA_Q_RANK, N_DEV, H * f_uq // N_DEV)
    mid_grads = {"w_uq": jnp.transpose(g_uq, (1, 0, 2)), "w_uk": g_wukv[:, :RW], "w_uv": g_wukv[:, RW:], "w_out": g_wout}
    g_w1, *l_a = _matmul("proj_ret_dw", xb, d_p1, ta=True, out_dtypes=(BF16,), jobs=[exchange(mid_grads, REL_NEIGHBOURS, True)])
    g_w2, *l_b = _matmul("proj_mla_dw", xb, d_p2, ta=True, out_dtypes=(BF16,), jobs=[exchange(mid_grads, REL_REST, False)])
    for n, a_, b_ in zip(mid_grads, l_a, l_b):
        landed[n] = [a_, b_]

    g_win = jnp.concatenate([g_w1, g_w2[:, :mla_in]], axis=1)
    g_win = {"w_in": jnp.transpose(g_win.reshape(D, N_DEV, in_width // N_DEV), (1, 0, 2))}
    small = jnp.concatenate([dgq, dgkv, dgn, dg1, db1, dg2, db2], axis=1)
    dx_a, l_a = _matmul("proj_mla_dx", d_p2, w2, tb=True, extras=(dz1,), epilogue=lambda acc, d: (acc + alpha * d,),
                        jobs=[exchange(g_win, REL_NEIGHBOURS, True)])
    grad_x, l_b, small_landed = _matmul("proj_ret_dx", d_p1, w1, tb=True, extras=(dx_a,), epilogue=lambda acc, d: (acc + d,),
                                        jobs=[exchange(g_win, REL_REST, False, by_source=small)])
    landed["w_in"] = [l_a, l_b]

    moments_m = [m_w_in, m_w_uq, m_w_uk, m_w_uv, m_w_out, m_w_up, m_w_down]
    moments_v = [v_w_in, v_w_uq, v_w_uk, v_w_uv, v_w_out, v_w_up, v_w_down]
    res = {}
    for n, m, v in zip(names, moments_m, moments_v):
        res[n] = [r[None] for r in _adamw("adamw_" + n, landed[n], big[n], m[0], v[0])]

    small_names = ["q_norm_g", "kv_norm_g", "ret_gn_g", "ln1_g", "ln1_b", "ln2_g", "ln2_b"]
    small_w = [q_norm_g, kv_norm_g, ret_gn_g, ln1_g, ln1_b, ln2_g, ln2_b]
    small_m = [m_q_norm_g, m_kv_norm_g, m_ret_gn_g, m_ln1_g, m_ln1_b, m_ln2_g, m_ln2_b]
    small_v = [v_q_norm_g, v_kv_norm_g, v_ret_gn_g, v_ln1_g, v_ln1_b, v_ln2_g, v_ln2_b]
    cat = lambda arrs: jnp.concatenate(arrs, axis=1)
    sres = _adamw_small(small_landed, cat(small_w), cat(small_m), cat(small_v))
    off = 0
    for n, w in zip(small_names, small_w):
        res[n] = [r[:, off:off + w.shape[1]] for r in sres]
        off += w.shape[1]

    order = ["w_in", "q_norm_g", "w_uq", "kv_norm_g", "w_uk", "w_uv", "ret_gn_g", "w_out", "ln1_g", "ln1_b",
             "w_up", "w_down", "ln2_g", "ln2_b"]
    outs = [loss, grad_x[None]]
    for k in range(4):
        outs += [res[n][k] for n in order]
    return tuple(outs)
```

```python
import functools

import numpy as np

import jax
import jax.numpy as jnp
from jax import lax
from jax.experimental import pallas as pl
from jax.experimental.pallas import tpu as pltpu

F32 = jnp.float32
BF16 = jnp.bfloat16

CHUNK = 64
RET_DK = 128
RET_DV = 128
MLA_NOPE = 128
MLA_ROPE = 64
MLA_DV = 128
MLA_Q_RANK = 768
MLA_KV_RANK = 512
MLA_HEAD_PAD = 256
ROPE_BASE = 10000.0
EPS = 1e-5
DEPTH = 1
DEEPNORM_ALPHA = (2.0 * DEPTH) ** 0.25
ADAM_LR = 0.001
ADAM_B1 = 0.9
ADAM_B2 = 0.999
ADAM_EPS = 1e-08
ADAM_WD = 0.01
ADAM_STEP = 10

N_DEV = 8
LANE = 128
SUBLANE = 8
VMEM_LIMIT = 48 * 1024 * 1024
ADAMW_TILE_ELEMS = 128 * 1024

MESH = pl.DeviceIdType.MESH
ANY = pl.BlockSpec(memory_space=pl.ANY)

NN = (((1,), (0,)), ((), ()))
NT = (((1,), (1,)), ((), ()))
TN = (((0,), (0,)), ((), ()))


def _dot(a, b, dims=NN):
    return lax.dot_general(a.astype(BF16), b.astype(BF16), dims, preferred_element_type=F32)


def _params(n_axes):
    return pltpu.CompilerParams(dimension_semantics=("arbitrary",) * n_axes, vmem_limit_bytes=VMEM_LIMIT)


def _tile(n, want):
    if n <= want:
        return n
    t = want
    while t >= LANE:
        if n % t == 0:
            return t
        t -= LANE
    return n


def _my_place():
    return lax.axis_index("x"), lax.axis_index("y"), lax.axis_index("c")


def _flat(px, py, pc):
    return 4 * px + 2 * py + pc


def _block_of(ref, kind, idx, rows, cols):
    if kind == "blk":
        return ref.at[idx]
    if kind == "rows":
        return ref.at[pl.ds(pl.multiple_of(idx * rows, rows), rows), :]
    return ref.at[:, pl.ds(pl.multiple_of(idx * cols, cols), cols)]


class _GatherJob:
    def __init__(self, shards, kinds, mid):
        self.ins = list(shards)
        self.kinds = list(kinds)
        nw = len(shards)
        self.out_shape = []
        for s, kind in zip(shards, kinds):
            r, c = s.shape
            shape = {"blk": (N_DEV, r, c), "rows": (N_DEV * r, c), "cols": (r, N_DEV * c)}[kind]
            self.out_shape.append(jax.ShapeDtypeStruct(shape, s.dtype))
        self.sems = [pltpu.SemaphoreType.DMA((7 * nw,)), pltpu.SemaphoreType.DMA((7 * nw,)), pltpu.SemaphoreType.DMA((nw,))]
        self.phases = [(0.0, self._start), (mid, self._forward), (1.0, self._finish)]

    def _ctx(self, src, dst, send_sems, recv_sems):
        x, y, c = _my_place()
        chips = [(1 - x, y), (x, 1 - y), (1 - x, 1 - y)]

        def place(w, dev):
            r, cc = self.ins[w].shape
            return _block_of(dst[w], self.kinds[w], _flat(*dev), r, cc)

        def copy(w, k, block, to, from_src=False):
            return pltpu.make_async_remote_copy(
                src_ref=src[w] if from_src else place(w, block), dst_ref=place(w, block),
                send_sem=send_sems.at[7 * w + k], recv_sem=recv_sems.at[7 * w + k],
                device_id=to, device_id_type=MESH)

        return (x, y, c), (x, y, 1 - c), chips, place, copy

    def _start(self, src, dst, send_sems, recv_sems, local_sems):
        me, sibling, chips, place, copy = self._ctx(src, dst, send_sems, recv_sems)
        for w in range(len(src)):
            pltpu.make_async_copy(src[w], place(w, me), local_sems.at[w]).start()
        for w in range(len(src)):
            copy(w, 0, me, sibling, from_src=True).start()
            for j, chip in enumerate(chips):
                copy(w, 1 + j, me, (*chip, me[2]), from_src=True).start()

    def _forward(self, src, dst, send_sems, recv_sems, local_sems):
        me, sibling, chips, place, copy = self._ctx(src, dst, send_sems, recv_sems)
        for w in range(len(src)):
            for j, chip in enumerate(chips):
                copy(w, 1 + j, (*chip, me[2]), me).wait_recv()
                copy(w, 4 + j, (*chip, me[2]), sibling).start()

    def _finish(self, src, dst, send_sems, recv_sems, local_sems):
        me, sibling, chips, place, copy = self._ctx(src, dst, send_sems, recv_sems)
        for w in range(len(src)):
            copy(w, 0, sibling, me).wait_recv()
            for j, chip in enumerate(chips):
                copy(w, 4 + j, (*chip, 1 - me[2]), me).wait_recv()
        for w in range(len(src)):
            copy(w, 0, me, sibling, from_src=True).wait_send()
            for j, chip in enumerate(chips):
                copy(w, 1 + j, me, (*chip, me[2]), from_src=True).wait_send()
                copy(w, 4 + j, (*chip, me[2]), sibling).wait_send()
            pltpu.make_async_copy(src[w], place(w, me), local_sems.at[w]).wait()


REL_NEIGHBOURS = [(1, 0, 0), (1, 0, 1), (0, 1, 0), (0, 1, 1)]
REL_REST = [(1, 1, 0), (1, 1, 1), (0, 0, 1)]
REL_ALL = REL_NEIGHBOURS + REL_REST


class _ExchangeJob:
    def __init__(self, grads, kinds, shard_shapes, rels, with_own, by_source=None):
        self.ins = list(grads) + ([by_source] if by_source is not None else [])
        self.kinds, self.shard_shapes, self.rels, self.with_own = list(kinds), list(shard_shapes), list(rels), with_own
        self.nw, self.has_src = len(grads), by_source is not None
        slots = len(rels) + int(with_own)
        self.out_shape = [jax.ShapeDtypeStruct((slots,) + tuple(s), g.dtype) for g, s in zip(grads, shard_shapes)]
        if self.has_src:
            self.out_shape.append(jax.ShapeDtypeStruct((N_DEV,) + by_source.shape, by_source.dtype))
        n = len(rels) * self.nw + len(REL_ALL) * int(self.has_src)
        self.sems = [pltpu.SemaphoreType.DMA((n,)), pltpu.SemaphoreType.DMA((n,)), pltpu.SemaphoreType.DMA((len(self.ins),))]
        self.phases = [(0.0, self._start), (1.0, self._finish)]

    def _copies(self, src, land, send_sems, recv_sems, local_sems):
        x, y, c = _my_place()
        my_idx = _flat(x, y, c)
        nw = self.nw

        def part(w, dev_idx):
            r, cc = self.shard_shapes[w]
            return _block_of(src[w], self.kinds[w], dev_idx, r, cc)

        def remote_copy(s, src_ref, dst_ref, rel):
            peer = (x ^ rel[0], y ^ rel[1], c ^ rel[2])
            return pltpu.make_async_remote_copy(src_ref=src_ref, dst_ref=dst_ref, send_sem=send_sems.at[s],
                                                recv_sem=recv_sems.at[s], device_id=peer, device_id_type=MESH)

        local, remote = [], []
        if self.with_own:
            local += [pltpu.make_async_copy(part(w, my_idx), land[w].at[0], local_sems.at[w]) for w in range(nw)]
        for k, rel in enumerate(self.rels):
            peer_idx = _flat(x ^ rel[0], y ^ rel[1], c ^ rel[2])
            remote += [remote_copy(nw * k + w, part(w, peer_idx), land[w].at[k + int(self.with_own)], rel) for w in range(nw)]
        if self.has_src:
            local.append(pltpu.make_async_copy(src[nw], land[nw].at[my_idx], local_sems.at[nw]))
            remote += [remote_copy(nw * len(self.rels) + k, src[nw], land[nw].at[my_idx], rel) for k, rel in enumerate(REL_ALL)]
        return local, remote

    def _start(self, *refs):
        local, remote = self._copies(*refs)
        for cp in local + remote:
            cp.start()

    def _finish(self, *refs):
        local, remote = self._copies(*refs)
        for cp in remote + local:
            cp.wait()


class _Carried:
    def __init__(self, jobs, grid):
        self.jobs = list(jobs)
        self.steps = int(np.prod(grid))
        self.grid = tuple(grid)
        self.operands = [a for j in self.jobs for a in j.ins]
        self.out_shape = [o for j in self.jobs for o in j.out_shape]
        self.sems = [s for j in self.jobs for s in j.sems]
        self.in_specs = [ANY] * len(self.operands)
        self.out_specs = [ANY] * len(self.out_shape)

    def _step(self):
        lin = pl.program_id(0)
        for a in range(1, len(self.grid)):
            lin = lin * self.grid[a] + pl.program_id(a)
        return lin

    def _run(self, in_refs, out_refs, sem_refs, last):
        lin = self._step()
        for job in self.jobs:
            ins, in_refs = in_refs[:len(job.ins)], in_refs[len(job.ins):]
            outs, out_refs = out_refs[:len(job.out_shape)], out_refs[len(job.out_shape):]
            sems, sem_refs = sem_refs[:len(job.sems)], sem_refs[len(job.sems):]
            for frac, fn in job.phases:
                if (frac >= 1.0) != last:
                    continue
                at = min(self.steps - 1, int(frac * (self.steps - 1)))
                pl.when(lin == at)(functools.partial(fn, ins, outs, *sems))

    def before(self, in_refs, out_refs, sem_refs):
        self._run(in_refs, out_refs, sem_refs, last=False)

    def after(self, in_refs, out_refs, sem_refs):
        self._run(in_refs, out_refs, sem_refs, last=True)


def _standalone(name, job):
    def body(*refs):
        ni, no = len(job.ins), len(job.out_shape)
        for _, fn in job.phases:
            fn(refs[:ni], refs[ni:ni + no], *refs[ni + no:])

    return pl.pallas_call(body, name=name, in_specs=[ANY] * len(job.ins), out_specs=[ANY] * len(job.out_shape),
                          out_shape=job.out_shape, scratch_shapes=job.sems)(*job.ins)


def _split_refs(refs, n_in, n_job_in, n_out, n_job_out, n_scratch):
    cuts = np.cumsum([0, n_in, n_job_in, n_out, n_job_out, n_scratch])
    return [refs[cuts[i]:cuts[i + 1]] for i in range(5)] + [refs[cuts[5]:]]


def _matmul(name, a, b, *, ta=False, tb=False, out_dtypes=(F32,), epilogue=None, extras=(), tm=1024, tn=1024, tk=512, jobs=()):
    if ta:
        K, M = a.shape
    else:
        M, K = a.shape
    N = b.shape[0] if tb else b.shape[1]
    tm, tn, tk = _tile(M, tm), _tile(N, tn), _tile(K, tk)
    nk = K // tk
    grid = (M // tm, N // tn, nk)
    dims = TN if ta else (NT if tb else NN)
    n_ex, n_out = len(extras), len(out_dtypes)
    carried = _Carried(jobs, grid)

    def body(*refs):
        ins, job_ins, out_refs, job_outs, (acc_ref,), job_sems = _split_refs(
            refs, 2 + n_ex, len(carried.operands), n_out, len(carried.out_shape), 1)
        a_ref, b_ref, ex_refs = ins[0], ins[1], ins[2:]
        k = pl.program_id(2)
        carried.before(job_ins, job_outs, job_sems)

        @pl.when(k == 0)
        def _():
            acc_ref[...] = jnp.zeros_like(acc_ref)

        acc_ref[...] += _dot(a_ref[...], b_ref[...], dims)

        @pl.when(k == nk - 1)
        def _():
            acc = acc_ref[...]
            res = (acc,) if epilogue is None else epilogue(acc, *[e[...] for e in ex_refs])
            for o_ref, r in zip(out_refs, res):
                o_ref[...] = r.astype(o_ref.dtype)

        carried.after(job_ins, job_outs, job_sems)

    a_spec = pl.BlockSpec((tk, tm), lambda i, j, k: (k, i)) if ta else pl.BlockSpec((tm, tk), lambda i, j, k: (i, k))
    b_spec = pl.BlockSpec((tn, tk), lambda i, j, k: (j, k)) if tb else pl.BlockSpec((tk, tn), lambda i, j, k: (k, j))
    tile_spec = pl.BlockSpec((tm, tn), lambda i, j, k: (i, j))
    outs = pl.pallas_call(
        body,
        name=name,
        grid=grid,
        in_specs=[a_spec, b_spec] + [tile_spec] * n_ex + carried.in_specs,
        out_specs=[tile_spec] * n_out + carried.out_specs,
        out_shape=[jax.ShapeDtypeStruct((M, N), d) for d in out_dtypes] + carried.out_shape,
        scratch_shapes=[pltpu.VMEM((tm, tn), F32)] + carried.sems,
        compiler_params=_params(3),
    )(a, b, *extras, *carried.operands)
    return outs[0] if len(outs) == 1 else outs


def _rowwise(name, fn, row_in, const_in, row_out, acc_out, tm):
    T = row_in[0][0].shape[0]
    n_r, n_c, n_o, n_a = len(row_in), len(const_in), len(row_out), len(acc_out)

    def body(*refs):
        ins = [r[...] for r in refs[:n_r + n_c]]
        out_refs = refs[n_r + n_c:n_r + n_c + n_o]
        acc_refs = refs[n_r + n_c + n_o:]
        res = fn(*ins)
        for o_ref, r in zip(out_refs, res[:n_o]):
            o_ref[...] = r.astype(o_ref.dtype)

        @pl.when(pl.program_id(0) == 0)
        def _():
            for a_ref in acc_refs:
                a_ref[...] = jnp.zeros_like(a_ref)

        for a_ref, r in zip(acc_refs, res[n_o:]):
            a_ref[...] += jnp.sum(r.reshape(tm // SUBLANE, SUBLANE, r.shape[-1]), axis=0)

    in_specs = [pl.BlockSpec((tm, w), functools.partial(lambda i, c: (i, c), c=cb)) for (_, w, cb) in row_in]
    in_specs += [pl.BlockSpec(c.shape, functools.partial(lambda i, nd: (0,) * nd, nd=c.ndim)) for c in const_in]
    out_specs = [pl.BlockSpec((tm, c), lambda i: (i, 0)) for (c, _) in row_out]
    out_specs += [pl.BlockSpec((SUBLANE, c), lambda i: (0, 0)) for c in acc_out]
    out_shape = [jax.ShapeDtypeStruct((T, c), d) for (c, d) in row_out]
    out_shape += [jax.ShapeDtypeStruct((SUBLANE, c), F32) for c in acc_out]
    return pl.pallas_call(
        body, name=name, grid=(T // tm,), in_specs=in_specs, out_specs=out_specs, out_shape=out_shape,
        compiler_params=_params(1),
    )(*[r[0] for r in row_in], *const_in)


def _ln_stats(z):
    mu = jnp.mean(z, axis=-1, keepdims=True)
    var = jnp.mean(jnp.square(z - mu), axis=-1, keepdims=True)
    rstd = lax.rsqrt(var + EPS)
    return (z - mu) * rstd, rstd


def _ln_bwd(dy, xhat, rstd, g):
    dxh = dy * g
    return rstd * (dxh - jnp.mean(dxh, axis=-1, keepdims=True) - xhat * jnp.mean(dxh * xhat, axis=-1, keepdims=True))


def _rms(x, g):
    return x * lax.rsqrt(jnp.mean(jnp.square(x), axis=-1, keepdims=True) + EPS) * g


def _rope128(x, cos2, sin2):
    return x * cos2 + pltpu.roll(x, RET_DK // 2, 1) * sin2


def _unrope128(dy, cos2, sin2):
    return dy * cos2 + pltpu.roll(dy * sin2, RET_DK // 2, 1)


def _rope64(x, cosp, sina, sinb):
    h = MLA_ROPE // 2
    return x * cosp + pltpu.roll(x, h, 1) * sina + pltpu.roll(x, LANE - h, 1) * sinb


def _unrope64(dy, cosp, sina, sinb):
    h = MLA_ROPE // 2
    return dy * cosp + pltpu.roll(dy * sina, LANE - h, 1) + pltpu.roll(dy * sinb, h, 1)


RET_BLOCK = 256


def _ret_tables(H, B):
    log_g = np.log1p(-np.exp2(-5.0 - np.arange(H, dtype=np.float32))).astype(np.float32)
    idx = np.arange(B, dtype=np.float32)
    dist = np.abs(idx[:, None] - idx[None, :])
    visible = (np.arange(B)[None, :] // CHUNK) <= (np.arange(B)[:, None] // CHUNK)
    dmat = np.exp(log_g[:, None, None] * dist).astype(np.float32) * visible[None].astype(np.float32)
    qd = np.exp(log_g[:, None] * (idx + 1.0)).astype(np.float32)
    kd = np.exp(log_g[:, None] * (B - 1 - idx)).astype(np.float32)
    sd = np.exp(log_g * B).astype(np.float32)
    bc = lambda v: np.ascontiguousarray(np.broadcast_to(v[:, :, None], (H, v.shape[1], LANE)))
    sdb = np.ascontiguousarray(np.broadcast_to(sd[:, None, None], (H, SUBLANE, LANE)))
    return jnp.asarray(dmat), jnp.asarray(bc(qd)), jnp.asarray(bc(kd)), jnp.asarray(sdb)


def _ret_specs(H, B, reverse, nb):
    blk = (lambda b: nb - 1 - b) if reverse else (lambda b: b)
    col = lambda off: pl.BlockSpec((B, LANE), functools.partial(lambda h, b, off: (blk(b), off + h), off=off))
    row = pl.BlockSpec((B, LANE), lambda h, b: (blk(b), 0))
    per_head = lambda r: pl.BlockSpec((1, r, LANE), lambda h, b: (h, 0, 0))
    dmat = pl.BlockSpec((1, B, B), lambda h, b: (h, 0, 0))
    return col, row, per_head, dmat


def _retention_fwd(p1, cos2, sin2, gn_g, H):
    T = p1.shape[0]
    B = min(RET_BLOCK, T)
    nb = T // B
    scale = RET_DK ** -0.5
    dmat, qd, kd, sd = _ret_tables(H, B)
    col, row, per_head, dspec = _ret_specs(H, B, False, nb)

    def body(q_ref, k_ref, v_ref, g_ref, cos_ref, sin_ref, d_ref, qd_ref, kd_ref, sd_ref, gn_ref, pre_ref, out_ref, s_ref):
        @pl.when(pl.program_id(1) == 0)
        def _():
            s_ref[...] = jnp.zeros_like(s_ref)

        cos, sin = cos_ref[...], sin_ref[...]
        q = _rope128(q_ref[...], cos, sin)
        k = _rope128(k_ref[...], cos, sin) * scale
        v = v_ref[...]
        s = s_ref[...]
        a = _dot(q, k, NT) * d_ref[0]
        ret = _dot(a, v) + _dot(q * qd_ref[0], s)
        s_ref[...] = sd_ref[0, 0:1, :] * s + _dot(k * kd_ref[0], v, TN)
        pre_ref[...] = ret
        nrm, _ = _ln_stats(ret)
        out_ref[...] = (jax.nn.silu(g_ref[...]) * (nrm * gn_ref[...])).astype(out_ref.dtype)

    out_spec = pl.BlockSpec((B, LANE), lambda h, b: (b, h))
    return pl.pallas_call(
        body, name="retention_fwd", grid=(H, nb),
        in_specs=[col(0), col(H), col(2 * H), col(3 * H), row, row, dspec, per_head(B), per_head(B), per_head(SUBLANE),
                  pl.BlockSpec((1, LANE), lambda h, b: (0, h))],
        out_specs=[out_spec, out_spec],
        out_shape=[jax.ShapeDtypeStruct((T, H * RET_DV), F32), jax.ShapeDtypeStruct((T, H * RET_DV), BF16)],
        scratch_shapes=[pltpu.VMEM((RET_DK, RET_DV), F32)],
        compiler_params=_params(2),
    )(p1, p1, p1, p1, cos2, sin2, dmat, qd, kd, sd, gn_g)


def _retention_bwd(p1, dpre, cos2, sin2, H):
    T = p1.shape[0]
    B = min(RET_BLOCK, T)
    nb = T // B
    scale = RET_DK ** -0.5
    dmat, qd, kd, sd = _ret_tables(H, B)

    def dq_body(q_ref, k_ref, v_ref, do_ref, cos_ref, sin_ref, d_ref, qd_ref, kd_ref, sd_ref, dq_ref, s_ref):
        @pl.when(pl.program_id(1) == 0)
        def _():
            s_ref[...] = jnp.zeros_like(s_ref)

        cos, sin = cos_ref[...], sin_ref[...]
        k = _rope128(k_ref[...], cos, sin) * scale
        v, do, s = v_ref[...], do_ref[...], s_ref[...]
        da = _dot(do, v, NT) * d_ref[0]
        dq = _dot(da, k) + _dot(do, s, NT) * qd_ref[0]
        dq_ref[...] = _unrope128(dq, cos, sin).astype(dq_ref.dtype)
        s_ref[...] = sd_ref[0, 0:1, :] * s + _dot(k * kd_ref[0], v, TN)

    col, row, per_head, dspec = _ret_specs(H, B, False, nb)
    dq = pl.pallas_call(
        dq_body, name="retention_bwd_dq", grid=(H, nb),
        in_specs=[col(0), col(H), col(2 * H), col(0), row, row, dspec, per_head(B), per_head(B), per_head(SUBLANE)],
        out_specs=pl.BlockSpec((B, LANE), lambda h, b: (b, h)),
        out_shape=jax.ShapeDtypeStruct((T, H * RET_DK), BF16),
        scratch_shapes=[pltpu.VMEM((RET_DK, RET_DV), F32)],
        compiler_params=_params(2),
    )(p1, p1, p1, dpre, cos2, sin2, dmat, qd, kd, sd)

    def dkv_body(q_ref, k_ref, v_ref, do_ref, cos_ref, sin_ref, d_ref, qd_ref, kd_ref, sd_ref, dk_ref, dv_ref, g_ref):
        @pl.when(pl.program_id(1) == 0)
        def _():
            g_ref[...] = jnp.zeros_like(g_ref)

        cos, sin = cos_ref[...], sin_ref[...]
        q = _rope128(q_ref[...], cos, sin)
        k = _rope128(k_ref[...], cos, sin) * scale
        v, do, g = v_ref[...], do_ref[...], g_ref[...]
        a = _dot(q, k, NT) * d_ref[0]
        da = _dot(do, v, NT) * d_ref[0]
        dv = _dot(a, do, TN) + _dot(k * kd_ref[0], g)
        dk = _dot(da, q, TN) + _dot(v, g, NT) * kd_ref[0]
        dv_ref[...] = dv.astype(dv_ref.dtype)
        dk_ref[...] = _unrope128(dk * scale, cos, sin).astype(dk_ref.dtype)
        g_ref[...] = sd_ref[0, 0:1, :] * g + _dot(q * qd_ref[0], do, TN)

    col, row, per_head, dspec = _ret_specs(H, B, True, nb)
    out_spec = pl.BlockSpec((B, LANE), lambda h, b: (nb - 1 - b, h))
    dk, dv = pl.pallas_call(
        dkv_body, name="retention_bwd_dkv", grid=(H, nb),
        in_specs=[col(0), col(H), col(2 * H), col(0), row, row, dspec, per_head(B), per_head(B), per_head(SUBLANE)],
        out_specs=[out_spec, out_spec],
        out_shape=[jax.ShapeDtypeStruct((T, H * RET_DK), BF16), jax.ShapeDtypeStruct((T, H * RET_DV), BF16)],
        scratch_shapes=[pltpu.VMEM((RET_DK, RET_DV), F32)],
        compiler_params=_params(2),
    )(p1, p1, p1, dpre, cos2, sin2, dmat, qd, kd, sd)
    return dq, dk, dv


ATT_BLOCK = 512


ATT_ROW_CHUNKS = 2
ATT_HEADS_PER_STEP = 8
ATT_BWD_HEADS_PER_STEP = 2
LOG2E = 1.4426950408889634


def _chunk_mask(rows, row0, cols):
    qi = (lax.broadcasted_iota(jnp.int32, (rows, cols), 0) + row0) // CHUNK
    kj = lax.broadcasted_iota(jnp.int32, (rows, cols), 1) // CHUNK
    return kj <= qi


def _attention_fwd(qh, kh, vx, H, jobs=()):
    T = qh.shape[0]
    t = min(ATT_BLOCK, T)
    n = T // t
    r = t // ATT_ROW_CHUNKS
    G = min(ATT_HEADS_PER_STEP, H)
    P = MLA_HEAD_PAD
    scale = (MLA_NOPE + MLA_ROPE) ** -0.5
    c2 = scale * LOG2E
    carried = _Carried(jobs, (H // G, n, n))

    def body(*refs):
        (q_ref, k_ref, v_ref), job_ins, (o_ref, ob_ref, lse_ref), job_outs, (m_ref, acc_ref), job_sems = _split_refs(
            refs, 3, len(carried.operands), 3, len(carried.out_shape), 2)
        i, j = pl.program_id(1), pl.program_id(2)
        carried.before(job_ins, job_outs, job_sems)

        @pl.when(j == 0)
        def _():
            m_ref[...] = jnp.full_like(m_ref, -jnp.inf)
            acc_ref[...] = jnp.zeros_like(acc_ref)

        def update(masked):
            for g in range(G):
                hc = pl.ds(g * P, P)
                for ch in range(ATT_ROW_CHUNKS):
                    rs = pl.ds(ch * r, r)
                    s = _dot(q_ref[rs, hc], k_ref[:, hc], NT)
                    if masked:
                        s = jnp.where(_chunk_mask(r, ch * r, t), s, -jnp.inf)
                    m_prev = m_ref[rs, pl.ds(g * LANE, LANE)]
                    m_new = jnp.maximum(m_prev, jnp.max(s, axis=-1, keepdims=True))
                    alpha = jnp.exp2((m_prev - m_new) * c2)
                    p = jnp.exp2((s - jnp.tile(m_new, (1, t // LANE))) * c2)
                    acc_ref[rs, hc] = jnp.tile(alpha, (1, 2)) * acc_ref[rs, hc] + _dot(p, v_ref[:, hc])
                    m_ref[rs, pl.ds(g * LANE, LANE)] = m_new

        pl.when(j < i)(functools.partial(update, False))
        pl.when(j == i)(functools.partial(update, True))

        @pl.when(j == n - 1)
        def _():
            for g in range(G):
                acc = acc_ref[:, pl.ds(g * P, P)]
                l = acc[:, MLA_DV:]
                o = acc[:, :MLA_DV] / l
                o_ref[:, pl.ds(g * MLA_DV, MLA_DV)] = o
                ob_ref[:, pl.ds(g * MLA_DV, MLA_DV)] = o.astype(ob_ref.dtype)
                lse_ref[g] = m_ref[:, pl.ds(g * LANE, LANE)] * scale + jnp.log(l)

        carried.after(job_ins, job_outs, job_sems)

    kclamp = lambda i, j: jnp.minimum(j, i)
    o_spec = pl.BlockSpec((t, G * MLA_DV), lambda h, i, j: (i, h))
    return pl.pallas_call(
        body, name="attention_fwd", grid=(H // G, n, n),
        in_specs=[pl.BlockSpec((t, G * P), lambda h, i, j: (i, h)),
                  pl.BlockSpec((t, G * P), lambda h, i, j: (kclamp(i, j), h)),
                  pl.BlockSpec((t, G * P), lambda h, i, j: (kclamp(i, j), h))] + carried.in_specs,
        out_specs=[o_spec, o_spec, pl.BlockSpec((G, t, LANE), lambda h, i, j: (h, i, 0))] + carried.out_specs,
        out_shape=[jax.ShapeDtypeStruct((T, H * MLA_DV), F32), jax.ShapeDtypeStruct((T, H * MLA_DV), BF16),
                   jax.ShapeDtypeStruct((H, T, LANE), F32)] + carried.out_shape,
        scratch_shapes=[pltpu.VMEM((t, G * LANE), F32), pltpu.VMEM((t, G * P), F32)] + carried.sems,
        compiler_params=_params(3),
    )(qh, kh, vx, *carried.operands)


def _attention_bwd(qh, kh, vx, o, lse, dmix, H, jobs=()):
    T = qh.shape[0]
    t = min(ATT_BLOCK, T)
    n = T // t
    r = t // ATT_ROW_CHUNKS
    G = min(ATT_BWD_HEADS_PER_STEP, H)
    P = MLA_HEAD_PAD
    scale = (MLA_NOPE + MLA_ROPE) ** -0.5
    c2 = scale * LOG2E
    carried = _Carried(jobs, (H // G, n, n))

    def body(*refs):
        ((q_ref, k_ref, v_ref, do_ref, o_ref, lse_ref), job_ins, (dq_ref, dkn_ref, dkr_ref, dv_ref), job_outs,
         (dk_acc, dv_acc), job_sems) = _split_refs(refs, 6, len(carried.operands), 4, len(carried.out_shape), 2)
        j, i = pl.program_id(1), pl.program_id(2)
        carried.before(job_ins, job_outs, job_sems)

        @pl.when(jnp.logical_and(j == 0, i == 0))
        def _():
            dq_ref[...] = jnp.zeros_like(dq_ref)

        @pl.when(i == j)
        def _():
            dk_acc[...] = jnp.zeros_like(dk_acc)
            dv_acc[...] = jnp.zeros_like(dv_acc)

        def update(masked):
            for g in range(G):
                hc, hv = pl.ds(g * P, P), pl.ds(g * MLA_DV, MLA_DV)
                k, v = k_ref[:, hc], v_ref[:, pl.ds(g * P, MLA_DV)]
                for ch in range(ATT_ROW_CHUNKS):
                    rs = pl.ds(ch * r, r)
                    q, do = q_ref[rs, hc], do_ref[rs, hv]
                    s = _dot(q, k, NT)
                    if masked:
                        s = jnp.where(_chunk_mask(r, ch * r, t), s, -jnp.inf)
                    p = jnp.exp2(s * c2 - jnp.tile(lse_ref[g, rs, :] * LOG2E, (1, t // LANE)))
                    dv_acc[:, hv] += _dot(p, do, TN)
                    dp = _dot(do, v, NT)
                    delta = jnp.sum(do * o_ref[rs, hv], axis=-1, keepdims=True)
                    ds = p * (dp - delta)
                    rows = pl.ds(pl.multiple_of(i * t, t) + ch * r, r)
                    dq_ref[rows, hc] += _dot(ds, k)
                    dk_acc[:, hc] += _dot(ds, q, TN)

        pl.when(i > j)(functools.partial(update, False))
        pl.when(i == j)(functools.partial(update, True))

        @pl.when(i == n - 1)
        def _():
            for g in range(G):
                dk = dk_acc[:, pl.ds(g * P, P)] * scale
                dkn_ref[:, pl.ds(g * LANE, LANE)] = dk[:, :MLA_NOPE].astype(dkn_ref.dtype)
                dkr_ref[:, pl.ds(g * LANE, LANE)] = dk[:, MLA_NOPE:]
            dv_ref[...] = dv_acc[...].astype(dv_ref.dtype)

        carried.after(job_ins, job_outs, job_sems)

    qclamp = lambda j, i: jnp.maximum(i, j)
    kv_out = pl.BlockSpec((t, G * LANE), lambda h, j, i: (j, h))
    q_in = pl.BlockSpec((t, G * MLA_DV), lambda h, j, i: (qclamp(j, i), h))
    return pl.pallas_call(
        body, name="attention_bwd", grid=(H // G, n, n),
        in_specs=[pl.BlockSpec((t, G * P), lambda h, j, i: (qclamp(j, i), h)),
                  pl.BlockSpec((t, G * P), lambda h, j, i: (j, h)),
                  pl.BlockSpec((t, G * P), lambda h, j, i: (j, h)),
                  pl.BlockSpec((t, G * MLA_DV), lambda h, j, i: (qclamp(j, i), H // G + h)),
                  q_in,
                  pl.BlockSpec((G, t, LANE), lambda h, j, i: (h, qclamp(j, i), 0))] + carried.in_specs,
        out_specs=[pl.BlockSpec((T, G * P), lambda h, j, i: (0, h)), kv_out, kv_out, kv_out] + carried.out_specs,
        out_shape=[jax.ShapeDtypeStruct((T, H * P), F32), jax.ShapeDtypeStruct((T, H * LANE), BF16),
                   jax.ShapeDtypeStruct((T, H * LANE), F32), jax.ShapeDtypeStruct((T, H * LANE), BF16)] + carried.out_shape,
        scratch_shapes=[pltpu.VMEM((t, G * P), F32), pltpu.VMEM((t, G * MLA_DV), F32)] + carried.sems,
        compiler_params=_params(3),
    )(qh, kh, vx, dmix, o, lse, *carried.operands)


def _adamw_math(w, g, m, v):
    m = ADAM_B1 * m + (1.0 - ADAM_B1) * g
    v = ADAM_B2 * v + (1.0 - ADAM_B2) * jnp.square(g)
    m_hat = m / (1.0 - ADAM_B1 ** ADAM_STEP)
    v_hat = v / (1.0 - ADAM_B2 ** ADAM_STEP)
    delta = -ADAM_LR * (m_hat / (jnp.sqrt(v_hat) + ADAM_EPS) + ADAM_WD * w)
    return delta, m, v


def _adamw(name, landed, w, m, v):
    R, C = w.shape
    tr = 16
    while tr * 2 * C <= ADAMW_TILE_ELEMS and R % (tr * 2) == 0:
        tr *= 2
    tr = tr if R % tr == 0 else R
    nl = len(landed)
    assert sum(l.shape[0] for l in landed) == N_DEV

    def body(*refs):
        l_refs, (w_ref, m_ref, v_ref, g_out, d_out, m_out, v_out) = refs[:nl], refs[nl:]
        g = None
        for l_ref in l_refs:
            for s in range(l_ref.shape[0]):
                part = l_ref[s].astype(F32)
                g = part if g is None else g + part
        delta, mn, vn = _adamw_math(w_ref[...], g, m_ref[...], v_ref[...])
        g_out[...] = g
        d_out[...] = delta
        m_out[...] = mn
        v_out[...] = vn

    spec = pl.BlockSpec((tr, C), lambda i: (i, 0))
    l_specs = [pl.BlockSpec((l.shape[0], tr, C), lambda i: (0, i, 0)) for l in landed]
    return pl.pallas_call(
        body, name=name, grid=(R // tr,), in_specs=l_specs + [spec, spec, spec], out_specs=[spec] * 4,
        out_shape=[jax.ShapeDtypeStruct((R, C), F32)] * 4, compiler_params=_params(1),
    )(*landed, w, m, v)


def _adamw_small(landed, w, m, v):
    P = w.shape[1]

    def body(l_ref, w_ref, m_ref, v_ref, g_out, d_out, m_out, v_out):
        acc = l_ref[0]
        for s in range(1, N_DEV):
            acc = acc + l_ref[s]
        g = jnp.sum(acc, axis=0, keepdims=True)
        delta, mn, vn = _adamw_math(w_ref[...], g, m_ref[...], v_ref[...])
        g_out[...] = g
        d_out[...] = delta
        m_out[...] = mn
        v_out[...] = vn

    return pl.pallas_call(
        body, name="adamw_replicated", out_shape=[jax.ShapeDtypeStruct((1, P), F32)] * 4,
    )(landed, w, m, v)


def _rope_tables(pos, T):
    p = pos.astype(F32)[:, None]
    inv_r = ROPE_BASE ** (-jnp.arange(0, RET_DK, 2, dtype=F32) / RET_DK)
    ang = p * inv_r
    cos, sin = jnp.cos(ang), jnp.sin(ang)
    cos2 = jnp.concatenate([cos, cos], axis=-1)
    sin2 = jnp.concatenate([-sin, sin], axis=-1)
    inv_m = ROPE_BASE ** (-jnp.arange(0, MLA_ROPE, 2, dtype=F32) / MLA_ROPE)
    ang = p * inv_m
    cos, sin = jnp.cos(ang), jnp.sin(ang)
    zh = jnp.zeros_like(cos)
    z2 = jnp.zeros((T, LANE - MLA_ROPE), F32)
    cosp = jnp.concatenate([cos, cos, z2], axis=-1)
    sina = jnp.concatenate([zh, sin, z2], axis=-1)
    sinb = jnp.concatenate([-sin, zh, z2], axis=-1)
    return cos2, sin2, cosp, sina, sinb


def kernel(x, positions, w_in, q_norm_g, w_uq, kv_norm_g, w_uk, w_uv, ret_gn_g, w_out, ln1_g, ln1_b, w_up, w_down, ln2_g, ln2_b, loss_target, m_w_in, m_q_norm_g, m_w_uq, m_kv_norm_g, m_w_uk, m_w_uv, m_ret_gn_g, m_w_out, m_ln1_g, m_ln1_b, m_w_up, m_w_down, m_ln2_g, m_ln2_b, v_w_in, v_q_norm_g, v_w_uq, v_kv_norm_g, v_w_uk, v_w_uv, v_ret_gn_g, v_w_out, v_ln1_g, v_ln1_b, v_w_up, v_w_down, v_ln2_g, v_ln2_b):
    T, D = x.shape[1], x.shape[2]
    H = D // 256
    RW = H * RET_DV
    x = x[0]
    tgt = loss_target[0]
    alpha = DEEPNORM_ALPHA
    in_width = w_in.shape[2] * N_DEV
    mla_in = in_width - 4 * RW
    mla_in_pad = -(-mla_in // 512) * 512
    f_uq = MLA_NOPE + MLA_ROPE

    names = ["w_in", "w_uq", "w_uk", "w_uv", "w_out", "w_up", "w_down"]
    big = dict(zip(names, [w_in[0], w_uq[0], w_uk[0], w_uv[0], w_out[0], w_up[0], w_down[0]]))
    kind = dict(zip(names, ["blk", "blk", "cols", "cols", "rows", "cols", "rows"]))
    half = {n: w.astype(BF16) for n, w in big.items()}

    def gather(ns, mid):
        return _GatherJob([half[n] for n in ns], [kind[n] for n in ns], mid)

    (g_in,) = _standalone("gather_w_in", gather(["w_in"], 0.0))
    win = jnp.transpose(g_in, (1, 0, 2)).reshape(D, in_width)
    w1 = win[:, :4 * RW]
    w2 = jnp.pad(win[:, 4 * RW:], ((0, 0), (0, mla_in_pad - mla_in)))

    cos2, sin2, cosp, sina, sinb = _rope_tables(positions[0], T)
    xb = x.astype(BF16)
    tr = min(256, T)

    p1, g_uq, wuk, wuv, wout = _matmul("proj_ret", xb, w1,
                                       jobs=[gather(["w_uq", "w_uk", "w_uv", "w_out"], 0.65)])
    wuq = jnp.transpose(g_uq, (1, 0, 2)).reshape(MLA_Q_RANK, H, f_uq)
    wuq = jnp.pad(wuq, ((0, 0), (0, 0), (0, MLA_HEAD_PAD - f_uq))).reshape(MLA_Q_RANK, H * MLA_HEAD_PAD)
    wukv = jnp.concatenate([wuk, wuv], axis=1)
    p2 = _matmul("proj_mla", xb, w2)
    c_kv, c_kr = MLA_Q_RANK, MLA_Q_RANK + MLA_KV_RANK

    def mla_prep(p, cp, sa, sb, gq, gkv):
        cqn = _rms(p[:, :c_kv], gq)
        ckvn = _rms(p[:, c_kv:c_kr], gkv)
        return cqn, ckvn, _rope64(p[:, c_kr:c_kr + LANE], cp, sa, sb)

    cqn, ckvn, krope = _rowwise(
        "mla_prep", mla_prep, [(p2, mla_in_pad, 0), (cosp, LANE, 0), (sina, LANE, 0), (sinb, LANE, 0)],
        [q_norm_g, kv_norm_g], [(MLA_Q_RANK, BF16), (MLA_KV_RANK, BF16), (LANE, BF16)], [], tr)
    qf = _matmul("mla_q", cqn, wuq)
    kv = _matmul("mla_kv", ckvn, wukv, out_dtypes=(BF16,))

    def mla_heads(q, kn, vv, kr, cp, sa, sb):
        qs, ks, vs = [], [], []
        ones = jnp.ones((q.shape[0], MLA_DV), vv.dtype)
        for h in range(H):
            o = h * MLA_HEAD_PAD
            qs += [q[:, o:o + MLA_NOPE], _rope64(q[:, o + MLA_NOPE:o + MLA_HEAD_PAD], cp, sa, sb)]
            ks += [kn[:, h * MLA_NOPE:(h + 1) * MLA_NOPE], kr]
            vs += [vv[:, h * MLA_DV:(h + 1) * MLA_DV], ones]
        return jnp.concatenate(qs, axis=1), jnp.concatenate(ks, axis=1), jnp.concatenate(vs, axis=1)

    qh, kh, vx = _rowwise(
        "mla_heads", mla_heads,
        [(qf, H * MLA_HEAD_PAD, 0), (kv, RW, 0), (kv, RW, 1), (krope, LANE, 0), (cosp, LANE, 0), (sina, LANE, 0), (sinb, LANE, 0)],
        [], [(H * MLA_HEAD_PAD, BF16)] * 3, [], tr)
    att_o, att_ob, lse, wup = _attention_fwd(qh, kh, vx, H, jobs=[gather(["w_up"], 0.5)])
    ret_pre, ret_ob = _retention_fwd(p1, cos2, sin2, ret_gn_g, H)
    mixin = jnp.concatenate([ret_ob, att_ob], axis=1)
    mix = _matmul("mix_out", mixin, wout)

    def ln1_fwd(xt, mt, g, b):
        z = alpha * xt + mt
        xhat, _ = _ln_stats(z)
        y = xhat * g + b
        return z, y, y

    z1, x1, x1b = _rowwise("ln1", ln1_fwd, [(x, D, 0), (mix, D, 0)], [ln1_g, ln1_b],
                           [(D, F32), (D, F32), (D, BF16)], [], tr)

    def relu2(acc):
        r = jnp.maximum(acc, 0.0)
        return r * r, r

    a2b, rb, wdown = _matmul("mlp_up", x1b, wup, out_dtypes=(BF16, BF16), epilogue=relu2, jobs=[gather(["w_down"], 0.8)])
    hmlp = _matmul("mlp_down", a2b, wdown)

    def ln2_loss(x1t, ht, tt, g, b):
        xhat, rstd = _ln_stats(alpha * x1t + ht)
        err = xhat * g + b - tt
        dy = err / D
        dz = _ln_bwd(dy, xhat, rstd, g)
        lrow = 0.5 * jnp.mean(jnp.square(err), axis=-1, keepdims=True)
        return dz, dz, dy * xhat, dy, jnp.broadcast_to(lrow, (lrow.shape[0], LANE))

    dz2, dz2b, dg2, db2, lpart = _rowwise("ln2_loss", ln2_loss, [(x1, D, 0), (hmlp, D, 0), (tgt, D, 0)], [ln2_g, ln2_b],
                                          [(D, F32), (D, BF16)], [D, D, LANE], tr)
    loss = lax.psum(jnp.sum(lpart[:, 0]), ("x", "y", "c"))

    def exchange(grads, rels, with_own, by_source=None):
        ns = list(grads)
        return _ExchangeJob([grads[n] for n in ns], [kind[n] for n in ns], [big[n].shape for n in ns], rels, with_own, by_source)

    landed = {}
    da = _matmul("mlp_down_dx", dz2b, wdown, tb=True, out_dtypes=(BF16,), extras=(rb,),
                 epilogue=lambda acc, r: (acc * (2.0 * r.astype(F32)),))
    g_wdown = _matmul("mlp_down_dw", a2b, dz2b, ta=True, out_dtypes=(BF16,))
    g_wup, l_a = _matmul("mlp_up_dw", x1b, da, ta=True, out_dtypes=(BF16,),
                         jobs=[exchange({"w_down": g_wdown}, REL_NEIGHBOURS, True)])
    dx1m, l_b = _matmul("mlp_up_dx", da, wup, tb=True, jobs=[exchange({"w_down": g_wdown}, REL_REST, False)])
    landed["w_down"] = [l_a, l_b]

    def ln1_bwd(dm, dzz, z, g):
        xhat, rstd = _ln_stats(z)
        dy = dm + alpha * dzz
        dz = _ln_bwd(dy, xhat, rstd, g)
        return dz, dz, dy * xhat, dy

    dz1, dz1b, dg1, db1 = _rowwise("ln1_bwd", ln1_bwd, [(dx1m, D, 0), (dz2, D, 0), (z1, D, 0)], [ln1_g],
                                   [(D, F32), (D, BF16)], [D, D], tr)
    g_wout = _matmul("mix_out_dw", mixin, dz1b, ta=True, out_dtypes=(BF16,))
    dmix = _matmul("mix_out_dx", dz1b, wout, tb=True)

    dqh, dkn, dkr, dvv, l_a = _attention_bwd(qh, kh, vx, att_o, lse, dmix, H, jobs=[exchange({"w_up": g_wup}, REL_ALL, True)])
    landed["w_up"] = [l_a]
    att_scale = (MLA_NOPE + MLA_ROPE) ** -0.5

    def mla_heads_bwd(dq, dkrh, cp, sa, sb):
        parts, acc = [], dkrh[:, :LANE]
        dq = dq * att_scale
        for h in range(H):
            o = h * MLA_HEAD_PAD
            parts += [dq[:, o:o + MLA_NOPE], _unrope64(dq[:, o + MLA_NOPE:o + MLA_HEAD_PAD], cp, sa, sb)]
            if h:
                acc = acc + dkrh[:, h * LANE:(h + 1) * LANE]
        return jnp.concatenate(parts, axis=1), _unrope64(acc, cp, sa, sb)

    dqb, dkr128 = _rowwise(
        "mla_heads_bwd", mla_heads_bwd,
        [(dqh, H * MLA_HEAD_PAD, 0), (dkr, H * LANE, 0), (cosp, LANE, 0), (sina, LANE, 0), (sinb, LANE, 0)],
        [], [(H * MLA_HEAD_PAD, BF16), (LANE, F32)], [], tr)
    dcqn = _matmul("mla_q_dx", dqb, wuq, tb=True)
    g_wuq = _matmul("mla_q_dw", cqn, dqb, ta=True, out_dtypes=(BF16,))
    dkvb = jnp.concatenate([dkn, dvv], axis=1)
    dckvn = _matmul("mla_kv_dx", dkvb, wukv, tb=True)
    g_wukv = _matmul("mla_kv_dw", ckvn, dkvb, ta=True, out_dtypes=(BF16,))

    def rms_bwd(c, dy, g):
        rstd = lax.rsqrt(jnp.mean(jnp.square(c), axis=-1, keepdims=True) + EPS)
        dyg = dy * g
        dc = rstd * (dyg - c * (rstd * rstd) * jnp.mean(dyg * c, axis=-1, keepdims=True))
        return dc, dy * c * rstd

    def mla_prep_bwd(p, dq_, dkv_, dkr_, gq, gkv):
        dcq, dgq_ = rms_bwd(p[:, :c_kv], dq_, gq)
        dckv, dgkv_ = rms_bwd(p[:, c_kv:c_kr], dkv_, gkv)
        pad = jnp.zeros((p.shape[0], mla_in_pad - c_kr - LANE), F32)
        return jnp.concatenate([dcq, dckv, dkr_, pad], axis=1), dgq_, dgkv_

    d_p2, dgq, dgkv = _rowwise(
        "mla_prep_bwd", mla_prep_bwd, [(p2, mla_in_pad, 0), (dcqn, MLA_Q_RANK, 0), (dckvn, MLA_KV_RANK, 0), (dkr128, LANE, 0)],
        [q_norm_g, kv_norm_g], [(mla_in_pad, BF16)], [MLA_Q_RANK, MLA_KV_RANK], tr)

    def gate_bwd(pre, gate, dout, gn):
        dpre, dgate, dgn = [], [], []
        for h in range(H):
            sl = slice(h * RET_DV, (h + 1) * RET_DV)
            nrm, rstd = _ln_stats(pre[:, sl])
            gate_h, do_h, gn_h = gate[:, sl], dout[:, sl], gn[:, sl]
            sg = jax.nn.sigmoid(gate_h)
            silu = gate_h * sg
            dgate.append(do_h * (nrm * gn_h) * (sg * (1.0 + gate_h * (1.0 - sg))))
            dn = do_h * silu
            dgn.append(dn * nrm)
            dpre.append(_ln_bwd(dn, nrm, rstd, gn_h))
        return jnp.concatenate(dpre, axis=1), jnp.concatenate(dgate, axis=1), jnp.concatenate(dgn, axis=1)

    dpre, drg, dgn = _rowwise("ret_gate_bwd", gate_bwd, [(ret_pre, RW, 0), (p1, RW, 3), (dmix, RW, 0)], [ret_gn_g],
                              [(RW, F32), (RW, BF16)], [RW], tr)
    drq, drk, drv = _retention_bwd(p1, dpre, cos2, sin2, H)
    d_p1 = jnp.concatenate([drq, drk, drv, drg], axis=1)

    g_uq = g_wuq.reshape(MLA_Q_RANK, H, MLA_HEAD_PAD)[:, :, :f_uq].reshape(MLA_Q_RANK, N_DEV, H * f_uq // N_DEV)
    mid_grads = {"w_uq": jnp.transpose(g_uq, (1, 0, 2)), "w_uk": g_wukv[:, :RW], "w_uv": g_wukv[:, RW:], "w_out": g_wout}
    g_w1, *l_a = _matmul("proj_ret_dw", xb, d_p1, ta=True, out_dtypes=(BF16,), jobs=[exchange(mid_grads, REL_NEIGHBOURS, True)])
    g_w2, *l_b = _matmul("proj_mla_dw", xb, d_p2, ta=True, out_dtypes=(BF16,), jobs=[exchange(mid_grads, REL_REST, False)])
    for n, a_, b_ in zip(mid_grads, l_a, l_b):
        landed[n] = [a_, b_]

    g_win = jnp.concatenate([g_w1, g_w2[:, :mla_in]], axis=1)
    g_win = {"w_in": jnp.transpose(g_win.reshape(D, N_DEV, in_width // N_DEV), (1, 0, 2))}
    small = jnp.concatenate([dgq, dgkv, dgn, dg1, db1, dg2, db2], axis=1)
    dx_a, l_a = _matmul("proj_mla_dx", d_p2, w2, tb=True, extras=(dz1,), epilogue=lambda acc, d: (acc + alpha * d,),
                        jobs=[exchange(g_win, REL_NEIGHBOURS, True)])
    grad_x, l_b, small_landed = _matmul("proj_ret_dx", d_p1, w1, tb=True, extras=(dx_a,), epilogue=lambda acc, d: (acc + d,),
                                        jobs=[exchange(g_win, REL_REST, False, by_source=small)])
    landed["w_in"] = [l_a, l_b]

    moments_m = [m_w_in, m_w_uq, m_w_uk, m_w_uv, m_w_out, m_w_up, m_w_down]
    moments_v = [v_w_in, v_w_uq, v_w_uk, v_w_uv, v_w_out, v_w_up, v_w_down]
    res = {}
    for n, m, v in zip(names, moments_m, moments_v):
        res[n] = [r[None] for r in _adamw("adamw_" + n, landed[n], big[n], m[0], v[0])]

    small_names = ["q_norm_g", "kv_norm_g", "ret_gn_g", "ln1_g", "ln1_b", "ln2_g", "ln2_b"]
    small_w = [q_norm_g, kv_norm_g, ret_gn_g, ln1_g, ln1_b, ln2_g, ln2_b]
    small_m = [m_q_norm_g, m_kv_norm_g, m_ret_gn_g, m_ln1_g, m_ln1_b, m_ln2_g, m_ln2_b]
    small_v = [v_q_norm_g, v_kv_norm_g, v_ret_gn_g, v_ln1_g, v_ln1_b, v_ln2_g, v_ln2_b]
    cat = lambda arrs: jnp.concatenate(arrs, axis=1)
    sres = _adamw_small(small_landed, cat(small_w), cat(small_m), cat(small_v))
    off = 0
    for n, w in zip(small_names, small_w):
        res[n] = [r[:, off:off + w.shape[1]] for r in sres]
        off += w.shape[1]

    order = ["w_in", "q_norm_g", "w_uq", "kv_norm_g", "w_uk", "w_uv", "ret_gn_g", "w_out", "ln1_g", "ln1_b",
             "w_up", "w_down", "ln2_g", "ln2_b"]
    outs = [loss, grad_x[None]]
    for k in range(4):
        outs += [res[n][k] for n in order]
    return tuple(outs)
```

```python
import functools

import numpy as np

import jax
import jax.numpy as jnp
from jax import lax
from jax.experimental import pallas as pl
from jax.experimental.pallas import tpu as pltpu

F32 = jnp.float32
BF16 = jnp.bfloat16

CHUNK = 64
RET_DK = 128
RET_DV = 128
MLA_NOPE = 128
MLA_ROPE = 64
MLA_DV = 128
MLA_Q_RANK = 768
MLA_KV_RANK = 512
MLA_HEAD_PAD = 256
ROPE_BASE = 10000.0
EPS = 1e-5
DEPTH = 1
DEEPNORM_ALPHA = (2.0 * DEPTH) ** 0.25
ADAM_LR = 0.001
ADAM_B1 = 0.9
ADAM_B2 = 0.999
ADAM_EPS = 1e-08
ADAM_WD = 0.01
ADAM_STEP = 10

N_DEV = 8
LANE = 128
SUBLANE = 8
VMEM_LIMIT = 48 * 1024 * 1024
ADAMW_TILE_ELEMS = 128 * 1024

MESH = pl.DeviceIdType.MESH
ANY = pl.BlockSpec(memory_space=pl.ANY)

NN = (((1,), (0,)), ((), ()))
NT = (((1,), (1,)), ((), ()))
TN = (((0,), (0,)), ((), ()))


def _dot(a, b, dims=NN):
    return lax.dot_general(a.astype(BF16), b.astype(BF16), dims, preferred_element_type=F32)


def _params(n_axes):
    return pltpu.CompilerParams(dimension_semantics=("arbitrary",) * n_axes, vmem_limit_bytes=VMEM_LIMIT)


def _tile(n, want):
    if n <= want:
        return n
    t = want
    while t >= LANE:
        if n % t == 0:
            return t
        t -= LANE
    return n


def _my_place():
    return lax.axis_index("x"), lax.axis_index("y"), lax.axis_index("c")


def _flat(px, py, pc):
    return 4 * px + 2 * py + pc


def _block_of(ref, kind, idx, rows, cols):
    if kind == "blk":
        return ref.at[idx]
    if kind == "rows":
        return ref.at[pl.ds(pl.multiple_of(idx * rows, rows), rows), :]
    return ref.at[:, pl.ds(pl.multiple_of(idx * cols, cols), cols)]


CHIP_X, CHIP_Y, CHIP_DIAGONAL = 0, 1, 2


class _GatherJob:
    def __init__(self, shards, kinds, mid, chips=(CHIP_X, CHIP_Y, CHIP_DIAGONAL), own=True, base=None):
        nw = len(shards)
        self.nw, self.kinds, self.chips, self.own = nw, list(kinds), tuple(chips), own
        self.shapes = [s.shape for s in shards]
        self.ins = list(shards) + (list(base) if base is not None else [])
        self.alias = [(nw + w, w) for w in range(nw)] if base is not None else []
        self.out_shape = []
        for s, kind in zip(shards, kinds):
            r, c = s.shape
            shape = {"blk": (N_DEV, r, c), "rows": (N_DEV * r, c), "cols": (r, N_DEV * c)}[kind]
            self.out_shape.append(jax.ShapeDtypeStruct(shape, s.dtype))
        self.sems = [pltpu.SemaphoreType.DMA((7 * nw,)), pltpu.SemaphoreType.DMA((7 * nw,)), pltpu.SemaphoreType.DMA((nw,))]
        self.phases = [(0.0, self._start), (mid, self._forward), (1.0, self._finish)]

    def _ctx(self, src, dst, send_sems, recv_sems):
        x, y, c = _my_place()
        all_chips = [(1 - x, y), (x, 1 - y), (1 - x, 1 - y)]
        chips = [(j, all_chips[j]) for j in self.chips]

        def place(w, dev):
            r, cc = self.shapes[w]
            return _block_of(dst[w], self.kinds[w], _flat(*dev), r, cc)

        def copy(w, k, block, to, from_src=False):
            return pltpu.make_async_remote_copy(
                src_ref=src[w] if from_src else place(w, block), dst_ref=place(w, block),
                send_sem=send_sems.at[7 * w + k], recv_sem=recv_sems.at[7 * w + k],
                device_id=to, device_id_type=MESH)

        return (x, y, c), (x, y, 1 - c), chips, place, copy

    def _start(self, src, dst, send_sems, recv_sems, local_sems):
        me, sibling, chips, place, copy = self._ctx(src, dst, send_sems, recv_sems)
        for w in range(self.nw):
            if self.own:
                pltpu.make_async_copy(src[w], place(w, me), local_sems.at[w]).start()
                copy(w, 0, me, sibling, from_src=True).start()
            for j, chip in chips:
                copy(w, 1 + j, me, (*chip, me[2]), from_src=True).start()

    def _forward(self, src, dst, send_sems, recv_sems, local_sems):
        me, sibling, chips, place, copy = self._ctx(src, dst, send_sems, recv_sems)
        for w in range(self.nw):
            for j, chip in chips:
                copy(w, 1 + j, (*chip, me[2]), me).wait_recv()
                copy(w, 4 + j, (*chip, me[2]), sibling).start()

    def _finish(self, src, dst, send_sems, recv_sems, local_sems):
        me, sibling, chips, place, copy = self._ctx(src, dst, send_sems, recv_sems)
        for w in range(self.nw):
            if self.own:
                copy(w, 0, sibling, me).wait_recv()
            for j, chip in chips:
                copy(w, 4 + j, (*chip, 1 - me[2]), me).wait_recv()
        for w in range(self.nw):
            if self.own:
                copy(w, 0, me, sibling, from_src=True).wait_send()
                pltpu.make_async_copy(src[w], place(w, me), local_sems.at[w]).wait()
            for j, chip in chips:
                copy(w, 1 + j, me, (*chip, me[2]), from_src=True).wait_send()
                copy(w, 4 + j, (*chip, me[2]), sibling).wait_send()


REL_NEIGHBOURS = [(1, 0, 0), (1, 0, 1), (0, 1, 0), (0, 1, 1)]
REL_REST = [(1, 1, 0), (1, 1, 1), (0, 0, 1)]
REL_ALL = REL_NEIGHBOURS + REL_REST


class _ExchangeJob:
    def __init__(self, grads, kinds, shard_shapes, rels, with_own, by_source=None):
        self.ins = list(grads) + ([by_source] if by_source is not None else [])
        self.kinds, self.shard_shapes, self.rels, self.with_own = list(kinds), list(shard_shapes), list(rels), with_own
        self.nw, self.has_src = len(grads), by_source is not None
        slots = len(rels) + int(with_own)
        self.out_shape = [jax.ShapeDtypeStruct((slots,) + tuple(s), g.dtype) for g, s in zip(grads, shard_shapes)]
        if self.has_src:
            self.out_shape.append(jax.ShapeDtypeStruct((N_DEV,) + by_source.shape, by_source.dtype))
        n = len(rels) * self.nw + len(REL_ALL) * int(self.has_src)
        self.sems = [pltpu.SemaphoreType.DMA((n,)), pltpu.SemaphoreType.DMA((n,)), pltpu.SemaphoreType.DMA((len(self.ins),))]
        self.phases = [(0.0, self._start), (1.0, self._finish)]

    def _copies(self, src, land, send_sems, recv_sems, local_sems):
        x, y, c = _my_place()
        my_idx = _flat(x, y, c)
        nw = self.nw

        def part(w, dev_idx):
            r, cc = self.shard_shapes[w]
            return _block_of(src[w], self.kinds[w], dev_idx, r, cc)

        def remote_copy(s, src_ref, dst_ref, rel):
            peer = (x ^ rel[0], y ^ rel[1], c ^ rel[2])
            return pltpu.make_async_remote_copy(src_ref=src_ref, dst_ref=dst_ref, send_sem=send_sems.at[s],
                                                recv_sem=recv_sems.at[s], device_id=peer, device_id_type=MESH)

        local, remote = [], []
        if self.with_own:
            local += [pltpu.make_async_copy(part(w, my_idx), land[w].at[0], local_sems.at[w]) for w in range(nw)]
        for k, rel in enumerate(self.rels):
            peer_idx = _flat(x ^ rel[0], y ^ rel[1], c ^ rel[2])
            remote += [remote_copy(nw * k + w, part(w, peer_idx), land[w].at[k + int(self.with_own)], rel) for w in range(nw)]
        if self.has_src:
            local.append(pltpu.make_async_copy(src[nw], land[nw].at[my_idx], local_sems.at[nw]))
            remote += [remote_copy(nw * len(self.rels) + k, src[nw], land[nw].at[my_idx], rel) for k, rel in enumerate(REL_ALL)]
        return local, remote

    def _start(self, *refs):
        local, remote = self._copies(*refs)
        for cp in local + remote:
            cp.start()

    def _finish(self, *refs):
        local, remote = self._copies(*refs)
        for cp in remote + local:
            cp.wait()


class _Carried:
    def __init__(self, jobs, grid):
        self.jobs = list(jobs)
        self.steps = int(np.prod(grid))
        self.grid = tuple(grid)
        self.operands = [a for j in self.jobs for a in j.ins]
        self.out_shape = [o for j in self.jobs for o in j.out_shape]
        self.sems = [s for j in self.jobs for s in j.sems]
        self.in_specs = [ANY] * len(self.operands)
        self.out_specs = [ANY] * len(self.out_shape)

    def aliases(self, n_in, n_out):
        out, i0, o0 = {}, n_in, n_out
        for job in self.jobs:
            for ji, jo in getattr(job, "alias", []):
                out[i0 + ji] = o0 + jo
            i0, o0 = i0 + len(job.ins), o0 + len(job.out_shape)
        return out

    def _step(self):
        lin = pl.program_id(0)
        for a in range(1, len(self.grid)):
            lin = lin * self.grid[a] + pl.program_id(a)
        return lin

    def _run(self, in_refs, out_refs, sem_refs, last):
        lin = self._step()
        for job in self.jobs:
            ins, in_refs = in_refs[:len(job.ins)], in_refs[len(job.ins):]
            outs, out_refs = out_refs[:len(job.out_shape)], out_refs[len(job.out_shape):]
            sems, sem_refs = sem_refs[:len(job.sems)], sem_refs[len(job.sems):]
            for frac, fn in job.phases:
                if (frac >= 1.0) != last:
                    continue
                at = min(self.steps - 1, int(frac * (self.steps - 1)))
                pl.when(lin == at)(functools.partial(fn, ins, outs, *sems))

    def before(self, in_refs, out_refs, sem_refs):
        self._run(in_refs, out_refs, sem_refs, last=False)

    def after(self, in_refs, out_refs, sem_refs):
        self._run(in_refs, out_refs, sem_refs, last=True)


def _standalone(name, job):
    def body(*refs):
        ni, no = len(job.ins), len(job.out_shape)
        for _, fn in job.phases:
            fn(refs[:ni], refs[ni:ni + no], *refs[ni + no:])

    return pl.pallas_call(body, name=name, in_specs=[ANY] * len(job.ins), out_specs=[ANY] * len(job.out_shape),
                          out_shape=job.out_shape, scratch_shapes=job.sems)(*job.ins)


def _split_refs(refs, n_in, n_job_in, n_out, n_job_out, n_scratch):
    cuts = np.cumsum([0, n_in, n_job_in, n_out, n_job_out, n_scratch])
    return [refs[cuts[i]:cuts[i + 1]] for i in range(5)] + [refs[cuts[5]:]]


def _matmul(name, a, b, *, ta=False, tb=False, out_dtypes=(F32,), epilogue=None, extras=(), tm=1024, tn=1024, tk=2048, jobs=()):
    if ta:
        K, M = a.shape
    else:
        M, K = a.shape
    N = b.shape[0] if tb else b.shape[1]
    tm, tn, tk = _tile(M, tm), _tile(N, tn), _tile(K, tk)
    nk = K // tk
    grid = (M // tm, N // tn, nk)
    dims = TN if ta else (NT if tb else NN)
    n_ex, n_out = len(extras), len(out_dtypes)
    carried = _Carried(jobs, grid)

    def body(*refs):
        ins, job_ins, out_refs, job_outs, scratch, job_sems = _split_refs(
            refs, 2 + n_ex, len(carried.operands), n_out, len(carried.out_shape), int(nk > 1))
        a_ref, b_ref, ex_refs = ins[0], ins[1], ins[2:]
        k = pl.program_id(2)
        carried.before(job_ins, job_outs, job_sems)

        def finish(acc):
            res = (acc,) if epilogue is None else epilogue(acc, *[e[...] for e in ex_refs])
            for o_ref, r in zip(out_refs, res):
                o_ref[...] = r.astype(o_ref.dtype)

        if nk == 1:
            finish(_dot(a_ref[...], b_ref[...], dims))
        else:
            acc_ref = scratch[0]

            @pl.when(k == 0)
            def _():
                acc_ref[...] = _dot(a_ref[...], b_ref[...], dims)

            @pl.when(k > 0)
            def _():
                acc_ref[...] += _dot(a_ref[...], b_ref[...], dims)

            pl.when(k == nk - 1)(lambda: finish(acc_ref[...]))

        carried.after(job_ins, job_outs, job_sems)

    a_spec = pl.BlockSpec((tk, tm), lambda i, j, k: (k, i)) if ta else pl.BlockSpec((tm, tk), lambda i, j, k: (i, k))
    b_spec = pl.BlockSpec((tn, tk), lambda i, j, k: (j, k)) if tb else pl.BlockSpec((tk, tn), lambda i, j, k: (k, j))
    tile_spec = pl.BlockSpec((tm, tn), lambda i, j, k: (i, j))
    outs = pl.pallas_call(
        body,
        name=name,
        grid=grid,
        in_specs=[a_spec, b_spec] + [tile_spec] * n_ex + carried.in_specs,
        out_specs=[tile_spec] * n_out + carried.out_specs,
        out_shape=[jax.ShapeDtypeStruct((M, N), d) for d in out_dtypes] + carried.out_shape,
        scratch_shapes=[pltpu.VMEM((tm, tn), F32)] * int(nk > 1) + carried.sems,
        input_output_aliases=carried.aliases(2 + n_ex, n_out),
        compiler_params=_params(3),
    )(a, b, *extras, *carried.operands)
    return outs[0] if len(outs) == 1 else outs


def _rowwise(name, fn, row_in, const_in, row_out, acc_out, tm):
    T = row_in[0][0].shape[0]
    n_r, n_c, n_o, n_a = len(row_in), len(const_in), len(row_out), len(acc_out)

    def body(*refs):
        ins = [r[...] for r in refs[:n_r + n_c]]
        out_refs = refs[n_r + n_c:n_r + n_c + n_o]
        acc_refs = refs[n_r + n_c + n_o:]
        res = fn(*ins)
        for o_ref, r in zip(out_refs, res[:n_o]):
            o_ref[...] = r.astype(o_ref.dtype)

        @pl.when(pl.program_id(0) == 0)
        def _():
            for a_ref in acc_refs:
                a_ref[...] = jnp.zeros_like(a_ref)

        for a_ref, r in zip(acc_refs, res[n_o:]):
            a_ref[...] += jnp.sum(r.reshape(tm // SUBLANE, SUBLANE, r.shape[-1]), axis=0)

    in_specs = [pl.BlockSpec((tm, w), functools.partial(lambda i, c: (i, c), c=cb)) for (_, w, cb) in row_in]
    in_specs += [pl.BlockSpec(c.shape, functools.partial(lambda i, nd: (0,) * nd, nd=c.ndim)) for c in const_in]
    out_specs = [pl.BlockSpec((tm, c), lambda i: (i, 0)) for (c, _) in row_out]
    out_specs += [pl.BlockSpec((SUBLANE, c), lambda i: (0, 0)) for c in acc_out]
    out_shape = [jax.ShapeDtypeStruct((T, c), d) for (c, d) in row_out]
    out_shape += [jax.ShapeDtypeStruct((SUBLANE, c), F32) for c in acc_out]
    return pl.pallas_call(
        body, name=name, grid=(T // tm,), in_specs=in_specs, out_specs=out_specs, out_shape=out_shape,
        compiler_params=_params(1),
    )(*[r[0] for r in row_in], *const_in)


def _ln_stats(z):
    mu = jnp.mean(z, axis=-1, keepdims=True)
    var = jnp.mean(jnp.square(z - mu), axis=-1, keepdims=True)
    rstd = lax.rsqrt(var + EPS)
    return (z - mu) * rstd, rstd


def _ln_bwd(dy, xhat, rstd, g):
    dxh = dy * g
    return rstd * (dxh - jnp.mean(dxh, axis=-1, keepdims=True) - xhat * jnp.mean(dxh * xhat, axis=-1, keepdims=True))


def _rms(x, g):
    return x * lax.rsqrt(jnp.mean(jnp.square(x), axis=-1, keepdims=True) + EPS) * g


def _rope128(x, cos2, sin2):
    return x * cos2 + pltpu.roll(x, RET_DK // 2, 1) * sin2


def _unrope128(dy, cos2, sin2):
    return dy * cos2 + pltpu.roll(dy * sin2, RET_DK // 2, 1)


def _rope64(x, cosp, sina, sinb):
    h = MLA_ROPE // 2
    return x * cosp + pltpu.roll(x, h, 1) * sina + pltpu.roll(x, LANE - h, 1) * sinb


def _unrope64(dy, cosp, sina, sinb):
    h = MLA_ROPE // 2
    return dy * cosp + pltpu.roll(dy * sina, LANE - h, 1) + pltpu.roll(dy * sinb, h, 1)


RET_BLOCK = 256
RET_HEADS_PER_STEP = 4


def _ret_tables(H, B):
    log_g = np.log1p(-np.exp2(-5.0 - np.arange(H, dtype=np.float32))).astype(np.float32)
    idx = np.arange(B, dtype=np.float32)
    dist = np.abs(idx[:, None] - idx[None, :])
    visible = (np.arange(B)[None, :] // CHUNK) <= (np.arange(B)[:, None] // CHUNK)
    dmat = np.exp(log_g[:, None, None] * dist).astype(np.float32) * visible[None].astype(np.float32)
    qd = np.exp(log_g[:, None] * (idx + 1.0)).astype(np.float32)
    kd = np.exp(log_g[:, None] * (B - 1 - idx)).astype(np.float32)
    sd = np.exp(log_g * B).astype(np.float32)
    bc = lambda v: np.ascontiguousarray(np.broadcast_to(v[:, :, None], (H, v.shape[1], LANE)))
    sdb = np.ascontiguousarray(np.broadcast_to(sd[:, None, None], (H, SUBLANE, LANE)))
    return jnp.asarray(dmat), jnp.asarray(bc(qd)), jnp.asarray(bc(kd)), jnp.asarray(sdb)


def _ret_specs(H, B, reverse, nb):
    blk = (lambda b: nb - 1 - b) if reverse else (lambda b: b)
    G = min(RET_HEADS_PER_STEP, H)
    W = G * LANE
    col = lambda off: pl.BlockSpec((B, W), functools.partial(lambda h, b, off: (blk(b), off // G + h), off=off))
    row = pl.BlockSpec((B, LANE), lambda h, b: (blk(b), 0))
    per_head = lambda r: pl.BlockSpec((G, r, LANE), lambda h, b: (h, 0, 0))
    dmat = pl.BlockSpec((G, B, B), lambda h, b: (h, 0, 0))
    return G, col, row, per_head, dmat


def _ret_call(name, body, n_in, n_out, operands, in_specs, out_specs, out_shape, grid, G, jobs):
    carried = _Carried(jobs, grid)

    def wrapped(*refs):
        ins, job_ins, outs, job_outs, (state,), job_sems = _split_refs(
            refs, n_in, len(carried.operands), n_out, len(carried.out_shape), 1)
        carried.before(job_ins, job_outs, job_sems)

        @pl.when(pl.program_id(1) == 0)
        def _():
            state[...] = jnp.zeros_like(state)

        body(*ins, *outs, state)
        carried.after(job_ins, job_outs, job_sems)

    return pl.pallas_call(
        wrapped, name=name, grid=grid,
        in_specs=in_specs + carried.in_specs, out_specs=out_specs + carried.out_specs,
        out_shape=out_shape + carried.out_shape,
        scratch_shapes=[pltpu.VMEM((G, RET_DK, RET_DV), F32)] + carried.sems,
        input_output_aliases=carried.aliases(n_in, n_out),
        compiler_params=_params(2),
    )(*operands, *carried.operands)


def _retention_fwd(p1, cos2, sin2, gn_g, H, jobs=()):
    T = p1.shape[0]
    B = min(RET_BLOCK, T)
    nb = T // B
    scale = RET_DK ** -0.5
    dmat, qd, kd, sd = _ret_tables(H, B)
    G, col, row, per_head, dspec = _ret_specs(H, B, False, nb)

    def body(q_ref, k_ref, v_ref, g_ref, cos_ref, sin_ref, d_ref, qd_ref, kd_ref, sd_ref, gn_ref, pre_ref, out_ref, s_ref):
        cos, sin = cos_ref[...], sin_ref[...]
        for g in range(G):
            hl = pl.ds(g * LANE, LANE)
            q = _rope128(q_ref[:, hl], cos, sin)
            k = _rope128(k_ref[:, hl], cos, sin) * scale
            v = v_ref[:, hl]
            s = s_ref[g]
            a = _dot(q, k, NT) * d_ref[g]
            ret = _dot(a, v) + _dot(q * qd_ref[g], s)
            s_ref[g] = sd_ref[g, 0:1, :] * s + _dot(k * kd_ref[g], v, TN)
            pre_ref[:, hl] = ret
            nrm, _ = _ln_stats(ret)
            out_ref[:, hl] = (jax.nn.silu(g_ref[:, hl]) * (nrm * gn_ref[:, hl])).astype(out_ref.dtype)

    out_spec = pl.BlockSpec((B, G * LANE), lambda h, b: (b, h))
    return _ret_call(
        "retention_fwd", body, 11, 2, (p1, p1, p1, p1, cos2, sin2, dmat, qd, kd, sd, gn_g),
        [col(0), col(H), col(2 * H), col(3 * H), row, row, dspec, per_head(B), per_head(B), per_head(SUBLANE),
         pl.BlockSpec((1, G * LANE), lambda h, b: (0, h))],
        [out_spec, out_spec],
        [jax.ShapeDtypeStruct((T, H * RET_DV), F32), jax.ShapeDtypeStruct((T, H * RET_DV), BF16)],
        (H // G, nb), G, jobs)


def _retention_bwd(p1, dpre, cos2, sin2, H, dq_jobs=(), dkv_jobs=()):
    T = p1.shape[0]
    B = min(RET_BLOCK, T)
    nb = T // B
    scale = RET_DK ** -0.5
    dmat, qd, kd, sd = _ret_tables(H, B)

    def dq_body(k_ref, v_ref, do_ref, cos_ref, sin_ref, d_ref, qd_ref, kd_ref, sd_ref, dq_ref, s_ref):
        cos, sin = cos_ref[...], sin_ref[...]
        for g in range(G):
            hl = pl.ds(g * LANE, LANE)
            k = _rope128(k_ref[:, hl], cos, sin) * scale
            v, do, s = v_ref[:, hl], do_ref[:, hl], s_ref[g]
            da = _dot(do, v, NT) * d_ref[g]
            dq = _dot(da, k) + _dot(do, s, NT) * qd_ref[g]
            dq_ref[:, hl] = _unrope128(dq, cos, sin).astype(dq_ref.dtype)
            s_ref[g] = sd_ref[g, 0:1, :] * s + _dot(k * kd_ref[g], v, TN)

    G, col, row, per_head, dspec = _ret_specs(H, B, False, nb)
    dq = _ret_call(
        "retention_bwd_dq", dq_body, 9, 1, (p1, p1, dpre, cos2, sin2, dmat, qd, kd, sd),
        [col(H), col(2 * H), col(0), row, row, dspec, per_head(B), per_head(B), per_head(SUBLANE)],
        [pl.BlockSpec((B, G * LANE), lambda h, b: (b, h))], [jax.ShapeDtypeStruct((T, H * RET_DK), BF16)],
        (H // G, nb), G, dq_jobs)

    def dkv_body(q_ref, k_ref, v_ref, do_ref, cos_ref, sin_ref, d_ref, qd_ref, kd_ref, sd_ref, dk_ref, dv_ref, g_ref):
        cos, sin = cos_ref[...], sin_ref[...]
        for g in range(G):
            hl = pl.ds(g * LANE, LANE)
            q = _rope128(q_ref[:, hl], cos, sin)
            k = _rope128(k_ref[:, hl], cos, sin) * scale
            v, do, st = v_ref[:, hl], do_ref[:, hl], g_ref[g]
            a = _dot(q, k, NT) * d_ref[g]
            da = _dot(do, v, NT) * d_ref[g]
            dv = _dot(a, do, TN) + _dot(k * kd_ref[g], st)
            dk = _dot(da, q, TN) + _dot(v, st, NT) * kd_ref[g]
            dv_ref[:, hl] = dv.astype(dv_ref.dtype)
            dk_ref[:, hl] = _unrope128(dk * scale, cos, sin).astype(dk_ref.dtype)
            g_ref[g] = sd_ref[g, 0:1, :] * st + _dot(q * qd_ref[g], do, TN)

    G, col, row, per_head, dspec = _ret_specs(H, B, True, nb)
    out_spec = pl.BlockSpec((B, G * LANE), lambda h, b: (nb - 1 - b, h))
    dkv = _ret_call(
        "retention_bwd_dkv", dkv_body, 10, 2, (p1, p1, p1, dpre, cos2, sin2, dmat, qd, kd, sd),
        [col(0), col(H), col(2 * H), col(0), row, row, dspec, per_head(B), per_head(B), per_head(SUBLANE)],
        [out_spec, out_spec],
        [jax.ShapeDtypeStruct((T, H * RET_DK), BF16), jax.ShapeDtypeStruct((T, H * RET_DV), BF16)],
        (H // G, nb), G, dkv_jobs)
    return dq, dkv


ATT_BLOCK = 512


ATT_ROW_CHUNKS = 2
ATT_HEADS_PER_STEP = 8
ATT_BWD_HEADS_PER_STEP = 2
LOG2E = 1.4426950408889634


def _chunk_mask(rows, row0, cols):
    qi = (lax.broadcasted_iota(jnp.int32, (rows, cols), 0) + row0) // CHUNK
    kj = lax.broadcasted_iota(jnp.int32, (rows, cols), 1) // CHUNK
    return kj <= qi


def _attention_fwd(qh, kh, vx, H, jobs=()):
    T = qh.shape[0]
    t = min(ATT_BLOCK, T)
    n = T // t
    r = t // ATT_ROW_CHUNKS
    G = min(ATT_HEADS_PER_STEP, H)
    P = MLA_HEAD_PAD
    scale = (MLA_NOPE + MLA_ROPE) ** -0.5
    c2 = scale * LOG2E
    carried = _Carried(jobs, (H // G, n, n))

    def body(*refs):
        (q_ref, k_ref, v_ref), job_ins, (o_ref, ob_ref, lse_ref), job_outs, (m_ref, acc_ref), job_sems = _split_refs(
            refs, 3, len(carried.operands), 3, len(carried.out_shape), 2)
        i, j = pl.program_id(1), pl.program_id(2)
        carried.before(job_ins, job_outs, job_sems)

        @pl.when(j == 0)
        def _():
            m_ref[...] = jnp.full_like(m_ref, -jnp.inf)
            acc_ref[...] = jnp.zeros_like(acc_ref)

        def update(masked):
            for g in range(G):
                hc = pl.ds(g * P, P)
                for ch in range(ATT_ROW_CHUNKS):
                    rs = pl.ds(ch * r, r)
                    s = _dot(q_ref[rs, hc], k_ref[:, hc], NT)
                    if masked:
                        s = jnp.where(_chunk_mask(r, ch * r, t), s, -jnp.inf)
                    m_prev = m_ref[rs, pl.ds(g * LANE, LANE)]
                    m_new = jnp.maximum(m_prev, jnp.max(s, axis=-1, keepdims=True))
                    alpha = jnp.exp2((m_prev - m_new) * c2)
                    p = jnp.exp2((s - jnp.tile(m_new, (1, t // LANE))) * c2)
                    acc_ref[rs, hc] = jnp.tile(alpha, (1, 2)) * acc_ref[rs, hc] + _dot(p, v_ref[:, hc])
                    m_ref[rs, pl.ds(g * LANE, LANE)] = m_new

        pl.when(j < i)(functools.partial(update, False))
        pl.when(j == i)(functools.partial(update, True))

        @pl.when(j == n - 1)
        def _():
            for g in range(G):
                acc = acc_ref[:, pl.ds(g * P, P)]
                l = acc[:, MLA_DV:]
                o = acc[:, :MLA_DV] / l
                o_ref[:, pl.ds(g * MLA_DV, MLA_DV)] = o
                ob_ref[:, pl.ds(g * MLA_DV, MLA_DV)] = o.astype(ob_ref.dtype)
                lse_ref[g] = m_ref[:, pl.ds(g * LANE, LANE)] * scale + jnp.log(l)

        carried.after(job_ins, job_outs, job_sems)

    kclamp = lambda i, j: jnp.minimum(j, i)
    o_spec = pl.BlockSpec((t, G * MLA_DV), lambda h, i, j: (i, h))
    return pl.pallas_call(
        body, name="attention_fwd", grid=(H // G, n, n),
        in_specs=[pl.BlockSpec((t, G * P), lambda h, i, j: (i, h)),
                  pl.BlockSpec((t, G * P), lambda h, i, j: (kclamp(i, j), h)),
                  pl.BlockSpec((t, G * P), lambda h, i, j: (kclamp(i, j), h))] + carried.in_specs,
        out_specs=[o_spec, o_spec, pl.BlockSpec((G, t, LANE), lambda h, i, j: (h, i, 0))] + carried.out_specs,
        out_shape=[jax.ShapeDtypeStruct((T, H * MLA_DV), F32), jax.ShapeDtypeStruct((T, H * MLA_DV), BF16),
                   jax.ShapeDtypeStruct((H, T, LANE), F32)] + carried.out_shape,
        scratch_shapes=[pltpu.VMEM((t, G * LANE), F32), pltpu.VMEM((t, G * P), F32)] + carried.sems,
        input_output_aliases=carried.aliases(3, 3),
        compiler_params=_params(3),
    )(qh, kh, vx, *carried.operands)


def _attention_bwd(qh, kh, vx, o, lse, dmix, H, jobs=()):
    T = qh.shape[0]
    t = min(ATT_BLOCK, T)
    n = T // t
    r = t // ATT_ROW_CHUNKS
    G = min(ATT_BWD_HEADS_PER_STEP, H)
    P = MLA_HEAD_PAD
    scale = (MLA_NOPE + MLA_ROPE) ** -0.5
    c2 = scale * LOG2E
    carried = _Carried(jobs, (H // G, n, n))

    def body(*refs):
        ((q_ref, k_ref, v_ref, do_ref, o_ref, lse_ref), job_ins, (dq_ref, dkn_ref, dkr_ref, dv_ref), job_outs,
         (dk_acc, dv_acc), job_sems) = _split_refs(refs, 6, len(carried.operands), 4, len(carried.out_shape), 2)
        j, i = pl.program_id(1), pl.program_id(2)
        carried.before(job_ins, job_outs, job_sems)

        @pl.when(jnp.logical_and(j == 0, i == 0))
        def _():
            dq_ref[...] = jnp.zeros_like(dq_ref)

        @pl.when(i == j)
        def _():
            dk_acc[...] = jnp.zeros_like(dk_acc)
            dv_acc[...] = jnp.zeros_like(dv_acc)

        def update(masked):
            for g in range(G):
                hc, hv = pl.ds(g * P, P), pl.ds(g * MLA_DV, MLA_DV)
                k, v = k_ref[:, hc], v_ref[:, pl.ds(g * P, MLA_DV)]
                for ch in range(ATT_ROW_CHUNKS):
                    rs = pl.ds(ch * r, r)
                    q, do = q_ref[rs, hc], do_ref[rs, hv]
                    s = _dot(q, k, NT)
                    if masked:
                        s = jnp.where(_chunk_mask(r, ch * r, t), s, -jnp.inf)
                    p = jnp.exp2(s * c2 - jnp.tile(lse_ref[g, rs, :] * LOG2E, (1, t // LANE)))
                    dv_acc[:, hv] += _dot(p, do, TN)
                    dp = _dot(do, v, NT)
                    delta = jnp.sum(do * o_ref[rs, hv], axis=-1, keepdims=True)
                    ds = p * (dp - delta)
                    rows = pl.ds(pl.multiple_of(i * t, t) + ch * r, r)
                    dq_ref[rows, hc] += _dot(ds, k)
                    dk_acc[:, hc] += _dot(ds, q, TN)

        pl.when(i > j)(functools.partial(update, False))
        pl.when(i == j)(functools.partial(update, True))

        @pl.when(i == n - 1)
        def _():
            for g in range(G):
                dk = dk_acc[:, pl.ds(g * P, P)] * scale
                dkn_ref[:, pl.ds(g * LANE, LANE)] = dk[:, :MLA_NOPE].astype(dkn_ref.dtype)
                dkr_ref[:, pl.ds(g * LANE, LANE)] = dk[:, MLA_NOPE:]
            dv_ref[...] = dv_acc[...].astype(dv_ref.dtype)

        carried.after(job_ins, job_outs, job_sems)

    qclamp = lambda j, i: jnp.maximum(i, j)
    kv_out = pl.BlockSpec((t, G * LANE), lambda h, j, i: (j, h))
    q_in = pl.BlockSpec((t, G * MLA_DV), lambda h, j, i: (qclamp(j, i), h))
    return pl.pallas_call(
        body, name="attention_bwd", grid=(H // G, n, n),
        in_specs=[pl.BlockSpec((t, G * P), lambda h, j, i: (qclamp(j, i), h)),
                  pl.BlockSpec((t, G * P), lambda h, j, i: (j, h)),
                  pl.BlockSpec((t, G * P), lambda h, j, i: (j, h)),
                  pl.BlockSpec((t, G * MLA_DV), lambda h, j, i: (qclamp(j, i), H // G + h)),
                  q_in,
                  pl.BlockSpec((G, t, LANE), lambda h, j, i: (h, qclamp(j, i), 0))] + carried.in_specs,
        out_specs=[pl.BlockSpec((T, G * P), lambda h, j, i: (0, h)), kv_out, kv_out, kv_out] + carried.out_specs,
        out_shape=[jax.ShapeDtypeStruct((T, H * P), F32), jax.ShapeDtypeStruct((T, H * LANE), BF16),
                   jax.ShapeDtypeStruct((T, H * LANE), F32), jax.ShapeDtypeStruct((T, H * LANE), BF16)] + carried.out_shape,
        scratch_shapes=[pltpu.VMEM((t, G * P), F32), pltpu.VMEM((t, G * MLA_DV), F32)] + carried.sems,
        input_output_aliases=carried.aliases(6, 4),
        compiler_params=_params(3),
    )(qh, kh, vx, dmix, o, lse, *carried.operands)


def _adamw_math(w, g, m, v):
    m = ADAM_B1 * m + (1.0 - ADAM_B1) * g
    v = ADAM_B2 * v + (1.0 - ADAM_B2) * jnp.square(g)
    m_hat = m / (1.0 - ADAM_B1 ** ADAM_STEP)
    v_hat = v / (1.0 - ADAM_B2 ** ADAM_STEP)
    delta = -ADAM_LR * (m_hat / (jnp.sqrt(v_hat) + ADAM_EPS) + ADAM_WD * w)
    return delta, m, v


def _adamw(name, landed, w, m, v):
    R, C = w.shape
    tr = 16
    while tr * 2 * C <= ADAMW_TILE_ELEMS and R % (tr * 2) == 0:
        tr *= 2
    tr = tr if R % tr == 0 else R
    nl = len(landed)
    assert sum(l.shape[0] for l in landed) == N_DEV

    def body(*refs):
        l_refs, (w_ref, m_ref, v_ref, g_out, d_out, m_out, v_out) = refs[:nl], refs[nl:]
        g = None
        for l_ref in l_refs:
            for s in range(l_ref.shape[0]):
                part = l_ref[s].astype(F32)
                g = part if g is None else g + part
        delta, mn, vn = _adamw_math(w_ref[...], g, m_ref[...], v_ref[...])
        g_out[...] = g
        d_out[...] = delta
        m_out[...] = mn
        v_out[...] = vn

    spec = pl.BlockSpec((tr, C), lambda i: (i, 0))
    l_specs = [pl.BlockSpec((l.shape[0], tr, C), lambda i: (0, i, 0)) for l in landed]
    return pl.pallas_call(
        body, name=name, grid=(R // tr,), in_specs=l_specs + [spec, spec, spec], out_specs=[spec] * 4,
        out_shape=[jax.ShapeDtypeStruct((R, C), F32)] * 4, compiler_params=_params(1),
    )(*landed, w, m, v)


def _adamw_small(landed, w, m, v):
    P = w.shape[1]

    def body(l_ref, w_ref, m_ref, v_ref, g_out, d_out, m_out, v_out):
        acc = l_ref[0]
        for s in range(1, N_DEV):
            acc = acc + l_ref[s]
        g = jnp.sum(acc, axis=0, keepdims=True)
        delta, mn, vn = _adamw_math(w_ref[...], g, m_ref[...], v_ref[...])
        g_out[...] = g
        d_out[...] = delta
        m_out[...] = mn
        v_out[...] = vn

    return pl.pallas_call(
        body, name="adamw_replicated", out_shape=[jax.ShapeDtypeStruct((1, P), F32)] * 4,
    )(landed, w, m, v)


def _rope_tables(pos, T):
    p = pos.astype(F32)[:, None]
    inv_r = ROPE_BASE ** (-jnp.arange(0, RET_DK, 2, dtype=F32) / RET_DK)
    ang = p * inv_r
    cos, sin = jnp.cos(ang), jnp.sin(ang)
    cos2 = jnp.concatenate([cos, cos], axis=-1)
    sin2 = jnp.concatenate([-sin, sin], axis=-1)
    inv_m = ROPE_BASE ** (-jnp.arange(0, MLA_ROPE, 2, dtype=F32) / MLA_ROPE)
    ang = p * inv_m
    cos, sin = jnp.cos(ang), jnp.sin(ang)
    zh = jnp.zeros_like(cos)
    z2 = jnp.zeros((T, LANE - MLA_ROPE), F32)
    cosp = jnp.concatenate([cos, cos, z2], axis=-1)
    sina = jnp.concatenate([zh, sin, z2], axis=-1)
    sinb = jnp.concatenate([-sin, zh, z2], axis=-1)
    return cos2, sin2, cosp, sina, sinb


def kernel(x, positions, w_in, q_norm_g, w_uq, kv_norm_g, w_uk, w_uv, ret_gn_g, w_out, ln1_g, ln1_b, w_up, w_down, ln2_g, ln2_b, loss_target, m_w_in, m_q_norm_g, m_w_uq, m_kv_norm_g, m_w_uk, m_w_uv, m_ret_gn_g, m_w_out, m_ln1_g, m_ln1_b, m_w_up, m_w_down, m_ln2_g, m_ln2_b, v_w_in, v_q_norm_g, v_w_uq, v_kv_norm_g, v_w_uk, v_w_uv, v_ret_gn_g, v_w_out, v_ln1_g, v_ln1_b, v_w_up, v_w_down, v_ln2_g, v_ln2_b):
    T, D = x.shape[1], x.shape[2]
    H = D // 256
    RW = H * RET_DV
    x = x[0]
    tgt = loss_target[0]
    alpha = DEEPNORM_ALPHA
    in_width = w_in.shape[2] * N_DEV
    mla_in = in_width - 4 * RW
    mla_in_pad = -(-mla_in // 512) * 512
    f_uq = MLA_NOPE + MLA_ROPE

    names = ["w_in", "w_uq", "w_uk", "w_uv", "w_out", "w_up", "w_down"]
    big = dict(zip(names, [w_in[0], w_uq[0], w_uk[0], w_uv[0], w_out[0], w_up[0], w_down[0]]))
    kind = dict(zip(names, ["blk", "blk", "cols", "cols", "rows", "cols", "rows"]))
    half = {n: w.astype(BF16) for n, w in big.items()}

    def gather(ns, mid, **parts):
        return _GatherJob([half[n] for n in ns], [kind[n] for n in ns], mid, **parts)

    (g_in,) = _standalone("gather_w_in", gather(["w_in"], 0.0))
    win = jnp.transpose(g_in, (1, 0, 2)).reshape(D, in_width)
    w1 = win[:, :4 * RW]
    w2 = jnp.pad(win[:, 4 * RW:], ((0, 0), (0, mla_in_pad - mla_in)))

    cos2, sin2, cosp, sina, sinb = _rope_tables(positions[0], T)
    xb = x.astype(BF16)
    tr = min(256, T)

    p1, g_uq, wuk, wuv, wout = _matmul("proj_ret", xb, w1,
                                       jobs=[gather(["w_uq", "w_uk", "w_uv", "w_out"], 0.65)])
    wuq = jnp.transpose(g_uq, (1, 0, 2)).reshape(MLA_Q_RANK, H, f_uq)
    wuq = jnp.pad(wuq, ((0, 0), (0, 0), (0, MLA_HEAD_PAD - f_uq))).reshape(MLA_Q_RANK, H * MLA_HEAD_PAD)
    wukv = jnp.concatenate([wuk, wuv], axis=1)
    p2 = _matmul("proj_mla", xb, w2)
    c_kv, c_kr = MLA_Q_RANK, MLA_Q_RANK + MLA_KV_RANK

    def mla_prep(p, cp, sa, sb, gq, gkv):
        cqn = _rms(p[:, :c_kv], gq)
        ckvn = _rms(p[:, c_kv:c_kr], gkv)
        return cqn, ckvn, _rope64(p[:, c_kr:c_kr + LANE], cp, sa, sb)

    cqn, ckvn, krope = _rowwise(
        "mla_prep", mla_prep, [(p2, mla_in_pad, 0), (cosp, LANE, 0), (sina, LANE, 0), (sinb, LANE, 0)],
        [q_norm_g, kv_norm_g], [(MLA_Q_RANK, BF16), (MLA_KV_RANK, BF16), (LANE, BF16)], [], tr)
    qf = _matmul("mla_q", cqn, wuq)
    kv = _matmul("mla_kv", ckvn, wukv, out_dtypes=(BF16,))

    def mla_heads(q, kn, vv, kr, cp, sa, sb):
        qs, ks, vs = [], [], []
        ones = jnp.ones((q.shape[0], MLA_DV), vv.dtype)
        for h in range(H):
            o = h * MLA_HEAD_PAD
            qs += [q[:, o:o + MLA_NOPE], _rope64(q[:, o + MLA_NOPE:o + MLA_HEAD_PAD], cp, sa, sb)]
            ks += [kn[:, h * MLA_NOPE:(h + 1) * MLA_NOPE], kr]
            vs += [vv[:, h * MLA_DV:(h + 1) * MLA_DV], ones]
        return jnp.concatenate(qs, axis=1), jnp.concatenate(ks, axis=1), jnp.concatenate(vs, axis=1)

    qh, kh, vx = _rowwise(
        "mla_heads", mla_heads,
        [(qf, H * MLA_HEAD_PAD, 0), (kv, RW, 0), (kv, RW, 1), (krope, LANE, 0), (cosp, LANE, 0), (sina, LANE, 0), (sinb, LANE, 0)],
        [], [(H * MLA_HEAD_PAD, BF16)] * 3, [], tr)
    att_o, att_ob, lse, wup_near = _attention_fwd(qh, kh, vx, H, jobs=[gather(["w_up"], 0.6, chips=(CHIP_X, CHIP_Y))])
    ret_pre, ret_ob, wup = _retention_fwd(
        p1, cos2, sin2, ret_gn_g, H, jobs=[gather(["w_up"], 0.7, chips=(CHIP_DIAGONAL,), own=False, base=[wup_near])])
    mixin = jnp.concatenate([ret_ob, att_ob], axis=1)
    mix = _matmul("mix_out", mixin, wout)

    def ln1_fwd(xt, mt, g, b):
        z = alpha * xt + mt
        xhat, _ = _ln_stats(z)
        y = xhat * g + b
        return z, y, y

    z1, x1, x1b = _rowwise("ln1", ln1_fwd, [(x, D, 0), (mix, D, 0)], [ln1_g, ln1_b],
                           [(D, F32), (D, F32), (D, BF16)], [], tr)

    def relu2(acc):
        r = jnp.maximum(acc, 0.0)
        return r * r, r

    a2b, rb, wdown = _matmul("mlp_up", x1b, wup, out_dtypes=(BF16, BF16), epilogue=relu2, jobs=[gather(["w_down"], 0.8)])
    hmlp = _matmul("mlp_down", a2b, wdown)

    def ln2_loss(x1t, ht, tt, g, b):
        xhat, rstd = _ln_stats(alpha * x1t + ht)
        err = xhat * g + b - tt
        dy = err / D
        dz = _ln_bwd(dy, xhat, rstd, g)
        lrow = 0.5 * jnp.mean(jnp.square(err), axis=-1, keepdims=True)
        return dz, dz, dy * xhat, dy, jnp.broadcast_to(lrow, (lrow.shape[0], LANE))

    dz2, dz2b, dg2, db2, lpart = _rowwise("ln2_loss", ln2_loss, [(x1, D, 0), (hmlp, D, 0), (tgt, D, 0)], [ln2_g, ln2_b],
                                          [(D, F32), (D, BF16)], [D, D, LANE], tr)
    loss = lax.psum(jnp.sum(lpart[:, 0]), ("x", "y", "c"))

    def exchange(grads, rels, with_own, by_source=None):
        ns = list(grads)
        return _ExchangeJob([grads[n] for n in ns], [kind[n] for n in ns], [big[n].shape for n in ns], rels, with_own, by_source)

    landed = {}
    da = _matmul("mlp_down_dx", dz2b, wdown, tb=True, out_dtypes=(BF16,), extras=(rb,),
                 epilogue=lambda acc, r: (acc * (2.0 * r.astype(F32)),))
    g_wdown = _matmul("mlp_down_dw", a2b, dz2b, ta=True, out_dtypes=(BF16,))
    g_wup, l_a = _matmul("mlp_up_dw", x1b, da, ta=True, out_dtypes=(BF16,),
                         jobs=[exchange({"w_down": g_wdown}, REL_NEIGHBOURS, True)])
    dx1m, l_b = _matmul("mlp_up_dx", da, wup, tb=True, jobs=[exchange({"w_down": g_wdown}, REL_REST, False)])
    landed["w_down"] = [l_a, l_b]

    def ln1_bwd(dm, dzz, z, g):
        xhat, rstd = _ln_stats(z)
        dy = dm + alpha * dzz
        dz = _ln_bwd(dy, xhat, rstd, g)
        return dz, dz, dy * xhat, dy

    dz1, dz1b, dg1, db1 = _rowwise("ln1_bwd", ln1_bwd, [(dx1m, D, 0), (dz2, D, 0), (z1, D, 0)], [ln1_g],
                                   [(D, F32), (D, BF16)], [D, D], tr)
    g_wout = _matmul("mix_out_dw", mixin, dz1b, ta=True, out_dtypes=(BF16,))
    dmix = _matmul("mix_out_dx", dz1b, wout, tb=True)

    dqh, dkn, dkr, dvv, l_a = _attention_bwd(qh, kh, vx, att_o, lse, dmix, H,
                                             jobs=[exchange({"w_up": g_wup}, REL_NEIGHBOURS, True)])
    att_scale = (MLA_NOPE + MLA_ROPE) ** -0.5

    def mla_heads_bwd(dq, dkrh, cp, sa, sb):
        parts, acc = [], dkrh[:, :LANE]
        dq = dq * att_scale
        for h in range(H):
            o = h * MLA_HEAD_PAD
            parts += [dq[:, o:o + MLA_NOPE], _unrope64(dq[:, o + MLA_NOPE:o + MLA_HEAD_PAD], cp, sa, sb)]
            if h:
                acc = acc + dkrh[:, h * LANE:(h + 1) * LANE]
        return jnp.concatenate(parts, axis=1), _unrope64(acc, cp, sa, sb)

    dqb, dkr128 = _rowwise(
        "mla_heads_bwd", mla_heads_bwd,
        [(dqh, H * MLA_HEAD_PAD, 0), (dkr, H * LANE, 0), (cosp, LANE, 0), (sina, LANE, 0), (sinb, LANE, 0)],
        [], [(H * MLA_HEAD_PAD, BF16), (LANE, F32)], [], tr)
    dcqn = _matmul("mla_q_dx", dqb, wuq, tb=True)
    g_wuq = _matmul("mla_q_dw", cqn, dqb, ta=True, out_dtypes=(BF16,))
    dkvb = jnp.concatenate([dkn, dvv], axis=1)
    dckvn = _matmul("mla_kv_dx", dkvb, wukv, tb=True)
    g_wukv = _matmul("mla_kv_dw", ckvn, dkvb, ta=True, out_dtypes=(BF16,))

    def rms_bwd(c, dy, g):
        rstd = lax.rsqrt(jnp.mean(jnp.square(c), axis=-1, keepdims=True) + EPS)
        dyg = dy * g
        dc = rstd * (dyg - c * (rstd * rstd) * jnp.mean(dyg * c, axis=-1, keepdims=True))
        return dc, dy * c * rstd

    def mla_prep_bwd(p, dq_, dkv_, dkr_, gq, gkv):
        dcq, dgq_ = rms_bwd(p[:, :c_kv], dq_, gq)
        dckv, dgkv_ = rms_bwd(p[:, c_kv:c_kr], dkv_, gkv)
        pad = jnp.zeros((p.shape[0], mla_in_pad - c_kr - LANE), F32)
        return jnp.concatenate([dcq, dckv, dkr_, pad], axis=1), dgq_, dgkv_

    d_p2, dgq, dgkv = _rowwise(
        "mla_prep_bwd", mla_prep_bwd, [(p2, mla_in_pad, 0), (dcqn, MLA_Q_RANK, 0), (dckvn, MLA_KV_RANK, 0), (dkr128, LANE, 0)],
        [q_norm_g, kv_norm_g], [(mla_in_pad, BF16)], [MLA_Q_RANK, MLA_KV_RANK], tr)

    def gate_bwd(pre, gate, dout, gn):
        dpre, dgate, dgn = [], [], []
        for h in range(H):
            sl = slice(h * RET_DV, (h + 1) * RET_DV)
            nrm, rstd = _ln_stats(pre[:, sl])
            gate_h, do_h, gn_h = gate[:, sl], dout[:, sl], gn[:, sl]
            sg = jax.nn.sigmoid(gate_h)
            silu = gate_h * sg
            dgate.append(do_h * (nrm * gn_h) * (sg * (1.0 + gate_h * (1.0 - sg))))
            dn = do_h * silu
            dgn.append(dn * nrm)
            dpre.append(_ln_bwd(dn, nrm, rstd, gn_h))
        return jnp.concatenate(dpre, axis=1), jnp.concatenate(dgate, axis=1), jnp.concatenate(dgn, axis=1)

    dpre, drg, dgn = _rowwise("ret_gate_bwd", gate_bwd, [(ret_pre, RW, 0), (p1, RW, 3), (dmix, RW, 0)], [ret_gn_g],
                              [(RW, F32), (RW, BF16)], [RW], tr)
    (drq, l_b), (drk, drv, l_c) = _retention_bwd(
        p1, dpre, cos2, sin2, H, dq_jobs=[exchange({"w_up": g_wup}, REL_REST[:1], False)],
        dkv_jobs=[exchange({"w_up": g_wup}, REL_REST[1:], False)])
    landed["w_up"] = [l_a, l_b, l_c]
    d_p1 = jnp.concatenate([drq, drk, drv, drg], axis=1)

    g_uq = g_wuq.reshape(MLA_Q_RANK, H, MLA_HEAD_PAD)[:, :, :f_uq].reshape(MLA_Q_RANK, N_DEV, H * f_uq // N_DEV)
    mid_grads = {"w_uq": jnp.transpose(g_uq, (1, 0, 2)), "w_uk": g_wukv[:, :RW], "w_uv": g_wukv[:, RW:], "w_out": g_wout}
    g_w1, *l_a = _matmul("proj_ret_dw", xb, d_p1, ta=True, out_dtypes=(BF16,), jobs=[exchange(mid_grads, REL_NEIGHBOURS, True)])
    g_w2, *l_b = _matmul("proj_mla_dw", xb, d_p2, ta=True, out_dtypes=(BF16,), jobs=[exchange(mid_grads, REL_REST, False)])
    for n, a_, b_ in zip(mid_grads, l_a, l_b):
        landed[n] = [a_, b_]

    g_win = jnp.concatenate([g_w1, g_w2[:, :mla_in]], axis=1)
    g_win = {"w_in": jnp.transpose(g_win.reshape(D, N_DEV, in_width // N_DEV), (1, 0, 2))}
    small = jnp.concatenate([dgq, dgkv, dgn, dg1, db1, dg2, db2], axis=1)
    dx_a, l_a = _matmul("proj_mla_dx", d_p2, w2, tb=True, extras=(dz1,), epilogue=lambda acc, d: (acc + alpha * d,),
                        jobs=[exchange(g_win, REL_NEIGHBOURS, True)])
    grad_x, l_b, small_landed = _matmul("proj_ret_dx", d_p1, w1, tb=True, extras=(dx_a,), epilogue=lambda acc, d: (acc + d,),
                                        jobs=[exchange(g_win, REL_REST, False, by_source=small)])
    landed["w_in"] = [l_a, l_b]

    moments_m = [m_w_in, m_w_uq, m_w_uk, m_w_uv, m_w_out, m_w_up, m_w_down]
    moments_v = [v_w_in, v_w_uq, v_w_uk, v_w_uv, v_w_out, v_w_up, v_w_down]
    res = {}
    for n, m, v in zip(names, moments_m, moments_v):
        res[n] = [r[None] for r in _adamw("adamw_" + n, landed[n], big[n], m[0], v[0])]

    small_names = ["q_norm_g", "kv_norm_g", "ret_gn_g", "ln1_g", "ln1_b", "ln2_g", "ln2_b"]
    small_w = [q_norm_g, kv_norm_g, ret_gn_g, ln1_g, ln1_b, ln2_g, ln2_b]
    small_m = [m_q_norm_g, m_kv_norm_g, m_ret_gn_g, m_ln1_g, m_ln1_b, m_ln2_g, m_ln2_b]
    small_v = [v_q_norm_g, v_kv_norm_g, v_ret_gn_g, v_ln1_g, v_ln1_b, v_ln2_g, v_ln2_b]
    cat = lambda arrs: jnp.concatenate(arrs, axis=1)
    sres = _adamw_small(small_landed, cat(small_w), cat(small_m), cat(small_v))
    off = 0
    for n, w in zip(small_names, small_w):
        res[n] = [r[:, off:off + w.shape[1]] for r in sres]
        off += w.shape[1]

    order = ["w_in", "q_norm_g", "w_uq", "kv_norm_g", "w_uk", "w_uv", "ret_gn_g", "w_out", "ln1_g", "ln1_b",
             "w_up", "w_down", "ln2_g", "ln2_b"]
    outs = [loss, grad_x[None]]
    for k in range(4):
        outs += [res[n][k] for n in order]
    return tuple(outs)
```

```python
import functools

import numpy as np

import jax
import jax.numpy as jnp
from jax import lax
from jax.experimental import pallas as pl
from jax.experimental.pallas import tpu as pltpu

F32 = jnp.float32
BF16 = jnp.bfloat16

CHUNK = 64
RET_DK = 128
RET_DV = 128
MLA_NOPE = 128
MLA_ROPE = 64
MLA_DV = 128
MLA_Q_RANK = 768
MLA_KV_RANK = 512
MLA_HEAD_PAD = 256
ROPE_BASE = 10000.0
EPS = 1e-5
DEPTH = 1
DEEPNORM_ALPHA = (2.0 * DEPTH) ** 0.25
ADAM_LR = 0.001
ADAM_B1 = 0.9
ADAM_B2 = 0.999
ADAM_EPS = 1e-08
ADAM_WD = 0.01
ADAM_STEP = 10

N_DEV = 8
LANE = 128
SUBLANE = 8
VMEM_LIMIT = 48 * 1024 * 1024
ADAMW_TILE_ELEMS = 128 * 1024

MESH = pl.DeviceIdType.MESH
ANY = pl.BlockSpec(memory_space=pl.ANY)

NN = (((1,), (0,)), ((), ()))
NT = (((1,), (1,)), ((), ()))
TN = (((0,), (0,)), ((), ()))


def _dot(a, b, dims=NN):
    return lax.dot_general(a.astype(BF16), b.astype(BF16), dims, preferred_element_type=F32)


def _params(n_axes):
    return pltpu.CompilerParams(dimension_semantics=("arbitrary",) * n_axes, vmem_limit_bytes=VMEM_LIMIT)


def _tile(n, want):
    if n <= want:
        return n
    t = want
    while t >= LANE:
        if n % t == 0:
            return t
        t -= LANE
    return n


def _my_place():
    return lax.axis_index("x"), lax.axis_index("y"), lax.axis_index("c")


def _flat(px, py, pc):
    return 4 * px + 2 * py + pc


def _block_of(ref, kind, idx, rows, cols):
    if kind == "blk":
        return ref.at[idx]
    if kind == "rows":
        return ref.at[pl.ds(pl.multiple_of(idx * rows, rows), rows), :]
    return ref.at[:, pl.ds(pl.multiple_of(idx * cols, cols), cols)]


CHIP_X, CHIP_Y, CHIP_DIAGONAL = 0, 1, 2


class _GatherJob:
    def __init__(self, shards, kinds, mid, chips=(CHIP_X, CHIP_Y, CHIP_DIAGONAL), own=True, base=None):
        nw = len(shards)
        self.nw, self.kinds, self.chips, self.own = nw, list(kinds), tuple(chips), own
        self.shapes = [s.shape for s in shards]
        self.ins = list(shards) + (list(base) if base is not None else [])
        self.alias = [(nw + w, w) for w in range(nw)] if base is not None else []
        self.out_shape = []
        for s, kind in zip(shards, kinds):
            r, c = s.shape
            shape = {"blk": (N_DEV, r, c), "rows": (N_DEV * r, c), "cols": (r, N_DEV * c)}[kind]
            self.out_shape.append(jax.ShapeDtypeStruct(shape, s.dtype))
        self.sems = [pltpu.SemaphoreType.DMA((7 * nw,)), pltpu.SemaphoreType.DMA((7 * nw,)), pltpu.SemaphoreType.DMA((nw,))]
        self.phases = [(0.0, self._start), (mid, self._forward), (1.0, self._finish)]

    def _ctx(self, src, dst, send_sems, recv_sems):
        x, y, c = _my_place()
        all_chips = [(1 - x, y), (x, 1 - y), (1 - x, 1 - y)]
        chips = [(j, all_chips[j]) for j in self.chips]

        def place(w, dev):
            r, cc = self.shapes[w]
            return _block_of(dst[w], self.kinds[w], _flat(*dev), r, cc)

        def copy(w, k, block, to, from_src=False):
            return pltpu.make_async_remote_copy(
                src_ref=src[w] if from_src else place(w, block), dst_ref=place(w, block),
                send_sem=send_sems.at[7 * w + k], recv_sem=recv_sems.at[7 * w + k],
                device_id=to, device_id_type=MESH)

        return (x, y, c), (x, y, 1 - c), chips, place, copy

    def _start(self, src, dst, send_sems, recv_sems, local_sems):
        me, sibling, chips, place, copy = self._ctx(src, dst, send_sems, recv_sems)
        for w in range(self.nw):
            if self.own:
                pltpu.make_async_copy(src[w], place(w, me), local_sems.at[w]).start()
                copy(w, 0, me, sibling, from_src=True).start()
            for j, chip in chips:
                copy(w, 1 + j, me, (*chip, me[2]), from_src=True).start()

    def _forward(self, src, dst, send_sems, recv_sems, local_sems):
        me, sibling, chips, place, copy = self._ctx(src, dst, send_sems, recv_sems)
        for w in range(self.nw):
            for j, chip in chips:
                copy(w, 1 + j, (*chip, me[2]), me).wait_recv()
                copy(w, 4 + j, (*chip, me[2]), sibling).start()

    def _finish(self, src, dst, send_sems, recv_sems, local_sems):
        me, sibling, chips, place, copy = self._ctx(src, dst, send_sems, recv_sems)
        for w in range(self.nw):
            if self.own:
                copy(w, 0, sibling, me).wait_recv()
            for j, chip in chips:
                copy(w, 4 + j, (*chip, 1 - me[2]), me).wait_recv()
        for w in range(self.nw):
            if self.own:
                copy(w, 0, me, sibling, from_src=True).wait_send()
                pltpu.make_async_copy(src[w], place(w, me), local_sems.at[w]).wait()
            for j, chip in chips:
                copy(w, 1 + j, me, (*chip, me[2]), from_src=True).wait_send()
                copy(w, 4 + j, (*chip, me[2]), sibling).wait_send()


REL_ALL = [(1, 0, 0), (1, 0, 1), (0, 1, 0), (0, 1, 1), (1, 1, 0), (1, 1, 1), (0, 0, 1)]
N_CHIP = 4


class _PairJob:
    def __init__(self, grads, kinds, shard_shapes):
        self.ins = list(grads)
        self.kinds, self.shard_shapes, self.nw = list(kinds), list(shard_shapes), len(grads)
        shapes = [jax.ShapeDtypeStruct((N_CHIP,) + tuple(s), g.dtype) for g, s in zip(grads, shard_shapes)]
        self.out_shape = shapes + shapes
        n = N_CHIP * self.nw
        self.sems = [pltpu.SemaphoreType.DMA((n,)), pltpu.SemaphoreType.DMA((n,)), pltpu.SemaphoreType.DMA((n,))]
        self.phases = [(0.0, self._start), (1.0, self._finish)]

    def _copies(self, src, outs, send_sems, recv_sems, local_sems):
        x, y, c = _my_place()
        mine, theirs = outs[:self.nw], outs[self.nw:]
        local, remote = [], []
        for w in range(self.nw):
            r, cc = self.shard_shapes[w]
            for q in range(N_CHIP):
                s = N_CHIP * w + q
                own_part = _block_of(src[w], self.kinds[w], 2 * q + c, r, cc)
                sib_part = _block_of(src[w], self.kinds[w], 2 * q + (1 - c), r, cc)
                local.append(pltpu.make_async_copy(own_part, mine[w].at[q], local_sems.at[s]))
                remote.append(pltpu.make_async_remote_copy(
                    src_ref=sib_part, dst_ref=theirs[w].at[q], send_sem=send_sems.at[s], recv_sem=recv_sems.at[s],
                    device_id=(x, y, 1 - c), device_id_type=MESH))
        return local, remote

    def _start(self, *refs):
        local, remote = self._copies(*refs)
        for cp in local + remote:
            cp.start()

    def _finish(self, *refs):
        local, remote = self._copies(*refs)
        for cp in remote + local:
            cp.wait()


class _ChipJob:
    RELS = [(1, 0), (0, 1), (1, 1)]

    def __init__(self, chip_sums, shard_shapes, by_source=None):
        self.ins = list(chip_sums) + ([by_source] if by_source is not None else [])
        self.nw, self.has_src = len(chip_sums), by_source is not None
        self.out_shape = [jax.ShapeDtypeStruct((N_CHIP,) + tuple(s), a.dtype) for a, s in zip(chip_sums, shard_shapes)]
        if self.has_src:
            self.out_shape.append(jax.ShapeDtypeStruct((N_DEV,) + by_source.shape, by_source.dtype))
        n = len(self.RELS) * self.nw + len(REL_ALL) * int(self.has_src)
        self.sems = [pltpu.SemaphoreType.DMA((n,)), pltpu.SemaphoreType.DMA((n,)), pltpu.SemaphoreType.DMA((len(self.ins),))]
        self.phases = [(0.0, self._start), (1.0, self._finish)]

    def _copies(self, src, land, send_sems, recv_sems, local_sems):
        x, y, c = _my_place()
        nw = self.nw

        def remote_copy(s, src_ref, dst_ref, peer):
            return pltpu.make_async_remote_copy(src_ref=src_ref, dst_ref=dst_ref, send_sem=send_sems.at[s],
                                                recv_sem=recv_sems.at[s], device_id=peer, device_id_type=MESH)

        local = [pltpu.make_async_copy(src[w].at[2 * x + y], land[w].at[len(self.RELS)], local_sems.at[w]) for w in range(nw)]
        remote = []
        for k, (fx, fy) in enumerate(self.RELS):
            px, py = x ^ fx, y ^ fy
            remote += [remote_copy(nw * k + w, src[w].at[2 * px + py], land[w].at[k], (px, py, c)) for w in range(nw)]
        if self.has_src:
            my_idx = _flat(x, y, c)
            local.append(pltpu.make_async_copy(src[nw], land[nw].at[my_idx], local_sems.at[nw]))
            remote += [remote_copy(nw * len(self.RELS) + k, src[nw], land[nw].at[my_idx], (x ^ fx, y ^ fy, c ^ fc))
                       for k, (fx, fy, fc) in enumerate(REL_ALL)]
        return local, remote

    def _start(self, *refs):
        local, remote = self._copies(*refs)
        for cp in local + remote:
            cp.start()

    def _finish(self, *refs):
        local, remote = self._copies(*refs)
        for cp in remote + local:
            cp.wait()


def _pair_sum(name, mine, theirs):
    _, R, C = mine.shape
    tr = 16
    while tr * 2 * C <= ADAMW_TILE_ELEMS and R % (tr * 2) == 0:
        tr *= 2
    tr = tr if R % tr == 0 else R

    def body(a_ref, b_ref, o_ref):
        o_ref[...] = (a_ref[...].astype(F32) + b_ref[...].astype(F32)).astype(o_ref.dtype)

    spec = pl.BlockSpec((N_CHIP, tr, C), lambda i: (0, i, 0))
    return pl.pallas_call(body, name=name, grid=(R // tr,), in_specs=[spec, spec], out_specs=spec,
                          out_shape=jax.ShapeDtypeStruct(mine.shape, mine.dtype), compiler_params=_params(1))(mine, theirs)


class _Carried:
    def __init__(self, jobs, grid):
        self.jobs = list(jobs)
        self.steps = int(np.prod(grid))
        self.grid = tuple(grid)
        self.operands = [a for j in self.jobs for a in j.ins]
        self.out_shape = [o for j in self.jobs for o in j.out_shape]
        self.sems = [s for j in self.jobs for s in j.sems]
        self.in_specs = [ANY] * len(self.operands)
        self.out_specs = [ANY] * len(self.out_shape)

    def aliases(self, n_in, n_out):
        out, i0, o0 = {}, n_in, n_out
        for job in self.jobs:
            for ji, jo in getattr(job, "alias", []):
                out[i0 + ji] = o0 + jo
            i0, o0 = i0 + len(job.ins), o0 + len(job.out_shape)
        return out

    def _step(self):
        lin = pl.program_id(0)
        for a in range(1, len(self.grid)):
            lin = lin * self.grid[a] + pl.program_id(a)
        return lin

    def _run(self, in_refs, out_refs, sem_refs, last):
        lin = self._step()
        for job in self.jobs:
            ins, in_refs = in_refs[:len(job.ins)], in_refs[len(job.ins):]
            outs, out_refs = out_refs[:len(job.out_shape)], out_refs[len(job.out_shape):]
            sems, sem_refs = sem_refs[:len(job.sems)], sem_refs[len(job.sems):]
            for frac, fn in job.phases:
                if (frac >= 1.0) != last:
                    continue
                at = min(self.steps - 1, int(frac * (self.steps - 1)))
                pl.when(lin == at)(functools.partial(fn, ins, outs, *sems))

    def before(self, in_refs, out_refs, sem_refs):
        self._run(in_refs, out_refs, sem_refs, last=False)

    def after(self, in_refs, out_refs, sem_refs):
        self._run(in_refs, out_refs, sem_refs, last=True)


def _standalone(name, job):
    def body(*refs):
        ni, no = len(job.ins), len(job.out_shape)
        for _, fn in job.phases:
            fn(refs[:ni], refs[ni:ni + no], *refs[ni + no:])

    return pl.pallas_call(body, name=name, in_specs=[ANY] * len(job.ins), out_specs=[ANY] * len(job.out_shape),
                          out_shape=job.out_shape, scratch_shapes=job.sems)(*job.ins)


def _split_refs(refs, n_in, n_job_in, n_out, n_job_out, n_scratch):
    cuts = np.cumsum([0, n_in, n_job_in, n_out, n_job_out, n_scratch])
    return [refs[cuts[i]:cuts[i + 1]] for i in range(5)] + [refs[cuts[5]:]]


def _matmul(name, a, b, *, ta=False, tb=False, out_dtypes=(F32,), epilogue=None, extras=(), tm=1024, tn=1024, tk=2048, jobs=()):
    if ta:
        K, M = a.shape
    else:
        M, K = a.shape
    N = b.shape[0] if tb else b.shape[1]
    tm, tn, tk = _tile(M, tm), _tile(N, tn), _tile(K, tk)
    nk = K // tk
    grid = (M // tm, N // tn, nk)
    dims = TN if ta else (NT if tb else NN)
    n_ex, n_out = len(extras), len(out_dtypes)
    carried = _Carried(jobs, grid)

    def body(*refs):
        ins, job_ins, out_refs, job_outs, scratch, job_sems = _split_refs(
            refs, 2 + n_ex, len(carried.operands), n_out, len(carried.out_shape), int(nk > 1))
        a_ref, b_ref, ex_refs = ins[0], ins[1], ins[2:]
        k = pl.program_id(2)
        carried.before(job_ins, job_outs, job_sems)

        def finish(acc):
            res = (acc,) if epilogue is None else epilogue(acc, *[e[...] for e in ex_refs])
            for o_ref, r in zip(out_refs, res):
                o_ref[...] = r.astype(o_ref.dtype)

        if nk == 1:
            finish(_dot(a_ref[...], b_ref[...], dims))
        else:
            acc_ref = scratch[0]

            @pl.when(k == 0)
            def _():
                acc_ref[...] = _dot(a_ref[...], b_ref[...], dims)

            @pl.when(k > 0)
            def _():
                acc_ref[...] += _dot(a_ref[...], b_ref[...], dims)

            pl.when(k == nk - 1)(lambda: finish(acc_ref[...]))

        carried.after(job_ins, job_outs, job_sems)

    a_spec = pl.BlockSpec((tk, tm), lambda i, j, k: (k, i)) if ta else pl.BlockSpec((tm, tk), lambda i, j, k: (i, k))
    b_spec = pl.BlockSpec((tn, tk), lambda i, j, k: (j, k)) if tb else pl.BlockSpec((tk, tn), lambda i, j, k: (k, j))
    tile_spec = pl.BlockSpec((tm, tn), lambda i, j, k: (i, j))
    outs = pl.pallas_call(
        body,
        name=name,
        grid=grid,
        in_specs=[a_spec, b_spec] + [tile_spec] * n_ex + carried.in_specs,
        out_specs=[tile_spec] * n_out + carried.out_specs,
        out_shape=[jax.ShapeDtypeStruct((M, N), d) for d in out_dtypes] + carried.out_shape,
        scratch_shapes=[pltpu.VMEM((tm, tn), F32)] * int(nk > 1) + carried.sems,
        input_output_aliases=carried.aliases(2 + n_ex, n_out),
        compiler_params=_params(3),
    )(a, b, *extras, *carried.operands)
    return outs[0] if len(outs) == 1 else outs


def _rowwise(name, fn, row_in, const_in, row_out, acc_out, tm):
    T = row_in[0][0].shape[0]
    n_r, n_c, n_o, n_a = len(row_in), len(const_in), len(row_out), len(acc_out)

    def body(*refs):
        ins = [r[...] for r in refs[:n_r + n_c]]
        out_refs = refs[n_r + n_c:n_r + n_c + n_o]
        acc_refs = refs[n_r + n_c + n_o:]
        res = fn(*ins)
        for o_ref, r in zip(out_refs, res[:n_o]):
            o_ref[...] = r.astype(o_ref.dtype)

        @pl.when(pl.program_id(0) == 0)
        def _():
            for a_ref in acc_refs:
                a_ref[...] = jnp.zeros_like(a_ref)

        for a_ref, r in zip(acc_refs, res[n_o:]):
            a_ref[...] += jnp.sum(r.reshape(tm // SUBLANE, SUBLANE, r.shape[-1]), axis=0)

    in_specs = [pl.BlockSpec((tm, w), functools.partial(lambda i, c: (i, c), c=cb)) for (_, w, cb) in row_in]
    in_specs += [pl.BlockSpec(c.shape, functools.partial(lambda i, nd: (0,) * nd, nd=c.ndim)) for c in const_in]
    out_specs = [pl.BlockSpec((tm, c), lambda i: (i, 0)) for (c, _) in row_out]
    out_specs += [pl.BlockSpec((SUBLANE, c), lambda i: (0, 0)) for c in acc_out]
    out_shape = [jax.ShapeDtypeStruct((T, c), d) for (c, d) in row_out]
    out_shape += [jax.ShapeDtypeStruct((SUBLANE, c), F32) for c in acc_out]
    return pl.pallas_call(
        body, name=name, grid=(T // tm,), in_specs=in_specs, out_specs=out_specs, out_shape=out_shape,
        compiler_params=_params(1),
    )(*[r[0] for r in row_in], *const_in)


def _ln_stats(z):
    mu = jnp.mean(z, axis=-1, keepdims=True)
    var = jnp.mean(jnp.square(z - mu), axis=-1, keepdims=True)
    rstd = lax.rsqrt(var + EPS)
    return (z - mu) * rstd, rstd


def _ln_bwd(dy, xhat, rstd, g):
    dxh = dy * g
    return rstd * (dxh - jnp.mean(dxh, axis=-1, keepdims=True) - xhat * jnp.mean(dxh * xhat, axis=-1, keepdims=True))


def _rms(x, g):
    return x * lax.rsqrt(jnp.mean(jnp.square(x), axis=-1, keepdims=True) + EPS) * g


def _rope128(x, cos2, sin2):
    return x * cos2 + pltpu.roll(x, RET_DK // 2, 1) * sin2


def _unrope128(dy, cos2, sin2):
    return dy * cos2 + pltpu.roll(dy * sin2, RET_DK // 2, 1)


def _rope64(x, cosp, sina, sinb):
    h = MLA_ROPE // 2
    return x * cosp + pltpu.roll(x, h, 1) * sina + pltpu.roll(x, LANE - h, 1) * sinb


def _unrope64(dy, cosp, sina, sinb):
    h = MLA_ROPE // 2
    return dy * cosp + pltpu.roll(dy * sina, LANE - h, 1) + pltpu.roll(dy * sinb, h, 1)


RET_BLOCK = 256
RET_HEADS_PER_STEP = 4


def _ret_tables(H, B):
    log_g = np.log1p(-np.exp2(-5.0 - np.arange(H, dtype=np.float32))).astype(np.float32)
    idx = np.arange(B, dtype=np.float32)
    dist = np.abs(idx[:, None] - idx[None, :])
    visible = (np.arange(B)[None, :] // CHUNK) <= (np.arange(B)[:, None] // CHUNK)
    dmat = np.exp(log_g[:, None, None] * dist).astype(np.float32) * visible[None].astype(np.float32)
    qd = np.exp(log_g[:, None] * (idx + 1.0)).astype(np.float32)
    kd = np.exp(log_g[:, None] * (B - 1 - idx)).astype(np.float32)
    sd = np.exp(log_g * B).astype(np.float32)
    bc = lambda v: np.ascontiguousarray(np.broadcast_to(v[:, :, None], (H, v.shape[1], LANE)))
    sdb = np.ascontiguousarray(np.broadcast_to(sd[:, None, None], (H, SUBLANE, LANE)))
    return jnp.asarray(dmat), jnp.asarray(bc(qd)), jnp.asarray(bc(kd)), jnp.asarray(sdb)


def _ret_specs(H, B, reverse, nb):
    blk = (lambda b: nb - 1 - b) if reverse else (lambda b: b)
    G = min(RET_HEADS_PER_STEP, H)
    W = G * LANE
    col = lambda off: pl.BlockSpec((B, W), functools.partial(lambda h, b, off: (blk(b), off // G + h), off=off))
    row = pl.BlockSpec((B, LANE), lambda h, b: (blk(b), 0))
    per_head = lambda r: pl.BlockSpec((G, r, LANE), lambda h, b: (h, 0, 0))
    dmat = pl.BlockSpec((G, B, B), lambda h, b: (h, 0, 0))
    return G, col, row, per_head, dmat


def _ret_call(name, body, n_in, n_out, operands, in_specs, out_specs, out_shape, grid, G, jobs):
    carried = _Carried(jobs, grid)

    def wrapped(*refs):
        ins, job_ins, outs, job_outs, (state,), job_sems = _split_refs(
            refs, n_in, len(carried.operands), n_out, len(carried.out_shape), 1)
        carried.before(job_ins, job_outs, job_sems)

        @pl.when(pl.program_id(1) == 0)
        def _():
            state[...] = jnp.zeros_like(state)

        body(*ins, *outs, state)
        carried.after(job_ins, job_outs, job_sems)

    return pl.pallas_call(
        wrapped, name=name, grid=grid,
        in_specs=in_specs + carried.in_specs, out_specs=out_specs + carried.out_specs,
        out_shape=out_shape + carried.out_shape,
        scratch_shapes=[pltpu.VMEM((G, RET_DK, RET_DV), F32)] + carried.sems,
        input_output_aliases=carried.aliases(n_in, n_out),
        compiler_params=_params(2),
    )(*operands, *carried.operands)


def _retention_fwd(p1, cos2, sin2, gn_g, H, jobs=()):
    T = p1.shape[0]
    B = min(RET_BLOCK, T)
    nb = T // B
    scale = RET_DK ** -0.5
    dmat, qd, kd, sd = _ret_tables(H, B)
    G, col, row, per_head, dspec = _ret_specs(H, B, False, nb)

    def body(q_ref, k_ref, v_ref, g_ref, cos_ref, sin_ref, d_ref, qd_ref, kd_ref, sd_ref, gn_ref, pre_ref, out_ref, s_ref):
        cos, sin = cos_ref[...], sin_ref[...]
        for g in range(G):
            hl = pl.ds(g * LANE, LANE)
            q = _rope128(q_ref[:, hl], cos, sin)
            k = _rope128(k_ref[:, hl], cos, sin) * scale
            v = v_ref[:, hl]
            s = s_ref[g]
            a = _dot(q, k, NT) * d_ref[g]
            ret = _dot(a, v) + _dot(q * qd_ref[g], s)
            s_ref[g] = sd_ref[g, 0:1, :] * s + _dot(k * kd_ref[g], v, TN)
            pre_ref[:, hl] = ret
            nrm, _ = _ln_stats(ret)
            out_ref[:, hl] = (jax.nn.silu(g_ref[:, hl]) * (nrm * gn_ref[:, hl])).astype(out_ref.dtype)

    out_spec = pl.BlockSpec((B, G * LANE), lambda h, b: (b, h))
    return _ret_call(
        "retention_fwd", body, 11, 2, (p1, p1, p1, p1, cos2, sin2, dmat, qd, kd, sd, gn_g),
        [col(0), col(H), col(2 * H), col(3 * H), row, row, dspec, per_head(B), per_head(B), per_head(SUBLANE),
         pl.BlockSpec((1, G * LANE), lambda h, b: (0, h))],
        [out_spec, out_spec],
        [jax.ShapeDtypeStruct((T, H * RET_DV), F32), jax.ShapeDtypeStruct((T, H * RET_DV), BF16)],
        (H // G, nb), G, jobs)


def _retention_bwd(p1, dpre, cos2, sin2, H, dq_jobs=(), dkv_jobs=()):
    T = p1.shape[0]
    B = min(RET_BLOCK, T)
    nb = T // B
    scale = RET_DK ** -0.5
    dmat, qd, kd, sd = _ret_tables(H, B)

    def dq_body(k_ref, v_ref, do_ref, cos_ref, sin_ref, d_ref, qd_ref, kd_ref, sd_ref, dq_ref, s_ref):
        cos, sin = cos_ref[...], sin_ref[...]
        for g in range(G):
            hl = pl.ds(g * LANE, LANE)
            k = _rope128(k_ref[:, hl], cos, sin) * scale
            v, do, s = v_ref[:, hl], do_ref[:, hl], s_ref[g]
            da = _dot(do, v, NT) * d_ref[g]
            dq = _dot(da, k) + _dot(do, s, NT) * qd_ref[g]
            dq_ref[:, hl] = _unrope128(dq, cos, sin).astype(dq_ref.dtype)
            s_ref[g] = sd_ref[g, 0:1, :] * s + _dot(k * kd_ref[g], v, TN)

    G, col, row, per_head, dspec = _ret_specs(H, B, False, nb)
    dq = _ret_call(
        "retention_bwd_dq", dq_body, 9, 1, (p1, p1, dpre, cos2, sin2, dmat, qd, kd, sd),
        [col(H), col(2 * H), col(0), row, row, dspec, per_head(B), per_head(B), per_head(SUBLANE)],
        [pl.BlockSpec((B, G * LANE), lambda h, b: (b, h))], [jax.ShapeDtypeStruct((T, H * RET_DK), BF16)],
        (H // G, nb), G, dq_jobs)

    def dkv_body(q_ref, k_ref, v_ref, do_ref, cos_ref, sin_ref, d_ref, qd_ref, kd_ref, sd_ref, dk_ref, dv_ref, g_ref):
        cos, sin = cos_ref[...], sin_ref[...]
        for g in range(G):
            hl = pl.ds(g * LANE, LANE)
            q = _rope128(q_ref[:, hl], cos, sin)
            k = _rope128(k_ref[:, hl], cos, sin) * scale
            v, do, st = v_ref[:, hl], do_ref[:, hl], g_ref[g]
            a = _dot(q, k, NT) * d_ref[g]
            da = _dot(do, v, NT) * d_ref[g]
            dv = _dot(a, do, TN) + _dot(k * kd_ref[g], st)
            dk = _dot(da, q, TN) + _dot(v, st, NT) * kd_ref[g]
            dv_ref[:, hl] = dv.astype(dv_ref.dtype)
            dk_ref[:, hl] = _unrope128(dk * scale, cos, sin).astype(dk_ref.dtype)
            g_ref[g] = sd_ref[g, 0:1, :] * st + _dot(q * qd_ref[g], do, TN)

    G, col, row, per_head, dspec = _ret_specs(H, B, True, nb)
    out_spec = pl.BlockSpec((B, G * LANE), lambda h, b: (nb - 1 - b, h))
    dkv = _ret_call(
        "retention_bwd_dkv", dkv_body, 10, 2, (p1, p1, p1, dpre, cos2, sin2, dmat, qd, kd, sd),
        [col(0), col(H), col(2 * H), col(0), row, row, dspec, per_head(B), per_head(B), per_head(SUBLANE)],
        [out_spec, out_spec],
        [jax.ShapeDtypeStruct((T, H * RET_DK), BF16), jax.ShapeDtypeStruct((T, H * RET_DV), BF16)],
        (H // G, nb), G, dkv_jobs)
    return dq, dkv


ATT_BLOCK = 512


ATT_ROW_CHUNKS = 2
ATT_HEADS_PER_STEP = 8
ATT_BWD_HEADS_PER_STEP = 2
LOG2E = 1.4426950408889634


def _chunk_mask(rows, row0, cols):
    qi = (lax.broadcasted_iota(jnp.int32, (rows, cols), 0) + row0) // CHUNK
    kj = lax.broadcasted_iota(jnp.int32, (rows, cols), 1) // CHUNK
    return kj <= qi


def _attention_fwd(qh, kh, vx, H, jobs=()):
    T = qh.shape[0]
    t = min(ATT_BLOCK, T)
    n = T // t
    r = t // ATT_ROW_CHUNKS
    G = min(ATT_HEADS_PER_STEP, H)
    P = MLA_HEAD_PAD
    scale = (MLA_NOPE + MLA_ROPE) ** -0.5
    c2 = scale * LOG2E
    carried = _Carried(jobs, (H // G, n, n))

    def body(*refs):
        (q_ref, k_ref, v_ref), job_ins, (o_ref, ob_ref, lse_ref), job_outs, (m_ref, acc_ref), job_sems = _split_refs(
            refs, 3, len(carried.operands), 3, len(carried.out_shape), 2)
        i, j = pl.program_id(1), pl.program_id(2)
        carried.before(job_ins, job_outs, job_sems)

        @pl.when(j == 0)
        def _():
            m_ref[...] = jnp.full_like(m_ref, -jnp.inf)
            acc_ref[...] = jnp.zeros_like(acc_ref)

        def update(masked):
            for g in range(G):
                hc = pl.ds(g * P, P)
                for ch in range(ATT_ROW_CHUNKS):
                    rs = pl.ds(ch * r, r)
                    s = _dot(q_ref[rs, hc], k_ref[:, hc], NT)
                    if masked:
                        s = jnp.where(_chunk_mask(r, ch * r, t), s, -jnp.inf)
                    m_prev = m_ref[rs, pl.ds(g * LANE, LANE)]
                    m_new = jnp.maximum(m_prev, jnp.max(s, axis=-1, keepdims=True))
                    alpha = jnp.exp2((m_prev - m_new) * c2)
                    p = jnp.exp2((s - jnp.tile(m_new, (1, t // LANE))) * c2)
                    acc_ref[rs, hc] = jnp.tile(alpha, (1, 2)) * acc_ref[rs, hc] + _dot(p, v_ref[:, hc])
                    m_ref[rs, pl.ds(g * LANE, LANE)] = m_new

        pl.when(j < i)(functools.partial(update, False))
        pl.when(j == i)(functools.partial(update, True))

        @pl.when(j == n - 1)
        def _():
            for g in range(G):
                acc = acc_ref[:, pl.ds(g * P, P)]
                l = acc[:, MLA_DV:]
                o = acc[:, :MLA_DV] / l
                o_ref[:, pl.ds(g * MLA_DV, MLA_DV)] = o
                ob_ref[:, pl.ds(g * MLA_DV, MLA_DV)] = o.astype(ob_ref.dtype)
                lse_ref[g] = m_ref[:, pl.ds(g * LANE, LANE)] * scale + jnp.log(l)

        carried.after(job_ins, job_outs, job_sems)

    kclamp = lambda i, j: jnp.minimum(j, i)
    o_spec = pl.BlockSpec((t, G * MLA_DV), lambda h, i, j: (i, h))
    return pl.pallas_call(
        body, name="attention_fwd", grid=(H // G, n, n),
        in_specs=[pl.BlockSpec((t, G * P), lambda h, i, j: (i, h)),
                  pl.BlockSpec((t, G * P), lambda h, i, j: (kclamp(i, j), h)),
                  pl.BlockSpec((t, G * P), lambda h, i, j: (kclamp(i, j), h))] + carried.in_specs,
        out_specs=[o_spec, o_spec, pl.BlockSpec((G, t, LANE), lambda h, i, j: (h, i, 0))] + carried.out_specs,
        out_shape=[jax.ShapeDtypeStruct((T, H * MLA_DV), F32), jax.ShapeDtypeStruct((T, H * MLA_DV), BF16),
                   jax.ShapeDtypeStruct((H, T, LANE), F32)] + carried.out_shape,
        scratch_shapes=[pltpu.VMEM((t, G * LANE), F32), pltpu.VMEM((t, G * P), F32)] + carried.sems,
        input_output_aliases=carried.aliases(3, 3),
        compiler_params=_params(3),
    )(qh, kh, vx, *carried.operands)


def _attention_bwd(qh, kh, vx, o, lse, dmix, H, jobs=()):
    T = qh.shape[0]
    t = min(ATT_BLOCK, T)
    n = T // t
    r = t // ATT_ROW_CHUNKS
    G = min(ATT_BWD_HEADS_PER_STEP, H)
    P = MLA_HEAD_PAD
    scale = (MLA_NOPE + MLA_ROPE) ** -0.5
    c2 = scale * LOG2E
    carried = _Carried(jobs, (H // G, n, n))

    def body(*refs):
        ((q_ref, k_ref, v_ref, do_ref, o_ref, lse_ref), job_ins, (dq_ref, dkn_ref, dkr_ref, dv_ref), job_outs,
         (dk_acc, dv_acc), job_sems) = _split_refs(refs, 6, len(carried.operands), 4, len(carried.out_shape), 2)
        j, i = pl.program_id(1), pl.program_id(2)
        carried.before(job_ins, job_outs, job_sems)

        @pl.when(jnp.logical_and(j == 0, i == 0))
        def _():
            dq_ref[...] = jnp.zeros_like(dq_ref)

        @pl.when(i == j)
        def _():
            dk_acc[...] = jnp.zeros_like(dk_acc)
            dv_acc[...] = jnp.zeros_like(dv_acc)

        def update(masked):
            for g in range(G):
                hc, hv = pl.ds(g * P, P), pl.ds(g * MLA_DV, MLA_DV)
                k, v = k_ref[:, hc], v_ref[:, pl.ds(g * P, MLA_DV)]
                for ch in range(ATT_ROW_CHUNKS):
                    rs = pl.ds(ch * r, r)
                    q, do = q_ref[rs, hc], do_ref[rs, hv]
                    s = _dot(q, k, NT)
                    if masked:
                        s = jnp.where(_chunk_mask(r, ch * r, t), s, -jnp.inf)
                    p = jnp.exp2(s * c2 - jnp.tile(lse_ref[g, rs, :] * LOG2E, (1, t // LANE)))
                    dv_acc[:, hv] += _dot(p, do, TN)
                    dp = _dot(do, v, NT)
                    delta = jnp.sum(do * o_ref[rs, hv], axis=-1, keepdims=True)
                    ds = p * (dp - delta)
                    rows = pl.ds(pl.multiple_of(i * t, t) + ch * r, r)
                    dq_ref[rows, hc] += _dot(ds, k)
                    dk_acc[:, hc] += _dot(ds, q, TN)

        pl.when(i > j)(functools.partial(update, False))
        pl.when(i == j)(functools.partial(update, True))

        @pl.when(i == n - 1)
        def _():
            for g in range(G):
                dk = dk_acc[:, pl.ds(g * P, P)] * scale
                dkn_ref[:, pl.ds(g * LANE, LANE)] = dk[:, :MLA_NOPE].astype(dkn_ref.dtype)
                dkr_ref[:, pl.ds(g * LANE, LANE)] = dk[:, MLA_NOPE:]
            dv_ref[...] = dv_acc[...].astype(dv_ref.dtype)

        carried.after(job_ins, job_outs, job_sems)

    qclamp = lambda j, i: jnp.maximum(i, j)
    kv_out = pl.BlockSpec((t, G * LANE), lambda h, j, i: (j, h))
    q_in = pl.BlockSpec((t, G * MLA_DV), lambda h, j, i: (qclamp(j, i), h))
    return pl.pallas_call(
        body, name="attention_bwd", grid=(H // G, n, n),
        in_specs=[pl.BlockSpec((t, G * P), lambda h, j, i: (qclamp(j, i), h)),
                  pl.BlockSpec((t, G * P), lambda h, j, i: (j, h)),
                  pl.BlockSpec((t, G * P), lambda h, j, i: (j, h)),
                  pl.BlockSpec((t, G * MLA_DV), lambda h, j, i: (qclamp(j, i), H // G + h)),
                  q_in,
                  pl.BlockSpec((G, t, LANE), lambda h, j, i: (h, qclamp(j, i), 0))] + carried.in_specs,
        out_specs=[pl.BlockSpec((T, G * P), lambda h, j, i: (0, h)), kv_out, kv_out, kv_out] + carried.out_specs,
        out_shape=[jax.ShapeDtypeStruct((T, H * P), F32), jax.ShapeDtypeStruct((T, H * LANE), BF16),
                   jax.ShapeDtypeStruct((T, H * LANE), F32), jax.ShapeDtypeStruct((T, H * LANE), BF16)] + carried.out_shape,
        scratch_shapes=[pltpu.VMEM((t, G * P), F32), pltpu.VMEM((t, G * MLA_DV), F32)] + carried.sems,
        input_output_aliases=carried.aliases(6, 4),
        compiler_params=_params(3),
    )(qh, kh, vx, dmix, o, lse, *carried.operands)


def _adamw_math(w, g, m, v):
    m = ADAM_B1 * m + (1.0 - ADAM_B1) * g
    v = ADAM_B2 * v + (1.0 - ADAM_B2) * jnp.square(g)
    m_hat = m / (1.0 - ADAM_B1 ** ADAM_STEP)
    v_hat = v / (1.0 - ADAM_B2 ** ADAM_STEP)
    delta = -ADAM_LR * (m_hat / (jnp.sqrt(v_hat) + ADAM_EPS) + ADAM_WD * w)
    return delta, m, v


def _adamw(name, landed, w, m, v):
    R, C = w.shape
    tr = 16
    while tr * 2 * C <= ADAMW_TILE_ELEMS and R % (tr * 2) == 0:
        tr *= 2
    tr = tr if R % tr == 0 else R
    nl = len(landed)
    assert sum(l.shape[0] for l in landed) in (N_CHIP, N_DEV)

    def body(*refs):
        l_refs, (w_ref, m_ref, v_ref, g_out, d_out, m_out, v_out) = refs[:nl], refs[nl:]
        g = None
        for l_ref in l_refs:
            for s in range(l_ref.shape[0]):
                part = l_ref[s].astype(F32)
                g = part if g is None else g + part
        delta, mn, vn = _adamw_math(w_ref[...], g, m_ref[...], v_ref[...])
        g_out[...] = g
        d_out[...] = delta
        m_out[...] = mn
        v_out[...] = vn

    spec = pl.BlockSpec((tr, C), lambda i: (i, 0))
    l_specs = [pl.BlockSpec((l.shape[0], tr, C), lambda i: (0, i, 0)) for l in landed]
    return pl.pallas_call(
        body, name=name, grid=(R // tr,), in_specs=l_specs + [spec, spec, spec], out_specs=[spec] * 4,
        out_shape=[jax.ShapeDtypeStruct((R, C), F32)] * 4, compiler_params=_params(1),
    )(*landed, w, m, v)


def _adamw_small(landed, w, m, v):
    P = w.shape[1]

    def body(l_ref, w_ref, m_ref, v_ref, g_out, d_out, m_out, v_out):
        acc = l_ref[0]
        for s in range(1, N_DEV):
            acc = acc + l_ref[s]
        g = jnp.sum(acc, axis=0, keepdims=True)
        delta, mn, vn = _adamw_math(w_ref[...], g, m_ref[...], v_ref[...])
        g_out[...] = g
        d_out[...] = delta
        m_out[...] = mn
        v_out[...] = vn

    return pl.pallas_call(
        body, name="adamw_replicated", out_shape=[jax.ShapeDtypeStruct((1, P), F32)] * 4,
    )(landed, w, m, v)


def _rope_tables(pos, T):
    p = pos.astype(F32)[:, None]
    inv_r = ROPE_BASE ** (-jnp.arange(0, RET_DK, 2, dtype=F32) / RET_DK)
    ang = p * inv_r
    cos, sin = jnp.cos(ang), jnp.sin(ang)
    cos2 = jnp.concatenate([cos, cos], axis=-1)
    sin2 = jnp.concatenate([-sin, sin], axis=-1)
    inv_m = ROPE_BASE ** (-jnp.arange(0, MLA_ROPE, 2, dtype=F32) / MLA_ROPE)
    ang = p * inv_m
    cos, sin = jnp.cos(ang), jnp.sin(ang)
    zh = jnp.zeros_like(cos)
    z2 = jnp.zeros((T, LANE - MLA_ROPE), F32)
    cosp = jnp.concatenate([cos, cos, z2], axis=-1)
    sina = jnp.concatenate([zh, sin, z2], axis=-1)
    sinb = jnp.concatenate([-sin, zh, z2], axis=-1)
    return cos2, sin2, cosp, sina, sinb


def kernel(x, positions, w_in, q_norm_g, w_uq, kv_norm_g, w_uk, w_uv, ret_gn_g, w_out, ln1_g, ln1_b, w_up, w_down, ln2_g, ln2_b, loss_target, m_w_in, m_q_norm_g, m_w_uq, m_kv_norm_g, m_w_uk, m_w_uv, m_ret_gn_g, m_w_out, m_ln1_g, m_ln1_b, m_w_up, m_w_down, m_ln2_g, m_ln2_b, v_w_in, v_q_norm_g, v_w_uq, v_kv_norm_g, v_w_uk, v_w_uv, v_ret_gn_g, v_w_out, v_ln1_g, v_ln1_b, v_w_up, v_w_down, v_ln2_g, v_ln2_b):
    T, D = x.shape[1], x.shape[2]
    H = D // 256
    RW = H * RET_DV
    x = x[0]
    tgt = loss_target[0]
    alpha = DEEPNORM_ALPHA
    in_width = w_in.shape[2] * N_DEV
    mla_in = in_width - 4 * RW
    mla_in_pad = -(-mla_in // 512) * 512
    f_uq = MLA_NOPE + MLA_ROPE

    names = ["w_in", "w_uq", "w_uk", "w_uv", "w_out", "w_up", "w_down"]
    big = dict(zip(names, [w_in[0], w_uq[0], w_uk[0], w_uv[0], w_out[0], w_up[0], w_down[0]]))
    kind = dict(zip(names, ["blk", "blk", "cols", "cols", "rows", "cols", "rows"]))
    half = {n: w.astype(BF16) for n, w in big.items()}

    def gather(ns, mid, **parts):
        return _GatherJob([half[n] for n in ns], [kind[n] for n in ns], mid, **parts)

    (g_in,) = _standalone("gather_w_in", gather(["w_in"], 0.0))
    win = jnp.transpose(g_in, (1, 0, 2)).reshape(D, in_width)
    w1 = win[:, :4 * RW]
    w2 = jnp.pad(win[:, 4 * RW:], ((0, 0), (0, mla_in_pad - mla_in)))

    cos2, sin2, cosp, sina, sinb = _rope_tables(positions[0], T)
    xb = x.astype(BF16)
    tr = min(256, T)

    p1, g_uq, wuk, wuv, wout = _matmul("proj_ret", xb, w1,
                                       jobs=[gather(["w_uq", "w_uk", "w_uv", "w_out"], 0.65)])
    wuq = jnp.transpose(g_uq, (1, 0, 2)).reshape(MLA_Q_RANK, H, f_uq)
    wuq = jnp.pad(wuq, ((0, 0), (0, 0), (0, MLA_HEAD_PAD - f_uq))).reshape(MLA_Q_RANK, H * MLA_HEAD_PAD)
    wukv = jnp.concatenate([wuk, wuv], axis=1)
    p2 = _matmul("proj_mla", xb, w2)
    c_kv, c_kr = MLA_Q_RANK, MLA_Q_RANK + MLA_KV_RANK

    def mla_prep(p, cp, sa, sb, gq, gkv):
        cqn = _rms(p[:, :c_kv], gq)
        ckvn = _rms(p[:, c_kv:c_kr], gkv)
        return cqn, ckvn, _rope64(p[:, c_kr:c_kr + LANE], cp, sa, sb)

    cqn, ckvn, krope = _rowwise(
        "mla_prep", mla_prep, [(p2, mla_in_pad, 0), (cosp, LANE, 0), (sina, LANE, 0), (sinb, LANE, 0)],
        [q_norm_g, kv_norm_g], [(MLA_Q_RANK, BF16), (MLA_KV_RANK, BF16), (LANE, BF16)], [], tr)
    qf = _matmul("mla_q", cqn, wuq)
    kv = _matmul("mla_kv", ckvn, wukv, out_dtypes=(BF16,))

    def mla_heads(q, kn, vv, kr, cp, sa, sb):
        qs, ks, vs = [], [], []
        ones = jnp.ones((q.shape[0], MLA_DV), vv.dtype)
        for h in range(H):
            o = h * MLA_HEAD_PAD
            qs += [q[:, o:o + MLA_NOPE], _rope64(q[:, o + MLA_NOPE:o + MLA_HEAD_PAD], cp, sa, sb)]
            ks += [kn[:, h * MLA_NOPE:(h + 1) * MLA_NOPE], kr]
            vs += [vv[:, h * MLA_DV:(h + 1) * MLA_DV], ones]
        return jnp.concatenate(qs, axis=1), jnp.concatenate(ks, axis=1), jnp.concatenate(vs, axis=1)

    qh, kh, vx = _rowwise(
        "mla_heads", mla_heads,
        [(qf, H * MLA_HEAD_PAD, 0), (kv, RW, 0), (kv, RW, 1), (krope, LANE, 0), (cosp, LANE, 0), (sina, LANE, 0), (sinb, LANE, 0)],
        [], [(H * MLA_HEAD_PAD, BF16)] * 3, [], tr)
    att_o, att_ob, lse, wup_near = _attention_fwd(qh, kh, vx, H, jobs=[gather(["w_up"], 0.6, chips=(CHIP_X, CHIP_Y))])
    ret_pre, ret_ob, wup = _retention_fwd(
        p1, cos2, sin2, ret_gn_g, H, jobs=[gather(["w_up"], 0.7, chips=(CHIP_DIAGONAL,), own=False, base=[wup_near])])
    mixin = jnp.concatenate([ret_ob, att_ob], axis=1)
    mix = _matmul("mix_out", mixin, wout)

    def ln1_fwd(xt, mt, g, b):
        z = alpha * xt + mt
        xhat, _ = _ln_stats(z)
        y = xhat * g + b
        return z, y, y

    z1, x1, x1b = _rowwise("ln1", ln1_fwd, [(x, D, 0), (mix, D, 0)], [ln1_g, ln1_b],
                           [(D, F32), (D, F32), (D, BF16)], [], tr)

    def relu2(acc):
        r = jnp.maximum(acc, 0.0)
        return r * r, r

    a2b, rb, wdown = _matmul("mlp_up", x1b, wup, out_dtypes=(BF16, BF16), epilogue=relu2, jobs=[gather(["w_down"], 0.8)])
    hmlp = _matmul("mlp_down", a2b, wdown)

    def ln2_loss(x1t, ht, tt, g, b):
        xhat, rstd = _ln_stats(alpha * x1t + ht)
        err = xhat * g + b - tt
        dy = err / D
        dz = _ln_bwd(dy, xhat, rstd, g)
        lrow = 0.5 * jnp.mean(jnp.square(err), axis=-1, keepdims=True)
        return dz, dz, dy * xhat, dy, jnp.broadcast_to(lrow, (lrow.shape[0], LANE))

    dz2, dz2b, dg2, db2, lpart = _rowwise("ln2_loss", ln2_loss, [(x1, D, 0), (hmlp, D, 0), (tgt, D, 0)], [ln2_g, ln2_b],
                                          [(D, F32), (D, BF16)], [D, D, LANE], tr)
    loss = lax.psum(jnp.sum(lpart[:, 0]), ("x", "y", "c"))

    def pair(grads):
        ns = list(grads)
        return _PairJob([grads[n] for n in ns], [kind[n] for n in ns], [big[n].shape for n in ns])

    def pair_sums(ns, results):
        return {n: _pair_sum("pair_sum_" + n, results[i], results[len(ns) + i]) for i, n in enumerate(ns)}

    def chip(sums, by_source=None):
        ns = list(sums)
        return _ChipJob([sums[n] for n in ns], [big[n].shape for n in ns], by_source)

    landed = {}
    da = _matmul("mlp_down_dx", dz2b, wdown, tb=True, out_dtypes=(BF16,), extras=(rb,),
                 epilogue=lambda acc, r: (acc * (2.0 * r.astype(F32)),))
    g_wdown = _matmul("mlp_down_dw", a2b, dz2b, ta=True, out_dtypes=(BF16,))
    g_wup, *pr = _matmul("mlp_up_dw", x1b, da, ta=True, out_dtypes=(BF16,), jobs=[pair({"w_down": g_wdown})])
    sums_down = pair_sums(["w_down"], pr)
    dx1m, l_a, *pr = _matmul("mlp_up_dx", da, wup, tb=True, jobs=[chip(sums_down), pair({"w_up": g_wup})])
    landed["w_down"] = [l_a]
    sums_up = pair_sums(["w_up"], pr)

    def ln1_bwd(dm, dzz, z, g):
        xhat, rstd = _ln_stats(z)
        dy = dm + alpha * dzz
        dz = _ln_bwd(dy, xhat, rstd, g)
        return dz, dz, dy * xhat, dy

    dz1, dz1b, dg1, db1 = _rowwise("ln1_bwd", ln1_bwd, [(dx1m, D, 0), (dz2, D, 0), (z1, D, 0)], [ln1_g],
                                   [(D, F32), (D, BF16)], [D, D], tr)
    g_wout = _matmul("mix_out_dw", mixin, dz1b, ta=True, out_dtypes=(BF16,))
    dmix = _matmul("mix_out_dx", dz1b, wout, tb=True)

    dqh, dkn, dkr, dvv, l_a = _attention_bwd(qh, kh, vx, att_o, lse, dmix, H, jobs=[chip(sums_up)])
    landed["w_up"] = [l_a]
    att_scale = (MLA_NOPE + MLA_ROPE) ** -0.5

    def mla_heads_bwd(dq, dkrh, cp, sa, sb):
        parts, acc = [], dkrh[:, :LANE]
        dq = dq * att_scale
        for h in range(H):
            o = h * MLA_HEAD_PAD
            parts += [dq[:, o:o + MLA_NOPE], _unrope64(dq[:, o + MLA_NOPE:o + MLA_HEAD_PAD], cp, sa, sb)]
            if h:
                acc = acc + dkrh[:, h * LANE:(h + 1) * LANE]
        return jnp.concatenate(parts, axis=1), _unrope64(acc, cp, sa, sb)

    dqb, dkr128 = _rowwise(
        "mla_heads_bwd", mla_heads_bwd,
        [(dqh, H * MLA_HEAD_PAD, 0), (dkr, H * LANE, 0), (cosp, LANE, 0), (sina, LANE, 0), (sinb, LANE, 0)],
        [], [(H * MLA_HEAD_PAD, BF16), (LANE, F32)], [], tr)
    dcqn = _matmul("mla_q_dx", dqb, wuq, tb=True)
    g_wuq = _matmul("mla_q_dw", cqn, dqb, ta=True, out_dtypes=(BF16,))
    dkvb = jnp.concatenate([dkn, dvv], axis=1)
    dckvn = _matmul("mla_kv_dx", dkvb, wukv, tb=True)
    g_wukv = _matmul("mla_kv_dw", ckvn, dkvb, ta=True, out_dtypes=(BF16,))

    def rms_bwd(c, dy, g):
        rstd = lax.rsqrt(jnp.mean(jnp.square(c), axis=-1, keepdims=True) + EPS)
        dyg = dy * g
        dc = rstd * (dyg - c * (rstd * rstd) * jnp.mean(dyg * c, axis=-1, keepdims=True))
        return dc, dy * c * rstd

    def mla_prep_bwd(p, dq_, dkv_, dkr_, gq, gkv):
        dcq, dgq_ = rms_bwd(p[:, :c_kv], dq_, gq)
        dckv, dgkv_ = rms_bwd(p[:, c_kv:c_kr], dkv_, gkv)
        pad = jnp.zeros((p.shape[0], mla_in_pad - c_kr - LANE), F32)
        return jnp.concatenate([dcq, dckv, dkr_, pad], axis=1), dgq_, dgkv_

    d_p2, dgq, dgkv = _rowwise(
        "mla_prep_bwd", mla_prep_bwd, [(p2, mla_in_pad, 0), (dcqn, MLA_Q_RANK, 0), (dckvn, MLA_KV_RANK, 0), (dkr128, LANE, 0)],
        [q_norm_g, kv_norm_g], [(mla_in_pad, BF16)], [MLA_Q_RANK, MLA_KV_RANK], tr)

    def gate_bwd(pre, gate, dout, gn):
        dpre, dgate, dgn = [], [], []
        for h in range(H):
            sl = slice(h * RET_DV, (h + 1) * RET_DV)
            nrm, rstd = _ln_stats(pre[:, sl])
            gate_h, do_h, gn_h = gate[:, sl], dout[:, sl], gn[:, sl]
            sg = jax.nn.sigmoid(gate_h)
            silu = gate_h * sg
            dgate.append(do_h * (nrm * gn_h) * (sg * (1.0 + gate_h * (1.0 - sg))))
            dn = do_h * silu
            dgn.append(dn * nrm)
            dpre.append(_ln_bwd(dn, nrm, rstd, gn_h))
        return jnp.concatenate(dpre, axis=1), jnp.concatenate(dgate, axis=1), jnp.concatenate(dgn, axis=1)

    dpre, drg, dgn = _rowwise("ret_gate_bwd", gate_bwd, [(ret_pre, RW, 0), (p1, RW, 3), (dmix, RW, 0)], [ret_gn_g],
                              [(RW, F32), (RW, BF16)], [RW], tr)
    g_uq = g_wuq.reshape(MLA_Q_RANK, H, MLA_HEAD_PAD)[:, :, :f_uq].reshape(MLA_Q_RANK, N_DEV, H * f_uq // N_DEV)
    mid_grads = {"w_uq": jnp.transpose(g_uq, (1, 0, 2)), "w_uk": g_wukv[:, :RW], "w_uv": g_wukv[:, RW:], "w_out": g_wout}
    (drq, *pr), (drk, drv) = _retention_bwd(p1, dpre, cos2, sin2, H, dq_jobs=[pair(mid_grads)])
    sums_mid = pair_sums(list(mid_grads), pr)
    d_p1 = jnp.concatenate([drq, drk, drv, drg], axis=1)

    g_w1, *l_a = _matmul("proj_ret_dw", xb, d_p1, ta=True, out_dtypes=(BF16,), jobs=[chip(sums_mid)])
    for n, a_ in zip(mid_grads, l_a):
        landed[n] = [a_]
    g_w2 = _matmul("proj_mla_dw", xb, d_p2, ta=True, out_dtypes=(BF16,))

    g_win = jnp.concatenate([g_w1, g_w2[:, :mla_in]], axis=1)
    g_win = {"w_in": jnp.transpose(g_win.reshape(D, N_DEV, in_width // N_DEV), (1, 0, 2))}
    small = jnp.concatenate([dgq, dgkv, dgn, dg1, db1, dg2, db2], axis=1)
    dx_a, *pr = _matmul("proj_mla_dx", d_p2, w2, tb=True, extras=(dz1,), epilogue=lambda acc, d: (acc + alpha * d,),
                        jobs=[pair(g_win)])
    sums_in = pair_sums(["w_in"], pr)
    grad_x, l_a, small_landed = _matmul("proj_ret_dx", d_p1, w1, tb=True, extras=(dx_a,), epilogue=lambda acc, d: (acc + d,),
                                        jobs=[chip(sums_in, by_source=small)])
    landed["w_in"] = [l_a]

    moments_m = [m_w_in, m_w_uq, m_w_uk, m_w_uv, m_w_out, m_w_up, m_w_down]
    moments_v = [v_w_in, v_w_uq, v_w_uk, v_w_uv, v_w_out, v_w_up, v_w_down]
    res = {}
    for n, m, v in zip(names, moments_m, moments_v):
        res[n] = [r[None] for r in _adamw("adamw_" + n, landed[n], big[n], m[0], v[0])]

    small_names = ["q_norm_g", "kv_norm_g", "ret_gn_g", "ln1_g", "ln1_b", "ln2_g", "ln2_b"]
    small_w = [q_norm_g, kv_norm_g, ret_gn_g, ln1_g, ln1_b, ln2_g, ln2_b]
    small_m = [m_q_norm_g, m_kv_norm_g, m_ret_gn_g, m_ln1_g, m_ln1_b, m_ln2_g, m_ln2_b]
    small_v = [v_q_norm_g, v_kv_norm_g, v_ret_gn_g, v_ln1_g, v_ln1_b, v_ln2_g, v_ln2_b]
    cat = lambda arrs: jnp.concatenate(arrs, axis=1)
    sres = _adamw_small(small_landed, cat(small_w), cat(small_m), cat(small_v))
    off = 0
    for n, w in zip(small_names, small_w):
        res[n] = [r[:, off:off + w.shape[1]] for r in sres]
        off += w.shape[1]

    order = ["w_in", "q_norm_g", "w_uq", "kv_norm_g", "w_uk", "w_uv", "ret_gn_g", "w_out", "ln1_g", "ln1_b",
             "w_up", "w_down", "ln2_g", "ln2_b"]
    outs = [loss, grad_x[None]]
    for k in range(4):
        outs += [res[n][k] for n in order]
    return tuple(outs)
```

```python
import functools

import numpy as np

import jax
import jax.numpy as jnp
from jax import lax
from jax.experimental import pallas as pl
from jax.experimental.pallas import tpu as pltpu

F32 = jnp.float32
BF16 = jnp.bfloat16

CHUNK = 64
RET_DK = 128
RET_DV = 128
MLA_NOPE = 128
MLA_ROPE = 64
MLA_DV = 128
MLA_Q_RANK = 768
MLA_KV_RANK = 512
MLA_HEAD_PAD = 256
ROPE_BASE = 10000.0
EPS = 1e-5
DEPTH = 1
DEEPNORM_ALPHA = (2.0 * DEPTH) ** 0.25
ADAM_LR = 0.001
ADAM_B1 = 0.9
ADAM_B2 = 0.999
ADAM_EPS = 1e-08
ADAM_WD = 0.01
ADAM_STEP = 10

N_DEV = 8
LANE = 128
SUBLANE = 8
VMEM_LIMIT = 48 * 1024 * 1024
ADAMW_TILE_ELEMS = 128 * 1024

MESH = pl.DeviceIdType.MESH
ANY = pl.BlockSpec(memory_space=pl.ANY)

NN = (((1,), (0,)), ((), ()))
NT = (((1,), (1,)), ((), ()))
TN = (((0,), (0,)), ((), ()))


def _dot(a, b, dims=NN):
    return lax.dot_general(a.astype(BF16), b.astype(BF16), dims, preferred_element_type=F32)


def _params(n_axes):
    return pltpu.CompilerParams(dimension_semantics=("arbitrary",) * n_axes, vmem_limit_bytes=VMEM_LIMIT)


def _tile(n, want):
    if n <= want:
        return n
    t = want
    while t >= LANE:
        if n % t == 0:
            return t
        t -= LANE
    return n


def _my_place():
    return lax.axis_index("x"), lax.axis_index("y"), lax.axis_index("c")


def _flat(px, py, pc):
    return 4 * px + 2 * py + pc


def _block_of(ref, kind, idx, rows, cols):
    if kind == "blk":
        return ref.at[idx]
    if kind == "rows":
        return ref.at[pl.ds(pl.multiple_of(idx * rows, rows), rows), :]
    return ref.at[:, pl.ds(pl.multiple_of(idx * cols, cols), cols)]


CHIP_X, CHIP_Y, CHIP_DIAGONAL = 0, 1, 2


class _GatherJob:
    def __init__(self, shards, kinds, mid, chips=(CHIP_X, CHIP_Y, CHIP_DIAGONAL), own=True, base=None):
        nw = len(shards)
        self.nw, self.kinds, self.chips, self.own = nw, list(kinds), tuple(chips), own
        self.shapes = [s.shape for s in shards]
        self.ins = list(shards) + (list(base) if base is not None else [])
        self.alias = [(nw + w, w) for w in range(nw)] if base is not None else []
        self.out_shape = []
        for s, kind in zip(shards, kinds):
            r, c = s.shape
            shape = {"blk": (N_DEV, r, c), "rows": (N_DEV * r, c), "cols": (r, N_DEV * c)}[kind]
            self.out_shape.append(jax.ShapeDtypeStruct(shape, s.dtype))
        self.sems = [pltpu.SemaphoreType.DMA((7 * nw,)), pltpu.SemaphoreType.DMA((7 * nw,)), pltpu.SemaphoreType.DMA((nw,))]
        self.phases = [(0.0, self._start), (mid, self._forward), (1.0, self._finish)]

    def _ctx(self, src, dst, send_sems, recv_sems):
        x, y, c = _my_place()
        all_chips = [(1 - x, y), (x, 1 - y), (1 - x, 1 - y)]
        chips = [(j, all_chips[j]) for j in self.chips]

        def place(w, dev):
            r, cc = self.shapes[w]
            return _block_of(dst[w], self.kinds[w], _flat(*dev), r, cc)

        def copy(w, k, block, to, from_src=False):
            return pltpu.make_async_remote_copy(
                src_ref=src[w] if from_src else place(w, block), dst_ref=place(w, block),
                send_sem=send_sems.at[7 * w + k], recv_sem=recv_sems.at[7 * w + k],
                device_id=to, device_id_type=MESH)

        return (x, y, c), (x, y, 1 - c), chips, place, copy

    def _start(self, src, dst, send_sems, recv_sems, local_sems):
        me, sibling, chips, place, copy = self._ctx(src, dst, send_sems, recv_sems)
        for w in range(self.nw):
            if self.own:
                pltpu.make_async_copy(src[w], place(w, me), local_sems.at[w]).start()
                copy(w, 0, me, sibling, from_src=True).start()
            for j, chip in chips:
                copy(w, 1 + j, me, (*chip, me[2]), from_src=True).start()

    def _forward(self, src, dst, send_sems, recv_sems, local_sems):
        me, sibling, chips, place, copy = self._ctx(src, dst, send_sems, recv_sems)
        for w in range(self.nw):
            for j, chip in chips:
                copy(w, 1 + j, (*chip, me[2]), me).wait_recv()
                copy(w, 4 + j, (*chip, me[2]), sibling).start()

    def _finish(self, src, dst, send_sems, recv_sems, local_sems):
        me, sibling, chips, place, copy = self._ctx(src, dst, send_sems, recv_sems)
        for w in range(self.nw):
            if self.own:
                copy(w, 0, sibling, me).wait_recv()
            for j, chip in chips:
                copy(w, 4 + j, (*chip, 1 - me[2]), me).wait_recv()
        for w in range(self.nw):
            if self.own:
                copy(w, 0, me, sibling, from_src=True).wait_send()
                pltpu.make_async_copy(src[w], place(w, me), local_sems.at[w]).wait()
            for j, chip in chips:
                copy(w, 1 + j, me, (*chip, me[2]), from_src=True).wait_send()
                copy(w, 4 + j, (*chip, me[2]), sibling).wait_send()


REL_ALL = [(1, 0, 0), (1, 0, 1), (0, 1, 0), (0, 1, 1), (1, 1, 0), (1, 1, 1), (0, 0, 1)]
N_CHIP = 4


class _PairJob:
    def __init__(self, grads, kinds, shard_shapes):
        self.ins = list(grads)
        self.kinds, self.shard_shapes, self.nw = list(kinds), list(shard_shapes), len(grads)
        self.out_shape = [jax.ShapeDtypeStruct((N_CHIP,) + tuple(s), g.dtype) for g, s in zip(grads, shard_shapes)]
        n = N_CHIP * self.nw
        self.sems = [pltpu.SemaphoreType.DMA((n,)), pltpu.SemaphoreType.DMA((n,))]
        self.phases = [(0.0, self._start), (1.0, self._finish)]

    def _copies(self, src, theirs, send_sems, recv_sems):
        x, y, c = _my_place()
        remote = []
        for w in range(self.nw):
            r, cc = self.shard_shapes[w]
            for q in range(N_CHIP):
                s = N_CHIP * w + q
                remote.append(pltpu.make_async_remote_copy(
                    src_ref=_block_of(src[w], self.kinds[w], 2 * q + (1 - c), r, cc), dst_ref=theirs[w].at[q],
                    send_sem=send_sems.at[s], recv_sem=recv_sems.at[s], device_id=(x, y, 1 - c), device_id_type=MESH))
        return remote

    def _start(self, *refs):
        for cp in self._copies(*refs):
            cp.start()

    def _finish(self, *refs):
        for cp in self._copies(*refs):
            cp.wait()


CHIP_RELS = [(1, 0), (0, 1), (1, 1)]


class _ChipJob:
    def __init__(self, chip_sums, shard_shapes, rels=CHIP_RELS, by_source=None):
        self.ins = list(chip_sums) + ([by_source] if by_source is not None else [])
        self.nw, self.has_src, self.rels = len(chip_sums), by_source is not None, list(rels)
        self.out_shape = [jax.ShapeDtypeStruct((len(rels),) + tuple(s), a.dtype) for a, s in zip(chip_sums, shard_shapes)]
        if self.has_src:
            self.out_shape.append(jax.ShapeDtypeStruct((N_DEV,) + by_source.shape, by_source.dtype))
        n = len(rels) * self.nw + len(REL_ALL) * int(self.has_src)
        self.sems = [pltpu.SemaphoreType.DMA((n,)), pltpu.SemaphoreType.DMA((n,)), pltpu.SemaphoreType.DMA((1,))]
        self.phases = [(0.0, self._start), (1.0, self._finish)]

    def _copies(self, src, land, send_sems, recv_sems, local_sems):
        x, y, c = _my_place()
        nw = self.nw

        def remote_copy(s, src_ref, dst_ref, peer):
            return pltpu.make_async_remote_copy(src_ref=src_ref, dst_ref=dst_ref, send_sem=send_sems.at[s],
                                                recv_sem=recv_sems.at[s], device_id=peer, device_id_type=MESH)

        local, remote = [], []
        for k, (fx, fy) in enumerate(self.rels):
            px, py = x ^ fx, y ^ fy
            remote += [remote_copy(nw * k + w, src[w].at[2 * px + py], land[w].at[k], (px, py, c)) for w in range(nw)]
        if self.has_src:
            my_idx = _flat(x, y, c)
            local.append(pltpu.make_async_copy(src[nw], land[nw].at[my_idx], local_sems.at[0]))
            remote += [remote_copy(nw * len(self.rels) + k, src[nw], land[nw].at[my_idx], (x ^ fx, y ^ fy, c ^ fc))
                       for k, (fx, fy, fc) in enumerate(REL_ALL)]
        return local, remote

    def _start(self, *refs):
        local, remote = self._copies(*refs)
        for cp in local + remote:
            cp.start()

    def _finish(self, *refs):
        local, remote = self._copies(*refs)
        for cp in remote + local:
            cp.wait()


def _row_tile(R, C):
    tr = 16
    while tr * 2 * C <= ADAMW_TILE_ELEMS and R % (tr * 2) == 0:
        tr *= 2
    return tr if R % tr == 0 else R


def _part_spec(kind, tr, C, R, n_lead):
    def dev(ids, pref):
        return 2 * ids[0] + pref[0] if n_lead == 2 else pref[0]

    def index(*args):
        ids, pref = args[:n_lead], args[n_lead]
        i = ids[-1]
        if kind == "blk":
            return dev(ids, pref), i, 0
        if kind == "rows":
            return dev(ids, pref) * (R // tr) + i, 0
        return i, dev(ids, pref)

    return pl.BlockSpec((None, tr, C) if kind == "blk" else (tr, C), index)


def _pair_sum(name, grad, kind, shard_shape, theirs, c):
    R, C = shard_shape
    tr = _row_tile(R, C)

    def body(c_ref, g_ref, t_ref, o_ref):
        o_ref[...] = (g_ref[...].astype(F32) + t_ref[...].astype(F32)).astype(o_ref.dtype)

    slot = pl.BlockSpec((None, tr, C), lambda q, i, c_ref: (q, i, 0))
    return pl.pallas_call(
        body, name=name,
        grid_spec=pltpu.PrefetchScalarGridSpec(
            num_scalar_prefetch=1, grid=(N_CHIP, R // tr),
            in_specs=[_part_spec(kind, tr, C, R, 2), slot], out_specs=slot),
        out_shape=jax.ShapeDtypeStruct(theirs.shape, theirs.dtype), compiler_params=_params(2))(c, grad, theirs)


class _Carried:
    def __init__(self, jobs, grid):
        self.jobs = list(jobs)
        self.steps = int(np.prod(grid))
        self.grid = tuple(grid)
        self.operands = [a for j in self.jobs for a in j.ins]
        self.out_shape = [o for j in self.jobs for o in j.out_shape]
        self.sems = [s for j in self.jobs for s in j.sems]
        self.in_specs = [ANY] * len(self.operands)
        self.out_specs = [ANY] * len(self.out_shape)

    def aliases(self, n_in, n_out):
        out, i0, o0 = {}, n_in, n_out
        for job in self.jobs:
            for ji, jo in getattr(job, "alias", []):
                out[i0 + ji] = o0 + jo
            i0, o0 = i0 + len(job.ins), o0 + len(job.out_shape)
        return out

    def _step(self):
        lin = pl.program_id(0)
        for a in range(1, len(self.grid)):
            lin = lin * self.grid[a] + pl.program_id(a)
        return lin

    def _run(self, in_refs, out_refs, sem_refs, last):
        lin = self._step()
        for job in self.jobs:
            ins, in_refs = in_refs[:len(job.ins)], in_refs[len(job.ins):]
            outs, out_refs = out_refs[:len(job.out_shape)], out_refs[len(job.out_shape):]
            sems, sem_refs = sem_refs[:len(job.sems)], sem_refs[len(job.sems):]
            for frac, fn in job.phases:
                if (frac >= 1.0) != last:
                    continue
                at = min(self.steps - 1, int(frac * (self.steps - 1)))
                pl.when(lin == at)(functools.partial(fn, ins, outs, *sems))

    def before(self, in_refs, out_refs, sem_refs):
        self._run(in_refs, out_refs, sem_refs, last=False)

    def after(self, in_refs, out_refs, sem_refs):
        self._run(in_refs, out_refs, sem_refs, last=True)


def _standalone(name, job):
    def body(*refs):
        ni, no = len(job.ins), len(job.out_shape)
        for _, fn in job.phases:
            fn(refs[:ni], refs[ni:ni + no], *refs[ni + no:])

    return pl.pallas_call(body, name=name, in_specs=[ANY] * len(job.ins), out_specs=[ANY] * len(job.out_shape),
                          out_shape=job.out_shape, scratch_shapes=job.sems)(*job.ins)


def _split_refs(refs, n_in, n_job_in, n_out, n_job_out, n_scratch):
    cuts = np.cumsum([0, n_in, n_job_in, n_out, n_job_out, n_scratch])
    return [refs[cuts[i]:cuts[i + 1]] for i in range(5)] + [refs[cuts[5]:]]


def _matmul(name, a, b, *, ta=False, tb=False, out_dtypes=(F32,), epilogue=None, extras=(), tm=1024, tn=1024, tk=2048, jobs=()):
    if ta:
        K, M = a.shape
    else:
        M, K = a.shape
    N = b.shape[0] if tb else b.shape[1]
    tm, tn, tk = _tile(M, tm), _tile(N, tn), _tile(K, tk)
    nk = K // tk
    grid = (M // tm, N // tn, nk)
    dims = TN if ta else (NT if tb else NN)
    n_ex, n_out = len(extras), len(out_dtypes)
    carried = _Carried(jobs, grid)

    def body(*refs):
        ins, job_ins, out_refs, job_outs, scratch, job_sems = _split_refs(
            refs, 2 + n_ex, len(carried.operands), n_out, len(carried.out_shape), int(nk > 1))
        a_ref, b_ref, ex_refs = ins[0], ins[1], ins[2:]
        k = pl.program_id(2)
        carried.before(job_ins, job_outs, job_sems)

        def finish(acc):
            res = (acc,) if epilogue is None else epilogue(acc, *[e[...] for e in ex_refs])
            for o_ref, r in zip(out_refs, res):
                o_ref[...] = r.astype(o_ref.dtype)

        if nk == 1:
            finish(_dot(a_ref[...], b_ref[...], dims))
        else:
            acc_ref = scratch[0]

            @pl.when(k == 0)
            def _():
                acc_ref[...] = _dot(a_ref[...], b_ref[...], dims)

            @pl.when(k > 0)
            def _():
                acc_ref[...] += _dot(a_ref[...], b_ref[...], dims)

            pl.when(k == nk - 1)(lambda: finish(acc_ref[...]))

        carried.after(job_ins, job_outs, job_sems)

    a_spec = pl.BlockSpec((tk, tm), lambda i, j, k: (k, i)) if ta else pl.BlockSpec((tm, tk), lambda i, j, k: (i, k))
    b_spec = pl.BlockSpec((tn, tk), lambda i, j, k: (j, k)) if tb else pl.BlockSpec((tk, tn), lambda i, j, k: (k, j))
    tile_spec = pl.BlockSpec((tm, tn), lambda i, j, k: (i, j))
    outs = pl.pallas_call(
        body,
        name=name,
        grid=grid,
        in_specs=[a_spec, b_spec] + [tile_spec] * n_ex + carried.in_specs,
        out_specs=[tile_spec] * n_out + carried.out_specs,
        out_shape=[jax.ShapeDtypeStruct((M, N), d) for d in out_dtypes] + carried.out_shape,
        scratch_shapes=[pltpu.VMEM((tm, tn), F32)] * int(nk > 1) + carried.sems,
        input_output_aliases=carried.aliases(2 + n_ex, n_out),
        compiler_params=_params(3),
    )(a, b, *extras, *carried.operands)
    return outs[0] if len(outs) == 1 else outs


def _rowwise(name, fn, row_in, const_in, row_out, acc_out, tm):
    T = row_in[0][0].shape[0]
    n_r, n_c, n_o, n_a = len(row_in), len(const_in), len(row_out), len(acc_out)

    def body(*refs):
        ins = [r[...] for r in refs[:n_r + n_c]]
        out_refs = refs[n_r + n_c:n_r + n_c + n_o]
        acc_refs = refs[n_r + n_c + n_o:]
        res = fn(*ins)
        for o_ref, r in zip(out_refs, res[:n_o]):
            o_ref[...] = r.astype(o_ref.dtype)

        @pl.when(pl.program_id(0) == 0)
        def _():
            for a_ref in acc_refs:
                a_ref[...] = jnp.zeros_like(a_ref)

        for a_ref, r in zip(acc_refs, res[n_o:]):
            a_ref[...] += jnp.sum(r.reshape(tm // SUBLANE, SUBLANE, r.shape[-1]), axis=0)

    in_specs = [pl.BlockSpec((tm, w), functools.partial(lambda i, c: (i, c), c=cb)) for (_, w, cb) in row_in]
    in_specs += [pl.BlockSpec(c.shape, functools.partial(lambda i, nd: (0,) * nd, nd=c.ndim)) for c in const_in]
    out_specs = [pl.BlockSpec((tm, c), lambda i: (i, 0)) for (c, _) in row_out]
    out_specs += [pl.BlockSpec((SUBLANE, c), lambda i: (0, 0)) for c in acc_out]
    out_shape = [jax.ShapeDtypeStruct((T, c), d) for (c, d) in row_out]
    out_shape += [jax.ShapeDtypeStruct((SUBLANE, c), F32) for c in acc_out]
    return pl.pallas_call(
        body, name=name, grid=(T // tm,), in_specs=in_specs, out_specs=out_specs, out_shape=out_shape,
        compiler_params=_params(1),
    )(*[r[0] for r in row_in], *const_in)


def _ln_stats(z):
    mu = jnp.mean(z, axis=-1, keepdims=True)
    var = jnp.mean(jnp.square(z - mu), axis=-1, keepdims=True)
    rstd = lax.rsqrt(var + EPS)
    return (z - mu) * rstd, rstd


def _ln_bwd(dy, xhat, rstd, g):
    dxh = dy * g
    return rstd * (dxh - jnp.mean(dxh, axis=-1, keepdims=True) - xhat * jnp.mean(dxh * xhat, axis=-1, keepdims=True))


def _rms(x, g):
    return x * lax.rsqrt(jnp.mean(jnp.square(x), axis=-1, keepdims=True) + EPS) * g


def _rope128(x, cos2, sin2):
    return x * cos2 + pltpu.roll(x, RET_DK // 2, 1) * sin2


def _unrope128(dy, cos2, sin2):
    return dy * cos2 + pltpu.roll(dy * sin2, RET_DK // 2, 1)


def _rope64(x, cosp, sina, sinb):
    h = MLA_ROPE // 2
    return x * cosp + pltpu.roll(x, h, 1) * sina + pltpu.roll(x, LANE - h, 1) * sinb


def _unrope64(dy, cosp, sina, sinb):
    h = MLA_ROPE // 2
    return dy * cosp + pltpu.roll(dy * sina, LANE - h, 1) + pltpu.roll(dy * sinb, h, 1)


RET_BLOCK = 256
RET_HEADS_PER_STEP = 4


def _ret_tables(H, B):
    log_g = np.log1p(-np.exp2(-5.0 - np.arange(H, dtype=np.float32))).astype(np.float32)
    idx = np.arange(B, dtype=np.float32)
    dist = np.abs(idx[:, None] - idx[None, :])
    visible = (np.arange(B)[None, :] // CHUNK) <= (np.arange(B)[:, None] // CHUNK)
    dmat = np.exp(log_g[:, None, None] * dist).astype(np.float32) * visible[None].astype(np.float32)
    qd = np.exp(log_g[:, None] * (idx + 1.0)).astype(np.float32)
    kd = np.exp(log_g[:, None] * (B - 1 - idx)).astype(np.float32)
    sd = np.exp(log_g * B).astype(np.float32)
    bc = lambda v: np.ascontiguousarray(np.broadcast_to(v[:, :, None], (H, v.shape[1], LANE)))
    sdb = np.ascontiguousarray(np.broadcast_to(sd[:, None, None], (H, SUBLANE, LANE)))
    return jnp.asarray(dmat), jnp.asarray(bc(qd)), jnp.asarray(bc(kd)), jnp.asarray(sdb)


def _ret_specs(H, B, reverse, nb):
    blk = (lambda b: nb - 1 - b) if reverse else (lambda b: b)
    G = min(RET_HEADS_PER_STEP, H)
    W = G * LANE
    col = lambda off: pl.BlockSpec((B, W), functools.partial(lambda h, b, off: (blk(b), off // G + h), off=off))
    row = pl.BlockSpec((B, LANE), lambda h, b: (blk(b), 0))
    per_head = lambda r: pl.BlockSpec((G, r, LANE), lambda h, b: (h, 0, 0))
    dmat = pl.BlockSpec((G, B, B), lambda h, b: (h, 0, 0))
    return G, col, row, per_head, dmat


def _ret_call(name, body, n_in, n_out, operands, in_specs, out_specs, out_shape, grid, G, jobs):
    carried = _Carried(jobs, grid)

    def wrapped(*refs):
        ins, job_ins, outs, job_outs, (state,), job_sems = _split_refs(
            refs, n_in, len(carried.operands), n_out, len(carried.out_shape), 1)
        carried.before(job_ins, job_outs, job_sems)

        @pl.when(pl.program_id(1) == 0)
        def _():
            state[...] = jnp.zeros_like(state)

        body(*ins, *outs, state)
        carried.after(job_ins, job_outs, job_sems)

    return pl.pallas_call(
        wrapped, name=name, grid=grid,
        in_specs=in_specs + carried.in_specs, out_specs=out_specs + carried.out_specs,
        out_shape=out_shape + carried.out_shape,
        scratch_shapes=[pltpu.VMEM((G, RET_DK, RET_DV), F32)] + carried.sems,
        input_output_aliases=carried.aliases(n_in, n_out),
        compiler_params=_params(2),
    )(*operands, *carried.operands)


def _retention_fwd(p1, cos2, sin2, gn_g, H, jobs=()):
    T = p1.shape[0]
    B = min(RET_BLOCK, T)
    nb = T // B
    scale = RET_DK ** -0.5
    dmat, qd, kd, sd = _ret_tables(H, B)
    G, col, row, per_head, dspec = _ret_specs(H, B, False, nb)

    def body(q_ref, k_ref, v_ref, g_ref, cos_ref, sin_ref, d_ref, qd_ref, kd_ref, sd_ref, gn_ref, pre_ref, out_ref, s_ref):
        cos, sin = cos_ref[...], sin_ref[...]
        for g in range(G):
            hl = pl.ds(g * LANE, LANE)
            q = _rope128(q_ref[:, hl], cos, sin)
            k = _rope128(k_ref[:, hl], cos, sin) * scale
            v = v_ref[:, hl]
            s = s_ref[g]
            a = _dot(q, k, NT) * d_ref[g]
            ret = _dot(a, v) + _dot(q * qd_ref[g], s)
            s_ref[g] = sd_ref[g, 0:1, :] * s + _dot(k * kd_ref[g], v, TN)
            pre_ref[:, hl] = ret
            nrm, _ = _ln_stats(ret)
            out_ref[:, hl] = (jax.nn.silu(g_ref[:, hl]) * (nrm * gn_ref[:, hl])).astype(out_ref.dtype)

    out_spec = pl.BlockSpec((B, G * LANE), lambda h, b: (b, h))
    return _ret_call(
        "retention_fwd", body, 11, 2, (p1, p1, p1, p1, cos2, sin2, dmat, qd, kd, sd, gn_g),
        [col(0), col(H), col(2 * H), col(3 * H), row, row, dspec, per_head(B), per_head(B), per_head(SUBLANE),
         pl.BlockSpec((1, G * LANE), lambda h, b: (0, h))],
        [out_spec, out_spec],
        [jax.ShapeDtypeStruct((T, H * RET_DV), F32), jax.ShapeDtypeStruct((T, H * RET_DV), BF16)],
        (H // G, nb), G, jobs)


def _retention_bwd(p1, dpre, cos2, sin2, H, dq_jobs=(), dkv_jobs=()):
    T = p1.shape[0]
    B = min(RET_BLOCK, T)
    nb = T // B
    scale = RET_DK ** -0.5
    dmat, qd, kd, sd = _ret_tables(H, B)

    def dq_body(k_ref, v_ref, do_ref, cos_ref, sin_ref, d_ref, qd_ref, kd_ref, sd_ref, dq_ref, s_ref):
        cos, sin = cos_ref[...], sin_ref[...]
        for g in range(G):
            hl = pl.ds(g * LANE, LANE)
            k = _rope128(k_ref[:, hl], cos, sin) * scale
            v, do, s = v_ref[:, hl], do_ref[:, hl], s_ref[g]
            da = _dot(do, v, NT) * d_ref[g]
            dq = _dot(da, k) + _dot(do, s, NT) * qd_ref[g]
            dq_ref[:, hl] = _unrope128(dq, cos, sin).astype(dq_ref.dtype)
            s_ref[g] = sd_ref[g, 0:1, :] * s + _dot(k * kd_ref[g], v, TN)

    G, col, row, per_head, dspec = _ret_specs(H, B, False, nb)
    dq = _ret_call(
        "retention_bwd_dq", dq_body, 9, 1, (p1, p1, dpre, cos2, sin2, dmat, qd, kd, sd),
        [col(H), col(2 * H), col(0), row, row, dspec, per_head(B), per_head(B), per_head(SUBLANE)],
        [pl.BlockSpec((B, G * LANE), lambda h, b: (b, h))], [jax.ShapeDtypeStruct((T, H * RET_DK), BF16)],
        (H // G, nb), G, dq_jobs)

    def dkv_body(q_ref, k_ref, v_ref, do_ref, cos_ref, sin_ref, d_ref, qd_ref, kd_ref, sd_ref, dk_ref, dv_ref, g_ref):
        cos, sin = cos_ref[...], sin_ref[...]
        for g in range(G):
            hl = pl.ds(g * LANE, LANE)
            q = _rope128(q_ref[:, hl], cos, sin)
            k = _rope128(k_ref[:, hl], cos, sin) * scale
            v, do, st = v_ref[:, hl], do_ref[:, hl], g_ref[g]
            a = _dot(q, k, NT) * d_ref[g]
            da = _dot(do, v, NT) * d_ref[g]
            dv = _dot(a, do, TN) + _dot(k * kd_ref[g], st)
            dk = _dot(da, q, TN) + _dot(v, st, NT) * kd_ref[g]
            dv_ref[:, hl] = dv.astype(dv_ref.dtype)
            dk_ref[:, hl] = _unrope128(dk * scale, cos, sin).astype(dk_ref.dtype)
            g_ref[g] = sd_ref[g, 0:1, :] * st + _dot(q * qd_ref[g], do, TN)

    G, col, row, per_head, dspec = _ret_specs(H, B, True, nb)
    out_spec = pl.BlockSpec((B, G * LANE), lambda h, b: (nb - 1 - b, h))
    dkv = _ret_call(
        "retention_bwd_dkv", dkv_body, 10, 2, (p1, p1, p1, dpre, cos2, sin2, dmat, qd, kd, sd),
        [col(0), col(H), col(2 * H), col(0), row, row, dspec, per_head(B), per_head(B), per_head(SUBLANE)],
        [out_spec, out_spec],
        [jax.ShapeDtypeStruct((T, H * RET_DK), BF16), jax.ShapeDtypeStruct((T, H * RET_DV), BF16)],
        (H // G, nb), G, dkv_jobs)
    return dq, dkv


ATT_BLOCK = 512


ATT_ROW_CHUNKS = 2
ATT_HEADS_PER_STEP = 8
ATT_BWD_HEADS_PER_STEP = 2
LOG2E = 1.4426950408889634


def _chunk_mask(rows, row0, cols):
    qi = (lax.broadcasted_iota(jnp.int32, (rows, cols), 0) + row0) // CHUNK
    kj = lax.broadcasted_iota(jnp.int32, (rows, cols), 1) // CHUNK
    return kj <= qi


def _attention_fwd(qh, kh, vx, H, jobs=()):
    T = qh.shape[0]
    t = min(ATT_BLOCK, T)
    n = T // t
    r = t // ATT_ROW_CHUNKS
    G = min(ATT_HEADS_PER_STEP, H)
    P = MLA_HEAD_PAD
    scale = (MLA_NOPE + MLA_ROPE) ** -0.5
    c2 = scale * LOG2E
    carried = _Carried(jobs, (H // G, n, n))

    def body(*refs):
        (q_ref, k_ref, v_ref), job_ins, (o_ref, ob_ref, lse_ref), job_outs, (m_ref, acc_ref), job_sems = _split_refs(
            refs, 3, len(carried.operands), 3, len(carried.out_shape), 2)
        i, j = pl.program_id(1), pl.program_id(2)
        carried.before(job_ins, job_outs, job_sems)

        @pl.when(j == 0)
        def _():
            m_ref[...] = jnp.full_like(m_ref, -jnp.inf)
            acc_ref[...] = jnp.zeros_like(acc_ref)

        def update(masked):
            for g in range(G):
                hc = pl.ds(g * P, P)
                for ch in range(ATT_ROW_CHUNKS):
                    rs = pl.ds(ch * r, r)
                    s = _dot(q_ref[rs, hc], k_ref[:, hc], NT)
                    if masked:
                        s = jnp.where(_chunk_mask(r, ch * r, t), s, -jnp.inf)
                    m_prev = m_ref[rs, pl.ds(g * LANE, LANE)]
                    m_new = jnp.maximum(m_prev, jnp.max(s, axis=-1, keepdims=True))
                    alpha = jnp.exp2((m_prev - m_new) * c2)
                    p = jnp.exp2((s - jnp.tile(m_new, (1, t // LANE))) * c2)
                    acc_ref[rs, hc] = jnp.tile(alpha, (1, 2)) * acc_ref[rs, hc] + _dot(p, v_ref[:, hc])
                    m_ref[rs, pl.ds(g * LANE, LANE)] = m_new

        pl.when(j < i)(functools.partial(update, False))
        pl.when(j == i)(functools.partial(update, True))

        @pl.when(j == n - 1)
        def _():
            for g in range(G):
                acc = acc_ref[:, pl.ds(g * P, P)]
                l = acc[:, MLA_DV:]
                o = acc[:, :MLA_DV] / l
                o_ref[:, pl.ds(g * MLA_DV, MLA_DV)] = o
                ob_ref[:, pl.ds(g * MLA_DV, MLA_DV)] = o.astype(ob_ref.dtype)
                lse_ref[g] = m_ref[:, pl.ds(g * LANE, LANE)] * scale + jnp.log(l)

        carried.after(job_ins, job_outs, job_sems)

    kclamp = lambda i, j: jnp.minimum(j, i)
    o_spec = pl.BlockSpec((t, G * MLA_DV), lambda h, i, j: (i, h))
    return pl.pallas_call(
        body, name="attention_fwd", grid=(H // G, n, n),
        in_specs=[pl.BlockSpec((t, G * P), lambda h, i, j: (i, h)),
                  pl.BlockSpec((t, G * P), lambda h, i, j: (kclamp(i, j), h)),
                  pl.BlockSpec((t, G * P), lambda h, i, j: (kclamp(i, j), h))] + carried.in_specs,
        out_specs=[o_spec, o_spec, pl.BlockSpec((G, t, LANE), lambda h, i, j: (h, i, 0))] + carried.out_specs,
        out_shape=[jax.ShapeDtypeStruct((T, H * MLA_DV), F32), jax.ShapeDtypeStruct((T, H * MLA_DV), BF16),
                   jax.ShapeDtypeStruct((H, T, LANE), F32)] + carried.out_shape,
        scratch_shapes=[pltpu.VMEM((t, G * LANE), F32), pltpu.VMEM((t, G * P), F32)] + carried.sems,
        input_output_aliases=carried.aliases(3, 3),
        compiler_params=_params(3),
    )(qh, kh, vx, *carried.operands)


def _attention_bwd(qh, kh, vx, o, lse, dmix, H, jobs=()):
    T = qh.shape[0]
    t = min(ATT_BLOCK, T)
    n = T // t
    r = t // ATT_ROW_CHUNKS
    G = min(ATT_BWD_HEADS_PER_STEP, H)
    P = MLA_HEAD_PAD
    scale = (MLA_NOPE + MLA_ROPE) ** -0.5
    c2 = scale * LOG2E
    carried = _Carried(jobs, (H // G, n, n))

    def body(*refs):
        ((q_ref, k_ref, v_ref, do_ref, o_ref, lse_ref), job_ins, (dq_ref, dkn_ref, dkr_ref, dv_ref), job_outs,
         (dk_acc, dv_acc), job_sems) = _split_refs(refs, 6, len(carried.operands), 4, len(carried.out_shape), 2)
        j, i = pl.program_id(1), pl.program_id(2)
        carried.before(job_ins, job_outs, job_sems)

        @pl.when(jnp.logical_and(j == 0, i == 0))
        def _():
            dq_ref[...] = jnp.zeros_like(dq_ref)

        @pl.when(i == j)
        def _():
            dk_acc[...] = jnp.zeros_like(dk_acc)
            dv_acc[...] = jnp.zeros_like(dv_acc)

        def update(masked):
            for g in range(G):
                hc, hv = pl.ds(g * P, P), pl.ds(g * MLA_DV, MLA_DV)
                k, v = k_ref[:, hc], v_ref[:, pl.ds(g * P, MLA_DV)]
                for ch in range(ATT_ROW_CHUNKS):
                    rs = pl.ds(ch * r, r)
                    q, do = q_ref[rs, hc], do_ref[rs, hv]
                    s = _dot(q, k, NT)
                    if masked:
                        s = jnp.where(_chunk_mask(r, ch * r, t), s, -jnp.inf)
                    p = jnp.exp2(s * c2 - jnp.tile(lse_ref[g, rs, :] * LOG2E, (1, t // LANE)))
                    dv_acc[:, hv] += _dot(p, do, TN)
                    dp = _dot(do, v, NT)
                    delta = jnp.sum(do * o_ref[rs, hv], axis=-1, keepdims=True)
                    ds = p * (dp - delta)
                    rows = pl.ds(pl.multiple_of(i * t, t) + ch * r, r)
                    dq_ref[rows, hc] += _dot(ds, k)
                    dk_acc[:, hc] += _dot(ds, q, TN)

        pl.when(i > j)(functools.partial(update, False))
        pl.when(i == j)(functools.partial(update, True))

        @pl.when(i == n - 1)
        def _():
            for g in range(G):
                dk = dk_acc[:, pl.ds(g * P, P)] * scale
                dkn_ref[:, pl.ds(g * LANE, LANE)] = dk[:, :MLA_NOPE].astype(dkn_ref.dtype)
                dkr_ref[:, pl.ds(g * LANE, LANE)] = dk[:, MLA_NOPE:]
            dv_ref[...] = dv_acc[...].astype(dv_ref.dtype)

        carried.after(job_ins, job_outs, job_sems)

    qclamp = lambda j, i: jnp.maximum(i, j)
    kv_out = pl.BlockSpec((t, G * LANE), lambda h, j, i: (j, h))
    q_in = pl.BlockSpec((t, G * MLA_DV), lambda h, j, i: (qclamp(j, i), h))
    return pl.pallas_call(
        body, name="attention_bwd", grid=(H // G, n, n),
        in_specs=[pl.BlockSpec((t, G * P), lambda h, j, i: (qclamp(j, i), h)),
                  pl.BlockSpec((t, G * P), lambda h, j, i: (j, h)),
                  pl.BlockSpec((t, G * P), lambda h, j, i: (j, h)),
                  pl.BlockSpec((t, G * MLA_DV), lambda h, j, i: (qclamp(j, i), H // G + h)),
                  q_in,
                  pl.BlockSpec((G, t, LANE), lambda h, j, i: (h, qclamp(j, i), 0))] + carried.in_specs,
        out_specs=[pl.BlockSpec((T, G * P), lambda h, j, i: (0, h)), kv_out, kv_out, kv_out] + carried.out_specs,
        out_shape=[jax.ShapeDtypeStruct((T, H * P), F32), jax.ShapeDtypeStruct((T, H * LANE), BF16),
                   jax.ShapeDtypeStruct((T, H * LANE), F32), jax.ShapeDtypeStruct((T, H * LANE), BF16)] + carried.out_shape,
        scratch_shapes=[pltpu.VMEM((t, G * P), F32), pltpu.VMEM((t, G * MLA_DV), F32)] + carried.sems,
        input_output_aliases=carried.aliases(6, 4),
        compiler_params=_params(3),
    )(qh, kh, vx, dmix, o, lse, *carried.operands)


def _adamw_math(w, g, m, v):
    m = ADAM_B1 * m + (1.0 - ADAM_B1) * g
    v = ADAM_B2 * v + (1.0 - ADAM_B2) * jnp.square(g)
    m_hat = m / (1.0 - ADAM_B1 ** ADAM_STEP)
    v_hat = v / (1.0 - ADAM_B2 ** ADAM_STEP)
    delta = -ADAM_LR * (m_hat / (jnp.sqrt(v_hat) + ADAM_EPS) + ADAM_WD * w)
    return delta, m, v


def _adamw(name, chip_sums, my_chip, landed, w, m, v):
    R, C = w.shape
    tr = _row_tile(R, C)
    nl = len(landed)
    assert sum(l.shape[0] for l in landed) == N_CHIP - 1

    def body(q_ref, own_ref, *refs):
        l_refs, (w_ref, m_ref, v_ref, g_out, d_out, m_out, v_out) = refs[:nl], refs[nl:]
        g = own_ref[...].astype(F32)
        for l_ref in l_refs:
            for s in range(l_ref.shape[0]):
                g = g + l_ref[s].astype(F32)
        delta, mn, vn = _adamw_math(w_ref[...], g, m_ref[...], v_ref[...])
        g_out[...] = g
        d_out[...] = delta
        m_out[...] = mn
        v_out[...] = vn

    spec = pl.BlockSpec((tr, C), lambda i, q_ref: (i, 0))
    own_spec = pl.BlockSpec((None, tr, C), lambda i, q_ref: (q_ref[0], i, 0))
    l_specs = [pl.BlockSpec((l.shape[0], tr, C), lambda i, q_ref: (0, i, 0)) for l in landed]
    return pl.pallas_call(
        body, name=name,
        grid_spec=pltpu.PrefetchScalarGridSpec(
            num_scalar_prefetch=1, grid=(R // tr,),
            in_specs=[own_spec] + l_specs + [spec, spec, spec], out_specs=[spec] * 4),
        out_shape=[jax.ShapeDtypeStruct((R, C), F32)] * 4, compiler_params=_params(1),
    )(my_chip, chip_sums, *landed, w, m, v)


def _adamw_small(landed, w, m, v):
    P = w.shape[1]

    def body(l_ref, w_ref, m_ref, v_ref, g_out, d_out, m_out, v_out):
        acc = l_ref[0]
        for s in range(1, N_DEV):
            acc = acc + l_ref[s]
        g = jnp.sum(acc, axis=0, keepdims=True)
        delta, mn, vn = _adamw_math(w_ref[...], g, m_ref[...], v_ref[...])
        g_out[...] = g
        d_out[...] = delta
        m_out[...] = mn
        v_out[...] = vn

    return pl.pallas_call(
        body, name="adamw_replicated", out_shape=[jax.ShapeDtypeStruct((1, P), F32)] * 4,
    )(landed, w, m, v)


def _rope_tables(pos, T):
    p = pos.astype(F32)[:, None]
    inv_r = ROPE_BASE ** (-jnp.arange(0, RET_DK, 2, dtype=F32) / RET_DK)
    ang = p * inv_r
    cos, sin = jnp.cos(ang), jnp.sin(ang)
    cos2 = jnp.concatenate([cos, cos], axis=-1)
    sin2 = jnp.concatenate([-sin, sin], axis=-1)
    inv_m = ROPE_BASE ** (-jnp.arange(0, MLA_ROPE, 2, dtype=F32) / MLA_ROPE)
    ang = p * inv_m
    cos, sin = jnp.cos(ang), jnp.sin(ang)
    zh = jnp.zeros_like(cos)
    z2 = jnp.zeros((T, LANE - MLA_ROPE), F32)
    cosp = jnp.concatenate([cos, cos, z2], axis=-1)
    sina = jnp.concatenate([zh, sin, z2], axis=-1)
    sinb = jnp.concatenate([-sin, zh, z2], axis=-1)
    return cos2, sin2, cosp, sina, sinb


def kernel(x, positions, w_in, q_norm_g, w_uq, kv_norm_g, w_uk, w_uv, ret_gn_g, w_out, ln1_g, ln1_b, w_up, w_down, ln2_g, ln2_b, loss_target, m_w_in, m_q_norm_g, m_w_uq, m_kv_norm_g, m_w_uk, m_w_uv, m_ret_gn_g, m_w_out, m_ln1_g, m_ln1_b, m_w_up, m_w_down, m_ln2_g, m_ln2_b, v_w_in, v_q_norm_g, v_w_uq, v_kv_norm_g, v_w_uk, v_w_uv, v_ret_gn_g, v_w_out, v_ln1_g, v_ln1_b, v_w_up, v_w_down, v_ln2_g, v_ln2_b):
    T, D = x.shape[1], x.shape[2]
    H = D // 256
    RW = H * RET_DV
    x = x[0]
    tgt = loss_target[0]
    alpha = DEEPNORM_ALPHA
    in_width = w_in.shape[2] * N_DEV
    mla_in = in_width - 4 * RW
    mla_in_pad = -(-mla_in // 512) * 512
    f_uq = MLA_NOPE + MLA_ROPE

    names = ["w_in", "w_uq", "w_uk", "w_uv", "w_out", "w_up", "w_down"]
    big = dict(zip(names, [w_in[0], w_uq[0], w_uk[0], w_uv[0], w_out[0], w_up[0], w_down[0]]))
    kind = dict(zip(names, ["blk", "blk", "cols", "cols", "rows", "cols", "rows"]))
    half = {n: w.astype(BF16) for n, w in big.items()}

    def gather(ns, mid, **parts):
        return _GatherJob([half[n] for n in ns], [kind[n] for n in ns], mid, **parts)

    (g_in,) = _standalone("gather_w_in", gather(["w_in"], 0.0))
    win = jnp.transpose(g_in, (1, 0, 2)).reshape(D, in_width)
    w1 = win[:, :4 * RW]
    w2 = jnp.pad(win[:, 4 * RW:], ((0, 0), (0, mla_in_pad - mla_in)))

    cos2, sin2, cosp, sina, sinb = _rope_tables(positions[0], T)
    xb = x.astype(BF16)
    tr = min(256, T)

    p1, g_uq, wuk, wuv, wout = _matmul("proj_ret", xb, w1,
                                       jobs=[gather(["w_uq", "w_uk", "w_uv", "w_out"], 0.65)])
    wuq = jnp.transpose(g_uq, (1, 0, 2)).reshape(MLA_Q_RANK, H, f_uq)
    wuq = jnp.pad(wuq, ((0, 0), (0, 0), (0, MLA_HEAD_PAD - f_uq))).reshape(MLA_Q_RANK, H * MLA_HEAD_PAD)
    wukv = jnp.concatenate([wuk, wuv], axis=1)
    p2 = _matmul("proj_mla", xb, w2)
    c_kv, c_kr = MLA_Q_RANK, MLA_Q_RANK + MLA_KV_RANK

    def mla_prep(p, cp, sa, sb, gq, gkv):
        cqn = _rms(p[:, :c_kv], gq)
        ckvn = _rms(p[:, c_kv:c_kr], gkv)
        return cqn, ckvn, _rope64(p[:, c_kr:c_kr + LANE], cp, sa, sb)

    cqn, ckvn, krope = _rowwise(
        "mla_prep", mla_prep, [(p2, mla_in_pad, 0), (cosp, LANE, 0), (sina, LANE, 0), (sinb, LANE, 0)],
        [q_norm_g, kv_norm_g], [(MLA_Q_RANK, BF16), (MLA_KV_RANK, BF16), (LANE, BF16)], [], tr)
    qf = _matmul("mla_q", cqn, wuq)
    kv = _matmul("mla_kv", ckvn, wukv, out_dtypes=(BF16,))

    def mla_heads(q, kn, vv, kr, cp, sa, sb):
        qs, ks, vs = [], [], []
        ones = jnp.ones((q.shape[0], MLA_DV), vv.dtype)
        for h in range(H):
            o = h * MLA_HEAD_PAD
            qs += [q[:, o:o + MLA_NOPE], _rope64(q[:, o + MLA_NOPE:o + MLA_HEAD_PAD], cp, sa, sb)]
            ks += [kn[:, h * MLA_NOPE:(h + 1) * MLA_NOPE], kr]
            vs += [vv[:, h * MLA_DV:(h + 1) * MLA_DV], ones]
        return jnp.concatenate(qs, axis=1), jnp.concatenate(ks, axis=1), jnp.concatenate(vs, axis=1)

    qh, kh, vx = _rowwise(
        "mla_heads", mla_heads,
        [(qf, H * MLA_HEAD_PAD, 0), (kv, RW, 0), (kv, RW, 1), (krope, LANE, 0), (cosp, LANE, 0), (sina, LANE, 0), (sinb, LANE, 0)],
        [], [(H * MLA_HEAD_PAD, BF16)] * 3, [], tr)
    att_o, att_ob, lse, wup_near = _attention_fwd(qh, kh, vx, H, jobs=[gather(["w_up"], 0.6, chips=(CHIP_X, CHIP_Y))])
    ret_pre, ret_ob, wup = _retention_fwd(
        p1, cos2, sin2, ret_gn_g, H, jobs=[gather(["w_up"], 0.7, chips=(CHIP_DIAGONAL,), own=False, base=[wup_near])])
    mixin = jnp.concatenate([ret_ob, att_ob], axis=1)
    mix = _matmul("mix_out", mixin, wout)

    def ln1_fwd(xt, mt, g, b):
        z = alpha * xt + mt
        xhat, _ = _ln_stats(z)
        y = xhat * g + b
        return z, y, y

    z1, x1, x1b = _rowwise("ln1", ln1_fwd, [(x, D, 0), (mix, D, 0)], [ln1_g, ln1_b],
                           [(D, F32), (D, F32), (D, BF16)], [], tr)

    def relu2(acc):
        r = jnp.maximum(acc, 0.0)
        return r * r, r

    a2b, rb, wdown = _matmul("mlp_up", x1b, wup, out_dtypes=(BF16, BF16), epilogue=relu2, jobs=[gather(["w_down"], 0.8)])
    hmlp = _matmul("mlp_down", a2b, wdown)

    def ln2_loss(x1t, ht, tt, g, b):
        xhat, rstd = _ln_stats(alpha * x1t + ht)
        err = xhat * g + b - tt
        dy = err / D
        dz = _ln_bwd(dy, xhat, rstd, g)
        lrow = 0.5 * jnp.mean(jnp.square(err), axis=-1, keepdims=True)
        return dz, dz, dy * xhat, dy, jnp.broadcast_to(lrow, (lrow.shape[0], LANE))

    dz2, dz2b, dg2, db2, lpart = _rowwise("ln2_loss", ln2_loss, [(x1, D, 0), (hmlp, D, 0), (tgt, D, 0)], [ln2_g, ln2_b],
                                          [(D, F32), (D, BF16)], [D, D, LANE], tr)
    loss = lax.psum(jnp.sum(lpart[:, 0]), ("x", "y", "c"))

    def pair(grads):
        ns = list(grads)
        return _PairJob([grads[n] for n in ns], [kind[n] for n in ns], [big[n].shape for n in ns])

    my_core = lax.axis_index("c").astype(jnp.int32).reshape(1)
    my_chip = (2 * lax.axis_index("x") + lax.axis_index("y")).astype(jnp.int32).reshape(1)
    chip_sums = {}

    def pair_sums(grads, theirs):
        for n, t in zip(grads, theirs):
            chip_sums[n] = _pair_sum("pair_sum_" + n, grads[n], kind[n], big[n].shape, t, my_core)
        return {n: chip_sums[n] for n in grads}

    def chip(sums, rels=CHIP_RELS, by_source=None):
        ns = list(sums)
        return _ChipJob([sums[n] for n in ns], [big[n].shape for n in ns], rels, by_source)

    landed = {}
    da = _matmul("mlp_down_dx", dz2b, wdown, tb=True, out_dtypes=(BF16,), extras=(rb,),
                 epilogue=lambda acc, r: (acc * (2.0 * r.astype(F32)),))
    g_down = {"w_down": _matmul("mlp_down_dw", a2b, dz2b, ta=True, out_dtypes=(BF16,))}
    g_wup, *pr = _matmul("mlp_up_dw", x1b, da, ta=True, out_dtypes=(BF16,), jobs=[pair(g_down)])
    sums_down = pair_sums(g_down, pr)
    g_up = {"w_up": g_wup}
    dx1m, l_a, *pr = _matmul("mlp_up_dx", da, wup, tb=True, jobs=[chip(sums_down, CHIP_RELS[:2]), pair(g_up)])
    sums_up = pair_sums(g_up, pr)

    def ln1_bwd(dm, dzz, z, g):
        xhat, rstd = _ln_stats(z)
        dy = dm + alpha * dzz
        dz = _ln_bwd(dy, xhat, rstd, g)
        return dz, dz, dy * xhat, dy

    dz1, dz1b, dg1, db1 = _rowwise("ln1_bwd", ln1_bwd, [(dx1m, D, 0), (dz2, D, 0), (z1, D, 0)], [ln1_g],
                                   [(D, F32), (D, BF16)], [D, D], tr)
    g_wout = _matmul("mix_out_dw", mixin, dz1b, ta=True, out_dtypes=(BF16,))
    dmix = _matmul("mix_out_dx", dz1b, wout, tb=True)

    dqh, dkn, dkr, dvv, l_b, l_c = _attention_bwd(qh, kh, vx, att_o, lse, dmix, H,
                                                  jobs=[chip(sums_down, CHIP_RELS[2:]), chip(sums_up)])
    landed["w_down"], landed["w_up"] = [l_a, l_b], [l_c]
    att_scale = (MLA_NOPE + MLA_ROPE) ** -0.5

    def mla_heads_bwd(dq, dkrh, cp, sa, sb):
        parts, acc = [], dkrh[:, :LANE]
        dq = dq * att_scale
        for h in range(H):
            o = h * MLA_HEAD_PAD
            parts += [dq[:, o:o + MLA_NOPE], _unrope64(dq[:, o + MLA_NOPE:o + MLA_HEAD_PAD], cp, sa, sb)]
            if h:
                acc = acc + dkrh[:, h * LANE:(h + 1) * LANE]
        return jnp.concatenate(parts, axis=1), _unrope64(acc, cp, sa, sb)

    dqb, dkr128 = _rowwise(
        "mla_heads_bwd", mla_heads_bwd,
        [(dqh, H * MLA_HEAD_PAD, 0), (dkr, H * LANE, 0), (cosp, LANE, 0), (sina, LANE, 0), (sinb, LANE, 0)],
        [], [(H * MLA_HEAD_PAD, BF16), (LANE, F32)], [], tr)
    dcqn = _matmul("mla_q_dx", dqb, wuq, tb=True)
    g_wuq = _matmul("mla_q_dw", cqn, dqb, ta=True, out_dtypes=(BF16,))
    dkvb = jnp.concatenate([dkn, dvv], axis=1)
    dckvn = _matmul("mla_kv_dx", dkvb, wukv, tb=True)
    g_wukv = _matmul("mla_kv_dw", ckvn, dkvb, ta=True, out_dtypes=(BF16,))

    def rms_bwd(c, dy, g):
        rstd = lax.rsqrt(jnp.mean(jnp.square(c), axis=-1, keepdims=True) + EPS)
        dyg = dy * g
        dc = rstd * (dyg - c * (rstd * rstd) * jnp.mean(dyg * c, axis=-1, keepdims=True))
        return dc, dy * c * rstd

    def mla_prep_bwd(p, dq_, dkv_, dkr_, gq, gkv):
        dcq, dgq_ = rms_bwd(p[:, :c_kv], dq_, gq)
        dckv, dgkv_ = rms_bwd(p[:, c_kv:c_kr], dkv_, gkv)
        pad = jnp.zeros((p.shape[0], mla_in_pad - c_kr - LANE), F32)
        return jnp.concatenate([dcq, dckv, dkr_, pad], axis=1), dgq_, dgkv_

    d_p2, dgq, dgkv = _rowwise(
        "mla_prep_bwd", mla_prep_bwd, [(p2, mla_in_pad, 0), (dcqn, MLA_Q_RANK, 0), (dckvn, MLA_KV_RANK, 0), (dkr128, LANE, 0)],
        [q_norm_g, kv_norm_g], [(mla_in_pad, BF16)], [MLA_Q_RANK, MLA_KV_RANK], tr)

    def gate_bwd(pre, gate, dout, gn):
        dpre, dgate, dgn = [], [], []
        for h in range(H):
            sl = slice(h * RET_DV, (h + 1) * RET_DV)
            nrm, rstd = _ln_stats(pre[:, sl])
            gate_h, do_h, gn_h = gate[:, sl], dout[:, sl], gn[:, sl]
            sg = jax.nn.sigmoid(gate_h)
            silu = gate_h * sg
            dgate.append(do_h * (nrm * gn_h) * (sg * (1.0 + gate_h * (1.0 - sg))))
            dn = do_h * silu
            dgn.append(dn * nrm)
            dpre.append(_ln_bwd(dn, nrm, rstd, gn_h))
        return jnp.concatenate(dpre, axis=1), jnp.concatenate(dgate, axis=1), jnp.concatenate(dgn, axis=1)

    dpre, drg, dgn = _rowwise("ret_gate_bwd", gate_bwd, [(ret_pre, RW, 0), (p1, RW, 3), (dmix, RW, 0)], [ret_gn_g],
                              [(RW, F32), (RW, BF16)], [RW], tr)
    g_uq = g_wuq.reshape(MLA_Q_RANK, H, MLA_HEAD_PAD)[:, :, :f_uq].reshape(MLA_Q_RANK, N_DEV, H * f_uq // N_DEV)
    mid_grads = {"w_uq": jnp.transpose(g_uq, (1, 0, 2)), "w_uk": g_wukv[:, :RW], "w_uv": g_wukv[:, RW:], "w_out": g_wout}
    (drq, *pr), (drk, drv) = _retention_bwd(p1, dpre, cos2, sin2, H, dq_jobs=[pair(mid_grads)])
    sums_mid = pair_sums(mid_grads, pr)
    d_p1 = jnp.concatenate([drq, drk, drv, drg], axis=1)

    g_w1, *l_a = _matmul("proj_ret_dw", xb, d_p1, ta=True, out_dtypes=(BF16,), jobs=[chip(sums_mid)])
    for n, a_ in zip(mid_grads, l_a):
        landed[n] = [a_]
    g_w2 = _matmul("proj_mla_dw", xb, d_p2, ta=True, out_dtypes=(BF16,))

    g_win = jnp.concatenate([g_w1, g_w2[:, :mla_in]], axis=1)
    g_win = {"w_in": jnp.transpose(g_win.reshape(D, N_DEV, in_width // N_DEV), (1, 0, 2))}
    small = jnp.concatenate([dgq, dgkv, dgn, dg1, db1, dg2, db2], axis=1)
    dx_a, *pr = _matmul("proj_mla_dx", d_p2, w2, tb=True, extras=(dz1,), epilogue=lambda acc, d: (acc + alpha * d,),
                        jobs=[pair(g_win)])
    sums_in = pair_sums(g_win, pr)
    grad_x, l_a, small_landed = _matmul("proj_ret_dx", d_p1, w1, tb=True, extras=(dx_a,), epilogue=lambda acc, d: (acc + d,),
                                        jobs=[chip(sums_in, by_source=small)])
    landed["w_in"] = [l_a]

    moments_m = [m_w_in, m_w_uq, m_w_uk, m_w_uv, m_w_out, m_w_up, m_w_down]
    moments_v = [v_w_in, v_w_uq, v_w_uk, v_w_uv, v_w_out, v_w_up, v_w_down]
    res = {}
    for n, m, v in zip(names, moments_m, moments_v):
        res[n] = [r[None] for r in _adamw("adamw_" + n, chip_sums[n], my_chip, landed[n], big[n], m[0], v[0])]

    small_names = ["q_norm_g", "kv_norm_g", "ret_gn_g", "ln1_g", "ln1_b", "ln2_g", "ln2_b"]
    small_w = [q_norm_g, kv_norm_g, ret_gn_g, ln1_g, ln1_b, ln2_g, ln2_b]
    small_m = [m_q_norm_g, m_kv_norm_g, m_ret_gn_g, m_ln1_g, m_ln1_b, m_ln2_g, m_ln2_b]
    small_v = [v_q_norm_g, v_kv_norm_g, v_ret_gn_g, v_ln1_g, v_ln1_b, v_ln2_g, v_ln2_b]
    cat = lambda arrs: jnp.concatenate(arrs, axis=1)
    sres = _adamw_small(small_landed, cat(small_w), cat(small_m), cat(small_v))
    off = 0
    for n, w in zip(small_names, small_w):
        res[n] = [r[:, off:off + w.shape[1]] for r in sres]
        off += w.shape[1]

    order = ["w_in", "q_norm_g", "w_uq", "kv_norm_g", "w_uk", "w_uv", "ret_gn_g", "w_out", "ln1_g", "ln1_b",
             "w_up", "w_down", "ln2_g", "ln2_b"]
    outs = [loss, grad_x[None]]
    for k in range(4):
        outs += [res[n][k] for n in order]
    return tuple(outs)
```

```python
import functools

import numpy as np

import jax
import jax.numpy as jnp
from jax import lax
from jax.experimental import pallas as pl
from jax.experimental.pallas import tpu as pltpu

F32 = jnp.float32
BF16 = jnp.bfloat16

CHUNK = 64
RET_DK = 128
RET_DV = 128
MLA_NOPE = 128
MLA_ROPE = 64
MLA_DV = 128
MLA_Q_RANK = 768
MLA_KV_RANK = 512
MLA_HEAD_PAD = 256
ROPE_BASE = 10000.0
EPS = 1e-5
DEPTH = 1
DEEPNORM_ALPHA = (2.0 * DEPTH) ** 0.25
ADAM_LR = 0.001
ADAM_B1 = 0.9
ADAM_B2 = 0.999
ADAM_EPS = 1e-08
ADAM_WD = 0.01
ADAM_STEP = 10

N_DEV = 8
LANE = 128
SUBLANE = 8
VMEM_LIMIT = 48 * 1024 * 1024
ADAMW_TILE_ELEMS = 128 * 1024
PAIR_TILE_ELEMS = 1024 * 1024

MESH = pl.DeviceIdType.MESH
ANY = pl.BlockSpec(memory_space=pl.ANY)

NN = (((1,), (0,)), ((), ()))
NT = (((1,), (1,)), ((), ()))
TN = (((0,), (0,)), ((), ()))


def _dot(a, b, dims=NN):
    return lax.dot_general(a.astype(BF16), b.astype(BF16), dims, preferred_element_type=F32)


def _params(n_axes):
    return pltpu.CompilerParams(dimension_semantics=("arbitrary",) * n_axes, vmem_limit_bytes=VMEM_LIMIT)


def _tile(n, want):
    if n <= want:
        return n
    t = want
    while t >= LANE:
        if n % t == 0:
            return t
        t -= LANE
    return n


def _my_place():
    return lax.axis_index("x"), lax.axis_index("y"), lax.axis_index("c")


def _flat(px, py, pc):
    return 4 * px + 2 * py + pc


def _block_of(ref, kind, idx, rows, cols, sub=None):
    lo, n = (0, rows) if sub is None else sub
    if kind == "blk":
        return ref.at[idx, pl.ds(lo, n), :]
    if kind == "rows":
        return ref.at[pl.ds(pl.multiple_of(idx * rows + lo, n), n), :]
    return ref.at[pl.ds(lo, n), pl.ds(pl.multiple_of(idx * cols, cols), cols)]


CHIP_X, CHIP_Y, CHIP_DIAGONAL = 0, 1, 2
ALL_CHIPS = (CHIP_X, CHIP_Y, CHIP_DIAGONAL)


class _GatherJob:
    def __init__(self, shards, kinds, mid, pieces=(((0, 1), ALL_CHIPS, True),), base=None):
        nw = len(shards)
        self.nw, self.kinds, self.pieces = nw, list(kinds), list(pieces)
        self.shapes = [s.shape for s in shards]
        self.ins = list(shards) + (list(base) if base is not None else [])
        self.alias = [(nw + w, w) for w in range(nw)] if base is not None else []
        self.out_shape = []
        for s, kind in zip(shards, kinds):
            r, c = s.shape
            shape = {"blk": (N_DEV, r, c), "rows": (N_DEV * r, c), "cols": (r, N_DEV * c)}[kind]
            self.out_shape.append(jax.ShapeDtypeStruct(shape, s.dtype))
        n = 7 * nw * len(self.pieces)
        self.sems = [pltpu.SemaphoreType.DMA((n,)), pltpu.SemaphoreType.DMA((n,)), pltpu.SemaphoreType.DMA((n,))]
        self.phases = [(0.0, self._start), (mid, self._forward), (1.0, self._finish)]

    def _each(self, src, dst, send_sems, recv_sems, local_sems):
        x, y, c = _my_place()
        all_chips = [(1 - x, y), (x, 1 - y), (1 - x, 1 - y)]
        out = []
        for w in range(self.nw):
            r, cc = self.shapes[w]
            for p, ((k_of, of), chips, own) in enumerate(self.pieces):
                sub = (k_of * (r // of), r // of)
                s0 = 7 * (w * len(self.pieces) + p)
                mine = src[w].at[pl.ds(*sub), :]

                def place(dev, w=w, r=r, cc=cc, sub=sub):
                    return _block_of(dst[w], self.kinds[w], _flat(*dev), r, cc, sub)

                def copy(k, block, to, from_src=False, place=place, mine=mine, s0=s0):
                    return pltpu.make_async_remote_copy(
                        src_ref=mine if from_src else place(block), dst_ref=place(block),
                        send_sem=send_sems.at[s0 + k], recv_sem=recv_sems.at[s0 + k], device_id=to, device_id_type=MESH)

                local = pltpu.make_async_copy(mine, place((x, y, c)), local_sems.at[s0])
                out.append(([(j, all_chips[j]) for j in chips], own, copy, local))
        return (x, y, c), (x, y, 1 - c), out

    def _start(self, *refs):
        me, sibling, each = self._each(*refs)
        for chips, own, copy, local in each:
            if own:
                local.start()
                copy(0, me, sibling, from_src=True).start()
            for j, chip in chips:
                copy(1 + j, me, (*chip, me[2]), from_src=True).start()

    def _forward(self, *refs):
        me, sibling, each = self._each(*refs)
        for chips, own, copy, local in each:
            for j, chip in chips:
                copy(1 + j, (*chip, me[2]), me).wait_recv()
                copy(4 + j, (*chip, me[2]), sibling).start()

    def _finish(self, *refs):
        me, sibling, each = self._each(*refs)
        for chips, own, copy, local in each:
            if own:
                copy(0, sibling, me).wait_recv()
            for j, chip in chips:
                copy(4 + j, (*chip, 1 - me[2]), me).wait_recv()
        for chips, own, copy, local in each:
            if own:
                copy(0, me, sibling, from_src=True).wait_send()
                local.wait()
            for j, chip in chips:
                copy(1 + j, me, (*chip, me[2]), from_src=True).wait_send()
                copy(4 + j, (*chip, me[2]), sibling).wait_send()


REL_ALL = [(1, 0, 0), (1, 0, 1), (0, 1, 0), (0, 1, 1), (1, 1, 0), (1, 1, 1), (0, 0, 1)]
N_CHIP = 4


class _PairJob:
    def __init__(self, grads, kinds, shard_shapes):
        self.ins = list(grads)
        self.kinds, self.shard_shapes, self.nw = list(kinds), list(shard_shapes), len(grads)
        self.out_shape = [jax.ShapeDtypeStruct((N_CHIP,) + tuple(s), g.dtype) for g, s in zip(grads, shard_shapes)]
        n = N_CHIP * self.nw
        self.sems = [pltpu.SemaphoreType.DMA((n,)), pltpu.SemaphoreType.DMA((n,))]
        self.phases = [(0.0, self._start), (1.0, self._finish)]

    def _copies(self, src, theirs, send_sems, recv_sems):
        x, y, c = _my_place()
        remote = []
        for w in range(self.nw):
            r, cc = self.shard_shapes[w]
            for q in range(N_CHIP):
                s = N_CHIP * w + q
                remote.append(pltpu.make_async_remote_copy(
                    src_ref=_block_of(src[w], self.kinds[w], 2 * q + (1 - c), r, cc), dst_ref=theirs[w].at[q],
                    send_sem=send_sems.at[s], recv_sem=recv_sems.at[s], device_id=(x, y, 1 - c), device_id_type=MESH))
        return remote

    def _start(self, *refs):
        for cp in self._copies(*refs):
            cp.start()

    def _finish(self, *refs):
        for cp in self._copies(*refs):
            cp.wait()


CHIP_RELS = [(1, 0), (0, 1), (1, 1)]


class _ChipJob:
    def __init__(self, chip_sums, shard_shapes, rels=CHIP_RELS, by_source=None):
        self.ins = list(chip_sums) + ([by_source] if by_source is not None else [])
        self.nw, self.has_src, self.rels = len(chip_sums), by_source is not None, list(rels)
        self.out_shape = [jax.ShapeDtypeStruct((len(rels),) + tuple(s), a.dtype) for a, s in zip(chip_sums, shard_shapes)]
        if self.has_src:
            self.out_shape.append(jax.ShapeDtypeStruct((N_DEV,) + by_source.shape, by_source.dtype))
        n = len(rels) * self.nw + len(REL_ALL) * int(self.has_src)
        self.sems = [pltpu.SemaphoreType.DMA((n,)), pltpu.SemaphoreType.DMA((n,)), pltpu.SemaphoreType.DMA((1,))]
        self.phases = [(0.0, self._start), (1.0, self._finish)]

    def _copies(self, src, land, send_sems, recv_sems, local_sems):
        x, y, c = _my_place()
        nw = self.nw

        def remote_copy(s, src_ref, dst_ref, peer):
            return pltpu.make_async_remote_copy(src_ref=src_ref, dst_ref=dst_ref, send_sem=send_sems.at[s],
                                                recv_sem=recv_sems.at[s], device_id=peer, device_id_type=MESH)

        local, remote = [], []
        for k, (fx, fy) in enumerate(self.rels):
            px, py = x ^ fx, y ^ fy
            remote += [remote_copy(nw * k + w, src[w].at[2 * px + py], land[w].at[k], (px, py, c)) for w in range(nw)]
        if self.has_src:
            my_idx = _flat(x, y, c)
            local.append(pltpu.make_async_copy(src[nw], land[nw].at[my_idx], local_sems.at[0]))
            remote += [remote_copy(nw * len(self.rels) + k, src[nw], land[nw].at[my_idx], (x ^ fx, y ^ fy, c ^ fc))
                       for k, (fx, fy, fc) in enumerate(REL_ALL)]
        return local, remote

    def _start(self, *refs):
        local, remote = self._copies(*refs)
        for cp in local + remote:
            cp.start()

    def _finish(self, *refs):
        local, remote = self._copies(*refs)
        for cp in remote + local:
            cp.wait()


def _row_tile(R, C, elems=ADAMW_TILE_ELEMS):
    tr = 16
    while tr * 2 * C <= elems and R % (tr * 2) == 0:
        tr *= 2
    return tr if R % tr == 0 else R


def _part_spec(kind, tr, C, R, n_lead):
    def dev(ids, pref):
        return 2 * ids[0] + pref[0] if n_lead == 2 else pref[0]

    def index(*args):
        ids, pref = args[:n_lead], args[n_lead]
        i = ids[-1]
        if kind == "blk":
            return dev(ids, pref), i, 0
        if kind == "rows":
            return dev(ids, pref) * (R // tr) + i, 0
        return i, dev(ids, pref)

    return pl.BlockSpec((None, tr, C) if kind == "blk" else (tr, C), index)


def _pair_sum(name, grad, kind, shard_shape, theirs, c):
    R, C = shard_shape
    tr = _row_tile(R, C, PAIR_TILE_ELEMS)

    def body(c_ref, g_ref, t_ref, o_ref):
        o_ref[...] = (g_ref[...].astype(F32) + t_ref[...].astype(F32)).astype(o_ref.dtype)

    slot = pl.BlockSpec((None, tr, C), lambda q, i, c_ref: (q, i, 0))
    return pl.pallas_call(
        body, name=name,
        grid_spec=pltpu.PrefetchScalarGridSpec(
            num_scalar_prefetch=1, grid=(N_CHIP, R // tr),
            in_specs=[_part_spec(kind, tr, C, R, 2), slot], out_specs=slot),
        out_shape=jax.ShapeDtypeStruct(theirs.shape, theirs.dtype), compiler_params=_params(2))(c, grad, theirs)


class _Carried:
    def __init__(self, jobs, grid):
        self.jobs = list(jobs)
        self.steps = int(np.prod(grid))
        self.grid = tuple(grid)
        self.operands = [a for j in self.jobs for a in j.ins]
        self.out_shape = [o for j in self.jobs for o in j.out_shape]
        self.sems = [s for j in self.jobs for s in j.sems]
        self.in_specs = [ANY] * len(self.operands)
        self.out_specs = [ANY] * len(self.out_shape)

    def aliases(self, n_in, n_out):
        out, i0, o0 = {}, n_in, n_out
        for job in self.jobs:
            for ji, jo in getattr(job, "alias", []):
                out[i0 + ji] = o0 + jo
            i0, o0 = i0 + len(job.ins), o0 + len(job.out_shape)
        return out

    def _step(self):
        lin = pl.program_id(0)
        for a in range(1, len(self.grid)):
            lin = lin * self.grid[a] + pl.program_id(a)
        return lin

    def _run(self, in_refs, out_refs, sem_refs, last):
        lin = self._step()
        for job in self.jobs:
            ins, in_refs = in_refs[:len(job.ins)], in_refs[len(job.ins):]
            outs, out_refs = out_refs[:len(job.out_shape)], out_refs[len(job.out_shape):]
            sems, sem_refs = sem_refs[:len(job.sems)], sem_refs[len(job.sems):]
            for frac, fn in job.phases:
                if (frac >= 1.0) != last:
                    continue
                at = min(self.steps - 1, int(frac * (self.steps - 1)))
                pl.when(lin == at)(functools.partial(fn, ins, outs, *sems))

    def before(self, in_refs, out_refs, sem_refs):
        self._run(in_refs, out_refs, sem_refs, last=False)

    def after(self, in_refs, out_refs, sem_refs):
        self._run(in_refs, out_refs, sem_refs, last=True)


def _standalone(name, job):
    def body(*refs):
        ni, no = len(job.ins), len(job.out_shape)
        for _, fn in job.phases:
            fn(refs[:ni], refs[ni:ni + no], *refs[ni + no:])

    return pl.pallas_call(body, name=name, in_specs=[ANY] * len(job.ins), out_specs=[ANY] * len(job.out_shape),
                          out_shape=job.out_shape, scratch_shapes=job.sems)(*job.ins)


def _split_refs(refs, n_in, n_job_in, n_out, n_job_out, n_scratch):
    cuts = np.cumsum([0, n_in, n_job_in, n_out, n_job_out, n_scratch])
    return [refs[cuts[i]:cuts[i + 1]] for i in range(5)] + [refs[cuts[5]:]]


def _matmul(name, a, b, *, ta=False, tb=False, out_dtypes=(F32,), epilogue=None, extras=(), tm=1024, tn=1024, tk=2048, jobs=()):
    if ta:
        K, M = a.shape
    else:
        M, K = a.shape
    N = b.shape[0] if tb else b.shape[1]
    tm, tn, tk = _tile(M, tm), _tile(N, tn), _tile(K, tk)
    nk = K // tk
    grid = (M // tm, N // tn, nk)
    dims = TN if ta else (NT if tb else NN)
    n_ex, n_out = len(extras), len(out_dtypes)
    carried = _Carried(jobs, grid)

    def body(*refs):
        ins, job_ins, out_refs, job_outs, scratch, job_sems = _split_refs(
            refs, 2 + n_ex, len(carried.operands), n_out, len(carried.out_shape), int(nk > 1))
        a_ref, b_ref, ex_refs = ins[0], ins[1], ins[2:]
        k = pl.program_id(2)
        carried.before(job_ins, job_outs, job_sems)

        def finish(acc):
            res = (acc,) if epilogue is None else epilogue(acc, *[e[...] for e in ex_refs])
            for o_ref, r in zip(out_refs, res):
                o_ref[...] = r.astype(o_ref.dtype)

        if nk == 1:
            finish(_dot(a_ref[...], b_ref[...], dims))
        else:
            acc_ref = scratch[0]

            @pl.when(k == 0)
            def _():
                acc_ref[...] = _dot(a_ref[...], b_ref[...], dims)

            @pl.when(k > 0)
            def _():
                acc_ref[...] += _dot(a_ref[...], b_ref[...], dims)

            pl.when(k == nk - 1)(lambda: finish(acc_ref[...]))

        carried.after(job_ins, job_outs, job_sems)

    a_spec = pl.BlockSpec((tk, tm), lambda i, j, k: (k, i)) if ta else pl.BlockSpec((tm, tk), lambda i, j, k: (i, k))
    b_spec = pl.BlockSpec((tn, tk), lambda i, j, k: (j, k)) if tb else pl.BlockSpec((tk, tn), lambda i, j, k: (k, j))
    tile_spec = pl.BlockSpec((tm, tn), lambda i, j, k: (i, j))
    outs = pl.pallas_call(
        body,
        name=name,
        grid=grid,
        in_specs=[a_spec, b_spec] + [tile_spec] * n_ex + carried.in_specs,
        out_specs=[tile_spec] * n_out + carried.out_specs,
        out_shape=[jax.ShapeDtypeStruct((M, N), d) for d in out_dtypes] + carried.out_shape,
        scratch_shapes=[pltpu.VMEM((tm, tn), F32)] * int(nk > 1) + carried.sems,
        input_output_aliases=carried.aliases(2 + n_ex, n_out),
        compiler_params=_params(3),
    )(a, b, *extras, *carried.operands)
    return outs[0] if len(outs) == 1 else outs


def _rowwise(name, fn, row_in, const_in, row_out, acc_out, tm):
    T = row_in[0][0].shape[0]
    n_r, n_c, n_o, n_a = len(row_in), len(const_in), len(row_out), len(acc_out)

    def body(*refs):
        ins = [r[...] for r in refs[:n_r + n_c]]
        out_refs = refs[n_r + n_c:n_r + n_c + n_o]
        acc_refs = refs[n_r + n_c + n_o:]
        res = fn(*ins)
        for o_ref, r in zip(out_refs, res[:n_o]):
            o_ref[...] = r.astype(o_ref.dtype)

        @pl.when(pl.program_id(0) == 0)
        def _():
            for a_ref in acc_refs:
                a_ref[...] = jnp.zeros_like(a_ref)

        for a_ref, r in zip(acc_refs, res[n_o:]):
            a_ref[...] += jnp.sum(r.reshape(tm // SUBLANE, SUBLANE, r.shape[-1]), axis=0)

    in_specs = [pl.BlockSpec((tm, w), functools.partial(lambda i, c: (i, c), c=cb)) for (_, w, cb) in row_in]
    in_specs += [pl.BlockSpec(c.shape, functools.partial(lambda i, nd: (0,) * nd, nd=c.ndim)) for c in const_in]
    out_specs = [pl.BlockSpec((tm, c), lambda i: (i, 0)) for (c, _) in row_out]
    out_specs += [pl.BlockSpec((SUBLANE, c), lambda i: (0, 0)) for c in acc_out]
    out_shape = [jax.ShapeDtypeStruct((T, c), d) for (c, d) in row_out]
    out_shape += [jax.ShapeDtypeStruct((SUBLANE, c), F32) for c in acc_out]
    return pl.pallas_call(
        body, name=name, grid=(T // tm,), in_specs=in_specs, out_specs=out_specs, out_shape=out_shape,
        compiler_params=_params(1),
    )(*[r[0] for r in row_in], *const_in)


def _ln_stats(z):
    mu = jnp.mean(z, axis=-1, keepdims=True)
    var = jnp.mean(jnp.square(z - mu), axis=-1, keepdims=True)
    rstd = lax.rsqrt(var + EPS)
    return (z - mu) * rstd, rstd


def _ln_bwd(dy, xhat, rstd, g):
    dxh = dy * g
    return rstd * (dxh - jnp.mean(dxh, axis=-1, keepdims=True) - xhat * jnp.mean(dxh * xhat, axis=-1, keepdims=True))


def _rms(x, g):
    return x * lax.rsqrt(jnp.mean(jnp.square(x), axis=-1, keepdims=True) + EPS) * g


def _rope128(x, cos2, sin2):
    return x * cos2 + pltpu.roll(x, RET_DK // 2, 1) * sin2


def _unrope128(dy, cos2, sin2):
    return dy * cos2 + pltpu.roll(dy * sin2, RET_DK // 2, 1)


def _rope64(x, cosp, sina, sinb):
    h = MLA_ROPE // 2
    return x * cosp + pltpu.roll(x, h, 1) * sina + pltpu.roll(x, LANE - h, 1) * sinb


def _unrope64(dy, cosp, sina, sinb):
    h = MLA_ROPE // 2
    return dy * cosp + pltpu.roll(dy * sina, LANE - h, 1) + pltpu.roll(dy * sinb, h, 1)


RET_BLOCK = 256
RET_HEADS_PER_STEP = 4


def _ret_tables(H, B):
    log_g = np.log1p(-np.exp2(-5.0 - np.arange(H, dtype=np.float32))).astype(np.float32)
    idx = np.arange(B, dtype=np.float32)
    dist = np.abs(idx[:, None] - idx[None, :])
    visible = (np.arange(B)[None, :] // CHUNK) <= (np.arange(B)[:, None] // CHUNK)
    dmat = np.exp(log_g[:, None, None] * dist).astype(np.float32) * visible[None].astype(np.float32)
    qd = np.exp(log_g[:, None] * (idx + 1.0)).astype(np.float32)
    kd = np.exp(log_g[:, None] * (B - 1 - idx)).astype(np.float32)
    sd = np.exp(log_g * B).astype(np.float32)
    bc = lambda v: np.ascontiguousarray(np.broadcast_to(v[:, :, None], (H, v.shape[1], LANE)))
    sdb = np.ascontiguousarray(np.broadcast_to(sd[:, None, None], (H, SUBLANE, LANE)))
    return jnp.asarray(dmat), jnp.asarray(bc(qd)), jnp.asarray(bc(kd)), jnp.asarray(sdb)


def _ret_specs(H, B, reverse, nb):
    blk = (lambda b: nb - 1 - b) if reverse else (lambda b: b)
    G = min(RET_HEADS_PER_STEP, H)
    W = G * LANE
    col = lambda off: pl.BlockSpec((B, W), functools.partial(lambda h, b, off: (blk(b), off // G + h), off=off))
    row = pl.BlockSpec((B, LANE), lambda h, b: (blk(b), 0))
    per_head = lambda r: pl.BlockSpec((G, r, LANE), lambda h, b: (h, 0, 0))
    dmat = pl.BlockSpec((G, B, B), lambda h, b: (h, 0, 0))
    return G, col, row, per_head, dmat


def _ret_call(name, body, n_in, n_out, operands, in_specs, out_specs, out_shape, grid, G, jobs):
    carried = _Carried(jobs, grid)

    def wrapped(*refs):
        ins, job_ins, outs, job_outs, (state,), job_sems = _split_refs(
            refs, n_in, len(carried.operands), n_out, len(carried.out_shape), 1)
        carried.before(job_ins, job_outs, job_sems)

        @pl.when(pl.program_id(1) == 0)
        def _():
            state[...] = jnp.zeros_like(state)

        body(*ins, *outs, state)
        carried.after(job_ins, job_outs, job_sems)

    return pl.pallas_call(
        wrapped, name=name, grid=grid,
        in_specs=in_specs + carried.in_specs, out_specs=out_specs + carried.out_specs,
        out_shape=out_shape + carried.out_shape,
        scratch_shapes=[pltpu.VMEM((G, RET_DK, RET_DV), F32)] + carried.sems,
        input_output_aliases=carried.aliases(n_in, n_out),
        compiler_params=_params(2),
    )(*operands, *carried.operands)


def _retention_fwd(p1, cos2, sin2, gn_g, H, jobs=()):
    T = p1.shape[0]
    B = min(RET_BLOCK, T)
    nb = T // B
    scale = RET_DK ** -0.5
    dmat, qd, kd, sd = _ret_tables(H, B)
    G, col, row, per_head, dspec = _ret_specs(H, B, False, nb)

    def body(q_ref, k_ref, v_ref, g_ref, cos_ref, sin_ref, d_ref, qd_ref, kd_ref, sd_ref, gn_ref, pre_ref, out_ref, s_ref):
        cos, sin = cos_ref[...], sin_ref[...]
        for g in range(G):
            hl = pl.ds(g * LANE, LANE)
            q = _rope128(q_ref[:, hl], cos, sin)
            k = _rope128(k_ref[:, hl], cos, sin) * scale
            v = v_ref[:, hl]
            s = s_ref[g]
            a = _dot(q, k, NT) * d_ref[g]
            ret = _dot(a, v) + _dot(q * qd_ref[g], s)
            s_ref[g] = sd_ref[g, 0:1, :] * s + _dot(k * kd_ref[g], v, TN)
            pre_ref[:, hl] = ret
            nrm, _ = _ln_stats(ret)
            out_ref[:, hl] = (jax.nn.silu(g_ref[:, hl]) * (nrm * gn_ref[:, hl])).astype(out_ref.dtype)

    out_spec = pl.BlockSpec((B, G * LANE), lambda h, b: (b, h))
    return _ret_call(
        "retention_fwd", body, 11, 2, (p1, p1, p1, p1, cos2, sin2, dmat, qd, kd, sd, gn_g),
        [col(0), col(H), col(2 * H), col(3 * H), row, row, dspec, per_head(B), per_head(B), per_head(SUBLANE),
         pl.BlockSpec((1, G * LANE), lambda h, b: (0, h))],
        [out_spec, out_spec],
        [jax.ShapeDtypeStruct((T, H * RET_DV), F32), jax.ShapeDtypeStruct((T, H * RET_DV), BF16)],
        (H // G, nb), G, jobs)


def _retention_bwd(p1, dpre, cos2, sin2, H, dq_jobs=(), dkv_jobs=()):
    T = p1.shape[0]
    B = min(RET_BLOCK, T)
    nb = T // B
    scale = RET_DK ** -0.5
    dmat, qd, kd, sd = _ret_tables(H, B)

    def dq_body(k_ref, v_ref, do_ref, cos_ref, sin_ref, d_ref, qd_ref, kd_ref, sd_ref, dq_ref, s_ref):
        cos, sin = cos_ref[...], sin_ref[...]
        for g in range(G):
            hl = pl.ds(g * LANE, LANE)
            k = _rope128(k_ref[:, hl], cos, sin) * scale
            v, do, s = v_ref[:, hl], do_ref[:, hl], s_ref[g]
            da = _dot(do, v, NT) * d_ref[g]
            dq = _dot(da, k) + _dot(do, s, NT) * qd_ref[g]
            dq_ref[:, hl] = _unrope128(dq, cos, sin).astype(dq_ref.dtype)
            s_ref[g] = sd_ref[g, 0:1, :] * s + _dot(k * kd_ref[g], v, TN)

    G, col, row, per_head, dspec = _ret_specs(H, B, False, nb)
    dq = _ret_call(
        "retention_bwd_dq", dq_body, 9, 1, (p1, p1, dpre, cos2, sin2, dmat, qd, kd, sd),
        [col(H), col(2 * H), col(0), row, row, dspec, per_head(B), per_head(B), per_head(SUBLANE)],
        [pl.BlockSpec((B, G * LANE), lambda h, b: (b, h))], [jax.ShapeDtypeStruct((T, H * RET_DK), BF16)],
        (H // G, nb), G, dq_jobs)

    def dkv_body(q_ref, k_ref, v_ref, do_ref, cos_ref, sin_ref, d_ref, qd_ref, kd_ref, sd_ref, dk_ref, dv_ref, g_ref):
        cos, sin = cos_ref[...], sin_ref[...]
        for g in range(G):
            hl = pl.ds(g * LANE, LANE)
            q = _rope128(q_ref[:, hl], cos, sin)
            k = _rope128(k_ref[:, hl], cos, sin) * scale
            v, do, st = v_ref[:, hl], do_ref[:, hl], g_ref[g]
            a = _dot(q, k, NT) * d_ref[g]
            da = _dot(do, v, NT) * d_ref[g]
            dv = _dot(a, do, TN) + _dot(k * kd_ref[g], st)
            dk = _dot(da, q, TN) + _dot(v, st, NT) * kd_ref[g]
            dv_ref[:, hl] = dv.astype(dv_ref.dtype)
            dk_ref[:, hl] = _unrope128(dk * scale, cos, sin).astype(dk_ref.dtype)
            g_ref[g] = sd_ref[g, 0:1, :] * st + _dot(q * qd_ref[g], do, TN)

    G, col, row, per_head, dspec = _ret_specs(H, B, True, nb)
    out_spec = pl.BlockSpec((B, G * LANE), lambda h, b: (nb - 1 - b, h))
    dkv = _ret_call(
        "retention_bwd_dkv", dkv_body, 10, 2, (p1, p1, p1, dpre, cos2, sin2, dmat, qd, kd, sd),
        [col(0), col(H), col(2 * H), col(0), row, row, dspec, per_head(B), per_head(B), per_head(SUBLANE)],
        [out_spec, out_spec],
        [jax.ShapeDtypeStruct((T, H * RET_DK), BF16), jax.ShapeDtypeStruct((T, H * RET_DV), BF16)],
        (H // G, nb), G, dkv_jobs)
    return dq, dkv


ATT_BLOCK = 512


ATT_ROW_CHUNKS = 2
ATT_HEADS_PER_STEP = 8
ATT_BWD_HEADS_PER_STEP = 2
LOG2E = 1.4426950408889634


def _chunk_mask(rows, row0, cols):
    qi = (lax.broadcasted_iota(jnp.int32, (rows, cols), 0) + row0) // CHUNK
    kj = lax.broadcasted_iota(jnp.int32, (rows, cols), 1) // CHUNK
    return kj <= qi


def _attention_fwd(qh, kh, vx, H, jobs=()):
    T = qh.shape[0]
    t = min(ATT_BLOCK, T)
    n = T // t
    r = t // ATT_ROW_CHUNKS
    G = min(ATT_HEADS_PER_STEP, H)
    P = MLA_HEAD_PAD
    scale = (MLA_NOPE + MLA_ROPE) ** -0.5
    c2 = scale * LOG2E
    carried = _Carried(jobs, (H // G, n, n))

    def body(*refs):
        (q_ref, k_ref, v_ref), job_ins, (o_ref, ob_ref, lse_ref), job_outs, (m_ref, acc_ref), job_sems = _split_refs(
            refs, 3, len(carried.operands), 3, len(carried.out_shape), 2)
        i, j = pl.program_id(1), pl.program_id(2)
        carried.before(job_ins, job_outs, job_sems)

        @pl.when(j == 0)
        def _():
            m_ref[...] = jnp.full_like(m_ref, -jnp.inf)
            acc_ref[...] = jnp.zeros_like(acc_ref)

        def update(masked):
            for g in range(G):
                hc = pl.ds(g * P, P)
                for ch in range(ATT_ROW_CHUNKS):
                    rs = pl.ds(ch * r, r)
                    s = _dot(q_ref[rs, hc], k_ref[:, hc], NT)
                    if masked:
                        s = jnp.where(_chunk_mask(r, ch * r, t), s, -jnp.inf)
                    m_prev = m_ref[rs, pl.ds(g * LANE, LANE)]
                    m_new = jnp.maximum(m_prev, jnp.max(s, axis=-1, keepdims=True))
                    alpha = jnp.exp2((m_prev - m_new) * c2)
                    p = jnp.exp2((s - jnp.tile(m_new, (1, t // LANE))) * c2)
                    acc_ref[rs, hc] = jnp.tile(alpha, (1, 2)) * acc_ref[rs, hc] + _dot(p, v_ref[:, hc])
                    m_ref[rs, pl.ds(g * LANE, LANE)] = m_new

        pl.when(j < i)(functools.partial(update, False))
        pl.when(j == i)(functools.partial(update, True))

        @pl.when(j == n - 1)
        def _():
            for g in range(G):
                acc = acc_ref[:, pl.ds(g * P, P)]
                l = acc[:, MLA_DV:]
                o = acc[:, :MLA_DV] / l
                o_ref[:, pl.ds(g * MLA_DV, MLA_DV)] = o
                ob_ref[:, pl.ds(g * MLA_DV, MLA_DV)] = o.astype(ob_ref.dtype)
                lse_ref[g] = m_ref[:, pl.ds(g * LANE, LANE)] * scale + jnp.log(l)

        carried.after(job_ins, job_outs, job_sems)

    kclamp = lambda i, j: jnp.minimum(j, i)
    o_spec = pl.BlockSpec((t, G * MLA_DV), lambda h, i, j: (i, h))
    return pl.pallas_call(
        body, name="attention_fwd", grid=(H // G, n, n),
        in_specs=[pl.BlockSpec((t, G * P), lambda h, i, j: (i, h)),
                  pl.BlockSpec((t, G * P), lambda h, i, j: (kclamp(i, j), h)),
                  pl.BlockSpec((t, G * P), lambda h, i, j: (kclamp(i, j), h))] + carried.in_specs,
        out_specs=[o_spec, o_spec, pl.BlockSpec((G, t, LANE), lambda h, i, j: (h, i, 0))] + carried.out_specs,
        out_shape=[jax.ShapeDtypeStruct((T, H * MLA_DV), F32), jax.ShapeDtypeStruct((T, H * MLA_DV), BF16),
                   jax.ShapeDtypeStruct((H, T, LANE), F32)] + carried.out_shape,
        scratch_shapes=[pltpu.VMEM((t, G * LANE), F32), pltpu.VMEM((t, G * P), F32)] + carried.sems,
        input_output_aliases=carried.aliases(3, 3),
        compiler_params=_params(3),
    )(qh, kh, vx, *carried.operands)


def _attention_bwd(qh, kh, vx, o, lse, dmix, H, jobs=()):
    T = qh.shape[0]
    t = min(ATT_BLOCK, T)
    n = T // t
    r = t // ATT_ROW_CHUNKS
    G = min(ATT_BWD_HEADS_PER_STEP, H)
    P = MLA_HEAD_PAD
    scale = (MLA_NOPE + MLA_ROPE) ** -0.5
    c2 = scale * LOG2E
    carried = _Carried(jobs, (H // G, n, n))

    def body(*refs):
        ((q_ref, k_ref, v_ref, do_ref, o_ref, lse_ref), job_ins, (dq_ref, dkn_ref, dkr_ref, dv_ref), job_outs,
         (dk_acc, dv_acc), job_sems) = _split_refs(refs, 6, len(carried.operands), 4, len(carried.out_shape), 2)
        j, i = pl.program_id(1), pl.program_id(2)
        carried.before(job_ins, job_outs, job_sems)

        @pl.when(jnp.logical_and(j == 0, i == 0))
        def _():
            dq_ref[...] = jnp.zeros_like(dq_ref)

        @pl.when(i == j)
        def _():
            dk_acc[...] = jnp.zeros_like(dk_acc)
            dv_acc[...] = jnp.zeros_like(dv_acc)

        def update(masked):
            for g in range(G):
                hc, hv = pl.ds(g * P, P), pl.ds(g * MLA_DV, MLA_DV)
                k, v = k_ref[:, hc], v_ref[:, pl.ds(g * P, MLA_DV)]
                for ch in range(ATT_ROW_CHUNKS):
                    rs = pl.ds(ch * r, r)
                    q, do = q_ref[rs, hc], do_ref[rs, hv]
                    s = _dot(q, k, NT)
                    if masked:
                        s = jnp.where(_chunk_mask(r, ch * r, t), s, -jnp.inf)
                    p = jnp.exp2(s * c2 - jnp.tile(lse_ref[g, rs, :] * LOG2E, (1, t // LANE)))
                    dv_acc[:, hv] += _dot(p, do, TN)
                    dp = _dot(do, v, NT)
                    delta = jnp.sum(do * o_ref[rs, hv], axis=-1, keepdims=True)
                    ds = p * (dp - delta)
                    rows = pl.ds(pl.multiple_of(i * t, t) + ch * r, r)
                    dq_ref[rows, hc] += _dot(ds, k)
                    dk_acc[:, hc] += _dot(ds, q, TN)

        pl.when(i > j)(functools.partial(update, False))
        pl.when(i == j)(functools.partial(update, True))

        @pl.when(i == n - 1)
        def _():
            for g in range(G):
                dk = dk_acc[:, pl.ds(g * P, P)] * scale
                dkn_ref[:, pl.ds(g * LANE, LANE)] = dk[:, :MLA_NOPE].astype(dkn_ref.dtype)
                dkr_ref[:, pl.ds(g * LANE, LANE)] = dk[:, MLA_NOPE:]
            dv_ref[...] = dv_acc[...].astype(dv_ref.dtype)

        carried.after(job_ins, job_outs, job_sems)

    qclamp = lambda j, i: jnp.maximum(i, j)
    kv_out = pl.BlockSpec((t, G * LANE), lambda h, j, i: (j, h))
    q_in = pl.BlockSpec((t, G * MLA_DV), lambda h, j, i: (qclamp(j, i), h))
    return pl.pallas_call(
        body, name="attention_bwd", grid=(H // G, n, n),
        in_specs=[pl.BlockSpec((t, G * P), lambda h, j, i: (qclamp(j, i), h)),
                  pl.BlockSpec((t, G * P), lambda h, j, i: (j, h)),
                  pl.BlockSpec((t, G * P), lambda h, j, i: (j, h)),
                  pl.BlockSpec((t, G * MLA_DV), lambda h, j, i: (qclamp(j, i), H // G + h)),
                  q_in,
                  pl.BlockSpec((G, t, LANE), lambda h, j, i: (h, qclamp(j, i), 0))] + carried.in_specs,
        out_specs=[pl.BlockSpec((T, G * P), lambda h, j, i: (0, h)), kv_out, kv_out, kv_out] + carried.out_specs,
        out_shape=[jax.ShapeDtypeStruct((T, H * P), F32), jax.ShapeDtypeStruct((T, H * LANE), BF16),
                   jax.ShapeDtypeStruct((T, H * LANE), F32), jax.ShapeDtypeStruct((T, H * LANE), BF16)] + carried.out_shape,
        scratch_shapes=[pltpu.VMEM((t, G * P), F32), pltpu.VMEM((t, G * MLA_DV), F32)] + carried.sems,
        input_output_aliases=carried.aliases(6, 4),
        compiler_params=_params(3),
    )(qh, kh, vx, dmix, o, lse, *carried.operands)


def _adamw_math(w, g, m, v):
    m = ADAM_B1 * m + (1.0 - ADAM_B1) * g
    v = ADAM_B2 * v + (1.0 - ADAM_B2) * jnp.square(g)
    m_hat = m / (1.0 - ADAM_B1 ** ADAM_STEP)
    v_hat = v / (1.0 - ADAM_B2 ** ADAM_STEP)
    delta = -ADAM_LR * (m_hat / (jnp.sqrt(v_hat) + ADAM_EPS) + ADAM_WD * w)
    return delta, m, v


def _adamw(name, chip_sums, my_chip, landed, w, m, v):
    R, C = w.shape
    tr = _row_tile(R, C)
    nl = len(landed)
    assert sum(l.shape[0] for l in landed) == N_CHIP - 1

    def body(q_ref, own_ref, *refs):
        l_refs, (w_ref, m_ref, v_ref, g_out, d_out, m_out, v_out) = refs[:nl], refs[nl:]
        g = own_ref[...].astype(F32)
        for l_ref in l_refs:
            for s in range(l_ref.shape[0]):
                g = g + l_ref[s].astype(F32)
        delta, mn, vn = _adamw_math(w_ref[...], g, m_ref[...], v_ref[...])
        g_out[...] = g
        d_out[...] = delta
        m_out[...] = mn
        v_out[...] = vn

    spec = pl.BlockSpec((tr, C), lambda i, q_ref: (i, 0))
    own_spec = pl.BlockSpec((None, tr, C), lambda i, q_ref: (q_ref[0], i, 0))
    l_specs = [pl.BlockSpec((l.shape[0], tr, C), lambda i, q_ref: (0, i, 0)) for l in landed]
    return pl.pallas_call(
        body, name=name,
        grid_spec=pltpu.PrefetchScalarGridSpec(
            num_scalar_prefetch=1, grid=(R // tr,),
            in_specs=[own_spec] + l_specs + [spec, spec, spec], out_specs=[spec] * 4),
        out_shape=[jax.ShapeDtypeStruct((R, C), F32)] * 4, compiler_params=_params(1),
    )(my_chip, chip_sums, *landed, w, m, v)


def _adamw_small(landed, w, m, v):
    P = w.shape[1]

    def body(l_ref, w_ref, m_ref, v_ref, g_out, d_out, m_out, v_out):
        acc = l_ref[0]
        for s in range(1, N_DEV):
            acc = acc + l_ref[s]
        g = jnp.sum(acc, axis=0, keepdims=True)
        delta, mn, vn = _adamw_math(w_ref[...], g, m_ref[...], v_ref[...])
        g_out[...] = g
        d_out[...] = delta
        m_out[...] = mn
        v_out[...] = vn

    return pl.pallas_call(
        body, name="adamw_replicated", out_shape=[jax.ShapeDtypeStruct((1, P), F32)] * 4,
    )(landed, w, m, v)


def _rope_tables(pos, T):
    p = pos.astype(F32)[:, None]
    inv_r = ROPE_BASE ** (-jnp.arange(0, RET_DK, 2, dtype=F32) / RET_DK)
    ang = p * inv_r
    cos, sin = jnp.cos(ang), jnp.sin(ang)
    cos2 = jnp.concatenate([cos, cos], axis=-1)
    sin2 = jnp.concatenate([-sin, sin], axis=-1)
    inv_m = ROPE_BASE ** (-jnp.arange(0, MLA_ROPE, 2, dtype=F32) / MLA_ROPE)
    ang = p * inv_m
    cos, sin = jnp.cos(ang), jnp.sin(ang)
    zh = jnp.zeros_like(cos)
    z2 = jnp.zeros((T, LANE - MLA_ROPE), F32)
    cosp = jnp.concatenate([cos, cos, z2], axis=-1)
    sina = jnp.concatenate([zh, sin, z2], axis=-1)
    sinb = jnp.concatenate([-sin, zh, z2], axis=-1)
    return cos2, sin2, cosp, sina, sinb


def kernel(x, positions, w_in, q_norm_g, w_uq, kv_norm_g, w_uk, w_uv, ret_gn_g, w_out, ln1_g, ln1_b, w_up, w_down, ln2_g, ln2_b, loss_target, m_w_in, m_q_norm_g, m_w_uq, m_kv_norm_g, m_w_uk, m_w_uv, m_ret_gn_g, m_w_out, m_ln1_g, m_ln1_b, m_w_up, m_w_down, m_ln2_g, m_ln2_b, v_w_in, v_q_norm_g, v_w_uq, v_kv_norm_g, v_w_uk, v_w_uv, v_ret_gn_g, v_w_out, v_ln1_g, v_ln1_b, v_w_up, v_w_down, v_ln2_g, v_ln2_b):
    T, D = x.shape[1], x.shape[2]
    H = D // 256
    RW = H * RET_DV
    x = x[0]
    tgt = loss_target[0]
    alpha = DEEPNORM_ALPHA
    in_width = w_in.shape[2] * N_DEV
    mla_in = in_width - 4 * RW
    mla_in_pad = -(-mla_in // 512) * 512
    f_uq = MLA_NOPE + MLA_ROPE

    names = ["w_in", "w_uq", "w_uk", "w_uv", "w_out", "w_up", "w_down"]
    big = dict(zip(names, [w_in[0], w_uq[0], w_uk[0], w_uv[0], w_out[0], w_up[0], w_down[0]]))
    kind = dict(zip(names, ["blk", "blk", "cols", "cols", "rows", "cols", "rows"]))
    half = {n: w.astype(BF16) for n, w in big.items()}

    def gather(ns, mid, **parts):
        return _GatherJob([half[n] for n in ns], [kind[n] for n in ns], mid, **parts)

    (g_in,) = _standalone("gather_w_in", gather(["w_in"], 0.0))
    win = jnp.transpose(g_in, (1, 0, 2)).reshape(D, in_width)
    w1 = win[:, :4 * RW]
    w2 = jnp.pad(win[:, 4 * RW:], ((0, 0), (0, mla_in_pad - mla_in)))

    cos2, sin2, cosp, sina, sinb = _rope_tables(positions[0], T)
    xb = x.astype(BF16)
    tr = min(256, T)

    first, second, near, far = (0, 2), (1, 2), (CHIP_X, CHIP_Y), (CHIP_DIAGONAL,)
    p1, g_uq, wuk, wuv, wout, wup_a = _matmul(
        "proj_ret", xb, w1, jobs=[gather(["w_uq", "w_uk", "w_uv", "w_out"], 0.5),
                                  gather(["w_up"], 0.8, pieces=[(first, near, True)])])
    wuq = jnp.transpose(g_uq, (1, 0, 2)).reshape(MLA_Q_RANK, H, f_uq)
    wuq = jnp.pad(wuq, ((0, 0), (0, 0), (0, MLA_HEAD_PAD - f_uq))).reshape(MLA_Q_RANK, H * MLA_HEAD_PAD)
    wukv = jnp.concatenate([wuk, wuv], axis=1)
    p2 = _matmul("proj_mla", xb, w2)
    c_kv, c_kr = MLA_Q_RANK, MLA_Q_RANK + MLA_KV_RANK

    def mla_prep(p, cp, sa, sb, gq, gkv):
        cqn = _rms(p[:, :c_kv], gq)
        ckvn = _rms(p[:, c_kv:c_kr], gkv)
        return cqn, ckvn, _rope64(p[:, c_kr:c_kr + LANE], cp, sa, sb)

    cqn, ckvn, krope = _rowwise(
        "mla_prep", mla_prep, [(p2, mla_in_pad, 0), (cosp, LANE, 0), (sina, LANE, 0), (sinb, LANE, 0)],
        [q_norm_g, kv_norm_g], [(MLA_Q_RANK, BF16), (MLA_KV_RANK, BF16), (LANE, BF16)], [], tr)
    qf = _matmul("mla_q", cqn, wuq)
    kv = _matmul("mla_kv", ckvn, wukv, out_dtypes=(BF16,))

    def mla_heads(q, kn, vv, kr, cp, sa, sb):
        qs, ks, vs = [], [], []
        ones = jnp.ones((q.shape[0], MLA_DV), vv.dtype)
        for h in range(H):
            o = h * MLA_HEAD_PAD
            qs += [q[:, o:o + MLA_NOPE], _rope64(q[:, o + MLA_NOPE:o + MLA_HEAD_PAD], cp, sa, sb)]
            ks += [kn[:, h * MLA_NOPE:(h + 1) * MLA_NOPE], kr]
            vs += [vv[:, h * MLA_DV:(h + 1) * MLA_DV], ones]
        return jnp.concatenate(qs, axis=1), jnp.concatenate(ks, axis=1), jnp.concatenate(vs, axis=1)

    qh, kh, vx = _rowwise(
        "mla_heads", mla_heads,
        [(qf, H * MLA_HEAD_PAD, 0), (kv, RW, 0), (kv, RW, 1), (krope, LANE, 0), (cosp, LANE, 0), (sina, LANE, 0), (sinb, LANE, 0)],
        [], [(H * MLA_HEAD_PAD, BF16)] * 3, [], tr)
    att_o, att_ob, lse, wup_b, wdown_a = _attention_fwd(
        qh, kh, vx, H, jobs=[gather(["w_up"], 0.7, pieces=[(second, near, True), (first, far, False)], base=[wup_a]),
                             gather(["w_down"], 0.7, pieces=[(first, near, True)])])
    ret_pre, ret_ob, wup = _retention_fwd(
        p1, cos2, sin2, ret_gn_g, H, jobs=[gather(["w_up"], 0.7, pieces=[(second, far, False)], base=[wup_b])])
    mixin = jnp.concatenate([ret_ob, att_ob], axis=1)
    mix, wdown_b = _matmul("mix_out", mixin, wout,
                           jobs=[gather(["w_down"], 0.7, pieces=[(second, near, True)], base=[wdown_a])])

    def ln1_fwd(xt, mt, g, b):
        z = alpha * xt + mt
        xhat, _ = _ln_stats(z)
        y = xhat * g + b
        return z, y, y

    z1, x1, x1b = _rowwise("ln1", ln1_fwd, [(x, D, 0), (mix, D, 0)], [ln1_g, ln1_b],
                           [(D, F32), (D, F32), (D, BF16)], [], tr)

    def relu2(acc):
        r = jnp.maximum(acc, 0.0)
        return r * r, r

    a2b, rb, wdown = _matmul("mlp_up", x1b, wup, out_dtypes=(BF16, BF16), epilogue=relu2,
                             jobs=[gather(["w_down"], 0.8, pieces=[((0, 1), far, False)], base=[wdown_b])])
    hmlp = _matmul("mlp_down", a2b, wdown)

    def ln2_loss(x1t, ht, tt, g, b):
        xhat, rstd = _ln_stats(alpha * x1t + ht)
        err = xhat * g + b - tt
        dy = err / D
        dz = _ln_bwd(dy, xhat, rstd, g)
        lrow = 0.5 * jnp.mean(jnp.square(err), axis=-1, keepdims=True)
        return dz, dz, dy * xhat, dy, jnp.broadcast_to(lrow, (lrow.shape[0], LANE))

    dz2, dz2b, dg2, db2, lpart = _rowwise("ln2_loss", ln2_loss, [(x1, D, 0), (hmlp, D, 0), (tgt, D, 0)], [ln2_g, ln2_b],
                                          [(D, F32), (D, BF16)], [D, D, LANE], tr)
    loss = lax.psum(jnp.sum(lpart[:, 0]), ("x", "y", "c"))

    def pair(grads):
        ns = list(grads)
        return _PairJob([grads[n] for n in ns], [kind[n] for n in ns], [big[n].shape for n in ns])

    my_core = lax.axis_index("c").astype(jnp.int32).reshape(1)
    my_chip = (2 * lax.axis_index("x") + lax.axis_index("y")).astype(jnp.int32).reshape(1)
    chip_sums = {}

    def pair_sums(grads, theirs):
        for n, t in zip(grads, theirs):
            chip_sums[n] = _pair_sum("pair_sum_" + n, grads[n], kind[n], big[n].shape, t, my_core)
        return {n: chip_sums[n] for n in grads}

    def chip(sums, rels=CHIP_RELS, by_source=None):
        ns = list(sums)
        return _ChipJob([sums[n] for n in ns], [big[n].shape for n in ns], rels, by_source)

    landed = {}
    da = _matmul("mlp_down_dx", dz2b, wdown, tb=True, out_dtypes=(BF16,), extras=(rb,),
                 epilogue=lambda acc, r: (acc * (2.0 * r.astype(F32)),))
    g_down = {"w_down": _matmul("mlp_down_dw", a2b, dz2b, ta=True, out_dtypes=(BF16,))}
    g_wup, *pr = _matmul("mlp_up_dw", x1b, da, ta=True, out_dtypes=(BF16,), jobs=[pair(g_down)])
    sums_down = pair_sums(g_down, pr)
    g_up = {"w_up": g_wup}
    dx1m, l_a, *pr = _matmul("mlp_up_dx", da, wup, tb=True, jobs=[chip(sums_down, CHIP_RELS[:2]), pair(g_up)])
    sums_up = pair_sums(g_up, pr)

    def ln1_bwd(dm, dzz, z, g):
        xhat, rstd = _ln_stats(z)
        dy = dm + alpha * dzz
        dz = _ln_bwd(dy, xhat, rstd, g)
        return dz, dz, dy * xhat, dy

    dz1, dz1b, dg1, db1 = _rowwise("ln1_bwd", ln1_bwd, [(dx1m, D, 0), (dz2, D, 0), (z1, D, 0)], [ln1_g],
                                   [(D, F32), (D, BF16)], [D, D], tr)
    g_wout = _matmul("mix_out_dw", mixin, dz1b, ta=True, out_dtypes=(BF16,))
    dmix = _matmul("mix_out_dx", dz1b, wout, tb=True)

    dqh, dkn, dkr, dvv, l_b, l_c = _attention_bwd(qh, kh, vx, att_o, lse, dmix, H,
                                                  jobs=[chip(sums_down, CHIP_RELS[2:]), chip(sums_up)])
    landed["w_down"], landed["w_up"] = [l_a, l_b], [l_c]
    att_scale = (MLA_NOPE + MLA_ROPE) ** -0.5

    def mla_heads_bwd(dq, dkrh, cp, sa, sb):
        parts, acc = [], dkrh[:, :LANE]
        dq = dq * att_scale
        for h in range(H):
            o = h * MLA_HEAD_PAD
            parts += [dq[:, o:o + MLA_NOPE], _unrope64(dq[:, o + MLA_NOPE:o + MLA_HEAD_PAD], cp, sa, sb)]
            if h:
                acc = acc + dkrh[:, h * LANE:(h + 1) * LANE]
        return jnp.concatenate(parts, axis=1), _unrope64(acc, cp, sa, sb)

    dqb, dkr128 = _rowwise(
        "mla_heads_bwd", mla_heads_bwd,
        [(dqh, H * MLA_HEAD_PAD, 0), (dkr, H * LANE, 0), (cosp, LANE, 0), (sina, LANE, 0), (sinb, LANE, 0)],
        [], [(H * MLA_HEAD_PAD, BF16), (LANE, F32)], [], tr)
    dcqn = _matmul("mla_q_dx", dqb, wuq, tb=True)
    g_wuq = _matmul("mla_q_dw", cqn, dqb, ta=True, out_dtypes=(BF16,))
    dkvb = jnp.concatenate([dkn, dvv], axis=1)
    dckvn = _matmul("mla_kv_dx", dkvb, wukv, tb=True)
    g_wukv = _matmul("mla_kv_dw", ckvn, dkvb, ta=True, out_dtypes=(BF16,))

    def rms_bwd(c, dy, g):
        rstd = lax.rsqrt(jnp.mean(jnp.square(c), axis=-1, keepdims=True) + EPS)
        dyg = dy * g
        dc = rstd * (dyg - c * (rstd * rstd) * jnp.mean(dyg * c, axis=-1, keepdims=True))
        return dc, dy * c * rstd

    def mla_prep_bwd(p, dq_, dkv_, dkr_, gq, gkv):
        dcq, dgq_ = rms_bwd(p[:, :c_kv], dq_, gq)
        dckv, dgkv_ = rms_bwd(p[:, c_kv:c_kr], dkv_, gkv)
        pad = jnp.zeros((p.shape[0], mla_in_pad - c_kr - LANE), F32)
        return jnp.concatenate([dcq, dckv, dkr_, pad], axis=1), dgq_, dgkv_

    d_p2, dgq, dgkv = _rowwise(
        "mla_prep_bwd", mla_prep_bwd, [(p2, mla_in_pad, 0), (dcqn, MLA_Q_RANK, 0), (dckvn, MLA_KV_RANK, 0), (dkr128, LANE, 0)],
        [q_norm_g, kv_norm_g], [(mla_in_pad, BF16)], [MLA_Q_RANK, MLA_KV_RANK], tr)

    def gate_bwd(pre, gate, dout, gn):
        dpre, dgate, dgn = [], [], []
        for h in range(H):
            sl = slice(h * RET_DV, (h + 1) * RET_DV)
            nrm, rstd = _ln_stats(pre[:, sl])
            gate_h, do_h, gn_h = gate[:, sl], dout[:, sl], gn[:, sl]
            sg = jax.nn.sigmoid(gate_h)
            silu = gate_h * sg
            dgate.append(do_h * (nrm * gn_h) * (sg * (1.0 + gate_h * (1.0 - sg))))
            dn = do_h * silu
            dgn.append(dn * nrm)
            dpre.append(_ln_bwd(dn, nrm, rstd, gn_h))
        return jnp.concatenate(dpre, axis=1), jnp.concatenate(dgate, axis=1), jnp.concatenate(dgn, axis=1)

    dpre, drg, dgn = _rowwise("ret_gate_bwd", gate_bwd, [(ret_pre, RW, 0), (p1, RW, 3), (dmix, RW, 0)], [ret_gn_g],
                              [(RW, F32), (RW, BF16)], [RW], tr)
    g_uq = g_wuq.reshape(MLA_Q_RANK, H, MLA_HEAD_PAD)[:, :, :f_uq].reshape(MLA_Q_RANK, N_DEV, H * f_uq // N_DEV)
    mid_grads = {"w_uq": jnp.transpose(g_uq, (1, 0, 2)), "w_uk": g_wukv[:, :RW], "w_uv": g_wukv[:, RW:], "w_out": g_wout}
    (drq, *pr), (drk, drv) = _retention_bwd(p1, dpre, cos2, sin2, H, dq_jobs=[pair(mid_grads)])
    sums_mid = pair_sums(mid_grads, pr)
    d_p1 = jnp.concatenate([drq, drk, drv, drg], axis=1)

    g_w1, *l_a = _matmul("proj_ret_dw", xb, d_p1, ta=True, out_dtypes=(BF16,), jobs=[chip(sums_mid)])
    for n, a_ in zip(mid_grads, l_a):
        landed[n] = [a_]
    g_w2 = _matmul("proj_mla_dw", xb, d_p2, ta=True, out_dtypes=(BF16,))

    g_win = jnp.concatenate([g_w1, g_w2[:, :mla_in]], axis=1)
    g_win = {"w_in": jnp.transpose(g_win.reshape(D, N_DEV, in_width // N_DEV), (1, 0, 2))}
    small = jnp.concatenate([dgq, dgkv, dgn, dg1, db1, dg2, db2], axis=1)
    dx_a, *pr = _matmul("proj_mla_dx", d_p2, w2, tb=True, extras=(dz1,), epilogue=lambda acc, d: (acc + alpha * d,),
                        jobs=[pair(g_win)])
    sums_in = pair_sums(g_win, pr)
    grad_x, l_a, small_landed = _matmul("proj_ret_dx", d_p1, w1, tb=True, extras=(dx_a,), epilogue=lambda acc, d: (acc + d,),
                                        jobs=[chip(sums_in, by_source=small)])
    landed["w_in"] = [l_a]

    moments_m = [m_w_in, m_w_uq, m_w_uk, m_w_uv, m_w_out, m_w_up, m_w_down]
    moments_v = [v_w_in, v_w_uq, v_w_uk, v_w_uv, v_w_out, v_w_up, v_w_down]
    res = {}
    for n, m, v in zip(names, moments_m, moments_v):
        res[n] = [r[None] for r in _adamw("adamw_" + n, chip_sums[n], my_chip, landed[n], big[n], m[0], v[0])]

    small_names = ["q_norm_g", "kv_norm_g", "ret_gn_g", "ln1_g", "ln1_b", "ln2_g", "ln2_b"]
    small_w = [q_norm_g, kv_norm_g, ret_gn_g, ln1_g, ln1_b, ln2_g, ln2_b]
    small_m = [m_q_norm_g, m_kv_norm_g, m_ret_gn_g, m_ln1_g, m_ln1_b, m_ln2_g, m_ln2_b]
    small_v = [v_q_norm_g, v_kv_norm_g, v_ret_gn_g, v_ln1_g, v_ln1_b, v_ln2_g, v_ln2_b]
    cat = lambda arrs: jnp.concatenate(arrs, axis=1)
    sres = _adamw_small(small_landed, cat(small_w), cat(small_m), cat(small_v))
    off = 0
    for n, w in zip(small_names, small_w):
        res[n] = [r[:, off:off + w.shape[1]] for r in sres]
        off += w.shape[1]

    order = ["w_in", "q_norm_g", "w_uq", "kv_norm_g", "w_uk", "w_uv", "ret_gn_g", "w_out", "ln1_g", "ln1_b",
             "w_up", "w_down", "ln2_g", "ln2_b"]
    outs = [loss, grad_x[None]]
    for k in range(4):
        outs += [res[n][k] for n in order]
    return tuple(outs)
```

```python
import functools

import numpy as np

import jax
import jax.numpy as jnp
from jax import lax
from jax.experimental import pallas as pl
from jax.experimental.pallas import tpu as pltpu

F32 = jnp.float32
BF16 = jnp.bfloat16

CHUNK = 64
RET_DK = 128
RET_DV = 128
MLA_NOPE = 128
MLA_ROPE = 64
MLA_DV = 128
MLA_Q_RANK = 768
MLA_KV_RANK = 512
MLA_HEAD_PAD = 256
ROPE_BASE = 10000.0
EPS = 1e-5
DEPTH = 1
DEEPNORM_ALPHA = (2.0 * DEPTH) ** 0.25
ADAM_LR = 0.001
ADAM_B1 = 0.9
ADAM_B2 = 0.999
ADAM_EPS = 1e-08
ADAM_WD = 0.01
ADAM_STEP = 10

N_DEV = 8
LANE = 128
SUBLANE = 8
VMEM_LIMIT = 48 * 1024 * 1024
ADAMW_TILE_ELEMS = 128 * 1024
PAIR_TILE_ELEMS = 1024 * 1024

MESH = pl.DeviceIdType.MESH
ANY = pl.BlockSpec(memory_space=pl.ANY)

NN = (((1,), (0,)), ((), ()))
NT = (((1,), (1,)), ((), ()))
TN = (((0,), (0,)), ((), ()))


def _dot(a, b, dims=NN):
    return lax.dot_general(a.astype(BF16), b.astype(BF16), dims, preferred_element_type=F32)


def _params(n_axes):
    return pltpu.CompilerParams(dimension_semantics=("arbitrary",) * n_axes, vmem_limit_bytes=VMEM_LIMIT)


def _tile(n, want):
    if n <= want:
        return n
    t = want
    while t >= LANE:
        if n % t == 0:
            return t
        t -= LANE
    return n


def _my_place():
    return lax.axis_index("x"), lax.axis_index("y"), lax.axis_index("c")


def _flat(px, py, pc):
    return 4 * px + 2 * py + pc


def _block_of(ref, kind, idx, rows, cols, sub=None):
    lo, n = (0, rows) if sub is None else sub
    if kind == "blk":
        return ref.at[idx, pl.ds(lo, n), :]
    if kind == "rows":
        return ref.at[pl.ds(pl.multiple_of(idx * rows + lo, n), n), :]
    return ref.at[pl.ds(lo, n), pl.ds(pl.multiple_of(idx * cols, cols), cols)]


CHIP_X, CHIP_Y, CHIP_DIAGONAL = 0, 1, 2
ALL_CHIPS = (CHIP_X, CHIP_Y, CHIP_DIAGONAL)


class _GatherJob:
    def __init__(self, shards, kinds, mid, pieces=(((0, 1), ALL_CHIPS, True),), base=None):
        nw = len(shards)
        self.nw, self.kinds, self.pieces = nw, list(kinds), list(pieces)
        self.shapes = [s.shape for s in shards]
        self.ins = list(shards) + (list(base) if base is not None else [])
        self.alias = [(nw + w, w) for w in range(nw)] if base is not None else []
        self.out_shape = []
        for s, kind in zip(shards, kinds):
            r, c = s.shape
            shape = {"blk": (N_DEV, r, c), "rows": (N_DEV * r, c), "cols": (r, N_DEV * c)}[kind]
            self.out_shape.append(jax.ShapeDtypeStruct(shape, s.dtype))
        n = 7 * nw * len(self.pieces)
        self.sems = [pltpu.SemaphoreType.DMA((n,)), pltpu.SemaphoreType.DMA((n,)), pltpu.SemaphoreType.DMA((n,))]
        self.phases = [(0.0, self._start), (mid, self._forward), (1.0, self._finish)]

    def _each(self, src, dst, send_sems, recv_sems, local_sems):
        x, y, c = _my_place()
        all_chips = [(1 - x, y), (x, 1 - y), (1 - x, 1 - y)]
        out = []
        for w in range(self.nw):
            r, cc = self.shapes[w]
            for p, ((k_of, of), chips, own) in enumerate(self.pieces):
                sub = (k_of * (r // of), r // of)
                s0 = 7 * (w * len(self.pieces) + p)
                mine = src[w].at[pl.ds(*sub), :]

                def place(dev, w=w, r=r, cc=cc, sub=sub):
                    return _block_of(dst[w], self.kinds[w], _flat(*dev), r, cc, sub)

                def copy(k, block, to, from_src=False, place=place, mine=mine, s0=s0):
                    return pltpu.make_async_remote_copy(
                        src_ref=mine if from_src else place(block), dst_ref=place(block),
                        send_sem=send_sems.at[s0 + k], recv_sem=recv_sems.at[s0 + k], device_id=to, device_id_type=MESH)

                local = pltpu.make_async_copy(mine, place((x, y, c)), local_sems.at[s0])
                out.append(([(j, all_chips[j]) for j in chips], own, copy, local))
        return (x, y, c), (x, y, 1 - c), out

    def _start(self, *refs):
        me, sibling, each = self._each(*refs)
        for chips, own, copy, local in each:
            if own:
                local.start()
                copy(0, me, sibling, from_src=True).start()
            for j, chip in chips:
                copy(1 + j, me, (*chip, me[2]), from_src=True).start()

    def _forward(self, *refs):
        me, sibling, each = self._each(*refs)
        for chips, own, copy, local in each:
            for j, chip in chips:
                copy(1 + j, (*chip, me[2]), me).wait_recv()
                copy(4 + j, (*chip, me[2]), sibling).start()

    def _finish(self, *refs):
        me, sibling, each = self._each(*refs)
        for chips, own, copy, local in each:
            if own:
                copy(0, sibling, me).wait_recv()
            for j, chip in chips:
                copy(4 + j, (*chip, 1 - me[2]), me).wait_recv()
        for chips, own, copy, local in each:
            if own:
                copy(0, me, sibling, from_src=True).wait_send()
                local.wait()
            for j, chip in chips:
                copy(1 + j, me, (*chip, me[2]), from_src=True).wait_send()
                copy(4 + j, (*chip, me[2]), sibling).wait_send()


REL_ALL = [(1, 0, 0), (1, 0, 1), (0, 1, 0), (0, 1, 1), (1, 1, 0), (1, 1, 1), (0, 0, 1)]
N_CHIP = 4


class _PairJob:
    def __init__(self, grads, kinds, shard_shapes):
        self.ins = list(grads)
        self.kinds, self.shard_shapes, self.nw = list(kinds), list(shard_shapes), len(grads)
        self.out_shape = [jax.ShapeDtypeStruct((N_CHIP,) + tuple(s), g.dtype) for g, s in zip(grads, shard_shapes)]
        n = N_CHIP * self.nw
        self.sems = [pltpu.SemaphoreType.DMA((n,)), pltpu.SemaphoreType.DMA((n,))]
        self.phases = [(0.0, self._start), (1.0, self._finish)]

    def _copies(self, src, theirs, send_sems, recv_sems):
        x, y, c = _my_place()
        remote = []
        for w in range(self.nw):
            r, cc = self.shard_shapes[w]
            for q in range(N_CHIP):
                s = N_CHIP * w + q
                remote.append(pltpu.make_async_remote_copy(
                    src_ref=_block_of(src[w], self.kinds[w], 2 * q + (1 - c), r, cc), dst_ref=theirs[w].at[q],
                    send_sem=send_sems.at[s], recv_sem=recv_sems.at[s], device_id=(x, y, 1 - c), device_id_type=MESH))
        return remote

    def _start(self, *refs):
        for cp in self._copies(*refs):
            cp.start()

    def _finish(self, *refs):
        for cp in self._copies(*refs):
            cp.wait()


CHIP_RELS = [(1, 0), (0, 1), (1, 1)]


class _ChipJob:
    def __init__(self, chip_sums, shard_shapes, rels=CHIP_RELS, by_source=None):
        self.ins = list(chip_sums) + ([by_source] if by_source is not None else [])
        self.nw, self.has_src, self.rels = len(chip_sums), by_source is not None, list(rels)
        self.out_shape = [jax.ShapeDtypeStruct((len(rels),) + tuple(s), a.dtype) for a, s in zip(chip_sums, shard_shapes)]
        if self.has_src:
            self.out_shape.append(jax.ShapeDtypeStruct((N_DEV,) + by_source.shape, by_source.dtype))
        n = len(rels) * self.nw + len(REL_ALL) * int(self.has_src)
        self.sems = [pltpu.SemaphoreType.DMA((n,)), pltpu.SemaphoreType.DMA((n,)), pltpu.SemaphoreType.DMA((1,))]
        self.phases = [(0.0, self._start), (1.0, self._finish)]

    def _copies(self, src, land, send_sems, recv_sems, local_sems):
        x, y, c = _my_place()
        nw = self.nw

        def remote_copy(s, src_ref, dst_ref, peer):
            return pltpu.make_async_remote_copy(src_ref=src_ref, dst_ref=dst_ref, send_sem=send_sems.at[s],
                                                recv_sem=recv_sems.at[s], device_id=peer, device_id_type=MESH)

        local, remote = [], []
        for k, (fx, fy) in enumerate(self.rels):
            px, py = x ^ fx, y ^ fy
            remote += [remote_copy(nw * k + w, src[w].at[2 * px + py], land[w].at[k], (px, py, c)) for w in range(nw)]
        if self.has_src:
            my_idx = _flat(x, y, c)
            local.append(pltpu.make_async_copy(src[nw], land[nw].at[my_idx], local_sems.at[0]))
            remote += [remote_copy(nw * len(self.rels) + k, src[nw], land[nw].at[my_idx], (x ^ fx, y ^ fy, c ^ fc))
                       for k, (fx, fy, fc) in enumerate(REL_ALL)]
        return local, remote

    def _start(self, *refs):
        local, remote = self._copies(*refs)
        for cp in local + remote:
            cp.start()

    def _finish(self, *refs):
        local, remote = self._copies(*refs)
        for cp in remote + local:
            cp.wait()


def _row_tile(R, C, elems=ADAMW_TILE_ELEMS):
    tr = 16
    while tr * 2 * C <= elems and R % (tr * 2) == 0:
        tr *= 2
    return tr if R % tr == 0 else R


def _part_spec(kind, tr, C, R, n_lead):
    def dev(ids, pref):
        return 2 * ids[0] + pref[0] if n_lead == 2 else pref[0]

    def index(*args):
        ids, pref = args[:n_lead], args[n_lead]
        i = ids[-1]
        if kind == "blk":
            return dev(ids, pref), i, 0
        if kind == "rows":
            return dev(ids, pref) * (R // tr) + i, 0
        return i, dev(ids, pref)

    return pl.BlockSpec((None, tr, C) if kind == "blk" else (tr, C), index)


def _pair_sum(name, grad, kind, shard_shape, theirs, c):
    R, C = shard_shape
    tr = _row_tile(R, C, PAIR_TILE_ELEMS)

    def body(c_ref, g_ref, t_ref, o_ref):
        o_ref[...] = (g_ref[...].astype(F32) + t_ref[...].astype(F32)).astype(o_ref.dtype)

    slot = pl.BlockSpec((None, tr, C), lambda q, i, c_ref: (q, i, 0))
    return pl.pallas_call(
        body, name=name,
        grid_spec=pltpu.PrefetchScalarGridSpec(
            num_scalar_prefetch=1, grid=(N_CHIP, R // tr),
            in_specs=[_part_spec(kind, tr, C, R, 2), slot], out_specs=slot),
        out_shape=jax.ShapeDtypeStruct(theirs.shape, theirs.dtype), compiler_params=_params(2))(c, grad, theirs)


class _Carried:
    def __init__(self, jobs, grid):
        self.jobs = list(jobs)
        self.steps = int(np.prod(grid))
        self.grid = tuple(grid)
        self.operands = [a for j in self.jobs for a in j.ins]
        self.out_shape = [o for j in self.jobs for o in j.out_shape]
        self.sems = [s for j in self.jobs for s in j.sems]
        self.in_specs = [ANY] * len(self.operands)
        self.out_specs = [ANY] * len(self.out_shape)

    def aliases(self, n_in, n_out):
        out, i0, o0 = {}, n_in, n_out
        for job in self.jobs:
            for ji, jo in getattr(job, "alias", []):
                out[i0 + ji] = o0 + jo
            i0, o0 = i0 + len(job.ins), o0 + len(job.out_shape)
        return out

    def _step(self):
        lin = pl.program_id(0)
        for a in range(1, len(self.grid)):
            lin = lin * self.grid[a] + pl.program_id(a)
        return lin

    def _run(self, in_refs, out_refs, sem_refs, last):
        lin = self._step()
        for job in self.jobs:
            ins, in_refs = in_refs[:len(job.ins)], in_refs[len(job.ins):]
            outs, out_refs = out_refs[:len(job.out_shape)], out_refs[len(job.out_shape):]
            sems, sem_refs = sem_refs[:len(job.sems)], sem_refs[len(job.sems):]
            for frac, fn in job.phases:
                if (frac >= 1.0) != last:
                    continue
                at = min(self.steps - 1, int(frac * (self.steps - 1)))
                pl.when(lin == at)(functools.partial(fn, ins, outs, *sems))

    def before(self, in_refs, out_refs, sem_refs):
        self._run(in_refs, out_refs, sem_refs, last=False)

    def after(self, in_refs, out_refs, sem_refs):
        self._run(in_refs, out_refs, sem_refs, last=True)


def _split_refs(refs, n_in, n_job_in, n_out, n_job_out, n_scratch):
    cuts = np.cumsum([0, n_in, n_job_in, n_out, n_job_out, n_scratch])
    return [refs[cuts[i]:cuts[i + 1]] for i in range(5)] + [refs[cuts[5]:]]


def _matmul(name, a, b, *, ta=False, tb=False, out_dtypes=(F32,), epilogue=None, extras=(), tm=1024, tn=1024, tk=2048, jobs=()):
    if ta:
        K, M = a.shape
    else:
        M, K = a.shape
    N = b.shape[0] if tb else b.shape[1]
    tm, tn, tk = _tile(M, tm), _tile(N, tn), _tile(K, tk)
    nk = K // tk
    grid = (M // tm, N // tn, nk)
    dims = TN if ta else (NT if tb else NN)
    n_ex, n_out = len(extras), len(out_dtypes)
    carried = _Carried(jobs, grid)

    def body(*refs):
        ins, job_ins, out_refs, job_outs, scratch, job_sems = _split_refs(
            refs, 2 + n_ex, len(carried.operands), n_out, len(carried.out_shape), int(nk > 1))
        a_ref, b_ref, ex_refs = ins[0], ins[1], ins[2:]
        k = pl.program_id(2)
        carried.before(job_ins, job_outs, job_sems)

        def finish(acc):
            res = (acc,) if epilogue is None else epilogue(acc, *[e[...] for e in ex_refs])
            for o_ref, r in zip(out_refs, res):
                o_ref[...] = r.astype(o_ref.dtype)

        if nk == 1:
            finish(_dot(a_ref[...], b_ref[...], dims))
        else:
            acc_ref = scratch[0]

            @pl.when(k == 0)
            def _():
                acc_ref[...] = _dot(a_ref[...], b_ref[...], dims)

            @pl.when(k > 0)
            def _():
                acc_ref[...] += _dot(a_ref[...], b_ref[...], dims)

            pl.when(k == nk - 1)(lambda: finish(acc_ref[...]))

        carried.after(job_ins, job_outs, job_sems)

    a_spec = pl.BlockSpec((tk, tm), lambda i, j, k: (k, i)) if ta else pl.BlockSpec((tm, tk), lambda i, j, k: (i, k))
    b_spec = pl.BlockSpec((tn, tk), lambda i, j, k: (j, k)) if tb else pl.BlockSpec((tk, tn), lambda i, j, k: (k, j))
    tile_spec = pl.BlockSpec((tm, tn), lambda i, j, k: (i, j))
    outs = pl.pallas_call(
        body,
        name=name,
        grid=grid,
        in_specs=[a_spec, b_spec] + [tile_spec] * n_ex + carried.in_specs,
        out_specs=[tile_spec] * n_out + carried.out_specs,
        out_shape=[jax.ShapeDtypeStruct((M, N), d) for d in out_dtypes] + carried.out_shape,
        scratch_shapes=[pltpu.VMEM((tm, tn), F32)] * int(nk > 1) + carried.sems,
        input_output_aliases=carried.aliases(2 + n_ex, n_out),
        compiler_params=_params(3),
    )(a, b, *extras, *carried.operands)
    return outs[0] if len(outs) == 1 else outs


def _rowwise(name, fn, row_in, const_in, row_out, acc_out, tm, jobs=()):
    T = row_in[0][0].shape[0]
    n_r, n_c, n_o, n_a = len(row_in), len(const_in), len(row_out), len(acc_out)
    carried = _Carried(jobs, (T // tm,))

    def body(*refs):
        in_refs, job_ins, own_outs, job_outs, _, job_sems = _split_refs(
            refs, n_r + n_c, len(carried.operands), n_o + n_a, len(carried.out_shape), 0)
        out_refs, acc_refs = own_outs[:n_o], own_outs[n_o:]
        carried.before(job_ins, job_outs, job_sems)
        res = fn(*[r[...] for r in in_refs])
        for o_ref, r in zip(out_refs, res[:n_o]):
            o_ref[...] = r.astype(o_ref.dtype)

        @pl.when(pl.program_id(0) == 0)
        def _():
            for a_ref in acc_refs:
                a_ref[...] = jnp.zeros_like(a_ref)

        for a_ref, r in zip(acc_refs, res[n_o:]):
            a_ref[...] += jnp.sum(r.reshape(tm // SUBLANE, SUBLANE, r.shape[-1]), axis=0)
        carried.after(job_ins, job_outs, job_sems)

    in_specs = [pl.BlockSpec((tm, w), functools.partial(lambda i, c: (i, c), c=cb)) for (_, w, cb) in row_in]
    in_specs += [pl.BlockSpec(c.shape, functools.partial(lambda i, nd: (0,) * nd, nd=c.ndim)) for c in const_in]
    out_specs = [pl.BlockSpec((tm, c), lambda i: (i, 0)) for (c, _) in row_out]
    out_specs += [pl.BlockSpec((SUBLANE, c), lambda i: (0, 0)) for c in acc_out]
    out_shape = [jax.ShapeDtypeStruct((T, c), d) for (c, d) in row_out]
    out_shape += [jax.ShapeDtypeStruct((SUBLANE, c), F32) for c in acc_out]
    return pl.pallas_call(
        body, name=name, grid=(T // tm,), in_specs=in_specs + carried.in_specs, out_specs=out_specs + carried.out_specs,
        out_shape=out_shape + carried.out_shape, scratch_shapes=carried.sems,
        input_output_aliases=carried.aliases(n_r + n_c, n_o + n_a), compiler_params=_params(1),
    )(*[r[0] for r in row_in], *const_in, *carried.operands)


def _ln_stats(z):
    mu = jnp.mean(z, axis=-1, keepdims=True)
    var = jnp.mean(jnp.square(z - mu), axis=-1, keepdims=True)
    rstd = lax.rsqrt(var + EPS)
    return (z - mu) * rstd, rstd


def _ln_bwd(dy, xhat, rstd, g):
    dxh = dy * g
    return rstd * (dxh - jnp.mean(dxh, axis=-1, keepdims=True) - xhat * jnp.mean(dxh * xhat, axis=-1, keepdims=True))


def _rms(x, g):
    return x * lax.rsqrt(jnp.mean(jnp.square(x), axis=-1, keepdims=True) + EPS) * g


def _rope128(x, cos2, sin2):
    return x * cos2 + pltpu.roll(x, RET_DK // 2, 1) * sin2


def _unrope128(dy, cos2, sin2):
    return dy * cos2 + pltpu.roll(dy * sin2, RET_DK // 2, 1)


def _rope64(x, cosp, sina, sinb):
    h = MLA_ROPE // 2
    return x * cosp + pltpu.roll(x, h, 1) * sina + pltpu.roll(x, LANE - h, 1) * sinb


def _unrope64(dy, cosp, sina, sinb):
    h = MLA_ROPE // 2
    return dy * cosp + pltpu.roll(dy * sina, LANE - h, 1) + pltpu.roll(dy * sinb, h, 1)


RET_BLOCK = 256
RET_HEADS_PER_STEP = 4


def _ret_tables(H, B):
    log_g = np.log1p(-np.exp2(-5.0 - np.arange(H, dtype=np.float32))).astype(np.float32)
    idx = np.arange(B, dtype=np.float32)
    dist = np.abs(idx[:, None] - idx[None, :])
    visible = (np.arange(B)[None, :] // CHUNK) <= (np.arange(B)[:, None] // CHUNK)
    dmat = np.exp(log_g[:, None, None] * dist).astype(np.float32) * visible[None].astype(np.float32)
    qd = np.exp(log_g[:, None] * (idx + 1.0)).astype(np.float32)
    kd = np.exp(log_g[:, None] * (B - 1 - idx)).astype(np.float32)
    sd = np.exp(log_g * B).astype(np.float32)
    bc = lambda v: np.ascontiguousarray(np.broadcast_to(v[:, :, None], (H, v.shape[1], LANE)))
    sdb = np.ascontiguousarray(np.broadcast_to(sd[:, None, None], (H, SUBLANE, LANE)))
    return jnp.asarray(dmat), jnp.asarray(bc(qd)), jnp.asarray(bc(kd)), jnp.asarray(sdb)


def _ret_specs(H, B, reverse, nb):
    blk = (lambda b: nb - 1 - b) if reverse else (lambda b: b)
    G = min(RET_HEADS_PER_STEP, H)
    W = G * LANE
    col = lambda off: pl.BlockSpec((B, W), functools.partial(lambda h, b, off: (blk(b), off // G + h), off=off))
    row = pl.BlockSpec((B, LANE), lambda h, b: (blk(b), 0))
    per_head = lambda r: pl.BlockSpec((G, r, LANE), lambda h, b: (h, 0, 0))
    dmat = pl.BlockSpec((G, B, B), lambda h, b: (h, 0, 0))
    return G, col, row, per_head, dmat


def _ret_call(name, body, n_in, n_out, operands, in_specs, out_specs, out_shape, grid, G, jobs):
    carried = _Carried(jobs, grid)

    def wrapped(*refs):
        ins, job_ins, outs, job_outs, (state,), job_sems = _split_refs(
            refs, n_in, len(carried.operands), n_out, len(carried.out_shape), 1)
        carried.before(job_ins, job_outs, job_sems)

        @pl.when(pl.program_id(1) == 0)
        def _():
            state[...] = jnp.zeros_like(state)

        body(*ins, *outs, state)
        carried.after(job_ins, job_outs, job_sems)

    return pl.pallas_call(
        wrapped, name=name, grid=grid,
        in_specs=in_specs + carried.in_specs, out_specs=out_specs + carried.out_specs,
        out_shape=out_shape + carried.out_shape,
        scratch_shapes=[pltpu.VMEM((G, RET_DK, RET_DV), F32)] + carried.sems,
        input_output_aliases=carried.aliases(n_in, n_out),
        compiler_params=_params(2),
    )(*operands, *carried.operands)


def _retention_fwd(p1, cos2, sin2, gn_g, H, jobs=()):
    T = p1.shape[0]
    B = min(RET_BLOCK, T)
    nb = T // B
    scale = RET_DK ** -0.5
    dmat, qd, kd, sd = _ret_tables(H, B)
    G, col, row, per_head, dspec = _ret_specs(H, B, False, nb)

    def body(q_ref, k_ref, v_ref, g_ref, cos_ref, sin_ref, d_ref, qd_ref, kd_ref, sd_ref, gn_ref, pre_ref, out_ref, s_ref):
        cos, sin = cos_ref[...], sin_ref[...]
        for g in range(G):
            hl = pl.ds(g * LANE, LANE)
            q = _rope128(q_ref[:, hl], cos, sin)
            k = _rope128(k_ref[:, hl], cos, sin) * scale
            v = v_ref[:, hl]
            s = s_ref[g]
            a = _dot(q, k, NT) * d_ref[g]
            ret = _dot(a, v) + _dot(q * qd_ref[g], s)
            s_ref[g] = sd_ref[g, 0:1, :] * s + _dot(k * kd_ref[g], v, TN)
            pre_ref[:, hl] = ret
            nrm, _ = _ln_stats(ret)
            out_ref[:, hl] = (jax.nn.silu(g_ref[:, hl]) * (nrm * gn_ref[:, hl])).astype(out_ref.dtype)

    out_spec = pl.BlockSpec((B, G * LANE), lambda h, b: (b, h))
    return _ret_call(
        "retention_fwd", body, 11, 2, (p1, p1, p1, p1, cos2, sin2, dmat, qd, kd, sd, gn_g),
        [col(0), col(H), col(2 * H), col(3 * H), row, row, dspec, per_head(B), per_head(B), per_head(SUBLANE),
         pl.BlockSpec((1, G * LANE), lambda h, b: (0, h))],
        [out_spec, out_spec],
        [jax.ShapeDtypeStruct((T, H * RET_DV), F32), jax.ShapeDtypeStruct((T, H * RET_DV), BF16)],
        (H // G, nb), G, jobs)


def _retention_bwd(p1, dpre, cos2, sin2, H, dq_jobs=(), dkv_jobs=()):
    T = p1.shape[0]
    B = min(RET_BLOCK, T)
    nb = T // B
    scale = RET_DK ** -0.5
    dmat, qd, kd, sd = _ret_tables(H, B)

    def dq_body(k_ref, v_ref, do_ref, cos_ref, sin_ref, d_ref, qd_ref, kd_ref, sd_ref, dq_ref, s_ref):
        cos, sin = cos_ref[...], sin_ref[...]
        for g in range(G):
            hl = pl.ds(g * LANE, LANE)
            k = _rope128(k_ref[:, hl], cos, sin) * scale
            v, do, s = v_ref[:, hl], do_ref[:, hl], s_ref[g]
            da = _dot(do, v, NT) * d_ref[g]
            dq = _dot(da, k) + _dot(do, s, NT) * qd_ref[g]
            dq_ref[:, hl] = _unrope128(dq, cos, sin).astype(dq_ref.dtype)
            s_ref[g] = sd_ref[g, 0:1, :] * s + _dot(k * kd_ref[g], v, TN)

    G, col, row, per_head, dspec = _ret_specs(H, B, False, nb)
    dq = _ret_call(
        "retention_bwd_dq", dq_body, 9, 1, (p1, p1, dpre, cos2, sin2, dmat, qd, kd, sd),
        [col(H), col(2 * H), col(0), row, row, dspec, per_head(B), per_head(B), per_head(SUBLANE)],
        [pl.BlockSpec((B, G * LANE), lambda h, b: (b, h))], [jax.ShapeDtypeStruct((T, H * RET_DK), BF16)],
        (H // G, nb), G, dq_jobs)

    def dkv_body(q_ref, k_ref, v_ref, do_ref, cos_ref, sin_ref, d_ref, qd_ref, kd_ref, sd_ref, dk_ref, dv_ref, g_ref):
        cos, sin = cos_ref[...], sin_ref[...]
        for g in range(G):
            hl = pl.ds(g * LANE, LANE)
            q = _rope128(q_ref[:, hl], cos, sin)
            k = _rope128(k_ref[:, hl], cos, sin) * scale
            v, do, st = v_ref[:, hl], do_ref[:, hl], g_ref[g]
            a = _dot(q, k, NT) * d_ref[g]
            da = _dot(do, v, NT) * d_ref[g]
            dv = _dot(a, do, TN) + _dot(k * kd_ref[g], st)
            dk = _dot(da, q, TN) + _dot(v, st, NT) * kd_ref[g]
            dv_ref[:, hl] = dv.astype(dv_ref.dtype)
            dk_ref[:, hl] = _unrope128(dk * scale, cos, sin).astype(dk_ref.dtype)
            g_ref[g] = sd_ref[g, 0:1, :] * st + _dot(q * qd_ref[g], do, TN)

    G, col, row, per_head, dspec = _ret_specs(H, B, True, nb)
    out_spec = pl.BlockSpec((B, G * LANE), lambda h, b: (nb - 1 - b, h))
    dkv = _ret_call(
        "retention_bwd_dkv", dkv_body, 10, 2, (p1, p1, p1, dpre, cos2, sin2, dmat, qd, kd, sd),
        [col(0), col(H), col(2 * H), col(0), row, row, dspec, per_head(B), per_head(B), per_head(SUBLANE)],
        [out_spec, out_spec],
        [jax.ShapeDtypeStruct((T, H * RET_DK), BF16), jax.ShapeDtypeStruct((T, H * RET_DV), BF16)],
        (H // G, nb), G, dkv_jobs)
    return dq, dkv


ATT_BLOCK = 512


ATT_ROW_CHUNKS = 2
ATT_HEADS_PER_STEP = 8
ATT_BWD_HEADS_PER_STEP = 2
LOG2E = 1.4426950408889634


def _chunk_mask(rows, row0, cols):
    qi = (lax.broadcasted_iota(jnp.int32, (rows, cols), 0) + row0) // CHUNK
    kj = lax.broadcasted_iota(jnp.int32, (rows, cols), 1) // CHUNK
    return kj <= qi


def _attention_fwd(qh, kh, vx, H, jobs=()):
    T = qh.shape[0]
    t = min(ATT_BLOCK, T)
    n = T // t
    r = t // ATT_ROW_CHUNKS
    G = min(ATT_HEADS_PER_STEP, H)
    P = MLA_HEAD_PAD
    scale = (MLA_NOPE + MLA_ROPE) ** -0.5
    c2 = scale * LOG2E
    carried = _Carried(jobs, (H // G, n, n))

    def body(*refs):
        (q_ref, k_ref, v_ref), job_ins, (o_ref, ob_ref, lse_ref), job_outs, (m_ref, acc_ref), job_sems = _split_refs(
            refs, 3, len(carried.operands), 3, len(carried.out_shape), 2)
        i, j = pl.program_id(1), pl.program_id(2)
        carried.before(job_ins, job_outs, job_sems)

        @pl.when(j == 0)
        def _():
            m_ref[...] = jnp.full_like(m_ref, -jnp.inf)
            acc_ref[...] = jnp.zeros_like(acc_ref)

        def update(masked):
            for g in range(G):
                hc = pl.ds(g * P, P)
                for ch in range(ATT_ROW_CHUNKS):
                    rs = pl.ds(ch * r, r)
                    s = _dot(q_ref[rs, hc], k_ref[:, hc], NT)
                    if masked:
                        s = jnp.where(_chunk_mask(r, ch * r, t), s, -jnp.inf)
                    m_prev = m_ref[rs, pl.ds(g * LANE, LANE)]
                    m_new = jnp.maximum(m_prev, jnp.max(s, axis=-1, keepdims=True))
                    alpha = jnp.exp2((m_prev - m_new) * c2)
                    p = jnp.exp2((s - jnp.tile(m_new, (1, t // LANE))) * c2)
                    acc_ref[rs, hc] = jnp.tile(alpha, (1, 2)) * acc_ref[rs, hc] + _dot(p, v_ref[:, hc])
                    m_ref[rs, pl.ds(g * LANE, LANE)] = m_new

        pl.when(j < i)(functools.partial(update, False))
        pl.when(j == i)(functools.partial(update, True))

        @pl.when(j == n - 1)
        def _():
            for g in range(G):
                acc = acc_ref[:, pl.ds(g * P, P)]
                l = acc[:, MLA_DV:]
                o = acc[:, :MLA_DV] / l
                o_ref[:, pl.ds(g * MLA_DV, MLA_DV)] = o
                ob_ref[:, pl.ds(g * MLA_DV, MLA_DV)] = o.astype(ob_ref.dtype)
                lse_ref[g] = m_ref[:, pl.ds(g * LANE, LANE)] * scale + jnp.log(l)

        carried.after(job_ins, job_outs, job_sems)

    kclamp = lambda i, j: jnp.minimum(j, i)
    o_spec = pl.BlockSpec((t, G * MLA_DV), lambda h, i, j: (i, h))
    return pl.pallas_call(
        body, name="attention_fwd", grid=(H // G, n, n),
        in_specs=[pl.BlockSpec((t, G * P), lambda h, i, j: (i, h)),
                  pl.BlockSpec((t, G * P), lambda h, i, j: (kclamp(i, j), h)),
                  pl.BlockSpec((t, G * P), lambda h, i, j: (kclamp(i, j), h))] + carried.in_specs,
        out_specs=[o_spec, o_spec, pl.BlockSpec((G, t, LANE), lambda h, i, j: (h, i, 0))] + carried.out_specs,
        out_shape=[jax.ShapeDtypeStruct((T, H * MLA_DV), F32), jax.ShapeDtypeStruct((T, H * MLA_DV), BF16),
                   jax.ShapeDtypeStruct((H, T, LANE), F32)] + carried.out_shape,
        scratch_shapes=[pltpu.VMEM((t, G * LANE), F32), pltpu.VMEM((t, G * P), F32)] + carried.sems,
        input_output_aliases=carried.aliases(3, 3),
        compiler_params=_params(3),
    )(qh, kh, vx, *carried.operands)


def _attention_bwd(qh, kh, vx, o, lse, dmix, H, jobs=()):
    T = qh.shape[0]
    t = min(ATT_BLOCK, T)
    n = T // t
    r = t // ATT_ROW_CHUNKS
    G = min(ATT_BWD_HEADS_PER_STEP, H)
    P = MLA_HEAD_PAD
    scale = (MLA_NOPE + MLA_ROPE) ** -0.5
    c2 = scale * LOG2E
    carried = _Carried(jobs, (H // G, n, n))

    def body(*refs):
        ((q_ref, k_ref, v_ref, do_ref, o_ref, lse_ref), job_ins, (dq_ref, dkn_ref, dkr_ref, dv_ref), job_outs,
         (dk_acc, dv_acc), job_sems) = _split_refs(refs, 6, len(carried.operands), 4, len(carried.out_shape), 2)
        j, i = pl.program_id(1), pl.program_id(2)
        carried.before(job_ins, job_outs, job_sems)

        @pl.when(jnp.logical_and(j == 0, i == 0))
        def _():
            dq_ref[...] = jnp.zeros_like(dq_ref)

        @pl.when(i == j)
        def _():
            dk_acc[...] = jnp.zeros_like(dk_acc)
            dv_acc[...] = jnp.zeros_like(dv_acc)

        def update(masked):
            for g in range(G):
                hc, hv = pl.ds(g * P, P), pl.ds(g * MLA_DV, MLA_DV)
                k, v = k_ref[:, hc], v_ref[:, pl.ds(g * P, MLA_DV)]
                for ch in range(ATT_ROW_CHUNKS):
                    rs = pl.ds(ch * r, r)
                    q, do = q_ref[rs, hc], do_ref[rs, hv]
                    s = _dot(q, k, NT)
                    if masked:
                        s = jnp.where(_chunk_mask(r, ch * r, t), s, -jnp.inf)
                    p = jnp.exp2(s * c2 - jnp.tile(lse_ref[g, rs, :] * LOG2E, (1, t // LANE)))
                    dv_acc[:, hv] += _dot(p, do, TN)
                    dp = _dot(do, v, NT)
                    delta = jnp.sum(do * o_ref[rs, hv], axis=-1, keepdims=True)
                    ds = p * (dp - delta)
                    rows = pl.ds(pl.multiple_of(i * t, t) + ch * r, r)
                    dq_ref[rows, hc] += _dot(ds, k)
                    dk_acc[:, hc] += _dot(ds, q, TN)

        pl.when(i > j)(functools.partial(update, False))
        pl.when(i == j)(functools.partial(update, True))

        @pl.when(i == n - 1)
        def _():
            for g in range(G):
                dk = dk_acc[:, pl.ds(g * P, P)] * scale
                dkn_ref[:, pl.ds(g * LANE, LANE)] = dk[:, :MLA_NOPE].astype(dkn_ref.dtype)
                dkr_ref[:, pl.ds(g * LANE, LANE)] = dk[:, MLA_NOPE:]
            dv_ref[...] = dv_acc[...].astype(dv_ref.dtype)

        carried.after(job_ins, job_outs, job_sems)

    qclamp = lambda j, i: jnp.maximum(i, j)
    kv_out = pl.BlockSpec((t, G * LANE), lambda h, j, i: (j, h))
    q_in = pl.BlockSpec((t, G * MLA_DV), lambda h, j, i: (qclamp(j, i), h))
    return pl.pallas_call(
        body, name="attention_bwd", grid=(H // G, n, n),
        in_specs=[pl.BlockSpec((t, G * P), lambda h, j, i: (qclamp(j, i), h)),
                  pl.BlockSpec((t, G * P), lambda h, j, i: (j, h)),
                  pl.BlockSpec((t, G * P), lambda h, j, i: (j, h)),
                  pl.BlockSpec((t, G * MLA_DV), lambda h, j, i: (qclamp(j, i), H // G + h)),
                  q_in,
                  pl.BlockSpec((G, t, LANE), lambda h, j, i: (h, qclamp(j, i), 0))] + carried.in_specs,
        out_specs=[pl.BlockSpec((T, G * P), lambda h, j, i: (0, h)), kv_out, kv_out, kv_out] + carried.out_specs,
        out_shape=[jax.ShapeDtypeStruct((T, H * P), F32), jax.ShapeDtypeStruct((T, H * LANE), BF16),
                   jax.ShapeDtypeStruct((T, H * LANE), F32), jax.ShapeDtypeStruct((T, H * LANE), BF16)] + carried.out_shape,
        scratch_shapes=[pltpu.VMEM((t, G * P), F32), pltpu.VMEM((t, G * MLA_DV), F32)] + carried.sems,
        input_output_aliases=carried.aliases(6, 4),
        compiler_params=_params(3),
    )(qh, kh, vx, dmix, o, lse, *carried.operands)


def _adamw_math(w, g, m, v):
    m = ADAM_B1 * m + (1.0 - ADAM_B1) * g
    v = ADAM_B2 * v + (1.0 - ADAM_B2) * jnp.square(g)
    m_hat = m / (1.0 - ADAM_B1 ** ADAM_STEP)
    v_hat = v / (1.0 - ADAM_B2 ** ADAM_STEP)
    delta = -ADAM_LR * (m_hat / (jnp.sqrt(v_hat) + ADAM_EPS) + ADAM_WD * w)
    return delta, m, v


def _adamw(name, chip_sums, my_chip, landed, w, m, v):
    R, C = w.shape
    tr = _row_tile(R, C)
    nl = len(landed)
    assert sum(l.shape[0] for l in landed) == N_CHIP - 1

    def body(q_ref, own_ref, *refs):
        l_refs, (w_ref, m_ref, v_ref, g_out, d_out, m_out, v_out) = refs[:nl], refs[nl:]
        g = own_ref[...].astype(F32)
        for l_ref in l_refs:
            for s in range(l_ref.shape[0]):
                g = g + l_ref[s].astype(F32)
        delta, mn, vn = _adamw_math(w_ref[...], g, m_ref[...], v_ref[...])
        g_out[...] = g
        d_out[...] = delta
        m_out[...] = mn
        v_out[...] = vn

    spec = pl.BlockSpec((tr, C), lambda i, q_ref: (i, 0))
    own_spec = pl.BlockSpec((None, tr, C), lambda i, q_ref: (q_ref[0], i, 0))
    l_specs = [pl.BlockSpec((l.shape[0], tr, C), lambda i, q_ref: (0, i, 0)) for l in landed]
    return pl.pallas_call(
        body, name=name,
        grid_spec=pltpu.PrefetchScalarGridSpec(
            num_scalar_prefetch=1, grid=(R // tr,),
            in_specs=[own_spec] + l_specs + [spec, spec, spec], out_specs=[spec] * 4),
        out_shape=[jax.ShapeDtypeStruct((R, C), F32)] * 4, compiler_params=_params(1),
    )(my_chip, chip_sums, *landed, w, m, v)


def _adamw_small(landed, w, m, v):
    P = w.shape[1]

    def body(l_ref, w_ref, m_ref, v_ref, g_out, d_out, m_out, v_out):
        acc = l_ref[0]
        for s in range(1, N_DEV):
            acc = acc + l_ref[s]
        g = jnp.sum(acc, axis=0, keepdims=True)
        delta, mn, vn = _adamw_math(w_ref[...], g, m_ref[...], v_ref[...])
        g_out[...] = g
        d_out[...] = delta
        m_out[...] = mn
        v_out[...] = vn

    return pl.pallas_call(
        body, name="adamw_replicated", out_shape=[jax.ShapeDtypeStruct((1, P), F32)] * 4,
    )(landed, w, m, v)


def _rope_rows():
    inv_r = ROPE_BASE ** (-jnp.arange(0, RET_DK, 2, dtype=F32) / RET_DK)
    inv_m = ROPE_BASE ** (-jnp.arange(0, MLA_ROPE, 2, dtype=F32) / MLA_ROPE)
    h = MLA_ROPE // 2
    one, zero = jnp.ones((h,), F32), jnp.zeros((h,), F32)
    pad = jnp.zeros((LANE - MLA_ROPE,), F32)
    rows = [jnp.concatenate([inv_r, inv_r]), jnp.concatenate([-jnp.ones((RET_DK // 2,), F32), jnp.ones((RET_DK // 2,), F32)]),
            jnp.concatenate([inv_m, inv_m, pad]), jnp.concatenate([one, one, pad]),
            jnp.concatenate([zero, one, pad]), jnp.concatenate([-one, zero, pad])]
    return jnp.stack(rows + [jnp.zeros((LANE,), F32)] * (SUBLANE - len(rows)))


def _prep_tile(xt, p, rows):
    ang2, angp = p * rows[0:1], p * rows[2:3]
    sp = jnp.sin(angp)
    return (xt, jnp.cos(ang2), jnp.sin(ang2) * rows[1:2], jnp.cos(angp) * rows[3:4], sp * rows[4:5], sp * rows[5:6])


def kernel(x, positions, w_in, q_norm_g, w_uq, kv_norm_g, w_uk, w_uv, ret_gn_g, w_out, ln1_g, ln1_b, w_up, w_down, ln2_g, ln2_b, loss_target, m_w_in, m_q_norm_g, m_w_uq, m_kv_norm_g, m_w_uk, m_w_uv, m_ret_gn_g, m_w_out, m_ln1_g, m_ln1_b, m_w_up, m_w_down, m_ln2_g, m_ln2_b, v_w_in, v_q_norm_g, v_w_uq, v_kv_norm_g, v_w_uk, v_w_uv, v_ret_gn_g, v_w_out, v_ln1_g, v_ln1_b, v_w_up, v_w_down, v_ln2_g, v_ln2_b):
    T, D = x.shape[1], x.shape[2]
    H = D // 256
    RW = H * RET_DV
    x = x[0]
    tgt = loss_target[0]
    alpha = DEEPNORM_ALPHA
    in_width = w_in.shape[2] * N_DEV
    mla_in = in_width - 4 * RW
    mla_in_pad = -(-mla_in // 512) * 512
    f_uq = MLA_NOPE + MLA_ROPE

    names = ["w_in", "w_uq", "w_uk", "w_uv", "w_out", "w_up", "w_down"]
    big = dict(zip(names, [w_in[0], w_uq[0], w_uk[0], w_uv[0], w_out[0], w_up[0], w_down[0]]))
    kind = dict(zip(names, ["blk", "blk", "cols", "cols", "rows", "cols", "rows"]))
    half = {n: w.astype(BF16) for n, w in big.items()}

    def gather(ns, mid, **parts):
        return _GatherJob([half[n] for n in ns], [kind[n] for n in ns], mid, **parts)

    tr = min(256, T)
    late = 0.9
    xb, cos2, sin2, cosp, sina, sinb, g_in = _rowwise(
        "prep", _prep_tile, [(x, D, 0), (positions[0].astype(F32)[:, None], 1, 0)], [_rope_rows()],
        [(D, BF16)] + [(LANE, F32)] * 5, [], tr, jobs=[gather(["w_in"], late)])
    win = jnp.transpose(g_in, (1, 0, 2)).reshape(D, in_width)
    w1 = win[:, :4 * RW]
    w2 = jnp.pad(win[:, 4 * RW:], ((0, 0), (0, mla_in_pad - mla_in)))

    first, second, near, far = (0, 2), (1, 2), (CHIP_X, CHIP_Y), (CHIP_DIAGONAL,)
    p1, g_uq, wuk, wuv, wout, wup_a = _matmul(
        "proj_ret", xb, w1, jobs=[gather(["w_uq", "w_uk", "w_uv", "w_out"], late),
                                  gather(["w_up"], late, pieces=[(first, near, True)])])
    wuq = jnp.transpose(g_uq, (1, 0, 2)).reshape(MLA_Q_RANK, H, f_uq)
    wuq = jnp.pad(wuq, ((0, 0), (0, 0), (0, MLA_HEAD_PAD - f_uq))).reshape(MLA_Q_RANK, H * MLA_HEAD_PAD)
    wukv = jnp.concatenate([wuk, wuv], axis=1)
    p2 = _matmul("proj_mla", xb, w2)
    c_kv, c_kr = MLA_Q_RANK, MLA_Q_RANK + MLA_KV_RANK

    def mla_prep(p, cp, sa, sb, gq, gkv):
        cqn = _rms(p[:, :c_kv], gq)
        ckvn = _rms(p[:, c_kv:c_kr], gkv)
        return cqn, ckvn, _rope64(p[:, c_kr:c_kr + LANE], cp, sa, sb)

    cqn, ckvn, krope = _rowwise(
        "mla_prep", mla_prep, [(p2, mla_in_pad, 0), (cosp, LANE, 0), (sina, LANE, 0), (sinb, LANE, 0)],
        [q_norm_g, kv_norm_g], [(MLA_Q_RANK, BF16), (MLA_KV_RANK, BF16), (LANE, BF16)], [], tr)
    qf = _matmul("mla_q", cqn, wuq)
    kv = _matmul("mla_kv", ckvn, wukv, out_dtypes=(BF16,))

    def mla_heads(q, kn, vv, kr, cp, sa, sb):
        qs, ks, vs = [], [], []
        ones = jnp.ones((q.shape[0], MLA_DV), vv.dtype)
        for h in range(H):
            o = h * MLA_HEAD_PAD
            qs += [q[:, o:o + MLA_NOPE], _rope64(q[:, o + MLA_NOPE:o + MLA_HEAD_PAD], cp, sa, sb)]
            ks += [kn[:, h * MLA_NOPE:(h + 1) * MLA_NOPE], kr]
            vs += [vv[:, h * MLA_DV:(h + 1) * MLA_DV], ones]
        return jnp.concatenate(qs, axis=1), jnp.concatenate(ks, axis=1), jnp.concatenate(vs, axis=1)

    qh, kh, vx = _rowwise(
        "mla_heads", mla_heads,
        [(qf, H * MLA_HEAD_PAD, 0), (kv, RW, 0), (kv, RW, 1), (krope, LANE, 0), (cosp, LANE, 0), (sina, LANE, 0), (sinb, LANE, 0)],
        [], [(H * MLA_HEAD_PAD, BF16)] * 3, [], tr)
    att_o, att_ob, lse, wup_b, wdown_a = _attention_fwd(
        qh, kh, vx, H, jobs=[gather(["w_up"], late, pieces=[(second, near, True), (first, far, False)], base=[wup_a]),
                             gather(["w_down"], late, pieces=[(first, near, True)])])
    ret_pre, ret_ob, wup = _retention_fwd(
        p1, cos2, sin2, ret_gn_g, H, jobs=[gather(["w_up"], late, pieces=[(second, far, False)], base=[wup_b])])
    mixin = jnp.concatenate([ret_ob, att_ob], axis=1)
    mix, wdown_b = _matmul("mix_out", mixin, wout,
                           jobs=[gather(["w_down"], late, pieces=[(second, near, True)], base=[wdown_a])])

    def ln1_fwd(xt, mt, g, b):
        z = alpha * xt + mt
        xhat, _ = _ln_stats(z)
        y = xhat * g + b
        return z, y, y

    z1, x1, x1b = _rowwise("ln1", ln1_fwd, [(x, D, 0), (mix, D, 0)], [ln1_g, ln1_b],
                           [(D, F32), (D, F32), (D, BF16)], [], tr)

    def relu2(acc):
        r = jnp.maximum(acc, 0.0)
        return r * r, r

    a2b, rb, wdown = _matmul("mlp_up", x1b, wup, out_dtypes=(BF16, BF16), epilogue=relu2,
                             jobs=[gather(["w_down"], 0.8, pieces=[((0, 1), far, False)], base=[wdown_b])])
    hmlp = _matmul("mlp_down", a2b, wdown)

    def ln2_loss(x1t, ht, tt, g, b):
        xhat, rstd = _ln_stats(alpha * x1t + ht)
        err = xhat * g + b - tt
        dy = err / D
        dz = _ln_bwd(dy, xhat, rstd, g)
        lrow = 0.5 * jnp.mean(jnp.square(err), axis=-1, keepdims=True)
        return dz, dz, dy * xhat, dy, jnp.broadcast_to(lrow, (lrow.shape[0], LANE))

    dz2, dz2b, dg2, db2, lpart = _rowwise("ln2_loss", ln2_loss, [(x1, D, 0), (hmlp, D, 0), (tgt, D, 0)], [ln2_g, ln2_b],
                                          [(D, F32), (D, BF16)], [D, D, LANE], tr)
    loss = lax.psum(jnp.sum(lpart[:, 0]), ("x", "y", "c"))

    def pair(grads):
        ns = list(grads)
        return _PairJob([grads[n] for n in ns], [kind[n] for n in ns], [big[n].shape for n in ns])

    my_core = lax.axis_index("c").astype(jnp.int32).reshape(1)
    my_chip = (2 * lax.axis_index("x") + lax.axis_index("y")).astype(jnp.int32).reshape(1)
    chip_sums = {}

    def pair_sums(grads, theirs):
        for n, t in zip(grads, theirs):
            chip_sums[n] = _pair_sum("pair_sum_" + n, grads[n], kind[n], big[n].shape, t, my_core)
        return {n: chip_sums[n] for n in grads}

    def chip(sums, rels=CHIP_RELS, by_source=None):
        ns = list(sums)
        return _ChipJob([sums[n] for n in ns], [big[n].shape for n in ns], rels, by_source)

    landed = {}
    da = _matmul("mlp_down_dx", dz2b, wdown, tb=True, out_dtypes=(BF16,), extras=(rb,),
                 epilogue=lambda acc, r: (acc * (2.0 * r.astype(F32)),))
    g_down = {"w_down": _matmul("mlp_down_dw", a2b, dz2b, ta=True, out_dtypes=(BF16,))}
    g_wup, *pr = _matmul("mlp_up_dw", x1b, da, ta=True, out_dtypes=(BF16,), jobs=[pair(g_down)])
    sums_down = pair_sums(g_down, pr)
    g_up = {"w_up": g_wup}
    dx1m, l_a, *pr = _matmul("mlp_up_dx", da, wup, tb=True, jobs=[chip(sums_down, CHIP_RELS[:2]), pair(g_up)])
    sums_up = pair_sums(g_up, pr)

    def ln1_bwd(dm, dzz, z, g):
        xhat, rstd = _ln_stats(z)
        dy = dm + alpha * dzz
        dz = _ln_bwd(dy, xhat, rstd, g)
        return dz, dz, dy * xhat, dy

    dz1, dz1b, dg1, db1 = _rowwise("ln1_bwd", ln1_bwd, [(dx1m, D, 0), (dz2, D, 0), (z1, D, 0)], [ln1_g],
                                   [(D, F32), (D, BF16)], [D, D], tr)
    g_wout = _matmul("mix_out_dw", mixin, dz1b, ta=True, out_dtypes=(BF16,))
    dmix = _matmul("mix_out_dx", dz1b, wout, tb=True)

    dqh, dkn, dkr, dvv, l_b, l_c = _attention_bwd(qh, kh, vx, att_o, lse, dmix, H,
                                                  jobs=[chip(sums_down, CHIP_RELS[2:]), chip(sums_up)])
    landed["w_down"], landed["w_up"] = [l_a, l_b], [l_c]
    att_scale = (MLA_NOPE + MLA_ROPE) ** -0.5

    def mla_heads_bwd(dq, dkrh, cp, sa, sb):
        parts, acc = [], dkrh[:, :LANE]
        dq = dq * att_scale
        for h in range(H):
            o = h * MLA_HEAD_PAD
            parts += [dq[:, o:o + MLA_NOPE], _unrope64(dq[:, o + MLA_NOPE:o + MLA_HEAD_PAD], cp, sa, sb)]
            if h:
                acc = acc + dkrh[:, h * LANE:(h + 1) * LANE]
        return jnp.concatenate(parts, axis=1), _unrope64(acc, cp, sa, sb)

    dqb, dkr128 = _rowwise(
        "mla_heads_bwd", mla_heads_bwd,
        [(dqh, H * MLA_HEAD_PAD, 0), (dkr, H * LANE, 0), (cosp, LANE, 0), (sina, LANE, 0), (sinb, LANE, 0)],
        [], [(H * MLA_HEAD_PAD, BF16), (LANE, F32)], [], tr)
    dcqn = _matmul("mla_q_dx", dqb, wuq, tb=True)
    g_wuq = _matmul("mla_q_dw", cqn, dqb, ta=True, out_dtypes=(BF16,))
    dkvb = jnp.concatenate([dkn, dvv], axis=1)
    dckvn = _matmul("mla_kv_dx", dkvb, wukv, tb=True)
    g_wukv = _matmul("mla_kv_dw", ckvn, dkvb, ta=True, out_dtypes=(BF16,))

    def rms_bwd(c, dy, g):
        rstd = lax.rsqrt(jnp.mean(jnp.square(c), axis=-1, keepdims=True) + EPS)
        dyg = dy * g
        dc = rstd * (dyg - c * (rstd * rstd) * jnp.mean(dyg * c, axis=-1, keepdims=True))
        return dc, dy * c * rstd

    def mla_prep_bwd(p, dq_, dkv_, dkr_, gq, gkv):
        dcq, dgq_ = rms_bwd(p[:, :c_kv], dq_, gq)
        dckv, dgkv_ = rms_bwd(p[:, c_kv:c_kr], dkv_, gkv)
        pad = jnp.zeros((p.shape[0], mla_in_pad - c_kr - LANE), F32)
        return jnp.concatenate([dcq, dckv, dkr_, pad], axis=1), dgq_, dgkv_

    d_p2, dgq, dgkv = _rowwise(
        "mla_prep_bwd", mla_prep_bwd, [(p2, mla_in_pad, 0), (dcqn, MLA_Q_RANK, 0), (dckvn, MLA_KV_RANK, 0), (dkr128, LANE, 0)],
        [q_norm_g, kv_norm_g], [(mla_in_pad, BF16)], [MLA_Q_RANK, MLA_KV_RANK], tr)

    def gate_bwd(pre, gate, dout, gn):
        dpre, dgate, dgn = [], [], []
        for h in range(H):
            sl = slice(h * RET_DV, (h + 1) * RET_DV)
            nrm, rstd = _ln_stats(pre[:, sl])
            gate_h, do_h, gn_h = gate[:, sl], dout[:, sl], gn[:, sl]
            sg = jax.nn.sigmoid(gate_h)
            silu = gate_h * sg
            dgate.append(do_h * (nrm * gn_h) * (sg * (1.0 + gate_h * (1.0 - sg))))
            dn = do_h * silu
            dgn.append(dn * nrm)
            dpre.append(_ln_bwd(dn, nrm, rstd, gn_h))
        return jnp.concatenate(dpre, axis=1), jnp.concatenate(dgate, axis=1), jnp.concatenate(dgn, axis=1)

    dpre, drg, dgn = _rowwise("ret_gate_bwd", gate_bwd, [(ret_pre, RW, 0), (p1, RW, 3), (dmix, RW, 0)], [ret_gn_g],
                              [(RW, F32), (RW, BF16)], [RW], tr)
    g_uq = g_wuq.reshape(MLA_Q_RANK, H, MLA_HEAD_PAD)[:, :, :f_uq].reshape(MLA_Q_RANK, N_DEV, H * f_uq // N_DEV)
    mid_grads = {"w_uq": jnp.transpose(g_uq, (1, 0, 2)), "w_uk": g_wukv[:, :RW], "w_uv": g_wukv[:, RW:], "w_out": g_wout}
    (drq, *pr), (drk, drv) = _retention_bwd(p1, dpre, cos2, sin2, H, dq_jobs=[pair(mid_grads)])
    sums_mid = pair_sums(mid_grads, pr)
    d_p1 = jnp.concatenate([drq, drk, drv, drg], axis=1)

    g_w1, *l_a = _matmul("proj_ret_dw", xb, d_p1, ta=True, out_dtypes=(BF16,), jobs=[chip(sums_mid)])
    for n, a_ in zip(mid_grads, l_a):
        landed[n] = [a_]
    g_w2 = _matmul("proj_mla_dw", xb, d_p2, ta=True, out_dtypes=(BF16,))

    g_win = jnp.concatenate([g_w1, g_w2[:, :mla_in]], axis=1)
    g_win = {"w_in": jnp.transpose(g_win.reshape(D, N_DEV, in_width // N_DEV), (1, 0, 2))}
    small = jnp.concatenate([dgq, dgkv, dgn, dg1, db1, dg2, db2], axis=1)
    dx_a, *pr = _matmul("proj_mla_dx", d_p2, w2, tb=True, extras=(dz1,), epilogue=lambda acc, d: (acc + alpha * d,),
                        jobs=[pair(g_win)])
    sums_in = pair_sums(g_win, pr)
    grad_x, l_a, small_landed = _matmul("proj_ret_dx", d_p1, w1, tb=True, extras=(dx_a,), epilogue=lambda acc, d: (acc + d,),
                                        jobs=[chip(sums_in, by_source=small)])
    landed["w_in"] = [l_a]

    moments_m = [m_w_in, m_w_uq, m_w_uk, m_w_uv, m_w_out, m_w_up, m_w_down]
    moments_v = [v_w_in, v_w_uq, v_w_uk, v_w_uv, v_w_out, v_w_up, v_w_down]
    res = {}
    for n, m, v in zip(names, moments_m, moments_v):
        res[n] = [r[None] for r in _adamw("adamw_" + n, chip_sums[n], my_chip, landed[n], big[n], m[0], v[0])]

    small_names = ["q_norm_g", "kv_norm_g", "ret_gn_g", "ln1_g", "ln1_b", "ln2_g", "ln2_b"]
    small_w = [q_norm_g, kv_norm_g, ret_gn_g, ln1_g, ln1_b, ln2_g, ln2_b]
    small_m = [m_q_norm_g, m_kv_norm_g, m_ret_gn_g, m_ln1_g, m_ln1_b, m_ln2_g, m_ln2_b]
    small_v = [v_q_norm_g, v_kv_norm_g, v_ret_gn_g, v_ln1_g, v_ln1_b, v_ln2_g, v_ln2_b]
    cat = lambda arrs: jnp.concatenate(arrs, axis=1)
    sres = _adamw_small(small_landed, cat(small_w), cat(small_m), cat(small_v))
    off = 0
    for n, w in zip(small_names, small_w):
        res[n] = [r[:, off:off + w.shape[1]] for r in sres]
        off += w.shape[1]

    order = ["w_in", "q_norm_g", "w_uq", "kv_norm_g", "w_uk", "w_uv", "ret_gn_g", "w_out", "ln1_g", "ln1_b",
             "w_up", "w_down", "ln2_g", "ln2_b"]
    outs = [loss, grad_x[None]]
    for k in range(4):
        outs += [res[n][k] for n in order]
    return tuple(outs)
```

```python
import functools

import numpy as np

import jax
import jax.numpy as jnp
from jax import lax
from jax.experimental import pallas as pl
from jax.experimental.pallas import tpu as pltpu

F32 = jnp.float32
BF16 = jnp.bfloat16

CHUNK = 64
RET_DK = 128
RET_DV = 128
MLA_NOPE = 128
MLA_ROPE = 64
MLA_DV = 128
MLA_Q_RANK = 768
MLA_KV_RANK = 512
MLA_HEAD_PAD = 256
ROPE_BASE = 10000.0
EPS = 1e-5
DEPTH = 1
DEEPNORM_ALPHA = (2.0 * DEPTH) ** 0.25
ADAM_LR = 0.001
ADAM_B1 = 0.9
ADAM_B2 = 0.999
ADAM_EPS = 1e-08
ADAM_WD = 0.01
ADAM_STEP = 10

N_DEV = 8
LANE = 128
SUBLANE = 8
VMEM_LIMIT = 48 * 1024 * 1024
ADAMW_TILE_ELEMS = 128 * 1024
PAIR_TILE_ELEMS = 1024 * 1024

MESH = pl.DeviceIdType.MESH
ANY = pl.BlockSpec(memory_space=pl.ANY)

NN = (((1,), (0,)), ((), ()))
NT = (((1,), (1,)), ((), ()))
TN = (((0,), (0,)), ((), ()))


def _dot(a, b, dims=NN):
    return lax.dot_general(a.astype(BF16), b.astype(BF16), dims, preferred_element_type=F32)


def _params(n_axes):
    return pltpu.CompilerParams(dimension_semantics=("arbitrary",) * n_axes, vmem_limit_bytes=VMEM_LIMIT)


def _tile(n, want):
    if n <= want:
        return n
    t = want
    while t >= LANE:
        if n % t == 0:
            return t
        t -= LANE
    return n


def _my_place():
    return lax.axis_index("x"), lax.axis_index("y"), lax.axis_index("c")


def _flat(px, py, pc):
    return 4 * px + 2 * py + pc


def _block_of(ref, kind, idx, rows, cols, sub=None):
    lo, n = (0, rows) if sub is None else sub
    if kind == "blk":
        return ref.at[idx, pl.ds(lo, n), :]
    if kind == "rows":
        return ref.at[pl.ds(pl.multiple_of(idx * rows + lo, n), n), :]
    return ref.at[pl.ds(lo, n), pl.ds(pl.multiple_of(idx * cols, cols), cols)]


CHIP_X, CHIP_Y, CHIP_DIAGONAL = 0, 1, 2
ALL_CHIPS = (CHIP_X, CHIP_Y, CHIP_DIAGONAL)


class _GatherJob:
    def __init__(self, shards, kinds, mid, pieces=(((0, 1), ALL_CHIPS, True),), base=None):
        nw = len(shards)
        self.nw, self.kinds, self.pieces = nw, list(kinds), list(pieces)
        self.shapes = [s.shape for s in shards]
        self.ins = list(shards) + (list(base) if base is not None else [])
        self.alias = [(nw + w, w) for w in range(nw)] if base is not None else []
        self.out_shape = []
        for s, kind in zip(shards, kinds):
            r, c = s.shape
            shape = {"blk": (N_DEV, r, c), "rows": (N_DEV * r, c), "cols": (r, N_DEV * c)}[kind]
            self.out_shape.append(jax.ShapeDtypeStruct(shape, s.dtype))
        n = 7 * nw * len(self.pieces)
        self.sems = [pltpu.SemaphoreType.DMA((n,)), pltpu.SemaphoreType.DMA((n,)), pltpu.SemaphoreType.DMA((n,))]
        self.phases = [(0.0, self._start), (mid, self._forward), (1.0, self._finish)]

    def _each(self, src, dst, send_sems, recv_sems, local_sems):
        x, y, c = _my_place()
        all_chips = [(1 - x, y), (x, 1 - y), (1 - x, 1 - y)]
        out = []
        for w in range(self.nw):
            r, cc = self.shapes[w]
            for p, ((k_of, of), chips, own) in enumerate(self.pieces):
                sub = (k_of * (r // of), r // of)
                s0 = 7 * (w * len(self.pieces) + p)
                mine = src[w].at[pl.ds(*sub), :]

                def place(dev, w=w, r=r, cc=cc, sub=sub):
                    return _block_of(dst[w], self.kinds[w], _flat(*dev), r, cc, sub)

                def copy(k, block, to, from_src=False, place=place, mine=mine, s0=s0):
                    return pltpu.make_async_remote_copy(
                        src_ref=mine if from_src else place(block), dst_ref=place(block),
                        send_sem=send_sems.at[s0 + k], recv_sem=recv_sems.at[s0 + k], device_id=to, device_id_type=MESH)

                local = pltpu.make_async_copy(mine, place((x, y, c)), local_sems.at[s0])
                out.append(([(j, all_chips[j]) for j in chips], own, copy, local))
        return (x, y, c), (x, y, 1 - c), out

    def _start(self, *refs):
        me, sibling, each = self._each(*refs)
        for chips, own, copy, local in each:
            if own:
                local.start()
                copy(0, me, sibling, from_src=True).start()
            for j, chip in chips:
                copy(1 + j, me, (*chip, me[2]), from_src=True).start()

    def _forward(self, *refs):
        me, sibling, each = self._each(*refs)
        for chips, own, copy, local in each:
            for j, chip in chips:
                copy(1 + j, (*chip, me[2]), me).wait_recv()
                copy(4 + j, (*chip, me[2]), sibling).start()

    def _finish(self, *refs):
        me, sibling, each = self._each(*refs)
        for chips, own, copy, local in each:
            if own:
                copy(0, sibling, me).wait_recv()
            for j, chip in chips:
                copy(4 + j, (*chip, 1 - me[2]), me).wait_recv()
        for chips, own, copy, local in each:
            if own:
                copy(0, me, sibling, from_src=True).wait_send()
                local.wait()
            for j, chip in chips:
                copy(1 + j, me, (*chip, me[2]), from_src=True).wait_send()
                copy(4 + j, (*chip, me[2]), sibling).wait_send()


REL_ALL = [(1, 0, 0), (1, 0, 1), (0, 1, 0), (0, 1, 1), (1, 1, 0), (1, 1, 1), (0, 0, 1)]
N_CHIP = 4


class _PairJob:
    def __init__(self, grads, kinds, shard_shapes):
        self.ins = list(grads)
        self.kinds, self.shard_shapes, self.nw = list(kinds), list(shard_shapes), len(grads)
        self.out_shape = [jax.ShapeDtypeStruct((N_CHIP,) + tuple(s), g.dtype) for g, s in zip(grads, shard_shapes)]
        n = N_CHIP * self.nw
        self.sems = [pltpu.SemaphoreType.DMA((n,)), pltpu.SemaphoreType.DMA((n,))]
        self.phases = [(0.0, self._start), (1.0, self._finish)]

    def _copies(self, src, theirs, send_sems, recv_sems):
        x, y, c = _my_place()
        remote = []
        for w in range(self.nw):
            r, cc = self.shard_shapes[w]
            for q in range(N_CHIP):
                s = N_CHIP * w + q
                remote.append(pltpu.make_async_remote_copy(
                    src_ref=_block_of(src[w], self.kinds[w], 2 * q + (1 - c), r, cc), dst_ref=theirs[w].at[q],
                    send_sem=send_sems.at[s], recv_sem=recv_sems.at[s], device_id=(x, y, 1 - c), device_id_type=MESH))
        return remote

    def _start(self, *refs):
        for cp in self._copies(*refs):
            cp.start()

    def _finish(self, *refs):
        for cp in self._copies(*refs):
            cp.wait()


CHIP_RELS = [(1, 0), (0, 1), (1, 1)]


class _ChipJob:
    def __init__(self, chip_sums, shard_shapes, rels=CHIP_RELS, by_source=None, piece=(0, 1), base=None):
        nw = len(chip_sums)
        self.ins = list(chip_sums) + ([by_source] if by_source is not None else []) + (list(base) if base is not None else [])
        self.nw, self.has_src, self.rels, self.piece = nw, by_source is not None, list(rels), piece
        self.rows = [s[0] for s in shard_shapes]
        self.alias = [(nw + int(self.has_src) + w, w) for w in range(nw)] if base is not None else []
        self.out_shape = [jax.ShapeDtypeStruct((len(rels),) + tuple(s), a.dtype) for a, s in zip(chip_sums, shard_shapes)]
        if self.has_src:
            self.out_shape.append(jax.ShapeDtypeStruct((N_DEV,) + by_source.shape, by_source.dtype))
        n = len(rels) * self.nw + len(REL_ALL) * int(self.has_src)
        self.sems = [pltpu.SemaphoreType.DMA((n,)), pltpu.SemaphoreType.DMA((n,)), pltpu.SemaphoreType.DMA((1,))]
        self.phases = [(0.0, self._start), (1.0, self._finish)]

    def _copies(self, src, land, send_sems, recv_sems, local_sems):
        x, y, c = _my_place()
        nw = self.nw

        def remote_copy(s, src_ref, dst_ref, peer):
            return pltpu.make_async_remote_copy(src_ref=src_ref, dst_ref=dst_ref, send_sem=send_sems.at[s],
                                                recv_sem=recv_sems.at[s], device_id=peer, device_id_type=MESH)

        def rows(w):
            n = self.rows[w] // self.piece[1]
            return pl.ds(self.piece[0] * n, n)

        local, remote = [], []
        for k, (fx, fy) in enumerate(self.rels):
            px, py = x ^ fx, y ^ fy
            remote += [remote_copy(nw * k + w, src[w].at[2 * px + py, rows(w), :], land[w].at[k, rows(w), :], (px, py, c))
                       for w in range(nw)]
        if self.has_src:
            my_idx = _flat(x, y, c)
            local.append(pltpu.make_async_copy(src[nw], land[nw].at[my_idx], local_sems.at[0]))
            remote += [remote_copy(nw * len(self.rels) + k, src[nw], land[nw].at[my_idx], (x ^ fx, y ^ fy, c ^ fc))
                       for k, (fx, fy, fc) in enumerate(REL_ALL)]
        return local, remote

    def _start(self, *refs):
        local, remote = self._copies(*refs)
        for cp in local + remote:
            cp.start()

    def _finish(self, *refs):
        local, remote = self._copies(*refs)
        for cp in remote + local:
            cp.wait()


def _row_tile(R, C, elems=ADAMW_TILE_ELEMS):
    tr = 16
    while tr * 2 * C <= elems and R % (tr * 2) == 0:
        tr *= 2
    return tr if R % tr == 0 else R


def _part_spec(kind, tr, C, R, n_lead):
    def dev(ids, pref):
        return 2 * ids[0] + pref[0] if n_lead == 2 else pref[0]

    def index(*args):
        ids, pref = args[:n_lead], args[n_lead]
        i = ids[-1]
        if kind == "blk":
            return dev(ids, pref), i, 0
        if kind == "rows":
            return dev(ids, pref) * (R // tr) + i, 0
        return i, dev(ids, pref)

    return pl.BlockSpec((None, tr, C) if kind == "blk" else (tr, C), index)


def _pair_sum(name, grad, kind, shard_shape, theirs, c):
    R, C = shard_shape
    tr = _row_tile(R, C, PAIR_TILE_ELEMS)

    def body(c_ref, g_ref, t_ref, o_ref):
        o_ref[...] = (g_ref[...].astype(F32) + t_ref[...].astype(F32)).astype(o_ref.dtype)

    slot = pl.BlockSpec((None, tr, C), lambda q, i, c_ref: (q, i, 0))
    return pl.pallas_call(
        body, name=name,
        grid_spec=pltpu.PrefetchScalarGridSpec(
            num_scalar_prefetch=1, grid=(N_CHIP, R // tr),
            in_specs=[_part_spec(kind, tr, C, R, 2), slot], out_specs=slot),
        out_shape=jax.ShapeDtypeStruct(theirs.shape, theirs.dtype), compiler_params=_params(2))(c, grad, theirs)


class _Carried:
    def __init__(self, jobs, grid):
        self.jobs = list(jobs)
        self.steps = int(np.prod(grid))
        self.grid = tuple(grid)
        self.operands = [a for j in self.jobs for a in j.ins]
        self.out_shape = [o for j in self.jobs for o in j.out_shape]
        self.sems = [s for j in self.jobs for s in j.sems]
        self.in_specs = [ANY] * len(self.operands)
        self.out_specs = [ANY] * len(self.out_shape)

    def aliases(self, n_in, n_out):
        out, i0, o0 = {}, n_in, n_out
        for job in self.jobs:
            for ji, jo in getattr(job, "alias", []):
                out[i0 + ji] = o0 + jo
            i0, o0 = i0 + len(job.ins), o0 + len(job.out_shape)
        return out

    def _step(self):
        lin = pl.program_id(0)
        for a in range(1, len(self.grid)):
            lin = lin * self.grid[a] + pl.program_id(a)
        return lin

    def _run(self, in_refs, out_refs, sem_refs, last):
        lin = self._step()
        for job in self.jobs:
            ins, in_refs = in_refs[:len(job.ins)], in_refs[len(job.ins):]
            outs, out_refs = out_refs[:len(job.out_shape)], out_refs[len(job.out_shape):]
            sems, sem_refs = sem_refs[:len(job.sems)], sem_refs[len(job.sems):]
            for frac, fn in job.phases:
                if (frac >= 1.0) != last:
                    continue
                at = min(self.steps - 1, int(frac * (self.steps - 1)))
                pl.when(lin == at)(functools.partial(fn, ins, outs, *sems))

    def before(self, in_refs, out_refs, sem_refs):
        self._run(in_refs, out_refs, sem_refs, last=False)

    def after(self, in_refs, out_refs, sem_refs):
        self._run(in_refs, out_refs, sem_refs, last=True)


def _split_refs(refs, n_in, n_job_in, n_out, n_job_out, n_scratch):
    cuts = np.cumsum([0, n_in, n_job_in, n_out, n_job_out, n_scratch])
    return [refs[cuts[i]:cuts[i + 1]] for i in range(5)] + [refs[cuts[5]:]]


def _matmul(name, a, b, *, ta=False, tb=False, out_dtypes=(F32,), epilogue=None, extras=(), tm=1024, tn=1024, tk=2048, jobs=()):
    if ta:
        K, M = a.shape
    else:
        M, K = a.shape
    N = b.shape[0] if tb else b.shape[1]
    tm, tn, tk = _tile(M, tm), _tile(N, tn), _tile(K, tk)
    nk = K // tk
    grid = (M // tm, N // tn, nk)
    dims = TN if ta else (NT if tb else NN)
    n_ex, n_out = len(extras), len(out_dtypes)
    carried = _Carried(jobs, grid)

    def body(*refs):
        ins, job_ins, out_refs, job_outs, scratch, job_sems = _split_refs(
            refs, 2 + n_ex, len(carried.operands), n_out, len(carried.out_shape), int(nk > 1))
        a_ref, b_ref, ex_refs = ins[0], ins[1], ins[2:]
        k = pl.program_id(2)
        carried.before(job_ins, job_outs, job_sems)

        def finish(acc):
            res = (acc,) if epilogue is None else epilogue(acc, *[e[...] for e in ex_refs])
            for o_ref, r in zip(out_refs, res):
                o_ref[...] = r.astype(o_ref.dtype)

        if nk == 1:
            finish(_dot(a_ref[...], b_ref[...], dims))
        else:
            acc_ref = scratch[0]

            @pl.when(k == 0)
            def _():
                acc_ref[...] = _dot(a_ref[...], b_ref[...], dims)

            @pl.when(k > 0)
            def _():
                acc_ref[...] += _dot(a_ref[...], b_ref[...], dims)

            pl.when(k == nk - 1)(lambda: finish(acc_ref[...]))

        carried.after(job_ins, job_outs, job_sems)

    a_spec = pl.BlockSpec((tk, tm), lambda i, j, k: (k, i)) if ta else pl.BlockSpec((tm, tk), lambda i, j, k: (i, k))
    b_spec = pl.BlockSpec((tn, tk), lambda i, j, k: (j, k)) if tb else pl.BlockSpec((tk, tn), lambda i, j, k: (k, j))
    tile_spec = pl.BlockSpec((tm, tn), lambda i, j, k: (i, j))
    outs = pl.pallas_call(
        body,
        name=name,
        grid=grid,
        in_specs=[a_spec, b_spec] + [tile_spec] * n_ex + carried.in_specs,
        out_specs=[tile_spec] * n_out + carried.out_specs,
        out_shape=[jax.ShapeDtypeStruct((M, N), d) for d in out_dtypes] + carried.out_shape,
        scratch_shapes=[pltpu.VMEM((tm, tn), F32)] * int(nk > 1) + carried.sems,
        input_output_aliases=carried.aliases(2 + n_ex, n_out),
        compiler_params=_params(3),
    )(a, b, *extras, *carried.operands)
    return outs[0] if len(outs) == 1 else outs


def _rowwise(name, fn, row_in, const_in, row_out, acc_out, tm, jobs=()):
    T = row_in[0][0].shape[0]
    n_r, n_c, n_o, n_a = len(row_in), len(const_in), len(row_out), len(acc_out)
    carried = _Carried(jobs, (T // tm,))

    def body(*refs):
        in_refs, job_ins, own_outs, job_outs, _, job_sems = _split_refs(
            refs, n_r + n_c, len(carried.operands), n_o + n_a, len(carried.out_shape), 0)
        out_refs, acc_refs = own_outs[:n_o], own_outs[n_o:]
        carried.before(job_ins, job_outs, job_sems)
        res = fn(*[r[...] for r in in_refs])
        for o_ref, r in zip(out_refs, res[:n_o]):
            o_ref[...] = r.astype(o_ref.dtype)

        @pl.when(pl.program_id(0) == 0)
        def _():
            for a_ref in acc_refs:
                a_ref[...] = jnp.zeros_like(a_ref)

        for a_ref, r in zip(acc_refs, res[n_o:]):
            a_ref[...] += jnp.sum(r.reshape(tm // SUBLANE, SUBLANE, r.shape[-1]), axis=0)
        carried.after(job_ins, job_outs, job_sems)

    in_specs = [pl.BlockSpec((tm, w), functools.partial(lambda i, c: (i, c), c=cb)) for (_, w, cb) in row_in]
    in_specs += [pl.BlockSpec(c.shape, functools.partial(lambda i, nd: (0,) * nd, nd=c.ndim)) for c in const_in]
    out_specs = [pl.BlockSpec((tm, c), lambda i: (i, 0)) for (c, _) in row_out]
    out_specs += [pl.BlockSpec((SUBLANE, c), lambda i: (0, 0)) for c in acc_out]
    out_shape = [jax.ShapeDtypeStruct((T, c), d) for (c, d) in row_out]
    out_shape += [jax.ShapeDtypeStruct((SUBLANE, c), F32) for c in acc_out]
    return pl.pallas_call(
        body, name=name, grid=(T // tm,), in_specs=in_specs + carried.in_specs, out_specs=out_specs + carried.out_specs,
        out_shape=out_shape + carried.out_shape, scratch_shapes=carried.sems,
        input_output_aliases=carried.aliases(n_r + n_c, n_o + n_a), compiler_params=_params(1),
    )(*[r[0] for r in row_in], *const_in, *carried.operands)


def _ln_stats(z):
    mu = jnp.mean(z, axis=-1, keepdims=True)
    var = jnp.mean(jnp.square(z - mu), axis=-1, keepdims=True)
    rstd = lax.rsqrt(var + EPS)
    return (z - mu) * rstd, rstd


def _ln_bwd(dy, xhat, rstd, g):
    dxh = dy * g
    return rstd * (dxh - jnp.mean(dxh, axis=-1, keepdims=True) - xhat * jnp.mean(dxh * xhat, axis=-1, keepdims=True))


def _rms(x, g):
    return x * lax.rsqrt(jnp.mean(jnp.square(x), axis=-1, keepdims=True) + EPS) * g


def _rope128(x, cos2, sin2):
    return x * cos2 + pltpu.roll(x, RET_DK // 2, 1) * sin2


def _unrope128(dy, cos2, sin2):
    return dy * cos2 + pltpu.roll(dy * sin2, RET_DK // 2, 1)


def _rope64(x, cosp, sina, sinb):
    h = MLA_ROPE // 2
    return x * cosp + pltpu.roll(x, h, 1) * sina + pltpu.roll(x, LANE - h, 1) * sinb


def _unrope64(dy, cosp, sina, sinb):
    h = MLA_ROPE // 2
    return dy * cosp + pltpu.roll(dy * sina, LANE - h, 1) + pltpu.roll(dy * sinb, h, 1)


RET_BLOCK = 256
RET_HEADS_PER_STEP = 4


def _ret_tables(H, B):
    log_g = np.log1p(-np.exp2(-5.0 - np.arange(H, dtype=np.float32))).astype(np.float32)
    idx = np.arange(B, dtype=np.float32)
    dist = np.abs(idx[:, None] - idx[None, :])
    visible = (np.arange(B)[None, :] // CHUNK) <= (np.arange(B)[:, None] // CHUNK)
    dmat = np.exp(log_g[:, None, None] * dist).astype(np.float32) * visible[None].astype(np.float32)
    qd = np.exp(log_g[:, None] * (idx + 1.0)).astype(np.float32)
    kd = np.exp(log_g[:, None] * (B - 1 - idx)).astype(np.float32)
    sd = np.exp(log_g * B).astype(np.float32)
    bc = lambda v: np.ascontiguousarray(np.broadcast_to(v[:, :, None], (H, v.shape[1], LANE)))
    sdb = np.ascontiguousarray(np.broadcast_to(sd[:, None, None], (H, SUBLANE, LANE)))
    return jnp.asarray(dmat), jnp.asarray(bc(qd)), jnp.asarray(bc(kd)), jnp.asarray(sdb)


def _ret_specs(H, B, reverse, nb):
    blk = (lambda b: nb - 1 - b) if reverse else (lambda b: b)
    G = min(RET_HEADS_PER_STEP, H)
    W = G * LANE
    col = lambda off: pl.BlockSpec((B, W), functools.partial(lambda h, b, off: (blk(b), off // G + h), off=off))
    row = pl.BlockSpec((B, LANE), lambda h, b: (blk(b), 0))
    per_head = lambda r: pl.BlockSpec((G, r, LANE), lambda h, b: (h, 0, 0))
    dmat = pl.BlockSpec((G, B, B), lambda h, b: (h, 0, 0))
    return G, col, row, per_head, dmat


def _ret_call(name, body, n_in, n_out, operands, in_specs, out_specs, out_shape, grid, G, jobs, own_alias=None):
    carried = _Carried(jobs, grid)

    def wrapped(*refs):
        ins, job_ins, outs, job_outs, (state,), job_sems = _split_refs(
            refs, n_in, len(carried.operands), n_out, len(carried.out_shape), 1)
        carried.before(job_ins, job_outs, job_sems)

        @pl.when(pl.program_id(1) == 0)
        def _():
            state[...] = jnp.zeros_like(state)

        body(*ins, *outs, state)
        carried.after(job_ins, job_outs, job_sems)

    return pl.pallas_call(
        wrapped, name=name, grid=grid,
        in_specs=in_specs + carried.in_specs, out_specs=out_specs + carried.out_specs,
        out_shape=out_shape + carried.out_shape,
        scratch_shapes=[pltpu.VMEM((G, RET_DK, RET_DV), F32)] + carried.sems,
        input_output_aliases={**(own_alias or {}), **carried.aliases(n_in, n_out)},
        compiler_params=_params(2),
    )(*operands, *carried.operands)


def _retention_fwd(p1, cos2, sin2, gn_g, mix_in, H, jobs=()):
    T = p1.shape[0]
    B = min(RET_BLOCK, T)
    nb = T // B
    scale = RET_DK ** -0.5
    dmat, qd, kd, sd = _ret_tables(H, B)
    G, col, row, per_head, dspec = _ret_specs(H, B, False, nb)

    def body(q_ref, k_ref, v_ref, g_ref, cos_ref, sin_ref, d_ref, qd_ref, kd_ref, sd_ref, gn_ref, _, pre_ref, out_ref, s_ref):
        cos, sin = cos_ref[...], sin_ref[...]
        for g in range(G):
            hl = pl.ds(g * LANE, LANE)
            q = _rope128(q_ref[:, hl], cos, sin)
            k = _rope128(k_ref[:, hl], cos, sin) * scale
            v = v_ref[:, hl]
            s = s_ref[g]
            a = _dot(q, k, NT) * d_ref[g]
            ret = _dot(a, v) + _dot(q * qd_ref[g], s)
            s_ref[g] = sd_ref[g, 0:1, :] * s + _dot(k * kd_ref[g], v, TN)
            pre_ref[:, hl] = ret
            nrm, _ = _ln_stats(ret)
            out_ref[:, hl] = (jax.nn.silu(g_ref[:, hl]) * (nrm * gn_ref[:, hl])).astype(out_ref.dtype)

    out_spec = pl.BlockSpec((B, G * LANE), lambda h, b: (b, h))
    return _ret_call(
        "retention_fwd", body, 12, 2, (p1, p1, p1, p1, cos2, sin2, dmat, qd, kd, sd, gn_g, mix_in),
        [col(0), col(H), col(2 * H), col(3 * H), row, row, dspec, per_head(B), per_head(B), per_head(SUBLANE),
         pl.BlockSpec((1, G * LANE), lambda h, b: (0, h)), ANY],
        [out_spec, out_spec],
        [jax.ShapeDtypeStruct((T, H * RET_DV), F32), jax.ShapeDtypeStruct(mix_in.shape, mix_in.dtype)],
        (H // G, nb), G, jobs, own_alias={11: 1})


def _retention_bwd(p1, dpre, cos2, sin2, H, dq_jobs=(), dkv_jobs=()):
    T = p1.shape[0]
    B = min(RET_BLOCK, T)
    nb = T // B
    scale = RET_DK ** -0.5
    dmat, qd, kd, sd = _ret_tables(H, B)

    def dq_body(k_ref, v_ref, do_ref, cos_ref, sin_ref, d_ref, qd_ref, kd_ref, sd_ref, dq_ref, s_ref):
        cos, sin = cos_ref[...], sin_ref[...]
        for g in range(G):
            hl = pl.ds(g * LANE, LANE)
            k = _rope128(k_ref[:, hl], cos, sin) * scale
            v, do, s = v_ref[:, hl], do_ref[:, hl], s_ref[g]
            da = _dot(do, v, NT) * d_ref[g]
            dq = _dot(da, k) + _dot(do, s, NT) * qd_ref[g]
            dq_ref[:, hl] = _unrope128(dq, cos, sin).astype(dq_ref.dtype)
            s_ref[g] = sd_ref[g, 0:1, :] * s + _dot(k * kd_ref[g], v, TN)

    G, col, row, per_head, dspec = _ret_specs(H, B, False, nb)
    dq = _ret_call(
        "retention_bwd_dq", dq_body, 9, 1, (p1, p1, dpre, cos2, sin2, dmat, qd, kd, sd),
        [col(H), col(2 * H), col(0), row, row, dspec, per_head(B), per_head(B), per_head(SUBLANE)],
        [pl.BlockSpec((B, G * LANE), lambda h, b: (b, h))], [jax.ShapeDtypeStruct((T, H * RET_DK), BF16)],
        (H // G, nb), G, dq_jobs)

    def dkv_body(q_ref, k_ref, v_ref, do_ref, cos_ref, sin_ref, d_ref, qd_ref, kd_ref, sd_ref, dk_ref, dv_ref, g_ref):
        cos, sin = cos_ref[...], sin_ref[...]
        for g in range(G):
            hl = pl.ds(g * LANE, LANE)
            q = _rope128(q_ref[:, hl], cos, sin)
            k = _rope128(k_ref[:, hl], cos, sin) * scale
            v, do, st = v_ref[:, hl], do_ref[:, hl], g_ref[g]
            a = _dot(q, k, NT) * d_ref[g]
            da = _dot(do, v, NT) * d_ref[g]
            dv = _dot(a, do, TN) + _dot(k * kd_ref[g], st)
            dk = _dot(da, q, TN) + _dot(v, st, NT) * kd_ref[g]
            dv_ref[:, hl] = dv.astype(dv_ref.dtype)
            dk_ref[:, hl] = _unrope128(dk * scale, cos, sin).astype(dk_ref.dtype)
            g_ref[g] = sd_ref[g, 0:1, :] * st + _dot(q * qd_ref[g], do, TN)

    G, col, row, per_head, dspec = _ret_specs(H, B, True, nb)
    out_spec = pl.BlockSpec((B, G * LANE), lambda h, b: (nb - 1 - b, h))
    dkv = _ret_call(
        "retention_bwd_dkv", dkv_body, 10, 2, (p1, p1, p1, dpre, cos2, sin2, dmat, qd, kd, sd),
        [col(0), col(H), col(2 * H), col(0), row, row, dspec, per_head(B), per_head(B), per_head(SUBLANE)],
        [out_spec, out_spec],
        [jax.ShapeDtypeStruct((T, H * RET_DK), BF16), jax.ShapeDtypeStruct((T, H * RET_DV), BF16)],
        (H // G, nb), G, dkv_jobs)
    return dq, dkv


ATT_BLOCK = 512


ATT_ROW_CHUNKS = 2
ATT_HEADS_PER_STEP = 8
ATT_BWD_HEADS_PER_STEP = 2
LOG2E = 1.4426950408889634


def _chunk_mask(rows, row0, cols):
    qi = (lax.broadcasted_iota(jnp.int32, (rows, cols), 0) + row0) // CHUNK
    kj = lax.broadcasted_iota(jnp.int32, (rows, cols), 1) // CHUNK
    return kj <= qi


def _attention_fwd(qh, kh, vx, H, jobs=()):
    T = qh.shape[0]
    t = min(ATT_BLOCK, T)
    n = T // t
    r = t // ATT_ROW_CHUNKS
    G = min(ATT_HEADS_PER_STEP, H)
    P = MLA_HEAD_PAD
    scale = (MLA_NOPE + MLA_ROPE) ** -0.5
    c2 = scale * LOG2E
    carried = _Carried(jobs, (H // G, n, n))

    def body(*refs):
        (q_ref, k_ref, v_ref), job_ins, (o_ref, ob_ref, lse_ref), job_outs, (m_ref, acc_ref), job_sems = _split_refs(
            refs, 3, len(carried.operands), 3, len(carried.out_shape), 2)
        i, j = pl.program_id(1), pl.program_id(2)
        carried.before(job_ins, job_outs, job_sems)

        @pl.when(j == 0)
        def _():
            m_ref[...] = jnp.full_like(m_ref, -jnp.inf)
            acc_ref[...] = jnp.zeros_like(acc_ref)

        def update(masked):
            for g in range(G):
                hc = pl.ds(g * P, P)
                for ch in range(ATT_ROW_CHUNKS):
                    rs = pl.ds(ch * r, r)
                    s = _dot(q_ref[rs, hc], k_ref[:, hc], NT)
                    if masked:
                        s = jnp.where(_chunk_mask(r, ch * r, t), s, -jnp.inf)
                    m_prev = m_ref[rs, pl.ds(g * LANE, LANE)]
                    m_new = jnp.maximum(m_prev, jnp.max(s, axis=-1, keepdims=True))
                    alpha = jnp.exp2((m_prev - m_new) * c2)
                    p = jnp.exp2((s - jnp.tile(m_new, (1, t // LANE))) * c2)
                    acc_ref[rs, hc] = jnp.tile(alpha, (1, 2)) * acc_ref[rs, hc] + _dot(p, v_ref[:, hc])
                    m_ref[rs, pl.ds(g * LANE, LANE)] = m_new

        pl.when(j < i)(functools.partial(update, False))
        pl.when(j == i)(functools.partial(update, True))

        @pl.when(j == n - 1)
        def _():
            for g in range(G):
                acc = acc_ref[:, pl.ds(g * P, P)]
                l = acc[:, MLA_DV:]
                o = acc[:, :MLA_DV] / l
                o_ref[:, pl.ds(g * MLA_DV, MLA_DV)] = o
                ob_ref[:, pl.ds(g * MLA_DV, MLA_DV)] = o.astype(ob_ref.dtype)
                lse_ref[g] = m_ref[:, pl.ds(g * LANE, LANE)] * scale + jnp.log(l)

        carried.after(job_ins, job_outs, job_sems)

    kclamp = lambda i, j: jnp.minimum(j, i)
    o_spec = pl.BlockSpec((t, G * MLA_DV), lambda h, i, j: (i, h))
    mix_spec = pl.BlockSpec((t, G * MLA_DV), lambda h, i, j: (i, H // G + h))
    return pl.pallas_call(
        body, name="attention_fwd", grid=(H // G, n, n),
        in_specs=[pl.BlockSpec((t, G * P), lambda h, i, j: (i, h)),
                  pl.BlockSpec((t, G * P), lambda h, i, j: (kclamp(i, j), h)),
                  pl.BlockSpec((t, G * P), lambda h, i, j: (kclamp(i, j), h))] + carried.in_specs,
        out_specs=[o_spec, mix_spec, pl.BlockSpec((G, t, LANE), lambda h, i, j: (h, i, 0))] + carried.out_specs,
        out_shape=[jax.ShapeDtypeStruct((T, H * MLA_DV), F32), jax.ShapeDtypeStruct((T, 2 * H * MLA_DV), BF16),
                   jax.ShapeDtypeStruct((H, T, LANE), F32)] + carried.out_shape,
        scratch_shapes=[pltpu.VMEM((t, G * LANE), F32), pltpu.VMEM((t, G * P), F32)] + carried.sems,
        input_output_aliases=carried.aliases(3, 3),
        compiler_params=_params(3),
    )(qh, kh, vx, *carried.operands)


def _attention_bwd(qh, kh, vx, o, lse, dmix, H, jobs=()):
    T = qh.shape[0]
    t = min(ATT_BLOCK, T)
    n = T // t
    r = t // ATT_ROW_CHUNKS
    G = min(ATT_BWD_HEADS_PER_STEP, H)
    P = MLA_HEAD_PAD
    scale = (MLA_NOPE + MLA_ROPE) ** -0.5
    c2 = scale * LOG2E
    carried = _Carried(jobs, (H // G, n, n))

    def body(*refs):
        ((q_ref, k_ref, v_ref, do_ref, o_ref, lse_ref), job_ins, (dq_ref, dkn_ref, dkr_ref, dv_ref), job_outs,
         (dk_acc, dv_acc), job_sems) = _split_refs(refs, 6, len(carried.operands), 4, len(carried.out_shape), 2)
        j, i = pl.program_id(1), pl.program_id(2)
        carried.before(job_ins, job_outs, job_sems)

        @pl.when(jnp.logical_and(j == 0, i == 0))
        def _():
            dq_ref[...] = jnp.zeros_like(dq_ref)

        @pl.when(i == j)
        def _():
            dk_acc[...] = jnp.zeros_like(dk_acc)
            dv_acc[...] = jnp.zeros_like(dv_acc)

        def update(masked):
            for g in range(G):
                hc, hv = pl.ds(g * P, P), pl.ds(g * MLA_DV, MLA_DV)
                k, v = k_ref[:, hc], v_ref[:, pl.ds(g * P, MLA_DV)]
                for ch in range(ATT_ROW_CHUNKS):
                    rs = pl.ds(ch * r, r)
                    q, do = q_ref[rs, hc], do_ref[rs, hv]
                    s = _dot(q, k, NT)
                    if masked:
                        s = jnp.where(_chunk_mask(r, ch * r, t), s, -jnp.inf)
                    p = jnp.exp2(s * c2 - jnp.tile(lse_ref[g, rs, :] * LOG2E, (1, t // LANE)))
                    dv_acc[:, hv] += _dot(p, do, TN)
                    dp = _dot(do, v, NT)
                    delta = jnp.sum(do * o_ref[rs, hv], axis=-1, keepdims=True)
                    ds = p * (dp - delta)
                    rows = pl.ds(pl.multiple_of(i * t, t) + ch * r, r)
                    dq_ref[rows, hc] += _dot(ds, k)
                    dk_acc[:, hc] += _dot(ds, q, TN)

        pl.when(i > j)(functools.partial(update, False))
        pl.when(i == j)(functools.partial(update, True))

        @pl.when(i == n - 1)
        def _():
            for g in range(G):
                dk = dk_acc[:, pl.ds(g * P, P)] * scale
                dkn_ref[:, pl.ds(g * LANE, LANE)] = dk[:, :MLA_NOPE].astype(dkn_ref.dtype)
                dkr_ref[:, pl.ds(g * LANE, LANE)] = dk[:, MLA_NOPE:]
            dv_ref[...] = dv_acc[...].astype(dv_ref.dtype)

        carried.after(job_ins, job_outs, job_sems)

    qclamp = lambda j, i: jnp.maximum(i, j)
    kv_out = pl.BlockSpec((t, G * LANE), lambda h, j, i: (j, h))
    q_in = pl.BlockSpec((t, G * MLA_DV), lambda h, j, i: (qclamp(j, i), h))
    return pl.pallas_call(
        body, name="attention_bwd", grid=(H // G, n, n),
        in_specs=[pl.BlockSpec((t, G * P), lambda h, j, i: (qclamp(j, i), h)),
                  pl.BlockSpec((t, G * P), lambda h, j, i: (j, h)),
                  pl.BlockSpec((t, G * P), lambda h, j, i: (j, h)),
                  pl.BlockSpec((t, G * MLA_DV), lambda h, j, i: (qclamp(j, i), H // G + h)),
                  q_in,
                  pl.BlockSpec((G, t, LANE), lambda h, j, i: (h, qclamp(j, i), 0))] + carried.in_specs,
        out_specs=[pl.BlockSpec((T, G * P), lambda h, j, i: (0, h)), kv_out, kv_out, kv_out] + carried.out_specs,
        out_shape=[jax.ShapeDtypeStruct((T, H * P), F32), jax.ShapeDtypeStruct((T, H * LANE), BF16),
                   jax.ShapeDtypeStruct((T, H * LANE), F32), jax.ShapeDtypeStruct((T, H * LANE), BF16)] + carried.out_shape,
        scratch_shapes=[pltpu.VMEM((t, G * P), F32), pltpu.VMEM((t, G * MLA_DV), F32)] + carried.sems,
        input_output_aliases=carried.aliases(6, 4),
        compiler_params=_params(3),
    )(qh, kh, vx, dmix, o, lse, *carried.operands)


def _adamw_math(w, g, m, v):
    m = ADAM_B1 * m + (1.0 - ADAM_B1) * g
    v = ADAM_B2 * v + (1.0 - ADAM_B2) * jnp.square(g)
    m_hat = m / (1.0 - ADAM_B1 ** ADAM_STEP)
    v_hat = v / (1.0 - ADAM_B2 ** ADAM_STEP)
    delta = -ADAM_LR * (m_hat / (jnp.sqrt(v_hat) + ADAM_EPS) + ADAM_WD * w)
    return delta, m, v


def _adamw(name, chip_sums, my_chip, landed, w, m, v, jobs=()):
    R, C = w.shape
    tr = _row_tile(R, C)
    nl = len(landed)
    assert sum(l.shape[0] for l in landed) == N_CHIP - 1
    carried = _Carried(jobs, (R // tr,))

    def body(q_ref, *refs):
        (own_ref, *rest), job_ins, (g_out, d_out, m_out, v_out), job_outs, _, job_sems = _split_refs(
            refs, nl + 4, len(carried.operands), 4, len(carried.out_shape), 0)
        l_refs, (w_ref, m_ref, v_ref) = rest[:nl], rest[nl:]
        carried.before(job_ins, job_outs, job_sems)
        g = own_ref[...].astype(F32)
        for l_ref in l_refs:
            for s in range(l_ref.shape[0]):
                g = g + l_ref[s].astype(F32)
        delta, mn, vn = _adamw_math(w_ref[...], g, m_ref[...], v_ref[...])
        g_out[...] = g
        d_out[...] = delta
        m_out[...] = mn
        v_out[...] = vn
        carried.after(job_ins, job_outs, job_sems)

    spec = pl.BlockSpec((tr, C), lambda i, q_ref: (i, 0))
    own_spec = pl.BlockSpec((None, tr, C), lambda i, q_ref: (q_ref[0], i, 0))
    l_specs = [pl.BlockSpec((l.shape[0], tr, C), lambda i, q_ref: (0, i, 0)) for l in landed]
    return pl.pallas_call(
        body, name=name,
        grid_spec=pltpu.PrefetchScalarGridSpec(
            num_scalar_prefetch=1, grid=(R // tr,),
            in_specs=[own_spec] + l_specs + [spec, spec, spec] + carried.in_specs,
            out_specs=[spec] * 4 + carried.out_specs, scratch_shapes=carried.sems),
        out_shape=[jax.ShapeDtypeStruct((R, C), F32)] * 4 + carried.out_shape,
        input_output_aliases={1 + i: o for i, o in carried.aliases(nl + 4, 4).items()},
        compiler_params=_params(1),
    )(my_chip, chip_sums, *landed, w, m, v, *carried.operands)


def _adamw_small(landed, w, m, v):
    P = w.shape[1]

    def body(l_ref, w_ref, m_ref, v_ref, g_out, d_out, m_out, v_out):
        acc = l_ref[0]
        for s in range(1, N_DEV):
            acc = acc + l_ref[s]
        g = jnp.sum(acc, axis=0, keepdims=True)
        delta, mn, vn = _adamw_math(w_ref[...], g, m_ref[...], v_ref[...])
        g_out[...] = g
        d_out[...] = delta
        m_out[...] = mn
        v_out[...] = vn

    return pl.pallas_call(
        body, name="adamw_replicated", out_shape=[jax.ShapeDtypeStruct((1, P), F32)] * 4,
    )(landed, w, m, v)


def _rope_rows():
    inv_r = ROPE_BASE ** (-jnp.arange(0, RET_DK, 2, dtype=F32) / RET_DK)
    inv_m = ROPE_BASE ** (-jnp.arange(0, MLA_ROPE, 2, dtype=F32) / MLA_ROPE)
    h = MLA_ROPE // 2
    one, zero = jnp.ones((h,), F32), jnp.zeros((h,), F32)
    pad = jnp.zeros((LANE - MLA_ROPE,), F32)
    rows = [jnp.concatenate([inv_r, inv_r]), jnp.concatenate([-jnp.ones((RET_DK // 2,), F32), jnp.ones((RET_DK // 2,), F32)]),
            jnp.concatenate([inv_m, inv_m, pad]), jnp.concatenate([one, one, pad]),
            jnp.concatenate([zero, one, pad]), jnp.concatenate([-one, zero, pad])]
    return jnp.stack(rows + [jnp.zeros((LANE,), F32)] * (SUBLANE - len(rows)))


def _prep_tile(xt, p, rows):
    ang2, angp = p * rows[0:1], p * rows[2:3]
    sp = jnp.sin(angp)
    return (xt, jnp.cos(ang2), jnp.sin(ang2) * rows[1:2], jnp.cos(angp) * rows[3:4], sp * rows[4:5], sp * rows[5:6])


def kernel(x, positions, w_in, q_norm_g, w_uq, kv_norm_g, w_uk, w_uv, ret_gn_g, w_out, ln1_g, ln1_b, w_up, w_down, ln2_g, ln2_b, loss_target, m_w_in, m_q_norm_g, m_w_uq, m_kv_norm_g, m_w_uk, m_w_uv, m_ret_gn_g, m_w_out, m_ln1_g, m_ln1_b, m_w_up, m_w_down, m_ln2_g, m_ln2_b, v_w_in, v_q_norm_g, v_w_uq, v_kv_norm_g, v_w_uk, v_w_uv, v_ret_gn_g, v_w_out, v_ln1_g, v_ln1_b, v_w_up, v_w_down, v_ln2_g, v_ln2_b):
    T, D = x.shape[1], x.shape[2]
    H = D // 256
    RW = H * RET_DV
    x = x[0]
    tgt = loss_target[0]
    alpha = DEEPNORM_ALPHA
    in_width = w_in.shape[2] * N_DEV
    mla_in = in_width - 4 * RW
    mla_in_pad = -(-mla_in // 512) * 512
    f_uq = MLA_NOPE + MLA_ROPE

    names = ["w_in", "w_uq", "w_uk", "w_uv", "w_out", "w_up", "w_down"]
    big = dict(zip(names, [w_in[0], w_uq[0], w_uk[0], w_uv[0], w_out[0], w_up[0], w_down[0]]))
    kind = dict(zip(names, ["blk", "blk", "cols", "cols", "rows", "cols", "rows"]))
    half = {n: w.astype(BF16) for n, w in big.items()}

    def gather(ns, mid, **parts):
        return _GatherJob([half[n] for n in ns], [kind[n] for n in ns], mid, **parts)

    tr = min(256, T)
    late = 0.9
    xb, cos2, sin2, cosp, sina, sinb, g_in = _rowwise(
        "prep", _prep_tile, [(x, D, 0), (positions[0].astype(F32)[:, None], 1, 0)], [_rope_rows()],
        [(D, BF16)] + [(LANE, F32)] * 5, [], tr, jobs=[gather(["w_in"], late)])
    win = jnp.transpose(g_in, (1, 0, 2)).reshape(D, in_width)
    w1 = win[:, :4 * RW]
    w2 = jnp.pad(win[:, 4 * RW:], ((0, 0), (0, mla_in_pad - mla_in)))

    first, second, near, far = (0, 2), (1, 2), (CHIP_X, CHIP_Y), (CHIP_DIAGONAL,)
    p1, g_uq, wuk, wuv, wout = _matmul("proj_ret", xb, w1,
                                       jobs=[gather(["w_uq", "w_uk", "w_uv", "w_out"], late)])
    wuq = jnp.transpose(g_uq, (1, 0, 2)).reshape(MLA_Q_RANK, H, f_uq)
    wuq = jnp.pad(wuq, ((0, 0), (0, 0), (0, MLA_HEAD_PAD - f_uq))).reshape(MLA_Q_RANK, H * MLA_HEAD_PAD)
    wukv = jnp.concatenate([wuk, wuv], axis=1)
    p2 = _matmul("proj_mla", xb, w2)
    c_kv, c_kr = MLA_Q_RANK, MLA_Q_RANK + MLA_KV_RANK

    def mla_prep(p, cp, sa, sb, gq, gkv):
        cqn = _rms(p[:, :c_kv], gq)
        ckvn = _rms(p[:, c_kv:c_kr], gkv)
        return cqn, ckvn, _rope64(p[:, c_kr:c_kr + LANE], cp, sa, sb)

    cqn, ckvn, krope = _rowwise(
        "mla_prep", mla_prep, [(p2, mla_in_pad, 0), (cosp, LANE, 0), (sina, LANE, 0), (sinb, LANE, 0)],
        [q_norm_g, kv_norm_g], [(MLA_Q_RANK, BF16), (MLA_KV_RANK, BF16), (LANE, BF16)], [], tr)
    qf = _matmul("mla_q", cqn, wuq)
    kv = _matmul("mla_kv", ckvn, wukv, out_dtypes=(BF16,))

    def mla_heads(q, kn, vv, kr, cp, sa, sb):
        qs, ks, vs = [], [], []
        ones = jnp.ones((q.shape[0], MLA_DV), vv.dtype)
        for h in range(H):
            o = h * MLA_HEAD_PAD
            qs += [q[:, o:o + MLA_NOPE], _rope64(q[:, o + MLA_NOPE:o + MLA_HEAD_PAD], cp, sa, sb)]
            ks += [kn[:, h * MLA_NOPE:(h + 1) * MLA_NOPE], kr]
            vs += [vv[:, h * MLA_DV:(h + 1) * MLA_DV], ones]
        return jnp.concatenate(qs, axis=1), jnp.concatenate(ks, axis=1), jnp.concatenate(vs, axis=1)

    qh, kh, vx = _rowwise(
        "mla_heads", mla_heads,
        [(qf, H * MLA_HEAD_PAD, 0), (kv, RW, 0), (kv, RW, 1), (krope, LANE, 0), (cosp, LANE, 0), (sina, LANE, 0), (sinb, LANE, 0)],
        [], [(H * MLA_HEAD_PAD, BF16)] * 3, [], tr)
    att_o, mix_half, lse, wup_a, wdown_a = _attention_fwd(
        qh, kh, vx, H, jobs=[gather(["w_up"], late, pieces=[(second, near, True), (first, far, False)]),
                             gather(["w_down"], late, pieces=[(first, near, True)])])
    ret_pre, mixin, wup_b = _retention_fwd(
        p1, cos2, sin2, ret_gn_g, mix_half, H, jobs=[gather(["w_up"], late, pieces=[(second, far, False)], base=[wup_a])])
    mix, wdown_b = _matmul("mix_out", mixin, wout,
                           jobs=[gather(["w_down"], late, pieces=[(second, near, True)], base=[wdown_a])])

    def ln1_fwd(xt, mt, g, b):
        z = alpha * xt + mt
        xhat, _ = _ln_stats(z)
        y = xhat * g + b
        return z, y, y

    z1, x1, x1b, wup = _rowwise("ln1", ln1_fwd, [(x, D, 0), (mix, D, 0)], [ln1_g, ln1_b],
                                [(D, F32), (D, F32), (D, BF16)], [], tr,
                                jobs=[gather(["w_up"], late, pieces=[(first, near, True)], base=[wup_b])])

    def relu2(acc):
        r = jnp.maximum(acc, 0.0)
        return r * r, r

    a2b, rb, wdown = _matmul("mlp_up", x1b, wup, out_dtypes=(BF16, BF16), epilogue=relu2,
                             jobs=[gather(["w_down"], 0.8, pieces=[((0, 1), far, False)], base=[wdown_b])])
    hmlp = _matmul("mlp_down", a2b, wdown)

    def ln2_loss(x1t, ht, tt, g, b):
        xhat, rstd = _ln_stats(alpha * x1t + ht)
        err = xhat * g + b - tt
        dy = err / D
        dz = _ln_bwd(dy, xhat, rstd, g)
        lrow = 0.5 * jnp.mean(jnp.square(err), axis=-1, keepdims=True)
        return dz, dz, dy * xhat, dy, jnp.broadcast_to(lrow, (lrow.shape[0], LANE))

    dz2, dz2b, dg2, db2, lpart = _rowwise("ln2_loss", ln2_loss, [(x1, D, 0), (hmlp, D, 0), (tgt, D, 0)], [ln2_g, ln2_b],
                                          [(D, F32), (D, BF16)], [D, D, LANE], tr)
    loss = lax.psum(jnp.sum(lpart[:, 0]), ("x", "y", "c"))

    def pair(grads):
        ns = list(grads)
        return _PairJob([grads[n] for n in ns], [kind[n] for n in ns], [big[n].shape for n in ns])

    my_core = lax.axis_index("c").astype(jnp.int32).reshape(1)
    my_chip = (2 * lax.axis_index("x") + lax.axis_index("y")).astype(jnp.int32).reshape(1)
    chip_sums = {}

    def pair_sums(grads, theirs):
        for n, t in zip(grads, theirs):
            chip_sums[n] = _pair_sum("pair_sum_" + n, grads[n], kind[n], big[n].shape, t, my_core)
        return {n: chip_sums[n] for n in grads}

    def chip(sums, rels=CHIP_RELS, by_source=None):
        ns = list(sums)
        return _ChipJob([sums[n] for n in ns], [big[n].shape for n in ns], rels, by_source)

    landed = {}
    da = _matmul("mlp_down_dx", dz2b, wdown, tb=True, out_dtypes=(BF16,), extras=(rb,),
                 epilogue=lambda acc, r: (acc * (2.0 * r.astype(F32)),))
    g_down = {"w_down": _matmul("mlp_down_dw", a2b, dz2b, ta=True, out_dtypes=(BF16,))}
    g_wup, *pr = _matmul("mlp_up_dw", x1b, da, ta=True, out_dtypes=(BF16,), jobs=[pair(g_down)])
    sums_down = pair_sums(g_down, pr)
    g_up = {"w_up": g_wup}
    dx1m, l_a, *pr = _matmul("mlp_up_dx", da, wup, tb=True, jobs=[chip(sums_down, CHIP_RELS[:2]), pair(g_up)])
    sums_up = pair_sums(g_up, pr)

    def ln1_bwd(dm, dzz, z, g):
        xhat, rstd = _ln_stats(z)
        dy = dm + alpha * dzz
        dz = _ln_bwd(dy, xhat, rstd, g)
        return dz, dz, dy * xhat, dy

    dz1, dz1b, dg1, db1 = _rowwise("ln1_bwd", ln1_bwd, [(dx1m, D, 0), (dz2, D, 0), (z1, D, 0)], [ln1_g],
                                   [(D, F32), (D, BF16)], [D, D], tr)
    g_wout = _matmul("mix_out_dw", mixin, dz1b, ta=True, out_dtypes=(BF16,))
    dmix = _matmul("mix_out_dx", dz1b, wout, tb=True)

    dqh, dkn, dkr, dvv, l_b, l_c = _attention_bwd(qh, kh, vx, att_o, lse, dmix, H,
                                                  jobs=[chip(sums_down, CHIP_RELS[2:]), chip(sums_up)])
    landed["w_down"], landed["w_up"] = [l_a, l_b], [l_c]
    att_scale = (MLA_NOPE + MLA_ROPE) ** -0.5

    def mla_heads_bwd(dq, dkrh, cp, sa, sb):
        parts, acc = [], dkrh[:, :LANE]
        dq = dq * att_scale
        for h in range(H):
            o = h * MLA_HEAD_PAD
            parts += [dq[:, o:o + MLA_NOPE], _unrope64(dq[:, o + MLA_NOPE:o + MLA_HEAD_PAD], cp, sa, sb)]
            if h:
                acc = acc + dkrh[:, h * LANE:(h + 1) * LANE]
        return jnp.concatenate(parts, axis=1), _unrope64(acc, cp, sa, sb)

    dqb, dkr128 = _rowwise(
        "mla_heads_bwd", mla_heads_bwd,
        [(dqh, H * MLA_HEAD_PAD, 0), (dkr, H * LANE, 0), (cosp, LANE, 0), (sina, LANE, 0), (sinb, LANE, 0)],
        [], [(H * MLA_HEAD_PAD, BF16), (LANE, F32)], [], tr)
    dcqn = _matmul("mla_q_dx", dqb, wuq, tb=True)
    g_wuq = _matmul("mla_q_dw", cqn, dqb, ta=True, out_dtypes=(BF16,))
    dkvb = jnp.concatenate([dkn, dvv], axis=1)
    dckvn = _matmul("mla_kv_dx", dkvb, wukv, tb=True)
    g_wukv = _matmul("mla_kv_dw", ckvn, dkvb, ta=True, out_dtypes=(BF16,))

    def rms_bwd(c, dy, g):
        rstd = lax.rsqrt(jnp.mean(jnp.square(c), axis=-1, keepdims=True) + EPS)
        dyg = dy * g
        dc = rstd * (dyg - c * (rstd * rstd) * jnp.mean(dyg * c, axis=-1, keepdims=True))
        return dc, dy * c * rstd

    def mla_prep_bwd(p, dq_, dkv_, dkr_, gq, gkv):
        dcq, dgq_ = rms_bwd(p[:, :c_kv], dq_, gq)
        dckv, dgkv_ = rms_bwd(p[:, c_kv:c_kr], dkv_, gkv)
        pad = jnp.zeros((p.shape[0], mla_in_pad - c_kr - LANE), F32)
        return jnp.concatenate([dcq, dckv, dkr_, pad], axis=1), dgq_, dgkv_

    d_p2, dgq, dgkv = _rowwise(
        "mla_prep_bwd", mla_prep_bwd, [(p2, mla_in_pad, 0), (dcqn, MLA_Q_RANK, 0), (dckvn, MLA_KV_RANK, 0), (dkr128, LANE, 0)],
        [q_norm_g, kv_norm_g], [(mla_in_pad, BF16)], [MLA_Q_RANK, MLA_KV_RANK], tr)

    def gate_bwd(pre, gate, dout, gn):
        dpre, dgate, dgn = [], [], []
        for h in range(H):
            sl = slice(h * RET_DV, (h + 1) * RET_DV)
            nrm, rstd = _ln_stats(pre[:, sl])
            gate_h, do_h, gn_h = gate[:, sl], dout[:, sl], gn[:, sl]
            sg = jax.nn.sigmoid(gate_h)
            silu = gate_h * sg
            dgate.append(do_h * (nrm * gn_h) * (sg * (1.0 + gate_h * (1.0 - sg))))
            dn = do_h * silu
            dgn.append(dn * nrm)
            dpre.append(_ln_bwd(dn, nrm, rstd, gn_h))
        return jnp.concatenate(dpre, axis=1), jnp.concatenate(dgate, axis=1), jnp.concatenate(dgn, axis=1)

    dpre, drg, dgn = _rowwise("ret_gate_bwd", gate_bwd, [(ret_pre, RW, 0), (p1, RW, 3), (dmix, RW, 0)], [ret_gn_g],
                              [(RW, F32), (RW, BF16)], [RW], tr)
    g_uq = g_wuq.reshape(MLA_Q_RANK, H, MLA_HEAD_PAD)[:, :, :f_uq].reshape(MLA_Q_RANK, N_DEV, H * f_uq // N_DEV)
    mid_grads = {"w_uq": jnp.transpose(g_uq, (1, 0, 2)), "w_uk": g_wukv[:, :RW], "w_uv": g_wukv[:, RW:], "w_out": g_wout}
    (drq, *pr), (drk, drv) = _retention_bwd(p1, dpre, cos2, sin2, H, dq_jobs=[pair(mid_grads)])
    sums_mid = pair_sums(mid_grads, pr)
    d_p1 = jnp.concatenate([drq, drk, drv, drg], axis=1)

    g_w1, *l_a = _matmul("proj_ret_dw", xb, d_p1, ta=True, out_dtypes=(BF16,), jobs=[chip(sums_mid)])
    for n, a_ in zip(mid_grads, l_a):
        landed[n] = [a_]
    g_w2 = _matmul("proj_mla_dw", xb, d_p2, ta=True, out_dtypes=(BF16,))

    g_win = jnp.concatenate([g_w1, g_w2[:, :mla_in]], axis=1)
    g_win = {"w_in": jnp.transpose(g_win.reshape(D, N_DEV, in_width // N_DEV), (1, 0, 2))}
    small = jnp.concatenate([dgq, dgkv, dgn, dg1, db1, dg2, db2], axis=1)
    dx_a, *pr = _matmul("proj_mla_dx", d_p2, w2, tb=True, extras=(dz1,), epilogue=lambda acc, d: (acc + alpha * d,),
                        jobs=[pair(g_win)])
    sums_in = pair_sums(g_win, pr)
    grad_x, l_a, small_landed = _matmul("proj_ret_dx", d_p1, w1, tb=True, extras=(dx_a,), epilogue=lambda acc, d: (acc + d,),
                                        jobs=[chip(sums_in, CHIP_RELS[:2], by_source=small)])

    moments = dict(zip(names, zip([m_w_in, m_w_uq, m_w_uk, m_w_uv, m_w_out, m_w_up, m_w_down],
                                  [v_w_in, v_w_uq, v_w_uk, v_w_uv, v_w_out, v_w_up, v_w_down])))
    res = {}

    def adamw(n, jobs=()):
        out = _adamw("adamw_" + n, chip_sums[n], my_chip, landed[n], big[n], moments[n][0][0], moments[n][1][0], jobs=jobs)
        res[n] = [r[None] for r in out[:4]]
        return out[4:]

    (l_b,) = adamw("w_up", jobs=[_ChipJob([sums_in["w_in"]], [big["w_in"].shape], CHIP_RELS[2:], piece=(0, 2))])
    (l_b,) = adamw("w_down", jobs=[_ChipJob([sums_in["w_in"]], [big["w_in"].shape], CHIP_RELS[2:], piece=(1, 2), base=[l_b])])
    landed["w_in"] = [l_a, l_b]
    for n in names[:5]:
        adamw(n)

    small_names = ["q_norm_g", "kv_norm_g", "ret_gn_g", "ln1_g", "ln1_b", "ln2_g", "ln2_b"]
    small_w = [q_norm_g, kv_norm_g, ret_gn_g, ln1_g, ln1_b, ln2_g, ln2_b]
    small_m = [m_q_norm_g, m_kv_norm_g, m_ret_gn_g, m_ln1_g, m_ln1_b, m_ln2_g, m_ln2_b]
    small_v = [v_q_norm_g, v_kv_norm_g, v_ret_gn_g, v_ln1_g, v_ln1_b, v_ln2_g, v_ln2_b]
    cat = lambda arrs: jnp.concatenate(arrs, axis=1)
    sres = _adamw_small(small_landed, cat(small_w), cat(small_m), cat(small_v))
    off = 0
    for n, w in zip(small_names, small_w):
        res[n] = [r[:, off:off + w.shape[1]] for r in sres]
        off += w.shape[1]

    order = ["w_in", "q_norm_g", "w_uq", "kv_norm_g", "w_uk", "w_uv", "ret_gn_g", "w_out", "ln1_g", "ln1_b",
             "w_up", "w_down", "ln2_g", "ln2_b"]
    outs = [loss, grad_x[None]]
    for k in range(4):
        outs += [res[n][k] for n in order]
    return tuple(outs)
```

```python
import functools

import numpy as np

import jax
import jax.numpy as jnp
from jax import lax
from jax.experimental import pallas as pl
from jax.experimental.pallas import tpu as pltpu

F32 = jnp.float32
BF16 = jnp.bfloat16

CHUNK = 64
RET_DK = 128
RET_DV = 128
MLA_NOPE = 128
MLA_ROPE = 64
MLA_DV = 128
MLA_Q_RANK = 768
MLA_KV_RANK = 512
MLA_HEAD_PAD = 256
ROPE_BASE = 10000.0
EPS = 1e-5
DEPTH = 1
DEEPNORM_ALPHA = (2.0 * DEPTH) ** 0.25
ADAM_LR = 0.001
ADAM_B1 = 0.9
ADAM_B2 = 0.999
ADAM_EPS = 1e-08
ADAM_WD = 0.01
ADAM_STEP = 10

N_DEV = 8
LANE = 128
SUBLANE = 8
VMEM_LIMIT = 48 * 1024 * 1024
ADAMW_TILE_ELEMS = 128 * 1024
PAIR_TILE_ELEMS = 1024 * 1024

MESH = pl.DeviceIdType.MESH
ANY = pl.BlockSpec(memory_space=pl.ANY)

NN = (((1,), (0,)), ((), ()))
NT = (((1,), (1,)), ((), ()))
TN = (((0,), (0,)), ((), ()))


def _dot(a, b, dims=NN):
    return lax.dot_general(a.astype(BF16), b.astype(BF16), dims, preferred_element_type=F32)


def _params(n_axes):
    return pltpu.CompilerParams(dimension_semantics=("arbitrary",) * n_axes, vmem_limit_bytes=VMEM_LIMIT)


def _tile(n, want):
    if n <= want:
        return n
    t = want
    while t >= LANE:
        if n % t == 0:
            return t
        t -= LANE
    return n


def _my_place():
    return lax.axis_index("x"), lax.axis_index("y"), lax.axis_index("c")


def _flat(px, py, pc):
    return 4 * px + 2 * py + pc


def _block_of(ref, kind, idx, rows, cols, sub=None):
    lo, n = (0, rows) if sub is None else sub
    if kind == "blk":
        return ref.at[idx, pl.ds(lo, n), :]
    if kind == "rows":
        return ref.at[pl.ds(pl.multiple_of(idx * rows + lo, n), n), :]
    return ref.at[pl.ds(lo, n), pl.ds(pl.multiple_of(idx * cols, cols), cols)]


CHIP_X, CHIP_Y, CHIP_DIAGONAL = 0, 1, 2
ALL_CHIPS = (CHIP_X, CHIP_Y, CHIP_DIAGONAL)


class _GatherJob:
    def __init__(self, shards, kinds, mid, pieces=(((0, 1), ALL_CHIPS, True),), base=None):
        nw = len(shards)
        self.nw, self.kinds, self.pieces = nw, list(kinds), list(pieces)
        self.shapes = [s.shape for s in shards]
        self.ins = list(shards) + (list(base) if base is not None else [])
        self.alias = [(nw + w, w) for w in range(nw)] if base is not None else []
        self.out_shape = []
        for s, kind in zip(shards, kinds):
            r, c = s.shape
            shape = {"blk": (N_DEV, r, c), "rows": (N_DEV * r, c), "cols": (r, N_DEV * c)}[kind]
            self.out_shape.append(jax.ShapeDtypeStruct(shape, s.dtype))
        n = 7 * nw * len(self.pieces)
        self.sems = [pltpu.SemaphoreType.DMA((n,)), pltpu.SemaphoreType.DMA((n,)), pltpu.SemaphoreType.DMA((n,))]
        self.phases = [(0.0, self._start), (mid, self._forward), (1.0, self._finish)]

    def _each(self, src, dst, send_sems, recv_sems, local_sems):
        x, y, c = _my_place()
        all_chips = [(1 - x, y), (x, 1 - y), (1 - x, 1 - y)]
        out = []
        for w in range(self.nw):
            r, cc = self.shapes[w]
            for p, ((k_of, of), chips, own) in enumerate(self.pieces):
                sub = (k_of * (r // of), r // of)
                s0 = 7 * (w * len(self.pieces) + p)
                mine = src[w].at[pl.ds(*sub), :]

                def place(dev, w=w, r=r, cc=cc, sub=sub):
                    return _block_of(dst[w], self.kinds[w], _flat(*dev), r, cc, sub)

                def copy(k, block, to, from_src=False, place=place, mine=mine, s0=s0):
                    return pltpu.make_async_remote_copy(
                        src_ref=mine if from_src else place(block), dst_ref=place(block),
                        send_sem=send_sems.at[s0 + k], recv_sem=recv_sems.at[s0 + k], device_id=to, device_id_type=MESH)

                local = pltpu.make_async_copy(mine, place((x, y, c)), local_sems.at[s0])
                out.append(([(j, all_chips[j]) for j in chips], own, copy, local))
        return (x, y, c), (x, y, 1 - c), out

    def _start(self, *refs):
        me, sibling, each = self._each(*refs)
        for chips, own, copy, local in each:
            if own:
                local.start()
                copy(0, me, sibling, from_src=True).start()
            for j, chip in chips:
                copy(1 + j, me, (*chip, me[2]), from_src=True).start()

    def _forward(self, *refs):
        me, sibling, each = self._each(*refs)
        for chips, own, copy, local in each:
            for j, chip in chips:
                copy(1 + j, (*chip, me[2]), me).wait_recv()
                copy(4 + j, (*chip, me[2]), sibling).start()

    def _finish(self, *refs):
        me, sibling, each = self._each(*refs)
        for chips, own, copy, local in each:
            if own:
                copy(0, sibling, me).wait_recv()
            for j, chip in chips:
                copy(4 + j, (*chip, 1 - me[2]), me).wait_recv()
        for chips, own, copy, local in each:
            if own:
                copy(0, me, sibling, from_src=True).wait_send()
                local.wait()
            for j, chip in chips:
                copy(1 + j, me, (*chip, me[2]), from_src=True).wait_send()
                copy(4 + j, (*chip, me[2]), sibling).wait_send()


REL_ALL = [(1, 0, 0), (1, 0, 1), (0, 1, 0), (0, 1, 1), (1, 1, 0), (1, 1, 1), (0, 0, 1)]
N_CHIP = 4


class _PairJob:
    def __init__(self, grads, kinds, shard_shapes):
        self.ins = list(grads)
        self.kinds, self.shard_shapes, self.nw = list(kinds), list(shard_shapes), len(grads)
        self.out_shape = [jax.ShapeDtypeStruct((N_CHIP,) + tuple(s), g.dtype) for g, s in zip(grads, shard_shapes)]
        n = N_CHIP * self.nw
        self.sems = [pltpu.SemaphoreType.DMA((n,)), pltpu.SemaphoreType.DMA((n,))]
        self.phases = [(0.0, self._start), (1.0, self._finish)]

    def _copies(self, src, theirs, send_sems, recv_sems):
        x, y, c = _my_place()
        remote = []
        for w in range(self.nw):
            r, cc = self.shard_shapes[w]
            for q in range(N_CHIP):
                s = N_CHIP * w + q
                remote.append(pltpu.make_async_remote_copy(
                    src_ref=_block_of(src[w], self.kinds[w], 2 * q + (1 - c), r, cc), dst_ref=theirs[w].at[q],
                    send_sem=send_sems.at[s], recv_sem=recv_sems.at[s], device_id=(x, y, 1 - c), device_id_type=MESH))
        return remote

    def _start(self, *refs):
        for cp in self._copies(*refs):
            cp.start()

    def _finish(self, *refs):
        for cp in self._copies(*refs):
            cp.wait()


CHIP_RELS = [(1, 0), (0, 1), (1, 1)]


class _ChipJob:
    def __init__(self, chip_sums, shard_shapes, rels=CHIP_RELS, by_source=None, piece=(0, 1), base=None):
        nw = len(chip_sums)
        self.ins = list(chip_sums) + ([by_source] if by_source is not None else []) + (list(base) if base is not None else [])
        self.nw, self.has_src, self.rels, self.piece = nw, by_source is not None, list(rels), piece
        self.rows = [s[0] for s in shard_shapes]
        self.alias = [(nw + int(self.has_src) + w, w) for w in range(nw)] if base is not None else []
        self.out_shape = [jax.ShapeDtypeStruct((len(rels),) + tuple(s), a.dtype) for a, s in zip(chip_sums, shard_shapes)]
        if self.has_src:
            self.out_shape.append(jax.ShapeDtypeStruct((N_DEV,) + by_source.shape, by_source.dtype))
        n = len(rels) * self.nw + len(REL_ALL) * int(self.has_src)
        self.sems = [pltpu.SemaphoreType.DMA((n,)), pltpu.SemaphoreType.DMA((n,)), pltpu.SemaphoreType.DMA((1,))]
        self.phases = [(0.0, self._start), (1.0, self._finish)]

    def _copies(self, src, land, send_sems, recv_sems, local_sems):
        x, y, c = _my_place()
        nw = self.nw

        def remote_copy(s, src_ref, dst_ref, peer):
            return pltpu.make_async_remote_copy(src_ref=src_ref, dst_ref=dst_ref, send_sem=send_sems.at[s],
                                                recv_sem=recv_sems.at[s], device_id=peer, device_id_type=MESH)

        def rows(w):
            n = self.rows[w] // self.piece[1]
            return pl.ds(self.piece[0] * n, n)

        local, remote = [], []
        for k, (fx, fy) in enumerate(self.rels):
            px, py = x ^ fx, y ^ fy
            remote += [remote_copy(nw * k + w, src[w].at[2 * px + py, rows(w), :], land[w].at[k, rows(w), :], (px, py, c))
                       for w in range(nw)]
        if self.has_src:
            my_idx = _flat(x, y, c)
            local.append(pltpu.make_async_copy(src[nw], land[nw].at[my_idx], local_sems.at[0]))
            remote += [remote_copy(nw * len(self.rels) + k, src[nw], land[nw].at[my_idx], (x ^ fx, y ^ fy, c ^ fc))
                       for k, (fx, fy, fc) in enumerate(REL_ALL)]
        return local, remote

    def _start(self, *refs):
        local, remote = self._copies(*refs)
        for cp in local + remote:
            cp.start()

    def _finish(self, *refs):
        local, remote = self._copies(*refs)
        for cp in remote + local:
            cp.wait()


def _row_tile(R, C, elems=ADAMW_TILE_ELEMS):
    tr = 16
    while tr * 2 * C <= elems and R % (tr * 2) == 0:
        tr *= 2
    return tr if R % tr == 0 else R


BF16_ROWS = 16


def _tile_2d(R, C, elems):
    if R % BF16_ROWS == 0:
        tr = _row_tile(R, C, elems)
        return tr, C, R // tr
    tc = C
    while tc % (2 * LANE) == 0 and R * tc > elems:
        tc //= 2
    return R, tc, C // tc


def _tile_index(tr, tc, C):
    return (lambda i: (i, 0)) if tc == C else (lambda i: (0, i))


def _part_spec(kind, tr, tc, R, C, n_lead):
    at = _tile_index(tr, tc, C)

    def dev(ids, pref):
        return 2 * ids[0] + pref[0] if n_lead == 2 else pref[0]

    def index(*args):
        ids, pref = args[:n_lead], args[n_lead]
        ri, ci = at(ids[-1])
        if kind == "blk":
            return dev(ids, pref), ri, ci
        if kind == "rows":
            return dev(ids, pref) * (R // tr) + ri, ci
        return ri, dev(ids, pref) * (C // tc) + ci

    return pl.BlockSpec((None, tr, tc) if kind == "blk" else (tr, tc), index)


def _pair_sum(name, grad, kind, shard_shape, theirs, c):
    R, C = shard_shape
    tr, tc, tiles = _tile_2d(R, C, PAIR_TILE_ELEMS)
    at = _tile_index(tr, tc, C)

    def body(c_ref, g_ref, t_ref, o_ref):
        o_ref[...] = (g_ref[...].astype(F32) + t_ref[...].astype(F32)).astype(o_ref.dtype)

    slot = pl.BlockSpec((None, tr, tc), lambda q, i, c_ref: (q, *at(i)))
    return pl.pallas_call(
        body, name=name,
        grid_spec=pltpu.PrefetchScalarGridSpec(
            num_scalar_prefetch=1, grid=(N_CHIP, tiles),
            in_specs=[_part_spec(kind, tr, tc, R, C, 2), slot], out_specs=slot),
        out_shape=jax.ShapeDtypeStruct(theirs.shape, theirs.dtype), compiler_params=_params(2))(c, grad, theirs)


class _Carried:
    def __init__(self, jobs, grid):
        self.jobs = list(jobs)
        self.steps = int(np.prod(grid))
        self.grid = tuple(grid)
        self.operands = [a for j in self.jobs for a in j.ins]
        self.out_shape = [o for j in self.jobs for o in j.out_shape]
        self.sems = [s for j in self.jobs for s in j.sems]
        self.in_specs = [ANY] * len(self.operands)
        self.out_specs = [ANY] * len(self.out_shape)

    def aliases(self, n_in, n_out):
        out, i0, o0 = {}, n_in, n_out
        for job in self.jobs:
            for ji, jo in getattr(job, "alias", []):
                out[i0 + ji] = o0 + jo
            i0, o0 = i0 + len(job.ins), o0 + len(job.out_shape)
        return out

    def _step(self):
        lin = pl.program_id(0)
        for a in range(1, len(self.grid)):
            lin = lin * self.grid[a] + pl.program_id(a)
        return lin

    def _run(self, in_refs, out_refs, sem_refs, last):
        lin = self._step()
        for job in self.jobs:
            ins, in_refs = in_refs[:len(job.ins)], in_refs[len(job.ins):]
            outs, out_refs = out_refs[:len(job.out_shape)], out_refs[len(job.out_shape):]
            sems, sem_refs = sem_refs[:len(job.sems)], sem_refs[len(job.sems):]
            for frac, fn in job.phases:
                if (frac >= 1.0) != last:
                    continue
                at = min(self.steps - 1, int(frac * (self.steps - 1)))
                pl.when(lin == at)(functools.partial(fn, ins, outs, *sems))

    def before(self, in_refs, out_refs, sem_refs):
        self._run(in_refs, out_refs, sem_refs, last=False)

    def after(self, in_refs, out_refs, sem_refs):
        self._run(in_refs, out_refs, sem_refs, last=True)


def _split_refs(refs, n_in, n_job_in, n_out, n_job_out, n_scratch):
    cuts = np.cumsum([0, n_in, n_job_in, n_out, n_job_out, n_scratch])
    return [refs[cuts[i]:cuts[i + 1]] for i in range(5)] + [refs[cuts[5]:]]


def _matmul(name, a, b, *, ta=False, tb=False, out_dtypes=(F32,), epilogue=None, extras=(), tm=1024, tn=1024, tk=2048, jobs=()):
    if ta:
        K, M = a.shape
    else:
        M, K = a.shape
    N = b.shape[0] if tb else b.shape[1]
    tm, tn, tk = _tile(M, tm), _tile(N, tn), _tile(K, tk)
    nk = K // tk
    grid = (M // tm, N // tn, nk)
    dims = TN if ta else (NT if tb else NN)
    n_ex, n_out = len(extras), len(out_dtypes)
    carried = _Carried(jobs, grid)

    def body(*refs):
        ins, job_ins, out_refs, job_outs, scratch, job_sems = _split_refs(
            refs, 2 + n_ex, len(carried.operands), n_out, len(carried.out_shape), int(nk > 1))
        a_ref, b_ref, ex_refs = ins[0], ins[1], ins[2:]
        k = pl.program_id(2)
        carried.before(job_ins, job_outs, job_sems)

        def finish(acc):
            res = (acc,) if epilogue is None else epilogue(acc, *[e[...] for e in ex_refs])
            for o_ref, r in zip(out_refs, res):
                o_ref[...] = r.astype(o_ref.dtype)

        if nk == 1:
            finish(_dot(a_ref[...], b_ref[...], dims))
        else:
            acc_ref = scratch[0]

            @pl.when(k == 0)
            def _():
                acc_ref[...] = _dot(a_ref[...], b_ref[...], dims)

            @pl.when(k > 0)
            def _():
                acc_ref[...] += _dot(a_ref[...], b_ref[...], dims)

            pl.when(k == nk - 1)(lambda: finish(acc_ref[...]))

        carried.after(job_ins, job_outs, job_sems)

    a_spec = pl.BlockSpec((tk, tm), lambda i, j, k: (k, i)) if ta else pl.BlockSpec((tm, tk), lambda i, j, k: (i, k))
    b_spec = pl.BlockSpec((tn, tk), lambda i, j, k: (j, k)) if tb else pl.BlockSpec((tk, tn), lambda i, j, k: (k, j))
    tile_spec = pl.BlockSpec((tm, tn), lambda i, j, k: (i, j))
    outs = pl.pallas_call(
        body,
        name=name,
        grid=grid,
        in_specs=[a_spec, b_spec] + [tile_spec] * n_ex + carried.in_specs,
        out_specs=[tile_spec] * n_out + carried.out_specs,
        out_shape=[jax.ShapeDtypeStruct((M, N), d) for d in out_dtypes] + carried.out_shape,
        scratch_shapes=[pltpu.VMEM((tm, tn), F32)] * int(nk > 1) + carried.sems,
        input_output_aliases=carried.aliases(2 + n_ex, n_out),
        compiler_params=_params(3),
    )(a, b, *extras, *carried.operands)
    return outs[0] if len(outs) == 1 else outs


def _rowwise(name, fn, row_in, const_in, row_out, acc_out, tm, jobs=()):
    T = row_in[0][0].shape[0]
    n_r, n_c, n_o, n_a = len(row_in), len(const_in), len(row_out), len(acc_out)
    carried = _Carried(jobs, (T // tm,))

    def body(*refs):
        in_refs, job_ins, own_outs, job_outs, _, job_sems = _split_refs(
            refs, n_r + n_c, len(carried.operands), n_o + n_a, len(carried.out_shape), 0)
        out_refs, acc_refs = own_outs[:n_o], own_outs[n_o:]
        carried.before(job_ins, job_outs, job_sems)
        res = fn(*[r[...] for r in in_refs])
        for o_ref, r in zip(out_refs, res[:n_o]):
            o_ref[...] = r.astype(o_ref.dtype)

        @pl.when(pl.program_id(0) == 0)
        def _():
            for a_ref in acc_refs:
                a_ref[...] = jnp.zeros_like(a_ref)

        for a_ref, r in zip(acc_refs, res[n_o:]):
            a_ref[...] += jnp.sum(r.reshape(tm // SUBLANE, SUBLANE, r.shape[-1]), axis=0)
        carried.after(job_ins, job_outs, job_sems)

    in_specs = [pl.BlockSpec((tm, w), functools.partial(lambda i, c: (i, c), c=cb)) for (_, w, cb) in row_in]
    in_specs += [pl.BlockSpec(c.shape, functools.partial(lambda i, nd: (0,) * nd, nd=c.ndim)) for c in const_in]
    out_specs = [pl.BlockSpec((tm, c), lambda i: (i, 0)) for (c, _) in row_out]
    out_specs += [pl.BlockSpec((SUBLANE, c), lambda i: (0, 0)) for c in acc_out]
    out_shape = [jax.ShapeDtypeStruct((T, c), d) for (c, d) in row_out]
    out_shape += [jax.ShapeDtypeStruct((SUBLANE, c), F32) for c in acc_out]
    return pl.pallas_call(
        body, name=name, grid=(T // tm,), in_specs=in_specs + carried.in_specs, out_specs=out_specs + carried.out_specs,
        out_shape=out_shape + carried.out_shape, scratch_shapes=carried.sems,
        input_output_aliases=carried.aliases(n_r + n_c, n_o + n_a), compiler_params=_params(1),
    )(*[r[0] for r in row_in], *const_in, *carried.operands)


def _ln_stats(z):
    mu = jnp.mean(z, axis=-1, keepdims=True)
    var = jnp.mean(jnp.square(z - mu), axis=-1, keepdims=True)
    rstd = lax.rsqrt(var + EPS)
    return (z - mu) * rstd, rstd


def _ln_bwd(dy, xhat, rstd, g):
    dxh = dy * g
    return rstd * (dxh - jnp.mean(dxh, axis=-1, keepdims=True) - xhat * jnp.mean(dxh * xhat, axis=-1, keepdims=True))


def _rms(x, g):
    return x * lax.rsqrt(jnp.mean(jnp.square(x), axis=-1, keepdims=True) + EPS) * g


def _rope128(x, cos2, sin2):
    return x * cos2 + pltpu.roll(x, RET_DK // 2, 1) * sin2


def _unrope128(dy, cos2, sin2):
    return dy * cos2 + pltpu.roll(dy * sin2, RET_DK // 2, 1)


def _rope64(x, cosp, sina, sinb):
    h = MLA_ROPE // 2
    return x * cosp + pltpu.roll(x, h, 1) * sina + pltpu.roll(x, LANE - h, 1) * sinb


def _unrope64(dy, cosp, sina, sinb):
    h = MLA_ROPE // 2
    return dy * cosp + pltpu.roll(dy * sina, LANE - h, 1) + pltpu.roll(dy * sinb, h, 1)


RET_BLOCK = 256
RET_HEADS_PER_STEP = 4


def _ret_tables(H, B):
    log_g = np.log1p(-np.exp2(-5.0 - np.arange(H, dtype=np.float32))).astype(np.float32)
    idx = np.arange(B, dtype=np.float32)
    dist = np.abs(idx[:, None] - idx[None, :])
    visible = (np.arange(B)[None, :] // CHUNK) <= (np.arange(B)[:, None] // CHUNK)
    dmat = np.exp(log_g[:, None, None] * dist).astype(np.float32) * visible[None].astype(np.float32)
    qd = np.exp(log_g[:, None] * (idx + 1.0)).astype(np.float32)
    kd = np.exp(log_g[:, None] * (B - 1 - idx)).astype(np.float32)
    sd = np.exp(log_g * B).astype(np.float32)
    bc = lambda v: np.ascontiguousarray(np.broadcast_to(v[:, :, None], (H, v.shape[1], LANE)))
    sdb = np.ascontiguousarray(np.broadcast_to(sd[:, None, None], (H, SUBLANE, LANE)))
    return jnp.asarray(dmat), jnp.asarray(bc(qd)), jnp.asarray(bc(kd)), jnp.asarray(sdb)


def _ret_specs(H, B, reverse, nb):
    blk = (lambda b: nb - 1 - b) if reverse else (lambda b: b)
    G = min(RET_HEADS_PER_STEP, H)
    W = G * LANE
    col = lambda off: pl.BlockSpec((B, W), functools.partial(lambda h, b, off: (blk(b), off // G + h), off=off))
    row = pl.BlockSpec((B, LANE), lambda h, b: (blk(b), 0))
    per_head = lambda r: pl.BlockSpec((G, r, LANE), lambda h, b: (h, 0, 0))
    dmat = pl.BlockSpec((G, B, B), lambda h, b: (h, 0, 0))
    return G, col, row, per_head, dmat


def _ret_call(name, body, n_in, n_out, operands, in_specs, out_specs, out_shape, grid, G, jobs, own_alias=None):
    carried = _Carried(jobs, grid)

    def wrapped(*refs):
        ins, job_ins, outs, job_outs, (state,), job_sems = _split_refs(
            refs, n_in, len(carried.operands), n_out, len(carried.out_shape), 1)
        carried.before(job_ins, job_outs, job_sems)

        @pl.when(pl.program_id(1) == 0)
        def _():
            state[...] = jnp.zeros_like(state)

        body(*ins, *outs, state)
        carried.after(job_ins, job_outs, job_sems)

    return pl.pallas_call(
        wrapped, name=name, grid=grid,
        in_specs=in_specs + carried.in_specs, out_specs=out_specs + carried.out_specs,
        out_shape=out_shape + carried.out_shape,
        scratch_shapes=[pltpu.VMEM((G, RET_DK, RET_DV), F32)] + carried.sems,
        input_output_aliases={**(own_alias or {}), **carried.aliases(n_in, n_out)},
        compiler_params=_params(2),
    )(*operands, *carried.operands)


def _retention_fwd(p1, cos2, sin2, gn_g, mix_in, H, jobs=()):
    T = p1.shape[0]
    B = min(RET_BLOCK, T)
    nb = T // B
    scale = RET_DK ** -0.5
    dmat, qd, kd, sd = _ret_tables(H, B)
    G, col, row, per_head, dspec = _ret_specs(H, B, False, nb)

    def body(q_ref, k_ref, v_ref, g_ref, cos_ref, sin_ref, d_ref, qd_ref, kd_ref, sd_ref, gn_ref, _, pre_ref, out_ref, s_ref):
        cos, sin = cos_ref[...], sin_ref[...]
        for g in range(G):
            hl = pl.ds(g * LANE, LANE)
            q = _rope128(q_ref[:, hl], cos, sin)
            k = _rope128(k_ref[:, hl], cos, sin) * scale
            v = v_ref[:, hl]
            s = s_ref[g]
            a = _dot(q, k, NT) * d_ref[g]
            ret = _dot(a, v) + _dot(q * qd_ref[g], s)
            s_ref[g] = sd_ref[g, 0:1, :] * s + _dot(k * kd_ref[g], v, TN)
            pre_ref[:, hl] = ret
            nrm, _ = _ln_stats(ret)
            out_ref[:, hl] = (jax.nn.silu(g_ref[:, hl]) * (nrm * gn_ref[:, hl])).astype(out_ref.dtype)

    out_spec = pl.BlockSpec((B, G * LANE), lambda h, b: (b, h))
    return _ret_call(
        "retention_fwd", body, 12, 2, (p1, p1, p1, p1, cos2, sin2, dmat, qd, kd, sd, gn_g, mix_in),
        [col(0), col(H), col(2 * H), col(3 * H), row, row, dspec, per_head(B), per_head(B), per_head(SUBLANE),
         pl.BlockSpec((1, G * LANE), lambda h, b: (0, h)), ANY],
        [out_spec, out_spec],
        [jax.ShapeDtypeStruct((T, H * RET_DV), F32), jax.ShapeDtypeStruct(mix_in.shape, mix_in.dtype)],
        (H // G, nb), G, jobs, own_alias={11: 1})


def _retention_bwd(p1, dpre, cos2, sin2, H, dq_jobs=(), dkv_jobs=()):
    T = p1.shape[0]
    B = min(RET_BLOCK, T)
    nb = T // B
    scale = RET_DK ** -0.5
    dmat, qd, kd, sd = _ret_tables(H, B)

    def dq_body(k_ref, v_ref, do_ref, cos_ref, sin_ref, d_ref, qd_ref, kd_ref, sd_ref, dq_ref, s_ref):
        cos, sin = cos_ref[...], sin_ref[...]
        for g in range(G):
            hl = pl.ds(g * LANE, LANE)
            k = _rope128(k_ref[:, hl], cos, sin) * scale
            v, do, s = v_ref[:, hl], do_ref[:, hl], s_ref[g]
            da = _dot(do, v, NT) * d_ref[g]
            dq = _dot(da, k) + _dot(do, s, NT) * qd_ref[g]
            dq_ref[:, hl] = _unrope128(dq, cos, sin).astype(dq_ref.dtype)
            s_ref[g] = sd_ref[g, 0:1, :] * s + _dot(k * kd_ref[g], v, TN)

    G, col, row, per_head, dspec = _ret_specs(H, B, False, nb)
    dq = _ret_call(
        "retention_bwd_dq", dq_body, 9, 1, (p1, p1, dpre, cos2, sin2, dmat, qd, kd, sd),
        [col(H), col(2 * H), col(0), row, row, dspec, per_head(B), per_head(B), per_head(SUBLANE)],
        [pl.BlockSpec((B, G * LANE), lambda h, b: (b, h))], [jax.ShapeDtypeStruct((T, H * RET_DK), BF16)],
        (H // G, nb), G, dq_jobs)

    def dkv_body(q_ref, k_ref, v_ref, do_ref, cos_ref, sin_ref, d_ref, qd_ref, kd_ref, sd_ref, dk_ref, dv_ref, g_ref):
        cos, sin = cos_ref[...], sin_ref[...]
        for g in range(G):
            hl = pl.ds(g * LANE, LANE)
            q = _rope128(q_ref[:, hl], cos, sin)
            k = _rope128(k_ref[:, hl], cos, sin) * scale
            v, do, st = v_ref[:, hl], do_ref[:, hl], g_ref[g]
            a = _dot(q, k, NT) * d_ref[g]
            da = _dot(do, v, NT) * d_ref[g]
            dv = _dot(a, do, TN) + _dot(k * kd_ref[g], st)
            dk = _dot(da, q, TN) + _dot(v, st, NT) * kd_ref[g]
            dv_ref[:, hl] = dv.astype(dv_ref.dtype)
            dk_ref[:, hl] = _unrope128(dk * scale, cos, sin).astype(dk_ref.dtype)
            g_ref[g] = sd_ref[g, 0:1, :] * st + _dot(q * qd_ref[g], do, TN)

    G, col, row, per_head, dspec = _ret_specs(H, B, True, nb)
    out_spec = pl.BlockSpec((B, G * LANE), lambda h, b: (nb - 1 - b, h))
    dkv = _ret_call(
        "retention_bwd_dkv", dkv_body, 10, 2, (p1, p1, p1, dpre, cos2, sin2, dmat, qd, kd, sd),
        [col(0), col(H), col(2 * H), col(0), row, row, dspec, per_head(B), per_head(B), per_head(SUBLANE)],
        [out_spec, out_spec],
        [jax.ShapeDtypeStruct((T, H * RET_DK), BF16), jax.ShapeDtypeStruct((T, H * RET_DV), BF16)],
        (H // G, nb), G, dkv_jobs)
    return dq, dkv


ATT_BLOCK = 512


ATT_ROW_CHUNKS = 2
ATT_HEADS_PER_STEP = 8
ATT_BWD_HEADS_PER_STEP = 2
LOG2E = 1.4426950408889634


def _chunk_mask(rows, row0, cols):
    qi = (lax.broadcasted_iota(jnp.int32, (rows, cols), 0) + row0) // CHUNK
    kj = lax.broadcasted_iota(jnp.int32, (rows, cols), 1) // CHUNK
    return kj <= qi


def _attention_fwd(qh, kh, vx, H, jobs=()):
    T = qh.shape[0]
    t = min(ATT_BLOCK, T)
    n = T // t
    r = t // ATT_ROW_CHUNKS
    G = min(ATT_HEADS_PER_STEP, H)
    P = MLA_HEAD_PAD
    scale = (MLA_NOPE + MLA_ROPE) ** -0.5
    c2 = scale * LOG2E
    carried = _Carried(jobs, (H // G, n, n))

    def body(*refs):
        (q_ref, k_ref, v_ref), job_ins, (o_ref, ob_ref, lse_ref), job_outs, (m_ref, acc_ref), job_sems = _split_refs(
            refs, 3, len(carried.operands), 3, len(carried.out_shape), 2)
        i, j = pl.program_id(1), pl.program_id(2)
        carried.before(job_ins, job_outs, job_sems)

        @pl.when(j == 0)
        def _():
            m_ref[...] = jnp.full_like(m_ref, -jnp.inf)
            acc_ref[...] = jnp.zeros_like(acc_ref)

        def update(masked):
            for g in range(G):
                hc = pl.ds(g * P, P)
                for ch in range(ATT_ROW_CHUNKS):
                    rs = pl.ds(ch * r, r)
                    s = _dot(q_ref[rs, hc], k_ref[:, hc], NT)
                    if masked:
                        s = jnp.where(_chunk_mask(r, ch * r, t), s, -jnp.inf)
                    m_prev = m_ref[rs, pl.ds(g * LANE, LANE)]
                    m_new = jnp.maximum(m_prev, jnp.max(s, axis=-1, keepdims=True))
                    alpha = jnp.exp2((m_prev - m_new) * c2)
                    p = jnp.exp2((s - jnp.tile(m_new, (1, t // LANE))) * c2)
                    acc_ref[rs, hc] = jnp.tile(alpha, (1, 2)) * acc_ref[rs, hc] + _dot(p, v_ref[:, hc])
                    m_ref[rs, pl.ds(g * LANE, LANE)] = m_new

        pl.when(j < i)(functools.partial(update, False))
        pl.when(j == i)(functools.partial(update, True))

        @pl.when(j == n - 1)
        def _():
            for g in range(G):
                acc = acc_ref[:, pl.ds(g * P, P)]
                l = acc[:, MLA_DV:]
                o = acc[:, :MLA_DV] / l
                o_ref[:, pl.ds(g * MLA_DV, MLA_DV)] = o
                ob_ref[:, pl.ds(g * MLA_DV, MLA_DV)] = o.astype(ob_ref.dtype)
                lse_ref[g] = m_ref[:, pl.ds(g * LANE, LANE)] * scale + jnp.log(l)

        carried.after(job_ins, job_outs, job_sems)

    kclamp = lambda i, j: jnp.minimum(j, i)
    o_spec = pl.BlockSpec((t, G * MLA_DV), lambda h, i, j: (i, h))
    mix_spec = pl.BlockSpec((t, G * MLA_DV), lambda h, i, j: (i, H // G + h))
    return pl.pallas_call(
        body, name="attention_fwd", grid=(H // G, n, n),
        in_specs=[pl.BlockSpec((t, G * P), lambda h, i, j: (i, h)),
                  pl.BlockSpec((t, G * P), lambda h, i, j: (kclamp(i, j), h)),
                  pl.BlockSpec((t, G * P), lambda h, i, j: (kclamp(i, j), h))] + carried.in_specs,
        out_specs=[o_spec, mix_spec, pl.BlockSpec((G, t, LANE), lambda h, i, j: (h, i, 0))] + carried.out_specs,
        out_shape=[jax.ShapeDtypeStruct((T, H * MLA_DV), F32), jax.ShapeDtypeStruct((T, 2 * H * MLA_DV), BF16),
                   jax.ShapeDtypeStruct((H, T, LANE), F32)] + carried.out_shape,
        scratch_shapes=[pltpu.VMEM((t, G * LANE), F32), pltpu.VMEM((t, G * P), F32)] + carried.sems,
        input_output_aliases=carried.aliases(3, 3),
        compiler_params=_params(3),
    )(qh, kh, vx, *carried.operands)


def _attention_bwd(qh, kh, vx, o, lse, dmix, H, jobs=()):
    T = qh.shape[0]
    t = min(ATT_BLOCK, T)
    n = T // t
    r = t // ATT_ROW_CHUNKS
    G = min(ATT_BWD_HEADS_PER_STEP, H)
    P = MLA_HEAD_PAD
    scale = (MLA_NOPE + MLA_ROPE) ** -0.5
    c2 = scale * LOG2E
    carried = _Carried(jobs, (H // G, n, n))

    def body(*refs):
        ((q_ref, k_ref, v_ref, do_ref, o_ref, lse_ref), job_ins, (dq_ref, dkn_ref, dkr_ref, dv_ref), job_outs,
         (dk_acc, dv_acc), job_sems) = _split_refs(refs, 6, len(carried.operands), 4, len(carried.out_shape), 2)
        j, i = pl.program_id(1), pl.program_id(2)
        carried.before(job_ins, job_outs, job_sems)

        @pl.when(jnp.logical_and(j == 0, i == 0))
        def _():
            dq_ref[...] = jnp.zeros_like(dq_ref)

        @pl.when(i == j)
        def _():
            dk_acc[...] = jnp.zeros_like(dk_acc)
            dv_acc[...] = jnp.zeros_like(dv_acc)

        def update(masked):
            for g in range(G):
                hc, hv = pl.ds(g * P, P), pl.ds(g * MLA_DV, MLA_DV)
                k, v = k_ref[:, hc], v_ref[:, pl.ds(g * P, MLA_DV)]
                for ch in range(ATT_ROW_CHUNKS):
                    rs = pl.ds(ch * r, r)
                    q, do = q_ref[rs, hc], do_ref[rs, hv]
                    s = _dot(q, k, NT)
                    if masked:
                        s = jnp.where(_chunk_mask(r, ch * r, t), s, -jnp.inf)
                    p = jnp.exp2(s * c2 - jnp.tile(lse_ref[g, rs, :] * LOG2E, (1, t // LANE)))
                    dv_acc[:, hv] += _dot(p, do, TN)
                    dp = _dot(do, v, NT)
                    delta = jnp.sum(do * o_ref[rs, hv], axis=-1, keepdims=True)
                    ds = p * (dp - delta)
                    rows = pl.ds(pl.multiple_of(i * t, t) + ch * r, r)
                    dq_ref[rows, hc] += _dot(ds, k)
                    dk_acc[:, hc] += _dot(ds, q, TN)

        pl.when(i > j)(functools.partial(update, False))
        pl.when(i == j)(functools.partial(update, True))

        @pl.when(i == n - 1)
        def _():
            for g in range(G):
                dk = dk_acc[:, pl.ds(g * P, P)] * scale
                dkn_ref[:, pl.ds(g * LANE, LANE)] = dk[:, :MLA_NOPE].astype(dkn_ref.dtype)
                dkr_ref[:, pl.ds(g * LANE, LANE)] = dk[:, MLA_NOPE:]
            dv_ref[...] = dv_acc[...].astype(dv_ref.dtype)

        carried.after(job_ins, job_outs, job_sems)

    qclamp = lambda j, i: jnp.maximum(i, j)
    kv_out = pl.BlockSpec((t, G * LANE), lambda h, j, i: (j, h))
    q_in = pl.BlockSpec((t, G * MLA_DV), lambda h, j, i: (qclamp(j, i), h))
    return pl.pallas_call(
        body, name="attention_bwd", grid=(H // G, n, n),
        in_specs=[pl.BlockSpec((t, G * P), lambda h, j, i: (qclamp(j, i), h)),
                  pl.BlockSpec((t, G * P), lambda h, j, i: (j, h)),
                  pl.BlockSpec((t, G * P), lambda h, j, i: (j, h)),
                  pl.BlockSpec((t, G * MLA_DV), lambda h, j, i: (qclamp(j, i), H // G + h)),
                  q_in,
                  pl.BlockSpec((G, t, LANE), lambda h, j, i: (h, qclamp(j, i), 0))] + carried.in_specs,
        out_specs=[pl.BlockSpec((T, G * P), lambda h, j, i: (0, h)), kv_out, kv_out, kv_out] + carried.out_specs,
        out_shape=[jax.ShapeDtypeStruct((T, H * P), F32), jax.ShapeDtypeStruct((T, H * LANE), BF16),
                   jax.ShapeDtypeStruct((T, H * LANE), F32), jax.ShapeDtypeStruct((T, H * LANE), BF16)] + carried.out_shape,
        scratch_shapes=[pltpu.VMEM((t, G * P), F32), pltpu.VMEM((t, G * MLA_DV), F32)] + carried.sems,
        input_output_aliases=carried.aliases(6, 4),
        compiler_params=_params(3),
    )(qh, kh, vx, dmix, o, lse, *carried.operands)


def _adamw_math(w, g, m, v):
    m = ADAM_B1 * m + (1.0 - ADAM_B1) * g
    v = ADAM_B2 * v + (1.0 - ADAM_B2) * jnp.square(g)
    m_hat = m / (1.0 - ADAM_B1 ** ADAM_STEP)
    v_hat = v / (1.0 - ADAM_B2 ** ADAM_STEP)
    delta = -ADAM_LR * (m_hat / (jnp.sqrt(v_hat) + ADAM_EPS) + ADAM_WD * w)
    return delta, m, v


def _adamw(name, chip_sums, my_chip, landed, w, m, v, jobs=()):
    R, C = w.shape
    tr, tc, tiles = _tile_2d(R, C, ADAMW_TILE_ELEMS)
    at = _tile_index(tr, tc, C)
    nl = len(landed)
    assert sum(l.shape[0] for l in landed) == N_CHIP - 1
    carried = _Carried(jobs, (tiles,))

    def body(q_ref, *refs):
        (own_ref, *rest), job_ins, (g_out, d_out, m_out, v_out), job_outs, _, job_sems = _split_refs(
            refs, nl + 4, len(carried.operands), 4, len(carried.out_shape), 0)
        l_refs, (w_ref, m_ref, v_ref) = rest[:nl], rest[nl:]
        carried.before(job_ins, job_outs, job_sems)
        g = own_ref[...].astype(F32)
        for l_ref in l_refs:
            for s in range(l_ref.shape[0]):
                g = g + l_ref[s].astype(F32)
        delta, mn, vn = _adamw_math(w_ref[...], g, m_ref[...], v_ref[...])
        g_out[...] = g
        d_out[...] = delta
        m_out[...] = mn
        v_out[...] = vn
        carried.after(job_ins, job_outs, job_sems)

    spec = pl.BlockSpec((tr, tc), lambda i, q_ref: at(i))
    own_spec = pl.BlockSpec((None, tr, tc), lambda i, q_ref: (q_ref[0], *at(i)))
    l_specs = [pl.BlockSpec((l.shape[0], tr, tc), lambda i, q_ref: (0, *at(i))) for l in landed]
    return pl.pallas_call(
        body, name=name,
        grid_spec=pltpu.PrefetchScalarGridSpec(
            num_scalar_prefetch=1, grid=(tiles,),
            in_specs=[own_spec] + l_specs + [spec, spec, spec] + carried.in_specs,
            out_specs=[spec] * 4 + carried.out_specs, scratch_shapes=carried.sems),
        out_shape=[jax.ShapeDtypeStruct((R, C), F32)] * 4 + carried.out_shape,
        input_output_aliases={1 + i: o for i, o in carried.aliases(nl + 4, 4).items()},
        compiler_params=_params(1),
    )(my_chip, chip_sums, *landed, w, m, v, *carried.operands)


def _adamw_small(landed, w, m, v):
    P = w.shape[1]

    def body(l_ref, w_ref, m_ref, v_ref, g_out, d_out, m_out, v_out):
        acc = l_ref[0]
        for s in range(1, N_DEV):
            acc = acc + l_ref[s]
        g = jnp.sum(acc, axis=0, keepdims=True)
        delta, mn, vn = _adamw_math(w_ref[...], g, m_ref[...], v_ref[...])
        g_out[...] = g
        d_out[...] = delta
        m_out[...] = mn
        v_out[...] = vn

    return pl.pallas_call(
        body, name="adamw_replicated", out_shape=[jax.ShapeDtypeStruct((1, P), F32)] * 4,
    )(landed, w, m, v)


def _rope_rows():
    inv_r = ROPE_BASE ** (-jnp.arange(0, RET_DK, 2, dtype=F32) / RET_DK)
    inv_m = ROPE_BASE ** (-jnp.arange(0, MLA_ROPE, 2, dtype=F32) / MLA_ROPE)
    h = MLA_ROPE // 2
    one, zero = jnp.ones((h,), F32), jnp.zeros((h,), F32)
    pad = jnp.zeros((LANE - MLA_ROPE,), F32)
    rows = [jnp.concatenate([inv_r, inv_r]), jnp.concatenate([-jnp.ones((RET_DK // 2,), F32), jnp.ones((RET_DK // 2,), F32)]),
            jnp.concatenate([inv_m, inv_m, pad]), jnp.concatenate([one, one, pad]),
            jnp.concatenate([zero, one, pad]), jnp.concatenate([-one, zero, pad])]
    return jnp.stack(rows + [jnp.zeros((LANE,), F32)] * (SUBLANE - len(rows)))


def _prep_tile(xt, p, rows):
    ang2, angp = p * rows[0:1], p * rows[2:3]
    sp = jnp.sin(angp)
    return (xt, jnp.cos(ang2), jnp.sin(ang2) * rows[1:2], jnp.cos(angp) * rows[3:4], sp * rows[4:5], sp * rows[5:6])


def kernel(x, positions, w_in, q_norm_g, w_uq, kv_norm_g, w_uk, w_uv, ret_gn_g, w_out, ln1_g, ln1_b, w_up, w_down, ln2_g, ln2_b, loss_target, m_w_in, m_q_norm_g, m_w_uq, m_kv_norm_g, m_w_uk, m_w_uv, m_ret_gn_g, m_w_out, m_ln1_g, m_ln1_b, m_w_up, m_w_down, m_ln2_g, m_ln2_b, v_w_in, v_q_norm_g, v_w_uq, v_kv_norm_g, v_w_uk, v_w_uv, v_ret_gn_g, v_w_out, v_ln1_g, v_ln1_b, v_w_up, v_w_down, v_ln2_g, v_ln2_b):
    T, D = x.shape[1], x.shape[2]
    H = D // 256
    RW = H * RET_DV
    x = x[0]
    tgt = loss_target[0]
    alpha = DEEPNORM_ALPHA
    in_width = w_in.shape[2] * N_DEV
    mla_in = in_width - 4 * RW
    mla_in_pad = -(-mla_in // 512) * 512
    f_uq = MLA_NOPE + MLA_ROPE

    names = ["w_in", "w_uq", "w_uk", "w_uv", "w_out", "w_up", "w_down"]
    big = dict(zip(names, [w_in[0].T, w_uq[0], w_uk[0], w_uv[0], w_out[0], w_up[0], w_down[0]]))
    kind = dict(zip(names, ["blk", "blk", "cols", "cols", "rows", "cols", "rows"]))
    half = {n: w.astype(BF16) for n, w in big.items()}

    def gather(ns, mid, **parts):
        return _GatherJob([half[n] for n in ns], [kind[n] for n in ns], mid, **parts)

    tr = min(256, T)
    late = 0.9
    xb, cos2, sin2, cosp, sina, sinb, g_in = _rowwise(
        "prep", _prep_tile, [(x, D, 0), (positions[0].astype(F32)[:, None], 1, 0)], [_rope_rows()],
        [(D, BF16)] + [(LANE, F32)] * 5, [], tr, jobs=[gather(["w_in"], late)])
    win = g_in.reshape(in_width, D)
    w1 = win[:4 * RW]
    w2 = jnp.pad(win[4 * RW:], ((0, mla_in_pad - mla_in), (0, 0)))

    first, second, near, far = (0, 2), (1, 2), (CHIP_X, CHIP_Y), (CHIP_DIAGONAL,)
    p1, g_uq, wuk, wuv, wout = _matmul("proj_ret", xb, w1, tb=True,
                                       jobs=[gather(["w_uq", "w_uk", "w_uv", "w_out"], late)])
    wuq = jnp.transpose(g_uq, (1, 0, 2)).reshape(MLA_Q_RANK, H, f_uq)
    wuq = jnp.pad(wuq, ((0, 0), (0, 0), (0, MLA_HEAD_PAD - f_uq))).reshape(MLA_Q_RANK, H * MLA_HEAD_PAD)
    wukv = jnp.concatenate([wuk, wuv], axis=1)
    p2 = _matmul("proj_mla", xb, w2, tb=True)
    c_kv, c_kr = MLA_Q_RANK, MLA_Q_RANK + MLA_KV_RANK

    def mla_prep(p, cp, sa, sb, gq, gkv):
        cqn = _rms(p[:, :c_kv], gq)
        ckvn = _rms(p[:, c_kv:c_kr], gkv)
        return cqn, ckvn, _rope64(p[:, c_kr:c_kr + LANE], cp, sa, sb)

    cqn, ckvn, krope = _rowwise(
        "mla_prep", mla_prep, [(p2, mla_in_pad, 0), (cosp, LANE, 0), (sina, LANE, 0), (sinb, LANE, 0)],
        [q_norm_g, kv_norm_g], [(MLA_Q_RANK, BF16), (MLA_KV_RANK, BF16), (LANE, BF16)], [], tr)
    qf = _matmul("mla_q", cqn, wuq)
    kv = _matmul("mla_kv", ckvn, wukv, out_dtypes=(BF16,))

    def mla_heads(q, kn, vv, kr, cp, sa, sb):
        qs, ks, vs = [], [], []
        ones = jnp.ones((q.shape[0], MLA_DV), vv.dtype)
        for h in range(H):
            o = h * MLA_HEAD_PAD
            qs += [q[:, o:o + MLA_NOPE], _rope64(q[:, o + MLA_NOPE:o + MLA_HEAD_PAD], cp, sa, sb)]
            ks += [kn[:, h * MLA_NOPE:(h + 1) * MLA_NOPE], kr]
            vs += [vv[:, h * MLA_DV:(h + 1) * MLA_DV], ones]
        return jnp.concatenate(qs, axis=1), jnp.concatenate(ks, axis=1), jnp.concatenate(vs, axis=1)

    qh, kh, vx = _rowwise(
        "mla_heads", mla_heads,
        [(qf, H * MLA_HEAD_PAD, 0), (kv, RW, 0), (kv, RW, 1), (krope, LANE, 0), (cosp, LANE, 0), (sina, LANE, 0), (sinb, LANE, 0)],
        [], [(H * MLA_HEAD_PAD, BF16)] * 3, [], tr)
    att_o, mix_half, lse, wup_a, wdown_a = _attention_fwd(
        qh, kh, vx, H, jobs=[gather(["w_up"], late, pieces=[(second, near, True), (first, far, False)]),
                             gather(["w_down"], late, pieces=[(first, near, True)])])
    ret_pre, mixin, wup_b = _retention_fwd(
        p1, cos2, sin2, ret_gn_g, mix_half, H, jobs=[gather(["w_up"], late, pieces=[(second, far, False)], base=[wup_a])])
    mix, wdown_b = _matmul("mix_out", mixin, wout,
                           jobs=[gather(["w_down"], late, pieces=[(second, near, True)], base=[wdown_a])])

    def ln1_fwd(xt, mt, g, b):
        z = alpha * xt + mt
        xhat, _ = _ln_stats(z)
        y = xhat * g + b
        return z, y, y

    z1, x1, x1b, wup = _rowwise("ln1", ln1_fwd, [(x, D, 0), (mix, D, 0)], [ln1_g, ln1_b],
                                [(D, F32), (D, F32), (D, BF16)], [], tr,
                                jobs=[gather(["w_up"], late, pieces=[(first, near, True)], base=[wup_b])])

    def relu2(acc):
        r = jnp.maximum(acc, 0.0)
        return r * r, r

    a2b, rb, wdown = _matmul("mlp_up", x1b, wup, out_dtypes=(BF16, BF16), epilogue=relu2,
                             jobs=[gather(["w_down"], 0.8, pieces=[((0, 1), far, False)], base=[wdown_b])])
    hmlp = _matmul("mlp_down", a2b, wdown)

    def ln2_loss(x1t, ht, tt, g, b):
        xhat, rstd = _ln_stats(alpha * x1t + ht)
        err = xhat * g + b - tt
        dy = err / D
        dz = _ln_bwd(dy, xhat, rstd, g)
        lrow = 0.5 * jnp.mean(jnp.square(err), axis=-1, keepdims=True)
        return dz, dz, dy * xhat, dy, jnp.broadcast_to(lrow, (lrow.shape[0], LANE))

    dz2, dz2b, dg2, db2, lpart = _rowwise("ln2_loss", ln2_loss, [(x1, D, 0), (hmlp, D, 0), (tgt, D, 0)], [ln2_g, ln2_b],
                                          [(D, F32), (D, BF16)], [D, D, LANE], tr)
    loss = lax.psum(jnp.sum(lpart[:, 0]), ("x", "y", "c"))

    def pair(grads):
        ns = list(grads)
        return _PairJob([grads[n] for n in ns], [kind[n] for n in ns], [big[n].shape for n in ns])

    my_core = lax.axis_index("c").astype(jnp.int32).reshape(1)
    my_chip = (2 * lax.axis_index("x") + lax.axis_index("y")).astype(jnp.int32).reshape(1)
    chip_sums = {}

    def pair_sums(grads, theirs):
        for n, t in zip(grads, theirs):
            chip_sums[n] = _pair_sum("pair_sum_" + n, grads[n], kind[n], big[n].shape, t, my_core)
        return {n: chip_sums[n] for n in grads}

    def chip(sums, rels=CHIP_RELS, by_source=None):
        ns = list(sums)
        return _ChipJob([sums[n] for n in ns], [big[n].shape for n in ns], rels, by_source)

    landed = {}
    da = _matmul("mlp_down_dx", dz2b, wdown, tb=True, out_dtypes=(BF16,), extras=(rb,),
                 epilogue=lambda acc, r: (acc * (2.0 * r.astype(F32)),))
    g_down = {"w_down": _matmul("mlp_down_dw", a2b, dz2b, ta=True, out_dtypes=(BF16,))}
    g_wup, *pr = _matmul("mlp_up_dw", x1b, da, ta=True, out_dtypes=(BF16,), jobs=[pair(g_down)])
    sums_down = pair_sums(g_down, pr)
    g_up = {"w_up": g_wup}
    dx1m, l_a, *pr = _matmul("mlp_up_dx", da, wup, tb=True, jobs=[chip(sums_down, CHIP_RELS[:2]), pair(g_up)])
    sums_up = pair_sums(g_up, pr)

    def ln1_bwd(dm, dzz, z, g):
        xhat, rstd = _ln_stats(z)
        dy = dm + alpha * dzz
        dz = _ln_bwd(dy, xhat, rstd, g)
        return dz, dz, dy * xhat, dy

    dz1, dz1b, dg1, db1 = _rowwise("ln1_bwd", ln1_bwd, [(dx1m, D, 0), (dz2, D, 0), (z1, D, 0)], [ln1_g],
                                   [(D, F32), (D, BF16)], [D, D], tr)
    g_wout = _matmul("mix_out_dw", mixin, dz1b, ta=True, out_dtypes=(BF16,))
    dmix = _matmul("mix_out_dx", dz1b, wout, tb=True)

    dqh, dkn, dkr, dvv, l_b, l_c = _attention_bwd(qh, kh, vx, att_o, lse, dmix, H,
                                                  jobs=[chip(sums_down, CHIP_RELS[2:]), chip(sums_up)])
    landed["w_down"], landed["w_up"] = [l_a, l_b], [l_c]
    att_scale = (MLA_NOPE + MLA_ROPE) ** -0.5

    def mla_heads_bwd(dq, dkrh, cp, sa, sb):
        parts, acc = [], dkrh[:, :LANE]
        dq = dq * att_scale
        for h in range(H):
            o = h * MLA_HEAD_PAD
            parts += [dq[:, o:o + MLA_NOPE], _unrope64(dq[:, o + MLA_NOPE:o + MLA_HEAD_PAD], cp, sa, sb)]
            if h:
                acc = acc + dkrh[:, h * LANE:(h + 1) * LANE]
        return jnp.concatenate(parts, axis=1), _unrope64(acc, cp, sa, sb)

    dqb, dkr128 = _rowwise(
        "mla_heads_bwd", mla_heads_bwd,
        [(dqh, H * MLA_HEAD_PAD, 0), (dkr, H * LANE, 0), (cosp, LANE, 0), (sina, LANE, 0), (sinb, LANE, 0)],
        [], [(H * MLA_HEAD_PAD, BF16), (LANE, F32)], [], tr)
    dcqn = _matmul("mla_q_dx", dqb, wuq, tb=True)
    g_wuq = _matmul("mla_q_dw", cqn, dqb, ta=True, out_dtypes=(BF16,))
    dkvb = jnp.concatenate([dkn, dvv], axis=1)
    dckvn = _matmul("mla_kv_dx", dkvb, wukv, tb=True)
    g_wukv = _matmul("mla_kv_dw", ckvn, dkvb, ta=True, out_dtypes=(BF16,))

    def rms_bwd(c, dy, g):
        rstd = lax.rsqrt(jnp.mean(jnp.square(c), axis=-1, keepdims=True) + EPS)
        dyg = dy * g
        dc = rstd * (dyg - c * (rstd * rstd) * jnp.mean(dyg * c, axis=-1, keepdims=True))
        return dc, dy * c * rstd

    def mla_prep_bwd(p, dq_, dkv_, dkr_, gq, gkv):
        dcq, dgq_ = rms_bwd(p[:, :c_kv], dq_, gq)
        dckv, dgkv_ = rms_bwd(p[:, c_kv:c_kr], dkv_, gkv)
        pad = jnp.zeros((p.shape[0], mla_in_pad - c_kr - LANE), F32)
        return jnp.concatenate([dcq, dckv, dkr_, pad], axis=1), dgq_, dgkv_

    d_p2, dgq, dgkv = _rowwise(
        "mla_prep_bwd", mla_prep_bwd, [(p2, mla_in_pad, 0), (dcqn, MLA_Q_RANK, 0), (dckvn, MLA_KV_RANK, 0), (dkr128, LANE, 0)],
        [q_norm_g, kv_norm_g], [(mla_in_pad, BF16)], [MLA_Q_RANK, MLA_KV_RANK], tr)

    def gate_bwd(pre, gate, dout, gn):
        dpre, dgate, dgn = [], [], []
        for h in range(H):
            sl = slice(h * RET_DV, (h + 1) * RET_DV)
            nrm, rstd = _ln_stats(pre[:, sl])
            gate_h, do_h, gn_h = gate[:, sl], dout[:, sl], gn[:, sl]
            sg = jax.nn.sigmoid(gate_h)
            silu = gate_h * sg
            dgate.append(do_h * (nrm * gn_h) * (sg * (1.0 + gate_h * (1.0 - sg))))
            dn = do_h * silu
            dgn.append(dn * nrm)
            dpre.append(_ln_bwd(dn, nrm, rstd, gn_h))
        return jnp.concatenate(dpre, axis=1), jnp.concatenate(dgate, axis=1), jnp.concatenate(dgn, axis=1)

    dpre, drg, dgn = _rowwise("ret_gate_bwd", gate_bwd, [(ret_pre, RW, 0), (p1, RW, 3), (dmix, RW, 0)], [ret_gn_g],
                              [(RW, F32), (RW, BF16)], [RW], tr)
    g_uq = g_wuq.reshape(MLA_Q_RANK, H, MLA_HEAD_PAD)[:, :, :f_uq].reshape(MLA_Q_RANK, N_DEV, H * f_uq // N_DEV)
    mid_grads = {"w_uq": jnp.transpose(g_uq, (1, 0, 2)), "w_uk": g_wukv[:, :RW], "w_uv": g_wukv[:, RW:], "w_out": g_wout}
    (drq, *pr), (drk, drv) = _retention_bwd(p1, dpre, cos2, sin2, H, dq_jobs=[pair(mid_grads)])
    sums_mid = pair_sums(mid_grads, pr)
    d_p1 = jnp.concatenate([drq, drk, drv, drg], axis=1)

    g_w1, *l_a = _matmul("proj_ret_dw", d_p1, xb, ta=True, out_dtypes=(BF16,), jobs=[chip(sums_mid)])
    for n, a_ in zip(mid_grads, l_a):
        landed[n] = [a_]
    g_w2 = _matmul("proj_mla_dw", d_p2, xb, ta=True, out_dtypes=(BF16,))

    g_win = {"w_in": jnp.concatenate([g_w1, g_w2[:mla_in]], axis=0).reshape(N_DEV, in_width // N_DEV, D)}
    small = jnp.concatenate([dgq, dgkv, dgn, dg1, db1, dg2, db2], axis=1)
    dx_a, *pr = _matmul("proj_mla_dx", d_p2, w2, extras=(dz1,), epilogue=lambda acc, d: (acc + alpha * d,), jobs=[pair(g_win)])
    sums_in = pair_sums(g_win, pr)
    grad_x, l_a, small_landed = _matmul("proj_ret_dx", d_p1, w1, extras=(dx_a,), epilogue=lambda acc, d: (acc + d,),
                                        jobs=[chip(sums_in, CHIP_RELS[:2], by_source=small)])

    moments = dict(zip(names, zip([m_w_in, m_w_uq, m_w_uk, m_w_uv, m_w_out, m_w_up, m_w_down],
                                  [v_w_in, v_w_uq, v_w_uk, v_w_uv, v_w_out, v_w_up, v_w_down])))
    res = {}

    def adamw(n, jobs=()):
        as_held = (lambda a: a[0].T) if n == "w_in" else (lambda a: a[0])
        as_given = (lambda r: r.T[None]) if n == "w_in" else (lambda r: r[None])
        out = _adamw("adamw_" + n, chip_sums[n], my_chip, landed[n], big[n], as_held(moments[n][0]), as_held(moments[n][1]),
                     jobs=jobs)
        res[n] = [as_given(r) for r in out[:4]]
        return out[4:]

    (l_b,) = adamw("w_out", jobs=[chip(sums_in, CHIP_RELS[2:])])
    landed["w_in"] = [l_a, l_b]
    for n in names:
        if n != "w_out":
            adamw(n)

    small_names = ["q_norm_g", "kv_norm_g", "ret_gn_g", "ln1_g", "ln1_b", "ln2_g", "ln2_b"]
    small_w = [q_norm_g, kv_norm_g, ret_gn_g, ln1_g, ln1_b, ln2_g, ln2_b]
    small_m = [m_q_norm_g, m_kv_norm_g, m_ret_gn_g, m_ln1_g, m_ln1_b, m_ln2_g, m_ln2_b]
    small_v = [v_q_norm_g, v_kv_norm_g, v_ret_gn_g, v_ln1_g, v_ln1_b, v_ln2_g, v_ln2_b]
    cat = lambda arrs: jnp.concatenate(arrs, axis=1)
    sres = _adamw_small(small_landed, cat(small_w), cat(small_m), cat(small_v))
    off = 0
    for n, w in zip(small_names, small_w):
        res[n] = [r[:, off:off + w.shape[1]] for r in sres]
        off += w.shape[1]

    order = ["w_in", "q_norm_g", "w_uq", "kv_norm_g", "w_uk", "w_uv", "ret_gn_g", "w_out", "ln1_g", "ln1_b",
             "w_up", "w_down", "ln2_g", "ln2_b"]
    outs = [loss, grad_x[None]]
    for k in range(4):
        outs += [res[n][k] for n in order]
    return tuple(outs)
```

```python
import functools

import numpy as np

import jax
import jax.numpy as jnp
from jax import lax
from jax.experimental import pallas as pl
from jax.experimental.pallas import tpu as pltpu

F32 = jnp.float32
BF16 = jnp.bfloat16

CHUNK = 64
RET_DK = 128
RET_DV = 128
MLA_NOPE = 128
MLA_ROPE = 64
MLA_DV = 128
MLA_Q_RANK = 768
MLA_KV_RANK = 512
MLA_HEAD_PAD = 256
ROPE_BASE = 10000.0
EPS = 1e-5
DEPTH = 1
DEEPNORM_ALPHA = (2.0 * DEPTH) ** 0.25
ADAM_LR = 0.001
ADAM_B1 = 0.9
ADAM_B2 = 0.999
ADAM_EPS = 1e-08
ADAM_WD = 0.01
ADAM_STEP = 10

N_DEV = 8
LANE = 128
SUBLANE = 8
VMEM_LIMIT = 48 * 1024 * 1024
ADAMW_TILE_ELEMS = 128 * 1024
PAIR_TILE_ELEMS = 1024 * 1024

MESH = pl.DeviceIdType.MESH
ANY = pl.BlockSpec(memory_space=pl.ANY)

NN = (((1,), (0,)), ((), ()))
NT = (((1,), (1,)), ((), ()))
TN = (((0,), (0,)), ((), ()))


def _dot(a, b, dims=NN):
    return lax.dot_general(a.astype(BF16), b.astype(BF16), dims, preferred_element_type=F32)


def _params(n_axes):
    return pltpu.CompilerParams(dimension_semantics=("arbitrary",) * n_axes, vmem_limit_bytes=VMEM_LIMIT)


def _tile(n, want):
    if n <= want:
        return n
    t = want
    while t >= LANE:
        if n % t == 0:
            return t
        t -= LANE
    return n


def _my_place():
    return lax.axis_index("x"), lax.axis_index("y"), lax.axis_index("c")


def _flat(px, py, pc):
    return 4 * px + 2 * py + pc


def _block_of(ref, kind, idx, rows, cols, sub=None):
    lo, n = (0, rows) if sub is None else sub
    if kind == "blk":
        return ref.at[idx, pl.ds(lo, n), :]
    if kind == "rows":
        return ref.at[pl.ds(pl.multiple_of(idx * rows + lo, n), n), :]
    return ref.at[pl.ds(lo, n), pl.ds(pl.multiple_of(idx * cols, cols), cols)]


CHIP_X, CHIP_Y, CHIP_DIAGONAL = 0, 1, 2
ALL_CHIPS = (CHIP_X, CHIP_Y, CHIP_DIAGONAL)


class _GatherJob:
    def __init__(self, shards, kinds, mid, pieces=(((0, 1), ALL_CHIPS, True),), base=None):
        nw = len(shards)
        self.nw, self.kinds, self.pieces = nw, list(kinds), list(pieces)
        self.shapes = [s.shape for s in shards]
        self.ins = list(shards) + (list(base) if base is not None else [])
        self.alias = [(nw + w, w) for w in range(nw)] if base is not None else []
        self.out_shape = []
        for s, kind in zip(shards, kinds):
            r, c = s.shape
            shape = {"blk": (N_DEV, r, c), "rows": (N_DEV * r, c), "cols": (r, N_DEV * c)}[kind]
            self.out_shape.append(jax.ShapeDtypeStruct(shape, s.dtype))
        n = 7 * nw * len(self.pieces)
        self.sems = [pltpu.SemaphoreType.DMA((n,)), pltpu.SemaphoreType.DMA((n,)), pltpu.SemaphoreType.DMA((n,))]
        self.phases = [(0.0, self._start), (mid, self._forward), (1.0, self._finish)]

    def _each(self, src, dst, send_sems, recv_sems, local_sems):
        x, y, c = _my_place()
        all_chips = [(1 - x, y), (x, 1 - y), (1 - x, 1 - y)]
        out = []
        for w in range(self.nw):
            r, cc = self.shapes[w]
            for p, ((k_of, of), chips, own) in enumerate(self.pieces):
                sub = (k_of * (r // of), r // of)
                s0 = 7 * (w * len(self.pieces) + p)
                mine = src[w].at[pl.ds(*sub), :]

                def place(dev, w=w, r=r, cc=cc, sub=sub):
                    return _block_of(dst[w], self.kinds[w], _flat(*dev), r, cc, sub)

                def copy(k, block, to, from_src=False, place=place, mine=mine, s0=s0):
                    return pltpu.make_async_remote_copy(
                        src_ref=mine if from_src else place(block), dst_ref=place(block),
                        send_sem=send_sems.at[s0 + k], recv_sem=recv_sems.at[s0 + k], device_id=to, device_id_type=MESH)

                local = pltpu.make_async_copy(mine, place((x, y, c)), local_sems.at[s0])
                out.append(([(j, all_chips[j]) for j in chips], own, copy, local))
        return (x, y, c), (x, y, 1 - c), out

    def _start(self, *refs):
        me, sibling, each = self._each(*refs)
        for chips, own, copy, local in each:
            if own:
                local.start()
                copy(0, me, sibling, from_src=True).start()
            for j, chip in chips:
                copy(1 + j, me, (*chip, me[2]), from_src=True).start()

    def _forward(self, *refs):
        me, sibling, each = self._each(*refs)
        for chips, own, copy, local in each:
            for j, chip in chips:
                copy(1 + j, (*chip, me[2]), me).wait_recv()
                copy(4 + j, (*chip, me[2]), sibling).start()

    def _finish(self, *refs):
        me, sibling, each = self._each(*refs)
        for chips, own, copy, local in each:
            if own:
                copy(0, sibling, me).wait_recv()
            for j, chip in chips:
                copy(4 + j, (*chip, 1 - me[2]), me).wait_recv()
        for chips, own, copy, local in each:
            if own:
                copy(0, me, sibling, from_src=True).wait_send()
                local.wait()
            for j, chip in chips:
                copy(1 + j, me, (*chip, me[2]), from_src=True).wait_send()
                copy(4 + j, (*chip, me[2]), sibling).wait_send()


REL_ALL = [(1, 0, 0), (1, 0, 1), (0, 1, 0), (0, 1, 1), (1, 1, 0), (1, 1, 1), (0, 0, 1)]
N_CHIP = 4


class _PairJob:
    def __init__(self, grads, kinds, shard_shapes):
        self.ins = list(grads)
        self.kinds, self.shard_shapes, self.nw = list(kinds), list(shard_shapes), len(grads)
        self.out_shape = [jax.ShapeDtypeStruct((N_CHIP,) + tuple(s), g.dtype) for g, s in zip(grads, shard_shapes)]
        n = N_CHIP * self.nw
        self.sems = [pltpu.SemaphoreType.DMA((n,)), pltpu.SemaphoreType.DMA((n,))]
        self.phases = [(0.0, self._start), (1.0, self._finish)]

    def _copies(self, src, theirs, send_sems, recv_sems):
        x, y, c = _my_place()
        remote = []
        for w in range(self.nw):
            r, cc = self.shard_shapes[w]
            for q in range(N_CHIP):
                s = N_CHIP * w + q
                remote.append(pltpu.make_async_remote_copy(
                    src_ref=_block_of(src[w], self.kinds[w], 2 * q + (1 - c), r, cc), dst_ref=theirs[w].at[q],
                    send_sem=send_sems.at[s], recv_sem=recv_sems.at[s], device_id=(x, y, 1 - c), device_id_type=MESH))
        return remote

    def _start(self, *refs):
        for cp in self._copies(*refs):
            cp.start()

    def _finish(self, *refs):
        for cp in self._copies(*refs):
            cp.wait()


CHIP_RELS = [(1, 0), (0, 1), (1, 1)]


class _ChipJob:
    def __init__(self, chip_sums, shard_shapes, rels=CHIP_RELS, by_source=None, piece=(0, 1), base=None):
        nw = len(chip_sums)
        self.ins = list(chip_sums) + ([by_source] if by_source is not None else []) + (list(base) if base is not None else [])
        self.nw, self.has_src, self.rels, self.piece = nw, by_source is not None, list(rels), piece
        self.rows = [s[0] for s in shard_shapes]
        self.alias = [(nw + int(self.has_src) + w, w) for w in range(nw)] if base is not None else []
        self.out_shape = [jax.ShapeDtypeStruct((len(rels),) + tuple(s), a.dtype) for a, s in zip(chip_sums, shard_shapes)]
        if self.has_src:
            self.out_shape.append(jax.ShapeDtypeStruct((N_DEV,) + by_source.shape, by_source.dtype))
        n = len(rels) * self.nw + len(REL_ALL) * int(self.has_src)
        self.sems = [pltpu.SemaphoreType.DMA((n,)), pltpu.SemaphoreType.DMA((n,)), pltpu.SemaphoreType.DMA((1,))]
        self.phases = [(0.0, self._start), (1.0, self._finish)]

    def _copies(self, src, land, send_sems, recv_sems, local_sems):
        x, y, c = _my_place()
        nw = self.nw

        def remote_copy(s, src_ref, dst_ref, peer):
            return pltpu.make_async_remote_copy(src_ref=src_ref, dst_ref=dst_ref, send_sem=send_sems.at[s],
                                                recv_sem=recv_sems.at[s], device_id=peer, device_id_type=MESH)

        def rows(w):
            n = self.rows[w] // self.piece[1]
            return pl.ds(self.piece[0] * n, n)

        local, remote = [], []
        for k, (fx, fy) in enumerate(self.rels):
            px, py = x ^ fx, y ^ fy
            remote += [remote_copy(nw * k + w, src[w].at[2 * px + py, rows(w), :], land[w].at[k, rows(w), :], (px, py, c))
                       for w in range(nw)]
        if self.has_src:
            my_idx = _flat(x, y, c)
            local.append(pltpu.make_async_copy(src[nw], land[nw].at[my_idx], local_sems.at[0]))
            remote += [remote_copy(nw * len(self.rels) + k, src[nw], land[nw].at[my_idx], (x ^ fx, y ^ fy, c ^ fc))
                       for k, (fx, fy, fc) in enumerate(REL_ALL)]
        return local, remote

    def _start(self, *refs):
        local, remote = self._copies(*refs)
        for cp in local + remote:
            cp.start()

    def _finish(self, *refs):
        local, remote = self._copies(*refs)
        for cp in remote + local:
            cp.wait()


def _row_tile(R, C, elems=ADAMW_TILE_ELEMS):
    tr = 16
    while tr * 2 * C <= elems and R % (tr * 2) == 0:
        tr *= 2
    return tr if R % tr == 0 else R


BF16_ROWS = 16


def _tile_2d(R, C, elems):
    if R % BF16_ROWS == 0:
        tr = _row_tile(R, C, elems)
        return tr, C, R // tr
    tc = C
    while tc % (2 * LANE) == 0 and R * tc > elems:
        tc //= 2
    return R, tc, C // tc


def _tile_index(tr, tc, C):
    return (lambda i: (i, 0)) if tc == C else (lambda i: (0, i))


def _part_spec(kind, tr, tc, R, C, n_lead):
    at = _tile_index(tr, tc, C)

    def dev(ids, pref):
        return 2 * ids[0] + pref[0] if n_lead == 2 else pref[0]

    def index(*args):
        ids, pref = args[:n_lead], args[n_lead]
        ri, ci = at(ids[-1])
        if kind == "blk":
            return dev(ids, pref), ri, ci
        if kind == "rows":
            return dev(ids, pref) * (R // tr) + ri, ci
        return ri, dev(ids, pref) * (C // tc) + ci

    return pl.BlockSpec((None, tr, tc) if kind == "blk" else (tr, tc), index)


def _pair_sum(name, grad, kind, shard_shape, theirs, c):
    R, C = shard_shape
    tr, tc, tiles = _tile_2d(R, C, PAIR_TILE_ELEMS)
    at = _tile_index(tr, tc, C)

    def body(c_ref, g_ref, t_ref, o_ref):
        o_ref[...] = (g_ref[...].astype(F32) + t_ref[...].astype(F32)).astype(o_ref.dtype)

    slot = pl.BlockSpec((None, tr, tc), lambda q, i, c_ref: (q, *at(i)))
    return pl.pallas_call(
        body, name=name,
        grid_spec=pltpu.PrefetchScalarGridSpec(
            num_scalar_prefetch=1, grid=(N_CHIP, tiles),
            in_specs=[_part_spec(kind, tr, tc, R, C, 2), slot], out_specs=slot),
        out_shape=jax.ShapeDtypeStruct(theirs.shape, theirs.dtype), compiler_params=_params(2))(c, grad, theirs)


class _Carried:
    def __init__(self, jobs, grid):
        self.jobs = list(jobs)
        self.steps = int(np.prod(grid))
        self.grid = tuple(grid)
        self.operands = [a for j in self.jobs for a in j.ins]
        self.out_shape = [o for j in self.jobs for o in j.out_shape]
        self.sems = [s for j in self.jobs for s in j.sems]
        self.in_specs = [ANY] * len(self.operands)
        self.out_specs = [ANY] * len(self.out_shape)

    def aliases(self, n_in, n_out):
        out, i0, o0 = {}, n_in, n_out
        for job in self.jobs:
            for ji, jo in getattr(job, "alias", []):
                out[i0 + ji] = o0 + jo
            i0, o0 = i0 + len(job.ins), o0 + len(job.out_shape)
        return out

    def _step(self):
        lin = pl.program_id(0)
        for a in range(1, len(self.grid)):
            lin = lin * self.grid[a] + pl.program_id(a)
        return lin

    def _run(self, in_refs, out_refs, sem_refs, last):
        lin = self._step()
        for job in self.jobs:
            ins, in_refs = in_refs[:len(job.ins)], in_refs[len(job.ins):]
            outs, out_refs = out_refs[:len(job.out_shape)], out_refs[len(job.out_shape):]
            sems, sem_refs = sem_refs[:len(job.sems)], sem_refs[len(job.sems):]
            for frac, fn in job.phases:
                if (frac >= 1.0) != last:
                    continue
                at = min(self.steps - 1, int(frac * (self.steps - 1)))
                pl.when(lin == at)(functools.partial(fn, ins, outs, *sems))

    def before(self, in_refs, out_refs, sem_refs):
        self._run(in_refs, out_refs, sem_refs, last=False)

    def after(self, in_refs, out_refs, sem_refs):
        self._run(in_refs, out_refs, sem_refs, last=True)


def _split_refs(refs, n_in, n_job_in, n_out, n_job_out, n_scratch):
    cuts = np.cumsum([0, n_in, n_job_in, n_out, n_job_out, n_scratch])
    return [refs[cuts[i]:cuts[i + 1]] for i in range(5)] + [refs[cuts[5]:]]


def _matmul(name, a, b, *, ta=False, tb=False, out_dtypes=(F32,), epilogue=None, extras=(), tm=1024, tn=1024, tk=2048, jobs=(),
            n=None):
    if ta:
        K, M = a.shape
    else:
        M, K = a.shape
    N = n if n is not None else (b.shape[0] if tb else b.shape[1])
    tm, tn, tk = _tile(M, tm), _tile(N, tn), _tile(K, tk)
    nk = K // tk
    grid = (M // tm, N // tn, nk)
    dims = TN if ta else (NT if tb else NN)
    n_ex, n_out = len(extras), len(out_dtypes)
    carried = _Carried(jobs, grid)

    def body(*refs):
        ins, job_ins, out_refs, job_outs, scratch, job_sems = _split_refs(
            refs, 2 + n_ex, len(carried.operands), n_out, len(carried.out_shape), int(nk > 1))
        a_ref, b_ref, ex_refs = ins[0], ins[1], ins[2:]
        k = pl.program_id(2)
        carried.before(job_ins, job_outs, job_sems)

        def finish(acc):
            res = (acc,) if epilogue is None else epilogue(acc, *[e[...] for e in ex_refs])
            for o_ref, r in zip(out_refs, res):
                o_ref[...] = r.astype(o_ref.dtype)

        if nk == 1:
            finish(_dot(a_ref[...], b_ref[...], dims))
        else:
            acc_ref = scratch[0]

            @pl.when(k == 0)
            def _():
                acc_ref[...] = _dot(a_ref[...], b_ref[...], dims)

            @pl.when(k > 0)
            def _():
                acc_ref[...] += _dot(a_ref[...], b_ref[...], dims)

            pl.when(k == nk - 1)(lambda: finish(acc_ref[...]))

        carried.after(job_ins, job_outs, job_sems)

    a_spec = pl.BlockSpec((tk, tm), lambda i, j, k: (k, i)) if ta else pl.BlockSpec((tm, tk), lambda i, j, k: (i, k))
    b_spec = pl.BlockSpec((tn, tk), lambda i, j, k: (j, k)) if tb else pl.BlockSpec((tk, tn), lambda i, j, k: (k, j))
    tile_spec = pl.BlockSpec((tm, tn), lambda i, j, k: (i, j))
    outs = pl.pallas_call(
        body,
        name=name,
        grid=grid,
        in_specs=[a_spec, b_spec] + [tile_spec] * n_ex + carried.in_specs,
        out_specs=[tile_spec] * n_out + carried.out_specs,
        out_shape=[jax.ShapeDtypeStruct((M, N), d) for d in out_dtypes] + carried.out_shape,
        scratch_shapes=[pltpu.VMEM((tm, tn), F32)] * int(nk > 1) + carried.sems,
        input_output_aliases=carried.aliases(2 + n_ex, n_out),
        compiler_params=_params(3),
    )(a, b, *extras, *carried.operands)
    return outs[0] if len(outs) == 1 else outs


def _rowwise(name, fn, row_in, const_in, row_out, acc_out, tm, jobs=()):
    T = row_in[0][0].shape[0]
    n_r, n_c, n_o, n_a = len(row_in), len(const_in), len(row_out), len(acc_out)
    carried = _Carried(jobs, (T // tm,))

    def body(*refs):
        in_refs, job_ins, own_outs, job_outs, _, job_sems = _split_refs(
            refs, n_r + n_c, len(carried.operands), n_o + n_a, len(carried.out_shape), 0)
        out_refs, acc_refs = own_outs[:n_o], own_outs[n_o:]
        carried.before(job_ins, job_outs, job_sems)
        res = fn(*[r[...] for r in in_refs])
        for o_ref, r in zip(out_refs, res[:n_o]):
            o_ref[...] = r.astype(o_ref.dtype)

        @pl.when(pl.program_id(0) == 0)
        def _():
            for a_ref in acc_refs:
                a_ref[...] = jnp.zeros_like(a_ref)

        for a_ref, r in zip(acc_refs, res[n_o:]):
            a_ref[...] += jnp.sum(r.reshape(tm // SUBLANE, SUBLANE, r.shape[-1]), axis=0)
        carried.after(job_ins, job_outs, job_sems)

    in_specs = [pl.BlockSpec((tm, w), functools.partial(lambda i, c: (i, c), c=cb)) for (_, w, cb) in row_in]
    in_specs += [pl.BlockSpec(c.shape, functools.partial(lambda i, nd: (0,) * nd, nd=c.ndim)) for c in const_in]
    out_specs = [pl.BlockSpec((tm, c), lambda i: (i, 0)) for (c, _) in row_out]
    out_specs += [pl.BlockSpec((SUBLANE, c), lambda i: (0, 0)) for c in acc_out]
    out_shape = [jax.ShapeDtypeStruct((T, c), d) for (c, d) in row_out]
    out_shape += [jax.ShapeDtypeStruct((SUBLANE, c), F32) for c in acc_out]
    return pl.pallas_call(
        body, name=name, grid=(T // tm,), in_specs=in_specs + carried.in_specs, out_specs=out_specs + carried.out_specs,
        out_shape=out_shape + carried.out_shape, scratch_shapes=carried.sems,
        input_output_aliases=carried.aliases(n_r + n_c, n_o + n_a), compiler_params=_params(1),
    )(*[r[0] for r in row_in], *const_in, *carried.operands)


def _ln_stats(z):
    mu = jnp.mean(z, axis=-1, keepdims=True)
    var = jnp.mean(jnp.square(z - mu), axis=-1, keepdims=True)
    rstd = lax.rsqrt(var + EPS)
    return (z - mu) * rstd, rstd


def _ln_bwd(dy, xhat, rstd, g):
    dxh = dy * g
    return rstd * (dxh - jnp.mean(dxh, axis=-1, keepdims=True) - xhat * jnp.mean(dxh * xhat, axis=-1, keepdims=True))


def _rms(x, g):
    return x * lax.rsqrt(jnp.mean(jnp.square(x), axis=-1, keepdims=True) + EPS) * g


def _rope128(x, cos2, sin2):
    return x * cos2 + pltpu.roll(x, RET_DK // 2, 1) * sin2


def _unrope128(dy, cos2, sin2):
    return dy * cos2 + pltpu.roll(dy * sin2, RET_DK // 2, 1)


def _rope64(x, cosp, sina, sinb):
    h = MLA_ROPE // 2
    return x * cosp + pltpu.roll(x, h, 1) * sina + pltpu.roll(x, LANE - h, 1) * sinb


def _unrope64(dy, cosp, sina, sinb):
    h = MLA_ROPE // 2
    return dy * cosp + pltpu.roll(dy * sina, LANE - h, 1) + pltpu.roll(dy * sinb, h, 1)


RET_BLOCK = 256
RET_HEADS_PER_STEP = 4


def _ret_tables(H, B):
    log_g = np.log1p(-np.exp2(-5.0 - np.arange(H, dtype=np.float32))).astype(np.float32)
    idx = np.arange(B, dtype=np.float32)
    dist = np.abs(idx[:, None] - idx[None, :])
    visible = (np.arange(B)[None, :] // CHUNK) <= (np.arange(B)[:, None] // CHUNK)
    dmat = np.exp(log_g[:, None, None] * dist).astype(np.float32) * visible[None].astype(np.float32)
    qd = np.exp(log_g[:, None] * (idx + 1.0)).astype(np.float32)
    kd = np.exp(log_g[:, None] * (B - 1 - idx)).astype(np.float32)
    sd = np.exp(log_g * B).astype(np.float32)
    bc = lambda v: np.ascontiguousarray(np.broadcast_to(v[:, :, None], (H, v.shape[1], LANE)))
    sdb = np.ascontiguousarray(np.broadcast_to(sd[:, None, None], (H, SUBLANE, LANE)))
    return jnp.asarray(dmat), jnp.asarray(bc(qd)), jnp.asarray(bc(kd)), jnp.asarray(sdb)


def _ret_specs(H, B, reverse, nb):
    blk = (lambda b: nb - 1 - b) if reverse else (lambda b: b)
    G = min(RET_HEADS_PER_STEP, H)
    W = G * LANE
    col = lambda off: pl.BlockSpec((B, W), functools.partial(lambda h, b, off: (blk(b), off // G + h), off=off))
    row = pl.BlockSpec((B, LANE), lambda h, b: (blk(b), 0))
    per_head = lambda r: pl.BlockSpec((G, r, LANE), lambda h, b: (h, 0, 0))
    dmat = pl.BlockSpec((G, B, B), lambda h, b: (h, 0, 0))
    return G, col, row, per_head, dmat


def _ret_call(name, body, n_in, n_out, operands, in_specs, out_specs, out_shape, grid, G, jobs, own_alias=None):
    carried = _Carried(jobs, grid)

    def wrapped(*refs):
        ins, job_ins, outs, job_outs, (state,), job_sems = _split_refs(
            refs, n_in, len(carried.operands), n_out, len(carried.out_shape), 1)
        carried.before(job_ins, job_outs, job_sems)

        @pl.when(pl.program_id(1) == 0)
        def _():
            state[...] = jnp.zeros_like(state)

        body(*ins, *outs, state)
        carried.after(job_ins, job_outs, job_sems)

    return pl.pallas_call(
        wrapped, name=name, grid=grid,
        in_specs=in_specs + carried.in_specs, out_specs=out_specs + carried.out_specs,
        out_shape=out_shape + carried.out_shape,
        scratch_shapes=[pltpu.VMEM((G, RET_DK, RET_DV), F32)] + carried.sems,
        input_output_aliases={**(own_alias or {}), **carried.aliases(n_in, n_out)},
        compiler_params=_params(2),
    )(*operands, *carried.operands)


def _retention_fwd(p1, cos2, sin2, gn_g, mix_in, H, jobs=()):
    T = p1.shape[0]
    B = min(RET_BLOCK, T)
    nb = T // B
    scale = RET_DK ** -0.5
    dmat, qd, kd, sd = _ret_tables(H, B)
    G, col, row, per_head, dspec = _ret_specs(H, B, False, nb)

    def body(q_ref, k_ref, v_ref, g_ref, cos_ref, sin_ref, d_ref, qd_ref, kd_ref, sd_ref, gn_ref, _, pre_ref, out_ref, s_ref):
        cos, sin = cos_ref[...], sin_ref[...]
        for g in range(G):
            hl = pl.ds(g * LANE, LANE)
            q = _rope128(q_ref[:, hl], cos, sin)
            k = _rope128(k_ref[:, hl], cos, sin) * scale
            v = v_ref[:, hl]
            s = s_ref[g]
            a = _dot(q, k, NT) * d_ref[g]
            ret = _dot(a, v) + _dot(q * qd_ref[g], s)
            s_ref[g] = sd_ref[g, 0:1, :] * s + _dot(k * kd_ref[g], v, TN)
            pre_ref[:, hl] = ret
            nrm, _ = _ln_stats(ret)
            out_ref[:, hl] = (jax.nn.silu(g_ref[:, hl]) * (nrm * gn_ref[:, hl])).astype(out_ref.dtype)

    out_spec = pl.BlockSpec((B, G * LANE), lambda h, b: (b, h))
    return _ret_call(
        "retention_fwd", body, 12, 2, (p1, p1, p1, p1, cos2, sin2, dmat, qd, kd, sd, gn_g, mix_in),
        [col(0), col(H), col(2 * H), col(3 * H), row, row, dspec, per_head(B), per_head(B), per_head(SUBLANE),
         pl.BlockSpec((1, G * LANE), lambda h, b: (0, h)), ANY],
        [out_spec, out_spec],
        [jax.ShapeDtypeStruct((T, H * RET_DV), F32), jax.ShapeDtypeStruct(mix_in.shape, mix_in.dtype)],
        (H // G, nb), G, jobs, own_alias={11: 1})


def _retention_bwd(p1, dpre, cos2, sin2, H, dq_jobs=(), dkv_jobs=()):
    T = p1.shape[0]
    B = min(RET_BLOCK, T)
    nb = T // B
    scale = RET_DK ** -0.5
    dmat, qd, kd, sd = _ret_tables(H, B)

    def dq_body(k_ref, v_ref, do_ref, cos_ref, sin_ref, d_ref, qd_ref, kd_ref, sd_ref, dq_ref, s_ref):
        cos, sin = cos_ref[...], sin_ref[...]
        for g in range(G):
            hl = pl.ds(g * LANE, LANE)
            k = _rope128(k_ref[:, hl], cos, sin) * scale
            v, do, s = v_ref[:, hl], do_ref[:, hl], s_ref[g]
            da = _dot(do, v, NT) * d_ref[g]
            dq = _dot(da, k) + _dot(do, s, NT) * qd_ref[g]
            dq_ref[:, hl] = _unrope128(dq, cos, sin).astype(dq_ref.dtype)
            s_ref[g] = sd_ref[g, 0:1, :] * s + _dot(k * kd_ref[g], v, TN)

    G, col, row, per_head, dspec = _ret_specs(H, B, False, nb)
    dq = _ret_call(
        "retention_bwd_dq", dq_body, 9, 1, (p1, p1, dpre, cos2, sin2, dmat, qd, kd, sd),
        [col(H), col(2 * H), col(0), row, row, dspec, per_head(B), per_head(B), per_head(SUBLANE)],
        [pl.BlockSpec((B, G * LANE), lambda h, b: (b, h))], [jax.ShapeDtypeStruct((T, H * RET_DK), BF16)],
        (H // G, nb), G, dq_jobs)

    def dkv_body(q_ref, k_ref, v_ref, do_ref, cos_ref, sin_ref, d_ref, qd_ref, kd_ref, sd_ref, dk_ref, dv_ref, g_ref):
        cos, sin = cos_ref[...], sin_ref[...]
        for g in range(G):
            hl = pl.ds(g * LANE, LANE)
            q = _rope128(q_ref[:, hl], cos, sin)
            k = _rope128(k_ref[:, hl], cos, sin) * scale
            v, do, st = v_ref[:, hl], do_ref[:, hl], g_ref[g]
            a = _dot(q, k, NT) * d_ref[g]
            da = _dot(do, v, NT) * d_ref[g]
            dv = _dot(a, do, TN) + _dot(k * kd_ref[g], st)
            dk = _dot(da, q, TN) + _dot(v, st, NT) * kd_ref[g]
            dv_ref[:, hl] = dv.astype(dv_ref.dtype)
            dk_ref[:, hl] = _unrope128(dk * scale, cos, sin).astype(dk_ref.dtype)
            g_ref[g] = sd_ref[g, 0:1, :] * st + _dot(q * qd_ref[g], do, TN)

    G, col, row, per_head, dspec = _ret_specs(H, B, True, nb)
    out_spec = pl.BlockSpec((B, G * LANE), lambda h, b: (nb - 1 - b, h))
    dkv = _ret_call(
        "retention_bwd_dkv", dkv_body, 10, 2, (p1, p1, p1, dpre, cos2, sin2, dmat, qd, kd, sd),
        [col(0), col(H), col(2 * H), col(0), row, row, dspec, per_head(B), per_head(B), per_head(SUBLANE)],
        [out_spec, out_spec],
        [jax.ShapeDtypeStruct((T, H * RET_DK), BF16), jax.ShapeDtypeStruct((T, H * RET_DV), BF16)],
        (H // G, nb), G, dkv_jobs)
    return dq, dkv


ATT_BLOCK = 512


ATT_ROW_CHUNKS = 2
ATT_HEADS_PER_STEP = 8
ATT_BWD_HEADS_PER_STEP = 2
LOG2E = 1.4426950408889634


def _chunk_mask(rows, row0, cols):
    qi = (lax.broadcasted_iota(jnp.int32, (rows, cols), 0) + row0) // CHUNK
    kj = lax.broadcasted_iota(jnp.int32, (rows, cols), 1) // CHUNK
    return kj <= qi


def _attention_fwd(qh, kh, vx, H, jobs=()):
    T = qh.shape[0]
    t = min(ATT_BLOCK, T)
    n = T // t
    r = t // ATT_ROW_CHUNKS
    G = min(ATT_HEADS_PER_STEP, H)
    P = MLA_HEAD_PAD
    scale = (MLA_NOPE + MLA_ROPE) ** -0.5
    c2 = scale * LOG2E
    carried = _Carried(jobs, (H // G, n, n))

    def body(*refs):
        (q_ref, k_ref, v_ref), job_ins, (o_ref, ob_ref, lse_ref), job_outs, (m_ref, acc_ref), job_sems = _split_refs(
            refs, 3, len(carried.operands), 3, len(carried.out_shape), 2)
        i, j = pl.program_id(1), pl.program_id(2)
        carried.before(job_ins, job_outs, job_sems)

        @pl.when(j == 0)
        def _():
            m_ref[...] = jnp.full_like(m_ref, -jnp.inf)
            acc_ref[...] = jnp.zeros_like(acc_ref)

        def update(masked):
            for g in range(G):
                hc = pl.ds(g * P, P)
                for ch in range(ATT_ROW_CHUNKS):
                    rs = pl.ds(ch * r, r)
                    kc = (ch + 1) * r if masked else t
                    ks = pl.ds(0, kc)
                    s = _dot(q_ref[rs, hc], k_ref[ks, hc], NT)
                    if masked:
                        s = jnp.where(_chunk_mask(r, ch * r, kc), s, -jnp.inf)
                    m_prev = m_ref[rs, pl.ds(g * LANE, LANE)]
                    m_new = jnp.maximum(m_prev, jnp.max(s, axis=-1, keepdims=True))
                    alpha = jnp.exp2((m_prev - m_new) * c2)
                    p = jnp.exp2((s - jnp.tile(m_new, (1, kc // LANE))) * c2)
                    acc_ref[rs, hc] = jnp.tile(alpha, (1, 2)) * acc_ref[rs, hc] + _dot(p, v_ref[ks, hc])
                    m_ref[rs, pl.ds(g * LANE, LANE)] = m_new

        pl.when(j < i)(functools.partial(update, False))
        pl.when(j == i)(functools.partial(update, True))

        @pl.when(j == n - 1)
        def _():
            for g in range(G):
                acc = acc_ref[:, pl.ds(g * P, P)]
                l = acc[:, MLA_DV:]
                o = acc[:, :MLA_DV] / l
                o_ref[:, pl.ds(g * MLA_DV, MLA_DV)] = o
                ob_ref[:, pl.ds(g * MLA_DV, MLA_DV)] = o.astype(ob_ref.dtype)
                lse_ref[g] = m_ref[:, pl.ds(g * LANE, LANE)] * scale + jnp.log(l)

        carried.after(job_ins, job_outs, job_sems)

    kclamp = lambda i, j: jnp.minimum(j, i)
    o_spec = pl.BlockSpec((t, G * MLA_DV), lambda h, i, j: (i, h))
    mix_spec = pl.BlockSpec((t, G * MLA_DV), lambda h, i, j: (i, H // G + h))
    return pl.pallas_call(
        body, name="attention_fwd", grid=(H // G, n, n),
        in_specs=[pl.BlockSpec((t, G * P), lambda h, i, j: (i, h)),
                  pl.BlockSpec((t, G * P), lambda h, i, j: (kclamp(i, j), h)),
                  pl.BlockSpec((t, G * P), lambda h, i, j: (kclamp(i, j), h))] + carried.in_specs,
        out_specs=[o_spec, mix_spec, pl.BlockSpec((G, t, LANE), lambda h, i, j: (h, i, 0))] + carried.out_specs,
        out_shape=[jax.ShapeDtypeStruct((T, H * MLA_DV), F32), jax.ShapeDtypeStruct((T, 2 * H * MLA_DV), BF16),
                   jax.ShapeDtypeStruct((H, T, LANE), F32)] + carried.out_shape,
        scratch_shapes=[pltpu.VMEM((t, G * LANE), F32), pltpu.VMEM((t, G * P), F32)] + carried.sems,
        input_output_aliases=carried.aliases(3, 3),
        compiler_params=_params(3),
    )(qh, kh, vx, *carried.operands)


def _attention_bwd(qh, kh, vx, o, lse, dmix, H, jobs=()):
    T = qh.shape[0]
    t = min(ATT_BLOCK, T)
    n = T // t
    r = t // ATT_ROW_CHUNKS
    G = min(ATT_BWD_HEADS_PER_STEP, H)
    P = MLA_HEAD_PAD
    scale = (MLA_NOPE + MLA_ROPE) ** -0.5
    c2 = scale * LOG2E
    carried = _Carried(jobs, (H // G, n, n))

    def body(*refs):
        ((q_ref, k_ref, v_ref, do_ref, o_ref, lse_ref), job_ins, (dq_ref, dkn_ref, dkr_ref, dv_ref), job_outs,
         (dk_acc, dv_acc), job_sems) = _split_refs(refs, 6, len(carried.operands), 4, len(carried.out_shape), 2)
        j, i = pl.program_id(1), pl.program_id(2)
        carried.before(job_ins, job_outs, job_sems)

        @pl.when(jnp.logical_and(j == 0, i == 0))
        def _():
            dq_ref[...] = jnp.zeros_like(dq_ref)

        @pl.when(i == j)
        def _():
            dk_acc[...] = jnp.zeros_like(dk_acc)
            dv_acc[...] = jnp.zeros_like(dv_acc)

        def update(masked):
            for g in range(G):
                hc, hv = pl.ds(g * P, P), pl.ds(g * MLA_DV, MLA_DV)
                for ch in range(ATT_ROW_CHUNKS):
                    rs = pl.ds(ch * r, r)
                    kc = (ch + 1) * r if masked else t
                    ks = pl.ds(0, kc)
                    k, v = k_ref[ks, hc], v_ref[ks, pl.ds(g * P, MLA_DV)]
                    q, do = q_ref[rs, hc], do_ref[rs, hv]
                    s = _dot(q, k, NT)
                    if masked:
                        s = jnp.where(_chunk_mask(r, ch * r, kc), s, -jnp.inf)
                    p = jnp.exp2(s * c2 - jnp.tile(lse_ref[g, rs, :] * LOG2E, (1, kc // LANE)))
                    dv_acc[ks, hv] += _dot(p, do, TN)
                    dp = _dot(do, v, NT)
                    delta = jnp.sum(do * o_ref[rs, hv], axis=-1, keepdims=True)
                    ds = p * (dp - delta)
                    rows = pl.ds(pl.multiple_of(i * t, t) + ch * r, r)
                    dq_ref[rows, hc] += _dot(ds, k)
                    dk_acc[ks, hc] += _dot(ds, q, TN)

        pl.when(i > j)(functools.partial(update, False))
        pl.when(i == j)(functools.partial(update, True))

        @pl.when(i == n - 1)
        def _():
            for g in range(G):
                dk = dk_acc[:, pl.ds(g * P, P)] * scale
                dkn_ref[:, pl.ds(g * LANE, LANE)] = dk[:, :MLA_NOPE].astype(dkn_ref.dtype)
                dkr_ref[:, pl.ds(g * LANE, LANE)] = dk[:, MLA_NOPE:]
            dv_ref[...] = dv_acc[...].astype(dv_ref.dtype)

        carried.after(job_ins, job_outs, job_sems)

    qclamp = lambda j, i: jnp.maximum(i, j)
    kv_out = pl.BlockSpec((t, G * LANE), lambda h, j, i: (j, h))
    q_in = pl.BlockSpec((t, G * MLA_DV), lambda h, j, i: (qclamp(j, i), h))
    return pl.pallas_call(
        body, name="attention_bwd", grid=(H // G, n, n),
        in_specs=[pl.BlockSpec((t, G * P), lambda h, j, i: (qclamp(j, i), h)),
                  pl.BlockSpec((t, G * P), lambda h, j, i: (j, h)),
                  pl.BlockSpec((t, G * P), lambda h, j, i: (j, h)),
                  pl.BlockSpec((t, G * MLA_DV), lambda h, j, i: (qclamp(j, i), H // G + h)),
                  q_in,
                  pl.BlockSpec((G, t, LANE), lambda h, j, i: (h, qclamp(j, i), 0))] + carried.in_specs,
        out_specs=[pl.BlockSpec((T, G * P), lambda h, j, i: (0, h)), kv_out, kv_out, kv_out] + carried.out_specs,
        out_shape=[jax.ShapeDtypeStruct((T, H * P), F32), jax.ShapeDtypeStruct((T, H * LANE), BF16),
                   jax.ShapeDtypeStruct((T, H * LANE), F32), jax.ShapeDtypeStruct((T, H * LANE), BF16)] + carried.out_shape,
        scratch_shapes=[pltpu.VMEM((t, G * P), F32), pltpu.VMEM((t, G * MLA_DV), F32)] + carried.sems,
        input_output_aliases=carried.aliases(6, 4),
        compiler_params=_params(3),
    )(qh, kh, vx, dmix, o, lse, *carried.operands)


def _adamw_math(w, g, m, v):
    m = ADAM_B1 * m + (1.0 - ADAM_B1) * g
    v = ADAM_B2 * v + (1.0 - ADAM_B2) * jnp.square(g)
    m_hat = m / (1.0 - ADAM_B1 ** ADAM_STEP)
    v_hat = v / (1.0 - ADAM_B2 ** ADAM_STEP)
    delta = -ADAM_LR * (m_hat / (jnp.sqrt(v_hat) + ADAM_EPS) + ADAM_WD * w)
    return delta, m, v


def _adamw(name, chip_sums, my_chip, landed, w, m, v, jobs=()):
    R, C = w.shape
    tr, tc, tiles = _tile_2d(R, C, ADAMW_TILE_ELEMS)
    at = _tile_index(tr, tc, C)
    nl = len(landed)
    assert sum(l.shape[0] for l in landed) == N_CHIP - 1
    carried = _Carried(jobs, (tiles,))

    def body(q_ref, *refs):
        (own_ref, *rest), job_ins, (g_out, d_out, m_out, v_out), job_outs, _, job_sems = _split_refs(
            refs, nl + 4, len(carried.operands), 4, len(carried.out_shape), 0)
        l_refs, (w_ref, m_ref, v_ref) = rest[:nl], rest[nl:]
        carried.before(job_ins, job_outs, job_sems)
        g = own_ref[...].astype(F32)
        for l_ref in l_refs:
            for s in range(l_ref.shape[0]):
                g = g + l_ref[s].astype(F32)
        delta, mn, vn = _adamw_math(w_ref[...], g, m_ref[...], v_ref[...])
        g_out[...] = g
        d_out[...] = delta
        m_out[...] = mn
        v_out[...] = vn
        carried.after(job_ins, job_outs, job_sems)

    spec = pl.BlockSpec((tr, tc), lambda i, q_ref: at(i))
    own_spec = pl.BlockSpec((None, tr, tc), lambda i, q_ref: (q_ref[0], *at(i)))
    l_specs = [pl.BlockSpec((l.shape[0], tr, tc), lambda i, q_ref: (0, *at(i))) for l in landed]
    return pl.pallas_call(
        body, name=name,
        grid_spec=pltpu.PrefetchScalarGridSpec(
            num_scalar_prefetch=1, grid=(tiles,),
            in_specs=[own_spec] + l_specs + [spec, spec, spec] + carried.in_specs,
            out_specs=[spec] * 4 + carried.out_specs, scratch_shapes=carried.sems),
        out_shape=[jax.ShapeDtypeStruct((R, C), F32)] * 4 + carried.out_shape,
        input_output_aliases={1 + i: o for i, o in carried.aliases(nl + 4, 4).items()},
        compiler_params=_params(1),
    )(my_chip, chip_sums, *landed, w, m, v, *carried.operands)


def _adamw_small(landed, w, m, v):
    P = w.shape[1]

    def body(l_ref, w_ref, m_ref, v_ref, g_out, d_out, m_out, v_out):
        acc = l_ref[0]
        for s in range(1, N_DEV):
            acc = acc + l_ref[s]
        g = jnp.sum(acc, axis=0, keepdims=True)
        delta, mn, vn = _adamw_math(w_ref[...], g, m_ref[...], v_ref[...])
        g_out[...] = g
        d_out[...] = delta
        m_out[...] = mn
        v_out[...] = vn

    return pl.pallas_call(
        body, name="adamw_replicated", out_shape=[jax.ShapeDtypeStruct((1, P), F32)] * 4,
    )(landed, w, m, v)


def _rope_rows():
    inv_r = ROPE_BASE ** (-jnp.arange(0, RET_DK, 2, dtype=F32) / RET_DK)
    inv_m = ROPE_BASE ** (-jnp.arange(0, MLA_ROPE, 2, dtype=F32) / MLA_ROPE)
    h = MLA_ROPE // 2
    one, zero = jnp.ones((h,), F32), jnp.zeros((h,), F32)
    pad = jnp.zeros((LANE - MLA_ROPE,), F32)
    rows = [jnp.concatenate([inv_r, inv_r]), jnp.concatenate([-jnp.ones((RET_DK // 2,), F32), jnp.ones((RET_DK // 2,), F32)]),
            jnp.concatenate([inv_m, inv_m, pad]), jnp.concatenate([one, one, pad]),
            jnp.concatenate([zero, one, pad]), jnp.concatenate([-one, zero, pad])]
    return jnp.stack(rows + [jnp.zeros((LANE,), F32)] * (SUBLANE - len(rows)))


def _prep_tile(xt, p, rows):
    ang2, angp = p * rows[0:1], p * rows[2:3]
    sp = jnp.sin(angp)
    return (xt, jnp.cos(ang2), jnp.sin(ang2) * rows[1:2], jnp.cos(angp) * rows[3:4], sp * rows[4:5], sp * rows[5:6])


def kernel(x, positions, w_in, q_norm_g, w_uq, kv_norm_g, w_uk, w_uv, ret_gn_g, w_out, ln1_g, ln1_b, w_up, w_down, ln2_g, ln2_b, loss_target, m_w_in, m_q_norm_g, m_w_uq, m_kv_norm_g, m_w_uk, m_w_uv, m_ret_gn_g, m_w_out, m_ln1_g, m_ln1_b, m_w_up, m_w_down, m_ln2_g, m_ln2_b, v_w_in, v_q_norm_g, v_w_uq, v_kv_norm_g, v_w_uk, v_w_uv, v_ret_gn_g, v_w_out, v_ln1_g, v_ln1_b, v_w_up, v_w_down, v_ln2_g, v_ln2_b):
    T, D = x.shape[1], x.shape[2]
    H = D // 256
    RW = H * RET_DV
    x = x[0]
    tgt = loss_target[0]
    alpha = DEEPNORM_ALPHA
    in_width = w_in.shape[2] * N_DEV
    mla_in = in_width - 4 * RW
    mla_in_pad = -(-mla_in // 512) * 512
    f_uq = MLA_NOPE + MLA_ROPE

    names = ["w_in", "w_uq", "w_uk", "w_uv", "w_out", "w_up", "w_down"]
    big = dict(zip(names, [w_in[0].T, w_uq[0], w_uk[0], w_uv[0], w_out[0], w_up[0], w_down[0]]))
    kind = dict(zip(names, ["blk", "blk", "cols", "cols", "rows", "cols", "rows"]))
    half = {n: w.astype(BF16) for n, w in big.items()}

    def gather(ns, mid, **parts):
        return _GatherJob([half[n] for n in ns], [kind[n] for n in ns], mid, **parts)

    tr = min(256, T)
    late = 0.9
    xb, cos2, sin2, cosp, sina, sinb, g_in = _rowwise(
        "prep", _prep_tile, [(x, D, 0), (positions[0].astype(F32)[:, None], 1, 0)], [_rope_rows()],
        [(D, BF16)] + [(LANE, F32)] * 5, [], tr, jobs=[gather(["w_in"], late)])
    win = g_in.reshape(in_width, D)
    w1, n1 = win, 4 * RW
    w2 = jnp.pad(win[4 * RW:], ((0, mla_in_pad - mla_in), (0, 0)))

    first, second, near, far = (0, 2), (1, 2), (CHIP_X, CHIP_Y), (CHIP_DIAGONAL,)
    p1, g_uq, wuk, wuv, wout = _matmul("proj_ret", xb, w1, tb=True, n=n1,
                                       jobs=[gather(["w_uq", "w_uk", "w_uv", "w_out"], late)])
    wuq = jnp.transpose(g_uq, (1, 0, 2)).reshape(MLA_Q_RANK, H, f_uq)
    wuq = jnp.pad(wuq, ((0, 0), (0, 0), (0, MLA_HEAD_PAD - f_uq))).reshape(MLA_Q_RANK, H * MLA_HEAD_PAD)
    wukv = jnp.concatenate([wuk, wuv], axis=1)
    p2 = _matmul("proj_mla", xb, w2, tb=True)
    c_kv, c_kr = MLA_Q_RANK, MLA_Q_RANK + MLA_KV_RANK

    def mla_prep(p, cp, sa, sb, gq, gkv):
        cqn = _rms(p[:, :c_kv], gq)
        ckvn = _rms(p[:, c_kv:c_kr], gkv)
        return cqn, ckvn, _rope64(p[:, c_kr:c_kr + LANE], cp, sa, sb)

    cqn, ckvn, krope = _rowwise(
        "mla_prep", mla_prep, [(p2, mla_in_pad, 0), (cosp, LANE, 0), (sina, LANE, 0), (sinb, LANE, 0)],
        [q_norm_g, kv_norm_g], [(MLA_Q_RANK, BF16), (MLA_KV_RANK, BF16), (LANE, BF16)], [], tr)
    qf = _matmul("mla_q", cqn, wuq)
    kv = _matmul("mla_kv", ckvn, wukv, out_dtypes=(BF16,))

    def mla_heads(q, kn, vv, kr, cp, sa, sb):
        qs, ks, vs = [], [], []
        ones = jnp.ones((q.shape[0], MLA_DV), vv.dtype)
        for h in range(H):
            o = h * MLA_HEAD_PAD
            qs += [q[:, o:o + MLA_NOPE], _rope64(q[:, o + MLA_NOPE:o + MLA_HEAD_PAD], cp, sa, sb)]
            ks += [kn[:, h * MLA_NOPE:(h + 1) * MLA_NOPE], kr]
            vs += [vv[:, h * MLA_DV:(h + 1) * MLA_DV], ones]
        return jnp.concatenate(qs, axis=1), jnp.concatenate(ks, axis=1), jnp.concatenate(vs, axis=1)

    qh, kh, vx = _rowwise(
        "mla_heads", mla_heads,
        [(qf, H * MLA_HEAD_PAD, 0), (kv, RW, 0), (kv, RW, 1), (krope, LANE, 0), (cosp, LANE, 0), (sina, LANE, 0), (sinb, LANE, 0)],
        [], [(H * MLA_HEAD_PAD, BF16)] * 3, [], tr)
    att_o, mix_half, lse, wup_a, wdown_a = _attention_fwd(
        qh, kh, vx, H, jobs=[gather(["w_up"], late, pieces=[(second, near, True), (first, far, False)]),
                             gather(["w_down"], late, pieces=[(first, near, True)])])
    ret_pre, mixin, wup_b = _retention_fwd(
        p1, cos2, sin2, ret_gn_g, mix_half, H, jobs=[gather(["w_up"], late, pieces=[(second, far, False)], base=[wup_a])])
    mix, wdown_b = _matmul("mix_out", mixin, wout,
                           jobs=[gather(["w_down"], late, pieces=[(second, near, True)], base=[wdown_a])])

    def ln1_fwd(xt, mt, g, b):
        z = alpha * xt + mt
        xhat, _ = _ln_stats(z)
        y = xhat * g + b
        return z, y, y

    z1, x1, x1b, wup = _rowwise("ln1", ln1_fwd, [(x, D, 0), (mix, D, 0)], [ln1_g, ln1_b],
                                [(D, F32), (D, F32), (D, BF16)], [], tr,
                                jobs=[gather(["w_up"], late, pieces=[(first, near, True)], base=[wup_b])])

    def relu2(acc):
        r = jnp.maximum(acc, 0.0)
        return r * r, r

    a2b, rb, wdown = _matmul("mlp_up", x1b, wup, out_dtypes=(BF16, BF16), epilogue=relu2,
                             jobs=[gather(["w_down"], 0.8, pieces=[((0, 1), far, False)], base=[wdown_b])])
    hmlp = _matmul("mlp_down", a2b, wdown)

    def ln2_loss(x1t, ht, tt, g, b):
        xhat, rstd = _ln_stats(alpha * x1t + ht)
        err = xhat * g + b - tt
        dy = err / D
        dz = _ln_bwd(dy, xhat, rstd, g)
        lrow = 0.5 * jnp.mean(jnp.square(err), axis=-1, keepdims=True)
        return dz, dz, dy * xhat, dy, jnp.broadcast_to(lrow, (lrow.shape[0], LANE))

    dz2, dz2b, dg2, db2, lpart = _rowwise("ln2_loss", ln2_loss, [(x1, D, 0), (hmlp, D, 0), (tgt, D, 0)], [ln2_g, ln2_b],
                                          [(D, F32), (D, BF16)], [D, D, LANE], tr)
    loss = lax.psum(jnp.sum(lpart[:, 0]), ("x", "y", "c"))

    def pair(grads):
        ns = list(grads)
        return _PairJob([grads[n] for n in ns], [kind[n] for n in ns], [big[n].shape for n in ns])

    my_core = lax.axis_index("c").astype(jnp.int32).reshape(1)
    my_chip = (2 * lax.axis_index("x") + lax.axis_index("y")).astype(jnp.int32).reshape(1)
    chip_sums = {}

    def pair_sums(grads, theirs):
        for n, t in zip(grads, theirs):
            chip_sums[n] = _pair_sum("pair_sum_" + n, grads[n], kind[n], big[n].shape, t, my_core)
        return {n: chip_sums[n] for n in grads}

    def chip(sums, rels=CHIP_RELS, by_source=None):
        ns = list(sums)
        return _ChipJob([sums[n] for n in ns], [big[n].shape for n in ns], rels, by_source)

    landed = {}
    da = _matmul("mlp_down_dx", dz2b, wdown, tb=True, out_dtypes=(BF16,), extras=(rb,),
                 epilogue=lambda acc, r: (acc * (2.0 * r.astype(F32)),))
    g_down = {"w_down": _matmul("mlp_down_dw", a2b, dz2b, ta=True, out_dtypes=(BF16,))}
    g_wup, *pr = _matmul("mlp_up_dw", x1b, da, ta=True, out_dtypes=(BF16,), jobs=[pair(g_down)])
    sums_down = pair_sums(g_down, pr)
    g_up = {"w_up": g_wup}
    dx1m, l_a, *pr = _matmul("mlp_up_dx", da, wup, tb=True, jobs=[chip(sums_down, CHIP_RELS[:2]), pair(g_up)])
    sums_up = pair_sums(g_up, pr)

    def ln1_bwd(dm, dzz, z, g):
        xhat, rstd = _ln_stats(z)
        dy = dm + alpha * dzz
        dz = _ln_bwd(dy, xhat, rstd, g)
        return dz, dz, dy * xhat, dy

    dz1, dz1b, dg1, db1 = _rowwise("ln1_bwd", ln1_bwd, [(dx1m, D, 0), (dz2, D, 0), (z1, D, 0)], [ln1_g],
                                   [(D, F32), (D, BF16)], [D, D], tr)
    g_wout = _matmul("mix_out_dw", mixin, dz1b, ta=True, out_dtypes=(BF16,))
    dmix = _matmul("mix_out_dx", dz1b, wout, tb=True)

    dqh, dkn, dkr, dvv, l_b, l_c = _attention_bwd(qh, kh, vx, att_o, lse, dmix, H,
                                                  jobs=[chip(sums_down, CHIP_RELS[2:]), chip(sums_up)])
    landed["w_down"], landed["w_up"] = [l_a, l_b], [l_c]
    att_scale = (MLA_NOPE + MLA_ROPE) ** -0.5

    def mla_heads_bwd(dq, dkrh, cp, sa, sb):
        parts, acc = [], dkrh[:, :LANE]
        dq = dq * att_scale
        for h in range(H):
            o = h * MLA_HEAD_PAD
            parts += [dq[:, o:o + MLA_NOPE], _unrope64(dq[:, o + MLA_NOPE:o + MLA_HEAD_PAD], cp, sa, sb)]
            if h:
                acc = acc + dkrh[:, h * LANE:(h + 1) * LANE]
        return jnp.concatenate(parts, axis=1), _unrope64(acc, cp, sa, sb)

    dqb, dkr128 = _rowwise(
        "mla_heads_bwd", mla_heads_bwd,
        [(dqh, H * MLA_HEAD_PAD, 0), (dkr, H * LANE, 0), (cosp, LANE, 0), (sina, LANE, 0), (sinb, LANE, 0)],
        [], [(H * MLA_HEAD_PAD, BF16), (LANE, F32)], [], tr)
    dcqn = _matmul("mla_q_dx", dqb, wuq, tb=True)
    g_wuq = _matmul("mla_q_dw", cqn, dqb, ta=True, out_dtypes=(BF16,))
    dkvb = jnp.concatenate([dkn, dvv], axis=1)
    dckvn = _matmul("mla_kv_dx", dkvb, wukv, tb=True)
    g_wukv = _matmul("mla_kv_dw", ckvn, dkvb, ta=True, out_dtypes=(BF16,))

    def rms_bwd(c, dy, g):
        rstd = lax.rsqrt(jnp.mean(jnp.square(c), axis=-1, keepdims=True) + EPS)
        dyg = dy * g
        dc = rstd * (dyg - c * (rstd * rstd) * jnp.mean(dyg * c, axis=-1, keepdims=True))
        return dc, dy * c * rstd

    def mla_prep_bwd(p, dq_, dkv_, dkr_, gq, gkv):
        dcq, dgq_ = rms_bwd(p[:, :c_kv], dq_, gq)
        dckv, dgkv_ = rms_bwd(p[:, c_kv:c_kr], dkv_, gkv)
        pad = jnp.zeros((p.shape[0], mla_in_pad - c_kr - LANE), F32)
        return jnp.concatenate([dcq, dckv, dkr_, pad], axis=1), dgq_, dgkv_

    d_p2, dgq, dgkv = _rowwise(
        "mla_prep_bwd", mla_prep_bwd, [(p2, mla_in_pad, 0), (dcqn, MLA_Q_RANK, 0), (dckvn, MLA_KV_RANK, 0), (dkr128, LANE, 0)],
        [q_norm_g, kv_norm_g], [(mla_in_pad, BF16)], [MLA_Q_RANK, MLA_KV_RANK], tr)

    def gate_bwd(pre, gate, dout, gn):
        dpre, dgate, dgn = [], [], []
        for h in range(H):
            sl = slice(h * RET_DV, (h + 1) * RET_DV)
            nrm, rstd = _ln_stats(pre[:, sl])
            gate_h, do_h, gn_h = gate[:, sl], dout[:, sl], gn[:, sl]
            sg = jax.nn.sigmoid(gate_h)
            silu = gate_h * sg
            dgate.append(do_h * (nrm * gn_h) * (sg * (1.0 + gate_h * (1.0 - sg))))
            dn = do_h * silu
            dgn.append(dn * nrm)
            dpre.append(_ln_bwd(dn, nrm, rstd, gn_h))
        return jnp.concatenate(dpre, axis=1), jnp.concatenate(dgate, axis=1), jnp.concatenate(dgn, axis=1)

    dpre, drg, dgn = _rowwise("ret_gate_bwd", gate_bwd, [(ret_pre, RW, 0), (p1, RW, 3), (dmix, RW, 0)], [ret_gn_g],
                              [(RW, F32), (RW, BF16)], [RW], tr)
    g_uq = g_wuq.reshape(MLA_Q_RANK, H, MLA_HEAD_PAD)[:, :, :f_uq].reshape(MLA_Q_RANK, N_DEV, H * f_uq // N_DEV)
    mid_grads = {"w_uq": jnp.transpose(g_uq, (1, 0, 2)), "w_uk": g_wukv[:, :RW], "w_uv": g_wukv[:, RW:], "w_out": g_wout}
    (drq, *pr), (drk, drv) = _retention_bwd(p1, dpre, cos2, sin2, H, dq_jobs=[pair(mid_grads)])
    sums_mid = pair_sums(mid_grads, pr)
    d_p1 = jnp.concatenate([drq, drk, drv, drg], axis=1)

    g_w1, *l_a = _matmul("proj_ret_dw", d_p1, xb, ta=True, out_dtypes=(BF16,), jobs=[chip(sums_mid)])
    for n, a_ in zip(mid_grads, l_a):
        landed[n] = [a_]
    g_w2 = _matmul("proj_mla_dw", d_p2, xb, ta=True, out_dtypes=(BF16,))

    g_win = {"w_in": jnp.concatenate([g_w1, g_w2[:mla_in]], axis=0).reshape(N_DEV, in_width // N_DEV, D)}
    small = jnp.concatenate([dgq, dgkv, dgn, dg1, db1, dg2, db2], axis=1)
    dx_a, *pr = _matmul("proj_mla_dx", d_p2, w2, extras=(dz1,), epilogue=lambda acc, d: (acc + alpha * d,), jobs=[pair(g_win)])
    sums_in = pair_sums(g_win, pr)
    grad_x, l_a, small_landed = _matmul("proj_ret_dx", d_p1, w1, extras=(dx_a,), epilogue=lambda acc, d: (acc + d,),
                                        jobs=[chip(sums_in, CHIP_RELS[:2], by_source=small)])

    moments = dict(zip(names, zip([m_w_in, m_w_uq, m_w_uk, m_w_uv, m_w_out, m_w_up, m_w_down],
                                  [v_w_in, v_w_uq, v_w_uk, v_w_uv, v_w_out, v_w_up, v_w_down])))
    res = {}

    def adamw(n, jobs=()):
        as_held = (lambda a: a[0].T) if n == "w_in" else (lambda a: a[0])
        as_given = (lambda r: r.T[None]) if n == "w_in" else (lambda r: r[None])
        out = _adamw("adamw_" + n, chip_sums[n], my_chip, landed[n], big[n], as_held(moments[n][0]), as_held(moments[n][1]),
                     jobs=jobs)
        res[n] = [as_given(r) for r in out[:4]]
        return out[4:]

    (l_b,) = adamw("w_out", jobs=[chip(sums_in, CHIP_RELS[2:])])
    landed["w_in"] = [l_a, l_b]
    for n in names:
        if n != "w_out":
            adamw(n)

    small_names = ["q_norm_g", "kv_norm_g", "ret_gn_g", "ln1_g", "ln1_b", "ln2_g", "ln2_b"]
    small_w = [q_norm_g, kv_norm_g, ret_gn_g, ln1_g, ln1_b, ln2_g, ln2_b]
    small_m = [m_q_norm_g, m_kv_norm_g, m_ret_gn_g, m_ln1_g, m_ln1_b, m_ln2_g, m_ln2_b]
    small_v = [v_q_norm_g, v_kv_norm_g, v_ret_gn_g, v_ln1_g, v_ln1_b, v_ln2_g, v_ln2_b]
    cat = lambda arrs: jnp.concatenate(arrs, axis=1)
    sres = _adamw_small(small_landed, cat(small_w), cat(small_m), cat(small_v))
    off = 0
    for n, w in zip(small_names, small_w):
        res[n] = [r[:, off:off + w.shape[1]] for r in sres]
        off += w.shape[1]

    order = ["w_in", "q_norm_g", "w_uq", "kv_norm_g", "w_uk", "w_uv", "ret_gn_g", "w_out", "ln1_g", "ln1_b",
             "w_up", "w_down", "ln2_g", "ln2_b"]
    outs = [loss, grad_x[None]]
    for k in range(4):
        outs += [res[n][k] for n in order]
    return tuple(outs)
```

```python
import functools

import numpy as np

import jax
import jax.numpy as jnp
from jax import lax
from jax.experimental import pallas as pl
from jax.experimental.pallas import tpu as pltpu

F32 = jnp.float32
BF16 = jnp.bfloat16

CHUNK = 64
RET_DK = 128
RET_DV = 128
MLA_NOPE = 128
MLA_ROPE = 64
MLA_DV = 128
MLA_Q_RANK = 768
MLA_KV_RANK = 512
MLA_HEAD_PAD = 256
ROPE_BASE = 10000.0
EPS = 1e-5
DEPTH = 1
DEEPNORM_ALPHA = (2.0 * DEPTH) ** 0.25
ADAM_LR = 0.001
ADAM_B1 = 0.9
ADAM_B2 = 0.999
ADAM_EPS = 1e-08
ADAM_WD = 0.01
ADAM_STEP = 10

N_DEV = 8
LANE = 128
SUBLANE = 8
VMEM_LIMIT = 48 * 1024 * 1024
ADAMW_TILE_ELEMS = 128 * 1024
PAIR_TILE_ELEMS = 1024 * 1024

MESH = pl.DeviceIdType.MESH
ANY = pl.BlockSpec(memory_space=pl.ANY)

NN = (((1,), (0,)), ((), ()))
NT = (((1,), (1,)), ((), ()))
TN = (((0,), (0,)), ((), ()))


def _dot(a, b, dims=NN):
    return lax.dot_general(a.astype(BF16), b.astype(BF16), dims, preferred_element_type=F32)


def _params(n_axes):
    return pltpu.CompilerParams(dimension_semantics=("arbitrary",) * n_axes, vmem_limit_bytes=VMEM_LIMIT)


def _tile(n, want):
    if n <= want:
        return n
    t = want
    while t >= LANE:
        if n % t == 0:
            return t
        t -= LANE
    return n


def _my_place():
    return lax.axis_index("x"), lax.axis_index("y"), lax.axis_index("c")


def _flat(px, py, pc):
    return 4 * px + 2 * py + pc


def _block_of(ref, kind, idx, rows, cols, sub=None):
    lo, n = (0, rows) if sub is None else sub
    if kind == "blk":
        return ref.at[idx, pl.ds(lo, n), :]
    if kind == "rows":
        return ref.at[pl.ds(pl.multiple_of(idx * rows + lo, n), n), :]
    return ref.at[pl.ds(lo, n), pl.ds(pl.multiple_of(idx * cols, cols), cols)]


CHIP_X, CHIP_Y, CHIP_DIAGONAL = 0, 1, 2
ALL_CHIPS = (CHIP_X, CHIP_Y, CHIP_DIAGONAL)


class _GatherJob:
    def __init__(self, shards, kinds, mid, late, pieces=(((0, 1), ALL_CHIPS, True),), base=None):
        nw = len(shards)
        self.nw, self.kinds, self.pieces = nw, list(kinds), list(pieces)
        for _, chips, _ in self.pieces:
            assert CHIP_DIAGONAL not in chips or len(chips) in (1, 3)
        self.shapes = [s.shape for s in shards]
        self.ins = list(shards) + (list(base) if base is not None else [])
        self.alias = [(nw + w, w) for w in range(nw)] if base is not None else []
        self.out_shape = []
        for s, kind in zip(shards, kinds):
            r, c = s.shape
            shape = {"blk": (N_DEV, r, c), "rows": (N_DEV * r, c), "cols": (r, N_DEV * c)}[kind]
            self.out_shape.append(jax.ShapeDtypeStruct(shape, s.dtype))
        n = 7 * nw * len(self.pieces)
        self.sems = [pltpu.SemaphoreType.DMA((n,)), pltpu.SemaphoreType.DMA((n,)), pltpu.SemaphoreType.DMA((n,))]
        self.phases = [(0.0, self._start), (mid, self._relay), (late, self._pass_on), (1.0, self._finish)]

    def _each(self, src, dst, send_sems, recv_sems, local_sems):
        x, y, c = _my_place()
        all_chips = [(1 - x, y), (x, 1 - y), (1 - x, 1 - y)]
        out = []
        for w in range(self.nw):
            r, cc = self.shapes[w]
            for p, ((k_of, of), chips, own) in enumerate(self.pieces):
                sub = (k_of * (r // of), r // of)
                s0 = 7 * (w * len(self.pieces) + p)
                mine = src[w].at[pl.ds(*sub), :]

                def place(dev, w=w, r=r, cc=cc, sub=sub):
                    return _block_of(dst[w], self.kinds[w], _flat(*dev), r, cc, sub)

                def copy(k, block, to, from_src=False, place=place, mine=mine, s0=s0):
                    return pltpu.make_async_remote_copy(
                        src_ref=mine if from_src else place(block), dst_ref=place(block),
                        send_sem=send_sems.at[s0 + k], recv_sem=recv_sems.at[s0 + k], device_id=to, device_id_type=MESH)

                relay = copy(1 + CHIP_DIAGONAL, (x ^ (1 - c), y ^ c, c), (x ^ c, y ^ (1 - c), c))
                local = pltpu.make_async_copy(mine, place((x, y, c)), local_sems.at[s0])
                near = [(j, all_chips[j]) for j in chips if j != CHIP_DIAGONAL]
                out.append((near, CHIP_DIAGONAL in chips, own, copy, relay, local))
        return (x, y, c), (x, y, 1 - c), all_chips[CHIP_DIAGONAL], out

    def _start(self, *refs):
        me, sibling, diagonal, each = self._each(*refs)
        for near, far, own, copy, relay, local in each:
            if own:
                local.start()
                copy(0, me, sibling, from_src=True).start()
            for j, chip in near:
                copy(1 + j, me, (*chip, me[2]), from_src=True).start()
            if far and not near:
                relay.start()

    def _landed(self, near, copy, me, sibling):
        for j, chip in near:
            copy(1 + j, (*chip, me[2]), me).wait_recv()
            copy(4 + j, (*chip, me[2]), sibling).start()

    def _relay(self, *refs):
        me, sibling, diagonal, each = self._each(*refs)
        for near, far, own, copy, relay, local in each:
            if far and near:
                self._landed(near, copy, me, sibling)
                relay.start()

    def _pass_on(self, *refs):
        me, sibling, diagonal, each = self._each(*refs)
        for near, far, own, copy, relay, local in each:
            if not far:
                self._landed(near, copy, me, sibling)
            else:
                self._landed([(CHIP_DIAGONAL, diagonal)], copy, me, sibling)

    def _finish(self, *refs):
        me, sibling, diagonal, each = self._each(*refs)
        for near, far, own, copy, relay, local in each:
            if own:
                copy(0, sibling, me).wait_recv()
            for j, chip in near + [(CHIP_DIAGONAL, diagonal)] * int(far):
                copy(4 + j, (*chip, 1 - me[2]), me).wait_recv()
        for near, far, own, copy, relay, local in each:
            if own:
                copy(0, me, sibling, from_src=True).wait_send()
                local.wait()
            for j, chip in near:
                copy(1 + j, me, (*chip, me[2]), from_src=True).wait_send()
                copy(4 + j, (*chip, me[2]), sibling).wait_send()
            if far:
                relay.wait_send()
                copy(4 + CHIP_DIAGONAL, (*diagonal, me[2]), sibling).wait_send()


REL_ALL = [(1, 0, 0), (1, 0, 1), (0, 1, 0), (0, 1, 1), (1, 1, 0), (1, 1, 1), (0, 0, 1)]
N_CHIP = 4


class _PairJob:
    def __init__(self, grads, kinds, shard_shapes):
        self.ins = list(grads)
        self.kinds, self.shard_shapes, self.nw = list(kinds), list(shard_shapes), len(grads)
        self.out_shape = [jax.ShapeDtypeStruct((N_CHIP,) + tuple(s), g.dtype) for g, s in zip(grads, shard_shapes)]
        n = N_CHIP * self.nw
        self.sems = [pltpu.SemaphoreType.DMA((n,)), pltpu.SemaphoreType.DMA((n,))]
        self.phases = [(0.0, self._start), (1.0, self._finish)]

    def _copies(self, src, theirs, send_sems, recv_sems):
        x, y, c = _my_place()
        remote = []
        for w in range(self.nw):
            r, cc = self.shard_shapes[w]
            for q in range(N_CHIP):
                s = N_CHIP * w + q
                remote.append(pltpu.make_async_remote_copy(
                    src_ref=_block_of(src[w], self.kinds[w], 2 * q + (1 - c), r, cc), dst_ref=theirs[w].at[q],
                    send_sem=send_sems.at[s], recv_sem=recv_sems.at[s], device_id=(x, y, 1 - c), device_id_type=MESH))
        return remote

    def _start(self, *refs):
        for cp in self._copies(*refs):
            cp.start()

    def _finish(self, *refs):
        for cp in self._copies(*refs):
            cp.wait()


CHIP_RELS = [(1, 0), (0, 1), (1, 1)]


class _ChipJob:
    def __init__(self, chip_sums, shard_shapes, rels=CHIP_RELS, by_source=None, piece=(0, 1), base=None):
        nw = len(chip_sums)
        self.ins = list(chip_sums) + ([by_source] if by_source is not None else []) + (list(base) if base is not None else [])
        self.nw, self.has_src, self.rels, self.piece = nw, by_source is not None, list(rels), piece
        self.rows = [s[0] for s in shard_shapes]
        self.alias = [(nw + int(self.has_src) + w, w) for w in range(nw)] if base is not None else []
        self.out_shape = [jax.ShapeDtypeStruct((len(rels),) + tuple(s), a.dtype) for a, s in zip(chip_sums, shard_shapes)]
        if self.has_src:
            self.out_shape.append(jax.ShapeDtypeStruct((N_DEV,) + by_source.shape, by_source.dtype))
        n = len(rels) * self.nw + len(REL_ALL) * int(self.has_src)
        self.sems = [pltpu.SemaphoreType.DMA((n,)), pltpu.SemaphoreType.DMA((n,)), pltpu.SemaphoreType.DMA((1,))]
        self.phases = [(0.0, self._start), (1.0, self._finish)]

    def _copies(self, src, land, send_sems, recv_sems, local_sems):
        x, y, c = _my_place()
        nw = self.nw

        def remote_copy(s, src_ref, dst_ref, peer):
            return pltpu.make_async_remote_copy(src_ref=src_ref, dst_ref=dst_ref, send_sem=send_sems.at[s],
                                                recv_sem=recv_sems.at[s], device_id=peer, device_id_type=MESH)

        def rows(w):
            n = self.rows[w] // self.piece[1]
            return pl.ds(self.piece[0] * n, n)

        local, remote = [], []
        for k, (fx, fy) in enumerate(self.rels):
            px, py = x ^ fx, y ^ fy
            remote += [remote_copy(nw * k + w, src[w].at[2 * px + py, rows(w), :], land[w].at[k, rows(w), :], (px, py, c))
                       for w in range(nw)]
        if self.has_src:
            my_idx = _flat(x, y, c)
            local.append(pltpu.make_async_copy(src[nw], land[nw].at[my_idx], local_sems.at[0]))
            remote += [remote_copy(nw * len(self.rels) + k, src[nw], land[nw].at[my_idx], (x ^ fx, y ^ fy, c ^ fc))
                       for k, (fx, fy, fc) in enumerate(REL_ALL)]
        return local, remote

    def _start(self, *refs):
        local, remote = self._copies(*refs)
        for cp in local + remote:
            cp.start()

    def _finish(self, *refs):
        local, remote = self._copies(*refs)
        for cp in remote + local:
            cp.wait()


def _row_tile(R, C, elems=ADAMW_TILE_ELEMS):
    tr = 16
    while tr * 2 * C <= elems and R % (tr * 2) == 0:
        tr *= 2
    return tr if R % tr == 0 else R


BF16_ROWS = 16


def _tile_2d(R, C, elems):
    if R % BF16_ROWS == 0:
        tr = _row_tile(R, C, elems)
        return tr, C, R // tr
    tc = C
    while tc % (2 * LANE) == 0 and R * tc > elems:
        tc //= 2
    return R, tc, C // tc


def _tile_index(tr, tc, C):
    return (lambda i: (i, 0)) if tc == C else (lambda i: (0, i))


def _part_spec(kind, tr, tc, R, C, n_lead):
    at = _tile_index(tr, tc, C)

    def dev(ids, pref):
        return 2 * ids[0] + pref[0] if n_lead == 2 else pref[0]

    def index(*args):
        ids, pref = args[:n_lead], args[n_lead]
        ri, ci = at(ids[-1])
        if kind == "blk":
            return dev(ids, pref), ri, ci
        if kind == "rows":
            return dev(ids, pref) * (R // tr) + ri, ci
        return ri, dev(ids, pref) * (C // tc) + ci

    return pl.BlockSpec((None, tr, tc) if kind == "blk" else (tr, tc), index)


def _pair_sum(name, grad, kind, shard_shape, theirs, c):
    R, C = shard_shape
    tr, tc, tiles = _tile_2d(R, C, PAIR_TILE_ELEMS)
    at = _tile_index(tr, tc, C)

    def body(c_ref, g_ref, t_ref, o_ref):
        o_ref[...] = (g_ref[...].astype(F32) + t_ref[...].astype(F32)).astype(o_ref.dtype)

    slot = pl.BlockSpec((None, tr, tc), lambda q, i, c_ref: (q, *at(i)))
    return pl.pallas_call(
        body, name=name,
        grid_spec=pltpu.PrefetchScalarGridSpec(
            num_scalar_prefetch=1, grid=(N_CHIP, tiles),
            in_specs=[_part_spec(kind, tr, tc, R, C, 2), slot], out_specs=slot),
        out_shape=jax.ShapeDtypeStruct(theirs.shape, theirs.dtype), compiler_params=_params(2))(c, grad, theirs)


class _Carried:
    def __init__(self, jobs, grid):
        self.jobs = list(jobs)
        self.steps = int(np.prod(grid))
        self.grid = tuple(grid)
        self.operands = [a for j in self.jobs for a in j.ins]
        self.out_shape = [o for j in self.jobs for o in j.out_shape]
        self.sems = [s for j in self.jobs for s in j.sems]
        self.in_specs = [ANY] * len(self.operands)
        self.out_specs = [ANY] * len(self.out_shape)

    def aliases(self, n_in, n_out):
        out, i0, o0 = {}, n_in, n_out
        for job in self.jobs:
            for ji, jo in getattr(job, "alias", []):
                out[i0 + ji] = o0 + jo
            i0, o0 = i0 + len(job.ins), o0 + len(job.out_shape)
        return out

    def _step(self):
        lin = pl.program_id(0)
        for a in range(1, len(self.grid)):
            lin = lin * self.grid[a] + pl.program_id(a)
        return lin

    def _run(self, in_refs, out_refs, sem_refs, last):
        lin = self._step()
        for job in self.jobs:
            ins, in_refs = in_refs[:len(job.ins)], in_refs[len(job.ins):]
            outs, out_refs = out_refs[:len(job.out_shape)], out_refs[len(job.out_shape):]
            sems, sem_refs = sem_refs[:len(job.sems)], sem_refs[len(job.sems):]
            for frac, fn in job.phases:
                if (frac >= 1.0) != last:
                    continue
                at = min(self.steps - 1, int(frac * (self.steps - 1)))
                pl.when(lin == at)(functools.partial(fn, ins, outs, *sems))

    def before(self, in_refs, out_refs, sem_refs):
        self._run(in_refs, out_refs, sem_refs, last=False)

    def after(self, in_refs, out_refs, sem_refs):
        self._run(in_refs, out_refs, sem_refs, last=True)


def _split_refs(refs, n_in, n_job_in, n_out, n_job_out, n_scratch):
    cuts = np.cumsum([0, n_in, n_job_in, n_out, n_job_out, n_scratch])
    return [refs[cuts[i]:cuts[i + 1]] for i in range(5)] + [refs[cuts[5]:]]


def _matmul(name, a, b, *, ta=False, tb=False, out_dtypes=(F32,), epilogue=None, extras=(), tm=1024, tn=1024, tk=2048, jobs=(),
            n=None):
    if ta:
        K, M = a.shape
    else:
        M, K = a.shape
    N = n if n is not None else (b.shape[0] if tb else b.shape[1])
    tm, tn, tk = _tile(M, tm), _tile(N, tn), _tile(K, tk)
    nk = K // tk
    grid = (M // tm, N // tn, nk)
    dims = TN if ta else (NT if tb else NN)
    n_ex, n_out = len(extras), len(out_dtypes)
    carried = _Carried(jobs, grid)

    def body(*refs):
        ins, job_ins, out_refs, job_outs, scratch, job_sems = _split_refs(
            refs, 2 + n_ex, len(carried.operands), n_out, len(carried.out_shape), int(nk > 1))
        a_ref, b_ref, ex_refs = ins[0], ins[1], ins[2:]
        k = pl.program_id(2)
        carried.before(job_ins, job_outs, job_sems)

        def finish(acc):
            res = (acc,) if epilogue is None else epilogue(acc, *[e[...] for e in ex_refs])
            for o_ref, r in zip(out_refs, res):
                o_ref[...] = r.astype(o_ref.dtype)

        if nk == 1:
            finish(_dot(a_ref[...], b_ref[...], dims))
        else:
            acc_ref = scratch[0]

            @pl.when(k == 0)
            def _():
                acc_ref[...] = _dot(a_ref[...], b_ref[...], dims)

            @pl.when(k > 0)
            def _():
                acc_ref[...] += _dot(a_ref[...], b_ref[...], dims)

            pl.when(k == nk - 1)(lambda: finish(acc_ref[...]))

        carried.after(job_ins, job_outs, job_sems)

    a_spec = pl.BlockSpec((tk, tm), lambda i, j, k: (k, i)) if ta else pl.BlockSpec((tm, tk), lambda i, j, k: (i, k))
    b_spec = pl.BlockSpec((tn, tk), lambda i, j, k: (j, k)) if tb else pl.BlockSpec((tk, tn), lambda i, j, k: (k, j))
    tile_spec = pl.BlockSpec((tm, tn), lambda i, j, k: (i, j))
    outs = pl.pallas_call(
        body,
        name=name,
        grid=grid,
        in_specs=[a_spec, b_spec] + [tile_spec] * n_ex + carried.in_specs,
        out_specs=[tile_spec] * n_out + carried.out_specs,
        out_shape=[jax.ShapeDtypeStruct((M, N), d) for d in out_dtypes] + carried.out_shape,
        scratch_shapes=[pltpu.VMEM((tm, tn), F32)] * int(nk > 1) + carried.sems,
        input_output_aliases=carried.aliases(2 + n_ex, n_out),
        compiler_params=_params(3),
    )(a, b, *extras, *carried.operands)
    return outs[0] if len(outs) == 1 else outs


def _rowwise(name, fn, row_in, const_in, row_out, acc_out, tm, jobs=()):
    T = row_in[0][0].shape[0]
    n_r, n_c, n_o, n_a = len(row_in), len(const_in), len(row_out), len(acc_out)
    carried = _Carried(jobs, (T // tm,))

    def body(*refs):
        in_refs, job_ins, own_outs, job_outs, _, job_sems = _split_refs(
            refs, n_r + n_c, len(carried.operands), n_o + n_a, len(carried.out_shape), 0)
        out_refs, acc_refs = own_outs[:n_o], own_outs[n_o:]
        carried.before(job_ins, job_outs, job_sems)
        res = fn(*[r[...] for r in in_refs])
        for o_ref, r in zip(out_refs, res[:n_o]):
            o_ref[...] = r.astype(o_ref.dtype)

        @pl.when(pl.program_id(0) == 0)
        def _():
            for a_ref in acc_refs:
                a_ref[...] = jnp.zeros_like(a_ref)

        for a_ref, r in zip(acc_refs, res[n_o:]):
            a_ref[...] += jnp.sum(r.reshape(tm // SUBLANE, SUBLANE, r.shape[-1]), axis=0)
        carried.after(job_ins, job_outs, job_sems)

    in_specs = [pl.BlockSpec((tm, w), functools.partial(lambda i, c: (i, c), c=cb)) for (_, w, cb) in row_in]
    in_specs += [pl.BlockSpec(c.shape, functools.partial(lambda i, nd: (0,) * nd, nd=c.ndim)) for c in const_in]
    out_specs = [pl.BlockSpec((tm, c), lambda i: (i, 0)) for (c, _) in row_out]
    out_specs += [pl.BlockSpec((SUBLANE, c), lambda i: (0, 0)) for c in acc_out]
    out_shape = [jax.ShapeDtypeStruct((T, c), d) for (c, d) in row_out]
    out_shape += [jax.ShapeDtypeStruct((SUBLANE, c), F32) for c in acc_out]
    return pl.pallas_call(
        body, name=name, grid=(T // tm,), in_specs=in_specs + carried.in_specs, out_specs=out_specs + carried.out_specs,
        out_shape=out_shape + carried.out_shape, scratch_shapes=carried.sems,
        input_output_aliases=carried.aliases(n_r + n_c, n_o + n_a), compiler_params=_params(1),
    )(*[r[0] for r in row_in], *const_in, *carried.operands)


def _ln_stats(z):
    mu = jnp.mean(z, axis=-1, keepdims=True)
    var = jnp.mean(jnp.square(z - mu), axis=-1, keepdims=True)
    rstd = lax.rsqrt(var + EPS)
    return (z - mu) * rstd, rstd


def _ln_bwd(dy, xhat, rstd, g):
    dxh = dy * g
    return rstd * (dxh - jnp.mean(dxh, axis=-1, keepdims=True) - xhat * jnp.mean(dxh * xhat, axis=-1, keepdims=True))


def _rms(x, g):
    return x * lax.rsqrt(jnp.mean(jnp.square(x), axis=-1, keepdims=True) + EPS) * g


def _rope128(x, cos2, sin2):
    return x * cos2 + pltpu.roll(x, RET_DK // 2, 1) * sin2


def _unrope128(dy, cos2, sin2):
    return dy * cos2 + pltpu.roll(dy * sin2, RET_DK // 2, 1)


def _rope64(x, cosp, sina, sinb):
    h = MLA_ROPE // 2
    return x * cosp + pltpu.roll(x, h, 1) * sina + pltpu.roll(x, LANE - h, 1) * sinb


def _unrope64(dy, cosp, sina, sinb):
    h = MLA_ROPE // 2
    return dy * cosp + pltpu.roll(dy * sina, LANE - h, 1) + pltpu.roll(dy * sinb, h, 1)


RET_BLOCK = 256
RET_HEADS_PER_STEP = 4


def _ret_tables(H, B):
    log_g = np.log1p(-np.exp2(-5.0 - np.arange(H, dtype=np.float32))).astype(np.float32)
    idx = np.arange(B, dtype=np.float32)
    dist = np.abs(idx[:, None] - idx[None, :])
    visible = (np.arange(B)[None, :] // CHUNK) <= (np.arange(B)[:, None] // CHUNK)
    dmat = np.exp(log_g[:, None, None] * dist).astype(np.float32) * visible[None].astype(np.float32)
    qd = np.exp(log_g[:, None] * (idx + 1.0)).astype(np.float32)
    kd = np.exp(log_g[:, None] * (B - 1 - idx)).astype(np.float32)
    sd = np.exp(log_g * B).astype(np.float32)
    bc = lambda v: np.ascontiguousarray(np.broadcast_to(v[:, :, None], (H, v.shape[1], LANE)))
    sdb = np.ascontiguousarray(np.broadcast_to(sd[:, None, None], (H, SUBLANE, LANE)))
    return jnp.asarray(dmat), jnp.asarray(bc(qd)), jnp.asarray(bc(kd)), jnp.asarray(sdb)


def _ret_specs(H, B, reverse, nb):
    blk = (lambda b: nb - 1 - b) if reverse else (lambda b: b)
    G = min(RET_HEADS_PER_STEP, H)
    W = G * LANE
    col = lambda off: pl.BlockSpec((B, W), functools.partial(lambda h, b, off: (blk(b), off // G + h), off=off))
    row = pl.BlockSpec((B, LANE), lambda h, b: (blk(b), 0))
    per_head = lambda r: pl.BlockSpec((G, r, LANE), lambda h, b: (h, 0, 0))
    dmat = pl.BlockSpec((G, B, B), lambda h, b: (h, 0, 0))
    return G, col, row, per_head, dmat


def _ret_call(name, body, n_in, n_out, operands, in_specs, out_specs, out_shape, grid, G, jobs, own_alias=None):
    carried = _Carried(jobs, grid)

    def wrapped(*refs):
        ins, job_ins, outs, job_outs, (state,), job_sems = _split_refs(
            refs, n_in, len(carried.operands), n_out, len(carried.out_shape), 1)
        carried.before(job_ins, job_outs, job_sems)

        @pl.when(pl.program_id(1) == 0)
        def _():
            state[...] = jnp.zeros_like(state)

        body(*ins, *outs, state)
        carried.after(job_ins, job_outs, job_sems)

    return pl.pallas_call(
        wrapped, name=name, grid=grid,
        in_specs=in_specs + carried.in_specs, out_specs=out_specs + carried.out_specs,
        out_shape=out_shape + carried.out_shape,
        scratch_shapes=[pltpu.VMEM((G, RET_DK, RET_DV), F32)] + carried.sems,
        input_output_aliases={**(own_alias or {}), **carried.aliases(n_in, n_out)},
        compiler_params=_params(2),
    )(*operands, *carried.operands)


def _retention_fwd(p1, cos2, sin2, gn_g, mix_in, H, jobs=()):
    T = p1.shape[0]
    B = min(RET_BLOCK, T)
    nb = T // B
    scale = RET_DK ** -0.5
    dmat, qd, kd, sd = _ret_tables(H, B)
    G, col, row, per_head, dspec = _ret_specs(H, B, False, nb)

    def body(q_ref, k_ref, v_ref, g_ref, cos_ref, sin_ref, d_ref, qd_ref, kd_ref, sd_ref, gn_ref, _, pre_ref, out_ref, s_ref):
        cos, sin = cos_ref[...], sin_ref[...]
        for g in range(G):
            hl = pl.ds(g * LANE, LANE)
            q = _rope128(q_ref[:, hl], cos, sin)
            k = _rope128(k_ref[:, hl], cos, sin) * scale
            v = v_ref[:, hl]
            s = s_ref[g]
            a = _dot(q, k, NT) * d_ref[g]
            ret = _dot(a, v) + _dot(q * qd_ref[g], s)
            s_ref[g] = sd_ref[g, 0:1, :] * s + _dot(k * kd_ref[g], v, TN)
            pre_ref[:, hl] = ret
            nrm, _ = _ln_stats(ret)
            out_ref[:, hl] = (jax.nn.silu(g_ref[:, hl]) * (nrm * gn_ref[:, hl])).astype(out_ref.dtype)

    out_spec = pl.BlockSpec((B, G * LANE), lambda h, b: (b, h))
    return _ret_call(
        "retention_fwd", body, 12, 2, (p1, p1, p1, p1, cos2, sin2, dmat, qd, kd, sd, gn_g, mix_in),
        [col(0), col(H), col(2 * H), col(3 * H), row, row, dspec, per_head(B), per_head(B), per_head(SUBLANE),
         pl.BlockSpec((1, G * LANE), lambda h, b: (0, h)), ANY],
        [out_spec, out_spec],
        [jax.ShapeDtypeStruct((T, H * RET_DV), F32), jax.ShapeDtypeStruct(mix_in.shape, mix_in.dtype)],
        (H // G, nb), G, jobs, own_alias={11: 1})


def _retention_bwd(p1, dpre, cos2, sin2, H, dq_jobs=(), dkv_jobs=()):
    T = p1.shape[0]
    B = min(RET_BLOCK, T)
    nb = T // B
    scale = RET_DK ** -0.5
    dmat, qd, kd, sd = _ret_tables(H, B)

    def dq_body(k_ref, v_ref, do_ref, cos_ref, sin_ref, d_ref, qd_ref, kd_ref, sd_ref, dq_ref, s_ref):
        cos, sin = cos_ref[...], sin_ref[...]
        for g in range(G):
            hl = pl.ds(g * LANE, LANE)
            k = _rope128(k_ref[:, hl], cos, sin) * scale
            v, do, s = v_ref[:, hl], do_ref[:, hl], s_ref[g]
            da = _dot(do, v, NT) * d_ref[g]
            dq = _dot(da, k) + _dot(do, s, NT) * qd_ref[g]
            dq_ref[:, hl] = _unrope128(dq, cos, sin).astype(dq_ref.dtype)
            s_ref[g] = sd_ref[g, 0:1, :] * s + _dot(k * kd_ref[g], v, TN)

    G, col, row, per_head, dspec = _ret_specs(H, B, False, nb)
    dq = _ret_call(
        "retention_bwd_dq", dq_body, 9, 1, (p1, p1, dpre, cos2, sin2, dmat, qd, kd, sd),
        [col(H), col(2 * H), col(0), row, row, dspec, per_head(B), per_head(B), per_head(SUBLANE)],
        [pl.BlockSpec((B, G * LANE), lambda h, b: (b, h))], [jax.ShapeDtypeStruct((T, H * RET_DK), BF16)],
        (H // G, nb), G, dq_jobs)

    def dkv_body(q_ref, k_ref, v_ref, do_ref, cos_ref, sin_ref, d_ref, qd_ref, kd_ref, sd_ref, dk_ref, dv_ref, g_ref):
        cos, sin = cos_ref[...], sin_ref[...]
        for g in range(G):
            hl = pl.ds(g * LANE, LANE)
            q = _rope128(q_ref[:, hl], cos, sin)
            k = _rope128(k_ref[:, hl], cos, sin) * scale
            v, do, st = v_ref[:, hl], do_ref[:, hl], g_ref[g]
            a = _dot(q, k, NT) * d_ref[g]
            da = _dot(do, v, NT) * d_ref[g]
            dv = _dot(a, do, TN) + _dot(k * kd_ref[g], st)
            dk = _dot(da, q, TN) + _dot(v, st, NT) * kd_ref[g]
            dv_ref[:, hl] = dv.astype(dv_ref.dtype)
            dk_ref[:, hl] = _unrope128(dk * scale, cos, sin).astype(dk_ref.dtype)
            g_ref[g] = sd_ref[g, 0:1, :] * st + _dot(q * qd_ref[g], do, TN)

    G, col, row, per_head, dspec = _ret_specs(H, B, True, nb)
    out_spec = pl.BlockSpec((B, G * LANE), lambda h, b: (nb - 1 - b, h))
    dkv = _ret_call(
        "retention_bwd_dkv", dkv_body, 10, 2, (p1, p1, p1, dpre, cos2, sin2, dmat, qd, kd, sd),
        [col(0), col(H), col(2 * H), col(0), row, row, dspec, per_head(B), per_head(B), per_head(SUBLANE)],
        [out_spec, out_spec],
        [jax.ShapeDtypeStruct((T, H * RET_DK), BF16), jax.ShapeDtypeStruct((T, H * RET_DV), BF16)],
        (H // G, nb), G, dkv_jobs)
    return dq, dkv


ATT_BLOCK = 512


ATT_ROW_CHUNKS = 2
ATT_HEADS_PER_STEP = 8
ATT_BWD_HEADS_PER_STEP = 2
LOG2E = 1.4426950408889634


def _chunk_mask(rows, row0, cols):
    qi = (lax.broadcasted_iota(jnp.int32, (rows, cols), 0) + row0) // CHUNK
    kj = lax.broadcasted_iota(jnp.int32, (rows, cols), 1) // CHUNK
    return kj <= qi


def _attention_fwd(qh, kh, vx, H, jobs=()):
    T = qh.shape[0]
    t = min(ATT_BLOCK, T)
    n = T // t
    r = t // ATT_ROW_CHUNKS
    G = min(ATT_HEADS_PER_STEP, H)
    P = MLA_HEAD_PAD
    scale = (MLA_NOPE + MLA_ROPE) ** -0.5
    c2 = scale * LOG2E
    carried = _Carried(jobs, (H // G, n, n))

    def body(*refs):
        (q_ref, k_ref, v_ref), job_ins, (o_ref, ob_ref, lse_ref), job_outs, (m_ref, acc_ref), job_sems = _split_refs(
            refs, 3, len(carried.operands), 3, len(carried.out_shape), 2)
        i, j = pl.program_id(1), pl.program_id(2)
        carried.before(job_ins, job_outs, job_sems)

        @pl.when(j == 0)
        def _():
            m_ref[...] = jnp.full_like(m_ref, -jnp.inf)
            acc_ref[...] = jnp.zeros_like(acc_ref)

        def update(masked):
            for g in range(G):
                hc = pl.ds(g * P, P)
                for ch in range(ATT_ROW_CHUNKS):
                    rs = pl.ds(ch * r, r)
                    kc = (ch + 1) * r if masked else t
                    ks = pl.ds(0, kc)
                    s = _dot(q_ref[rs, hc], k_ref[ks, hc], NT)
                    if masked:
                        s = jnp.where(_chunk_mask(r, ch * r, kc), s, -jnp.inf)
                    m_prev = m_ref[rs, pl.ds(g * LANE, LANE)]
                    m_new = jnp.maximum(m_prev, jnp.max(s, axis=-1, keepdims=True))
                    alpha = jnp.exp2((m_prev - m_new) * c2)
                    p = jnp.exp2((s - jnp.tile(m_new, (1, kc // LANE))) * c2)
                    acc_ref[rs, hc] = jnp.tile(alpha, (1, 2)) * acc_ref[rs, hc] + _dot(p, v_ref[ks, hc])
                    m_ref[rs, pl.ds(g * LANE, LANE)] = m_new

        pl.when(j < i)(functools.partial(update, False))
        pl.when(j == i)(functools.partial(update, True))

        @pl.when(j == n - 1)
        def _():
            for g in range(G):
                acc = acc_ref[:, pl.ds(g * P, P)]
                l = acc[:, MLA_DV:]
                o = acc[:, :MLA_DV] / l
                o_ref[:, pl.ds(g * MLA_DV, MLA_DV)] = o
                ob_ref[:, pl.ds(g * MLA_DV, MLA_DV)] = o.astype(ob_ref.dtype)
                lse_ref[g] = m_ref[:, pl.ds(g * LANE, LANE)] * scale + jnp.log(l)

        carried.after(job_ins, job_outs, job_sems)

    kclamp = lambda i, j: jnp.minimum(j, i)
    o_spec = pl.BlockSpec((t, G * MLA_DV), lambda h, i, j: (i, h))
    mix_spec = pl.BlockSpec((t, G * MLA_DV), lambda h, i, j: (i, H // G + h))
    return pl.pallas_call(
        body, name="attention_fwd", grid=(H // G, n, n),
        in_specs=[pl.BlockSpec((t, G * P), lambda h, i, j: (i, h)),
                  pl.BlockSpec((t, G * P), lambda h, i, j: (kclamp(i, j), h)),
                  pl.BlockSpec((t, G * P), lambda h, i, j: (kclamp(i, j), h))] + carried.in_specs,
        out_specs=[o_spec, mix_spec, pl.BlockSpec((G, t, LANE), lambda h, i, j: (h, i, 0))] + carried.out_specs,
        out_shape=[jax.ShapeDtypeStruct((T, H * MLA_DV), F32), jax.ShapeDtypeStruct((T, 2 * H * MLA_DV), BF16),
                   jax.ShapeDtypeStruct((H, T, LANE), F32)] + carried.out_shape,
        scratch_shapes=[pltpu.VMEM((t, G * LANE), F32), pltpu.VMEM((t, G * P), F32)] + carried.sems,
        input_output_aliases=carried.aliases(3, 3),
        compiler_params=_params(3),
    )(qh, kh, vx, *carried.operands)


def _attention_bwd(qh, kh, vx, o, lse, dmix, H, jobs=()):
    T = qh.shape[0]
    t = min(ATT_BLOCK, T)
    n = T // t
    r = t // ATT_ROW_CHUNKS
    G = min(ATT_BWD_HEADS_PER_STEP, H)
    P = MLA_HEAD_PAD
    scale = (MLA_NOPE + MLA_ROPE) ** -0.5
    c2 = scale * LOG2E
    carried = _Carried(jobs, (H // G, n, n))

    def body(*refs):
        ((q_ref, k_ref, v_ref, do_ref, o_ref, lse_ref), job_ins, (dq_ref, dkn_ref, dkr_ref, dv_ref), job_outs,
         (dk_acc, dv_acc), job_sems) = _split_refs(refs, 6, len(carried.operands), 4, len(carried.out_shape), 2)
        j, i = pl.program_id(1), pl.program_id(2)
        carried.before(job_ins, job_outs, job_sems)

        @pl.when(jnp.logical_and(j == 0, i == 0))
        def _():
            dq_ref[...] = jnp.zeros_like(dq_ref)

        @pl.when(i == j)
        def _():
            dk_acc[...] = jnp.zeros_like(dk_acc)
            dv_acc[...] = jnp.zeros_like(dv_acc)

        def update(masked):
            for g in range(G):
                hc, hv = pl.ds(g * P, P), pl.ds(g * MLA_DV, MLA_DV)
                for ch in range(ATT_ROW_CHUNKS):
                    rs = pl.ds(ch * r, r)
                    kc = (ch + 1) * r if masked else t
                    ks = pl.ds(0, kc)
                    k, v = k_ref[ks, hc], v_ref[ks, pl.ds(g * P, MLA_DV)]
                    q, do = q_ref[rs, hc], do_ref[rs, hv]
                    s = _dot(q, k, NT)
                    if masked:
                        s = jnp.where(_chunk_mask(r, ch * r, kc), s, -jnp.inf)
                    p = jnp.exp2(s * c2 - jnp.tile(lse_ref[g, rs, :] * LOG2E, (1, kc // LANE)))
                    dv_acc[ks, hv] += _dot(p, do, TN)
                    dp = _dot(do, v, NT)
                    delta = jnp.sum(do * o_ref[rs, hv], axis=-1, keepdims=True)
                    ds = p * (dp - delta)
                    rows = pl.ds(pl.multiple_of(i * t, t) + ch * r, r)
                    dq_ref[rows, hc] += _dot(ds, k)
                    dk_acc[ks, hc] += _dot(ds, q, TN)

        pl.when(i > j)(functools.partial(update, False))
        pl.when(i == j)(functools.partial(update, True))

        @pl.when(i == n - 1)
        def _():
            for g in range(G):
                dk = dk_acc[:, pl.ds(g * P, P)] * scale
                dkn_ref[:, pl.ds(g * LANE, LANE)] = dk[:, :MLA_NOPE].astype(dkn_ref.dtype)
                dkr_ref[:, pl.ds(g * LANE, LANE)] = dk[:, MLA_NOPE:]
            dv_ref[...] = dv_acc[...].astype(dv_ref.dtype)

        carried.after(job_ins, job_outs, job_sems)

    qclamp = lambda j, i: jnp.maximum(i, j)
    kv_out = pl.BlockSpec((t, G * LANE), lambda h, j, i: (j, h))
    q_in = pl.BlockSpec((t, G * MLA_DV), lambda h, j, i: (qclamp(j, i), h))
    return pl.pallas_call(
        body, name="attention_bwd", grid=(H // G, n, n),
        in_specs=[pl.BlockSpec((t, G * P), lambda h, j, i: (qclamp(j, i), h)),
                  pl.BlockSpec((t, G * P), lambda h, j, i: (j, h)),
                  pl.BlockSpec((t, G * P), lambda h, j, i: (j, h)),
                  pl.BlockSpec((t, G * MLA_DV), lambda h, j, i: (qclamp(j, i), H // G + h)),
                  q_in,
                  pl.BlockSpec((G, t, LANE), lambda h, j, i: (h, qclamp(j, i), 0))] + carried.in_specs,
        out_specs=[pl.BlockSpec((T, G * P), lambda h, j, i: (0, h)), kv_out, kv_out, kv_out] + carried.out_specs,
        out_shape=[jax.ShapeDtypeStruct((T, H * P), F32), jax.ShapeDtypeStruct((T, H * LANE), BF16),
                   jax.ShapeDtypeStruct((T, H * LANE), F32), jax.ShapeDtypeStruct((T, H * LANE), BF16)] + carried.out_shape,
        scratch_shapes=[pltpu.VMEM((t, G * P), F32), pltpu.VMEM((t, G * MLA_DV), F32)] + carried.sems,
        input_output_aliases=carried.aliases(6, 4),
        compiler_params=_params(3),
    )(qh, kh, vx, dmix, o, lse, *carried.operands)


def _adamw_math(w, g, m, v):
    m = ADAM_B1 * m + (1.0 - ADAM_B1) * g
    v = ADAM_B2 * v + (1.0 - ADAM_B2) * jnp.square(g)
    m_hat = m / (1.0 - ADAM_B1 ** ADAM_STEP)
    v_hat = v / (1.0 - ADAM_B2 ** ADAM_STEP)
    delta = -ADAM_LR * (m_hat / (jnp.sqrt(v_hat) + ADAM_EPS) + ADAM_WD * w)
    return delta, m, v


def _adamw(name, chip_sums, my_chip, landed, w, m, v, jobs=()):
    R, C = w.shape
    tr, tc, tiles = _tile_2d(R, C, ADAMW_TILE_ELEMS)
    at = _tile_index(tr, tc, C)
    nl = len(landed)
    assert sum(l.shape[0] for l in landed) == N_CHIP - 1
    carried = _Carried(jobs, (tiles,))

    def body(q_ref, *refs):
        (own_ref, *rest), job_ins, (g_out, d_out, m_out, v_out), job_outs, _, job_sems = _split_refs(
            refs, nl + 4, len(carried.operands), 4, len(carried.out_shape), 0)
        l_refs, (w_ref, m_ref, v_ref) = rest[:nl], rest[nl:]
        carried.before(job_ins, job_outs, job_sems)
        g = own_ref[...].astype(F32)
        for l_ref in l_refs:
            for s in range(l_ref.shape[0]):
                g = g + l_ref[s].astype(F32)
        delta, mn, vn = _adamw_math(w_ref[...], g, m_ref[...], v_ref[...])
        g_out[...] = g
        d_out[...] = delta
        m_out[...] = mn
        v_out[...] = vn
        carried.after(job_ins, job_outs, job_sems)

    spec = pl.BlockSpec((tr, tc), lambda i, q_ref: at(i))
    own_spec = pl.BlockSpec((None, tr, tc), lambda i, q_ref: (q_ref[0], *at(i)))
    l_specs = [pl.BlockSpec((l.shape[0], tr, tc), lambda i, q_ref: (0, *at(i))) for l in landed]
    return pl.pallas_call(
        body, name=name,
        grid_spec=pltpu.PrefetchScalarGridSpec(
            num_scalar_prefetch=1, grid=(tiles,),
            in_specs=[own_spec] + l_specs + [spec, spec, spec] + carried.in_specs,
            out_specs=[spec] * 4 + carried.out_specs, scratch_shapes=carried.sems),
        out_shape=[jax.ShapeDtypeStruct((R, C), F32)] * 4 + carried.out_shape,
        input_output_aliases={1 + i: o for i, o in carried.aliases(nl + 4, 4).items()},
        compiler_params=_params(1),
    )(my_chip, chip_sums, *landed, w, m, v, *carried.operands)


def _adamw_small(landed, w, m, v):
    P = w.shape[1]

    def body(l_ref, w_ref, m_ref, v_ref, g_out, d_out, m_out, v_out):
        acc = l_ref[0]
        for s in range(1, N_DEV):
            acc = acc + l_ref[s]
        g = jnp.sum(acc, axis=0, keepdims=True)
        delta, mn, vn = _adamw_math(w_ref[...], g, m_ref[...], v_ref[...])
        g_out[...] = g
        d_out[...] = delta
        m_out[...] = mn
        v_out[...] = vn

    return pl.pallas_call(
        body, name="adamw_replicated", out_shape=[jax.ShapeDtypeStruct((1, P), F32)] * 4,
    )(landed, w, m, v)


def _rope_rows():
    inv_r = ROPE_BASE ** (-jnp.arange(0, RET_DK, 2, dtype=F32) / RET_DK)
    inv_m = ROPE_BASE ** (-jnp.arange(0, MLA_ROPE, 2, dtype=F32) / MLA_ROPE)
    h = MLA_ROPE // 2
    one, zero = jnp.ones((h,), F32), jnp.zeros((h,), F32)
    pad = jnp.zeros((LANE - MLA_ROPE,), F32)
    rows = [jnp.concatenate([inv_r, inv_r]), jnp.concatenate([-jnp.ones((RET_DK // 2,), F32), jnp.ones((RET_DK // 2,), F32)]),
            jnp.concatenate([inv_m, inv_m, pad]), jnp.concatenate([one, one, pad]),
            jnp.concatenate([zero, one, pad]), jnp.concatenate([-one, zero, pad])]
    return jnp.stack(rows + [jnp.zeros((LANE,), F32)] * (SUBLANE - len(rows)))


def _prep_tile(xt, p, rows):
    ang2, angp = p * rows[0:1], p * rows[2:3]
    sp = jnp.sin(angp)
    return (xt, jnp.cos(ang2), jnp.sin(ang2) * rows[1:2], jnp.cos(angp) * rows[3:4], sp * rows[4:5], sp * rows[5:6])


def kernel(x, positions, w_in, q_norm_g, w_uq, kv_norm_g, w_uk, w_uv, ret_gn_g, w_out, ln1_g, ln1_b, w_up, w_down, ln2_g, ln2_b, loss_target, m_w_in, m_q_norm_g, m_w_uq, m_kv_norm_g, m_w_uk, m_w_uv, m_ret_gn_g, m_w_out, m_ln1_g, m_ln1_b, m_w_up, m_w_down, m_ln2_g, m_ln2_b, v_w_in, v_q_norm_g, v_w_uq, v_kv_norm_g, v_w_uk, v_w_uv, v_ret_gn_g, v_w_out, v_ln1_g, v_ln1_b, v_w_up, v_w_down, v_ln2_g, v_ln2_b):
    T, D = x.shape[1], x.shape[2]
    H = D // 256
    RW = H * RET_DV
    x = x[0]
    tgt = loss_target[0]
    alpha = DEEPNORM_ALPHA
    in_width = w_in.shape[2] * N_DEV
    mla_in = in_width - 4 * RW
    mla_in_pad = -(-mla_in // 512) * 512
    f_uq = MLA_NOPE + MLA_ROPE

    names = ["w_in", "w_uq", "w_uk", "w_uv", "w_out", "w_up", "w_down"]
    big = dict(zip(names, [w_in[0].T, w_uq[0], w_uk[0], w_uv[0], w_out[0], w_up[0], w_down[0]]))
    kind = dict(zip(names, ["blk", "blk", "cols", "cols", "rows", "cols", "rows"]))
    half = {n: w.astype(BF16) for n, w in big.items()}

    def gather(ns, **parts):
        return _GatherJob([half[n] for n in ns], [kind[n] for n in ns], 0.55, 0.9, **parts)

    tr = min(256, T)
    xb, cos2, sin2, cosp, sina, sinb, g_in = _rowwise(
        "prep", _prep_tile, [(x, D, 0), (positions[0].astype(F32)[:, None], 1, 0)], [_rope_rows()],
        [(D, BF16)] + [(LANE, F32)] * 5, [], tr, jobs=[gather(["w_in"])])
    win = g_in.reshape(in_width, D)
    w1, n1 = win, 4 * RW
    w2 = jnp.pad(win[4 * RW:], ((0, mla_in_pad - mla_in), (0, 0)))

    first, second, near, far = (0, 2), (1, 2), (CHIP_X, CHIP_Y), (CHIP_DIAGONAL,)
    p1, g_uq, wuk, wuv, wout = _matmul("proj_ret", xb, w1, tb=True, n=n1,
                                       jobs=[gather(["w_uq", "w_uk", "w_uv", "w_out"])])
    wuq = jnp.transpose(g_uq, (1, 0, 2)).reshape(MLA_Q_RANK, H, f_uq)
    wuq = jnp.pad(wuq, ((0, 0), (0, 0), (0, MLA_HEAD_PAD - f_uq))).reshape(MLA_Q_RANK, H * MLA_HEAD_PAD)
    wukv = jnp.concatenate([wuk, wuv], axis=1)
    p2 = _matmul("proj_mla", xb, w2, tb=True)
    c_kv, c_kr = MLA_Q_RANK, MLA_Q_RANK + MLA_KV_RANK

    def mla_prep(p, cp, sa, sb, gq, gkv):
        cqn = _rms(p[:, :c_kv], gq)
        ckvn = _rms(p[:, c_kv:c_kr], gkv)
        return cqn, ckvn, _rope64(p[:, c_kr:c_kr + LANE], cp, sa, sb)

    cqn, ckvn, krope = _rowwise(
        "mla_prep", mla_prep, [(p2, mla_in_pad, 0), (cosp, LANE, 0), (sina, LANE, 0), (sinb, LANE, 0)],
        [q_norm_g, kv_norm_g], [(MLA_Q_RANK, BF16), (MLA_KV_RANK, BF16), (LANE, BF16)], [], tr)
    qf = _matmul("mla_q", cqn, wuq)
    kv = _matmul("mla_kv", ckvn, wukv, out_dtypes=(BF16,))

    def mla_heads(q, kn, vv, kr, cp, sa, sb):
        qs, ks, vs = [], [], []
        ones = jnp.ones((q.shape[0], MLA_DV), vv.dtype)
        for h in range(H):
            o = h * MLA_HEAD_PAD
            qs += [q[:, o:o + MLA_NOPE], _rope64(q[:, o + MLA_NOPE:o + MLA_HEAD_PAD], cp, sa, sb)]
            ks += [kn[:, h * MLA_NOPE:(h + 1) * MLA_NOPE], kr]
            vs += [vv[:, h * MLA_DV:(h + 1) * MLA_DV], ones]
        return jnp.concatenate(qs, axis=1), jnp.concatenate(ks, axis=1), jnp.concatenate(vs, axis=1)

    qh, kh, vx = _rowwise(
        "mla_heads", mla_heads,
        [(qf, H * MLA_HEAD_PAD, 0), (kv, RW, 0), (kv, RW, 1), (krope, LANE, 0), (cosp, LANE, 0), (sina, LANE, 0), (sinb, LANE, 0)],
        [], [(H * MLA_HEAD_PAD, BF16)] * 3, [], tr)
    whole = (0, 1)
    att_o, mix_half, lse, wup_a = _attention_fwd(qh, kh, vx, H, jobs=[gather(["w_up"], pieces=[(whole, near, True)])])
    ret_pre, mixin, wup = _retention_fwd(
        p1, cos2, sin2, ret_gn_g, mix_half, H, jobs=[gather(["w_up"], pieces=[(whole, far, False)], base=[wup_a])])
    mix, wdown_a = _matmul("mix_out", mixin, wout, jobs=[gather(["w_down"], pieces=[(first, near, True)])])

    def ln1_fwd(xt, mt, g, b):
        z = alpha * xt + mt
        xhat, _ = _ln_stats(z)
        y = xhat * g + b
        return z, y, y

    z1, x1, x1b, wdown_b = _rowwise("ln1", ln1_fwd, [(x, D, 0), (mix, D, 0)], [ln1_g, ln1_b],
                                    [(D, F32), (D, F32), (D, BF16)], [], tr,
                                    jobs=[gather(["w_down"], pieces=[(second, near, True)], base=[wdown_a])])

    def relu2(acc):
        r = jnp.maximum(acc, 0.0)
        return r * r, r

    a2b, rb, wdown = _matmul("mlp_up", x1b, wup, out_dtypes=(BF16, BF16), epilogue=relu2,
                             jobs=[gather(["w_down"], pieces=[(whole, far, False)], base=[wdown_b])])
    hmlp = _matmul("mlp_down", a2b, wdown)

    def ln2_loss(x1t, ht, tt, g, b):
        xhat, rstd = _ln_stats(alpha * x1t + ht)
        err = xhat * g + b - tt
        dy = err / D
        dz = _ln_bwd(dy, xhat, rstd, g)
        lrow = 0.5 * jnp.mean(jnp.square(err), axis=-1, keepdims=True)
        return dz, dz, dy * xhat, dy, jnp.broadcast_to(lrow, (lrow.shape[0], LANE))

    dz2, dz2b, dg2, db2, lpart = _rowwise("ln2_loss", ln2_loss, [(x1, D, 0), (hmlp, D, 0), (tgt, D, 0)], [ln2_g, ln2_b],
                                          [(D, F32), (D, BF16)], [D, D, LANE], tr)
    loss = lax.psum(jnp.sum(lpart[:, 0]), ("x", "y", "c"))

    def pair(grads):
        ns = list(grads)
        return _PairJob([grads[n] for n in ns], [kind[n] for n in ns], [big[n].shape for n in ns])

    my_core = lax.axis_index("c").astype(jnp.int32).reshape(1)
    my_chip = (2 * lax.axis_index("x") + lax.axis_index("y")).astype(jnp.int32).reshape(1)
    chip_sums = {}

    def pair_sums(grads, theirs):
        for n, t in zip(grads, theirs):
            chip_sums[n] = _pair_sum("pair_sum_" + n, grads[n], kind[n], big[n].shape, t, my_core)
        return {n: chip_sums[n] for n in grads}

    def chip(sums, rels=CHIP_RELS, by_source=None):
        ns = list(sums)
        return _ChipJob([sums[n] for n in ns], [big[n].shape for n in ns], rels, by_source)

    landed = {}
    da = _matmul("mlp_down_dx", dz2b, wdown, tb=True, out_dtypes=(BF16,), extras=(rb,),
                 epilogue=lambda acc, r: (acc * (2.0 * r.astype(F32)),))
    g_down = {"w_down": _matmul("mlp_down_dw", a2b, dz2b, ta=True, out_dtypes=(BF16,))}
    g_wup, *pr = _matmul("mlp_up_dw", x1b, da, ta=True, out_dtypes=(BF16,), jobs=[pair(g_down)])
    sums_down = pair_sums(g_down, pr)
    g_up = {"w_up": g_wup}
    dx1m, l_a, *pr = _matmul("mlp_up_dx", da, wup, tb=True, jobs=[chip(sums_down, CHIP_RELS[:2]), pair(g_up)])
    sums_up = pair_sums(g_up, pr)

    def ln1_bwd(dm, dzz, z, g):
        xhat, rstd = _ln_stats(z)
        dy = dm + alpha * dzz
        dz = _ln_bwd(dy, xhat, rstd, g)
        return dz, dz, dy * xhat, dy

    dz1, dz1b, dg1, db1 = _rowwise("ln1_bwd", ln1_bwd, [(dx1m, D, 0), (dz2, D, 0), (z1, D, 0)], [ln1_g],
                                   [(D, F32), (D, BF16)], [D, D], tr)
    g_wout = _matmul("mix_out_dw", mixin, dz1b, ta=True, out_dtypes=(BF16,))
    dmix = _matmul("mix_out_dx", dz1b, wout, tb=True)

    dqh, dkn, dkr, dvv, l_b, l_c = _attention_bwd(qh, kh, vx, att_o, lse, dmix, H,
                                                  jobs=[chip(sums_down, CHIP_RELS[2:]), chip(sums_up)])
    landed["w_down"], landed["w_up"] = [l_a, l_b], [l_c]
    att_scale = (MLA_NOPE + MLA_ROPE) ** -0.5

    def mla_heads_bwd(dq, dkrh, cp, sa, sb):
        parts, acc = [], dkrh[:, :LANE]
        dq = dq * att_scale
        for h in range(H):
            o = h * MLA_HEAD_PAD
            parts += [dq[:, o:o + MLA_NOPE], _unrope64(dq[:, o + MLA_NOPE:o + MLA_HEAD_PAD], cp, sa, sb)]
            if h:
                acc = acc + dkrh[:, h * LANE:(h + 1) * LANE]
        return jnp.concatenate(parts, axis=1), _unrope64(acc, cp, sa, sb)

    dqb, dkr128 = _rowwise(
        "mla_heads_bwd", mla_heads_bwd,
        [(dqh, H * MLA_HEAD_PAD, 0), (dkr, H * LANE, 0), (cosp, LANE, 0), (sina, LANE, 0), (sinb, LANE, 0)],
        [], [(H * MLA_HEAD_PAD, BF16), (LANE, F32)], [], tr)
    dcqn = _matmul("mla_q_dx", dqb, wuq, tb=True)
    g_wuq = _matmul("mla_q_dw", cqn, dqb, ta=True, out_dtypes=(BF16,))
    dkvb = jnp.concatenate([dkn, dvv], axis=1)
    dckvn = _matmul("mla_kv_dx", dkvb, wukv, tb=True)
    g_wukv = _matmul("mla_kv_dw", ckvn, dkvb, ta=True, out_dtypes=(BF16,))

    def rms_bwd(c, dy, g):
        rstd = lax.rsqrt(jnp.mean(jnp.square(c), axis=-1, keepdims=True) + EPS)
        dyg = dy * g
        dc = rstd * (dyg - c * (rstd * rstd) * jnp.mean(dyg * c, axis=-1, keepdims=True))
        return dc, dy * c * rstd

    def mla_prep_bwd(p, dq_, dkv_, dkr_, gq, gkv):
        dcq, dgq_ = rms_bwd(p[:, :c_kv], dq_, gq)
        dckv, dgkv_ = rms_bwd(p[:, c_kv:c_kr], dkv_, gkv)
        pad = jnp.zeros((p.shape[0], mla_in_pad - c_kr - LANE), F32)
        return jnp.concatenate([dcq, dckv, dkr_, pad], axis=1), dgq_, dgkv_

    d_p2, dgq, dgkv = _rowwise(
        "mla_prep_bwd", mla_prep_bwd, [(p2, mla_in_pad, 0), (dcqn, MLA_Q_RANK, 0), (dckvn, MLA_KV_RANK, 0), (dkr128, LANE, 0)],
        [q_norm_g, kv_norm_g], [(mla_in_pad, BF16)], [MLA_Q_RANK, MLA_KV_RANK], tr)

    def gate_bwd(pre, gate, dout, gn):
        dpre, dgate, dgn = [], [], []
        for h in range(H):
            sl = slice(h * RET_DV, (h + 1) * RET_DV)
            nrm, rstd = _ln_stats(pre[:, sl])
            gate_h, do_h, gn_h = gate[:, sl], dout[:, sl], gn[:, sl]
            sg = jax.nn.sigmoid(gate_h)
            silu = gate_h * sg
            dgate.append(do_h * (nrm * gn_h) * (sg * (1.0 + gate_h * (1.0 - sg))))
            dn = do_h * silu
            dgn.append(dn * nrm)
            dpre.append(_ln_bwd(dn, nrm, rstd, gn_h))
        return jnp.concatenate(dpre, axis=1), jnp.concatenate(dgate, axis=1), jnp.concatenate(dgn, axis=1)

    dpre, drg, dgn = _rowwise("ret_gate_bwd", gate_bwd, [(ret_pre, RW, 0), (p1, RW, 3), (dmix, RW, 0)], [ret_gn_g],
                              [(RW, F32), (RW, BF16)], [RW], tr)
    g_uq = g_wuq.reshape(MLA_Q_RANK, H, MLA_HEAD_PAD)[:, :, :f_uq].reshape(MLA_Q_RANK, N_DEV, H * f_uq // N_DEV)
    mid_grads = {"w_uq": jnp.transpose(g_uq, (1, 0, 2)), "w_uk": g_wukv[:, :RW], "w_uv": g_wukv[:, RW:], "w_out": g_wout}
    (drq, *pr), (drk, drv) = _retention_bwd(p1, dpre, cos2, sin2, H, dq_jobs=[pair(mid_grads)])
    sums_mid = pair_sums(mid_grads, pr)
    d_p1 = jnp.concatenate([drq, drk, drv, drg], axis=1)

    g_w1, *l_a = _matmul("proj_ret_dw", d_p1, xb, ta=True, out_dtypes=(BF16,), jobs=[chip(sums_mid)])
    for n, a_ in zip(mid_grads, l_a):
        landed[n] = [a_]
    g_w2 = _matmul("proj_mla_dw", d_p2, xb, ta=True, out_dtypes=(BF16,))

    g_win = {"w_in": jnp.concatenate([g_w1, g_w2[:mla_in]], axis=0).reshape(N_DEV, in_width // N_DEV, D)}
    small = jnp.concatenate([dgq, dgkv, dgn, dg1, db1, dg2, db2], axis=1)
    dx_a, *pr = _matmul("proj_mla_dx", d_p2, w2, extras=(dz1,), epilogue=lambda acc, d: (acc + alpha * d,), jobs=[pair(g_win)])
    sums_in = pair_sums(g_win, pr)
    grad_x, l_a, small_landed = _matmul("proj_ret_dx", d_p1, w1, extras=(dx_a,), epilogue=lambda acc, d: (acc + d,),
                                        jobs=[chip(sums_in, CHIP_RELS[:2], by_source=small)])

    moments = dict(zip(names, zip([m_w_in, m_w_uq, m_w_uk, m_w_uv, m_w_out, m_w_up, m_w_down],
                                  [v_w_in, v_w_uq, v_w_uk, v_w_uv, v_w_out, v_w_up, v_w_down])))
    res = {}

    def adamw(n, jobs=()):
        as_held = (lambda a: a[0].T) if n == "w_in" else (lambda a: a[0])
        as_given = (lambda r: r.T[None]) if n == "w_in" else (lambda r: r[None])
        out = _adamw("adamw_" + n, chip_sums[n], my_chip, landed[n], big[n], as_held(moments[n][0]), as_held(moments[n][1]),
                     jobs=jobs)
        res[n] = [as_given(r) for r in out[:4]]
        return out[4:]

    (l_b,) = adamw("w_out", jobs=[chip(sums_in, CHIP_RELS[2:])])
    landed["w_in"] = [l_a, l_b]
    for n in names:
        if n != "w_out":
            adamw(n)

    small_names = ["q_norm_g", "kv_norm_g", "ret_gn_g", "ln1_g", "ln1_b", "ln2_g", "ln2_b"]
    small_w = [q_norm_g, kv_norm_g, ret_gn_g, ln1_g, ln1_b, ln2_g, ln2_b]
    small_m = [m_q_norm_g, m_kv_norm_g, m_ret_gn_g, m_ln1_g, m_ln1_b, m_ln2_g, m_ln2_b]
    small_v = [v_q_norm_g, v_kv_norm_g, v_ret_gn_g, v_ln1_g, v_ln1_b, v_ln2_g, v_ln2_b]
    cat = lambda arrs: jnp.concatenate(arrs, axis=1)
    sres = _adamw_small(small_landed, cat(small_w), cat(small_m), cat(small_v))
    off = 0
    for n, w in zip(small_names, small_w):
        res[n] = [r[:, off:off + w.shape[1]] for r in sres]
        off += w.shape[1]

    order = ["w_in", "q_norm_g", "w_uq", "kv_norm_g", "w_uk", "w_uv", "ret_gn_g", "w_out", "ln1_g", "ln1_b",
             "w_up", "w_down", "ln2_g", "ln2_b"]
    outs = [loss, grad_x[None]]
    for k in range(4):
        outs += [res[n][k] for n in order]
    return tuple(outs)
```

```python
import functools

import numpy as np

import jax
import jax.numpy as jnp
from jax import lax
from jax.experimental import pallas as pl
from jax.experimental.pallas import tpu as pltpu

F32 = jnp.float32
BF16 = jnp.bfloat16

CHUNK = 64
RET_DK = 128
RET_DV = 128
MLA_NOPE = 128
MLA_ROPE = 64
MLA_DV = 128
MLA_Q_RANK = 768
MLA_KV_RANK = 512
MLA_HEAD_PAD = 256
ROPE_BASE = 10000.0
EPS = 1e-5
DEPTH = 1
DEEPNORM_ALPHA = (2.0 * DEPTH) ** 0.25
ADAM_LR = 0.001
ADAM_B1 = 0.9
ADAM_B2 = 0.999
ADAM_EPS = 1e-08
ADAM_WD = 0.01
ADAM_STEP = 10

N_DEV = 8
LANE = 128
SUBLANE = 8
VMEM_LIMIT = 48 * 1024 * 1024
ADAMW_TILE_ELEMS = 128 * 1024
PAIR_TILE_ELEMS = 1024 * 1024

MESH = pl.DeviceIdType.MESH
ANY = pl.BlockSpec(memory_space=pl.ANY)

NN = (((1,), (0,)), ((), ()))
NT = (((1,), (1,)), ((), ()))
TN = (((0,), (0,)), ((), ()))


def _dot(a, b, dims=NN):
    return lax.dot_general(a.astype(BF16), b.astype(BF16), dims, preferred_element_type=F32)


def _params(n_axes):
    return pltpu.CompilerParams(dimension_semantics=("arbitrary",) * n_axes, vmem_limit_bytes=VMEM_LIMIT)


def _tile(n, want):
    if n <= want:
        return n
    t = want
    while t >= LANE:
        if n % t == 0:
            return t
        t -= LANE
    return n


def _my_place():
    return lax.axis_index("x"), lax.axis_index("y"), lax.axis_index("c")


def _flat(px, py, pc):
    return 4 * px + 2 * py + pc


def _block_of(ref, kind, idx, rows, cols, sub=None):
    lo, n = (0, rows) if sub is None else sub
    if kind == "blk":
        return ref.at[idx, pl.ds(lo, n), :]
    if kind == "rows":
        return ref.at[pl.ds(pl.multiple_of(idx * rows + lo, n), n), :]
    return ref.at[pl.ds(lo, n), pl.ds(pl.multiple_of(idx * cols, cols), cols)]


CHIP_X, CHIP_Y, CHIP_DIAGONAL = 0, 1, 2
ALL_CHIPS = (CHIP_X, CHIP_Y, CHIP_DIAGONAL)


class _GatherJob:
    def __init__(self, shards, kinds, mid, late, pieces=(((0, 1), ALL_CHIPS, True),), base=None):
        nw = len(shards)
        self.nw, self.kinds = nw, list(kinds)
        self.pieces = [tuple(p) if len(p) == 4 else (*p, late) for p in pieces]
        for _, chips, _, _ in self.pieces:
            assert CHIP_DIAGONAL not in chips or len(chips) in (1, 3)
        self.shapes = [s.shape for s in shards]
        self.ins = list(shards) + (list(base) if base is not None else [])
        self.alias = [(nw + w, w) for w in range(nw)] if base is not None else []
        self.out_shape = []
        for s, kind in zip(shards, kinds):
            r, c = s.shape
            shape = {"blk": (N_DEV, r, c), "rows": (N_DEV * r, c), "cols": (r, N_DEV * c)}[kind]
            self.out_shape.append(jax.ShapeDtypeStruct(shape, s.dtype))
        n = 7 * nw * len(self.pieces)
        self.sems = [pltpu.SemaphoreType.DMA((n,)), pltpu.SemaphoreType.DMA((n,)), pltpu.SemaphoreType.DMA((n,))]
        whens = sorted({p[3] for p in self.pieces})
        self.phases = ([(0.0, self._start), (mid, self._relay)] + [(f, functools.partial(self._pass_on, f)) for f in whens]
                       + [(1.0, self._finish)])

    def _each(self, src, dst, send_sems, recv_sems, local_sems):
        x, y, c = _my_place()
        all_chips = [(1 - x, y), (x, 1 - y), (1 - x, 1 - y)]
        out = []
        for w in range(self.nw):
            r, cc = self.shapes[w]
            for p, ((k_of, of), chips, own, when) in enumerate(self.pieces):
                sub = (k_of * (r // of), r // of)
                s0 = 7 * (w * len(self.pieces) + p)
                mine = src[w].at[pl.ds(*sub), :]

                def place(dev, w=w, r=r, cc=cc, sub=sub):
                    return _block_of(dst[w], self.kinds[w], _flat(*dev), r, cc, sub)

                def copy(k, block, to, from_src=False, place=place, mine=mine, s0=s0):
                    return pltpu.make_async_remote_copy(
                        src_ref=mine if from_src else place(block), dst_ref=place(block),
                        send_sem=send_sems.at[s0 + k], recv_sem=recv_sems.at[s0 + k], device_id=to, device_id_type=MESH)

                relay = copy(1 + CHIP_DIAGONAL, (x ^ (1 - c), y ^ c, c), (x ^ c, y ^ (1 - c), c))
                local = pltpu.make_async_copy(mine, place((x, y, c)), local_sems.at[s0])
                near = [(j, all_chips[j]) for j in chips if j != CHIP_DIAGONAL]
                out.append((near, CHIP_DIAGONAL in chips, own, copy, relay, local, when))
        return (x, y, c), (x, y, 1 - c), all_chips[CHIP_DIAGONAL], out

    def _start(self, *refs):
        me, sibling, diagonal, each = self._each(*refs)
        for near, far, own, copy, relay, local, when in each:
            if own:
                local.start()
                copy(0, me, sibling, from_src=True).start()
            for j, chip in near:
                copy(1 + j, me, (*chip, me[2]), from_src=True).start()
            if far and not near:
                relay.start()

    def _landed(self, near, copy, me, sibling):
        for j, chip in near:
            copy(1 + j, (*chip, me[2]), me).wait_recv()
            copy(4 + j, (*chip, me[2]), sibling).start()

    def _relay(self, *refs):
        me, sibling, diagonal, each = self._each(*refs)
        for near, far, own, copy, relay, local, when in each:
            if far and near:
                self._landed(near, copy, me, sibling)
                relay.start()

    def _pass_on(self, frac, *refs):
        me, sibling, diagonal, each = self._each(*refs)
        for near, far, own, copy, relay, local, when in each:
            if when == frac:
                self._landed([(CHIP_DIAGONAL, diagonal)] if far else near, copy, me, sibling)

    def _finish(self, *refs):
        me, sibling, diagonal, each = self._each(*refs)
        for near, far, own, copy, relay, local, when in each:
            if own:
                copy(0, sibling, me).wait_recv()
            for j, chip in near + [(CHIP_DIAGONAL, diagonal)] * int(far):
                copy(4 + j, (*chip, 1 - me[2]), me).wait_recv()
        for near, far, own, copy, relay, local, when in each:
            if own:
                copy(0, me, sibling, from_src=True).wait_send()
                local.wait()
            for j, chip in near:
                copy(1 + j, me, (*chip, me[2]), from_src=True).wait_send()
                copy(4 + j, (*chip, me[2]), sibling).wait_send()
            if far:
                relay.wait_send()
                copy(4 + CHIP_DIAGONAL, (*diagonal, me[2]), sibling).wait_send()


REL_ALL = [(1, 0, 0), (1, 0, 1), (0, 1, 0), (0, 1, 1), (1, 1, 0), (1, 1, 1), (0, 0, 1)]
N_CHIP = 4


class _PairJob:
    def __init__(self, grads, kinds, shard_shapes):
        self.ins = list(grads)
        self.kinds, self.shard_shapes, self.nw = list(kinds), list(shard_shapes), len(grads)
        self.out_shape = [jax.ShapeDtypeStruct((N_CHIP,) + tuple(s), g.dtype) for g, s in zip(grads, shard_shapes)]
        n = N_CHIP * self.nw
        self.sems = [pltpu.SemaphoreType.DMA((n,)), pltpu.SemaphoreType.DMA((n,))]
        self.phases = [(0.0, self._start), (1.0, self._finish)]

    def _copies(self, src, theirs, send_sems, recv_sems):
        x, y, c = _my_place()
        remote = []
        for w in range(self.nw):
            r, cc = self.shard_shapes[w]
            for q in range(N_CHIP):
                s = N_CHIP * w + q
                remote.append(pltpu.make_async_remote_copy(
                    src_ref=_block_of(src[w], self.kinds[w], 2 * q + (1 - c), r, cc), dst_ref=theirs[w].at[q],
                    send_sem=send_sems.at[s], recv_sem=recv_sems.at[s], device_id=(x, y, 1 - c), device_id_type=MESH))
        return remote

    def _start(self, *refs):
        for cp in self._copies(*refs):
            cp.start()

    def _finish(self, *refs):
        for cp in self._copies(*refs):
            cp.wait()


CHIP_RELS = [(1, 0), (0, 1), (1, 1)]


class _ChipJob:
    def __init__(self, chip_sums, shard_shapes, rels=CHIP_RELS, by_source=None, piece=(0, 1), base=None):
        nw = len(chip_sums)
        self.ins = list(chip_sums) + ([by_source] if by_source is not None else []) + (list(base) if base is not None else [])
        self.nw, self.has_src, self.rels, self.piece = nw, by_source is not None, list(rels), piece
        self.rows = [s[0] for s in shard_shapes]
        self.alias = [(nw + int(self.has_src) + w, w) for w in range(nw)] if base is not None else []
        self.out_shape = [jax.ShapeDtypeStruct((len(rels),) + tuple(s), a.dtype) for a, s in zip(chip_sums, shard_shapes)]
        if self.has_src:
            self.out_shape.append(jax.ShapeDtypeStruct((N_DEV,) + by_source.shape, by_source.dtype))
        n = len(rels) * self.nw + len(REL_ALL) * int(self.has_src)
        self.sems = [pltpu.SemaphoreType.DMA((n,)), pltpu.SemaphoreType.DMA((n,)), pltpu.SemaphoreType.DMA((1,))]
        self.phases = [(0.0, self._start), (1.0, self._finish)]

    def _copies(self, src, land, send_sems, recv_sems, local_sems):
        x, y, c = _my_place()
        nw = self.nw

        def remote_copy(s, src_ref, dst_ref, peer):
            return pltpu.make_async_remote_copy(src_ref=src_ref, dst_ref=dst_ref, send_sem=send_sems.at[s],
                                                recv_sem=recv_sems.at[s], device_id=peer, device_id_type=MESH)

        def rows(w):
            n = self.rows[w] // self.piece[1]
            return pl.ds(self.piece[0] * n, n)

        local, remote = [], []
        for k, (fx, fy) in enumerate(self.rels):
            px, py = x ^ fx, y ^ fy
            remote += [remote_copy(nw * k + w, src[w].at[2 * px + py, rows(w), :], land[w].at[k, rows(w), :], (px, py, c))
                       for w in range(nw)]
        if self.has_src:
            my_idx = _flat(x, y, c)
            local.append(pltpu.make_async_copy(src[nw], land[nw].at[my_idx], local_sems.at[0]))
            remote += [remote_copy(nw * len(self.rels) + k, src[nw], land[nw].at[my_idx], (x ^ fx, y ^ fy, c ^ fc))
                       for k, (fx, fy, fc) in enumerate(REL_ALL)]
        return local, remote

    def _start(self, *refs):
        local, remote = self._copies(*refs)
        for cp in local + remote:
            cp.start()

    def _finish(self, *refs):
        local, remote = self._copies(*refs)
        for cp in remote + local:
            cp.wait()


def _row_tile(R, C, elems=ADAMW_TILE_ELEMS):
    tr = 16
    while tr * 2 * C <= elems and R % (tr * 2) == 0:
        tr *= 2
    return tr if R % tr == 0 else R


BF16_ROWS = 16


def _tile_2d(R, C, elems):
    if R % BF16_ROWS == 0:
        tr = _row_tile(R, C, elems)
        return tr, C, R // tr
    tc = C
    while tc % (2 * LANE) == 0 and R * tc > elems:
        tc //= 2
    return R, tc, C // tc


def _tile_index(tr, tc, C):
    return (lambda i: (i, 0)) if tc == C else (lambda i: (0, i))


def _part_spec(kind, tr, tc, R, C, n_lead):
    at = _tile_index(tr, tc, C)

    def dev(ids, pref):
        return 2 * ids[0] + pref[0] if n_lead == 2 else pref[0]

    def index(*args):
        ids, pref = args[:n_lead], args[n_lead]
        ri, ci = at(ids[-1])
        if kind == "blk":
            return dev(ids, pref), ri, ci
        if kind == "rows":
            return dev(ids, pref) * (R // tr) + ri, ci
        return ri, dev(ids, pref) * (C // tc) + ci

    return pl.BlockSpec((None, tr, tc) if kind == "blk" else (tr, tc), index)


def _pair_sum(name, grad, kind, shard_shape, theirs, c):
    R, C = shard_shape
    tr, tc, tiles = _tile_2d(R, C, PAIR_TILE_ELEMS)
    at = _tile_index(tr, tc, C)

    def body(c_ref, g_ref, t_ref, o_ref):
        o_ref[...] = (g_ref[...].astype(F32) + t_ref[...].astype(F32)).astype(o_ref.dtype)

    slot = pl.BlockSpec((None, tr, tc), lambda q, i, c_ref: (q, *at(i)))
    return pl.pallas_call(
        body, name=name,
        grid_spec=pltpu.PrefetchScalarGridSpec(
            num_scalar_prefetch=1, grid=(N_CHIP, tiles),
            in_specs=[_part_spec(kind, tr, tc, R, C, 2), slot], out_specs=slot),
        out_shape=jax.ShapeDtypeStruct(theirs.shape, theirs.dtype), compiler_params=_params(2))(c, grad, theirs)


class _Carried:
    def __init__(self, jobs, grid):
        self.jobs = list(jobs)
        self.steps = int(np.prod(grid))
        self.grid = tuple(grid)
        self.operands = [a for j in self.jobs for a in j.ins]
        self.out_shape = [o for j in self.jobs for o in j.out_shape]
        self.sems = [s for j in self.jobs for s in j.sems]
        self.in_specs = [ANY] * len(self.operands)
        self.out_specs = [ANY] * len(self.out_shape)

    def aliases(self, n_in, n_out):
        out, i0, o0 = {}, n_in, n_out
        for job in self.jobs:
            for ji, jo in getattr(job, "alias", []):
                out[i0 + ji] = o0 + jo
            i0, o0 = i0 + len(job.ins), o0 + len(job.out_shape)
        return out

    def _step(self):
        lin = pl.program_id(0)
        for a in range(1, len(self.grid)):
            lin = lin * self.grid[a] + pl.program_id(a)
        return lin

    def _run(self, in_refs, out_refs, sem_refs, last):
        lin = self._step()
        for job in self.jobs:
            ins, in_refs = in_refs[:len(job.ins)], in_refs[len(job.ins):]
            outs, out_refs = out_refs[:len(job.out_shape)], out_refs[len(job.out_shape):]
            sems, sem_refs = sem_refs[:len(job.sems)], sem_refs[len(job.sems):]
            for frac, fn in job.phases:
                if (frac >= 1.0) != last:
                    continue
                at = min(self.steps - 1, int(frac * (self.steps - 1)))
                pl.when(lin == at)(functools.partial(fn, ins, outs, *sems))

    def before(self, in_refs, out_refs, sem_refs):
        self._run(in_refs, out_refs, sem_refs, last=False)

    def after(self, in_refs, out_refs, sem_refs):
        self._run(in_refs, out_refs, sem_refs, last=True)


def _split_refs(refs, n_in, n_job_in, n_out, n_job_out, n_scratch):
    cuts = np.cumsum([0, n_in, n_job_in, n_out, n_job_out, n_scratch])
    return [refs[cuts[i]:cuts[i + 1]] for i in range(5)] + [refs[cuts[5]:]]


def _matmul(name, a, b, *, ta=False, tb=False, out_dtypes=(F32,), epilogue=None, extras=(), tm=1024, tn=1024, tk=2048, jobs=(),
            n=None):
    if ta:
        K, M = a.shape
    else:
        M, K = a.shape
    N = n if n is not None else (b.shape[0] if tb else b.shape[1])
    tm, tn, tk = _tile(M, tm), _tile(N, tn), _tile(K, tk)
    nk = K // tk
    grid = (M // tm, N // tn, nk)
    dims = TN if ta else (NT if tb else NN)
    n_ex, n_out = len(extras), len(out_dtypes)
    carried = _Carried(jobs, grid)

    def body(*refs):
        ins, job_ins, out_refs, job_outs, scratch, job_sems = _split_refs(
            refs, 2 + n_ex, len(carried.operands), n_out, len(carried.out_shape), int(nk > 1))
        a_ref, b_ref, ex_refs = ins[0], ins[1], ins[2:]
        k = pl.program_id(2)
        carried.before(job_ins, job_outs, job_sems)

        def finish(acc):
            res = (acc,) if epilogue is None else epilogue(acc, *[e[...] for e in ex_refs])
            for o_ref, r in zip(out_refs, res):
                o_ref[...] = r.astype(o_ref.dtype)

        if nk == 1:
            finish(_dot(a_ref[...], b_ref[...], dims))
        else:
            acc_ref = scratch[0]

            @pl.when(k == 0)
            def _():
                acc_ref[...] = _dot(a_ref[...], b_ref[...], dims)

            @pl.when(k > 0)
            def _():
                acc_ref[...] += _dot(a_ref[...], b_ref[...], dims)

            pl.when(k == nk - 1)(lambda: finish(acc_ref[...]))

        carried.after(job_ins, job_outs, job_sems)

    a_spec = pl.BlockSpec((tk, tm), lambda i, j, k: (k, i)) if ta else pl.BlockSpec((tm, tk), lambda i, j, k: (i, k))
    b_spec = pl.BlockSpec((tn, tk), lambda i, j, k: (j, k)) if tb else pl.BlockSpec((tk, tn), lambda i, j, k: (k, j))
    tile_spec = pl.BlockSpec((tm, tn), lambda i, j, k: (i, j))
    outs = pl.pallas_call(
        body,
        name=name,
        grid=grid,
        in_specs=[a_spec, b_spec] + [tile_spec] * n_ex + carried.in_specs,
        out_specs=[tile_spec] * n_out + carried.out_specs,
        out_shape=[jax.ShapeDtypeStruct((M, N), d) for d in out_dtypes] + carried.out_shape,
        scratch_shapes=[pltpu.VMEM((tm, tn), F32)] * int(nk > 1) + carried.sems,
        input_output_aliases=carried.aliases(2 + n_ex, n_out),
        compiler_params=_params(3),
    )(a, b, *extras, *carried.operands)
    return outs[0] if len(outs) == 1 else outs


def _rowwise(name, fn, row_in, const_in, row_out, acc_out, tm, jobs=()):
    T = row_in[0][0].shape[0]
    n_r, n_c, n_o, n_a = len(row_in), len(const_in), len(row_out), len(acc_out)
    carried = _Carried(jobs, (T // tm,))

    def body(*refs):
        in_refs, job_ins, own_outs, job_outs, _, job_sems = _split_refs(
            refs, n_r + n_c, len(carried.operands), n_o + n_a, len(carried.out_shape), 0)
        out_refs, acc_refs = own_outs[:n_o], own_outs[n_o:]
        carried.before(job_ins, job_outs, job_sems)
        res = fn(*[r[...] for r in in_refs])
        for o_ref, r in zip(out_refs, res[:n_o]):
            o_ref[...] = r.astype(o_ref.dtype)

        @pl.when(pl.program_id(0) == 0)
        def _():
            for a_ref in acc_refs:
                a_ref[...] = jnp.zeros_like(a_ref)

        for a_ref, r in zip(acc_refs, res[n_o:]):
            a_ref[...] += jnp.sum(r.reshape(tm // SUBLANE, SUBLANE, r.shape[-1]), axis=0)
        carried.after(job_ins, job_outs, job_sems)

    in_specs = [pl.BlockSpec((tm, w), functools.partial(lambda i, c: (i, c), c=cb)) for (_, w, cb) in row_in]
    in_specs += [pl.BlockSpec(c.shape, functools.partial(lambda i, nd: (0,) * nd, nd=c.ndim)) for c in const_in]
    out_specs = [pl.BlockSpec((tm, c), lambda i: (i, 0)) for (c, _) in row_out]
    out_specs += [pl.BlockSpec((SUBLANE, c), lambda i: (0, 0)) for c in acc_out]
    out_shape = [jax.ShapeDtypeStruct((T, c), d) for (c, d) in row_out]
    out_shape += [jax.ShapeDtypeStruct((SUBLANE, c), F32) for c in acc_out]
    return pl.pallas_call(
        body, name=name, grid=(T // tm,), in_specs=in_specs + carried.in_specs, out_specs=out_specs + carried.out_specs,
        out_shape=out_shape + carried.out_shape, scratch_shapes=carried.sems,
        input_output_aliases=carried.aliases(n_r + n_c, n_o + n_a), compiler_params=_params(1),
    )(*[r[0] for r in row_in], *const_in, *carried.operands)


def _ln_stats(z):
    mu = jnp.mean(z, axis=-1, keepdims=True)
    var = jnp.mean(jnp.square(z - mu), axis=-1, keepdims=True)
    rstd = lax.rsqrt(var + EPS)
    return (z - mu) * rstd, rstd


def _ln_bwd(dy, xhat, rstd, g):
    dxh = dy * g
    return rstd * (dxh - jnp.mean(dxh, axis=-1, keepdims=True) - xhat * jnp.mean(dxh * xhat, axis=-1, keepdims=True))


def _rms(x, g):
    return x * lax.rsqrt(jnp.mean(jnp.square(x), axis=-1, keepdims=True) + EPS) * g


def _rope128(x, cos2, sin2):
    return x * cos2 + pltpu.roll(x, RET_DK // 2, 1) * sin2


def _unrope128(dy, cos2, sin2):
    return dy * cos2 + pltpu.roll(dy * sin2, RET_DK // 2, 1)


def _rope64(x, cosp, sina, sinb):
    h = MLA_ROPE // 2
    return x * cosp + pltpu.roll(x, h, 1) * sina + pltpu.roll(x, LANE - h, 1) * sinb


def _unrope64(dy, cosp, sina, sinb):
    h = MLA_ROPE // 2
    return dy * cosp + pltpu.roll(dy * sina, LANE - h, 1) + pltpu.roll(dy * sinb, h, 1)


RET_BLOCK = 256
RET_HEADS_PER_STEP = 4


def _ret_tables(H, B):
    log_g = np.log1p(-np.exp2(-5.0 - np.arange(H, dtype=np.float32))).astype(np.float32)
    idx = np.arange(B, dtype=np.float32)
    dist = np.abs(idx[:, None] - idx[None, :])
    visible = (np.arange(B)[None, :] // CHUNK) <= (np.arange(B)[:, None] // CHUNK)
    dmat = np.exp(log_g[:, None, None] * dist).astype(np.float32) * visible[None].astype(np.float32)
    qd = np.exp(log_g[:, None] * (idx + 1.0)).astype(np.float32)
    kd = np.exp(log_g[:, None] * (B - 1 - idx)).astype(np.float32)
    sd = np.exp(log_g * B).astype(np.float32)
    bc = lambda v: np.ascontiguousarray(np.broadcast_to(v[:, :, None], (H, v.shape[1], LANE)))
    sdb = np.ascontiguousarray(np.broadcast_to(sd[:, None, None], (H, SUBLANE, LANE)))
    return jnp.asarray(dmat), jnp.asarray(bc(qd)), jnp.asarray(bc(kd)), jnp.asarray(sdb)


def _ret_specs(H, B, reverse, nb):
    blk = (lambda b: nb - 1 - b) if reverse else (lambda b: b)
    G = min(RET_HEADS_PER_STEP, H)
    W = G * LANE
    col = lambda off: pl.BlockSpec((B, W), functools.partial(lambda h, b, off: (blk(b), off // G + h), off=off))
    row = pl.BlockSpec((B, LANE), lambda h, b: (blk(b), 0))
    per_head = lambda r: pl.BlockSpec((G, r, LANE), lambda h, b: (h, 0, 0))
    dmat = pl.BlockSpec((G, B, B), lambda h, b: (h, 0, 0))
    return G, col, row, per_head, dmat


def _ret_call(name, body, n_in, n_out, operands, in_specs, out_specs, out_shape, grid, G, jobs, own_alias=None):
    carried = _Carried(jobs, grid)

    def wrapped(*refs):
        ins, job_ins, outs, job_outs, (state,), job_sems = _split_refs(
            refs, n_in, len(carried.operands), n_out, len(carried.out_shape), 1)
        carried.before(job_ins, job_outs, job_sems)

        @pl.when(pl.program_id(1) == 0)
        def _():
            state[...] = jnp.zeros_like(state)

        body(*ins, *outs, state)
        carried.after(job_ins, job_outs, job_sems)

    return pl.pallas_call(
        wrapped, name=name, grid=grid,
        in_specs=in_specs + carried.in_specs, out_specs=out_specs + carried.out_specs,
        out_shape=out_shape + carried.out_shape,
        scratch_shapes=[pltpu.VMEM((G, RET_DK, RET_DV), F32)] + carried.sems,
        input_output_aliases={**(own_alias or {}), **carried.aliases(n_in, n_out)},
        compiler_params=_params(2),
    )(*operands, *carried.operands)


def _retention_fwd(p1, cos2, sin2, gn_g, mix_in, H, jobs=()):
    T = p1.shape[0]
    B = min(RET_BLOCK, T)
    nb = T // B
    scale = RET_DK ** -0.5
    dmat, qd, kd, sd = _ret_tables(H, B)
    G, col, row, per_head, dspec = _ret_specs(H, B, False, nb)

    def body(q_ref, k_ref, v_ref, g_ref, cos_ref, sin_ref, d_ref, qd_ref, kd_ref, sd_ref, gn_ref, _, pre_ref, out_ref, s_ref):
        cos, sin = cos_ref[...], sin_ref[...]
        for g in range(G):
            hl = pl.ds(g * LANE, LANE)
            q = _rope128(q_ref[:, hl], cos, sin)
            k = _rope128(k_ref[:, hl], cos, sin) * scale
            v = v_ref[:, hl]
            s = s_ref[g]
            a = _dot(q, k, NT) * d_ref[g]
            ret = _dot(a, v) + _dot(q * qd_ref[g], s)
            s_ref[g] = sd_ref[g, 0:1, :] * s + _dot(k * kd_ref[g], v, TN)
            pre_ref[:, hl] = ret
            nrm, _ = _ln_stats(ret)
            out_ref[:, hl] = (jax.nn.silu(g_ref[:, hl]) * (nrm * gn_ref[:, hl])).astype(out_ref.dtype)

    out_spec = pl.BlockSpec((B, G * LANE), lambda h, b: (b, h))
    return _ret_call(
        "retention_fwd", body, 12, 2, (p1, p1, p1, p1, cos2, sin2, dmat, qd, kd, sd, gn_g, mix_in),
        [col(0), col(H), col(2 * H), col(3 * H), row, row, dspec, per_head(B), per_head(B), per_head(SUBLANE),
         pl.BlockSpec((1, G * LANE), lambda h, b: (0, h)), ANY],
        [out_spec, out_spec],
        [jax.ShapeDtypeStruct((T, H * RET_DV), F32), jax.ShapeDtypeStruct(mix_in.shape, mix_in.dtype)],
        (H // G, nb), G, jobs, own_alias={11: 1})


def _retention_bwd(p1, dpre, cos2, sin2, H, dq_jobs=(), dkv_jobs=()):
    T = p1.shape[0]
    B = min(RET_BLOCK, T)
    nb = T // B
    scale = RET_DK ** -0.5
    dmat, qd, kd, sd = _ret_tables(H, B)

    def dq_body(k_ref, v_ref, do_ref, cos_ref, sin_ref, d_ref, qd_ref, kd_ref, sd_ref, dq_ref, s_ref):
        cos, sin = cos_ref[...], sin_ref[...]
        for g in range(G):
            hl = pl.ds(g * LANE, LANE)
            k = _rope128(k_ref[:, hl], cos, sin) * scale
            v, do, s = v_ref[:, hl], do_ref[:, hl], s_ref[g]
            da = _dot(do, v, NT) * d_ref[g]
            dq = _dot(da, k) + _dot(do, s, NT) * qd_ref[g]
            dq_ref[:, hl] = _unrope128(dq, cos, sin).astype(dq_ref.dtype)
            s_ref[g] = sd_ref[g, 0:1, :] * s + _dot(k * kd_ref[g], v, TN)

    G, col, row, per_head, dspec = _ret_specs(H, B, False, nb)
    dq = _ret_call(
        "retention_bwd_dq", dq_body, 9, 1, (p1, p1, dpre, cos2, sin2, dmat, qd, kd, sd),
        [col(H), col(2 * H), col(0), row, row, dspec, per_head(B), per_head(B), per_head(SUBLANE)],
        [pl.BlockSpec((B, G * LANE), lambda h, b: (b, h))], [jax.ShapeDtypeStruct((T, H * RET_DK), BF16)],
        (H // G, nb), G, dq_jobs)

    def dkv_body(q_ref, k_ref, v_ref, do_ref, cos_ref, sin_ref, d_ref, qd_ref, kd_ref, sd_ref, dk_ref, dv_ref, g_ref):
        cos, sin = cos_ref[...], sin_ref[...]
        for g in range(G):
            hl = pl.ds(g * LANE, LANE)
            q = _rope128(q_ref[:, hl], cos, sin)
            k = _rope128(k_ref[:, hl], cos, sin) * scale
            v, do, st = v_ref[:, hl], do_ref[:, hl], g_ref[g]
            a = _dot(q, k, NT) * d_ref[g]
            da = _dot(do, v, NT) * d_ref[g]
            dv = _dot(a, do, TN) + _dot(k * kd_ref[g], st)
            dk = _dot(da, q, TN) + _dot(v, st, NT) * kd_ref[g]
            dv_ref[:, hl] = dv.astype(dv_ref.dtype)
            dk_ref[:, hl] = _unrope128(dk * scale, cos, sin).astype(dk_ref.dtype)
            g_ref[g] = sd_ref[g, 0:1, :] * st + _dot(q * qd_ref[g], do, TN)

    G, col, row, per_head, dspec = _ret_specs(H, B, True, nb)
    out_spec = pl.BlockSpec((B, G * LANE), lambda h, b: (nb - 1 - b, h))
    dkv = _ret_call(
        "retention_bwd_dkv", dkv_body, 10, 2, (p1, p1, p1, dpre, cos2, sin2, dmat, qd, kd, sd),
        [col(0), col(H), col(2 * H), col(0), row, row, dspec, per_head(B), per_head(B), per_head(SUBLANE)],
        [out_spec, out_spec],
        [jax.ShapeDtypeStruct((T, H * RET_DK), BF16), jax.ShapeDtypeStruct((T, H * RET_DV), BF16)],
        (H // G, nb), G, dkv_jobs)
    return dq, dkv


ATT_BLOCK = 512


ATT_ROW_CHUNKS = 2
ATT_HEADS_PER_STEP = 8
ATT_BWD_HEADS_PER_STEP = 2
LOG2E = 1.4426950408889634


def _chunk_mask(rows, row0, cols):
    qi = (lax.broadcasted_iota(jnp.int32, (rows, cols), 0) + row0) // CHUNK
    kj = lax.broadcasted_iota(jnp.int32, (rows, cols), 1) // CHUNK
    return kj <= qi


def _attention_fwd(qh, kh, vx, H, jobs=()):
    T = qh.shape[0]
    t = min(ATT_BLOCK, T)
    n = T // t
    r = t // ATT_ROW_CHUNKS
    G = min(ATT_HEADS_PER_STEP, H)
    P = MLA_HEAD_PAD
    scale = (MLA_NOPE + MLA_ROPE) ** -0.5
    c2 = scale * LOG2E
    carried = _Carried(jobs, (H // G, n, n))

    def body(*refs):
        (q_ref, k_ref, v_ref), job_ins, (o_ref, ob_ref, lse_ref), job_outs, (m_ref, acc_ref), job_sems = _split_refs(
            refs, 3, len(carried.operands), 3, len(carried.out_shape), 2)
        i, j = pl.program_id(1), pl.program_id(2)
        carried.before(job_ins, job_outs, job_sems)

        @pl.when(j == 0)
        def _():
            m_ref[...] = jnp.full_like(m_ref, -jnp.inf)
            acc_ref[...] = jnp.zeros_like(acc_ref)

        def update(masked):
            for g in range(G):
                hc = pl.ds(g * P, P)
                for ch in range(ATT_ROW_CHUNKS):
                    rs = pl.ds(ch * r, r)
                    kc = (ch + 1) * r if masked else t
                    ks = pl.ds(0, kc)
                    s = _dot(q_ref[rs, hc], k_ref[ks, hc], NT)
                    if masked:
                        s = jnp.where(_chunk_mask(r, ch * r, kc), s, -jnp.inf)
                    m_prev = m_ref[rs, pl.ds(g * LANE, LANE)]
                    m_new = jnp.maximum(m_prev, jnp.max(s, axis=-1, keepdims=True))
                    alpha = jnp.exp2((m_prev - m_new) * c2)
                    p = jnp.exp2((s - jnp.tile(m_new, (1, kc // LANE))) * c2)
                    acc_ref[rs, hc] = jnp.tile(alpha, (1, 2)) * acc_ref[rs, hc] + _dot(p, v_ref[ks, hc])
                    m_ref[rs, pl.ds(g * LANE, LANE)] = m_new

        pl.when(j < i)(functools.partial(update, False))
        pl.when(j == i)(functools.partial(update, True))

        @pl.when(j == n - 1)
        def _():
            for g in range(G):
                acc = acc_ref[:, pl.ds(g * P, P)]
                l = acc[:, MLA_DV:]
                o = acc[:, :MLA_DV] / l
                o_ref[:, pl.ds(g * MLA_DV, MLA_DV)] = o
                ob_ref[:, pl.ds(g * MLA_DV, MLA_DV)] = o.astype(ob_ref.dtype)
                lse_ref[g] = m_ref[:, pl.ds(g * LANE, LANE)] * scale + jnp.log(l)

        carried.after(job_ins, job_outs, job_sems)

    kclamp = lambda i, j: jnp.minimum(j, i)
    o_spec = pl.BlockSpec((t, G * MLA_DV), lambda h, i, j: (i, h))
    mix_spec = pl.BlockSpec((t, G * MLA_DV), lambda h, i, j: (i, H // G + h))
    return pl.pallas_call(
        body, name="attention_fwd", grid=(H // G, n, n),
        in_specs=[pl.BlockSpec((t, G * P), lambda h, i, j: (i, h)),
                  pl.BlockSpec((t, G * P), lambda h, i, j: (kclamp(i, j), h)),
                  pl.BlockSpec((t, G * P), lambda h, i, j: (kclamp(i, j), h))] + carried.in_specs,
        out_specs=[o_spec, mix_spec, pl.BlockSpec((G, t, LANE), lambda h, i, j: (h, i, 0))] + carried.out_specs,
        out_shape=[jax.ShapeDtypeStruct((T, H * MLA_DV), F32), jax.ShapeDtypeStruct((T, 2 * H * MLA_DV), BF16),
                   jax.ShapeDtypeStruct((H, T, LANE), F32)] + carried.out_shape,
        scratch_shapes=[pltpu.VMEM((t, G * LANE), F32), pltpu.VMEM((t, G * P), F32)] + carried.sems,
        input_output_aliases=carried.aliases(3, 3),
        compiler_params=_params(3),
    )(qh, kh, vx, *carried.operands)


def _attention_bwd(qh, kh, vx, o, lse, dmix, H, jobs=()):
    T = qh.shape[0]
    t = min(ATT_BLOCK, T)
    n = T // t
    r = t // ATT_ROW_CHUNKS
    G = min(ATT_BWD_HEADS_PER_STEP, H)
    P = MLA_HEAD_PAD
    scale = (MLA_NOPE + MLA_ROPE) ** -0.5
    c2 = scale * LOG2E
    carried = _Carried(jobs, (H // G, n, n))

    def body(*refs):
        ((q_ref, k_ref, v_ref, do_ref, o_ref, lse_ref), job_ins, (dq_ref, dkn_ref, dkr_ref, dv_ref), job_outs,
         (dk_acc, dv_acc), job_sems) = _split_refs(refs, 6, len(carried.operands), 4, len(carried.out_shape), 2)
        j, i = pl.program_id(1), pl.program_id(2)
        carried.before(job_ins, job_outs, job_sems)

        @pl.when(jnp.logical_and(j == 0, i == 0))
        def _():
            dq_ref[...] = jnp.zeros_like(dq_ref)

        @pl.when(i == j)
        def _():
            dk_acc[...] = jnp.zeros_like(dk_acc)
            dv_acc[...] = jnp.zeros_like(dv_acc)

        def update(masked):
            for g in range(G):
                hc, hv = pl.ds(g * P, P), pl.ds(g * MLA_DV, MLA_DV)
                for ch in range(ATT_ROW_CHUNKS):
                    rs = pl.ds(ch * r, r)
                    kc = (ch + 1) * r if masked else t
                    ks = pl.ds(0, kc)
                    k, v = k_ref[ks, hc], v_ref[ks, pl.ds(g * P, MLA_DV)]
                    q, do = q_ref[rs, hc], do_ref[rs, hv]
                    s = _dot(q, k, NT)
                    if masked:
                        s = jnp.where(_chunk_mask(r, ch * r, kc), s, -jnp.inf)
                    p = jnp.exp2(s * c2 - jnp.tile(lse_ref[g, rs, :] * LOG2E, (1, kc // LANE)))
                    dv_acc[ks, hv] += _dot(p, do, TN)
                    dp = _dot(do, v, NT)
                    delta = jnp.sum(do * o_ref[rs, hv], axis=-1, keepdims=True)
                    ds = p * (dp - delta)
                    rows = pl.ds(pl.multiple_of(i * t, t) + ch * r, r)
                    dq_ref[rows, hc] += _dot(ds, k)
                    dk_acc[ks, hc] += _dot(ds, q, TN)

        pl.when(i > j)(functools.partial(update, False))
        pl.when(i == j)(functools.partial(update, True))

        @pl.when(i == n - 1)
        def _():
            for g in range(G):
                dk = dk_acc[:, pl.ds(g * P, P)] * scale
                dkn_ref[:, pl.ds(g * LANE, LANE)] = dk[:, :MLA_NOPE].astype(dkn_ref.dtype)
                dkr_ref[:, pl.ds(g * LANE, LANE)] = dk[:, MLA_NOPE:]
            dv_ref[...] = dv_acc[...].astype(dv_ref.dtype)

        carried.after(job_ins, job_outs, job_sems)

    qclamp = lambda j, i: jnp.maximum(i, j)
    kv_out = pl.BlockSpec((t, G * LANE), lambda h, j, i: (j, h))
    q_in = pl.BlockSpec((t, G * MLA_DV), lambda h, j, i: (qclamp(j, i), h))
    return pl.pallas_call(
        body, name="attention_bwd", grid=(H // G, n, n),
        in_specs=[pl.BlockSpec((t, G * P), lambda h, j, i: (qclamp(j, i), h)),
                  pl.BlockSpec((t, G * P), lambda h, j, i: (j, h)),
                  pl.BlockSpec((t, G * P), lambda h, j, i: (j, h)),
                  pl.BlockSpec((t, G * MLA_DV), lambda h, j, i: (qclamp(j, i), H // G + h)),
                  q_in,
                  pl.BlockSpec((G, t, LANE), lambda h, j, i: (h, qclamp(j, i), 0))] + carried.in_specs,
        out_specs=[pl.BlockSpec((T, G * P), lambda h, j, i: (0, h)), kv_out, kv_out, kv_out] + carried.out_specs,
        out_shape=[jax.ShapeDtypeStruct((T, H * P), F32), jax.ShapeDtypeStruct((T, H * LANE), BF16),
                   jax.ShapeDtypeStruct((T, H * LANE), F32), jax.ShapeDtypeStruct((T, H * LANE), BF16)] + carried.out_shape,
        scratch_shapes=[pltpu.VMEM((t, G * P), F32), pltpu.VMEM((t, G * MLA_DV), F32)] + carried.sems,
        input_output_aliases=carried.aliases(6, 4),
        compiler_params=_params(3),
    )(qh, kh, vx, dmix, o, lse, *carried.operands)


def _adamw_math(w, g, m, v):
    m = ADAM_B1 * m + (1.0 - ADAM_B1) * g
    v = ADAM_B2 * v + (1.0 - ADAM_B2) * jnp.square(g)
    m_hat = m / (1.0 - ADAM_B1 ** ADAM_STEP)
    v_hat = v / (1.0 - ADAM_B2 ** ADAM_STEP)
    delta = -ADAM_LR * (m_hat / (jnp.sqrt(v_hat) + ADAM_EPS) + ADAM_WD * w)
    return delta, m, v


def _adamw(name, chip_sums, my_chip, landed, w, m, v, jobs=()):
    R, C = w.shape
    tr, tc, tiles = _tile_2d(R, C, ADAMW_TILE_ELEMS)
    at = _tile_index(tr, tc, C)
    nl = len(landed)
    assert sum(l.shape[0] for l in landed) == N_CHIP - 1
    carried = _Carried(jobs, (tiles,))

    def body(q_ref, *refs):
        (own_ref, *rest), job_ins, (g_out, d_out, m_out, v_out), job_outs, _, job_sems = _split_refs(
            refs, nl + 4, len(carried.operands), 4, len(carried.out_shape), 0)
        l_refs, (w_ref, m_ref, v_ref) = rest[:nl], rest[nl:]
        carried.before(job_ins, job_outs, job_sems)
        g = own_ref[...].astype(F32)
        for l_ref in l_refs:
            for s in range(l_ref.shape[0]):
                g = g + l_ref[s].astype(F32)
        delta, mn, vn = _adamw_math(w_ref[...], g, m_ref[...], v_ref[...])
        g_out[...] = g
        d_out[...] = delta
        m_out[...] = mn
        v_out[...] = vn
        carried.after(job_ins, job_outs, job_sems)

    spec = pl.BlockSpec((tr, tc), lambda i, q_ref: at(i))
    own_spec = pl.BlockSpec((None, tr, tc), lambda i, q_ref: (q_ref[0], *at(i)))
    l_specs = [pl.BlockSpec((l.shape[0], tr, tc), lambda i, q_ref: (0, *at(i))) for l in landed]
    return pl.pallas_call(
        body, name=name,
        grid_spec=pltpu.PrefetchScalarGridSpec(
            num_scalar_prefetch=1, grid=(tiles,),
            in_specs=[own_spec] + l_specs + [spec, spec, spec] + carried.in_specs,
            out_specs=[spec] * 4 + carried.out_specs, scratch_shapes=carried.sems),
        out_shape=[jax.ShapeDtypeStruct((R, C), F32)] * 4 + carried.out_shape,
        input_output_aliases={1 + i: o for i, o in carried.aliases(nl + 4, 4).items()},
        compiler_params=_params(1),
    )(my_chip, chip_sums, *landed, w, m, v, *carried.operands)


def _adamw_small(landed, w, m, v):
    P = w.shape[1]

    def body(l_ref, w_ref, m_ref, v_ref, g_out, d_out, m_out, v_out):
        acc = l_ref[0]
        for s in range(1, N_DEV):
            acc = acc + l_ref[s]
        g = jnp.sum(acc, axis=0, keepdims=True)
        delta, mn, vn = _adamw_math(w_ref[...], g, m_ref[...], v_ref[...])
        g_out[...] = g
        d_out[...] = delta
        m_out[...] = mn
        v_out[...] = vn

    return pl.pallas_call(
        body, name="adamw_replicated", out_shape=[jax.ShapeDtypeStruct((1, P), F32)] * 4,
    )(landed, w, m, v)


def _rope_rows():
    inv_r = ROPE_BASE ** (-jnp.arange(0, RET_DK, 2, dtype=F32) / RET_DK)
    inv_m = ROPE_BASE ** (-jnp.arange(0, MLA_ROPE, 2, dtype=F32) / MLA_ROPE)
    h = MLA_ROPE // 2
    one, zero = jnp.ones((h,), F32), jnp.zeros((h,), F32)
    pad = jnp.zeros((LANE - MLA_ROPE,), F32)
    rows = [jnp.concatenate([inv_r, inv_r]), jnp.concatenate([-jnp.ones((RET_DK // 2,), F32), jnp.ones((RET_DK // 2,), F32)]),
            jnp.concatenate([inv_m, inv_m, pad]), jnp.concatenate([one, one, pad]),
            jnp.concatenate([zero, one, pad]), jnp.concatenate([-one, zero, pad])]
    return jnp.stack(rows + [jnp.zeros((LANE,), F32)] * (SUBLANE - len(rows)))


def _prep_tile(xt, p, rows):
    ang2, angp = p * rows[0:1], p * rows[2:3]
    sp = jnp.sin(angp)
    return (xt, jnp.cos(ang2), jnp.sin(ang2) * rows[1:2], jnp.cos(angp) * rows[3:4], sp * rows[4:5], sp * rows[5:6])


def kernel(x, positions, w_in, q_norm_g, w_uq, kv_norm_g, w_uk, w_uv, ret_gn_g, w_out, ln1_g, ln1_b, w_up, w_down, ln2_g, ln2_b, loss_target, m_w_in, m_q_norm_g, m_w_uq, m_kv_norm_g, m_w_uk, m_w_uv, m_ret_gn_g, m_w_out, m_ln1_g, m_ln1_b, m_w_up, m_w_down, m_ln2_g, m_ln2_b, v_w_in, v_q_norm_g, v_w_uq, v_kv_norm_g, v_w_uk, v_w_uv, v_ret_gn_g, v_w_out, v_ln1_g, v_ln1_b, v_w_up, v_w_down, v_ln2_g, v_ln2_b):
    T, D = x.shape[1], x.shape[2]
    H = D // 256
    RW = H * RET_DV
    x = x[0]
    tgt = loss_target[0]
    alpha = DEEPNORM_ALPHA
    in_width = w_in.shape[2] * N_DEV
    mla_in = in_width - 4 * RW
    mla_in_pad = -(-mla_in // 512) * 512
    f_uq = MLA_NOPE + MLA_ROPE

    names = ["w_in", "w_uq", "w_uk", "w_uv", "w_out", "w_up", "w_down"]
    big = dict(zip(names, [w_in[0].T, w_uq[0], w_uk[0], w_uv[0], w_out[0], w_up[0], w_down[0]]))
    kind = dict(zip(names, ["blk", "blk", "cols", "cols", "rows", "cols", "rows"]))
    half = {n: w.astype(BF16) for n, w in big.items()}

    def gather(ns, **parts):
        return _GatherJob([half[n] for n in ns], [kind[n] for n in ns], 0.55, 0.9, **parts)

    tr = min(256, T)
    xb, cos2, sin2, cosp, sina, sinb, g_in = _rowwise(
        "prep", _prep_tile, [(x, D, 0), (positions[0].astype(F32)[:, None], 1, 0)], [_rope_rows()],
        [(D, BF16)] + [(LANE, F32)] * 5, [], tr, jobs=[gather(["w_in"])])
    win = g_in.reshape(in_width, D)
    w1, n1 = win, 4 * RW
    w2 = jnp.pad(win[4 * RW:], ((0, mla_in_pad - mla_in), (0, 0)))

    first, second, near, far = (0, 2), (1, 2), (CHIP_X, CHIP_Y), (CHIP_DIAGONAL,)
    p1, g_uq, wuk, wuv, wout = _matmul("proj_ret", xb, w1, tb=True, n=n1,
                                       jobs=[gather(["w_uq", "w_uk", "w_uv", "w_out"])])
    wuq = jnp.transpose(g_uq, (1, 0, 2)).reshape(MLA_Q_RANK, H, f_uq)
    wuq = jnp.pad(wuq, ((0, 0), (0, 0), (0, MLA_HEAD_PAD - f_uq))).reshape(MLA_Q_RANK, H * MLA_HEAD_PAD)
    wukv = jnp.concatenate([wuk, wuv], axis=1)
    p2 = _matmul("proj_mla", xb, w2, tb=True)
    c_kv, c_kr = MLA_Q_RANK, MLA_Q_RANK + MLA_KV_RANK

    def mla_prep(p, cp, sa, sb, gq, gkv):
        cqn = _rms(p[:, :c_kv], gq)
        ckvn = _rms(p[:, c_kv:c_kr], gkv)
        return cqn, ckvn, _rope64(p[:, c_kr:c_kr + LANE], cp, sa, sb)

    cqn, ckvn, krope = _rowwise(
        "mla_prep", mla_prep, [(p2, mla_in_pad, 0), (cosp, LANE, 0), (sina, LANE, 0), (sinb, LANE, 0)],
        [q_norm_g, kv_norm_g], [(MLA_Q_RANK, BF16), (MLA_KV_RANK, BF16), (LANE, BF16)], [], tr)
    qf = _matmul("mla_q", cqn, wuq)
    kv = _matmul("mla_kv", ckvn, wukv, out_dtypes=(BF16,))

    def mla_heads(q, kn, vv, kr, cp, sa, sb):
        qs, ks, vs = [], [], []
        ones = jnp.ones((q.shape[0], MLA_DV), vv.dtype)
        for h in range(H):
            o = h * MLA_HEAD_PAD
            qs += [q[:, o:o + MLA_NOPE], _rope64(q[:, o + MLA_NOPE:o + MLA_HEAD_PAD], cp, sa, sb)]
            ks += [kn[:, h * MLA_NOPE:(h + 1) * MLA_NOPE], kr]
            vs += [vv[:, h * MLA_DV:(h + 1) * MLA_DV], ones]
        return jnp.concatenate(qs, axis=1), jnp.concatenate(ks, axis=1), jnp.concatenate(vs, axis=1)

    qh, kh, vx = _rowwise(
        "mla_heads", mla_heads,
        [(qf, H * MLA_HEAD_PAD, 0), (kv, RW, 0), (kv, RW, 1), (krope, LANE, 0), (cosp, LANE, 0), (sina, LANE, 0), (sinb, LANE, 0)],
        [], [(H * MLA_HEAD_PAD, BF16)] * 3, [], tr)
    whole = (0, 1)
    att_o, mix_half, lse, wup_a = _attention_fwd(
        qh, kh, vx, H, jobs=[gather(["w_up"], pieces=[(first, near, True, 0.45), (second, near, True, 0.75)])])
    ret_pre, mixin, wup = _retention_fwd(
        p1, cos2, sin2, ret_gn_g, mix_half, H,
        jobs=[gather(["w_up"], pieces=[(first, far, False, 0.55), (second, far, False, 0.9)], base=[wup_a])])
    mix, wdown_a = _matmul("mix_out", mixin, wout,
                           jobs=[gather(["w_down"], pieces=[((0, 4), near, True, 0.5), ((1, 4), near, True, 0.9)])])

    def ln1_fwd(xt, mt, g, b):
        z = alpha * xt + mt
        xhat, _ = _ln_stats(z)
        y = xhat * g + b
        return z, y, y

    z1, x1, x1b, wdown_b = _rowwise("ln1", ln1_fwd, [(x, D, 0), (mix, D, 0)], [ln1_g, ln1_b],
                                    [(D, F32), (D, F32), (D, BF16)], [], tr,
                                    jobs=[gather(["w_down"], pieces=[((2, 4), near, True, 0.5), ((3, 4), near, True, 0.9)],
                                                 base=[wdown_a])])

    def relu2(acc):
        r = jnp.maximum(acc, 0.0)
        return r * r, r

    a2b, rb, wdown = _matmul("mlp_up", x1b, wup, out_dtypes=(BF16, BF16), epilogue=relu2,
                             jobs=[gather(["w_down"], pieces=[(whole, far, False, 0.6)], base=[wdown_b])])
    hmlp = _matmul("mlp_down", a2b, wdown)

    def ln2_loss(x1t, ht, tt, g, b):
        xhat, rstd = _ln_stats(alpha * x1t + ht)
        err = xhat * g + b - tt
        dy = err / D
        dz = _ln_bwd(dy, xhat, rstd, g)
        lrow = 0.5 * jnp.mean(jnp.square(err), axis=-1, keepdims=True)
        return dz, dz, dy * xhat, dy, jnp.broadcast_to(lrow, (lrow.shape[0], LANE))

    dz2, dz2b, dg2, db2, lpart = _rowwise("ln2_loss", ln2_loss, [(x1, D, 0), (hmlp, D, 0), (tgt, D, 0)], [ln2_g, ln2_b],
                                          [(D, F32), (D, BF16)], [D, D, LANE], tr)
    loss = lax.psum(jnp.sum(lpart[:, 0]), ("x", "y", "c"))

    def pair(grads):
        ns = list(grads)
        return _PairJob([grads[n] for n in ns], [kind[n] for n in ns], [big[n].shape for n in ns])

    my_core = lax.axis_index("c").astype(jnp.int32).reshape(1)
    my_chip = (2 * lax.axis_index("x") + lax.axis_index("y")).astype(jnp.int32).reshape(1)
    chip_sums = {}

    def pair_sums(grads, theirs):
        for n, t in zip(grads, theirs):
            chip_sums[n] = _pair_sum("pair_sum_" + n, grads[n], kind[n], big[n].shape, t, my_core)
        return {n: chip_sums[n] for n in grads}

    def chip(sums, rels=CHIP_RELS, by_source=None):
        ns = list(sums)
        return _ChipJob([sums[n] for n in ns], [big[n].shape for n in ns], rels, by_source)

    landed = {}
    da = _matmul("mlp_down_dx", dz2b, wdown, tb=True, out_dtypes=(BF16,), extras=(rb,),
                 epilogue=lambda acc, r: (acc * (2.0 * r.astype(F32)),))
    g_down = {"w_down": _matmul("mlp_down_dw", a2b, dz2b, ta=True, out_dtypes=(BF16,))}
    g_wup, *pr = _matmul("mlp_up_dw", x1b, da, ta=True, out_dtypes=(BF16,), jobs=[pair(g_down)])
    sums_down = pair_sums(g_down, pr)
    g_up = {"w_up": g_wup}
    dx1m, l_a, *pr = _matmul("mlp_up_dx", da, wup, tb=True, jobs=[chip(sums_down, CHIP_RELS[:2]), pair(g_up)])
    sums_up = pair_sums(g_up, pr)

    def ln1_bwd(dm, dzz, z, g):
        xhat, rstd = _ln_stats(z)
        dy = dm + alpha * dzz
        dz = _ln_bwd(dy, xhat, rstd, g)
        return dz, dz, dy * xhat, dy

    dz1, dz1b, dg1, db1 = _rowwise("ln1_bwd", ln1_bwd, [(dx1m, D, 0), (dz2, D, 0), (z1, D, 0)], [ln1_g],
                                   [(D, F32), (D, BF16)], [D, D], tr)
    g_wout = _matmul("mix_out_dw", mixin, dz1b, ta=True, out_dtypes=(BF16,))
    dmix = _matmul("mix_out_dx", dz1b, wout, tb=True)

    dqh, dkn, dkr, dvv, l_b, l_c = _attention_bwd(qh, kh, vx, att_o, lse, dmix, H,
                                                  jobs=[chip(sums_down, CHIP_RELS[2:]), chip(sums_up)])
    landed["w_down"], landed["w_up"] = [l_a, l_b], [l_c]
    att_scale = (MLA_NOPE + MLA_ROPE) ** -0.5

    def mla_heads_bwd(dq, dkrh, cp, sa, sb):
        parts, acc = [], dkrh[:, :LANE]
        dq = dq * att_scale
        for h in range(H):
            o = h * MLA_HEAD_PAD
            parts += [dq[:, o:o + MLA_NOPE], _unrope64(dq[:, o + MLA_NOPE:o + MLA_HEAD_PAD], cp, sa, sb)]
            if h:
                acc = acc + dkrh[:, h * LANE:(h + 1) * LANE]
        return jnp.concatenate(parts, axis=1), _unrope64(acc, cp, sa, sb)

    dqb, dkr128 = _rowwise(
        "mla_heads_bwd", mla_heads_bwd,
        [(dqh, H * MLA_HEAD_PAD, 0), (dkr, H * LANE, 0), (cosp, LANE, 0), (sina, LANE, 0), (sinb, LANE, 0)],
        [], [(H * MLA_HEAD_PAD, BF16), (LANE, F32)], [], tr)
    dcqn = _matmul("mla_q_dx", dqb, wuq, tb=True)
    g_wuq = _matmul("mla_q_dw", cqn, dqb, ta=True, out_dtypes=(BF16,))
    dkvb = jnp.concatenate([dkn, dvv], axis=1)
    dckvn = _matmul("mla_kv_dx", dkvb, wukv, tb=True)
    g_wukv = _matmul("mla_kv_dw", ckvn, dkvb, ta=True, out_dtypes=(BF16,))

    def rms_bwd(c, dy, g):
        rstd = lax.rsqrt(jnp.mean(jnp.square(c), axis=-1, keepdims=True) + EPS)
        dyg = dy * g
        dc = rstd * (dyg - c * (rstd * rstd) * jnp.mean(dyg * c, axis=-1, keepdims=True))
        return dc, dy * c * rstd

    def mla_prep_bwd(p, dq_, dkv_, dkr_, gq, gkv):
        dcq, dgq_ = rms_bwd(p[:, :c_kv], dq_, gq)
        dckv, dgkv_ = rms_bwd(p[:, c_kv:c_kr], dkv_, gkv)
        pad = jnp.zeros((p.shape[0], mla_in_pad - c_kr - LANE), F32)
        return jnp.concatenate([dcq, dckv, dkr_, pad], axis=1), dgq_, dgkv_

    d_p2, dgq, dgkv = _rowwise(
        "mla_prep_bwd", mla_prep_bwd, [(p2, mla_in_pad, 0), (dcqn, MLA_Q_RANK, 0), (dckvn, MLA_KV_RANK, 0), (dkr128, LANE, 0)],
        [q_norm_g, kv_norm_g], [(mla_in_pad, BF16)], [MLA_Q_RANK, MLA_KV_RANK], tr)

    def gate_bwd(pre, gate, dout, gn):
        dpre, dgate, dgn = [], [], []
        for h in range(H):
            sl = slice(h * RET_DV, (h + 1) * RET_DV)
            nrm, rstd = _ln_stats(pre[:, sl])
            gate_h, do_h, gn_h = gate[:, sl], dout[:, sl], gn[:, sl]
            sg = jax.nn.sigmoid(gate_h)
            silu = gate_h * sg
            dgate.append(do_h * (nrm * gn_h) * (sg * (1.0 + gate_h * (1.0 - sg))))
            dn = do_h * silu
            dgn.append(dn * nrm)
            dpre.append(_ln_bwd(dn, nrm, rstd, gn_h))
        return jnp.concatenate(dpre, axis=1), jnp.concatenate(dgate, axis=1), jnp.concatenate(dgn, axis=1)

    dpre, drg, dgn = _rowwise("ret_gate_bwd", gate_bwd, [(ret_pre, RW, 0), (p1, RW, 3), (dmix, RW, 0)], [ret_gn_g],
                              [(RW, F32), (RW, BF16)], [RW], tr)
    g_uq = g_wuq.reshape(MLA_Q_RANK, H, MLA_HEAD_PAD)[:, :, :f_uq].reshape(MLA_Q_RANK, N_DEV, H * f_uq // N_DEV)
    mid_grads = {"w_uq": jnp.transpose(g_uq, (1, 0, 2)), "w_uk": g_wukv[:, :RW], "w_uv": g_wukv[:, RW:], "w_out": g_wout}
    (drq, *pr), (drk, drv) = _retention_bwd(p1, dpre, cos2, sin2, H, dq_jobs=[pair(mid_grads)])
    sums_mid = pair_sums(mid_grads, pr)
    d_p1 = jnp.concatenate([drq, drk, drv, drg], axis=1)

    g_w1, *l_a = _matmul("proj_ret_dw", d_p1, xb, ta=True, out_dtypes=(BF16,), jobs=[chip(sums_mid)])
    for n, a_ in zip(mid_grads, l_a):
        landed[n] = [a_]
    g_w2 = _matmul("proj_mla_dw", d_p2, xb, ta=True, out_dtypes=(BF16,))

    g_win = {"w_in": jnp.concatenate([g_w1, g_w2[:mla_in]], axis=0).reshape(N_DEV, in_width // N_DEV, D)}
    small = jnp.concatenate([dgq, dgkv, dgn, dg1, db1, dg2, db2], axis=1)
    dx_a, *pr = _matmul("proj_mla_dx", d_p2, w2, extras=(dz1,), epilogue=lambda acc, d: (acc + alpha * d,), jobs=[pair(g_win)])
    sums_in = pair_sums(g_win, pr)
    grad_x, l_a, small_landed = _matmul("proj_ret_dx", d_p1, w1, extras=(dx_a,), epilogue=lambda acc, d: (acc + d,),
                                        jobs=[chip(sums_in, CHIP_RELS[:2], by_source=small)])

    moments = dict(zip(names, zip([m_w_in, m_w_uq, m_w_uk, m_w_uv, m_w_out, m_w_up, m_w_down],
                                  [v_w_in, v_w_uq, v_w_uk, v_w_uv, v_w_out, v_w_up, v_w_down])))
    res = {}

    def adamw(n, jobs=()):
        as_held = (lambda a: a[0].T) if n == "w_in" else (lambda a: a[0])
        as_given = (lambda r: r.T[None]) if n == "w_in" else (lambda r: r[None])
        out = _adamw("adamw_" + n, chip_sums[n], my_chip, landed[n], big[n], as_held(moments[n][0]), as_held(moments[n][1]),
                     jobs=jobs)
        res[n] = [as_given(r) for r in out[:4]]
        return out[4:]

    (l_b,) = adamw("w_out", jobs=[chip(sums_in, CHIP_RELS[2:])])
    landed["w_in"] = [l_a, l_b]
    for n in names:
        if n != "w_out":
            adamw(n)

    small_names = ["q_norm_g", "kv_norm_g", "ret_gn_g", "ln1_g", "ln1_b", "ln2_g", "ln2_b"]
    small_w = [q_norm_g, kv_norm_g, ret_gn_g, ln1_g, ln1_b, ln2_g, ln2_b]
    small_m = [m_q_norm_g, m_kv_norm_g, m_ret_gn_g, m_ln1_g, m_ln1_b, m_ln2_g, m_ln2_b]
    small_v = [v_q_norm_g, v_kv_norm_g, v_ret_gn_g, v_ln1_g, v_ln1_b, v_ln2_g, v_ln2_b]
    cat = lambda arrs: jnp.concatenate(arrs, axis=1)
    sres = _adamw_small(small_landed, cat(small_w), cat(small_m), cat(small_v))
    off = 0
    for n, w in zip(small_names, small_w):
        res[n] = [r[:, off:off + w.shape[1]] for r in sres]
        off += w.shape[1]

    order = ["w_in", "q_norm_g", "w_uq", "kv_norm_g", "w_uk", "w_uv", "ret_gn_g", "w_out", "ln1_g", "ln1_b",
             "w_up", "w_down", "ln2_g", "ln2_b"]
    outs = [loss, grad_x[None]]
    for k in range(4):
        outs += [res[n][k] for n in order]
    return tuple(outs)
```

```python
import functools

import numpy as np

import jax
import jax.numpy as jnp
from jax import lax
from jax.experimental import pallas as pl
from jax.experimental.pallas import tpu as pltpu

F32 = jnp.float32
BF16 = jnp.bfloat16

CHUNK = 64
RET_DK = 128
RET_DV = 128
MLA_NOPE = 128
MLA_ROPE = 64
MLA_DV = 128
MLA_Q_RANK = 768
MLA_KV_RANK = 512
MLA_HEAD_PAD = 256
ROPE_BASE = 10000.0
EPS = 1e-5
DEPTH = 1
DEEPNORM_ALPHA = (2.0 * DEPTH) ** 0.25
ADAM_LR = 0.001
ADAM_B1 = 0.9
ADAM_B2 = 0.999
ADAM_EPS = 1e-08
ADAM_WD = 0.01
ADAM_STEP = 10

N_DEV = 8
LANE = 128
SUBLANE = 8
VMEM_LIMIT = 48 * 1024 * 1024
ADAMW_TILE_ELEMS = 128 * 1024
PAIR_TILE_ELEMS = 1024 * 1024

MESH = pl.DeviceIdType.MESH
ANY = pl.BlockSpec(memory_space=pl.ANY)

NN = (((1,), (0,)), ((), ()))
NT = (((1,), (1,)), ((), ()))
TN = (((0,), (0,)), ((), ()))


def _dot(a, b, dims=NN):
    return lax.dot_general(a.astype(BF16), b.astype(BF16), dims, preferred_element_type=F32)


def _params(n_axes):
    return pltpu.CompilerParams(dimension_semantics=("arbitrary",) * n_axes, vmem_limit_bytes=VMEM_LIMIT)


def _tile(n, want):
    if n <= want:
        return n
    t = want
    while t >= LANE:
        if n % t == 0:
            return t
        t -= LANE
    return n


def _my_place():
    return lax.axis_index("x"), lax.axis_index("y"), lax.axis_index("c")


def _flat(px, py, pc):
    return 4 * px + 2 * py + pc


def _block_of(ref, kind, idx, rows, cols, sub=None):
    lo, n = (0, rows) if sub is None else sub
    if kind == "blk":
        return ref.at[idx, pl.ds(lo, n), :]
    if kind == "rows":
        return ref.at[pl.ds(pl.multiple_of(idx * rows + lo, n), n), :]
    return ref.at[pl.ds(lo, n), pl.ds(pl.multiple_of(idx * cols, cols), cols)]


CHIP_X, CHIP_Y, CHIP_DIAGONAL = 0, 1, 2
ALL_CHIPS = (CHIP_X, CHIP_Y, CHIP_DIAGONAL)


class _GatherJob:
    def __init__(self, shards, kinds, mid, late, pieces=(((0, 1), ALL_CHIPS, True),), base=None):
        nw = len(shards)
        self.nw, self.kinds = nw, list(kinds)
        self.pieces = [tuple(p) if len(p) == 4 else (*p, late) for p in pieces]
        for _, chips, _, _ in self.pieces:
            assert CHIP_DIAGONAL not in chips or len(chips) in (1, 3)
        self.shapes = [s.shape for s in shards]
        self.ins = list(shards) + (list(base) if base is not None else [])
        self.alias = [(nw + w, w) for w in range(nw)] if base is not None else []
        self.out_shape = []
        for s, kind in zip(shards, kinds):
            r, c = s.shape
            shape = {"blk": (N_DEV, r, c), "rows": (N_DEV * r, c), "cols": (r, N_DEV * c)}[kind]
            self.out_shape.append(jax.ShapeDtypeStruct(shape, s.dtype))
        n = 7 * nw * len(self.pieces)
        self.sems = [pltpu.SemaphoreType.DMA((n,)), pltpu.SemaphoreType.DMA((n,)), pltpu.SemaphoreType.DMA((n,))]
        whens = sorted({p[3] for p in self.pieces})
        self.phases = ([(0.0, self._start), (mid, self._relay)] + [(f, functools.partial(self._pass_on, f)) for f in whens]
                       + [(1.0, self._finish)])

    def _each(self, src, dst, send_sems, recv_sems, local_sems):
        x, y, c = _my_place()
        all_chips = [(1 - x, y), (x, 1 - y), (1 - x, 1 - y)]
        out = []
        for w in range(self.nw):
            r, cc = self.shapes[w]
            for p, ((k_of, of), chips, own, when) in enumerate(self.pieces):
                sub = (k_of * (r // of), r // of)
                s0 = 7 * (w * len(self.pieces) + p)
                mine = src[w].at[pl.ds(*sub), :]

                def place(dev, w=w, r=r, cc=cc, sub=sub):
                    return _block_of(dst[w], self.kinds[w], _flat(*dev), r, cc, sub)

                def copy(k, block, to, from_src=False, place=place, mine=mine, s0=s0):
                    return pltpu.make_async_remote_copy(
                        src_ref=mine if from_src else place(block), dst_ref=place(block),
                        send_sem=send_sems.at[s0 + k], recv_sem=recv_sems.at[s0 + k], device_id=to, device_id_type=MESH)

                relay = functools.partial(copy, 1 + CHIP_DIAGONAL, (x ^ (1 - c), y ^ c, c), (x ^ c, y ^ (1 - c), c))
                local = functools.partial(pltpu.make_async_copy, mine, place((x, y, c)), local_sems.at[s0])
                near = [(j, all_chips[j]) for j in chips if j != CHIP_DIAGONAL]
                out.append((near, CHIP_DIAGONAL in chips, own, copy, relay, local, when))
        return (x, y, c), (x, y, 1 - c), all_chips[CHIP_DIAGONAL], out

    def _start(self, *refs):
        me, sibling, diagonal, each = self._each(*refs)
        for near, far, own, copy, relay, local, when in each:
            if own:
                local().start()
                copy(0, me, sibling, from_src=True).start()
            for j, chip in near:
                copy(1 + j, me, (*chip, me[2]), from_src=True).start()
            if far and not near:
                relay().start()

    def _landed(self, near, copy, me, sibling):
        for j, chip in near:
            copy(1 + j, (*chip, me[2]), me).wait_recv()
            copy(4 + j, (*chip, me[2]), sibling).start()

    def _relay(self, *refs):
        me, sibling, diagonal, each = self._each(*refs)
        for near, far, own, copy, relay, local, when in each:
            if far and near:
                self._landed(near, copy, me, sibling)
                relay().start()

    def _pass_on(self, frac, *refs):
        me, sibling, diagonal, each = self._each(*refs)
        for near, far, own, copy, relay, local, when in each:
            if when == frac:
                self._landed([(CHIP_DIAGONAL, diagonal)] if far else near, copy, me, sibling)

    def _finish(self, *refs):
        me, sibling, diagonal, each = self._each(*refs)
        for near, far, own, copy, relay, local, when in each:
            if own:
                copy(0, sibling, me).wait_recv()
            for j, chip in near + [(CHIP_DIAGONAL, diagonal)] * int(far):
                copy(4 + j, (*chip, 1 - me[2]), me).wait_recv()
        for near, far, own, copy, relay, local, when in each:
            if own:
                copy(0, me, sibling, from_src=True).wait_send()
                local().wait()
            for j, chip in near:
                copy(1 + j, me, (*chip, me[2]), from_src=True).wait_send()
                copy(4 + j, (*chip, me[2]), sibling).wait_send()
            if far:
                relay().wait_send()
                copy(4 + CHIP_DIAGONAL, (*diagonal, me[2]), sibling).wait_send()


REL_ALL = [(1, 0, 0), (1, 0, 1), (0, 1, 0), (0, 1, 1), (1, 1, 0), (1, 1, 1), (0, 0, 1)]
N_CHIP = 4


class _PairJob:
    def __init__(self, grads, kinds, shard_shapes):
        self.ins = list(grads)
        self.kinds, self.shard_shapes, self.nw = list(kinds), list(shard_shapes), len(grads)
        self.out_shape = [jax.ShapeDtypeStruct((N_CHIP,) + tuple(s), g.dtype) for g, s in zip(grads, shard_shapes)]
        n = N_CHIP * self.nw
        self.sems = [pltpu.SemaphoreType.DMA((n,)), pltpu.SemaphoreType.DMA((n,))]
        self.phases = [(0.0, self._start), (1.0, self._finish)]

    def _copies(self, src, theirs, send_sems, recv_sems):
        x, y, c = _my_place()
        remote = []
        for w in range(self.nw):
            r, cc = self.shard_shapes[w]
            for q in range(N_CHIP):
                s = N_CHIP * w + q
                remote.append(pltpu.make_async_remote_copy(
                    src_ref=_block_of(src[w], self.kinds[w], 2 * q + (1 - c), r, cc), dst_ref=theirs[w].at[q],
                    send_sem=send_sems.at[s], recv_sem=recv_sems.at[s], device_id=(x, y, 1 - c), device_id_type=MESH))
        return remote

    def _start(self, *refs):
        for cp in self._copies(*refs):
            cp.start()

    def _finish(self, *refs):
        for cp in self._copies(*refs):
            cp.wait()


CHIP_RELS = [(1, 0), (0, 1), (1, 1)]


class _ChipJob:
    def __init__(self, chip_sums, shard_shapes, rels=CHIP_RELS, by_source=None, piece=(0, 1), base=None):
        nw = len(chip_sums)
        self.ins = list(chip_sums) + ([by_source] if by_source is not None else []) + (list(base) if base is not None else [])
        self.nw, self.has_src, self.rels, self.piece = nw, by_source is not None, list(rels), piece
        self.rows = [s[0] for s in shard_shapes]
        self.alias = [(nw + int(self.has_src) + w, w) for w in range(nw)] if base is not None else []
        self.out_shape = [jax.ShapeDtypeStruct((len(rels),) + tuple(s), a.dtype) for a, s in zip(chip_sums, shard_shapes)]
        if self.has_src:
            self.out_shape.append(jax.ShapeDtypeStruct((N_DEV,) + by_source.shape, by_source.dtype))
        n = len(rels) * self.nw + len(REL_ALL) * int(self.has_src)
        self.sems = [pltpu.SemaphoreType.DMA((n,)), pltpu.SemaphoreType.DMA((n,)), pltpu.SemaphoreType.DMA((1,))]
        self.phases = [(0.0, self._start), (1.0, self._finish)]

    def _copies(self, src, land, send_sems, recv_sems, local_sems):
        x, y, c = _my_place()
        nw = self.nw

        def remote_copy(s, src_ref, dst_ref, peer):
            return pltpu.make_async_remote_copy(src_ref=src_ref, dst_ref=dst_ref, send_sem=send_sems.at[s],
                                                recv_sem=recv_sems.at[s], device_id=peer, device_id_type=MESH)

        def rows(w):
            n = self.rows[w] // self.piece[1]
            return pl.ds(self.piece[0] * n, n)

        local, remote = [], []
        for k, (fx, fy) in enumerate(self.rels):
            px, py = x ^ fx, y ^ fy
            remote += [remote_copy(nw * k + w, src[w].at[2 * px + py, rows(w), :], land[w].at[k, rows(w), :], (px, py, c))
                       for w in range(nw)]
        if self.has_src:
            my_idx = _flat(x, y, c)
            local.append(pltpu.make_async_copy(src[nw], land[nw].at[my_idx], local_sems.at[0]))
            remote += [remote_copy(nw * len(self.rels) + k, src[nw], land[nw].at[my_idx], (x ^ fx, y ^ fy, c ^ fc))
                       for k, (fx, fy, fc) in enumerate(REL_ALL)]
        return local, remote

    def _start(self, *refs):
        local, remote = self._copies(*refs)
        for cp in local + remote:
            cp.start()

    def _finish(self, *refs):
        local, remote = self._copies(*refs)
        for cp in remote + local:
            cp.wait()


def _row_tile(R, C, elems=ADAMW_TILE_ELEMS):
    tr = 16
    while tr * 2 * C <= elems and R % (tr * 2) == 0:
        tr *= 2
    return tr if R % tr == 0 else R


BF16_ROWS = 16


def _tile_2d(R, C, elems):
    if R % BF16_ROWS == 0:
        tr = _row_tile(R, C, elems)
        return tr, C, R // tr
    tc = C
    while tc % (2 * LANE) == 0 and R * tc > elems:
        tc //= 2
    return R, tc, C // tc


def _tile_index(tr, tc, C):
    return (lambda i: (i, 0)) if tc == C else (lambda i: (0, i))


def _part_spec(kind, tr, tc, R, C, n_lead):
    at = _tile_index(tr, tc, C)

    def dev(ids, pref):
        return 2 * ids[0] + pref[0] if n_lead == 2 else pref[0]

    def index(*args):
        ids, pref = args[:n_lead], args[n_lead]
        ri, ci = at(ids[-1])
        if kind == "blk":
            return dev(ids, pref), ri, ci
        if kind == "rows":
            return dev(ids, pref) * (R // tr) + ri, ci
        return ri, dev(ids, pref) * (C // tc) + ci

    return pl.BlockSpec((None, tr, tc) if kind == "blk" else (tr, tc), index)


def _pair_sum(name, grad, kind, shard_shape, theirs, c):
    R, C = shard_shape
    tr, tc, tiles = _tile_2d(R, C, PAIR_TILE_ELEMS)
    at = _tile_index(tr, tc, C)

    def body(c_ref, g_ref, t_ref, o_ref):
        o_ref[...] = (g_ref[...].astype(F32) + t_ref[...].astype(F32)).astype(o_ref.dtype)

    slot = pl.BlockSpec((None, tr, tc), lambda q, i, c_ref: (q, *at(i)))
    return pl.pallas_call(
        body, name=name,
        grid_spec=pltpu.PrefetchScalarGridSpec(
            num_scalar_prefetch=1, grid=(N_CHIP, tiles),
            in_specs=[_part_spec(kind, tr, tc, R, C, 2), slot], out_specs=slot),
        out_shape=jax.ShapeDtypeStruct(theirs.shape, theirs.dtype), compiler_params=_params(2))(c, grad, theirs)


class _Carried:
    def __init__(self, jobs, grid):
        self.jobs = list(jobs)
        self.steps = int(np.prod(grid))
        self.grid = tuple(grid)
        self.operands = [a for j in self.jobs for a in j.ins]
        self.out_shape = [o for j in self.jobs for o in j.out_shape]
        self.sems = [s for j in self.jobs for s in j.sems]
        self.in_specs = [ANY] * len(self.operands)
        self.out_specs = [ANY] * len(self.out_shape)

    def aliases(self, n_in, n_out):
        out, i0, o0 = {}, n_in, n_out
        for job in self.jobs:
            for ji, jo in getattr(job, "alias", []):
                out[i0 + ji] = o0 + jo
            i0, o0 = i0 + len(job.ins), o0 + len(job.out_shape)
        return out

    def _step(self):
        lin = pl.program_id(0)
        for a in range(1, len(self.grid)):
            lin = lin * self.grid[a] + pl.program_id(a)
        return lin

    def _run(self, in_refs, out_refs, sem_refs, last):
        lin = self._step()
        for job in self.jobs:
            ins, in_refs = in_refs[:len(job.ins)], in_refs[len(job.ins):]
            outs, out_refs = out_refs[:len(job.out_shape)], out_refs[len(job.out_shape):]
            sems, sem_refs = sem_refs[:len(job.sems)], sem_refs[len(job.sems):]
            for frac, fn in job.phases:
                if (frac >= 1.0) != last:
                    continue
                at = min(self.steps - 1, int(frac * (self.steps - 1)))
                pl.when(lin == at)(functools.partial(fn, ins, outs, *sems))

    def before(self, in_refs, out_refs, sem_refs):
        self._run(in_refs, out_refs, sem_refs, last=False)

    def after(self, in_refs, out_refs, sem_refs):
        self._run(in_refs, out_refs, sem_refs, last=True)


def _split_refs(refs, n_in, n_job_in, n_out, n_job_out, n_scratch):
    cuts = np.cumsum([0, n_in, n_job_in, n_out, n_job_out, n_scratch])
    return [refs[cuts[i]:cuts[i + 1]] for i in range(5)] + [refs[cuts[5]:]]


def _matmul(name, a, b, *, ta=False, tb=False, out_dtypes=(F32,), epilogue=None, extras=(), tm=1024, tn=1024, tk=2048, jobs=(),
            n=None):
    if ta:
        K, M = a.shape
    else:
        M, K = a.shape
    N = n if n is not None else (b.shape[0] if tb else b.shape[1])
    tm, tn, tk = _tile(M, tm), _tile(N, tn), _tile(K, tk)
    nk = K // tk
    grid = (M // tm, N // tn, nk)
    dims = TN if ta else (NT if tb else NN)
    n_ex, n_out = len(extras), len(out_dtypes)
    carried = _Carried(jobs, grid)

    def body(*refs):
        ins, job_ins, out_refs, job_outs, scratch, job_sems = _split_refs(
            refs, 2 + n_ex, len(carried.operands), n_out, len(carried.out_shape), int(nk > 1))
        a_ref, b_ref, ex_refs = ins[0], ins[1], ins[2:]
        k = pl.program_id(2)
        carried.before(job_ins, job_outs, job_sems)

        def finish(acc):
            res = (acc,) if epilogue is None else epilogue(acc, *[e[...] for e in ex_refs])
            for o_ref, r in zip(out_refs, res):
                o_ref[...] = r.astype(o_ref.dtype)

        if nk == 1:
            finish(_dot(a_ref[...], b_ref[...], dims))
        else:
            acc_ref = scratch[0]

            @pl.when(k == 0)
            def _():
                acc_ref[...] = _dot(a_ref[...], b_ref[...], dims)

            @pl.when(k > 0)
            def _():
                acc_ref[...] += _dot(a_ref[...], b_ref[...], dims)

            pl.when(k == nk - 1)(lambda: finish(acc_ref[...]))

        carried.after(job_ins, job_outs, job_sems)

    a_spec = pl.BlockSpec((tk, tm), lambda i, j, k: (k, i)) if ta else pl.BlockSpec((tm, tk), lambda i, j, k: (i, k))
    b_spec = pl.BlockSpec((tn, tk), lambda i, j, k: (j, k)) if tb else pl.BlockSpec((tk, tn), lambda i, j, k: (k, j))
    tile_spec = pl.BlockSpec((tm, tn), lambda i, j, k: (i, j))
    outs = pl.pallas_call(
        body,
        name=name,
        grid=grid,
        in_specs=[a_spec, b_spec] + [tile_spec] * n_ex + carried.in_specs,
        out_specs=[tile_spec] * n_out + carried.out_specs,
        out_shape=[jax.ShapeDtypeStruct((M, N), d) for d in out_dtypes] + carried.out_shape,
        scratch_shapes=[pltpu.VMEM((tm, tn), F32)] * int(nk > 1) + carried.sems,
        input_output_aliases=carried.aliases(2 + n_ex, n_out),
        compiler_params=_params(3),
    )(a, b, *extras, *carried.operands)
    return outs[0] if len(outs) == 1 else outs


def _rowwise(name, fn, row_in, const_in, row_out, acc_out, tm, jobs=()):
    T = row_in[0][0].shape[0]
    n_r, n_c, n_o, n_a = len(row_in), len(const_in), len(row_out), len(acc_out)
    carried = _Carried(jobs, (T // tm,))

    def body(*refs):
        in_refs, job_ins, own_outs, job_outs, _, job_sems = _split_refs(
            refs, n_r + n_c, len(carried.operands), n_o + n_a, len(carried.out_shape), 0)
        out_refs, acc_refs = own_outs[:n_o], own_outs[n_o:]
        carried.before(job_ins, job_outs, job_sems)
        res = fn(*[r[...] for r in in_refs])
        for o_ref, r in zip(out_refs, res[:n_o]):
            o_ref[...] = r.astype(o_ref.dtype)

        @pl.when(pl.program_id(0) == 0)
        def _():
            for a_ref in acc_refs:
                a_ref[...] = jnp.zeros_like(a_ref)

        for a_ref, r in zip(acc_refs, res[n_o:]):
            a_ref[...] += jnp.sum(r.reshape(tm // SUBLANE, SUBLANE, r.shape[-1]), axis=0)
        carried.after(job_ins, job_outs, job_sems)

    in_specs = [pl.BlockSpec((tm, w), functools.partial(lambda i, c: (i, c), c=cb)) for (_, w, cb) in row_in]
    in_specs += [pl.BlockSpec(c.shape, functools.partial(lambda i, nd: (0,) * nd, nd=c.ndim)) for c in const_in]
    out_specs = [pl.BlockSpec((tm, c), lambda i: (i, 0)) for (c, _) in row_out]
    out_specs += [pl.BlockSpec((SUBLANE, c), lambda i: (0, 0)) for c in acc_out]
    out_shape = [jax.ShapeDtypeStruct((T, c), d) for (c, d) in row_out]
    out_shape += [jax.ShapeDtypeStruct((SUBLANE, c), F32) for c in acc_out]
    return pl.pallas_call(
        body, name=name, grid=(T // tm,), in_specs=in_specs + carried.in_specs, out_specs=out_specs + carried.out_specs,
        out_shape=out_shape + carried.out_shape, scratch_shapes=carried.sems,
        input_output_aliases=carried.aliases(n_r + n_c, n_o + n_a), compiler_params=_params(1),
    )(*[r[0] for r in row_in], *const_in, *carried.operands)


def _ln_stats(z):
    mu = jnp.mean(z, axis=-1, keepdims=True)
    var = jnp.mean(jnp.square(z - mu), axis=-1, keepdims=True)
    rstd = lax.rsqrt(var + EPS)
    return (z - mu) * rstd, rstd


def _ln_bwd(dy, xhat, rstd, g):
    dxh = dy * g
    return rstd * (dxh - jnp.mean(dxh, axis=-1, keepdims=True) - xhat * jnp.mean(dxh * xhat, axis=-1, keepdims=True))


def _rms(x, g):
    return x * lax.rsqrt(jnp.mean(jnp.square(x), axis=-1, keepdims=True) + EPS) * g


def _rope128(x, cos2, sin2):
    return x * cos2 + pltpu.roll(x, RET_DK // 2, 1) * sin2


def _unrope128(dy, cos2, sin2):
    return dy * cos2 + pltpu.roll(dy * sin2, RET_DK // 2, 1)


def _rope64(x, cosp, sina, sinb):
    h = MLA_ROPE // 2
    return x * cosp + pltpu.roll(x, h, 1) * sina + pltpu.roll(x, LANE - h, 1) * sinb


def _unrope64(dy, cosp, sina, sinb):
    h = MLA_ROPE // 2
    return dy * cosp + pltpu.roll(dy * sina, LANE - h, 1) + pltpu.roll(dy * sinb, h, 1)


RET_BLOCK = 256
RET_HEADS_PER_STEP = 4


def _ret_tables(H, B):
    log_g = np.log1p(-np.exp2(-5.0 - np.arange(H, dtype=np.float32))).astype(np.float32)
    idx = np.arange(B, dtype=np.float32)
    dist = np.abs(idx[:, None] - idx[None, :])
    visible = (np.arange(B)[None, :] // CHUNK) <= (np.arange(B)[:, None] // CHUNK)
    dmat = np.exp(log_g[:, None, None] * dist).astype(np.float32) * visible[None].astype(np.float32)
    qd = np.exp(log_g[:, None] * (idx + 1.0)).astype(np.float32)
    kd = np.exp(log_g[:, None] * (B - 1 - idx)).astype(np.float32)
    sd = np.exp(log_g * B).astype(np.float32)
    bc = lambda v: np.ascontiguousarray(np.broadcast_to(v[:, :, None], (H, v.shape[1], LANE)))
    sdb = np.ascontiguousarray(np.broadcast_to(sd[:, None, None], (H, SUBLANE, LANE)))
    return jnp.asarray(dmat), jnp.asarray(bc(qd)), jnp.asarray(bc(kd)), jnp.asarray(sdb)


def _ret_specs(H, B, reverse, nb):
    blk = (lambda b: nb - 1 - b) if reverse else (lambda b: b)
    G = min(RET_HEADS_PER_STEP, H)
    W = G * LANE
    col = lambda off: pl.BlockSpec((B, W), functools.partial(lambda h, b, off: (blk(b), off // G + h), off=off))
    row = pl.BlockSpec((B, LANE), lambda h, b: (blk(b), 0))
    per_head = lambda r: pl.BlockSpec((G, r, LANE), lambda h, b: (h, 0, 0))
    dmat = pl.BlockSpec((G, B, B), lambda h, b: (h, 0, 0))
    return G, col, row, per_head, dmat


def _ret_call(name, body, n_in, n_out, operands, in_specs, out_specs, out_shape, grid, G, jobs, own_alias=None):
    carried = _Carried(jobs, grid)

    def wrapped(*refs):
        ins, job_ins, outs, job_outs, (state,), job_sems = _split_refs(
            refs, n_in, len(carried.operands), n_out, len(carried.out_shape), 1)
        carried.before(job_ins, job_outs, job_sems)

        @pl.when(pl.program_id(1) == 0)
        def _():
            state[...] = jnp.zeros_like(state)

        body(*ins, *outs, state)
        carried.after(job_ins, job_outs, job_sems)

    return pl.pallas_call(
        wrapped, name=name, grid=grid,
        in_specs=in_specs + carried.in_specs, out_specs=out_specs + carried.out_specs,
        out_shape=out_shape + carried.out_shape,
        scratch_shapes=[pltpu.VMEM((G, RET_DK, RET_DV), F32)] + carried.sems,
        input_output_aliases={**(own_alias or {}), **carried.aliases(n_in, n_out)},
        compiler_params=_params(2),
    )(*operands, *carried.operands)


def _retention_fwd(p1, cos2, sin2, gn_g, mix_in, H, jobs=()):
    T = p1.shape[0]
    B = min(RET_BLOCK, T)
    nb = T // B
    scale = RET_DK ** -0.5
    dmat, qd, kd, sd = _ret_tables(H, B)
    G, col, row, per_head, dspec = _ret_specs(H, B, False, nb)

    def body(q_ref, k_ref, v_ref, g_ref, cos_ref, sin_ref, d_ref, qd_ref, kd_ref, sd_ref, gn_ref, _, pre_ref, out_ref, s_ref):
        cos, sin = cos_ref[...], sin_ref[...]
        for g in range(G):
            hl = pl.ds(g * LANE, LANE)
            q = _rope128(q_ref[:, hl], cos, sin)
            k = _rope128(k_ref[:, hl], cos, sin) * scale
            v = v_ref[:, hl]
            s = s_ref[g]
            a = _dot(q, k, NT) * d_ref[g]
            ret = _dot(a, v) + _dot(q * qd_ref[g], s)
            s_ref[g] = sd_ref[g, 0:1, :] * s + _dot(k * kd_ref[g], v, TN)
            pre_ref[:, hl] = ret
            nrm, _ = _ln_stats(ret)
            out_ref[:, hl] = (jax.nn.silu(g_ref[:, hl]) * (nrm * gn_ref[:, hl])).astype(out_ref.dtype)

    out_spec = pl.BlockSpec((B, G * LANE), lambda h, b: (b, h))
    return _ret_call(
        "retention_fwd", body, 12, 2, (p1, p1, p1, p1, cos2, sin2, dmat, qd, kd, sd, gn_g, mix_in),
        [col(0), col(H), col(2 * H), col(3 * H), row, row, dspec, per_head(B), per_head(B), per_head(SUBLANE),
         pl.BlockSpec((1, G * LANE), lambda h, b: (0, h)), ANY],
        [out_spec, out_spec],
        [jax.ShapeDtypeStruct((T, H * RET_DV), F32), jax.ShapeDtypeStruct(mix_in.shape, mix_in.dtype)],
        (H // G, nb), G, jobs, own_alias={11: 1})


def _retention_bwd(p1, dpre, cos2, sin2, H, dq_jobs=(), dkv_jobs=()):
    T = p1.shape[0]
    B = min(RET_BLOCK, T)
    nb = T // B
    scale = RET_DK ** -0.5
    dmat, qd, kd, sd = _ret_tables(H, B)

    def dq_body(k_ref, v_ref, do_ref, cos_ref, sin_ref, d_ref, qd_ref, kd_ref, sd_ref, dq_ref, s_ref):
        cos, sin = cos_ref[...], sin_ref[...]
        for g in range(G):
            hl = pl.ds(g * LANE, LANE)
            k = _rope128(k_ref[:, hl], cos, sin) * scale
            v, do, s = v_ref[:, hl], do_ref[:, hl], s_ref[g]
            da = _dot(do, v, NT) * d_ref[g]
            dq = _dot(da, k) + _dot(do, s, NT) * qd_ref[g]
            dq_ref[:, hl] = _unrope128(dq, cos, sin).astype(dq_ref.dtype)
            s_ref[g] = sd_ref[g, 0:1, :] * s + _dot(k * kd_ref[g], v, TN)

    G, col, row, per_head, dspec = _ret_specs(H, B, False, nb)
    dq = _ret_call(
        "retention_bwd_dq", dq_body, 9, 1, (p1, p1, dpre, cos2, sin2, dmat, qd, kd, sd),
        [col(H), col(2 * H), col(0), row, row, dspec, per_head(B), per_head(B), per_head(SUBLANE)],
        [pl.BlockSpec((B, G * LANE), lambda h, b: (b, h))], [jax.ShapeDtypeStruct((T, H * RET_DK), BF16)],
        (H // G, nb), G, dq_jobs)

    def dkv_body(q_ref, k_ref, v_ref, do_ref, cos_ref, sin_ref, d_ref, qd_ref, kd_ref, sd_ref, dk_ref, dv_ref, g_ref):
        cos, sin = cos_ref[...], sin_ref[...]
        for g in range(G):
            hl = pl.ds(g * LANE, LANE)
            q = _rope128(q_ref[:, hl], cos, sin)
            k = _rope128(k_ref[:, hl], cos, sin) * scale
            v, do, st = v_ref[:, hl], do_ref[:, hl], g_ref[g]
            a = _dot(q, k, NT) * d_ref[g]
            da = _dot(do, v, NT) * d_ref[g]
            dv = _dot(a, do, TN) + _dot(k * kd_ref[g], st)
            dk = _dot(da, q, TN) + _dot(v, st, NT) * kd_ref[g]
            dv_ref[:, hl] = dv.astype(dv_ref.dtype)
            dk_ref[:, hl] = _unrope128(dk * scale, cos, sin).astype(dk_ref.dtype)
            g_ref[g] = sd_ref[g, 0:1, :] * st + _dot(q * qd_ref[g], do, TN)

    G, col, row, per_head, dspec = _ret_specs(H, B, True, nb)
    out_spec = pl.BlockSpec((B, G * LANE), lambda h, b: (nb - 1 - b, h))
    dkv = _ret_call(
        "retention_bwd_dkv", dkv_body, 10, 2, (p1, p1, p1, dpre, cos2, sin2, dmat, qd, kd, sd),
        [col(0), col(H), col(2 * H), col(0), row, row, dspec, per_head(B), per_head(B), per_head(SUBLANE)],
        [out_spec, out_spec],
        [jax.ShapeDtypeStruct((T, H * RET_DK), BF16), jax.ShapeDtypeStruct((T, H * RET_DV), BF16)],
        (H // G, nb), G, dkv_jobs)
    return dq, dkv


ATT_BLOCK = 512


ATT_ROW_CHUNKS = 2
ATT_BWD_ROW_CHUNKS = 1
ATT_HEADS_PER_STEP = 8
ATT_BWD_HEADS_PER_STEP = 2
LOG2E = 1.4426950408889634


def _chunk_mask(rows, row0, cols):
    qi = (lax.broadcasted_iota(jnp.int32, (rows, cols), 0) + row0) // CHUNK
    kj = lax.broadcasted_iota(jnp.int32, (rows, cols), 1) // CHUNK
    return kj <= qi


def _attention_fwd(qh, kh, vx, H, jobs=()):
    T = qh.shape[0]
    t = min(ATT_BLOCK, T)
    n = T // t
    r = t // ATT_ROW_CHUNKS
    G = min(ATT_HEADS_PER_STEP, H)
    P = MLA_HEAD_PAD
    scale = (MLA_NOPE + MLA_ROPE) ** -0.5
    c2 = scale * LOG2E
    carried = _Carried(jobs, (H // G, n, n))

    def body(*refs):
        (q_ref, k_ref, v_ref), job_ins, (o_ref, ob_ref, lse_ref), job_outs, (m_ref, acc_ref), job_sems = _split_refs(
            refs, 3, len(carried.operands), 3, len(carried.out_shape), 2)
        i, j = pl.program_id(1), pl.program_id(2)
        carried.before(job_ins, job_outs, job_sems)

        @pl.when(j == 0)
        def _():
            m_ref[...] = jnp.full_like(m_ref, -jnp.inf)
            acc_ref[...] = jnp.zeros_like(acc_ref)

        def update(masked):
            for g in range(G):
                hc = pl.ds(g * P, P)
                for ch in range(ATT_ROW_CHUNKS):
                    rs = pl.ds(ch * r, r)
                    kc = (ch + 1) * r if masked else t
                    ks = pl.ds(0, kc)
                    s = _dot(q_ref[rs, hc], k_ref[ks, hc], NT)
                    if masked:
                        s = jnp.where(_chunk_mask(r, ch * r, kc), s, -jnp.inf)
                    m_prev = m_ref[rs, pl.ds(g * LANE, LANE)]
                    m_new = jnp.maximum(m_prev, jnp.max(s, axis=-1, keepdims=True))
                    alpha = jnp.exp2((m_prev - m_new) * c2)
                    p = jnp.exp2((s - jnp.tile(m_new, (1, kc // LANE))) * c2)
                    acc_ref[rs, hc] = jnp.tile(alpha, (1, 2)) * acc_ref[rs, hc] + _dot(p, v_ref[ks, hc])
                    m_ref[rs, pl.ds(g * LANE, LANE)] = m_new

        pl.when(j < i)(functools.partial(update, False))
        pl.when(j == i)(functools.partial(update, True))

        @pl.when(j == n - 1)
        def _():
            for g in range(G):
                acc = acc_ref[:, pl.ds(g * P, P)]
                l = acc[:, MLA_DV:]
                o = acc[:, :MLA_DV] / l
                o_ref[:, pl.ds(g * MLA_DV, MLA_DV)] = o
                ob_ref[:, pl.ds(g * MLA_DV, MLA_DV)] = o.astype(ob_ref.dtype)
                lse_ref[g] = m_ref[:, pl.ds(g * LANE, LANE)] * scale + jnp.log(l)

        carried.after(job_ins, job_outs, job_sems)

    kclamp = lambda i, j: jnp.minimum(j, i)
    o_spec = pl.BlockSpec((t, G * MLA_DV), lambda h, i, j: (i, h))
    mix_spec = pl.BlockSpec((t, G * MLA_DV), lambda h, i, j: (i, H // G + h))
    return pl.pallas_call(
        body, name="attention_fwd", grid=(H // G, n, n),
        in_specs=[pl.BlockSpec((t, G * P), lambda h, i, j: (i, h)),
                  pl.BlockSpec((t, G * P), lambda h, i, j: (kclamp(i, j), h)),
                  pl.BlockSpec((t, G * P), lambda h, i, j: (kclamp(i, j), h))] + carried.in_specs,
        out_specs=[o_spec, mix_spec, pl.BlockSpec((G, t, LANE), lambda h, i, j: (h, i, 0))] + carried.out_specs,
        out_shape=[jax.ShapeDtypeStruct((T, H * MLA_DV), F32), jax.ShapeDtypeStruct((T, 2 * H * MLA_DV), BF16),
                   jax.ShapeDtypeStruct((H, T, LANE), F32)] + carried.out_shape,
        scratch_shapes=[pltpu.VMEM((t, G * LANE), F32), pltpu.VMEM((t, G * P), F32)] + carried.sems,
        input_output_aliases=carried.aliases(3, 3),
        compiler_params=_params(3),
    )(qh, kh, vx, *carried.operands)


def _attention_bwd(qh, kh, vx, o, lse, dmix, H, jobs=()):
    T = qh.shape[0]
    t = min(ATT_BLOCK, T)
    n = T // t
    r = t // ATT_BWD_ROW_CHUNKS
    G = min(ATT_BWD_HEADS_PER_STEP, H)
    P = MLA_HEAD_PAD
    scale = (MLA_NOPE + MLA_ROPE) ** -0.5
    c2 = scale * LOG2E
    carried = _Carried(jobs, (H // G, n, n))

    def body(*refs):
        ((q_ref, k_ref, v_ref, do_ref, o_ref, lse_ref), job_ins, (dq_ref, dkn_ref, dkr_ref, dv_ref), job_outs,
         (dk_acc, dv_acc), job_sems) = _split_refs(refs, 6, len(carried.operands), 4, len(carried.out_shape), 2)
        j, i = pl.program_id(1), pl.program_id(2)
        carried.before(job_ins, job_outs, job_sems)

        @pl.when(jnp.logical_and(j == 0, i == 0))
        def _():
            dq_ref[...] = jnp.zeros_like(dq_ref)

        @pl.when(i == j)
        def _():
            dk_acc[...] = jnp.zeros_like(dk_acc)
            dv_acc[...] = jnp.zeros_like(dv_acc)

        def update(masked):
            for g in range(G):
                hc, hv = pl.ds(g * P, P), pl.ds(g * MLA_DV, MLA_DV)
                for ch in range(ATT_BWD_ROW_CHUNKS):
                    rs = pl.ds(ch * r, r)
                    kc = (ch + 1) * r if masked else t
                    ks = pl.ds(0, kc)
                    k, v = k_ref[ks, hc], v_ref[ks, pl.ds(g * P, MLA_DV)]
                    q, do = q_ref[rs, hc], do_ref[rs, hv]
                    s = _dot(q, k, NT)
                    if masked:
                        s = jnp.where(_chunk_mask(r, ch * r, kc), s, -jnp.inf)
                    p = jnp.exp2(s * c2 - jnp.tile(lse_ref[g, rs, :] * LOG2E, (1, kc // LANE)))
                    dv_acc[ks, hv] += _dot(p, do, TN)
                    dp = _dot(do, v, NT)
                    delta = jnp.sum(do * o_ref[rs, hv], axis=-1, keepdims=True)
                    ds = p * (dp - delta)
                    rows = pl.ds(pl.multiple_of(i * t, t) + ch * r, r)
                    dq_ref[rows, hc] += _dot(ds, k)
                    dk_acc[ks, hc] += _dot(ds, q, TN)

        pl.when(i > j)(functools.partial(update, False))
        pl.when(i == j)(functools.partial(update, True))

        @pl.when(i == n - 1)
        def _():
            for g in range(G):
                dk = dk_acc[:, pl.ds(g * P, P)] * scale
                dkn_ref[:, pl.ds(g * LANE, LANE)] = dk[:, :MLA_NOPE].astype(dkn_ref.dtype)
                dkr_ref[:, pl.ds(g * LANE, LANE)] = dk[:, MLA_NOPE:]
            dv_ref[...] = dv_acc[...].astype(dv_ref.dtype)

        carried.after(job_ins, job_outs, job_sems)

    qclamp = lambda j, i: jnp.maximum(i, j)
    kv_out = pl.BlockSpec((t, G * LANE), lambda h, j, i: (j, h))
    q_in = pl.BlockSpec((t, G * MLA_DV), lambda h, j, i: (qclamp(j, i), h))
    return pl.pallas_call(
        body, name="attention_bwd", grid=(H // G, n, n),
        in_specs=[pl.BlockSpec((t, G * P), lambda h, j, i: (qclamp(j, i), h)),
                  pl.BlockSpec((t, G * P), lambda h, j, i: (j, h)),
                  pl.BlockSpec((t, G * P), lambda h, j, i: (j, h)),
                  pl.BlockSpec((t, G * MLA_DV), lambda h, j, i: (qclamp(j, i), H // G + h)),
                  q_in,
                  pl.BlockSpec((G, t, LANE), lambda h, j, i: (h, qclamp(j, i), 0))] + carried.in_specs,
        out_specs=[pl.BlockSpec((T, G * P), lambda h, j, i: (0, h)), kv_out, kv_out, kv_out] + carried.out_specs,
        out_shape=[jax.ShapeDtypeStruct((T, H * P), F32), jax.ShapeDtypeStruct((T, H * LANE), BF16),
                   jax.ShapeDtypeStruct((T, H * LANE), F32), jax.ShapeDtypeStruct((T, H * LANE), BF16)] + carried.out_shape,
        scratch_shapes=[pltpu.VMEM((t, G * P), F32), pltpu.VMEM((t, G * MLA_DV), F32)] + carried.sems,
        input_output_aliases=carried.aliases(6, 4),
        compiler_params=_params(3),
    )(qh, kh, vx, dmix, o, lse, *carried.operands)


def _adamw_math(w, g, m, v):
    m = ADAM_B1 * m + (1.0 - ADAM_B1) * g
    v = ADAM_B2 * v + (1.0 - ADAM_B2) * jnp.square(g)
    m_hat = m / (1.0 - ADAM_B1 ** ADAM_STEP)
    v_hat = v / (1.0 - ADAM_B2 ** ADAM_STEP)
    delta = -ADAM_LR * (m_hat / (jnp.sqrt(v_hat) + ADAM_EPS) + ADAM_WD * w)
    return delta, m, v


def _adamw(name, chip_sums, my_chip, landed, w, m, v, jobs=()):
    R, C = w.shape
    tr, tc, tiles = _tile_2d(R, C, ADAMW_TILE_ELEMS)
    at = _tile_index(tr, tc, C)
    nl = len(landed)
    assert sum(l.shape[0] for l in landed) == N_CHIP - 1
    carried = _Carried(jobs, (tiles,))

    def body(q_ref, *refs):
        (own_ref, *rest), job_ins, (g_out, d_out, m_out, v_out), job_outs, _, job_sems = _split_refs(
            refs, nl + 4, len(carried.operands), 4, len(carried.out_shape), 0)
        l_refs, (w_ref, m_ref, v_ref) = rest[:nl], rest[nl:]
        carried.before(job_ins, job_outs, job_sems)
        g = own_ref[...].astype(F32)
        for l_ref in l_refs:
            for s in range(l_ref.shape[0]):
                g = g + l_ref[s].astype(F32)
        delta, mn, vn = _adamw_math(w_ref[...], g, m_ref[...], v_ref[...])
        g_out[...] = g
        d_out[...] = delta
        m_out[...] = mn
        v_out[...] = vn
        carried.after(job_ins, job_outs, job_sems)

    spec = pl.BlockSpec((tr, tc), lambda i, q_ref: at(i))
    own_spec = pl.BlockSpec((None, tr, tc), lambda i, q_ref: (q_ref[0], *at(i)))
    l_specs = [pl.BlockSpec((l.shape[0], tr, tc), lambda i, q_ref: (0, *at(i))) for l in landed]
    return pl.pallas_call(
        body, name=name,
        grid_spec=pltpu.PrefetchScalarGridSpec(
            num_scalar_prefetch=1, grid=(tiles,),
            in_specs=[own_spec] + l_specs + [spec, spec, spec] + carried.in_specs,
            out_specs=[spec] * 4 + carried.out_specs, scratch_shapes=carried.sems),
        out_shape=[jax.ShapeDtypeStruct((R, C), F32)] * 4 + carried.out_shape,
        input_output_aliases={1 + i: o for i, o in carried.aliases(nl + 4, 4).items()},
        compiler_params=_params(1),
    )(my_chip, chip_sums, *landed, w, m, v, *carried.operands)


def _adamw_small(landed, w, m, v):
    P = w.shape[1]

    def body(l_ref, w_ref, m_ref, v_ref, g_out, d_out, m_out, v_out):
        acc = l_ref[0]
        for s in range(1, N_DEV):
            acc = acc + l_ref[s]
        g = jnp.sum(acc, axis=0, keepdims=True)
        delta, mn, vn = _adamw_math(w_ref[...], g, m_ref[...], v_ref[...])
        g_out[...] = g
        d_out[...] = delta
        m_out[...] = mn
        v_out[...] = vn

    return pl.pallas_call(
        body, name="adamw_replicated", out_shape=[jax.ShapeDtypeStruct((1, P), F32)] * 4,
    )(landed, w, m, v)


def _rope_rows():
    inv_r = ROPE_BASE ** (-jnp.arange(0, RET_DK, 2, dtype=F32) / RET_DK)
    inv_m = ROPE_BASE ** (-jnp.arange(0, MLA_ROPE, 2, dtype=F32) / MLA_ROPE)
    h = MLA_ROPE // 2
    one, zero = jnp.ones((h,), F32), jnp.zeros((h,), F32)
    pad = jnp.zeros((LANE - MLA_ROPE,), F32)
    rows = [jnp.concatenate([inv_r, inv_r]), jnp.concatenate([-jnp.ones((RET_DK // 2,), F32), jnp.ones((RET_DK // 2,), F32)]),
            jnp.concatenate([inv_m, inv_m, pad]), jnp.concatenate([one, one, pad]),
            jnp.concatenate([zero, one, pad]), jnp.concatenate([-one, zero, pad])]
    return jnp.stack(rows + [jnp.zeros((LANE,), F32)] * (SUBLANE - len(rows)))


def _prep_tile(xt, p, rows):
    ang2, angp = p * rows[0:1], p * rows[2:3]
    sp = jnp.sin(angp)
    return (xt, jnp.cos(ang2), jnp.sin(ang2) * rows[1:2], jnp.cos(angp) * rows[3:4], sp * rows[4:5], sp * rows[5:6])


def kernel(x, positions, w_in, q_norm_g, w_uq, kv_norm_g, w_uk, w_uv, ret_gn_g, w_out, ln1_g, ln1_b, w_up, w_down, ln2_g, ln2_b, loss_target, m_w_in, m_q_norm_g, m_w_uq, m_kv_norm_g, m_w_uk, m_w_uv, m_ret_gn_g, m_w_out, m_ln1_g, m_ln1_b, m_w_up, m_w_down, m_ln2_g, m_ln2_b, v_w_in, v_q_norm_g, v_w_uq, v_kv_norm_g, v_w_uk, v_w_uv, v_ret_gn_g, v_w_out, v_ln1_g, v_ln1_b, v_w_up, v_w_down, v_ln2_g, v_ln2_b):
    T, D = x.shape[1], x.shape[2]
    H = D // 256
    RW = H * RET_DV
    x = x[0]
    tgt = loss_target[0]
    alpha = DEEPNORM_ALPHA
    in_width = w_in.shape[2] * N_DEV
    mla_in = in_width - 4 * RW
    mla_in_pad = -(-mla_in // 512) * 512
    f_uq = MLA_NOPE + MLA_ROPE

    names = ["w_in", "w_uq", "w_uk", "w_uv", "w_out", "w_up", "w_down"]
    big = dict(zip(names, [w_in[0].T, w_uq[0], w_uk[0], w_uv[0], w_out[0], w_up[0], w_down[0]]))
    kind = dict(zip(names, ["blk", "blk", "cols", "cols", "rows", "cols", "rows"]))
    half = {n: w.astype(BF16) for n, w in big.items()}

    def gather(ns, **parts):
        return _GatherJob([half[n] for n in ns], [kind[n] for n in ns], 0.55, 0.9, **parts)

    tr = min(256, T)
    xb, cos2, sin2, cosp, sina, sinb, g_in = _rowwise(
        "prep", _prep_tile, [(x, D, 0), (positions[0].astype(F32)[:, None], 1, 0)], [_rope_rows()],
        [(D, BF16)] + [(LANE, F32)] * 5, [], tr, jobs=[gather(["w_in"])])
    win = g_in.reshape(in_width, D)
    w1, n1 = win, 4 * RW
    w2 = jnp.pad(win[4 * RW:], ((0, mla_in_pad - mla_in), (0, 0)))

    first, second, near, far = (0, 2), (1, 2), (CHIP_X, CHIP_Y), (CHIP_DIAGONAL,)
    p1, g_uq, wuk, wuv, wout = _matmul("proj_ret", xb, w1, tb=True, n=n1,
                                       jobs=[gather(["w_uq", "w_uk", "w_uv", "w_out"])])
    wuq = jnp.transpose(g_uq, (1, 0, 2)).reshape(MLA_Q_RANK, H, f_uq)
    wuq = jnp.pad(wuq, ((0, 0), (0, 0), (0, MLA_HEAD_PAD - f_uq))).reshape(MLA_Q_RANK, H * MLA_HEAD_PAD)
    wukv = jnp.concatenate([wuk, wuv], axis=1)
    p2 = _matmul("proj_mla", xb, w2, tb=True)
    c_kv, c_kr = MLA_Q_RANK, MLA_Q_RANK + MLA_KV_RANK

    def mla_prep(p, cp, sa, sb, gq, gkv):
        cqn = _rms(p[:, :c_kv], gq)
        ckvn = _rms(p[:, c_kv:c_kr], gkv)
        return cqn, ckvn, _rope64(p[:, c_kr:c_kr + LANE], cp, sa, sb)

    cqn, ckvn, krope = _rowwise(
        "mla_prep", mla_prep, [(p2, mla_in_pad, 0), (cosp, LANE, 0), (sina, LANE, 0), (sinb, LANE, 0)],
        [q_norm_g, kv_norm_g], [(MLA_Q_RANK, BF16), (MLA_KV_RANK, BF16), (LANE, BF16)], [], tr)
    qf = _matmul("mla_q", cqn, wuq)
    kv = _matmul("mla_kv", ckvn, wukv, out_dtypes=(BF16,))

    def mla_heads(q, kn, vv, kr, cp, sa, sb):
        qs, ks, vs = [], [], []
        ones = jnp.ones((q.shape[0], MLA_DV), vv.dtype)
        for h in range(H):
            o = h * MLA_HEAD_PAD
            qs += [q[:, o:o + MLA_NOPE], _rope64(q[:, o + MLA_NOPE:o + MLA_HEAD_PAD], cp, sa, sb)]
            ks += [kn[:, h * MLA_NOPE:(h + 1) * MLA_NOPE], kr]
            vs += [vv[:, h * MLA_DV:(h + 1) * MLA_DV], ones]
        return jnp.concatenate(qs, axis=1), jnp.concatenate(ks, axis=1), jnp.concatenate(vs, axis=1)

    qh, kh, vx = _rowwise(
        "mla_heads", mla_heads,
        [(qf, H * MLA_HEAD_PAD, 0), (kv, RW, 0), (kv, RW, 1), (krope, LANE, 0), (cosp, LANE, 0), (sina, LANE, 0), (sinb, LANE, 0)],
        [], [(H * MLA_HEAD_PAD, BF16)] * 3, [], tr)
    whole = (0, 1)
    att_o, mix_half, lse, wup_a = _attention_fwd(
        qh, kh, vx, H, jobs=[gather(["w_up"], pieces=[(first, near, True, 0.45), (second, near, True, 0.75)])])
    ret_pre, mixin, wup = _retention_fwd(
        p1, cos2, sin2, ret_gn_g, mix_half, H,
        jobs=[gather(["w_up"], pieces=[(first, far, False, 0.55), (second, far, False, 0.9)], base=[wup_a])])
    mix, wdown_a = _matmul("mix_out", mixin, wout,
                           jobs=[gather(["w_down"], pieces=[((0, 4), near, True, 0.5), ((1, 4), near, True, 0.9)])])

    def ln1_fwd(xt, mt, g, b):
        z = alpha * xt + mt
        xhat, _ = _ln_stats(z)
        y = xhat * g + b
        return z, y, y

    z1, x1, x1b, wdown_b = _rowwise("ln1", ln1_fwd, [(x, D, 0), (mix, D, 0)], [ln1_g, ln1_b],
                                    [(D, F32), (D, F32), (D, BF16)], [], tr,
                                    jobs=[gather(["w_down"], pieces=[((2, 4), near, True, 0.5), ((3, 4), near, True, 0.9)],
                                                 base=[wdown_a])])

    def relu2(acc):
        r = jnp.maximum(acc, 0.0)
        return r * r, r

    a2b, rb, wdown = _matmul("mlp_up", x1b, wup, out_dtypes=(BF16, BF16), epilogue=relu2,
                             jobs=[gather(["w_down"], pieces=[(whole, far, False, 0.6)], base=[wdown_b])])
    hmlp = _matmul("mlp_down", a2b, wdown)

    def ln2_loss(x1t, ht, tt, g, b):
        xhat, rstd = _ln_stats(alpha * x1t + ht)
        err = xhat * g + b - tt
        dy = err / D
        dz = _ln_bwd(dy, xhat, rstd, g)
        lrow = 0.5 * jnp.mean(jnp.square(err), axis=-1, keepdims=True)
        return dz, dz, dy * xhat, dy, jnp.broadcast_to(lrow, (lrow.shape[0], LANE))

    dz2, dz2b, dg2, db2, lpart = _rowwise("ln2_loss", ln2_loss, [(x1, D, 0), (hmlp, D, 0), (tgt, D, 0)], [ln2_g, ln2_b],
                                          [(D, F32), (D, BF16)], [D, D, LANE], tr)
    loss = lax.psum(jnp.sum(lpart[:, 0]), ("x", "y", "c"))

    def pair(grads):
        ns = list(grads)
        return _PairJob([grads[n] for n in ns], [kind[n] for n in ns], [big[n].shape for n in ns])

    my_core = lax.axis_index("c").astype(jnp.int32).reshape(1)
    my_chip = (2 * lax.axis_index("x") + lax.axis_index("y")).astype(jnp.int32).reshape(1)
    chip_sums = {}

    def pair_sums(grads, theirs):
        for n, t in zip(grads, theirs):
            chip_sums[n] = _pair_sum("pair_sum_" + n, grads[n], kind[n], big[n].shape, t, my_core)
        return {n: chip_sums[n] for n in grads}

    def chip(sums, rels=CHIP_RELS, by_source=None):
        ns = list(sums)
        return _ChipJob([sums[n] for n in ns], [big[n].shape for n in ns], rels, by_source)

    landed = {}
    da = _matmul("mlp_down_dx", dz2b, wdown, tb=True, out_dtypes=(BF16,), extras=(rb,),
                 epilogue=lambda acc, r: (acc * (2.0 * r.astype(F32)),))
    g_down = {"w_down": _matmul("mlp_down_dw", a2b, dz2b, ta=True, out_dtypes=(BF16,))}
    g_wup, *pr = _matmul("mlp_up_dw", x1b, da, ta=True, out_dtypes=(BF16,), jobs=[pair(g_down)])
    sums_down = pair_sums(g_down, pr)
    g_up = {"w_up": g_wup}
    dx1m, l_a, *pr = _matmul("mlp_up_dx", da, wup, tb=True, jobs=[chip(sums_down, CHIP_RELS[:2]), pair(g_up)])
    sums_up = pair_sums(g_up, pr)

    def ln1_bwd(dm, dzz, z, g):
        xhat, rstd = _ln_stats(z)
        dy = dm + alpha * dzz
        dz = _ln_bwd(dy, xhat, rstd, g)
        return dz, dz, dy * xhat, dy

    dz1, dz1b, dg1, db1 = _rowwise("ln1_bwd", ln1_bwd, [(dx1m, D, 0), (dz2, D, 0), (z1, D, 0)], [ln1_g],
                                   [(D, F32), (D, BF16)], [D, D], tr)
    g_wout = _matmul("mix_out_dw", mixin, dz1b, ta=True, out_dtypes=(BF16,))
    dmix = _matmul("mix_out_dx", dz1b, wout, tb=True)

    dqh, dkn, dkr, dvv, l_b, l_c = _attention_bwd(qh, kh, vx, att_o, lse, dmix, H,
                                                  jobs=[chip(sums_down, CHIP_RELS[2:]), chip(sums_up)])
    landed["w_down"], landed["w_up"] = [l_a, l_b], [l_c]
    att_scale = (MLA_NOPE + MLA_ROPE) ** -0.5

    def mla_heads_bwd(dq, dkrh, cp, sa, sb):
        parts, acc = [], dkrh[:, :LANE]
        dq = dq * att_scale
        for h in range(H):
            o = h * MLA_HEAD_PAD
            parts += [dq[:, o:o + MLA_NOPE], _unrope64(dq[:, o + MLA_NOPE:o + MLA_HEAD_PAD], cp, sa, sb)]
            if h:
                acc = acc + dkrh[:, h * LANE:(h + 1) * LANE]
        return jnp.concatenate(parts, axis=1), _unrope64(acc, cp, sa, sb)

    dqb, dkr128 = _rowwise(
        "mla_heads_bwd", mla_heads_bwd,
        [(dqh, H * MLA_HEAD_PAD, 0), (dkr, H * LANE, 0), (cosp, LANE, 0), (sina, LANE, 0), (sinb, LANE, 0)],
        [], [(H * MLA_HEAD_PAD, BF16), (LANE, F32)], [], tr)
    dcqn = _matmul("mla_q_dx", dqb, wuq, tb=True)
    g_wuq = _matmul("mla_q_dw", cqn, dqb, ta=True, out_dtypes=(BF16,))
    dkvb = jnp.concatenate([dkn, dvv], axis=1)
    dckvn = _matmul("mla_kv_dx", dkvb, wukv, tb=True)
    g_wukv = _matmul("mla_kv_dw", ckvn, dkvb, ta=True, out_dtypes=(BF16,))

    def rms_bwd(c, dy, g):
        rstd = lax.rsqrt(jnp.mean(jnp.square(c), axis=-1, keepdims=True) + EPS)
        dyg = dy * g
        dc = rstd * (dyg - c * (rstd * rstd) * jnp.mean(dyg * c, axis=-1, keepdims=True))
        return dc, dy * c * rstd

    def mla_prep_bwd(p, dq_, dkv_, dkr_, gq, gkv):
        dcq, dgq_ = rms_bwd(p[:, :c_kv], dq_, gq)
        dckv, dgkv_ = rms_bwd(p[:, c_kv:c_kr], dkv_, gkv)
        pad = jnp.zeros((p.shape[0], mla_in_pad - c_kr - LANE), F32)
        return jnp.concatenate([dcq, dckv, dkr_, pad], axis=1), dgq_, dgkv_

    d_p2, dgq, dgkv = _rowwise(
        "mla_prep_bwd", mla_prep_bwd, [(p2, mla_in_pad, 0), (dcqn, MLA_Q_RANK, 0), (dckvn, MLA_KV_RANK, 0), (dkr128, LANE, 0)],
        [q_norm_g, kv_norm_g], [(mla_in_pad, BF16)], [MLA_Q_RANK, MLA_KV_RANK], tr)

    def gate_bwd(pre, gate, dout, gn):
        dpre, dgate, dgn = [], [], []
        for h in range(H):
            sl = slice(h * RET_DV, (h + 1) * RET_DV)
            nrm, rstd = _ln_stats(pre[:, sl])
            gate_h, do_h, gn_h = gate[:, sl], dout[:, sl], gn[:, sl]
            sg = jax.nn.sigmoid(gate_h)
            silu = gate_h * sg
            dgate.append(do_h * (nrm * gn_h) * (sg * (1.0 + gate_h * (1.0 - sg))))
            dn = do_h * silu
            dgn.append(dn * nrm)
            dpre.append(_ln_bwd(dn, nrm, rstd, gn_h))
        return jnp.concatenate(dpre, axis=1), jnp.concatenate(dgate, axis=1), jnp.concatenate(dgn, axis=1)

    dpre, drg, dgn = _rowwise("ret_gate_bwd", gate_bwd, [(ret_pre, RW, 0), (p1, RW, 3), (dmix, RW, 0)], [ret_gn_g],
                              [(RW, F32), (RW, BF16)], [RW], tr)
    g_uq = g_wuq.reshape(MLA_Q_RANK, H, MLA_HEAD_PAD)[:, :, :f_uq].reshape(MLA_Q_RANK, N_DEV, H * f_uq // N_DEV)
    mid_grads = {"w_uq": jnp.transpose(g_uq, (1, 0, 2)), "w_uk": g_wukv[:, :RW], "w_uv": g_wukv[:, RW:], "w_out": g_wout}
    (drq, *pr), (drk, drv) = _retention_bwd(p1, dpre, cos2, sin2, H, dq_jobs=[pair(mid_grads)])
    sums_mid = pair_sums(mid_grads, pr)
    d_p1 = jnp.concatenate([drq, drk, drv, drg], axis=1)

    g_w1, *l_a = _matmul("proj_ret_dw", d_p1, xb, ta=True, out_dtypes=(BF16,), jobs=[chip(sums_mid)])
    for n, a_ in zip(mid_grads, l_a):
        landed[n] = [a_]
    g_w2 = _matmul("proj_mla_dw", d_p2, xb, ta=True, out_dtypes=(BF16,))

    g_win = {"w_in": jnp.concatenate([g_w1, g_w2[:mla_in]], axis=0).reshape(N_DEV, in_width // N_DEV, D)}
    small = jnp.concatenate([dgq, dgkv, dgn, dg1, db1, dg2, db2], axis=1)
    dx_a, *pr = _matmul("proj_mla_dx", d_p2, w2, extras=(dz1,), epilogue=lambda acc, d: (acc + alpha * d,), jobs=[pair(g_win)])
    sums_in = pair_sums(g_win, pr)
    grad_x, l_a, small_landed = _matmul("proj_ret_dx", d_p1, w1, extras=(dx_a,), epilogue=lambda acc, d: (acc + d,),
                                        jobs=[chip(sums_in, CHIP_RELS[:2], by_source=small)])

    moments = dict(zip(names, zip([m_w_in, m_w_uq, m_w_uk, m_w_uv, m_w_out, m_w_up, m_w_down],
                                  [v_w_in, v_w_uq, v_w_uk, v_w_uv, v_w_out, v_w_up, v_w_down])))
    res = {}

    def adamw(n, jobs=()):
        as_held = (lambda a: a[0].T) if n == "w_in" else (lambda a: a[0])
        as_given = (lambda r: r.T[None]) if n == "w_in" else (lambda r: r[None])
        out = _adamw("adamw_" + n, chip_sums[n], my_chip, landed[n], big[n], as_held(moments[n][0]), as_held(moments[n][1]),
                     jobs=jobs)
        res[n] = [as_given(r) for r in out[:4]]
        return out[4:]

    (l_b,) = adamw("w_out", jobs=[chip(sums_in, CHIP_RELS[2:])])
    landed["w_in"] = [l_a, l_b]
    for n in names:
        if n != "w_out":
            adamw(n)

    small_names = ["q_norm_g", "kv_norm_g", "ret_gn_g", "ln1_g", "ln1_b", "ln2_g", "ln2_b"]
    small_w = [q_norm_g, kv_norm_g, ret_gn_g, ln1_g, ln1_b, ln2_g, ln2_b]
    small_m = [m_q_norm_g, m_kv_norm_g, m_ret_gn_g, m_ln1_g, m_ln1_b, m_ln2_g, m_ln2_b]
    small_v = [v_q_norm_g, v_kv_norm_g, v_ret_gn_g, v_ln1_g, v_ln1_b, v_ln2_g, v_ln2_b]
    cat = lambda arrs: jnp.concatenate(arrs, axis=1)
    sres = _adamw_small(small_landed, cat(small_w), cat(small_m), cat(small_v))
    off = 0
    for n, w in zip(small_names, small_w):
        res[n] = [r[:, off:off + w.shape[1]] for r in sres]
        off += w.shape[1]

    order = ["w_in", "q_norm_g", "w_uq", "kv_norm_g", "w_uk", "w_uv", "ret_gn_g", "w_out", "ln1_g", "ln1_b",
             "w_up", "w_down", "ln2_g", "ln2_b"]
    outs = [loss, grad_x[None]]
    for k in range(4):
        outs += [res[n][k] for n in order]
    return tuple(outs)
```

```python
import functools

import numpy as np

import jax
import jax.numpy as jnp
from jax import lax
from jax.experimental import pallas as pl
from jax.experimental.pallas import tpu as pltpu

F32 = jnp.float32
BF16 = jnp.bfloat16

CHUNK = 64
RET_DK = 128
RET_DV = 128
MLA_NOPE = 128
MLA_ROPE = 64
MLA_DV = 128
MLA_Q_RANK = 768
MLA_KV_RANK = 512
MLA_HEAD_PAD = 256
ROPE_BASE = 10000.0
EPS = 1e-5
DEPTH = 1
DEEPNORM_ALPHA = (2.0 * DEPTH) ** 0.25
ADAM_LR = 0.001
ADAM_B1 = 0.9
ADAM_B2 = 0.999
ADAM_EPS = 1e-08
ADAM_WD = 0.01
ADAM_STEP = 10

N_DEV = 8
LANE = 128
SUBLANE = 8
VMEM_LIMIT = 48 * 1024 * 1024
ADAMW_TILE_ELEMS = 128 * 1024
PAIR_TILE_ELEMS = 1024 * 1024
DW_TK = 4096

MESH = pl.DeviceIdType.MESH
ANY = pl.BlockSpec(memory_space=pl.ANY)

NN = (((1,), (0,)), ((), ()))
NT = (((1,), (1,)), ((), ()))
TN = (((0,), (0,)), ((), ()))


def _dot(a, b, dims=NN):
    return lax.dot_general(a.astype(BF16), b.astype(BF16), dims, preferred_element_type=F32)


def _params(n_axes):
    return pltpu.CompilerParams(dimension_semantics=("arbitrary",) * n_axes, vmem_limit_bytes=VMEM_LIMIT)


def _tile(n, want):
    if n <= want:
        return n
    t = want
    while t >= LANE:
        if n % t == 0:
            return t
        t -= LANE
    return n


def _my_place():
    return lax.axis_index("x"), lax.axis_index("y"), lax.axis_index("c")


def _flat(px, py, pc):
    return 4 * px + 2 * py + pc


def _block_of(ref, kind, idx, rows, cols, sub=None):
    lo, n = (0, rows) if sub is None else sub
    if kind == "blk":
        return ref.at[idx, pl.ds(lo, n), :]
    if kind == "rows":
        return ref.at[pl.ds(pl.multiple_of(idx * rows + lo, n), n), :]
    return ref.at[pl.ds(lo, n), pl.ds(pl.multiple_of(idx * cols, cols), cols)]


CHIP_X, CHIP_Y, CHIP_DIAGONAL = 0, 1, 2
ALL_CHIPS = (CHIP_X, CHIP_Y, CHIP_DIAGONAL)


class _GatherJob:
    def __init__(self, shards, kinds, mid, late, pieces=(((0, 1), ALL_CHIPS, True),), base=None):
        nw = len(shards)
        self.nw, self.kinds = nw, list(kinds)
        self.pieces = [tuple(p) if len(p) == 4 else (*p, late) for p in pieces]
        for _, chips, _, _ in self.pieces:
            assert CHIP_DIAGONAL not in chips or len(chips) in (1, 3)
        self.shapes = [s.shape for s in shards]
        self.ins = list(shards) + (list(base) if base is not None else [])
        self.alias = [(nw + w, w) for w in range(nw)] if base is not None else []
        self.out_shape = []
        for s, kind in zip(shards, kinds):
            r, c = s.shape
            shape = {"blk": (N_DEV, r, c), "rows": (N_DEV * r, c), "cols": (r, N_DEV * c)}[kind]
            self.out_shape.append(jax.ShapeDtypeStruct(shape, s.dtype))
        n = 7 * nw * len(self.pieces)
        self.sems = [pltpu.SemaphoreType.DMA((n,)), pltpu.SemaphoreType.DMA((n,)), pltpu.SemaphoreType.DMA((n,))]
        whens = sorted({p[3] for p in self.pieces})
        self.phases = ([(0.0, self._start), (mid, self._relay)] + [(f, functools.partial(self._pass_on, f)) for f in whens]
                       + [(1.0, self._finish)])

    def _each(self, src, dst, send_sems, recv_sems, local_sems):
        x, y, c = _my_place()
        all_chips = [(1 - x, y), (x, 1 - y), (1 - x, 1 - y)]
        out = []
        for w in range(self.nw):
            r, cc = self.shapes[w]
            for p, ((k_of, of), chips, own, when) in enumerate(self.pieces):
                sub = (k_of * (r // of), r // of)
                s0 = 7 * (w * len(self.pieces) + p)
                mine = src[w].at[pl.ds(*sub), :]

                def place(dev, w=w, r=r, cc=cc, sub=sub):
                    return _block_of(dst[w], self.kinds[w], _flat(*dev), r, cc, sub)

                def copy(k, block, to, from_src=False, place=place, mine=mine, s0=s0):
                    return pltpu.make_async_remote_copy(
                        src_ref=mine if from_src else place(block), dst_ref=place(block),
                        send_sem=send_sems.at[s0 + k], recv_sem=recv_sems.at[s0 + k], device_id=to, device_id_type=MESH)

                relay = functools.partial(copy, 1 + CHIP_DIAGONAL, (x ^ (1 - c), y ^ c, c), (x ^ c, y ^ (1 - c), c))
                local = functools.partial(pltpu.make_async_copy, mine, place((x, y, c)), local_sems.at[s0])
                near = [(j, all_chips[j]) for j in chips if j != CHIP_DIAGONAL]
                out.append((near, CHIP_DIAGONAL in chips, own, copy, relay, local, when))
        return (x, y, c), (x, y, 1 - c), all_chips[CHIP_DIAGONAL], out

    def _start(self, *refs):
        me, sibling, diagonal, each = self._each(*refs)
        for near, far, own, copy, relay, local, when in each:
            if own:
                local().start()
                copy(0, me, sibling, from_src=True).start()
            for j, chip in near:
                copy(1 + j, me, (*chip, me[2]), from_src=True).start()
            if far and not near:
                relay().start()

    def _landed(self, near, copy, me, sibling):
        for j, chip in near:
            copy(1 + j, (*chip, me[2]), me).wait_recv()
            copy(4 + j, (*chip, me[2]), sibling).start()

    def _relay(self, *refs):
        me, sibling, diagonal, each = self._each(*refs)
        for near, far, own, copy, relay, local, when in each:
            if far and near:
                self._landed(near, copy, me, sibling)
                relay().start()

    def _pass_on(self, frac, *refs):
        me, sibling, diagonal, each = self._each(*refs)
        for near, far, own, copy, relay, local, when in each:
            if when == frac:
                self._landed([(CHIP_DIAGONAL, diagonal)] if far else near, copy, me, sibling)

    def _finish(self, *refs):
        me, sibling, diagonal, each = self._each(*refs)
        for near, far, own, copy, relay, local, when in each:
            if own:
                copy(0, sibling, me).wait_recv()
            for j, chip in near + [(CHIP_DIAGONAL, diagonal)] * int(far):
                copy(4 + j, (*chip, 1 - me[2]), me).wait_recv()
        for near, far, own, copy, relay, local, when in each:
            if own:
                copy(0, me, sibling, from_src=True).wait_send()
                local().wait()
            for j, chip in near:
                copy(1 + j, me, (*chip, me[2]), from_src=True).wait_send()
                copy(4 + j, (*chip, me[2]), sibling).wait_send()
            if far:
                relay().wait_send()
                copy(4 + CHIP_DIAGONAL, (*diagonal, me[2]), sibling).wait_send()


REL_ALL = [(1, 0, 0), (1, 0, 1), (0, 1, 0), (0, 1, 1), (1, 1, 0), (1, 1, 1), (0, 0, 1)]
N_CHIP = 4


class _PairJob:
    def __init__(self, grads, kinds, shard_shapes):
        self.ins = list(grads)
        self.kinds, self.shard_shapes, self.nw = list(kinds), list(shard_shapes), len(grads)
        self.out_shape = [jax.ShapeDtypeStruct((N_CHIP,) + tuple(s), g.dtype) for g, s in zip(grads, shard_shapes)]
        n = N_CHIP * self.nw
        self.sems = [pltpu.SemaphoreType.DMA((n,)), pltpu.SemaphoreType.DMA((n,))]
        self.phases = [(0.0, self._start), (1.0, self._finish)]

    def _copies(self, src, theirs, send_sems, recv_sems):
        x, y, c = _my_place()
        remote = []
        for w in range(self.nw):
            r, cc = self.shard_shapes[w]
            for q in range(N_CHIP):
                s = N_CHIP * w + q
                remote.append(pltpu.make_async_remote_copy(
                    src_ref=_block_of(src[w], self.kinds[w], 2 * q + (1 - c), r, cc), dst_ref=theirs[w].at[q],
                    send_sem=send_sems.at[s], recv_sem=recv_sems.at[s], device_id=(x, y, 1 - c), device_id_type=MESH))
        return remote

    def _start(self, *refs):
        for cp in self._copies(*refs):
            cp.start()

    def _finish(self, *refs):
        for cp in self._copies(*refs):
            cp.wait()


CHIP_RELS = [(1, 0), (0, 1), (1, 1)]


class _ChipJob:
    def __init__(self, chip_sums, shard_shapes, rels=CHIP_RELS, by_source=None, piece=(0, 1), base=None):
        nw = len(chip_sums)
        self.ins = list(chip_sums) + ([by_source] if by_source is not None else []) + (list(base) if base is not None else [])
        self.nw, self.has_src, self.rels, self.piece = nw, by_source is not None, list(rels), piece
        self.rows = [s[0] for s in shard_shapes]
        self.alias = [(nw + int(self.has_src) + w, w) for w in range(nw)] if base is not None else []
        self.out_shape = [jax.ShapeDtypeStruct((len(rels),) + tuple(s), a.dtype) for a, s in zip(chip_sums, shard_shapes)]
        if self.has_src:
            self.out_shape.append(jax.ShapeDtypeStruct((N_DEV,) + by_source.shape, by_source.dtype))
        n = len(rels) * self.nw + len(REL_ALL) * int(self.has_src)
        self.sems = [pltpu.SemaphoreType.DMA((n,)), pltpu.SemaphoreType.DMA((n,)), pltpu.SemaphoreType.DMA((1,))]
        self.phases = [(0.0, self._start), (1.0, self._finish)]

    def _copies(self, src, land, send_sems, recv_sems, local_sems):
        x, y, c = _my_place()
        nw = self.nw

        def remote_copy(s, src_ref, dst_ref, peer):
            return pltpu.make_async_remote_copy(src_ref=src_ref, dst_ref=dst_ref, send_sem=send_sems.at[s],
                                                recv_sem=recv_sems.at[s], device_id=peer, device_id_type=MESH)

        def rows(w):
            n = self.rows[w] // self.piece[1]
            return pl.ds(self.piece[0] * n, n)

        local, remote = [], []
        for k, (fx, fy) in enumerate(self.rels):
            px, py = x ^ fx, y ^ fy
            remote += [remote_copy(nw * k + w, src[w].at[2 * px + py, rows(w), :], land[w].at[k, rows(w), :], (px, py, c))
                       for w in range(nw)]
        if self.has_src:
            my_idx = _flat(x, y, c)
            local.append(pltpu.make_async_copy(src[nw], land[nw].at[my_idx], local_sems.at[0]))
            remote += [remote_copy(nw * len(self.rels) + k, src[nw], land[nw].at[my_idx], (x ^ fx, y ^ fy, c ^ fc))
                       for k, (fx, fy, fc) in enumerate(REL_ALL)]
        return local, remote

    def _start(self, *refs):
        local, remote = self._copies(*refs)
        for cp in local + remote:
            cp.start()

    def _finish(self, *refs):
        local, remote = self._copies(*refs)
        for cp in remote + local:
            cp.wait()


def _row_tile(R, C, elems=ADAMW_TILE_ELEMS):
    tr = 16
    while tr * 2 * C <= elems and R % (tr * 2) == 0:
        tr *= 2
    return tr if R % tr == 0 else R


BF16_ROWS = 16


def _tile_2d(R, C, elems):
    if R % BF16_ROWS == 0:
        tr = _row_tile(R, C, elems)
        return tr, C, R // tr
    tc = C
    while tc % (2 * LANE) == 0 and R * tc > elems:
        tc //= 2
    return R, tc, C // tc


def _tile_index(tr, tc, C):
    return (lambda i: (i, 0)) if tc == C else (lambda i: (0, i))


def _part_spec(kind, tr, tc, R, C, n_lead):
    at = _tile_index(tr, tc, C)

    def dev(ids, pref):
        return 2 * ids[0] + pref[0] if n_lead == 2 else pref[0]

    def index(*args):
        ids, pref = args[:n_lead], args[n_lead]
        ri, ci = at(ids[-1])
        if kind == "blk":
            return dev(ids, pref), ri, ci
        if kind == "rows":
            return dev(ids, pref) * (R // tr) + ri, ci
        return ri, dev(ids, pref) * (C // tc) + ci

    return pl.BlockSpec((None, tr, tc) if kind == "blk" else (tr, tc), index)


def _pair_sum(name, grad, kind, shard_shape, theirs, c):
    R, C = shard_shape
    tr, tc, tiles = _tile_2d(R, C, PAIR_TILE_ELEMS)
    at = _tile_index(tr, tc, C)

    def body(c_ref, g_ref, t_ref, o_ref):
        o_ref[...] = (g_ref[...].astype(F32) + t_ref[...].astype(F32)).astype(o_ref.dtype)

    slot = pl.BlockSpec((None, tr, tc), lambda q, i, c_ref: (q, *at(i)))
    return pl.pallas_call(
        body, name=name,
        grid_spec=pltpu.PrefetchScalarGridSpec(
            num_scalar_prefetch=1, grid=(N_CHIP, tiles),
            in_specs=[_part_spec(kind, tr, tc, R, C, 2), slot], out_specs=slot),
        out_shape=jax.ShapeDtypeStruct(theirs.shape, theirs.dtype), compiler_params=_params(2))(c, grad, theirs)


class _Carried:
    def __init__(self, jobs, grid):
        self.jobs = list(jobs)
        self.steps = int(np.prod(grid))
        self.grid = tuple(grid)
        self.operands = [a for j in self.jobs for a in j.ins]
        self.out_shape = [o for j in self.jobs for o in j.out_shape]
        self.sems = [s for j in self.jobs for s in j.sems]
        self.in_specs = [ANY] * len(self.operands)
        self.out_specs = [ANY] * len(self.out_shape)

    def aliases(self, n_in, n_out):
        out, i0, o0 = {}, n_in, n_out
        for job in self.jobs:
            for ji, jo in getattr(job, "alias", []):
                out[i0 + ji] = o0 + jo
            i0, o0 = i0 + len(job.ins), o0 + len(job.out_shape)
        return out

    def _step(self):
        lin = pl.program_id(0)
        for a in range(1, len(self.grid)):
            lin = lin * self.grid[a] + pl.program_id(a)
        return lin

    def _run(self, in_refs, out_refs, sem_refs, last):
        lin = self._step()
        for job in self.jobs:
            ins, in_refs = in_refs[:len(job.ins)], in_refs[len(job.ins):]
            outs, out_refs = out_refs[:len(job.out_shape)], out_refs[len(job.out_shape):]
            sems, sem_refs = sem_refs[:len(job.sems)], sem_refs[len(job.sems):]
            for frac, fn in job.phases:
                if (frac >= 1.0) != last:
                    continue
                at = min(self.steps - 1, int(frac * (self.steps - 1)))
                pl.when(lin == at)(functools.partial(fn, ins, outs, *sems))

    def before(self, in_refs, out_refs, sem_refs):
        self._run(in_refs, out_refs, sem_refs, last=False)

    def after(self, in_refs, out_refs, sem_refs):
        self._run(in_refs, out_refs, sem_refs, last=True)


def _split_refs(refs, n_in, n_job_in, n_out, n_job_out, n_scratch):
    cuts = np.cumsum([0, n_in, n_job_in, n_out, n_job_out, n_scratch])
    return [refs[cuts[i]:cuts[i + 1]] for i in range(5)] + [refs[cuts[5]:]]


def _matmul(name, a, b, *, ta=False, tb=False, out_dtypes=(F32,), epilogue=None, extras=(), tm=1024, tn=1024, tk=2048, jobs=(),
            n=None):
    if ta:
        K, M = a.shape
    else:
        M, K = a.shape
    N = n if n is not None else (b.shape[0] if tb else b.shape[1])
    if ta:
        tk = max(tk, DW_TK)
    tm, tn, tk = _tile(M, tm), _tile(N, tn), _tile(K, tk)
    nk = K // tk
    grid = (M // tm, N // tn, nk)
    dims = TN if ta else (NT if tb else NN)
    n_ex, n_out = len(extras), len(out_dtypes)
    carried = _Carried(jobs, grid)

    def body(*refs):
        ins, job_ins, out_refs, job_outs, scratch, job_sems = _split_refs(
            refs, 2 + n_ex, len(carried.operands), n_out, len(carried.out_shape), int(nk > 1))
        a_ref, b_ref, ex_refs = ins[0], ins[1], ins[2:]
        k = pl.program_id(2)
        carried.before(job_ins, job_outs, job_sems)

        def finish(acc):
            res = (acc,) if epilogue is None else epilogue(acc, *[e[...] for e in ex_refs])
            for o_ref, r in zip(out_refs, res):
                o_ref[...] = r.astype(o_ref.dtype)

        if nk == 1:
            finish(_dot(a_ref[...], b_ref[...], dims))
        else:
            acc_ref = scratch[0]

            @pl.when(k == 0)
            def _():
                acc_ref[...] = _dot(a_ref[...], b_ref[...], dims)

            @pl.when(k > 0)
            def _():
                acc_ref[...] += _dot(a_ref[...], b_ref[...], dims)

            pl.when(k == nk - 1)(lambda: finish(acc_ref[...]))

        carried.after(job_ins, job_outs, job_sems)

    a_spec = pl.BlockSpec((tk, tm), lambda i, j, k: (k, i)) if ta else pl.BlockSpec((tm, tk), lambda i, j, k: (i, k))
    b_spec = pl.BlockSpec((tn, tk), lambda i, j, k: (j, k)) if tb else pl.BlockSpec((tk, tn), lambda i, j, k: (k, j))
    tile_spec = pl.BlockSpec((tm, tn), lambda i, j, k: (i, j))
    outs = pl.pallas_call(
        body,
        name=name,
        grid=grid,
        in_specs=[a_spec, b_spec] + [tile_spec] * n_ex + carried.in_specs,
        out_specs=[tile_spec] * n_out + carried.out_specs,
        out_shape=[jax.ShapeDtypeStruct((M, N), d) for d in out_dtypes] + carried.out_shape,
        scratch_shapes=[pltpu.VMEM((tm, tn), F32)] * int(nk > 1) + carried.sems,
        input_output_aliases=carried.aliases(2 + n_ex, n_out),
        compiler_params=_params(3),
    )(a, b, *extras, *carried.operands)
    return outs[0] if len(outs) == 1 else outs


def _rowwise(name, fn, row_in, const_in, row_out, acc_out, tm, jobs=()):
    T = row_in[0][0].shape[0]
    n_r, n_c, n_o, n_a = len(row_in), len(const_in), len(row_out), len(acc_out)
    carried = _Carried(jobs, (T // tm,))

    def body(*refs):
        in_refs, job_ins, own_outs, job_outs, _, job_sems = _split_refs(
            refs, n_r + n_c, len(carried.operands), n_o + n_a, len(carried.out_shape), 0)
        out_refs, acc_refs = own_outs[:n_o], own_outs[n_o:]
        carried.before(job_ins, job_outs, job_sems)
        res = fn(*[r[...] for r in in_refs])
        for o_ref, r in zip(out_refs, res[:n_o]):
            o_ref[...] = r.astype(o_ref.dtype)

        @pl.when(pl.program_id(0) == 0)
        def _():
            for a_ref in acc_refs:
                a_ref[...] = jnp.zeros_like(a_ref)

        for a_ref, r in zip(acc_refs, res[n_o:]):
            a_ref[...] += jnp.sum(r.reshape(tm // SUBLANE, SUBLANE, r.shape[-1]), axis=0)
        carried.after(job_ins, job_outs, job_sems)

    in_specs = [pl.BlockSpec((tm, w), functools.partial(lambda i, c: (i, c), c=cb)) for (_, w, cb) in row_in]
    in_specs += [pl.BlockSpec(c.shape, functools.partial(lambda i, nd: (0,) * nd, nd=c.ndim)) for c in const_in]
    out_specs = [pl.BlockSpec((tm, c), lambda i: (i, 0)) for (c, _) in row_out]
    out_specs += [pl.BlockSpec((SUBLANE, c), lambda i: (0, 0)) for c in acc_out]
    out_shape = [jax.ShapeDtypeStruct((T, c), d) for (c, d) in row_out]
    out_shape += [jax.ShapeDtypeStruct((SUBLANE, c), F32) for c in acc_out]
    return pl.pallas_call(
        body, name=name, grid=(T // tm,), in_specs=in_specs + carried.in_specs, out_specs=out_specs + carried.out_specs,
        out_shape=out_shape + carried.out_shape, scratch_shapes=carried.sems,
        input_output_aliases=carried.aliases(n_r + n_c, n_o + n_a), compiler_params=_params(1),
    )(*[r[0] for r in row_in], *const_in, *carried.operands)


def _ln_stats(z):
    mu = jnp.mean(z, axis=-1, keepdims=True)
    var = jnp.mean(jnp.square(z - mu), axis=-1, keepdims=True)
    rstd = lax.rsqrt(var + EPS)
    return (z - mu) * rstd, rstd


def _ln_bwd(dy, xhat, rstd, g):
    dxh = dy * g
    return rstd * (dxh - jnp.mean(dxh, axis=-1, keepdims=True) - xhat * jnp.mean(dxh * xhat, axis=-1, keepdims=True))


def _rms(x, g):
    return x * lax.rsqrt(jnp.mean(jnp.square(x), axis=-1, keepdims=True) + EPS) * g


def _rope128(x, cos2, sin2):
    return x * cos2 + pltpu.roll(x, RET_DK // 2, 1) * sin2


def _unrope128(dy, cos2, sin2):
    return dy * cos2 + pltpu.roll(dy * sin2, RET_DK // 2, 1)


def _rope64(x, cosp, sina, sinb):
    h = MLA_ROPE // 2
    return x * cosp + pltpu.roll(x, h, 1) * sina + pltpu.roll(x, LANE - h, 1) * sinb


def _unrope64(dy, cosp, sina, sinb):
    h = MLA_ROPE // 2
    return dy * cosp + pltpu.roll(dy * sina, LANE - h, 1) + pltpu.roll(dy * sinb, h, 1)


RET_BLOCK = 256
RET_HEADS_PER_STEP = 4


def _ret_tables(H, B):
    log_g = np.log1p(-np.exp2(-5.0 - np.arange(H, dtype=np.float32))).astype(np.float32)
    idx = np.arange(B, dtype=np.float32)
    dist = np.abs(idx[:, None] - idx[None, :])
    visible = (np.arange(B)[None, :] // CHUNK) <= (np.arange(B)[:, None] // CHUNK)
    dmat = np.exp(log_g[:, None, None] * dist).astype(np.float32) * visible[None].astype(np.float32)
    qd = np.exp(log_g[:, None] * (idx + 1.0)).astype(np.float32)
    kd = np.exp(log_g[:, None] * (B - 1 - idx)).astype(np.float32)
    sd = np.exp(log_g * B).astype(np.float32)
    bc = lambda v: np.ascontiguousarray(np.broadcast_to(v[:, :, None], (H, v.shape[1], LANE)))
    sdb = np.ascontiguousarray(np.broadcast_to(sd[:, None, None], (H, SUBLANE, LANE)))
    return jnp.asarray(dmat), jnp.asarray(bc(qd)), jnp.asarray(bc(kd)), jnp.asarray(sdb)


def _ret_specs(H, B, reverse, nb):
    blk = (lambda b: nb - 1 - b) if reverse else (lambda b: b)
    G = min(RET_HEADS_PER_STEP, H)
    W = G * LANE
    col = lambda off: pl.BlockSpec((B, W), functools.partial(lambda h, b, off: (blk(b), off // G + h), off=off))
    row = pl.BlockSpec((B, LANE), lambda h, b: (blk(b), 0))
    per_head = lambda r: pl.BlockSpec((G, r, LANE), lambda h, b: (h, 0, 0))
    dmat = pl.BlockSpec((G, B, B), lambda h, b: (h, 0, 0))
    return G, col, row, per_head, dmat


def _ret_call(name, body, n_in, n_out, operands, in_specs, out_specs, out_shape, grid, G, jobs, own_alias=None):
    carried = _Carried(jobs, grid)

    def wrapped(*refs):
        ins, job_ins, outs, job_outs, (state,), job_sems = _split_refs(
            refs, n_in, len(carried.operands), n_out, len(carried.out_shape), 1)
        carried.before(job_ins, job_outs, job_sems)

        @pl.when(pl.program_id(1) == 0)
        def _():
            state[...] = jnp.zeros_like(state)

        body(*ins, *outs, state)
        carried.after(job_ins, job_outs, job_sems)

    return pl.pallas_call(
        wrapped, name=name, grid=grid,
        in_specs=in_specs + carried.in_specs, out_specs=out_specs + carried.out_specs,
        out_shape=out_shape + carried.out_shape,
        scratch_shapes=[pltpu.VMEM((G, RET_DK, RET_DV), F32)] + carried.sems,
        input_output_aliases={**(own_alias or {}), **carried.aliases(n_in, n_out)},
        compiler_params=_params(2),
    )(*operands, *carried.operands)


def _retention_fwd(p1, cos2, sin2, gn_g, mix_in, H, jobs=()):
    T = p1.shape[0]
    B = min(RET_BLOCK, T)
    nb = T // B
    scale = RET_DK ** -0.5
    dmat, qd, kd, sd = _ret_tables(H, B)
    G, col, row, per_head, dspec = _ret_specs(H, B, False, nb)

    def body(q_ref, k_ref, v_ref, g_ref, cos_ref, sin_ref, d_ref, qd_ref, kd_ref, sd_ref, gn_ref, _, pre_ref, out_ref, s_ref):
        cos, sin = cos_ref[...], sin_ref[...]
        for g in range(G):
            hl = pl.ds(g * LANE, LANE)
            q = _rope128(q_ref[:, hl], cos, sin)
            k = _rope128(k_ref[:, hl], cos, sin) * scale
            v = v_ref[:, hl]
            s = s_ref[g]
            a = _dot(q, k, NT) * d_ref[g]
            ret = _dot(a, v) + _dot(q * qd_ref[g], s)
            s_ref[g] = sd_ref[g, 0:1, :] * s + _dot(k * kd_ref[g], v, TN)
            pre_ref[:, hl] = ret
            nrm, _ = _ln_stats(ret)
            out_ref[:, hl] = (jax.nn.silu(g_ref[:, hl]) * (nrm * gn_ref[:, hl])).astype(out_ref.dtype)

    out_spec = pl.BlockSpec((B, G * LANE), lambda h, b: (b, h))
    return _ret_call(
        "retention_fwd", body, 12, 2, (p1, p1, p1, p1, cos2, sin2, dmat, qd, kd, sd, gn_g, mix_in),
        [col(0), col(H), col(2 * H), col(3 * H), row, row, dspec, per_head(B), per_head(B), per_head(SUBLANE),
         pl.BlockSpec((1, G * LANE), lambda h, b: (0, h)), ANY],
        [out_spec, out_spec],
        [jax.ShapeDtypeStruct((T, H * RET_DV), F32), jax.ShapeDtypeStruct(mix_in.shape, mix_in.dtype)],
        (H // G, nb), G, jobs, own_alias={11: 1})


def _retention_bwd(p1, dpre, cos2, sin2, H, dq_jobs=(), dkv_jobs=()):
    T = p1.shape[0]
    B = min(RET_BLOCK, T)
    nb = T // B
    scale = RET_DK ** -0.5
    dmat, qd, kd, sd = _ret_tables(H, B)

    def dq_body(k_ref, v_ref, do_ref, cos_ref, sin_ref, d_ref, qd_ref, kd_ref, sd_ref, dq_ref, s_ref):
        cos, sin = cos_ref[...], sin_ref[...]
        for g in range(G):
            hl = pl.ds(g * LANE, LANE)
            k = _rope128(k_ref[:, hl], cos, sin) * scale
            v, do, s = v_ref[:, hl], do_ref[:, hl], s_ref[g]
            da = _dot(do, v, NT) * d_ref[g]
            dq = _dot(da, k) + _dot(do, s, NT) * qd_ref[g]
            dq_ref[:, hl] = _unrope128(dq, cos, sin).astype(dq_ref.dtype)
            s_ref[g] = sd_ref[g, 0:1, :] * s + _dot(k * kd_ref[g], v, TN)

    G, col, row, per_head, dspec = _ret_specs(H, B, False, nb)
    dq = _ret_call(
        "retention_bwd_dq", dq_body, 9, 1, (p1, p1, dpre, cos2, sin2, dmat, qd, kd, sd),
        [col(H), col(2 * H), col(0), row, row, dspec, per_head(B), per_head(B), per_head(SUBLANE)],
        [pl.BlockSpec((B, G * LANE), lambda h, b: (b, h))], [jax.ShapeDtypeStruct((T, H * RET_DK), BF16)],
        (H // G, nb), G, dq_jobs)

    def dkv_body(q_ref, k_ref, v_ref, do_ref, cos_ref, sin_ref, d_ref, qd_ref, kd_ref, sd_ref, dk_ref, dv_ref, g_ref):
        cos, sin = cos_ref[...], sin_ref[...]
        for g in range(G):
            hl = pl.ds(g * LANE, LANE)
            q = _rope128(q_ref[:, hl], cos, sin)
            k = _rope128(k_ref[:, hl], cos, sin) * scale
            v, do, st = v_ref[:, hl], do_ref[:, hl], g_ref[g]
            a = _dot(q, k, NT) * d_ref[g]
            da = _dot(do, v, NT) * d_ref[g]
            dv = _dot(a, do, TN) + _dot(k * kd_ref[g], st)
            dk = _dot(da, q, TN) + _dot(v, st, NT) * kd_ref[g]
            dv_ref[:, hl] = dv.astype(dv_ref.dtype)
            dk_ref[:, hl] = _unrope128(dk * scale, cos, sin).astype(dk_ref.dtype)
            g_ref[g] = sd_ref[g, 0:1, :] * st + _dot(q * qd_ref[g], do, TN)

    G, col, row, per_head, dspec = _ret_specs(H, B, True, nb)
    out_spec = pl.BlockSpec((B, G * LANE), lambda h, b: (nb - 1 - b, h))
    dkv = _ret_call(
        "retention_bwd_dkv", dkv_body, 10, 2, (p1, p1, p1, dpre, cos2, sin2, dmat, qd, kd, sd),
        [col(0), col(H), col(2 * H), col(0), row, row, dspec, per_head(B), per_head(B), per_head(SUBLANE)],
        [out_spec, out_spec],
        [jax.ShapeDtypeStruct((T, H * RET_DK), BF16), jax.ShapeDtypeStruct((T, H * RET_DV), BF16)],
        (H // G, nb), G, dkv_jobs)
    return dq, dkv


ATT_BLOCK = 512


ATT_ROW_CHUNKS = 2
ATT_BWD_ROW_CHUNKS = 1
ATT_HEADS_PER_STEP = 8
ATT_BWD_HEADS_PER_STEP = 2
LOG2E = 1.4426950408889634


def _chunk_mask(rows, row0, cols):
    qi = (lax.broadcasted_iota(jnp.int32, (rows, cols), 0) + row0) // CHUNK
    kj = lax.broadcasted_iota(jnp.int32, (rows, cols), 1) // CHUNK
    return kj <= qi


def _attention_fwd(qh, kh, vx, H, jobs=()):
    T = qh.shape[0]
    t = min(ATT_BLOCK, T)
    n = T // t
    r = t // ATT_ROW_CHUNKS
    G = min(ATT_HEADS_PER_STEP, H)
    P = MLA_HEAD_PAD
    scale = (MLA_NOPE + MLA_ROPE) ** -0.5
    c2 = scale * LOG2E
    carried = _Carried(jobs, (H // G, n, n))

    def body(*refs):
        (q_ref, k_ref, v_ref), job_ins, (o_ref, ob_ref, lse_ref), job_outs, (m_ref, acc_ref), job_sems = _split_refs(
            refs, 3, len(carried.operands), 3, len(carried.out_shape), 2)
        i, j = pl.program_id(1), pl.program_id(2)
        carried.before(job_ins, job_outs, job_sems)

        @pl.when(j == 0)
        def _():
            m_ref[...] = jnp.full_like(m_ref, -jnp.inf)
            acc_ref[...] = jnp.zeros_like(acc_ref)

        def update(masked):
            for g in range(G):
                hc = pl.ds(g * P, P)
                for ch in range(ATT_ROW_CHUNKS):
                    rs = pl.ds(ch * r, r)
                    kc = (ch + 1) * r if masked else t
                    ks = pl.ds(0, kc)
                    s = _dot(q_ref[rs, hc], k_ref[ks, hc], NT)
                    if masked:
                        s = jnp.where(_chunk_mask(r, ch * r, kc), s, -jnp.inf)
                    m_prev = m_ref[rs, pl.ds(g * LANE, LANE)]
                    m_new = jnp.maximum(m_prev, jnp.max(s, axis=-1, keepdims=True))
                    alpha = jnp.exp2((m_prev - m_new) * c2)
                    p = jnp.exp2((s - jnp.tile(m_new, (1, kc // LANE))) * c2)
                    acc_ref[rs, hc] = jnp.tile(alpha, (1, 2)) * acc_ref[rs, hc] + _dot(p, v_ref[ks, hc])
                    m_ref[rs, pl.ds(g * LANE, LANE)] = m_new

        pl.when(j < i)(functools.partial(update, False))
        pl.when(j == i)(functools.partial(update, True))

        @pl.when(j == n - 1)
        def _():
            for g in range(G):
                acc = acc_ref[:, pl.ds(g * P, P)]
                l = acc[:, MLA_DV:]
                o = acc[:, :MLA_DV] / l
                o_ref[:, pl.ds(g * MLA_DV, MLA_DV)] = o
                ob_ref[:, pl.ds(g * MLA_DV, MLA_DV)] = o.astype(ob_ref.dtype)
                lse_ref[g] = m_ref[:, pl.ds(g * LANE, LANE)] * scale + jnp.log(l)

        carried.after(job_ins, job_outs, job_sems)

    kclamp = lambda i, j: jnp.minimum(j, i)
    o_spec = pl.BlockSpec((t, G * MLA_DV), lambda h, i, j: (i, h))
    mix_spec = pl.BlockSpec((t, G * MLA_DV), lambda h, i, j: (i, H // G + h))
    return pl.pallas_call(
        body, name="attention_fwd", grid=(H // G, n, n),
        in_specs=[pl.BlockSpec((t, G * P), lambda h, i, j: (i, h)),
                  pl.BlockSpec((t, G * P), lambda h, i, j: (kclamp(i, j), h)),
                  pl.BlockSpec((t, G * P), lambda h, i, j: (kclamp(i, j), h))] + carried.in_specs,
        out_specs=[o_spec, mix_spec, pl.BlockSpec((G, t, LANE), lambda h, i, j: (h, i, 0))] + carried.out_specs,
        out_shape=[jax.ShapeDtypeStruct((T, H * MLA_DV), F32), jax.ShapeDtypeStruct((T, 2 * H * MLA_DV), BF16),
                   jax.ShapeDtypeStruct((H, T, LANE), F32)] + carried.out_shape,
        scratch_shapes=[pltpu.VMEM((t, G * LANE), F32), pltpu.VMEM((t, G * P), F32)] + carried.sems,
        input_output_aliases=carried.aliases(3, 3),
        compiler_params=_params(3),
    )(qh, kh, vx, *carried.operands)


def _attention_bwd(qh, kh, vx, o, lse, dmix, H, jobs=()):
    T = qh.shape[0]
    t = min(ATT_BLOCK, T)
    n = T // t
    r = t // ATT_BWD_ROW_CHUNKS
    G = min(ATT_BWD_HEADS_PER_STEP, H)
    P = MLA_HEAD_PAD
    scale = (MLA_NOPE + MLA_ROPE) ** -0.5
    c2 = scale * LOG2E
    carried = _Carried(jobs, (H // G, n, n))

    def body(*refs):
        ((q_ref, k_ref, v_ref, do_ref, o_ref, lse_ref), job_ins, (dq_ref, dkn_ref, dkr_ref, dv_ref), job_outs,
         (dk_acc, dv_acc), job_sems) = _split_refs(refs, 6, len(carried.operands), 4, len(carried.out_shape), 2)
        j, i = pl.program_id(1), pl.program_id(2)
        carried.before(job_ins, job_outs, job_sems)

        @pl.when(jnp.logical_and(j == 0, i == 0))
        def _():
            dq_ref[...] = jnp.zeros_like(dq_ref)

        @pl.when(i == j)
        def _():
            dk_acc[...] = jnp.zeros_like(dk_acc)
            dv_acc[...] = jnp.zeros_like(dv_acc)

        def update(masked):
            for g in range(G):
                hc, hv = pl.ds(g * P, P), pl.ds(g * MLA_DV, MLA_DV)
                for ch in range(ATT_BWD_ROW_CHUNKS):
                    rs = pl.ds(ch * r, r)
                    kc = (ch + 1) * r if masked else t
                    ks = pl.ds(0, kc)
                    k, v = k_ref[ks, hc], v_ref[ks, pl.ds(g * P, MLA_DV)]
                    q, do = q_ref[rs, hc], do_ref[rs, hv]
                    s = _dot(q, k, NT)
                    if masked:
                        s = jnp.where(_chunk_mask(r, ch * r, kc), s, -jnp.inf)
                    p = jnp.exp2(s * c2 - jnp.tile(lse_ref[g, rs, :] * LOG2E, (1, kc // LANE)))
                    dv_acc[ks, hv] += _dot(p, do, TN)
                    dp = _dot(do, v, NT)
                    delta = jnp.sum(do * o_ref[rs, hv], axis=-1, keepdims=True)
                    ds = p * (dp - delta)
                    rows = pl.ds(pl.multiple_of(i * t, t) + ch * r, r)
                    dq_ref[rows, hc] += _dot(ds, k)
                    dk_acc[ks, hc] += _dot(ds, q, TN)

        pl.when(i > j)(functools.partial(update, False))
        pl.when(i == j)(functools.partial(update, True))

        @pl.when(i == n - 1)
        def _():
            for g in range(G):
                dk = dk_acc[:, pl.ds(g * P, P)] * scale
                dkn_ref[:, pl.ds(g * LANE, LANE)] = dk[:, :MLA_NOPE].astype(dkn_ref.dtype)
                dkr_ref[:, pl.ds(g * LANE, LANE)] = dk[:, MLA_NOPE:]
            dv_ref[...] = dv_acc[...].astype(dv_ref.dtype)

        carried.after(job_ins, job_outs, job_sems)

    qclamp = lambda j, i: jnp.maximum(i, j)
    kv_out = pl.BlockSpec((t, G * LANE), lambda h, j, i: (j, h))
    q_in = pl.BlockSpec((t, G * MLA_DV), lambda h, j, i: (qclamp(j, i), h))
    return pl.pallas_call(
        body, name="attention_bwd", grid=(H // G, n, n),
        in_specs=[pl.BlockSpec((t, G * P), lambda h, j, i: (qclamp(j, i), h)),
                  pl.BlockSpec((t, G * P), lambda h, j, i: (j, h)),
                  pl.BlockSpec((t, G * P), lambda h, j, i: (j, h)),
                  pl.BlockSpec((t, G * MLA_DV), lambda h, j, i: (qclamp(j, i), H // G + h)),
                  q_in,
                  pl.BlockSpec((G, t, LANE), lambda h, j, i: (h, qclamp(j, i), 0))] + carried.in_specs,
        out_specs=[pl.BlockSpec((T, G * P), lambda h, j, i: (0, h)), kv_out, kv_out, kv_out] + carried.out_specs,
        out_shape=[jax.ShapeDtypeStruct((T, H * P), F32), jax.ShapeDtypeStruct((T, H * LANE), BF16),
                   jax.ShapeDtypeStruct((T, H * LANE), F32), jax.ShapeDtypeStruct((T, H * LANE), BF16)] + carried.out_shape,
        scratch_shapes=[pltpu.VMEM((t, G * P), F32), pltpu.VMEM((t, G * MLA_DV), F32)] + carried.sems,
        input_output_aliases=carried.aliases(6, 4),
        compiler_params=_params(3),
    )(qh, kh, vx, dmix, o, lse, *carried.operands)


def _adamw_math(w, g, m, v):
    m = ADAM_B1 * m + (1.0 - ADAM_B1) * g
    v = ADAM_B2 * v + (1.0 - ADAM_B2) * jnp.square(g)
    m_hat = m / (1.0 - ADAM_B1 ** ADAM_STEP)
    v_hat = v / (1.0 - ADAM_B2 ** ADAM_STEP)
    delta = -ADAM_LR * (m_hat / (jnp.sqrt(v_hat) + ADAM_EPS) + ADAM_WD * w)
    return delta, m, v


def _adamw(name, chip_sums, my_chip, landed, w, m, v, jobs=()):
    R, C = w.shape
    tr, tc, tiles = _tile_2d(R, C, ADAMW_TILE_ELEMS)
    at = _tile_index(tr, tc, C)
    nl = len(landed)
    assert sum(l.shape[0] for l in landed) == N_CHIP - 1
    carried = _Carried(jobs, (tiles,))

    def body(q_ref, *refs):
        (own_ref, *rest), job_ins, (g_out, d_out, m_out, v_out), job_outs, _, job_sems = _split_refs(
            refs, nl + 4, len(carried.operands), 4, len(carried.out_shape), 0)
        l_refs, (w_ref, m_ref, v_ref) = rest[:nl], rest[nl:]
        carried.before(job_ins, job_outs, job_sems)
        g = own_ref[...].astype(F32)
        for l_ref in l_refs:
            for s in range(l_ref.shape[0]):
                g = g + l_ref[s].astype(F32)
        delta, mn, vn = _adamw_math(w_ref[...], g, m_ref[...], v_ref[...])
        g_out[...] = g
        d_out[...] = delta
        m_out[...] = mn
        v_out[...] = vn
        carried.after(job_ins, job_outs, job_sems)

    spec = pl.BlockSpec((tr, tc), lambda i, q_ref: at(i))
    own_spec = pl.BlockSpec((None, tr, tc), lambda i, q_ref: (q_ref[0], *at(i)))
    l_specs = [pl.BlockSpec((l.shape[0], tr, tc), lambda i, q_ref: (0, *at(i))) for l in landed]
    return pl.pallas_call(
        body, name=name,
        grid_spec=pltpu.PrefetchScalarGridSpec(
            num_scalar_prefetch=1, grid=(tiles,),
            in_specs=[own_spec] + l_specs + [spec, spec, spec] + carried.in_specs,
            out_specs=[spec] * 4 + carried.out_specs, scratch_shapes=carried.sems),
        out_shape=[jax.ShapeDtypeStruct((R, C), F32)] * 4 + carried.out_shape,
        input_output_aliases={1 + i: o for i, o in carried.aliases(nl + 4, 4).items()},
        compiler_params=_params(1),
    )(my_chip, chip_sums, *landed, w, m, v, *carried.operands)


def _adamw_small(landed, w, m, v):
    P = w.shape[1]

    def body(l_ref, w_ref, m_ref, v_ref, g_out, d_out, m_out, v_out, loss_out):
        acc = l_ref[0]
        for s in range(1, N_DEV):
            acc = acc + l_ref[s]
        total = jnp.sum(acc, axis=0, keepdims=True)
        g = total[:, :P]
        delta, mn, vn = _adamw_math(w_ref[...], g, m_ref[...], v_ref[...])
        g_out[...] = g
        d_out[...] = delta
        m_out[...] = mn
        v_out[...] = vn
        loss_out[...] = total[:, P:]

    return pl.pallas_call(
        body, name="adamw_replicated",
        out_shape=[jax.ShapeDtypeStruct((1, P), F32)] * 4 + [jax.ShapeDtypeStruct((1, LANE), F32)],
    )(landed, w, m, v)


def _rope_rows():
    inv_r = ROPE_BASE ** (-jnp.arange(0, RET_DK, 2, dtype=F32) / RET_DK)
    inv_m = ROPE_BASE ** (-jnp.arange(0, MLA_ROPE, 2, dtype=F32) / MLA_ROPE)
    h = MLA_ROPE // 2
    one, zero = jnp.ones((h,), F32), jnp.zeros((h,), F32)
    pad = jnp.zeros((LANE - MLA_ROPE,), F32)
    rows = [jnp.concatenate([inv_r, inv_r]), jnp.concatenate([-jnp.ones((RET_DK // 2,), F32), jnp.ones((RET_DK // 2,), F32)]),
            jnp.concatenate([inv_m, inv_m, pad]), jnp.concatenate([one, one, pad]),
            jnp.concatenate([zero, one, pad]), jnp.concatenate([-one, zero, pad])]
    return jnp.stack(rows + [jnp.zeros((LANE,), F32)] * (SUBLANE - len(rows)))


def _prep_tile(xt, p, rows):
    ang2, angp = p * rows[0:1], p * rows[2:3]
    sp = jnp.sin(angp)
    return (xt, jnp.cos(ang2), jnp.sin(ang2) * rows[1:2], jnp.cos(angp) * rows[3:4], sp * rows[4:5], sp * rows[5:6])


def kernel(x, positions, w_in, q_norm_g, w_uq, kv_norm_g, w_uk, w_uv, ret_gn_g, w_out, ln1_g, ln1_b, w_up, w_down, ln2_g, ln2_b, loss_target, m_w_in, m_q_norm_g, m_w_uq, m_kv_norm_g, m_w_uk, m_w_uv, m_ret_gn_g, m_w_out, m_ln1_g, m_ln1_b, m_w_up, m_w_down, m_ln2_g, m_ln2_b, v_w_in, v_q_norm_g, v_w_uq, v_kv_norm_g, v_w_uk, v_w_uv, v_ret_gn_g, v_w_out, v_ln1_g, v_ln1_b, v_w_up, v_w_down, v_ln2_g, v_ln2_b):
    T, D = x.shape[1], x.shape[2]
    H = D // 256
    RW = H * RET_DV
    x = x[0]
    tgt = loss_target[0]
    alpha = DEEPNORM_ALPHA
    in_width = w_in.shape[2] * N_DEV
    mla_in = in_width - 4 * RW
    mla_in_pad = -(-mla_in // 512) * 512
    f_uq = MLA_NOPE + MLA_ROPE

    names = ["w_in", "w_uq", "w_uk", "w_uv", "w_out", "w_up", "w_down"]
    big = dict(zip(names, [w_in[0].T, w_uq[0], w_uk[0], w_uv[0], w_out[0], w_up[0], w_down[0]]))
    kind = dict(zip(names, ["blk", "blk", "cols", "cols", "rows", "cols", "rows"]))
    half = {n: w.astype(BF16) for n, w in big.items()}

    def gather(ns, **parts):
        return _GatherJob([half[n] for n in ns], [kind[n] for n in ns], 0.55, 0.9, **parts)

    tr = min(256, T)
    xb, cos2, sin2, cosp, sina, sinb, g_in = _rowwise(
        "prep", _prep_tile, [(x, D, 0), (positions[0].astype(F32)[:, None], 1, 0)], [_rope_rows()],
        [(D, BF16)] + [(LANE, F32)] * 5, [], tr, jobs=[gather(["w_in"])])
    win = g_in.reshape(in_width, D)
    w1, n1 = win, 4 * RW
    w2 = jnp.pad(win[4 * RW:], ((0, mla_in_pad - mla_in), (0, 0)))

    first, second, near, far = (0, 2), (1, 2), (CHIP_X, CHIP_Y), (CHIP_DIAGONAL,)
    p1, g_uq, wuk, wuv, wout = _matmul("proj_ret", xb, w1, tb=True, n=n1,
                                       jobs=[gather(["w_uq", "w_uk", "w_uv", "w_out"])])
    wuq = jnp.transpose(g_uq, (1, 0, 2)).reshape(MLA_Q_RANK, H, f_uq)
    wuq = jnp.pad(wuq, ((0, 0), (0, 0), (0, MLA_HEAD_PAD - f_uq))).reshape(MLA_Q_RANK, H * MLA_HEAD_PAD)
    wukv = jnp.concatenate([wuk, wuv], axis=1)
    p2 = _matmul("proj_mla", xb, w2, tb=True)
    c_kv, c_kr = MLA_Q_RANK, MLA_Q_RANK + MLA_KV_RANK

    def mla_prep(p, cp, sa, sb, gq, gkv):
        cqn = _rms(p[:, :c_kv], gq)
        ckvn = _rms(p[:, c_kv:c_kr], gkv)
        return cqn, ckvn, _rope64(p[:, c_kr:c_kr + LANE], cp, sa, sb)

    cqn, ckvn, krope = _rowwise(
        "mla_prep", mla_prep, [(p2, mla_in_pad, 0), (cosp, LANE, 0), (sina, LANE, 0), (sinb, LANE, 0)],
        [q_norm_g, kv_norm_g], [(MLA_Q_RANK, BF16), (MLA_KV_RANK, BF16), (LANE, BF16)], [], tr)
    qf = _matmul("mla_q", cqn, wuq)
    kv = _matmul("mla_kv", ckvn, wukv, out_dtypes=(BF16,))

    def mla_heads(q, kn, vv, kr, cp, sa, sb):
        qs, ks, vs = [], [], []
        ones = jnp.ones((q.shape[0], MLA_DV), vv.dtype)
        for h in range(H):
            o = h * MLA_HEAD_PAD
            qs += [q[:, o:o + MLA_NOPE], _rope64(q[:, o + MLA_NOPE:o + MLA_HEAD_PAD], cp, sa, sb)]
            ks += [kn[:, h * MLA_NOPE:(h + 1) * MLA_NOPE], kr]
            vs += [vv[:, h * MLA_DV:(h + 1) * MLA_DV], ones]
        return jnp.concatenate(qs, axis=1), jnp.concatenate(ks, axis=1), jnp.concatenate(vs, axis=1)

    qh, kh, vx = _rowwise(
        "mla_heads", mla_heads,
        [(qf, H * MLA_HEAD_PAD, 0), (kv, RW, 0), (kv, RW, 1), (krope, LANE, 0), (cosp, LANE, 0), (sina, LANE, 0), (sinb, LANE, 0)],
        [], [(H * MLA_HEAD_PAD, BF16)] * 3, [], tr)
    whole = (0, 1)
    att_o, mix_half, lse, wup_a = _attention_fwd(
        qh, kh, vx, H, jobs=[gather(["w_up"], pieces=[(first, near, True, 0.45), (second, near, True, 0.75)])])
    ret_pre, mixin, wup = _retention_fwd(
        p1, cos2, sin2, ret_gn_g, mix_half, H,
        jobs=[gather(["w_up"], pieces=[(first, far, False, 0.55), (second, far, False, 0.9)], base=[wup_a])])
    mix, wdown_a = _matmul("mix_out", mixin, wout,
                           jobs=[gather(["w_down"], pieces=[((0, 4), near, True, 0.5), ((1, 4), near, True, 0.9)])])

    def ln1_fwd(xt, mt, g, b):
        z = alpha * xt + mt
        xhat, _ = _ln_stats(z)
        y = xhat * g + b
        return z, y, y

    z1, x1, x1b, wdown_b = _rowwise("ln1", ln1_fwd, [(x, D, 0), (mix, D, 0)], [ln1_g, ln1_b],
                                    [(D, F32), (D, F32), (D, BF16)], [], tr,
                                    jobs=[gather(["w_down"], pieces=[((2, 4), near, True, 0.5), ((3, 4), near, True, 0.9)],
                                                 base=[wdown_a])])

    def relu2(acc):
        r = jnp.maximum(acc, 0.0)
        return r * r, r

    a2b, rb, wdown = _matmul("mlp_up", x1b, wup, out_dtypes=(BF16, BF16), epilogue=relu2,
                             jobs=[gather(["w_down"], pieces=[(whole, far, False, 0.6)], base=[wdown_b])])
    hmlp = _matmul("mlp_down", a2b, wdown)

    def ln2_loss(x1t, ht, tt, g, b):
        xhat, rstd = _ln_stats(alpha * x1t + ht)
        err = xhat * g + b - tt
        dy = err / D
        dz = _ln_bwd(dy, xhat, rstd, g)
        lrow = 0.5 * jnp.mean(jnp.square(err), axis=-1, keepdims=True)
        return dz, dz, dy * xhat, dy, jnp.broadcast_to(lrow, (lrow.shape[0], LANE))

    dz2, dz2b, dg2, db2, lpart = _rowwise("ln2_loss", ln2_loss, [(x1, D, 0), (hmlp, D, 0), (tgt, D, 0)], [ln2_g, ln2_b],
                                          [(D, F32), (D, BF16)], [D, D, LANE], tr)

    def pair(grads):
        ns = list(grads)
        return _PairJob([grads[n] for n in ns], [kind[n] for n in ns], [big[n].shape for n in ns])

    my_core = lax.axis_index("c").astype(jnp.int32).reshape(1)
    my_chip = (2 * lax.axis_index("x") + lax.axis_index("y")).astype(jnp.int32).reshape(1)
    chip_sums = {}

    def pair_sums(grads, theirs):
        for n, t in zip(grads, theirs):
            chip_sums[n] = _pair_sum("pair_sum_" + n, grads[n], kind[n], big[n].shape, t, my_core)
        return {n: chip_sums[n] for n in grads}

    def chip(sums, rels=CHIP_RELS, by_source=None):
        ns = list(sums)
        return _ChipJob([sums[n] for n in ns], [big[n].shape for n in ns], rels, by_source)

    landed = {}
    da = _matmul("mlp_down_dx", dz2b, wdown, tb=True, out_dtypes=(BF16,), extras=(rb,),
                 epilogue=lambda acc, r: (acc * (2.0 * r.astype(F32)),))
    g_down = {"w_down": _matmul("mlp_down_dw", a2b, dz2b, ta=True, out_dtypes=(BF16,))}
    g_wup, *pr = _matmul("mlp_up_dw", x1b, da, ta=True, out_dtypes=(BF16,), jobs=[pair(g_down)])
    sums_down = pair_sums(g_down, pr)
    g_up = {"w_up": g_wup}
    dx1m, l_a, *pr = _matmul("mlp_up_dx", da, wup, tb=True, jobs=[chip(sums_down, CHIP_RELS[:2]), pair(g_up)])
    sums_up = pair_sums(g_up, pr)

    def ln1_bwd(dm, dzz, z, g):
        xhat, rstd = _ln_stats(z)
        dy = dm + alpha * dzz
        dz = _ln_bwd(dy, xhat, rstd, g)
        return dz, dz, dy * xhat, dy

    dz1, dz1b, dg1, db1 = _rowwise("ln1_bwd", ln1_bwd, [(dx1m, D, 0), (dz2, D, 0), (z1, D, 0)], [ln1_g],
                                   [(D, F32), (D, BF16)], [D, D], tr)
    g_wout = _matmul("mix_out_dw", mixin, dz1b, ta=True, out_dtypes=(BF16,))
    dmix = _matmul("mix_out_dx", dz1b, wout, tb=True)

    dqh, dkn, dkr, dvv, l_b, l_c = _attention_bwd(qh, kh, vx, att_o, lse, dmix, H,
                                                  jobs=[chip(sums_down, CHIP_RELS[2:]), chip(sums_up)])
    landed["w_down"], landed["w_up"] = [l_a, l_b], [l_c]
    att_scale = (MLA_NOPE + MLA_ROPE) ** -0.5

    def mla_heads_bwd(dq, dkrh, cp, sa, sb):
        parts, acc = [], dkrh[:, :LANE]
        dq = dq * att_scale
        for h in range(H):
            o = h * MLA_HEAD_PAD
            parts += [dq[:, o:o + MLA_NOPE], _unrope64(dq[:, o + MLA_NOPE:o + MLA_HEAD_PAD], cp, sa, sb)]
            if h:
                acc = acc + dkrh[:, h * LANE:(h + 1) * LANE]
        return jnp.concatenate(parts, axis=1), _unrope64(acc, cp, sa, sb)

    dqb, dkr128 = _rowwise(
        "mla_heads_bwd", mla_heads_bwd,
        [(dqh, H * MLA_HEAD_PAD, 0), (dkr, H * LANE, 0), (cosp, LANE, 0), (sina, LANE, 0), (sinb, LANE, 0)],
        [], [(H * MLA_HEAD_PAD, BF16), (LANE, F32)], [], tr)
    dcqn = _matmul("mla_q_dx", dqb, wuq, tb=True)
    g_wuq = _matmul("mla_q_dw", cqn, dqb, ta=True, out_dtypes=(BF16,))
    dkvb = jnp.concatenate([dkn, dvv], axis=1)
    dckvn = _matmul("mla_kv_dx", dkvb, wukv, tb=True)
    g_wukv = _matmul("mla_kv_dw", ckvn, dkvb, ta=True, out_dtypes=(BF16,))

    def rms_bwd(c, dy, g):
        rstd = lax.rsqrt(jnp.mean(jnp.square(c), axis=-1, keepdims=True) + EPS)
        dyg = dy * g
        dc = rstd * (dyg - c * (rstd * rstd) * jnp.mean(dyg * c, axis=-1, keepdims=True))
        return dc, dy * c * rstd

    def mla_prep_bwd(p, dq_, dkv_, dkr_, gq, gkv):
        dcq, dgq_ = rms_bwd(p[:, :c_kv], dq_, gq)
        dckv, dgkv_ = rms_bwd(p[:, c_kv:c_kr], dkv_, gkv)
        pad = jnp.zeros((p.shape[0], mla_in_pad - c_kr - LANE), F32)
        return jnp.concatenate([dcq, dckv, dkr_, pad], axis=1), dgq_, dgkv_

    d_p2, dgq, dgkv = _rowwise(
        "mla_prep_bwd", mla_prep_bwd, [(p2, mla_in_pad, 0), (dcqn, MLA_Q_RANK, 0), (dckvn, MLA_KV_RANK, 0), (dkr128, LANE, 0)],
        [q_norm_g, kv_norm_g], [(mla_in_pad, BF16)], [MLA_Q_RANK, MLA_KV_RANK], tr)

    def gate_bwd(pre, gate, dout, gn):
        dpre, dgate, dgn = [], [], []
        for h in range(H):
            sl = slice(h * RET_DV, (h + 1) * RET_DV)
            nrm, rstd = _ln_stats(pre[:, sl])
            gate_h, do_h, gn_h = gate[:, sl], dout[:, sl], gn[:, sl]
            sg = jax.nn.sigmoid(gate_h)
            silu = gate_h * sg
            dgate.append(do_h * (nrm * gn_h) * (sg * (1.0 + gate_h * (1.0 - sg))))
            dn = do_h * silu
            dgn.append(dn * nrm)
            dpre.append(_ln_bwd(dn, nrm, rstd, gn_h))
        return jnp.concatenate(dpre, axis=1), jnp.concatenate(dgate, axis=1), jnp.concatenate(dgn, axis=1)

    dpre, drg, dgn = _rowwise("ret_gate_bwd", gate_bwd, [(ret_pre, RW, 0), (p1, RW, 3), (dmix, RW, 0)], [ret_gn_g],
                              [(RW, F32), (RW, BF16)], [RW], tr)
    g_uq = g_wuq.reshape(MLA_Q_RANK, H, MLA_HEAD_PAD)[:, :, :f_uq].reshape(MLA_Q_RANK, N_DEV, H * f_uq // N_DEV)
    mid_grads = {"w_uq": jnp.transpose(g_uq, (1, 0, 2)), "w_uk": g_wukv[:, :RW], "w_uv": g_wukv[:, RW:], "w_out": g_wout}
    (drq, *pr), (drk, drv) = _retention_bwd(p1, dpre, cos2, sin2, H, dq_jobs=[pair(mid_grads)])
    sums_mid = pair_sums(mid_grads, pr)
    d_p1 = jnp.concatenate([drq, drk, drv, drg], axis=1)

    g_w1, *l_a = _matmul("proj_ret_dw", d_p1, xb, ta=True, out_dtypes=(BF16,), jobs=[chip(sums_mid)])
    for n, a_ in zip(mid_grads, l_a):
        landed[n] = [a_]
    g_w2 = _matmul("proj_mla_dw", d_p2, xb, ta=True, out_dtypes=(BF16,))

    g_win = {"w_in": jnp.concatenate([g_w1, g_w2[:mla_in]], axis=0).reshape(N_DEV, in_width // N_DEV, D)}
    small = jnp.concatenate([dgq, dgkv, dgn, dg1, db1, dg2, db2, lpart], axis=1)
    dx_a, *pr = _matmul("proj_mla_dx", d_p2, w2, extras=(dz1,), epilogue=lambda acc, d: (acc + alpha * d,), jobs=[pair(g_win)])
    sums_in = pair_sums(g_win, pr)
    grad_x, l_a, small_landed = _matmul("proj_ret_dx", d_p1, w1, extras=(dx_a,), epilogue=lambda acc, d: (acc + d,),
                                        jobs=[chip(sums_in, CHIP_RELS[:2], by_source=small)])

    moments = dict(zip(names, zip([m_w_in, m_w_uq, m_w_uk, m_w_uv, m_w_out, m_w_up, m_w_down],
                                  [v_w_in, v_w_uq, v_w_uk, v_w_uv, v_w_out, v_w_up, v_w_down])))
    res = {}

    def adamw(n, jobs=()):
        as_held = (lambda a: a[0].T) if n == "w_in" else (lambda a: a[0])
        as_given = (lambda r: r.T[None]) if n == "w_in" else (lambda r: r[None])
        out = _adamw("adamw_" + n, chip_sums[n], my_chip, landed[n], big[n], as_held(moments[n][0]), as_held(moments[n][1]),
                     jobs=jobs)
        res[n] = [as_given(r) for r in out[:4]]
        return out[4:]

    (l_b,) = adamw("w_out", jobs=[chip(sums_in, CHIP_RELS[2:])])
    landed["w_in"] = [l_a, l_b]
    for n in names:
        if n != "w_out":
            adamw(n)

    small_names = ["q_norm_g", "kv_norm_g", "ret_gn_g", "ln1_g", "ln1_b", "ln2_g", "ln2_b"]
    small_w = [q_norm_g, kv_norm_g, ret_gn_g, ln1_g, ln1_b, ln2_g, ln2_b]
    small_m = [m_q_norm_g, m_kv_norm_g, m_ret_gn_g, m_ln1_g, m_ln1_b, m_ln2_g, m_ln2_b]
    small_v = [v_q_norm_g, v_kv_norm_g, v_ret_gn_g, v_ln1_g, v_ln1_b, v_ln2_g, v_ln2_b]
    cat = lambda arrs: jnp.concatenate(arrs, axis=1)
    *sres, loss_lanes = _adamw_small(small_landed, cat(small_w), cat(small_m), cat(small_v))
    loss = loss_lanes[0, 0]
    off = 0
    for n, w in zip(small_names, small_w):
        res[n] = [r[:, off:off + w.shape[1]] for r in sres]
        off += w.shape[1]

    order = ["w_in", "q_norm_g", "w_uq", "kv_norm_g", "w_uk", "w_uv", "ret_gn_g", "w_out", "ln1_g", "ln1_b",
             "w_up", "w_down", "ln2_g", "ln2_b"]
    outs = [loss, grad_x[None]]
    for k in range(4):
        outs += [res[n][k] for n in order]
    return tuple(outs)
```

```python
import functools

import numpy as np

import jax
import jax.numpy as jnp
from jax import lax
from jax.experimental import pallas as pl
from jax.experimental.pallas import tpu as pltpu

F32 = jnp.float32
BF16 = jnp.bfloat16

CHUNK = 64
RET_DK = 128
RET_DV = 128
MLA_NOPE = 128
MLA_ROPE = 64
MLA_DV = 128
MLA_Q_RANK = 768
MLA_KV_RANK = 512
MLA_HEAD_PAD = 256
ROPE_BASE = 10000.0
EPS = 1e-5
DEPTH = 1
DEEPNORM_ALPHA = (2.0 * DEPTH) ** 0.25
ADAM_LR = 0.001
ADAM_B1 = 0.9
ADAM_B2 = 0.999
ADAM_EPS = 1e-08
ADAM_WD = 0.01
ADAM_STEP = 10

N_DEV = 8
LANE = 128
SUBLANE = 8
VMEM_LIMIT = 56 * 1024 * 1024
ADAMW_TILE_ELEMS = 128 * 1024
PAIR_TILE_ELEMS = 1024 * 1024

MESH = pl.DeviceIdType.MESH
ANY = pl.BlockSpec(memory_space=pl.ANY)

NN = (((1,), (0,)), ((), ()))
NT = (((1,), (1,)), ((), ()))
TN = (((0,), (0,)), ((), ()))


def _dot(a, b, dims=NN):
    return lax.dot_general(a.astype(BF16), b.astype(BF16), dims, preferred_element_type=F32)


def _params(n_axes):
    return pltpu.CompilerParams(dimension_semantics=("arbitrary",) * n_axes, vmem_limit_bytes=VMEM_LIMIT)


def _tile(n, want):
    if n <= want:
        return n
    t = want
    while t >= LANE:
        if n % t == 0:
            return t
        t -= LANE
    return n


def _my_place():
    return lax.axis_index("x"), lax.axis_index("y"), lax.axis_index("c")


def _flat(px, py, pc):
    return 4 * px + 2 * py + pc


def _block_of(ref, kind, idx, rows, cols, sub=None):
    lo, n = (0, rows) if sub is None else sub
    if kind == "blk":
        return ref.at[idx, pl.ds(lo, n), :]
    if kind == "rows":
        return ref.at[pl.ds(pl.multiple_of(idx * rows + lo, n), n), :]
    return ref.at[pl.ds(lo, n), pl.ds(pl.multiple_of(idx * cols, cols), cols)]


CHIP_X, CHIP_Y, CHIP_DIAGONAL = 0, 1, 2
ALL_CHIPS = (CHIP_X, CHIP_Y, CHIP_DIAGONAL)


class _GatherJob:
    def __init__(self, shards, kinds, mid, late, pieces=(((0, 1), ALL_CHIPS, True),), base=None):
        nw = len(shards)
        self.nw, self.kinds = nw, list(kinds)
        self.pieces = [tuple(p) if len(p) == 4 else (*p, late) for p in pieces]
        for _, chips, _, _ in self.pieces:
            assert CHIP_DIAGONAL not in chips or len(chips) in (1, 3)
        self.shapes = [s.shape for s in shards]
        self.ins = list(shards) + (list(base) if base is not None else [])
        self.alias = [(nw + w, w) for w in range(nw)] if base is not None else []
        self.out_shape = []
        for s, kind in zip(shards, kinds):
            r, c = s.shape
            shape = {"blk": (N_DEV, r, c), "rows": (N_DEV * r, c), "cols": (r, N_DEV * c)}[kind]
            self.out_shape.append(jax.ShapeDtypeStruct(shape, s.dtype))
        n = 7 * nw * len(self.pieces)
        self.sems = [pltpu.SemaphoreType.DMA((n,)), pltpu.SemaphoreType.DMA((n,)), pltpu.SemaphoreType.DMA((n,))]
        whens = sorted({p[3] for p in self.pieces})
        self.phases = ([(0.0, self._start), (mid, self._relay)] + [(f, functools.partial(self._pass_on, f)) for f in whens]
                       + [(1.0, self._finish)])

    def _each(self, src, dst, send_sems, recv_sems, local_sems):
        x, y, c = _my_place()
        all_chips = [(1 - x, y), (x, 1 - y), (1 - x, 1 - y)]
        out = []
        for w in range(self.nw):
            r, cc = self.shapes[w]
            for p, ((k_of, of), chips, own, when) in enumerate(self.pieces):
                sub = (k_of * (r // of), r // of)
                s0 = 7 * (w * len(self.pieces) + p)
                mine = src[w].at[pl.ds(*sub), :]

                def place(dev, w=w, r=r, cc=cc, sub=sub):
                    return _block_of(dst[w], self.kinds[w], _flat(*dev), r, cc, sub)

                def copy(k, block, to, from_src=False, place=place, mine=mine, s0=s0):
                    return pltpu.make_async_remote_copy(
                        src_ref=mine if from_src else place(block), dst_ref=place(block),
                        send_sem=send_sems.at[s0 + k], recv_sem=recv_sems.at[s0 + k], device_id=to, device_id_type=MESH)

                relay = functools.partial(copy, 1 + CHIP_DIAGONAL, (x ^ (1 - c), y ^ c, c), (x ^ c, y ^ (1 - c), c))
                local = functools.partial(pltpu.make_async_copy, mine, place((x, y, c)), local_sems.at[s0])
                near = [(j, all_chips[j]) for j in chips if j != CHIP_DIAGONAL]
                out.append((near, CHIP_DIAGONAL in chips, own, copy, relay, local, when))
        return (x, y, c), (x, y, 1 - c), all_chips[CHIP_DIAGONAL], out

    def _start(self, *refs):
        me, sibling, diagonal, each = self._each(*refs)
        for near, far, own, copy, relay, local, when in each:
            if own:
                local().start()
                copy(0, me, sibling, from_src=True).start()
            for j, chip in near:
                copy(1 + j, me, (*chip, me[2]), from_src=True).start()
            if far and not near:
                relay().start()

    def _landed(self, near, copy, me, sibling):
        for j, chip in near:
            copy(1 + j, (*chip, me[2]), me).wait_recv()
            copy(4 + j, (*chip, me[2]), sibling).start()

    def _relay(self, *refs):
        me, sibling, diagonal, each = self._each(*refs)
        for near, far, own, copy, relay, local, when in each:
            if far and near:
                self._landed(near, copy, me, sibling)
                relay().start()

    def _pass_on(self, frac, *refs):
        me, sibling, diagonal, each = self._each(*refs)
        for near, far, own, copy, relay, local, when in each:
            if when == frac:
                self._landed([(CHIP_DIAGONAL, diagonal)] if far else near, copy, me, sibling)

    def _finish(self, *refs):
        me, sibling, diagonal, each = self._each(*refs)
        for near, far, own, copy, relay, local, when in each:
            if own:
                copy(0, sibling, me).wait_recv()
            for j, chip in near + [(CHIP_DIAGONAL, diagonal)] * int(far):
                copy(4 + j, (*chip, 1 - me[2]), me).wait_recv()
        for near, far, own, copy, relay, local, when in each:
            if own:
                copy(0, me, sibling, from_src=True).wait_send()
                local().wait()
            for j, chip in near:
                copy(1 + j, me, (*chip, me[2]), from_src=True).wait_send()
                copy(4 + j, (*chip, me[2]), sibling).wait_send()
            if far:
                relay().wait_send()
                copy(4 + CHIP_DIAGONAL, (*diagonal, me[2]), sibling).wait_send()


REL_ALL = [(1, 0, 0), (1, 0, 1), (0, 1, 0), (0, 1, 1), (1, 1, 0), (1, 1, 1), (0, 0, 1)]
N_CHIP = 4


class _PairJob:
    def __init__(self, grads, kinds, shard_shapes):
        self.ins = list(grads)
        self.kinds, self.shard_shapes, self.nw = list(kinds), list(shard_shapes), len(grads)
        self.out_shape = [jax.ShapeDtypeStruct((N_CHIP,) + tuple(s), g.dtype) for g, s in zip(grads, shard_shapes)]
        n = N_CHIP * self.nw
        self.sems = [pltpu.SemaphoreType.DMA((n,)), pltpu.SemaphoreType.DMA((n,))]
        self.phases = [(0.0, self._start), (1.0, self._finish)]

    def _copies(self, src, theirs, send_sems, recv_sems):
        x, y, c = _my_place()
        remote = []
        for w in range(self.nw):
            r, cc = self.shard_shapes[w]
            for q in range(N_CHIP):
                s = N_CHIP * w + q
                remote.append(pltpu.make_async_remote_copy(
                    src_ref=_block_of(src[w], self.kinds[w], 2 * q + (1 - c), r, cc), dst_ref=theirs[w].at[q],
                    send_sem=send_sems.at[s], recv_sem=recv_sems.at[s], device_id=(x, y, 1 - c), device_id_type=MESH))
        return remote

    def _start(self, *refs):
        for cp in self._copies(*refs):
            cp.start()

    def _finish(self, *refs):
        for cp in self._copies(*refs):
            cp.wait()


CHIP_RELS = [(1, 0), (0, 1), (1, 1)]


class _ChipJob:
    def __init__(self, chip_sums, shard_shapes, rels=CHIP_RELS, by_source=None, piece=(0, 1), base=None):
        nw = len(chip_sums)
        self.ins = list(chip_sums) + ([by_source] if by_source is not None else []) + (list(base) if base is not None else [])
        self.nw, self.has_src, self.rels, self.piece = nw, by_source is not None, list(rels), piece
        self.rows = [s[0] for s in shard_shapes]
        self.alias = [(nw + int(self.has_src) + w, w) for w in range(nw)] if base is not None else []
        self.out_shape = [jax.ShapeDtypeStruct((len(rels),) + tuple(s), a.dtype) for a, s in zip(chip_sums, shard_shapes)]
        if self.has_src:
            self.out_shape.append(jax.ShapeDtypeStruct((N_DEV,) + by_source.shape, by_source.dtype))
        n = len(rels) * self.nw + len(REL_ALL) * int(self.has_src)
        self.sems = [pltpu.SemaphoreType.DMA((n,)), pltpu.SemaphoreType.DMA((n,)), pltpu.SemaphoreType.DMA((1,))]
        self.phases = [(0.0, self._start), (1.0, self._finish)]

    def _copies(self, src, land, send_sems, recv_sems, local_sems):
        x, y, c = _my_place()
        nw = self.nw

        def remote_copy(s, src_ref, dst_ref, peer):
            return pltpu.make_async_remote_copy(src_ref=src_ref, dst_ref=dst_ref, send_sem=send_sems.at[s],
                                                recv_sem=recv_sems.at[s], device_id=peer, device_id_type=MESH)

        def rows(w):
            n = self.rows[w] // self.piece[1]
            return pl.ds(self.piece[0] * n, n)

        local, remote = [], []
        for k, (fx, fy) in enumerate(self.rels):
            px, py = x ^ fx, y ^ fy
            remote += [remote_copy(nw * k + w, src[w].at[2 * px + py, rows(w), :], land[w].at[k, rows(w), :], (px, py, c))
                       for w in range(nw)]
        if self.has_src:
            my_idx = _flat(x, y, c)
            local.append(pltpu.make_async_copy(src[nw], land[nw].at[my_idx], local_sems.at[0]))
            remote += [remote_copy(nw * len(self.rels) + k, src[nw], land[nw].at[my_idx], (x ^ fx, y ^ fy, c ^ fc))
                       for k, (fx, fy, fc) in enumerate(REL_ALL)]
        return local, remote

    def _start(self, *refs):
        local, remote = self._copies(*refs)
        for cp in local + remote:
            cp.start()

    def _finish(self, *refs):
        local, remote = self._copies(*refs)
        for cp in remote + local:
            cp.wait()


def _row_tile(R, C, elems=ADAMW_TILE_ELEMS):
    tr = 16
    while tr * 2 * C <= elems and R % (tr * 2) == 0:
        tr *= 2
    return tr if R % tr == 0 else R


BF16_ROWS = 16


def _tile_2d(R, C, elems):
    if R % BF16_ROWS == 0:
        tr = _row_tile(R, C, elems)
        return tr, C, R // tr
    tc = C
    while tc % (2 * LANE) == 0 and R * tc > elems:
        tc //= 2
    return R, tc, C // tc


def _tile_index(tr, tc, C):
    return (lambda i: (i, 0)) if tc == C else (lambda i: (0, i))


def _part_spec(kind, tr, tc, R, C, n_lead):
    at = _tile_index(tr, tc, C)

    def dev(ids, pref):
        return 2 * ids[0] + pref[0] if n_lead == 2 else pref[0]

    def index(*args):
        ids, pref = args[:n_lead], args[n_lead]
        ri, ci = at(ids[-1])
        if kind == "blk":
            return dev(ids, pref), ri, ci
        if kind == "rows":
            return dev(ids, pref) * (R // tr) + ri, ci
        return ri, dev(ids, pref) * (C // tc) + ci

    return pl.BlockSpec((None, tr, tc) if kind == "blk" else (tr, tc), index)


def _pair_sum(name, grad, kind, shard_shape, theirs, c):
    R, C = shard_shape
    tr, tc, tiles = _tile_2d(R, C, PAIR_TILE_ELEMS)
    at = _tile_index(tr, tc, C)

    def body(c_ref, g_ref, t_ref, o_ref):
        o_ref[...] = (g_ref[...].astype(F32) + t_ref[...].astype(F32)).astype(o_ref.dtype)

    slot = pl.BlockSpec((None, tr, tc), lambda q, i, c_ref: (q, *at(i)))
    return pl.pallas_call(
        body, name=name,
        grid_spec=pltpu.PrefetchScalarGridSpec(
            num_scalar_prefetch=1, grid=(N_CHIP, tiles),
            in_specs=[_part_spec(kind, tr, tc, R, C, 2), slot], out_specs=slot),
        out_shape=jax.ShapeDtypeStruct(theirs.shape, theirs.dtype), compiler_params=_params(2))(c, grad, theirs)


class _Carried:
    def __init__(self, jobs, grid):
        self.jobs = list(jobs)
        self.steps = int(np.prod(grid))
        self.grid = tuple(grid)
        self.operands = [a for j in self.jobs for a in j.ins]
        self.out_shape = [o for j in self.jobs for o in j.out_shape]
        self.sems = [s for j in self.jobs for s in j.sems]
        self.in_specs = [ANY] * len(self.operands)
        self.out_specs = [ANY] * len(self.out_shape)

    def aliases(self, n_in, n_out):
        out, i0, o0 = {}, n_in, n_out
        for job in self.jobs:
            for ji, jo in getattr(job, "alias", []):
                out[i0 + ji] = o0 + jo
            i0, o0 = i0 + len(job.ins), o0 + len(job.out_shape)
        return out

    def _step(self):
        lin = pl.program_id(0)
        for a in range(1, len(self.grid)):
            lin = lin * self.grid[a] + pl.program_id(a)
        return lin

    def _run(self, in_refs, out_refs, sem_refs, last):
        lin = self._step()
        for job in self.jobs:
            ins, in_refs = in_refs[:len(job.ins)], in_refs[len(job.ins):]
            outs, out_refs = out_refs[:len(job.out_shape)], out_refs[len(job.out_shape):]
            sems, sem_refs = sem_refs[:len(job.sems)], sem_refs[len(job.sems):]
            for frac, fn in job.phases:
                if (frac >= 1.0) != last:
                    continue
                at = min(self.steps - 1, int(frac * (self.steps - 1)))
                pl.when(lin == at)(functools.partial(fn, ins, outs, *sems))

    def before(self, in_refs, out_refs, sem_refs):
        self._run(in_refs, out_refs, sem_refs, last=False)

    def after(self, in_refs, out_refs, sem_refs):
        self._run(in_refs, out_refs, sem_refs, last=True)


def _split_refs(refs, n_in, n_job_in, n_out, n_job_out, n_scratch):
    cuts = np.cumsum([0, n_in, n_job_in, n_out, n_job_out, n_scratch])
    return [refs[cuts[i]:cuts[i + 1]] for i in range(5)] + [refs[cuts[5]:]]


def _matmul(name, a, b, *, ta=False, tb=False, out_dtypes=(F32,), epilogue=None, extras=(), tm=1024, tn=1024, tk=4096, jobs=(),
            n=None):
    if ta:
        K, M = a.shape
    else:
        M, K = a.shape
    N = n if n is not None else (b.shape[0] if tb else b.shape[1])
    tm, tn, tk = _tile(M, tm), _tile(N, tn), _tile(K, tk)
    nk = K // tk
    grid = (M // tm, N // tn, nk)
    dims = TN if ta else (NT if tb else NN)
    n_ex, n_out = len(extras), len(out_dtypes)
    carried = _Carried(jobs, grid)

    def body(*refs):
        ins, job_ins, out_refs, job_outs, scratch, job_sems = _split_refs(
            refs, 2 + n_ex, len(carried.operands), n_out, len(carried.out_shape), int(nk > 1))
        a_ref, b_ref, ex_refs = ins[0], ins[1], ins[2:]
        k = pl.program_id(2)
        carried.before(job_ins, job_outs, job_sems)

        def finish(acc):
            res = (acc,) if epilogue is None else epilogue(acc, *[e[...] for e in ex_refs])
            for o_ref, r in zip(out_refs, res):
                o_ref[...] = r.astype(o_ref.dtype)

        if nk == 1:
            finish(_dot(a_ref[...], b_ref[...], dims))
        else:
            acc_ref = scratch[0]

            @pl.when(k == 0)
            def _():
                acc_ref[...] = _dot(a_ref[...], b_ref[...], dims)

            @pl.when(k > 0)
            def _():
                acc_ref[...] += _dot(a_ref[...], b_ref[...], dims)

            pl.when(k == nk - 1)(lambda: finish(acc_ref[...]))

        carried.after(job_ins, job_outs, job_sems)

    a_spec = pl.BlockSpec((tk, tm), lambda i, j, k: (k, i)) if ta else pl.BlockSpec((tm, tk), lambda i, j, k: (i, k))
    b_spec = pl.BlockSpec((tn, tk), lambda i, j, k: (j, k)) if tb else pl.BlockSpec((tk, tn), lambda i, j, k: (k, j))
    tile_spec = pl.BlockSpec((tm, tn), lambda i, j, k: (i, j))
    outs = pl.pallas_call(
        body,
        name=name,
        grid=grid,
        in_specs=[a_spec, b_spec] + [tile_spec] * n_ex + carried.in_specs,
        out_specs=[tile_spec] * n_out + carried.out_specs,
        out_shape=[jax.ShapeDtypeStruct((M, N), d) for d in out_dtypes] + carried.out_shape,
        scratch_shapes=[pltpu.VMEM((tm, tn), F32)] * int(nk > 1) + carried.sems,
        input_output_aliases=carried.aliases(2 + n_ex, n_out),
        compiler_params=_params(3),
    )(a, b, *extras, *carried.operands)
    return outs[0] if len(outs) == 1 else outs


def _rowwise(name, fn, row_in, const_in, row_out, acc_out, tm, jobs=()):
    T = row_in[0][0].shape[0]
    n_r, n_c, n_o, n_a = len(row_in), len(const_in), len(row_out), len(acc_out)
    carried = _Carried(jobs, (T // tm,))

    def body(*refs):
        in_refs, job_ins, own_outs, job_outs, _, job_sems = _split_refs(
            refs, n_r + n_c, len(carried.operands), n_o + n_a, len(carried.out_shape), 0)
        out_refs, acc_refs = own_outs[:n_o], own_outs[n_o:]
        carried.before(job_ins, job_outs, job_sems)
        res = fn(*[r[...] for r in in_refs])
        for o_ref, r in zip(out_refs, res[:n_o]):
            o_ref[...] = r.astype(o_ref.dtype)

        @pl.when(pl.program_id(0) == 0)
        def _():
            for a_ref in acc_refs:
                a_ref[...] = jnp.zeros_like(a_ref)

        for a_ref, r in zip(acc_refs, res[n_o:]):
            a_ref[...] += jnp.sum(r.reshape(tm // SUBLANE, SUBLANE, r.shape[-1]), axis=0)
        carried.after(job_ins, job_outs, job_sems)

    in_specs = [pl.BlockSpec((tm, w), functools.partial(lambda i, c: (i, c), c=cb)) for (_, w, cb) in row_in]
    in_specs += [pl.BlockSpec(c.shape, functools.partial(lambda i, nd: (0,) * nd, nd=c.ndim)) for c in const_in]
    out_specs = [pl.BlockSpec((tm, c), lambda i: (i, 0)) for (c, _) in row_out]
    out_specs += [pl.BlockSpec((SUBLANE, c), lambda i: (0, 0)) for c in acc_out]
    out_shape = [jax.ShapeDtypeStruct((T, c), d) for (c, d) in row_out]
    out_shape += [jax.ShapeDtypeStruct((SUBLANE, c), F32) for c in acc_out]
    return pl.pallas_call(
        body, name=name, grid=(T // tm,), in_specs=in_specs + carried.in_specs, out_specs=out_specs + carried.out_specs,
        out_shape=out_shape + carried.out_shape, scratch_shapes=carried.sems,
        input_output_aliases=carried.aliases(n_r + n_c, n_o + n_a), compiler_params=_params(1),
    )(*[r[0] for r in row_in], *const_in, *carried.operands)


def _ln_stats(z):
    mu = jnp.mean(z, axis=-1, keepdims=True)
    var = jnp.mean(jnp.square(z - mu), axis=-1, keepdims=True)
    rstd = lax.rsqrt(var + EPS)
    return (z - mu) * rstd, rstd


def _ln_bwd(dy, xhat, rstd, g):
    dxh = dy * g
    return rstd * (dxh - jnp.mean(dxh, axis=-1, keepdims=True) - xhat * jnp.mean(dxh * xhat, axis=-1, keepdims=True))


def _rms(x, g):
    return x * lax.rsqrt(jnp.mean(jnp.square(x), axis=-1, keepdims=True) + EPS) * g


def _rope128(x, cos2, sin2):
    return x * cos2 + pltpu.roll(x, RET_DK // 2, 1) * sin2


def _unrope128(dy, cos2, sin2):
    return dy * cos2 + pltpu.roll(dy * sin2, RET_DK // 2, 1)


def _rope64(x, cosp, sina, sinb):
    h = MLA_ROPE // 2
    return x * cosp + pltpu.roll(x, h, 1) * sina + pltpu.roll(x, LANE - h, 1) * sinb


def _unrope64(dy, cosp, sina, sinb):
    h = MLA_ROPE // 2
    return dy * cosp + pltpu.roll(dy * sina, LANE - h, 1) + pltpu.roll(dy * sinb, h, 1)


RET_BLOCK = 256
RET_HEADS_PER_STEP = 4


def _ret_tables(H, B):
    log_g = np.log1p(-np.exp2(-5.0 - np.arange(H, dtype=np.float32))).astype(np.float32)
    idx = np.arange(B, dtype=np.float32)
    dist = np.abs(idx[:, None] - idx[None, :])
    visible = (np.arange(B)[None, :] // CHUNK) <= (np.arange(B)[:, None] // CHUNK)
    dmat = np.exp(log_g[:, None, None] * dist).astype(np.float32) * visible[None].astype(np.float32)
    qd = np.exp(log_g[:, None] * (idx + 1.0)).astype(np.float32)
    kd = np.exp(log_g[:, None] * (B - 1 - idx)).astype(np.float32)
    sd = np.exp(log_g * B).astype(np.float32)
    bc = lambda v: np.ascontiguousarray(np.broadcast_to(v[:, :, None], (H, v.shape[1], LANE)))
    sdb = np.ascontiguousarray(np.broadcast_to(sd[:, None, None], (H, SUBLANE, LANE)))
    return jnp.asarray(dmat), jnp.asarray(bc(qd)), jnp.asarray(bc(kd)), jnp.asarray(sdb)


def _ret_specs(H, B, reverse, nb):
    blk = (lambda b: nb - 1 - b) if reverse else (lambda b: b)
    G = min(RET_HEADS_PER_STEP, H)
    W = G * LANE
    col = lambda off: pl.BlockSpec((B, W), functools.partial(lambda h, b, off: (blk(b), off // G + h), off=off))
    row = pl.BlockSpec((B, LANE), lambda h, b: (blk(b), 0))
    per_head = lambda r: pl.BlockSpec((G, r, LANE), lambda h, b: (h, 0, 0))
    dmat = pl.BlockSpec((G, B, B), lambda h, b: (h, 0, 0))
    return G, col, row, per_head, dmat


def _ret_call(name, body, n_in, n_out, operands, in_specs, out_specs, out_shape, grid, G, jobs, own_alias=None):
    carried = _Carried(jobs, grid)

    def wrapped(*refs):
        ins, job_ins, outs, job_outs, (state,), job_sems = _split_refs(
            refs, n_in, len(carried.operands), n_out, len(carried.out_shape), 1)
        carried.before(job_ins, job_outs, job_sems)

        @pl.when(pl.program_id(1) == 0)
        def _():
            state[...] = jnp.zeros_like(state)

        body(*ins, *outs, state)
        carried.after(job_ins, job_outs, job_sems)

    return pl.pallas_call(
        wrapped, name=name, grid=grid,
        in_specs=in_specs + carried.in_specs, out_specs=out_specs + carried.out_specs,
        out_shape=out_shape + carried.out_shape,
        scratch_shapes=[pltpu.VMEM((G, RET_DK, RET_DV), F32)] + carried.sems,
        input_output_aliases={**(own_alias or {}), **carried.aliases(n_in, n_out)},
        compiler_params=_params(2),
    )(*operands, *carried.operands)


def _retention_fwd(p1, cos2, sin2, gn_g, mix_in, H, jobs=()):
    T = p1.shape[0]
    B = min(RET_BLOCK, T)
    nb = T // B
    scale = RET_DK ** -0.5
    dmat, qd, kd, sd = _ret_tables(H, B)
    G, col, row, per_head, dspec = _ret_specs(H, B, False, nb)

    def body(q_ref, k_ref, v_ref, g_ref, cos_ref, sin_ref, d_ref, qd_ref, kd_ref, sd_ref, gn_ref, _, pre_ref, out_ref, s_ref):
        cos, sin = cos_ref[...], sin_ref[...]
        for g in range(G):
            hl = pl.ds(g * LANE, LANE)
            q = _rope128(q_ref[:, hl], cos, sin)
            k = _rope128(k_ref[:, hl], cos, sin) * scale
            v = v_ref[:, hl]
            s = s_ref[g]
            a = _dot(q, k, NT) * d_ref[g]
            ret = _dot(a, v) + _dot(q * qd_ref[g], s)
            s_ref[g] = sd_ref[g, 0:1, :] * s + _dot(k * kd_ref[g], v, TN)
            pre_ref[:, hl] = ret
            nrm, _ = _ln_stats(ret)
            out_ref[:, hl] = (jax.nn.silu(g_ref[:, hl]) * (nrm * gn_ref[:, hl])).astype(out_ref.dtype)

    out_spec = pl.BlockSpec((B, G * LANE), lambda h, b: (b, h))
    return _ret_call(
        "retention_fwd", body, 12, 2, (p1, p1, p1, p1, cos2, sin2, dmat, qd, kd, sd, gn_g, mix_in),
        [col(0), col(H), col(2 * H), col(3 * H), row, row, dspec, per_head(B), per_head(B), per_head(SUBLANE),
         pl.BlockSpec((1, G * LANE), lambda h, b: (0, h)), ANY],
        [out_spec, out_spec],
        [jax.ShapeDtypeStruct((T, H * RET_DV), F32), jax.ShapeDtypeStruct(mix_in.shape, mix_in.dtype)],
        (H // G, nb), G, jobs, own_alias={11: 1})


def _retention_bwd(p1, dpre, cos2, sin2, H, dq_jobs=(), dkv_jobs=()):
    T = p1.shape[0]
    B = min(RET_BLOCK, T)
    nb = T // B
    scale = RET_DK ** -0.5
    dmat, qd, kd, sd = _ret_tables(H, B)

    def dq_body(k_ref, v_ref, do_ref, cos_ref, sin_ref, d_ref, qd_ref, kd_ref, sd_ref, dq_ref, s_ref):
        cos, sin = cos_ref[...], sin_ref[...]
        for g in range(G):
            hl = pl.ds(g * LANE, LANE)
            k = _rope128(k_ref[:, hl], cos, sin) * scale
            v, do, s = v_ref[:, hl], do_ref[:, hl], s_ref[g]
            da = _dot(do, v, NT) * d_ref[g]
            dq = _dot(da, k) + _dot(do, s, NT) * qd_ref[g]
            dq_ref[:, hl] = _unrope128(dq, cos, sin).astype(dq_ref.dtype)
            s_ref[g] = sd_ref[g, 0:1, :] * s + _dot(k * kd_ref[g], v, TN)

    G, col, row, per_head, dspec = _ret_specs(H, B, False, nb)
    dq = _ret_call(
        "retention_bwd_dq", dq_body, 9, 1, (p1, p1, dpre, cos2, sin2, dmat, qd, kd, sd),
        [col(H), col(2 * H), col(0), row, row, dspec, per_head(B), per_head(B), per_head(SUBLANE)],
        [pl.BlockSpec((B, G * LANE), lambda h, b: (b, h))], [jax.ShapeDtypeStruct((T, H * RET_DK), BF16)],
        (H // G, nb), G, dq_jobs)

    def dkv_body(q_ref, k_ref, v_ref, do_ref, cos_ref, sin_ref, d_ref, qd_ref, kd_ref, sd_ref, dk_ref, dv_ref, g_ref):
        cos, sin = cos_ref[...], sin_ref[...]
        for g in range(G):
            hl = pl.ds(g * LANE, LANE)
            q = _rope128(q_ref[:, hl], cos, sin)
            k = _rope128(k_ref[:, hl], cos, sin) * scale
            v, do, st = v_ref[:, hl], do_ref[:, hl], g_ref[g]
            a = _dot(q, k, NT) * d_ref[g]
            da = _dot(do, v, NT) * d_ref[g]
            dv = _dot(a, do, TN) + _dot(k * kd_ref[g], st)
            dk = _dot(da, q, TN) + _dot(v, st, NT) * kd_ref[g]
            dv_ref[:, hl] = dv.astype(dv_ref.dtype)
            dk_ref[:, hl] = _unrope128(dk * scale, cos, sin).astype(dk_ref.dtype)
            g_ref[g] = sd_ref[g, 0:1, :] * st + _dot(q * qd_ref[g], do, TN)

    G, col, row, per_head, dspec = _ret_specs(H, B, True, nb)
    out_spec = pl.BlockSpec((B, G * LANE), lambda h, b: (nb - 1 - b, h))
    dkv = _ret_call(
        "retention_bwd_dkv", dkv_body, 10, 2, (p1, p1, p1, dpre, cos2, sin2, dmat, qd, kd, sd),
        [col(0), col(H), col(2 * H), col(0), row, row, dspec, per_head(B), per_head(B), per_head(SUBLANE)],
        [out_spec, out_spec],
        [jax.ShapeDtypeStruct((T, H * RET_DK), BF16), jax.ShapeDtypeStruct((T, H * RET_DV), BF16)],
        (H // G, nb), G, dkv_jobs)
    return dq, dkv


ATT_BLOCK = 512


ATT_ROW_CHUNKS = 2
ATT_BWD_ROW_CHUNKS = 1
ATT_HEADS_PER_STEP = 8
ATT_BWD_HEADS_PER_STEP = 2
LOG2E = 1.4426950408889634


def _chunk_mask(rows, row0, cols):
    qi = (lax.broadcasted_iota(jnp.int32, (rows, cols), 0) + row0) // CHUNK
    kj = lax.broadcasted_iota(jnp.int32, (rows, cols), 1) // CHUNK
    return kj <= qi


def _attention_fwd(qh, kh, vx, H, jobs=()):
    T = qh.shape[0]
    t = min(ATT_BLOCK, T)
    n = T // t
    r = t // ATT_ROW_CHUNKS
    G = min(ATT_HEADS_PER_STEP, H)
    P = MLA_HEAD_PAD
    scale = (MLA_NOPE + MLA_ROPE) ** -0.5
    c2 = scale * LOG2E
    carried = _Carried(jobs, (H // G, n, n))

    def body(*refs):
        (q_ref, k_ref, v_ref), job_ins, (o_ref, ob_ref, lse_ref), job_outs, (m_ref, acc_ref), job_sems = _split_refs(
            refs, 3, len(carried.operands), 3, len(carried.out_shape), 2)
        i, j = pl.program_id(1), pl.program_id(2)
        carried.before(job_ins, job_outs, job_sems)

        @pl.when(j == 0)
        def _():
            m_ref[...] = jnp.full_like(m_ref, -jnp.inf)
            acc_ref[...] = jnp.zeros_like(acc_ref)

        def update(masked):
            for g in range(G):
                hc = pl.ds(g * P, P)
                for ch in range(ATT_ROW_CHUNKS):
                    rs = pl.ds(ch * r, r)
                    kc = (ch + 1) * r if masked else t
                    ks = pl.ds(0, kc)
                    s = _dot(q_ref[rs, hc], k_ref[ks, hc], NT)
                    if masked:
                        s = jnp.where(_chunk_mask(r, ch * r, kc), s, -jnp.inf)
                    m_prev = m_ref[rs, pl.ds(g * LANE, LANE)]
                    m_new = jnp.maximum(m_prev, jnp.max(s, axis=-1, keepdims=True))
                    alpha = jnp.exp2((m_prev - m_new) * c2)
                    p = jnp.exp2((s - jnp.tile(m_new, (1, kc // LANE))) * c2)
                    acc_ref[rs, hc] = jnp.tile(alpha, (1, 2)) * acc_ref[rs, hc] + _dot(p, v_ref[ks, hc])
                    m_ref[rs, pl.ds(g * LANE, LANE)] = m_new

        pl.when(j < i)(functools.partial(update, False))
        pl.when(j == i)(functools.partial(update, True))

        @pl.when(j == n - 1)
        def _():
            for g in range(G):
                acc = acc_ref[:, pl.ds(g * P, P)]
                l = acc[:, MLA_DV:]
                o = acc[:, :MLA_DV] / l
                o_ref[:, pl.ds(g * MLA_DV, MLA_DV)] = o
                ob_ref[:, pl.ds(g * MLA_DV, MLA_DV)] = o.astype(ob_ref.dtype)
                lse_ref[g] = m_ref[:, pl.ds(g * LANE, LANE)] * scale + jnp.log(l)

        carried.after(job_ins, job_outs, job_sems)

    kclamp = lambda i, j: jnp.minimum(j, i)
    o_spec = pl.BlockSpec((t, G * MLA_DV), lambda h, i, j: (i, h))
    mix_spec = pl.BlockSpec((t, G * MLA_DV), lambda h, i, j: (i, H // G + h))
    return pl.pallas_call(
        body, name="attention_fwd", grid=(H // G, n, n),
        in_specs=[pl.BlockSpec((t, G * P), lambda h, i, j: (i, h)),
                  pl.BlockSpec((t, G * P), lambda h, i, j: (kclamp(i, j), h)),
                  pl.BlockSpec((t, G * P), lambda h, i, j: (kclamp(i, j), h))] + carried.in_specs,
        out_specs=[o_spec, mix_spec, pl.BlockSpec((G, t, LANE), lambda h, i, j: (h, i, 0))] + carried.out_specs,
        out_shape=[jax.ShapeDtypeStruct((T, H * MLA_DV), F32), jax.ShapeDtypeStruct((T, 2 * H * MLA_DV), BF16),
                   jax.ShapeDtypeStruct((H, T, LANE), F32)] + carried.out_shape,
        scratch_shapes=[pltpu.VMEM((t, G * LANE), F32), pltpu.VMEM((t, G * P), F32)] + carried.sems,
        input_output_aliases=carried.aliases(3, 3),
        compiler_params=_params(3),
    )(qh, kh, vx, *carried.operands)


def _attention_bwd(qh, kh, vx, o, lse, dmix, H, jobs=()):
    T = qh.shape[0]
    t = min(ATT_BLOCK, T)
    n = T // t
    r = t // ATT_BWD_ROW_CHUNKS
    G = min(ATT_BWD_HEADS_PER_STEP, H)
    P = MLA_HEAD_PAD
    scale = (MLA_NOPE + MLA_ROPE) ** -0.5
    c2 = scale * LOG2E
    carried = _Carried(jobs, (H // G, n, n))

    def body(*refs):
        ((q_ref, k_ref, v_ref, do_ref, o_ref, lse_ref), job_ins, (dq_ref, dkn_ref, dkr_ref, dv_ref), job_outs,
         (dk_acc, dv_acc), job_sems) = _split_refs(refs, 6, len(carried.operands), 4, len(carried.out_shape), 2)
        j, i = pl.program_id(1), pl.program_id(2)
        carried.before(job_ins, job_outs, job_sems)

        @pl.when(jnp.logical_and(j == 0, i == 0))
        def _():
            dq_ref[...] = jnp.zeros_like(dq_ref)

        @pl.when(i == j)
        def _():
            dk_acc[...] = jnp.zeros_like(dk_acc)
            dv_acc[...] = jnp.zeros_like(dv_acc)

        def update(masked):
            for g in range(G):
                hc, hv = pl.ds(g * P, P), pl.ds(g * MLA_DV, MLA_DV)
                for ch in range(ATT_BWD_ROW_CHUNKS):
                    rs = pl.ds(ch * r, r)
                    kc = (ch + 1) * r if masked else t
                    ks = pl.ds(0, kc)
                    k, v = k_ref[ks, hc], v_ref[ks, pl.ds(g * P, MLA_DV)]
                    q, do = q_ref[rs, hc], do_ref[rs, hv]
                    s = _dot(q, k, NT)
                    if masked:
                        s = jnp.where(_chunk_mask(r, ch * r, kc), s, -jnp.inf)
                    p = jnp.exp2(s * c2 - jnp.tile(lse_ref[g, rs, :] * LOG2E, (1, kc // LANE)))
                    dv_acc[ks, hv] += _dot(p, do, TN)
                    dp = _dot(do, v, NT)
                    delta = jnp.sum(do * o_ref[rs, hv], axis=-1, keepdims=True)
                    ds = p * (dp - delta)
                    rows = pl.ds(pl.multiple_of(i * t, t) + ch * r, r)
                    dq_ref[rows, hc] += _dot(ds, k)
                    dk_acc[ks, hc] += _dot(ds, q, TN)

        pl.when(i > j)(functools.partial(update, False))
        pl.when(i == j)(functools.partial(update, True))

        @pl.when(i == n - 1)
        def _():
            for g in range(G):
                dk = dk_acc[:, pl.ds(g * P, P)] * scale
                dkn_ref[:, pl.ds(g * LANE, LANE)] = dk[:, :MLA_NOPE].astype(dkn_ref.dtype)
                dkr_ref[:, pl.ds(g * LANE, LANE)] = dk[:, MLA_NOPE:]
            dv_ref[...] = dv_acc[...].astype(dv_ref.dtype)

        carried.after(job_ins, job_outs, job_sems)

    qclamp = lambda j, i: jnp.maximum(i, j)
    kv_out = pl.BlockSpec((t, G * LANE), lambda h, j, i: (j, h))
    q_in = pl.BlockSpec((t, G * MLA_DV), lambda h, j, i: (qclamp(j, i), h))
    return pl.pallas_call(
        body, name="attention_bwd", grid=(H // G, n, n),
        in_specs=[pl.BlockSpec((t, G * P), lambda h, j, i: (qclamp(j, i), h)),
                  pl.BlockSpec((t, G * P), lambda h, j, i: (j, h)),
                  pl.BlockSpec((t, G * P), lambda h, j, i: (j, h)),
                  pl.BlockSpec((t, G * MLA_DV), lambda h, j, i: (qclamp(j, i), H // G + h)),
                  q_in,
                  pl.BlockSpec((G, t, LANE), lambda h, j, i: (h, qclamp(j, i), 0))] + carried.in_specs,
        out_specs=[pl.BlockSpec((T, G * P), lambda h, j, i: (0, h)), kv_out, kv_out, kv_out] + carried.out_specs,
        out_shape=[jax.ShapeDtypeStruct((T, H * P), F32), jax.ShapeDtypeStruct((T, H * LANE), BF16),
                   jax.ShapeDtypeStruct((T, H * LANE), F32), jax.ShapeDtypeStruct((T, H * LANE), BF16)] + carried.out_shape,
        scratch_shapes=[pltpu.VMEM((t, G * P), F32), pltpu.VMEM((t, G * MLA_DV), F32)] + carried.sems,
        input_output_aliases=carried.aliases(6, 4),
        compiler_params=_params(3),
    )(qh, kh, vx, dmix, o, lse, *carried.operands)


def _adamw_math(w, g, m, v):
    m = ADAM_B1 * m + (1.0 - ADAM_B1) * g
    v = ADAM_B2 * v + (1.0 - ADAM_B2) * jnp.square(g)
    m_hat = m / (1.0 - ADAM_B1 ** ADAM_STEP)
    v_hat = v / (1.0 - ADAM_B2 ** ADAM_STEP)
    delta = -ADAM_LR * (m_hat / (jnp.sqrt(v_hat) + ADAM_EPS) + ADAM_WD * w)
    return delta, m, v


def _adamw(name, chip_sums, my_chip, landed, w, m, v, jobs=()):
    R, C = w.shape
    tr, tc, tiles = _tile_2d(R, C, ADAMW_TILE_ELEMS)
    at = _tile_index(tr, tc, C)
    nl = len(landed)
    assert sum(l.shape[0] for l in landed) == N_CHIP - 1
    carried = _Carried(jobs, (tiles,))

    def body(q_ref, *refs):
        (own_ref, *rest), job_ins, (g_out, d_out, m_out, v_out), job_outs, _, job_sems = _split_refs(
            refs, nl + 4, len(carried.operands), 4, len(carried.out_shape), 0)
        l_refs, (w_ref, m_ref, v_ref) = rest[:nl], rest[nl:]
        carried.before(job_ins, job_outs, job_sems)
        g = own_ref[...].astype(F32)
        for l_ref in l_refs:
            for s in range(l_ref.shape[0]):
                g = g + l_ref[s].astype(F32)
        delta, mn, vn = _adamw_math(w_ref[...], g, m_ref[...], v_ref[...])
        g_out[...] = g
        d_out[...] = delta
        m_out[...] = mn
        v_out[...] = vn
        carried.after(job_ins, job_outs, job_sems)

    spec = pl.BlockSpec((tr, tc), lambda i, q_ref: at(i))
    own_spec = pl.BlockSpec((None, tr, tc), lambda i, q_ref: (q_ref[0], *at(i)))
    l_specs = [pl.BlockSpec((l.shape[0], tr, tc), lambda i, q_ref: (0, *at(i))) for l in landed]
    return pl.pallas_call(
        body, name=name,
        grid_spec=pltpu.PrefetchScalarGridSpec(
            num_scalar_prefetch=1, grid=(tiles,),
            in_specs=[own_spec] + l_specs + [spec, spec, spec] + carried.in_specs,
            out_specs=[spec] * 4 + carried.out_specs, scratch_shapes=carried.sems),
        out_shape=[jax.ShapeDtypeStruct((R, C), F32)] * 4 + carried.out_shape,
        input_output_aliases={1 + i: o for i, o in carried.aliases(nl + 4, 4).items()},
        compiler_params=_params(1),
    )(my_chip, chip_sums, *landed, w, m, v, *carried.operands)


def _adamw_small(landed, w, m, v):
    P = w.shape[1]

    def body(l_ref, w_ref, m_ref, v_ref, g_out, d_out, m_out, v_out, loss_out):
        acc = l_ref[0]
        for s in range(1, N_DEV):
            acc = acc + l_ref[s]
        total = jnp.sum(acc, axis=0, keepdims=True)
        g = total[:, :P]
        delta, mn, vn = _adamw_math(w_ref[...], g, m_ref[...], v_ref[...])
        g_out[...] = g
        d_out[...] = delta
        m_out[...] = mn
        v_out[...] = vn
        loss_out[...] = total[:, P:]

    return pl.pallas_call(
        body, name="adamw_replicated",
        out_shape=[jax.ShapeDtypeStruct((1, P), F32)] * 4 + [jax.ShapeDtypeStruct((1, LANE), F32)],
    )(landed, w, m, v)


def _rope_rows():
    inv_r = ROPE_BASE ** (-jnp.arange(0, RET_DK, 2, dtype=F32) / RET_DK)
    inv_m = ROPE_BASE ** (-jnp.arange(0, MLA_ROPE, 2, dtype=F32) / MLA_ROPE)
    h = MLA_ROPE // 2
    one, zero = jnp.ones((h,), F32), jnp.zeros((h,), F32)
    pad = jnp.zeros((LANE - MLA_ROPE,), F32)
    rows = [jnp.concatenate([inv_r, inv_r]), jnp.concatenate([-jnp.ones((RET_DK // 2,), F32), jnp.ones((RET_DK // 2,), F32)]),
            jnp.concatenate([inv_m, inv_m, pad]), jnp.concatenate([one, one, pad]),
            jnp.concatenate([zero, one, pad]), jnp.concatenate([-one, zero, pad])]
    return jnp.stack(rows + [jnp.zeros((LANE,), F32)] * (SUBLANE - len(rows)))


def _prep_tile(xt, p, rows):
    ang2, angp = p * rows[0:1], p * rows[2:3]
    sp = jnp.sin(angp)
    return (xt, jnp.cos(ang2), jnp.sin(ang2) * rows[1:2], jnp.cos(angp) * rows[3:4], sp * rows[4:5], sp * rows[5:6])


def kernel(x, positions, w_in, q_norm_g, w_uq, kv_norm_g, w_uk, w_uv, ret_gn_g, w_out, ln1_g, ln1_b, w_up, w_down, ln2_g, ln2_b, loss_target, m_w_in, m_q_norm_g, m_w_uq, m_kv_norm_g, m_w_uk, m_w_uv, m_ret_gn_g, m_w_out, m_ln1_g, m_ln1_b, m_w_up, m_w_down, m_ln2_g, m_ln2_b, v_w_in, v_q_norm_g, v_w_uq, v_kv_norm_g, v_w_uk, v_w_uv, v_ret_gn_g, v_w_out, v_ln1_g, v_ln1_b, v_w_up, v_w_down, v_ln2_g, v_ln2_b):
    T, D = x.shape[1], x.shape[2]
    H = D // 256
    RW = H * RET_DV
    x = x[0]
    tgt = loss_target[0]
    alpha = DEEPNORM_ALPHA
    in_width = w_in.shape[2] * N_DEV
    mla_in = in_width - 4 * RW
    mla_in_pad = -(-mla_in // 512) * 512
    f_uq = MLA_NOPE + MLA_ROPE

    names = ["w_in", "w_uq", "w_uk", "w_uv", "w_out", "w_up", "w_down"]
    big = dict(zip(names, [w_in[0].T, w_uq[0], w_uk[0], w_uv[0], w_out[0], w_up[0], w_down[0]]))
    kind = dict(zip(names, ["blk", "blk", "cols", "cols", "rows", "cols", "rows"]))
    half = {n: w.astype(BF16) for n, w in big.items()}

    def gather(ns, **parts):
        return _GatherJob([half[n] for n in ns], [kind[n] for n in ns], 0.55, 0.9, **parts)

    tr = min(256, T)
    xb, cos2, sin2, cosp, sina, sinb, g_in = _rowwise(
        "prep", _prep_tile, [(x, D, 0), (positions[0].astype(F32)[:, None], 1, 0)], [_rope_rows()],
        [(D, BF16)] + [(LANE, F32)] * 5, [], tr, jobs=[gather(["w_in"])])
    win = g_in.reshape(in_width, D)
    w1, n1 = win, 4 * RW
    w2 = jnp.pad(win[4 * RW:], ((0, mla_in_pad - mla_in), (0, 0)))

    first, second, near, far = (0, 2), (1, 2), (CHIP_X, CHIP_Y), (CHIP_DIAGONAL,)
    p1, g_uq, wuk, wuv, wout = _matmul("proj_ret", xb, w1, tb=True, n=n1,
                                       jobs=[gather(["w_uq", "w_uk", "w_uv", "w_out"])])
    wuq = jnp.transpose(g_uq, (1, 0, 2)).reshape(MLA_Q_RANK, H, f_uq)
    wuq = jnp.pad(wuq, ((0, 0), (0, 0), (0, MLA_HEAD_PAD - f_uq))).reshape(MLA_Q_RANK, H * MLA_HEAD_PAD)
    wukv = jnp.concatenate([wuk, wuv], axis=1)
    p2 = _matmul("proj_mla", xb, w2, tb=True)
    c_kv, c_kr = MLA_Q_RANK, MLA_Q_RANK + MLA_KV_RANK

    def mla_prep(p, cp, sa, sb, gq, gkv):
        cqn = _rms(p[:, :c_kv], gq)
        ckvn = _rms(p[:, c_kv:c_kr], gkv)
        return cqn, ckvn, _rope64(p[:, c_kr:c_kr + LANE], cp, sa, sb)

    cqn, ckvn, krope = _rowwise(
        "mla_prep", mla_prep, [(p2, mla_in_pad, 0), (cosp, LANE, 0), (sina, LANE, 0), (sinb, LANE, 0)],
        [q_norm_g, kv_norm_g], [(MLA_Q_RANK, BF16), (MLA_KV_RANK, BF16), (LANE, BF16)], [], tr)
    qf = _matmul("mla_q", cqn, wuq)
    kv = _matmul("mla_kv", ckvn, wukv, out_dtypes=(BF16,))

    def mla_heads(q, kn, vv, kr, cp, sa, sb):
        qs, ks, vs = [], [], []
        ones = jnp.ones((q.shape[0], MLA_DV), vv.dtype)
        for h in range(H):
            o = h * MLA_HEAD_PAD
            qs += [q[:, o:o + MLA_NOPE], _rope64(q[:, o + MLA_NOPE:o + MLA_HEAD_PAD], cp, sa, sb)]
            ks += [kn[:, h * MLA_NOPE:(h + 1) * MLA_NOPE], kr]
            vs += [vv[:, h * MLA_DV:(h + 1) * MLA_DV], ones]
        return jnp.concatenate(qs, axis=1), jnp.concatenate(ks, axis=1), jnp.concatenate(vs, axis=1)

    qh, kh, vx = _rowwise(
        "mla_heads", mla_heads,
        [(qf, H * MLA_HEAD_PAD, 0), (kv, RW, 0), (kv, RW, 1), (krope, LANE, 0), (cosp, LANE, 0), (sina, LANE, 0), (sinb, LANE, 0)],
        [], [(H * MLA_HEAD_PAD, BF16)] * 3, [], tr)
    whole = (0, 1)
    att_o, mix_half, lse, wup_a = _attention_fwd(
        qh, kh, vx, H, jobs=[gather(["w_up"], pieces=[(first, near, True, 0.45), (second, near, True, 0.75)])])
    ret_pre, mixin, wup = _retention_fwd(
        p1, cos2, sin2, ret_gn_g, mix_half, H,
        jobs=[gather(["w_up"], pieces=[(first, far, False, 0.55), (second, far, False, 0.9)], base=[wup_a])])
    mix, wdown_a = _matmul("mix_out", mixin, wout,
                           jobs=[gather(["w_down"], pieces=[((0, 4), near, True, 0.5), ((1, 4), near, True, 0.9)])])

    def ln1_fwd(xt, mt, g, b):
        z = alpha * xt + mt
        xhat, _ = _ln_stats(z)
        y = xhat * g + b
        return z, y, y

    z1, x1, x1b, wdown_b = _rowwise("ln1", ln1_fwd, [(x, D, 0), (mix, D, 0)], [ln1_g, ln1_b],
                                    [(D, F32), (D, F32), (D, BF16)], [], tr,
                                    jobs=[gather(["w_down"], pieces=[((2, 4), near, True, 0.5), ((3, 4), near, True, 0.9)],
                                                 base=[wdown_a])])

    def relu2(acc):
        r = jnp.maximum(acc, 0.0)
        return r * r, r

    a2b, rb, wdown = _matmul("mlp_up", x1b, wup, out_dtypes=(BF16, BF16), epilogue=relu2,
                             jobs=[gather(["w_down"], pieces=[(whole, far, False, 0.6)], base=[wdown_b])])
    hmlp = _matmul("mlp_down", a2b, wdown)

    def ln2_loss(x1t, ht, tt, g, b):
        xhat, rstd = _ln_stats(alpha * x1t + ht)
        err = xhat * g + b - tt
        dy = err / D
        dz = _ln_bwd(dy, xhat, rstd, g)
        lrow = 0.5 * jnp.mean(jnp.square(err), axis=-1, keepdims=True)
        return dz, dz, dy * xhat, dy, jnp.broadcast_to(lrow, (lrow.shape[0], LANE))

    dz2, dz2b, dg2, db2, lpart = _rowwise("ln2_loss", ln2_loss, [(x1, D, 0), (hmlp, D, 0), (tgt, D, 0)], [ln2_g, ln2_b],
                                          [(D, F32), (D, BF16)], [D, D, LANE], tr)

    def pair(grads):
        ns = list(grads)
        return _PairJob([grads[n] for n in ns], [kind[n] for n in ns], [big[n].shape for n in ns])

    my_core = lax.axis_index("c").astype(jnp.int32).reshape(1)
    my_chip = (2 * lax.axis_index("x") + lax.axis_index("y")).astype(jnp.int32).reshape(1)
    chip_sums = {}

    def pair_sums(grads, theirs):
        for n, t in zip(grads, theirs):
            chip_sums[n] = _pair_sum("pair_sum_" + n, grads[n], kind[n], big[n].shape, t, my_core)
        return {n: chip_sums[n] for n in grads}

    def chip(sums, rels=CHIP_RELS, by_source=None):
        ns = list(sums)
        return _ChipJob([sums[n] for n in ns], [big[n].shape for n in ns], rels, by_source)

    landed = {}
    da = _matmul("mlp_down_dx", dz2b, wdown, tb=True, out_dtypes=(BF16,), extras=(rb,),
                 epilogue=lambda acc, r: (acc * (2.0 * r.astype(F32)),))
    g_down = {"w_down": _matmul("mlp_down_dw", a2b, dz2b, ta=True, out_dtypes=(BF16,))}
    g_wup, *pr = _matmul("mlp_up_dw", x1b, da, ta=True, out_dtypes=(BF16,), jobs=[pair(g_down)])
    sums_down = pair_sums(g_down, pr)
    g_up = {"w_up": g_wup}
    dx1m, l_a, *pr = _matmul("mlp_up_dx", da, wup, tb=True, jobs=[chip(sums_down, CHIP_RELS[:2]), pair(g_up)])
    sums_up = pair_sums(g_up, pr)

    def ln1_bwd(dm, dzz, z, g):
        xhat, rstd = _ln_stats(z)
        dy = dm + alpha * dzz
        dz = _ln_bwd(dy, xhat, rstd, g)
        return dz, dz, dy * xhat, dy

    dz1, dz1b, dg1, db1 = _rowwise("ln1_bwd", ln1_bwd, [(dx1m, D, 0), (dz2, D, 0), (z1, D, 0)], [ln1_g],
                                   [(D, F32), (D, BF16)], [D, D], tr)
    g_wout = _matmul("mix_out_dw", mixin, dz1b, ta=True, out_dtypes=(BF16,))
    dmix = _matmul("mix_out_dx", dz1b, wout, tb=True)

    dqh, dkn, dkr, dvv, l_b, l_c = _attention_bwd(qh, kh, vx, att_o, lse, dmix, H,
                                                  jobs=[chip(sums_down, CHIP_RELS[2:]), chip(sums_up)])
    landed["w_down"], landed["w_up"] = [l_a, l_b], [l_c]
    att_scale = (MLA_NOPE + MLA_ROPE) ** -0.5

    def mla_heads_bwd(dq, dkrh, cp, sa, sb):
        parts, acc = [], dkrh[:, :LANE]
        dq = dq * att_scale
        for h in range(H):
            o = h * MLA_HEAD_PAD
            parts += [dq[:, o:o + MLA_NOPE], _unrope64(dq[:, o + MLA_NOPE:o + MLA_HEAD_PAD], cp, sa, sb)]
            if h:
                acc = acc + dkrh[:, h * LANE:(h + 1) * LANE]
        return jnp.concatenate(parts, axis=1), _unrope64(acc, cp, sa, sb)

    dqb, dkr128 = _rowwise(
        "mla_heads_bwd", mla_heads_bwd,
        [(dqh, H * MLA_HEAD_PAD, 0), (dkr, H * LANE, 0), (cosp, LANE, 0), (sina, LANE, 0), (sinb, LANE, 0)],
        [], [(H * MLA_HEAD_PAD, BF16), (LANE, F32)], [], tr)
    dcqn = _matmul("mla_q_dx", dqb, wuq, tb=True)
    g_wuq = _matmul("mla_q_dw", cqn, dqb, ta=True, out_dtypes=(BF16,))
    dkvb = jnp.concatenate([dkn, dvv], axis=1)
    dckvn = _matmul("mla_kv_dx", dkvb, wukv, tb=True)
    g_wukv = _matmul("mla_kv_dw", ckvn, dkvb, ta=True, out_dtypes=(BF16,))

    def rms_bwd(c, dy, g):
        rstd = lax.rsqrt(jnp.mean(jnp.square(c), axis=-1, keepdims=True) + EPS)
        dyg = dy * g
        dc = rstd * (dyg - c * (rstd * rstd) * jnp.mean(dyg * c, axis=-1, keepdims=True))
        return dc, dy * c * rstd

    def mla_prep_bwd(p, dq_, dkv_, dkr_, gq, gkv):
        dcq, dgq_ = rms_bwd(p[:, :c_kv], dq_, gq)
        dckv, dgkv_ = rms_bwd(p[:, c_kv:c_kr], dkv_, gkv)
        pad = jnp.zeros((p.shape[0], mla_in_pad - c_kr - LANE), F32)
        return jnp.concatenate([dcq, dckv, dkr_, pad], axis=1), dgq_, dgkv_

    d_p2, dgq, dgkv = _rowwise(
        "mla_prep_bwd", mla_prep_bwd, [(p2, mla_in_pad, 0), (dcqn, MLA_Q_RANK, 0), (dckvn, MLA_KV_RANK, 0), (dkr128, LANE, 0)],
        [q_norm_g, kv_norm_g], [(mla_in_pad, BF16)], [MLA_Q_RANK, MLA_KV_RANK], tr)

    def gate_bwd(pre, gate, dout, gn):
        dpre, dgate, dgn = [], [], []
        for h in range(H):
            sl = slice(h * RET_DV, (h + 1) * RET_DV)
            nrm, rstd = _ln_stats(pre[:, sl])
            gate_h, do_h, gn_h = gate[:, sl], dout[:, sl], gn[:, sl]
            sg = jax.nn.sigmoid(gate_h)
            silu = gate_h * sg
            dgate.append(do_h * (nrm * gn_h) * (sg * (1.0 + gate_h * (1.0 - sg))))
            dn = do_h * silu
            dgn.append(dn * nrm)
            dpre.append(_ln_bwd(dn, nrm, rstd, gn_h))
        return jnp.concatenate(dpre, axis=1), jnp.concatenate(dgate, axis=1), jnp.concatenate(dgn, axis=1)

    dpre, drg, dgn = _rowwise("ret_gate_bwd", gate_bwd, [(ret_pre, RW, 0), (p1, RW, 3), (dmix, RW, 0)], [ret_gn_g],
                              [(RW, F32), (RW, BF16)], [RW], tr)
    g_uq = g_wuq.reshape(MLA_Q_RANK, H, MLA_HEAD_PAD)[:, :, :f_uq].reshape(MLA_Q_RANK, N_DEV, H * f_uq // N_DEV)
    mid_grads = {"w_uq": jnp.transpose(g_uq, (1, 0, 2)), "w_uk": g_wukv[:, :RW], "w_uv": g_wukv[:, RW:], "w_out": g_wout}
    (drq, *pr), (drk, drv) = _retention_bwd(p1, dpre, cos2, sin2, H, dq_jobs=[pair(mid_grads)])
    sums_mid = pair_sums(mid_grads, pr)
    d_p1 = jnp.concatenate([drq, drk, drv, drg], axis=1)

    g_w1, *l_a = _matmul("proj_ret_dw", d_p1, xb, ta=True, out_dtypes=(BF16,), jobs=[chip(sums_mid)])
    for n, a_ in zip(mid_grads, l_a):
        landed[n] = [a_]
    g_w2 = _matmul("proj_mla_dw", d_p2, xb, ta=True, out_dtypes=(BF16,))

    g_win = {"w_in": jnp.concatenate([g_w1, g_w2[:mla_in]], axis=0).reshape(N_DEV, in_width // N_DEV, D)}
    small = jnp.concatenate([dgq, dgkv, dgn, dg1, db1, dg2, db2, lpart], axis=1)
    dx_a, *pr = _matmul("proj_mla_dx", d_p2, w2, extras=(dz1,), epilogue=lambda acc, d: (acc + alpha * d,), jobs=[pair(g_win)])
    sums_in = pair_sums(g_win, pr)
    grad_x, l_a, small_landed = _matmul("proj_ret_dx", d_p1, w1, extras=(dx_a,), epilogue=lambda acc, d: (acc + d,),
                                        jobs=[chip(sums_in, CHIP_RELS[:2], by_source=small)])

    moments = dict(zip(names, zip([m_w_in, m_w_uq, m_w_uk, m_w_uv, m_w_out, m_w_up, m_w_down],
                                  [v_w_in, v_w_uq, v_w_uk, v_w_uv, v_w_out, v_w_up, v_w_down])))
    res = {}

    def adamw(n, jobs=()):
        as_held = (lambda a: a[0].T) if n == "w_in" else (lambda a: a[0])
        as_given = (lambda r: r.T[None]) if n == "w_in" else (lambda r: r[None])
        out = _adamw("adamw_" + n, chip_sums[n], my_chip, landed[n], big[n], as_held(moments[n][0]), as_held(moments[n][1]),
                     jobs=jobs)
        res[n] = [as_given(r) for r in out[:4]]
        return out[4:]

    (l_b,) = adamw("w_out", jobs=[chip(sums_in, CHIP_RELS[2:])])
    landed["w_in"] = [l_a, l_b]
    for n in names:
        if n != "w_out":
            adamw(n)

    small_names = ["q_norm_g", "kv_norm_g", "ret_gn_g", "ln1_g", "ln1_b", "ln2_g", "ln2_b"]
    small_w = [q_norm_g, kv_norm_g, ret_gn_g, ln1_g, ln1_b, ln2_g, ln2_b]
    small_m = [m_q_norm_g, m_kv_norm_g, m_ret_gn_g, m_ln1_g, m_ln1_b, m_ln2_g, m_ln2_b]
    small_v = [v_q_norm_g, v_kv_norm_g, v_ret_gn_g, v_ln1_g, v_ln1_b, v_ln2_g, v_ln2_b]
    cat = lambda arrs: jnp.concatenate(arrs, axis=1)
    *sres, loss_lanes = _adamw_small(small_landed, cat(small_w), cat(small_m), cat(small_v))
    loss = loss_lanes[0, 0]
    off = 0
    for n, w in zip(small_names, small_w):
        res[n] = [r[:, off:off + w.shape[1]] for r in sres]
        off += w.shape[1]

    order = ["w_in", "q_norm_g", "w_uq", "kv_norm_g", "w_uk", "w_uv", "ret_gn_g", "w_out", "ln1_g", "ln1_b",
             "w_up", "w_down", "ln2_g", "ln2_b"]
    outs = [loss, grad_x[None]]
    for k in range(4):
        outs += [res[n][k] for n in order]
    return tuple(outs)
```

```python
import functools

import numpy as np

import jax
import jax.numpy as jnp
from jax import lax
from jax.experimental import pallas as pl
from jax.experimental.pallas import tpu as pltpu

F32 = jnp.float32
BF16 = jnp.bfloat16

CHUNK = 64
RET_DK = 128
RET_DV = 128
MLA_NOPE = 128
MLA_ROPE = 64
MLA_DV = 128
MLA_Q_RANK = 768
MLA_KV_RANK = 512
MLA_HEAD_PAD = 256
ROPE_BASE = 10000.0
EPS = 1e-5
DEPTH = 1
DEEPNORM_ALPHA = (2.0 * DEPTH) ** 0.25
ADAM_LR = 0.001
ADAM_B1 = 0.9
ADAM_B2 = 0.999
ADAM_EPS = 1e-08
ADAM_WD = 0.01
ADAM_STEP = 10

N_DEV = 8
LANE = 128
SUBLANE = 8
VMEM_LIMIT = 48 * 1024 * 1024
VMEM_SMALL = 32 * 1024 * 1024
ADAMW_TILE_ELEMS = 128 * 1024
PAIR_TILE_ELEMS = 1024 * 1024
DW_TK = 4096

MESH = pl.DeviceIdType.MESH
ANY = pl.BlockSpec(memory_space=pl.ANY)

NN = (((1,), (0,)), ((), ()))
NT = (((1,), (1,)), ((), ()))
TN = (((0,), (0,)), ((), ()))


def _dot(a, b, dims=NN):
    return lax.dot_general(a.astype(BF16), b.astype(BF16), dims, preferred_element_type=F32)


def _params(n_axes, vmem=VMEM_LIMIT):
    return pltpu.CompilerParams(dimension_semantics=("arbitrary",) * n_axes, vmem_limit_bytes=vmem)


def _tile(n, want):
    if n <= want:
        return n
    t = want
    while t >= LANE:
        if n % t == 0:
            return t
        t -= LANE
    return n


def _my_place():
    return lax.axis_index("x"), lax.axis_index("y"), lax.axis_index("c")


def _flat(px, py, pc):
    return 4 * px + 2 * py + pc


def _block_of(ref, kind, idx, rows, cols, sub=None):
    lo, n = (0, rows) if sub is None else sub
    if kind == "blk":
        return ref.at[idx, pl.ds(lo, n), :]
    if kind == "rows":
        return ref.at[pl.ds(pl.multiple_of(idx * rows + lo, n), n), :]
    return ref.at[pl.ds(lo, n), pl.ds(pl.multiple_of(idx * cols, cols), cols)]


CHIP_X, CHIP_Y, CHIP_DIAGONAL = 0, 1, 2
ALL_CHIPS = (CHIP_X, CHIP_Y, CHIP_DIAGONAL)


class _GatherJob:
    def __init__(self, shards, kinds, mid, late, pieces=(((0, 1), ALL_CHIPS, True),), base=None):
        nw = len(shards)
        self.nw, self.kinds = nw, list(kinds)
        self.pieces = [tuple(p) if len(p) == 4 else (*p, late) for p in pieces]
        for _, chips, _, _ in self.pieces:
            assert CHIP_DIAGONAL not in chips or len(chips) in (1, 3)
        self.shapes = [s.shape for s in shards]
        self.ins = list(shards) + (list(base) if base is not None else [])
        self.alias = [(nw + w, w) for w in range(nw)] if base is not None else []
        self.out_shape = []
        for s, kind in zip(shards, kinds):
            r, c = s.shape
            shape = {"blk": (N_DEV, r, c), "rows": (N_DEV * r, c), "cols": (r, N_DEV * c)}[kind]
            self.out_shape.append(jax.ShapeDtypeStruct(shape, s.dtype))
        n = 7 * nw * len(self.pieces)
        self.sems = [pltpu.SemaphoreType.DMA((n,)), pltpu.SemaphoreType.DMA((n,)), pltpu.SemaphoreType.DMA((n,))]
        whens = sorted({p[3] for p in self.pieces})
        self.phases = ([(0.0, self._start), (mid, self._relay)] + [(f, functools.partial(self._pass_on, f)) for f in whens]
                       + [(1.0, self._finish)])

    def _each(self, src, dst, send_sems, recv_sems, local_sems):
        x, y, c = _my_place()
        all_chips = [(1 - x, y), (x, 1 - y), (1 - x, 1 - y)]
        out = []
        for w in range(self.nw):
            r, cc = self.shapes[w]
            for p, ((k_of, of), chips, own, when) in enumerate(self.pieces):
                sub = (k_of * (r // of), r // of)
                s0 = 7 * (w * len(self.pieces) + p)
                mine = src[w].at[pl.ds(*sub), :]

                def place(dev, w=w, r=r, cc=cc, sub=sub):
                    return _block_of(dst[w], self.kinds[w], _flat(*dev), r, cc, sub)

                def copy(k, block, to, from_src=False, place=place, mine=mine, s0=s0):
                    return pltpu.make_async_remote_copy(
                        src_ref=mine if from_src else place(block), dst_ref=place(block),
                        send_sem=send_sems.at[s0 + k], recv_sem=recv_sems.at[s0 + k], device_id=to, device_id_type=MESH)

                relay = functools.partial(copy, 1 + CHIP_DIAGONAL, (x ^ (1 - c), y ^ c, c), (x ^ c, y ^ (1 - c), c))
                local = functools.partial(pltpu.make_async_copy, mine, place((x, y, c)), local_sems.at[s0])
                near = [(j, all_chips[j]) for j in chips if j != CHIP_DIAGONAL]
                out.append((near, CHIP_DIAGONAL in chips, own, copy, relay, local, when))
        return (x, y, c), (x, y, 1 - c), all_chips[CHIP_DIAGONAL], out

    def _start(self, *refs):
        me, sibling, diagonal, each = self._each(*refs)
        for near, far, own, copy, relay, local, when in each:
            if own:
                local().start()
                copy(0, me, sibling, from_src=True).start()
            for j, chip in near:
                copy(1 + j, me, (*chip, me[2]), from_src=True).start()
            if far and not near:
                relay().start()

    def _landed(self, near, copy, me, sibling):
        for j, chip in near:
            copy(1 + j, (*chip, me[2]), me).wait_recv()
            copy(4 + j, (*chip, me[2]), sibling).start()

    def _relay(self, *refs):
        me, sibling, diagonal, each = self._each(*refs)
        for near, far, own, copy, relay, local, when in each:
            if far and near:
                self._landed(near, copy, me, sibling)
                relay().start()

    def _pass_on(self, frac, *refs):
        me, sibling, diagonal, each = self._each(*refs)
        for near, far, own, copy, relay, local, when in each:
            if when == frac:
                self._landed([(CHIP_DIAGONAL, diagonal)] if far else near, copy, me, sibling)

    def _finish(self, *refs):
        me, sibling, diagonal, each = self._each(*refs)
        for near, far, own, copy, relay, local, when in each:
            if own:
                copy(0, sibling, me).wait_recv()
            for j, chip in near + [(CHIP_DIAGONAL, diagonal)] * int(far):
                copy(4 + j, (*chip, 1 - me[2]), me).wait_recv()
        for near, far, own, copy, relay, local, when in each:
            if own:
                copy(0, me, sibling, from_src=True).wait_send()
                local().wait()
            for j, chip in near:
                copy(1 + j, me, (*chip, me[2]), from_src=True).wait_send()
                copy(4 + j, (*chip, me[2]), sibling).wait_send()
            if far:
                relay().wait_send()
                copy(4 + CHIP_DIAGONAL, (*diagonal, me[2]), sibling).wait_send()


REL_ALL = [(1, 0, 0), (1, 0, 1), (0, 1, 0), (0, 1, 1), (1, 1, 0), (1, 1, 1), (0, 0, 1)]
N_CHIP = 4


class _PairJob:
    def __init__(self, grads, kinds, shard_shapes):
        self.ins = list(grads)
        self.kinds, self.shard_shapes, self.nw = list(kinds), list(shard_shapes), len(grads)
        self.out_shape = [jax.ShapeDtypeStruct((N_CHIP,) + tuple(s), g.dtype) for g, s in zip(grads, shard_shapes)]
        n = N_CHIP * self.nw
        self.sems = [pltpu.SemaphoreType.DMA((n,)), pltpu.SemaphoreType.DMA((n,))]
        self.phases = [(0.0, self._start), (1.0, self._finish)]

    def _copies(self, src, theirs, send_sems, recv_sems):
        x, y, c = _my_place()
        remote = []
        for w in range(self.nw):
            r, cc = self.shard_shapes[w]
            for q in range(N_CHIP):
                s = N_CHIP * w + q
                remote.append(pltpu.make_async_remote_copy(
                    src_ref=_block_of(src[w], self.kinds[w], 2 * q + (1 - c), r, cc), dst_ref=theirs[w].at[q],
                    send_sem=send_sems.at[s], recv_sem=recv_sems.at[s], device_id=(x, y, 1 - c), device_id_type=MESH))
        return remote

    def _start(self, *refs):
        for cp in self._copies(*refs):
            cp.start()

    def _finish(self, *refs):
        for cp in self._copies(*refs):
            cp.wait()


CHIP_RELS = [(1, 0), (0, 1), (1, 1)]


class _ChipJob:
    def __init__(self, chip_sums, shard_shapes, rels=CHIP_RELS, by_source=None, piece=(0, 1), base=None):
        nw = len(chip_sums)
        self.ins = list(chip_sums) + ([by_source] if by_source is not None else []) + (list(base) if base is not None else [])
        self.nw, self.has_src, self.rels, self.piece = nw, by_source is not None, list(rels), piece
        self.rows = [s[0] for s in shard_shapes]
        self.alias = [(nw + int(self.has_src) + w, w) for w in range(nw)] if base is not None else []
        self.out_shape = [jax.ShapeDtypeStruct((len(rels),) + tuple(s), a.dtype) for a, s in zip(chip_sums, shard_shapes)]
        if self.has_src:
            self.out_shape.append(jax.ShapeDtypeStruct((N_DEV,) + by_source.shape, by_source.dtype))
        n = len(rels) * self.nw + len(REL_ALL) * int(self.has_src)
        self.sems = [pltpu.SemaphoreType.DMA((n,)), pltpu.SemaphoreType.DMA((n,)), pltpu.SemaphoreType.DMA((1,))]
        self.phases = [(0.0, self._start), (1.0, self._finish)]

    def _copies(self, src, land, send_sems, recv_sems, local_sems):
        x, y, c = _my_place()
        nw = self.nw

        def remote_copy(s, src_ref, dst_ref, peer):
            return pltpu.make_async_remote_copy(src_ref=src_ref, dst_ref=dst_ref, send_sem=send_sems.at[s],
                                                recv_sem=recv_sems.at[s], device_id=peer, device_id_type=MESH)

        def rows(w):
            n = self.rows[w] // self.piece[1]
            return pl.ds(self.piece[0] * n, n)

        local, remote = [], []
        for k, (fx, fy) in enumerate(self.rels):
            px, py = x ^ fx, y ^ fy
            remote += [remote_copy(nw * k + w, src[w].at[2 * px + py, rows(w), :], land[w].at[k, rows(w), :], (px, py, c))
                       for w in range(nw)]
        if self.has_src:
            my_idx = _flat(x, y, c)
            local.append(pltpu.make_async_copy(src[nw], land[nw].at[my_idx], local_sems.at[0]))
            remote += [remote_copy(nw * len(self.rels) + k, src[nw], land[nw].at[my_idx], (x ^ fx, y ^ fy, c ^ fc))
                       for k, (fx, fy, fc) in enumerate(REL_ALL)]
        return local, remote

    def _start(self, *refs):
        local, remote = self._copies(*refs)
        for cp in local + remote:
            cp.start()

    def _finish(self, *refs):
        local, remote = self._copies(*refs)
        for cp in remote + local:
            cp.wait()


def _row_tile(R, C, elems=ADAMW_TILE_ELEMS):
    tr = 16
    while tr * 2 * C <= elems and R % (tr * 2) == 0:
        tr *= 2
    return tr if R % tr == 0 else R


BF16_ROWS = 16


def _tile_2d(R, C, elems):
    if R % BF16_ROWS == 0:
        tr = _row_tile(R, C, elems)
        return tr, C, R // tr
    tc = C
    while tc % (2 * LANE) == 0 and R * tc > elems:
        tc //= 2
    return R, tc, C // tc


def _tile_index(tr, tc, C):
    return (lambda i: (i, 0)) if tc == C else (lambda i: (0, i))


def _part_spec(kind, tr, tc, R, C, n_lead):
    at = _tile_index(tr, tc, C)

    def dev(ids, pref):
        return 2 * ids[0] + pref[0] if n_lead == 2 else pref[0]

    def index(*args):
        ids, pref = args[:n_lead], args[n_lead]
        ri, ci = at(ids[-1])
        if kind == "blk":
            return dev(ids, pref), ri, ci
        if kind == "rows":
            return dev(ids, pref) * (R // tr) + ri, ci
        return ri, dev(ids, pref) * (C // tc) + ci

    return pl.BlockSpec((None, tr, tc) if kind == "blk" else (tr, tc), index)


def _pair_sum(name, grad, kind, shard_shape, theirs, c):
    R, C = shard_shape
    tr, tc, tiles = _tile_2d(R, C, PAIR_TILE_ELEMS)
    at = _tile_index(tr, tc, C)

    def body(c_ref, g_ref, t_ref, o_ref):
        o_ref[...] = (g_ref[...].astype(F32) + t_ref[...].astype(F32)).astype(o_ref.dtype)

    slot = pl.BlockSpec((None, tr, tc), lambda q, i, c_ref: (q, *at(i)))
    return pl.pallas_call(
        body, name=name,
        grid_spec=pltpu.PrefetchScalarGridSpec(
            num_scalar_prefetch=1, grid=(N_CHIP, tiles),
            in_specs=[_part_spec(kind, tr, tc, R, C, 2), slot], out_specs=slot),
        out_shape=jax.ShapeDtypeStruct(theirs.shape, theirs.dtype), compiler_params=_params(2, VMEM_SMALL))(c, grad, theirs)


class _Carried:
    def __init__(self, jobs, grid):
        self.jobs = list(jobs)
        self.steps = int(np.prod(grid))
        self.grid = tuple(grid)
        self.operands = [a for j in self.jobs for a in j.ins]
        self.out_shape = [o for j in self.jobs for o in j.out_shape]
        self.sems = [s for j in self.jobs for s in j.sems]
        self.in_specs = [ANY] * len(self.operands)
        self.out_specs = [ANY] * len(self.out_shape)

    def aliases(self, n_in, n_out):
        out, i0, o0 = {}, n_in, n_out
        for job in self.jobs:
            for ji, jo in getattr(job, "alias", []):
                out[i0 + ji] = o0 + jo
            i0, o0 = i0 + len(job.ins), o0 + len(job.out_shape)
        return out

    def _step(self):
        lin = pl.program_id(0)
        for a in range(1, len(self.grid)):
            lin = lin * self.grid[a] + pl.program_id(a)
        return lin

    def _run(self, in_refs, out_refs, sem_refs, last):
        lin = self._step()
        for job in self.jobs:
            ins, in_refs = in_refs[:len(job.ins)], in_refs[len(job.ins):]
            outs, out_refs = out_refs[:len(job.out_shape)], out_refs[len(job.out_shape):]
            sems, sem_refs = sem_refs[:len(job.sems)], sem_refs[len(job.sems):]
            for frac, fn in job.phases:
                if (frac >= 1.0) != last:
                    continue
                at = min(self.steps - 1, int(frac * (self.steps - 1)))
                pl.when(lin == at)(functools.partial(fn, ins, outs, *sems))

    def before(self, in_refs, out_refs, sem_refs):
        self._run(in_refs, out_refs, sem_refs, last=False)

    def after(self, in_refs, out_refs, sem_refs):
        self._run(in_refs, out_refs, sem_refs, last=True)


def _split_refs(refs, n_in, n_job_in, n_out, n_job_out, n_scratch):
    cuts = np.cumsum([0, n_in, n_job_in, n_out, n_job_out, n_scratch])
    return [refs[cuts[i]:cuts[i + 1]] for i in range(5)] + [refs[cuts[5]:]]


def _matmul(name, a, b, *, ta=False, tb=False, out_dtypes=(F32,), epilogue=None, extras=(), tm=1024, tn=1024, tk=2048, jobs=(),
            n=None):
    if ta:
        K, M = a.shape
    else:
        M, K = a.shape
    N = n if n is not None else (b.shape[0] if tb else b.shape[1])
    if ta:
        tk = max(tk, DW_TK)
    tm, tn, tk = _tile(M, tm), _tile(N, tn), _tile(K, tk)
    nk = K // tk
    grid = (M // tm, N // tn, nk)
    dims = TN if ta else (NT if tb else NN)
    n_ex, n_out = len(extras), len(out_dtypes)
    carried = _Carried(jobs, grid)

    def body(*refs):
        ins, job_ins, out_refs, job_outs, scratch, job_sems = _split_refs(
            refs, 2 + n_ex, len(carried.operands), n_out, len(carried.out_shape), int(nk > 1))
        a_ref, b_ref, ex_refs = ins[0], ins[1], ins[2:]
        k = pl.program_id(2)
        carried.before(job_ins, job_outs, job_sems)

        def finish(acc):
            res = (acc,) if epilogue is None else epilogue(acc, *[e[...] for e in ex_refs])
            for o_ref, r in zip(out_refs, res):
                o_ref[...] = r.astype(o_ref.dtype)

        if nk == 1:
            finish(_dot(a_ref[...], b_ref[...], dims))
        else:
            acc_ref = scratch[0]

            @pl.when(k == 0)
            def _():
                acc_ref[...] = _dot(a_ref[...], b_ref[...], dims)

            @pl.when(k > 0)
            def _():
                acc_ref[...] += _dot(a_ref[...], b_ref[...], dims)

            pl.when(k == nk - 1)(lambda: finish(acc_ref[...]))

        carried.after(job_ins, job_outs, job_sems)

    a_spec = pl.BlockSpec((tk, tm), lambda i, j, k: (k, i)) if ta else pl.BlockSpec((tm, tk), lambda i, j, k: (i, k))
    b_spec = pl.BlockSpec((tn, tk), lambda i, j, k: (j, k)) if tb else pl.BlockSpec((tk, tn), lambda i, j, k: (k, j))
    tile_spec = pl.BlockSpec((tm, tn), lambda i, j, k: (i, j))
    outs = pl.pallas_call(
        body,
        name=name,
        grid=grid,
        in_specs=[a_spec, b_spec] + [tile_spec] * n_ex + carried.in_specs,
        out_specs=[tile_spec] * n_out + carried.out_specs,
        out_shape=[jax.ShapeDtypeStruct((M, N), d) for d in out_dtypes] + carried.out_shape,
        scratch_shapes=[pltpu.VMEM((tm, tn), F32)] * int(nk > 1) + carried.sems,
        input_output_aliases=carried.aliases(2 + n_ex, n_out),
        compiler_params=_params(3),
    )(a, b, *extras, *carried.operands)
    return outs[0] if len(outs) == 1 else outs


def _rowwise(name, fn, row_in, const_in, row_out, acc_out, tm, jobs=()):
    T = row_in[0][0].shape[0]
    n_r, n_c, n_o, n_a = len(row_in), len(const_in), len(row_out), len(acc_out)
    carried = _Carried(jobs, (T // tm,))

    def body(*refs):
        in_refs, job_ins, own_outs, job_outs, _, job_sems = _split_refs(
            refs, n_r + n_c, len(carried.operands), n_o + n_a, len(carried.out_shape), 0)
        out_refs, acc_refs = own_outs[:n_o], own_outs[n_o:]
        carried.before(job_ins, job_outs, job_sems)
        res = fn(*[r[...] for r in in_refs])
        for o_ref, r in zip(out_refs, res[:n_o]):
            o_ref[...] = r.astype(o_ref.dtype)

        @pl.when(pl.program_id(0) == 0)
        def _():
            for a_ref in acc_refs:
                a_ref[...] = jnp.zeros_like(a_ref)

        for a_ref, r in zip(acc_refs, res[n_o:]):
            a_ref[...] += jnp.sum(r.reshape(tm // SUBLANE, SUBLANE, r.shape[-1]), axis=0)
        carried.after(job_ins, job_outs, job_sems)

    in_specs = [pl.BlockSpec((tm, w), functools.partial(lambda i, c: (i, c), c=cb)) for (_, w, cb) in row_in]
    in_specs += [pl.BlockSpec(c.shape, functools.partial(lambda i, nd: (0,) * nd, nd=c.ndim)) for c in const_in]
    out_specs = [pl.BlockSpec((tm, c), lambda i: (i, 0)) for (c, _) in row_out]
    out_specs += [pl.BlockSpec((SUBLANE, c), lambda i: (0, 0)) for c in acc_out]
    out_shape = [jax.ShapeDtypeStruct((T, c), d) for (c, d) in row_out]
    out_shape += [jax.ShapeDtypeStruct((SUBLANE, c), F32) for c in acc_out]
    return pl.pallas_call(
        body, name=name, grid=(T // tm,), in_specs=in_specs + carried.in_specs, out_specs=out_specs + carried.out_specs,
        out_shape=out_shape + carried.out_shape, scratch_shapes=carried.sems,
        input_output_aliases=carried.aliases(n_r + n_c, n_o + n_a), compiler_params=_params(1, VMEM_SMALL),
    )(*[r[0] for r in row_in], *const_in, *carried.operands)


def _ln_stats(z):
    mu = jnp.mean(z, axis=-1, keepdims=True)
    var = jnp.mean(jnp.square(z - mu), axis=-1, keepdims=True)
    rstd = lax.rsqrt(var + EPS)
    return (z - mu) * rstd, rstd


def _ln_bwd(dy, xhat, rstd, g):
    dxh = dy * g
    return rstd * (dxh - jnp.mean(dxh, axis=-1, keepdims=True) - xhat * jnp.mean(dxh * xhat, axis=-1, keepdims=True))


def _rms(x, g):
    return x * lax.rsqrt(jnp.mean(jnp.square(x), axis=-1, keepdims=True) + EPS) * g


def _rope128(x, cos2, sin2):
    return x * cos2 + pltpu.roll(x, RET_DK // 2, 1) * sin2


def _unrope128(dy, cos2, sin2):
    return dy * cos2 + pltpu.roll(dy * sin2, RET_DK // 2, 1)


def _rope64(x, cosp, sina, sinb):
    h = MLA_ROPE // 2
    return x * cosp + pltpu.roll(x, h, 1) * sina + pltpu.roll(x, LANE - h, 1) * sinb


def _unrope64(dy, cosp, sina, sinb):
    h = MLA_ROPE // 2
    return dy * cosp + pltpu.roll(dy * sina, LANE - h, 1) + pltpu.roll(dy * sinb, h, 1)


RET_BLOCK = 256
RET_HEADS_PER_STEP = 4


def _ret_tables(H, B):
    log_g = np.log1p(-np.exp2(-5.0 - np.arange(H, dtype=np.float32))).astype(np.float32)
    idx = np.arange(B, dtype=np.float32)
    dist = np.abs(idx[:, None] - idx[None, :])
    visible = (np.arange(B)[None, :] // CHUNK) <= (np.arange(B)[:, None] // CHUNK)
    dmat = np.exp(log_g[:, None, None] * dist).astype(np.float32) * visible[None].astype(np.float32)
    qd = np.exp(log_g[:, None] * (idx + 1.0)).astype(np.float32)
    kd = np.exp(log_g[:, None] * (B - 1 - idx)).astype(np.float32)
    sd = np.exp(log_g * B).astype(np.float32)
    bc = lambda v: np.ascontiguousarray(np.broadcast_to(v[:, :, None], (H, v.shape[1], LANE)))
    sdb = np.ascontiguousarray(np.broadcast_to(sd[:, None, None], (H, SUBLANE, LANE)))
    return jnp.asarray(dmat), jnp.asarray(bc(qd)), jnp.asarray(bc(kd)), jnp.asarray(sdb)


def _ret_specs(H, B, reverse, nb):
    blk = (lambda b: nb - 1 - b) if reverse else (lambda b: b)
    G = min(RET_HEADS_PER_STEP, H)
    W = G * LANE
    col = lambda off: pl.BlockSpec((B, W), functools.partial(lambda h, b, off: (blk(b), off // G + h), off=off))
    row = pl.BlockSpec((B, LANE), lambda h, b: (blk(b), 0))
    per_head = lambda r: pl.BlockSpec((G, r, LANE), lambda h, b: (h, 0, 0))
    dmat = pl.BlockSpec((G, B, B), lambda h, b: (h, 0, 0))
    return G, col, row, per_head, dmat


def _ret_call(name, body, n_in, n_out, operands, in_specs, out_specs, out_shape, grid, G, jobs, own_alias=None):
    carried = _Carried(jobs, grid)

    def wrapped(*refs):
        ins, job_ins, outs, job_outs, (state,), job_sems = _split_refs(
            refs, n_in, len(carried.operands), n_out, len(carried.out_shape), 1)
        carried.before(job_ins, job_outs, job_sems)

        @pl.when(pl.program_id(1) == 0)
        def _():
            state[...] = jnp.zeros_like(state)

        body(*ins, *outs, state)
        carried.after(job_ins, job_outs, job_sems)

    return pl.pallas_call(
        wrapped, name=name, grid=grid,
        in_specs=in_specs + carried.in_specs, out_specs=out_specs + carried.out_specs,
        out_shape=out_shape + carried.out_shape,
        scratch_shapes=[pltpu.VMEM((G, RET_DK, RET_DV), F32)] + carried.sems,
        input_output_aliases={**(own_alias or {}), **carried.aliases(n_in, n_out)},
        compiler_params=_params(2, VMEM_SMALL),
    )(*operands, *carried.operands)


def _retention_fwd(p1, cos2, sin2, gn_g, mix_in, H, jobs=()):
    T = p1.shape[0]
    B = min(RET_BLOCK, T)
    nb = T // B
    scale = RET_DK ** -0.5
    dmat, qd, kd, sd = _ret_tables(H, B)
    G, col, row, per_head, dspec = _ret_specs(H, B, False, nb)

    def body(q_ref, k_ref, v_ref, g_ref, cos_ref, sin_ref, d_ref, qd_ref, kd_ref, sd_ref, gn_ref, _, pre_ref, out_ref, s_ref):
        cos, sin = cos_ref[...], sin_ref[...]
        for g in range(G):
            hl = pl.ds(g * LANE, LANE)
            q = _rope128(q_ref[:, hl], cos, sin)
            k = _rope128(k_ref[:, hl], cos, sin) * scale
            v = v_ref[:, hl]
            s = s_ref[g]
            a = _dot(q, k, NT) * d_ref[g]
            ret = _dot(a, v) + _dot(q * qd_ref[g], s)
            s_ref[g] = sd_ref[g, 0:1, :] * s + _dot(k * kd_ref[g], v, TN)
            pre_ref[:, hl] = ret
            nrm, _ = _ln_stats(ret)
            out_ref[:, hl] = (jax.nn.silu(g_ref[:, hl]) * (nrm * gn_ref[:, hl])).astype(out_ref.dtype)

    out_spec = pl.BlockSpec((B, G * LANE), lambda h, b: (b, h))
    return _ret_call(
        "retention_fwd", body, 12, 2, (p1, p1, p1, p1, cos2, sin2, dmat, qd, kd, sd, gn_g, mix_in),
        [col(0), col(H), col(2 * H), col(3 * H), row, row, dspec, per_head(B), per_head(B), per_head(SUBLANE),
         pl.BlockSpec((1, G * LANE), lambda h, b: (0, h)), ANY],
        [out_spec, out_spec],
        [jax.ShapeDtypeStruct((T, H * RET_DV), F32), jax.ShapeDtypeStruct(mix_in.shape, mix_in.dtype)],
        (H // G, nb), G, jobs, own_alias={11: 1})


def _retention_bwd(p1, dpre, cos2, sin2, H, dq_jobs=(), dkv_jobs=()):
    T = p1.shape[0]
    B = min(RET_BLOCK, T)
    nb = T // B
    scale = RET_DK ** -0.5
    dmat, qd, kd, sd = _ret_tables(H, B)

    def dq_body(k_ref, v_ref, do_ref, cos_ref, sin_ref, d_ref, qd_ref, kd_ref, sd_ref, dq_ref, s_ref):
        cos, sin = cos_ref[...], sin_ref[...]
        for g in range(G):
            hl = pl.ds(g * LANE, LANE)
            k = _rope128(k_ref[:, hl], cos, sin) * scale
            v, do, s = v_ref[:, hl], do_ref[:, hl], s_ref[g]
            da = _dot(do, v, NT) * d_ref[g]
            dq = _dot(da, k) + _dot(do, s, NT) * qd_ref[g]
            dq_ref[:, hl] = _unrope128(dq, cos, sin).astype(dq_ref.dtype)
            s_ref[g] = sd_ref[g, 0:1, :] * s + _dot(k * kd_ref[g], v, TN)

    G, col, row, per_head, dspec = _ret_specs(H, B, False, nb)
    dq = _ret_call(
        "retention_bwd_dq", dq_body, 9, 1, (p1, p1, dpre, cos2, sin2, dmat, qd, kd, sd),
        [col(H), col(2 * H), col(0), row, row, dspec, per_head(B), per_head(B), per_head(SUBLANE)],
        [pl.BlockSpec((B, G * LANE), lambda h, b: (b, h))], [jax.ShapeDtypeStruct((T, H * RET_DK), BF16)],
        (H // G, nb), G, dq_jobs)

    def dkv_body(q_ref, k_ref, v_ref, do_ref, cos_ref, sin_ref, d_ref, qd_ref, kd_ref, sd_ref, dk_ref, dv_ref, g_ref):
        cos, sin = cos_ref[...], sin_ref[...]
        for g in range(G):
            hl = pl.ds(g * LANE, LANE)
            q = _rope128(q_ref[:, hl], cos, sin)
            k = _rope128(k_ref[:, hl], cos, sin) * scale
            v, do, st = v_ref[:, hl], do_ref[:, hl], g_ref[g]
            a = _dot(q, k, NT) * d_ref[g]
            da = _dot(do, v, NT) * d_ref[g]
            dv = _dot(a, do, TN) + _dot(k * kd_ref[g], st)
            dk = _dot(da, q, TN) + _dot(v, st, NT) * kd_ref[g]
            dv_ref[:, hl] = dv.astype(dv_ref.dtype)
            dk_ref[:, hl] = _unrope128(dk * scale, cos, sin).astype(dk_ref.dtype)
            g_ref[g] = sd_ref[g, 0:1, :] * st + _dot(q * qd_ref[g], do, TN)

    G, col, row, per_head, dspec = _ret_specs(H, B, True, nb)
    out_spec = pl.BlockSpec((B, G * LANE), lambda h, b: (nb - 1 - b, h))
    dkv = _ret_call(
        "retention_bwd_dkv", dkv_body, 10, 2, (p1, p1, p1, dpre, cos2, sin2, dmat, qd, kd, sd),
        [col(0), col(H), col(2 * H), col(0), row, row, dspec, per_head(B), per_head(B), per_head(SUBLANE)],
        [out_spec, out_spec],
        [jax.ShapeDtypeStruct((T, H * RET_DK), BF16), jax.ShapeDtypeStruct((T, H * RET_DV), BF16)],
        (H // G, nb), G, dkv_jobs)
    return dq, dkv


ATT_BLOCK = 512


ATT_ROW_CHUNKS = 2
ATT_BWD_ROW_CHUNKS = 1
ATT_HEADS_PER_STEP = 8
ATT_BWD_HEADS_PER_STEP = 2
LOG2E = 1.4426950408889634


def _chunk_mask(rows, row0, cols):
    qi = (lax.broadcasted_iota(jnp.int32, (rows, cols), 0) + row0) // CHUNK
    kj = lax.broadcasted_iota(jnp.int32, (rows, cols), 1) // CHUNK
    return kj <= qi


def _attention_fwd(qh, kh, vx, H, jobs=()):
    T = qh.shape[0]
    t = min(ATT_BLOCK, T)
    n = T // t
    r = t // ATT_ROW_CHUNKS
    G = min(ATT_HEADS_PER_STEP, H)
    P = MLA_HEAD_PAD
    scale = (MLA_NOPE + MLA_ROPE) ** -0.5
    c2 = scale * LOG2E
    carried = _Carried(jobs, (H // G, n, n))

    def body(*refs):
        (q_ref, k_ref, v_ref), job_ins, (o_ref, ob_ref, lse_ref), job_outs, (m_ref, acc_ref), job_sems = _split_refs(
            refs, 3, len(carried.operands), 3, len(carried.out_shape), 2)
        i, j = pl.program_id(1), pl.program_id(2)
        carried.before(job_ins, job_outs, job_sems)

        @pl.when(j == 0)
        def _():
            m_ref[...] = jnp.full_like(m_ref, -jnp.inf)
            acc_ref[...] = jnp.zeros_like(acc_ref)

        def update(masked):
            for g in range(G):
                hc = pl.ds(g * P, P)
                for ch in range(ATT_ROW_CHUNKS):
                    rs = pl.ds(ch * r, r)
                    kc = (ch + 1) * r if masked else t
                    ks = pl.ds(0, kc)
                    s = _dot(q_ref[rs, hc], k_ref[ks, hc], NT)
                    if masked:
                        s = jnp.where(_chunk_mask(r, ch * r, kc), s, -jnp.inf)
                    m_prev = m_ref[rs, pl.ds(g * LANE, LANE)]
                    m_new = jnp.maximum(m_prev, jnp.max(s, axis=-1, keepdims=True))
                    alpha = jnp.exp2((m_prev - m_new) * c2)
                    p = jnp.exp2((s - jnp.tile(m_new, (1, kc // LANE))) * c2)
                    acc_ref[rs, hc] = jnp.tile(alpha, (1, 2)) * acc_ref[rs, hc] + _dot(p, v_ref[ks, hc])
                    m_ref[rs, pl.ds(g * LANE, LANE)] = m_new

        pl.when(j < i)(functools.partial(update, False))
        pl.when(j == i)(functools.partial(update, True))

        @pl.when(j == n - 1)
        def _():
            for g in range(G):
                acc = acc_ref[:, pl.ds(g * P, P)]
                l = acc[:, MLA_DV:]
                o = acc[:, :MLA_DV] / l
                o_ref[:, pl.ds(g * MLA_DV, MLA_DV)] = o
                ob_ref[:, pl.ds(g * MLA_DV, MLA_DV)] = o.astype(ob_ref.dtype)
                lse_ref[g] = m_ref[:, pl.ds(g * LANE, LANE)] * scale + jnp.log(l)

        carried.after(job_ins, job_outs, job_sems)

    kclamp = lambda i, j: jnp.minimum(j, i)
    o_spec = pl.BlockSpec((t, G * MLA_DV), lambda h, i, j: (i, h))
    mix_spec = pl.BlockSpec((t, G * MLA_DV), lambda h, i, j: (i, H // G + h))
    return pl.pallas_call(
        body, name="attention_fwd", grid=(H // G, n, n),
        in_specs=[pl.BlockSpec((t, G * P), lambda h, i, j: (i, h)),
                  pl.BlockSpec((t, G * P), lambda h, i, j: (kclamp(i, j), h)),
                  pl.BlockSpec((t, G * P), lambda h, i, j: (kclamp(i, j), h))] + carried.in_specs,
        out_specs=[o_spec, mix_spec, pl.BlockSpec((G, t, LANE), lambda h, i, j: (h, i, 0))] + carried.out_specs,
        out_shape=[jax.ShapeDtypeStruct((T, H * MLA_DV), F32), jax.ShapeDtypeStruct((T, 2 * H * MLA_DV), BF16),
                   jax.ShapeDtypeStruct((H, T, LANE), F32)] + carried.out_shape,
        scratch_shapes=[pltpu.VMEM((t, G * LANE), F32), pltpu.VMEM((t, G * P), F32)] + carried.sems,
        input_output_aliases=carried.aliases(3, 3),
        compiler_params=_params(3),
    )(qh, kh, vx, *carried.operands)


def _attention_bwd(qh, kh, vx, o, lse, dmix, H, jobs=()):
    T = qh.shape[0]
    t = min(ATT_BLOCK, T)
    n = T // t
    r = t // ATT_BWD_ROW_CHUNKS
    G = min(ATT_BWD_HEADS_PER_STEP, H)
    P = MLA_HEAD_PAD
    scale = (MLA_NOPE + MLA_ROPE) ** -0.5
    c2 = scale * LOG2E
    carried = _Carried(jobs, (H // G, n, n))

    def body(*refs):
        ((q_ref, k_ref, v_ref, do_ref, o_ref, lse_ref), job_ins, (dq_ref, dkn_ref, dkr_ref, dv_ref), job_outs,
         (dk_acc, dv_acc), job_sems) = _split_refs(refs, 6, len(carried.operands), 4, len(carried.out_shape), 2)
        j, i = pl.program_id(1), pl.program_id(2)
        carried.before(job_ins, job_outs, job_sems)

        @pl.when(jnp.logical_and(j == 0, i == 0))
        def _():
            dq_ref[...] = jnp.zeros_like(dq_ref)

        @pl.when(i == j)
        def _():
            dk_acc[...] = jnp.zeros_like(dk_acc)
            dv_acc[...] = jnp.zeros_like(dv_acc)

        def update(masked):
            for g in range(G):
                hc, hv = pl.ds(g * P, P), pl.ds(g * MLA_DV, MLA_DV)
                for ch in range(ATT_BWD_ROW_CHUNKS):
                    rs = pl.ds(ch * r, r)
                    kc = (ch + 1) * r if masked else t
                    ks = pl.ds(0, kc)
                    k, v = k_ref[ks, hc], v_ref[ks, pl.ds(g * P, MLA_DV)]
                    q, do = q_ref[rs, hc], do_ref[rs, hv]
                    s = _dot(q, k, NT)
                    if masked:
                        s = jnp.where(_chunk_mask(r, ch * r, kc), s, -jnp.inf)
                    p = jnp.exp2(s * c2 - jnp.tile(lse_ref[g, rs, :] * LOG2E, (1, kc // LANE)))
                    dv_acc[ks, hv] += _dot(p, do, TN)
                    dp = _dot(do, v, NT)
                    delta = jnp.sum(do * o_ref[rs, hv], axis=-1, keepdims=True)
                    ds = p * (dp - delta)
                    rows = pl.ds(pl.multiple_of(i * t, t) + ch * r, r)
                    dq_ref[rows, hc] += _dot(ds, k)
                    dk_acc[ks, hc] += _dot(ds, q, TN)

        pl.when(i > j)(functools.partial(update, False))
        pl.when(i == j)(functools.partial(update, True))

        @pl.when(i == n - 1)
        def _():
            for g in range(G):
                dk = dk_acc[:, pl.ds(g * P, P)] * scale
                dkn_ref[:, pl.ds(g * LANE, LANE)] = dk[:, :MLA_NOPE].astype(dkn_ref.dtype)
                dkr_ref[:, pl.ds(g * LANE, LANE)] = dk[:, MLA_NOPE:]
            dv_ref[...] = dv_acc[...].astype(dv_ref.dtype)

        carried.after(job_ins, job_outs, job_sems)

    qclamp = lambda j, i: jnp.maximum(i, j)
    kv_out = pl.BlockSpec((t, G * LANE), lambda h, j, i: (j, h))
    q_in = pl.BlockSpec((t, G * MLA_DV), lambda h, j, i: (qclamp(j, i), h))
    return pl.pallas_call(
        body, name="attention_bwd", grid=(H // G, n, n),
        in_specs=[pl.BlockSpec((t, G * P), lambda h, j, i: (qclamp(j, i), h)),
                  pl.BlockSpec((t, G * P), lambda h, j, i: (j, h)),
                  pl.BlockSpec((t, G * P), lambda h, j, i: (j, h)),
                  pl.BlockSpec((t, G * MLA_DV), lambda h, j, i: (qclamp(j, i), H // G + h)),
                  q_in,
                  pl.BlockSpec((G, t, LANE), lambda h, j, i: (h, qclamp(j, i), 0))] + carried.in_specs,
        out_specs=[pl.BlockSpec((T, G * P), lambda h, j, i: (0, h)), kv_out, kv_out, kv_out] + carried.out_specs,
        out_shape=[jax.ShapeDtypeStruct((T, H * P), F32), jax.ShapeDtypeStruct((T, H * LANE), BF16),
                   jax.ShapeDtypeStruct((T, H * LANE), F32), jax.ShapeDtypeStruct((T, H * LANE), BF16)] + carried.out_shape,
        scratch_shapes=[pltpu.VMEM((t, G * P), F32), pltpu.VMEM((t, G * MLA_DV), F32)] + carried.sems,
        input_output_aliases=carried.aliases(6, 4),
        compiler_params=_params(3),
    )(qh, kh, vx, dmix, o, lse, *carried.operands)


def _adamw_math(w, g, m, v):
    m = ADAM_B1 * m + (1.0 - ADAM_B1) * g
    v = ADAM_B2 * v + (1.0 - ADAM_B2) * jnp.square(g)
    m_hat = m / (1.0 - ADAM_B1 ** ADAM_STEP)
    v_hat = v / (1.0 - ADAM_B2 ** ADAM_STEP)
    delta = -ADAM_LR * (m_hat / (jnp.sqrt(v_hat) + ADAM_EPS) + ADAM_WD * w)
    return delta, m, v


def _adamw(name, chip_sums, my_chip, landed, w, m, v, jobs=()):
    R, C = w.shape
    tr, tc, tiles = _tile_2d(R, C, ADAMW_TILE_ELEMS)
    at = _tile_index(tr, tc, C)
    nl = len(landed)
    assert sum(l.shape[0] for l in landed) == N_CHIP - 1
    carried = _Carried(jobs, (tiles,))

    def body(q_ref, *refs):
        (own_ref, *rest), job_ins, (g_out, d_out, m_out, v_out), job_outs, _, job_sems = _split_refs(
            refs, nl + 4, len(carried.operands), 4, len(carried.out_shape), 0)
        l_refs, (w_ref, m_ref, v_ref) = rest[:nl], rest[nl:]
        carried.before(job_ins, job_outs, job_sems)
        g = own_ref[...].astype(F32)
        for l_ref in l_refs:
            for s in range(l_ref.shape[0]):
                g = g + l_ref[s].astype(F32)
        delta, mn, vn = _adamw_math(w_ref[...], g, m_ref[...], v_ref[...])
        g_out[...] = g
        d_out[...] = delta
        m_out[...] = mn
        v_out[...] = vn
        carried.after(job_ins, job_outs, job_sems)

    spec = pl.BlockSpec((tr, tc), lambda i, q_ref: at(i))
    own_spec = pl.BlockSpec((None, tr, tc), lambda i, q_ref: (q_ref[0], *at(i)))
    l_specs = [pl.BlockSpec((l.shape[0], tr, tc), lambda i, q_ref: (0, *at(i))) for l in landed]
    return pl.pallas_call(
        body, name=name,
        grid_spec=pltpu.PrefetchScalarGridSpec(
            num_scalar_prefetch=1, grid=(tiles,),
            in_specs=[own_spec] + l_specs + [spec, spec, spec] + carried.in_specs,
            out_specs=[spec] * 4 + carried.out_specs, scratch_shapes=carried.sems),
        out_shape=[jax.ShapeDtypeStruct((R, C), F32)] * 4 + carried.out_shape,
        input_output_aliases={1 + i: o for i, o in carried.aliases(nl + 4, 4).items()},
        compiler_params=_params(1, VMEM_SMALL),
    )(my_chip, chip_sums, *landed, w, m, v, *carried.operands)


def _adamw_small(landed, w, m, v):
    P = w.shape[1]

    def body(l_ref, w_ref, m_ref, v_ref, g_out, d_out, m_out, v_out, loss_out):
        acc = l_ref[0]
        for s in range(1, N_DEV):
            acc = acc + l_ref[s]
        total = jnp.sum(acc, axis=0, keepdims=True)
        g = total[:, :P]
        delta, mn, vn = _adamw_math(w_ref[...], g, m_ref[...], v_ref[...])
        g_out[...] = g
        d_out[...] = delta
        m_out[...] = mn
        v_out[...] = vn
        loss_out[...] = total[:, P:]

    return pl.pallas_call(
        body, name="adamw_replicated",
        out_shape=[jax.ShapeDtypeStruct((1, P), F32)] * 4 + [jax.ShapeDtypeStruct((1, LANE), F32)],
    )(landed, w, m, v)


def _rope_rows():
    inv_r = ROPE_BASE ** (-jnp.arange(0, RET_DK, 2, dtype=F32) / RET_DK)
    inv_m = ROPE_BASE ** (-jnp.arange(0, MLA_ROPE, 2, dtype=F32) / MLA_ROPE)
    h = MLA_ROPE // 2
    one, zero = jnp.ones((h,), F32), jnp.zeros((h,), F32)
    pad = jnp.zeros((LANE - MLA_ROPE,), F32)
    rows = [jnp.concatenate([inv_r, inv_r]), jnp.concatenate([-jnp.ones((RET_DK // 2,), F32), jnp.ones((RET_DK // 2,), F32)]),
            jnp.concatenate([inv_m, inv_m, pad]), jnp.concatenate([one, one, pad]),
            jnp.concatenate([zero, one, pad]), jnp.concatenate([-one, zero, pad])]
    return jnp.stack(rows + [jnp.zeros((LANE,), F32)] * (SUBLANE - len(rows)))


def _prep_tile(xt, p, rows):
    ang2, angp = p * rows[0:1], p * rows[2:3]
    sp = jnp.sin(angp)
    return (xt, jnp.cos(ang2), jnp.sin(ang2) * rows[1:2], jnp.cos(angp) * rows[3:4], sp * rows[4:5], sp * rows[5:6])


def kernel(x, positions, w_in, q_norm_g, w_uq, kv_norm_g, w_uk, w_uv, ret_gn_g, w_out, ln1_g, ln1_b, w_up, w_down, ln2_g, ln2_b, loss_target, m_w_in, m_q_norm_g, m_w_uq, m_kv_norm_g, m_w_uk, m_w_uv, m_ret_gn_g, m_w_out, m_ln1_g, m_ln1_b, m_w_up, m_w_down, m_ln2_g, m_ln2_b, v_w_in, v_q_norm_g, v_w_uq, v_kv_norm_g, v_w_uk, v_w_uv, v_ret_gn_g, v_w_out, v_ln1_g, v_ln1_b, v_w_up, v_w_down, v_ln2_g, v_ln2_b):
    T, D = x.shape[1], x.shape[2]
    H = D // 256
    RW = H * RET_DV
    x = x[0]
    tgt = loss_target[0]
    alpha = DEEPNORM_ALPHA
    in_width = w_in.shape[2] * N_DEV
    mla_in = in_width - 4 * RW
    mla_in_pad = -(-mla_in // 512) * 512
    f_uq = MLA_NOPE + MLA_ROPE

    names = ["w_in", "w_uq", "w_uk", "w_uv", "w_out", "w_up", "w_down"]
    big = dict(zip(names, [w_in[0].T, w_uq[0], w_uk[0], w_uv[0], w_out[0], w_up[0], w_down[0]]))
    kind = dict(zip(names, ["blk", "blk", "cols", "cols", "rows", "cols", "rows"]))
    half = {n: w.astype(BF16) for n, w in big.items()}

    def gather(ns, **parts):
        return _GatherJob([half[n] for n in ns], [kind[n] for n in ns], 0.55, 0.9, **parts)

    tr = min(256, T)
    xb, cos2, sin2, cosp, sina, sinb, g_in = _rowwise(
        "prep", _prep_tile, [(x, D, 0), (positions[0].astype(F32)[:, None], 1, 0)], [_rope_rows()],
        [(D, BF16)] + [(LANE, F32)] * 5, [], tr, jobs=[gather(["w_in"])])
    win = g_in.reshape(in_width, D)
    w1, n1 = win, 4 * RW
    w2 = jnp.pad(win[4 * RW:], ((0, mla_in_pad - mla_in), (0, 0)))

    first, second, near, far = (0, 2), (1, 2), (CHIP_X, CHIP_Y), (CHIP_DIAGONAL,)
    p1, g_uq, wuk, wuv, wout = _matmul("proj_ret", xb, w1, tb=True, n=n1,
                                       jobs=[gather(["w_uq", "w_uk", "w_uv", "w_out"])])
    wuq = jnp.transpose(g_uq, (1, 0, 2)).reshape(MLA_Q_RANK, H, f_uq)
    wuq = jnp.pad(wuq, ((0, 0), (0, 0), (0, MLA_HEAD_PAD - f_uq))).reshape(MLA_Q_RANK, H * MLA_HEAD_PAD)
    wukv = jnp.concatenate([wuk, wuv], axis=1)
    p2 = _matmul("proj_mla", xb, w2, tb=True)
    c_kv, c_kr = MLA_Q_RANK, MLA_Q_RANK + MLA_KV_RANK

    def mla_prep(p, cp, sa, sb, gq, gkv):
        cqn = _rms(p[:, :c_kv], gq)
        ckvn = _rms(p[:, c_kv:c_kr], gkv)
        return cqn, ckvn, _rope64(p[:, c_kr:c_kr + LANE], cp, sa, sb)

    cqn, ckvn, krope = _rowwise(
        "mla_prep", mla_prep, [(p2, mla_in_pad, 0), (cosp, LANE, 0), (sina, LANE, 0), (sinb, LANE, 0)],
        [q_norm_g, kv_norm_g], [(MLA_Q_RANK, BF16), (MLA_KV_RANK, BF16), (LANE, BF16)], [], tr)
    qf = _matmul("mla_q", cqn, wuq)
    kv = _matmul("mla_kv", ckvn, wukv, out_dtypes=(BF16,))

    def mla_heads(q, kn, vv, kr, cp, sa, sb):
        qs, ks, vs = [], [], []
        ones = jnp.ones((q.shape[0], MLA_DV), vv.dtype)
        for h in range(H):
            o = h * MLA_HEAD_PAD
            qs += [q[:, o:o + MLA_NOPE], _rope64(q[:, o + MLA_NOPE:o + MLA_HEAD_PAD], cp, sa, sb)]
            ks += [kn[:, h * MLA_NOPE:(h + 1) * MLA_NOPE], kr]
            vs += [vv[:, h * MLA_DV:(h + 1) * MLA_DV], ones]
        return jnp.concatenate(qs, axis=1), jnp.concatenate(ks, axis=1), jnp.concatenate(vs, axis=1)

    qh, kh, vx = _rowwise(
        "mla_heads", mla_heads,
        [(qf, H * MLA_HEAD_PAD, 0), (kv, RW, 0), (kv, RW, 1), (krope, LANE, 0), (cosp, LANE, 0), (sina, LANE, 0), (sinb, LANE, 0)],
        [], [(H * MLA_HEAD_PAD, BF16)] * 3, [], tr)
    whole = (0, 1)
    att_o, mix_half, lse, wup_a = _attention_fwd(
        qh, kh, vx, H, jobs=[gather(["w_up"], pieces=[(first, near, True, 0.45), (second, near, True, 0.75)])])
    ret_pre, mixin, wup = _retention_fwd(
        p1, cos2, sin2, ret_gn_g, mix_half, H,
        jobs=[gather(["w_up"], pieces=[(first, far, False, 0.55), (second, far, False, 0.9)], base=[wup_a])])
    mix, wdown_a = _matmul("mix_out", mixin, wout,
                           jobs=[gather(["w_down"], pieces=[((0, 4), near, True, 0.5), ((1, 4), near, True, 0.9)])])

    def ln1_fwd(xt, mt, g, b):
        z = alpha * xt + mt
        xhat, _ = _ln_stats(z)
        y = xhat * g + b
        return z, y, y

    z1, x1, x1b, wdown_b = _rowwise("ln1", ln1_fwd, [(x, D, 0), (mix, D, 0)], [ln1_g, ln1_b],
                                    [(D, F32), (D, F32), (D, BF16)], [], tr,
                                    jobs=[gather(["w_down"], pieces=[((2, 4), near, True, 0.5), ((3, 4), near, True, 0.9)],
                                                 base=[wdown_a])])

    def relu2(acc):
        r = jnp.maximum(acc, 0.0)
        return r * r, r

    a2b, rb, wdown = _matmul("mlp_up", x1b, wup, out_dtypes=(BF16, BF16), epilogue=relu2,
                             jobs=[gather(["w_down"], pieces=[(whole, far, False, 0.6)], base=[wdown_b])])
    hmlp = _matmul("mlp_down", a2b, wdown)

    def ln2_loss(x1t, ht, tt, g, b):
        xhat, rstd = _ln_stats(alpha * x1t + ht)
        err = xhat * g + b - tt
        dy = err / D
        dz = _ln_bwd(dy, xhat, rstd, g)
        lrow = 0.5 * jnp.mean(jnp.square(err), axis=-1, keepdims=True)
        return dz, dz, dy * xhat, dy, jnp.broadcast_to(lrow, (lrow.shape[0], LANE))

    dz2, dz2b, dg2, db2, lpart = _rowwise("ln2_loss", ln2_loss, [(x1, D, 0), (hmlp, D, 0), (tgt, D, 0)], [ln2_g, ln2_b],
                                          [(D, F32), (D, BF16)], [D, D, LANE], tr)

    def pair(grads):
        ns = list(grads)
        return _PairJob([grads[n] for n in ns], [kind[n] for n in ns], [big[n].shape for n in ns])

    my_core = lax.axis_index("c").astype(jnp.int32).reshape(1)
    my_chip = (2 * lax.axis_index("x") + lax.axis_index("y")).astype(jnp.int32).reshape(1)
    chip_sums = {}

    def pair_sums(grads, theirs):
        for n, t in zip(grads, theirs):
            chip_sums[n] = _pair_sum("pair_sum_" + n, grads[n], kind[n], big[n].shape, t, my_core)
        return {n: chip_sums[n] for n in grads}

    def chip(sums, rels=CHIP_RELS, by_source=None):
        ns = list(sums)
        return _ChipJob([sums[n] for n in ns], [big[n].shape for n in ns], rels, by_source)

    landed = {}
    da = _matmul("mlp_down_dx", dz2b, wdown, tb=True, out_dtypes=(BF16,), extras=(rb,),
                 epilogue=lambda acc, r: (acc * (2.0 * r.astype(F32)),))
    g_down = {"w_down": _matmul("mlp_down_dw", a2b, dz2b, ta=True, out_dtypes=(BF16,))}
    g_wup, *pr = _matmul("mlp_up_dw", x1b, da, ta=True, out_dtypes=(BF16,), jobs=[pair(g_down)])
    sums_down = pair_sums(g_down, pr)
    g_up = {"w_up": g_wup}
    dx1m, l_a, *pr = _matmul("mlp_up_dx", da, wup, tb=True, jobs=[chip(sums_down, CHIP_RELS[:2]), pair(g_up)])
    sums_up = pair_sums(g_up, pr)

    def ln1_bwd(dm, dzz, z, g):
        xhat, rstd = _ln_stats(z)
        dy = dm + alpha * dzz
        dz = _ln_bwd(dy, xhat, rstd, g)
        return dz, dz, dy * xhat, dy

    dz1, dz1b, dg1, db1 = _rowwise("ln1_bwd", ln1_bwd, [(dx1m, D, 0), (dz2, D, 0), (z1, D, 0)], [ln1_g],
                                   [(D, F32), (D, BF16)], [D, D], tr)
    g_wout = _matmul("mix_out_dw", mixin, dz1b, ta=True, out_dtypes=(BF16,))
    dmix = _matmul("mix_out_dx", dz1b, wout, tb=True)

    dqh, dkn, dkr, dvv, l_b, l_c = _attention_bwd(qh, kh, vx, att_o, lse, dmix, H,
                                                  jobs=[chip(sums_down, CHIP_RELS[2:]), chip(sums_up)])
    landed["w_down"], landed["w_up"] = [l_a, l_b], [l_c]
    att_scale = (MLA_NOPE + MLA_ROPE) ** -0.5

    def mla_heads_bwd(dq, dkrh, cp, sa, sb):
        parts, acc = [], dkrh[:, :LANE]
        dq = dq * att_scale
        for h in range(H):
            o = h * MLA_HEAD_PAD
            parts += [dq[:, o:o + MLA_NOPE], _unrope64(dq[:, o + MLA_NOPE:o + MLA_HEAD_PAD], cp, sa, sb)]
            if h:
                acc = acc + dkrh[:, h * LANE:(h + 1) * LANE]
        return jnp.concatenate(parts, axis=1), _unrope64(acc, cp, sa, sb)

    dqb, dkr128 = _rowwise(
        "mla_heads_bwd", mla_heads_bwd,
        [(dqh, H * MLA_HEAD_PAD, 0), (dkr, H * LANE, 0), (cosp, LANE, 0), (sina, LANE, 0), (sinb, LANE, 0)],
        [], [(H * MLA_HEAD_PAD, BF16), (LANE, F32)], [], tr)
    dcqn = _matmul("mla_q_dx", dqb, wuq, tb=True)
    g_wuq = _matmul("mla_q_dw", cqn, dqb, ta=True, out_dtypes=(BF16,))
    dkvb = jnp.concatenate([dkn, dvv], axis=1)
    dckvn = _matmul("mla_kv_dx", dkvb, wukv, tb=True)
    g_wukv = _matmul("mla_kv_dw", ckvn, dkvb, ta=True, out_dtypes=(BF16,))

    def rms_bwd(c, dy, g):
        rstd = lax.rsqrt(jnp.mean(jnp.square(c), axis=-1, keepdims=True) + EPS)
        dyg = dy * g
        dc = rstd * (dyg - c * (rstd * rstd) * jnp.mean(dyg * c, axis=-1, keepdims=True))
        return dc, dy * c * rstd

    def mla_prep_bwd(p, dq_, dkv_, dkr_, gq, gkv):
        dcq, dgq_ = rms_bwd(p[:, :c_kv], dq_, gq)
        dckv, dgkv_ = rms_bwd(p[:, c_kv:c_kr], dkv_, gkv)
        pad = jnp.zeros((p.shape[0], mla_in_pad - c_kr - LANE), F32)
        return jnp.concatenate([dcq, dckv, dkr_, pad], axis=1), dgq_, dgkv_

    d_p2, dgq, dgkv = _rowwise(
        "mla_prep_bwd", mla_prep_bwd, [(p2, mla_in_pad, 0), (dcqn, MLA_Q_RANK, 0), (dckvn, MLA_KV_RANK, 0), (dkr128, LANE, 0)],
        [q_norm_g, kv_norm_g], [(mla_in_pad, BF16)], [MLA_Q_RANK, MLA_KV_RANK], tr)

    def gate_bwd(pre, gate, dout, gn):
        dpre, dgate, dgn = [], [], []
        for h in range(H):
            sl = slice(h * RET_DV, (h + 1) * RET_DV)
            nrm, rstd = _ln_stats(pre[:, sl])
            gate_h, do_h, gn_h = gate[:, sl], dout[:, sl], gn[:, sl]
            sg = jax.nn.sigmoid(gate_h)
            silu = gate_h * sg
            dgate.append(do_h * (nrm * gn_h) * (sg * (1.0 + gate_h * (1.0 - sg))))
            dn = do_h * silu
            dgn.append(dn * nrm)
            dpre.append(_ln_bwd(dn, nrm, rstd, gn_h))
        return jnp.concatenate(dpre, axis=1), jnp.concatenate(dgate, axis=1), jnp.concatenate(dgn, axis=1)

    dpre, drg, dgn = _rowwise("ret_gate_bwd", gate_bwd, [(ret_pre, RW, 0), (p1, RW, 3), (dmix, RW, 0)], [ret_gn_g],
                              [(RW, F32), (RW, BF16)], [RW], tr)
    g_uq = g_wuq.reshape(MLA_Q_RANK, H, MLA_HEAD_PAD)[:, :, :f_uq].reshape(MLA_Q_RANK, N_DEV, H * f_uq // N_DEV)
    mid_grads = {"w_uq": jnp.transpose(g_uq, (1, 0, 2)), "w_uk": g_wukv[:, :RW], "w_uv": g_wukv[:, RW:], "w_out": g_wout}
    (drq, *pr), (drk, drv) = _retention_bwd(p1, dpre, cos2, sin2, H, dq_jobs=[pair(mid_grads)])
    sums_mid = pair_sums(mid_grads, pr)
    d_p1 = jnp.concatenate([drq, drk, drv, drg], axis=1)

    g_w1, *l_a = _matmul("proj_ret_dw", d_p1, xb, ta=True, out_dtypes=(BF16,), jobs=[chip(sums_mid)])
    for n, a_ in zip(mid_grads, l_a):
        landed[n] = [a_]
    g_w2 = _matmul("proj_mla_dw", d_p2, xb, ta=True, out_dtypes=(BF16,))

    g_win = {"w_in": jnp.concatenate([g_w1, g_w2[:mla_in]], axis=0).reshape(N_DEV, in_width // N_DEV, D)}
    small = jnp.concatenate([dgq, dgkv, dgn, dg1, db1, dg2, db2, lpart], axis=1)
    dx_a, *pr = _matmul("proj_mla_dx", d_p2, w2, extras=(dz1,), epilogue=lambda acc, d: (acc + alpha * d,), jobs=[pair(g_win)])
    sums_in = pair_sums(g_win, pr)
    grad_x, l_a, small_landed = _matmul("proj_ret_dx", d_p1, w1, extras=(dx_a,), epilogue=lambda acc, d: (acc + d,),
                                        jobs=[chip(sums_in, CHIP_RELS[:2], by_source=small)])

    moments = dict(zip(names, zip([m_w_in, m_w_uq, m_w_uk, m_w_uv, m_w_out, m_w_up, m_w_down],
                                  [v_w_in, v_w_uq, v_w_uk, v_w_uv, v_w_out, v_w_up, v_w_down])))
    res = {}

    def adamw(n, jobs=()):
        as_held = (lambda a: a[0].T) if n == "w_in" else (lambda a: a[0])
        as_given = (lambda r: r.T[None]) if n == "w_in" else (lambda r: r[None])
        out = _adamw("adamw_" + n, chip_sums[n], my_chip, landed[n], big[n], as_held(moments[n][0]), as_held(moments[n][1]),
                     jobs=jobs)
        res[n] = [as_given(r) for r in out[:4]]
        return out[4:]

    (l_b,) = adamw("w_out", jobs=[chip(sums_in, CHIP_RELS[2:])])
    landed["w_in"] = [l_a, l_b]
    for n in names:
        if n != "w_out":
            adamw(n)

    small_names = ["q_norm_g", "kv_norm_g", "ret_gn_g", "ln1_g", "ln1_b", "ln2_g", "ln2_b"]
    small_w = [q_norm_g, kv_norm_g, ret_gn_g, ln1_g, ln1_b, ln2_g, ln2_b]
    small_m = [m_q_norm_g, m_kv_norm_g, m_ret_gn_g, m_ln1_g, m_ln1_b, m_ln2_g, m_ln2_b]
    small_v = [v_q_norm_g, v_kv_norm_g, v_ret_gn_g, v_ln1_g, v_ln1_b, v_ln2_g, v_ln2_b]
    cat = lambda arrs: jnp.concatenate(arrs, axis=1)
    *sres, loss_lanes = _adamw_small(small_landed, cat(small_w), cat(small_m), cat(small_v))
    loss = loss_lanes[0, 0]
    off = 0
    for n, w in zip(small_names, small_w):
        res[n] = [r[:, off:off + w.shape[1]] for r in sres]
        off += w.shape[1]

    order = ["w_in", "q_norm_g", "w_uq", "kv_norm_g", "w_uk", "w_uv", "ret_gn_g", "w_out", "ln1_g", "ln1_b",
             "w_up", "w_down", "ln2_g", "ln2_b"]
    outs = [loss, grad_x[None]]
    for k in range(4):
        outs += [res[n][k] for n in order]
    return tuple(outs)
```

```python
import functools

import numpy as np

import jax
import jax.numpy as jnp
from jax import lax
from jax.experimental import pallas as pl
from jax.experimental.pallas import tpu as pltpu

F32 = jnp.float32
BF16 = jnp.bfloat16

CHUNK = 64
RET_DK = 128
RET_DV = 128
MLA_NOPE = 128
MLA_ROPE = 64
MLA_DV = 128
MLA_Q_RANK = 768
MLA_KV_RANK = 512
MLA_HEAD_PAD = 256
ROPE_BASE = 10000.0
EPS = 1e-5
DEPTH = 1
DEEPNORM_ALPHA = (2.0 * DEPTH) ** 0.25
ADAM_LR = 0.001
ADAM_B1 = 0.9
ADAM_B2 = 0.999
ADAM_EPS = 1e-08
ADAM_WD = 0.01
ADAM_STEP = 10

N_DEV = 8
LANE = 128
SUBLANE = 8
VMEM_LIMIT = 48 * 1024 * 1024
ADAMW_TILE_ELEMS = 128 * 1024
PAIR_TILE_ELEMS = 1024 * 1024
DW_TK = 4096

MESH = pl.DeviceIdType.MESH
ANY = pl.BlockSpec(memory_space=pl.ANY)

NN = (((1,), (0,)), ((), ()))
NT = (((1,), (1,)), ((), ()))
TN = (((0,), (0,)), ((), ()))


def _dot(a, b, dims=NN):
    return lax.dot_general(a.astype(BF16), b.astype(BF16), dims, preferred_element_type=F32)


def _params(n_axes):
    return pltpu.CompilerParams(dimension_semantics=("arbitrary",) * n_axes, vmem_limit_bytes=VMEM_LIMIT)


def _tile(n, want):
    if n <= want:
        return n
    t = want
    while t >= LANE:
        if n % t == 0:
            return t
        t -= LANE
    return n


def _my_place():
    return lax.axis_index("x"), lax.axis_index("y"), lax.axis_index("c")


def _flat(px, py, pc):
    return 4 * px + 2 * py + pc


def _block_of(ref, kind, idx, rows, cols, sub=None):
    lo, n = (0, rows) if sub is None else sub
    if kind == "blk":
        return ref.at[idx, pl.ds(lo, n), :]
    if kind == "rows":
        return ref.at[pl.ds(pl.multiple_of(idx * rows + lo, n), n), :]
    return ref.at[pl.ds(lo, n), pl.ds(pl.multiple_of(idx * cols, cols), cols)]


CHIP_X, CHIP_Y, CHIP_DIAGONAL = 0, 1, 2
ALL_CHIPS = (CHIP_X, CHIP_Y, CHIP_DIAGONAL)


class _GatherJob:
    def __init__(self, shards, kinds, mid, late, pieces=(((0, 1), ALL_CHIPS, True),), base=None):
        nw = len(shards)
        self.nw, self.kinds = nw, list(kinds)
        self.pieces = [tuple(p) if len(p) == 4 else (*p, late) for p in pieces]
        for _, chips, _, _ in self.pieces:
            assert CHIP_DIAGONAL not in chips or len(chips) in (1, 3)
        self.shapes = [s.shape for s in shards]
        self.ins = list(shards) + (list(base) if base is not None else [])
        self.alias = [(nw + w, w) for w in range(nw)] if base is not None else []
        self.out_shape = []
        for s, kind in zip(shards, kinds):
            r, c = s.shape
            shape = {"blk": (N_DEV, r, c), "rows": (N_DEV * r, c), "cols": (r, N_DEV * c)}[kind]
            self.out_shape.append(jax.ShapeDtypeStruct(shape, s.dtype))
        n = 7 * nw * len(self.pieces)
        self.sems = [pltpu.SemaphoreType.DMA((n,)), pltpu.SemaphoreType.DMA((n,)), pltpu.SemaphoreType.DMA((n,))]
        whens = sorted({p[3] for p in self.pieces})
        self.phases = ([(0.0, self._start), (mid, self._relay)] + [(f, functools.partial(self._pass_on, f)) for f in whens]
                       + [(1.0, self._finish)])

    def _each(self, src, dst, send_sems, recv_sems, local_sems):
        x, y, c = _my_place()
        all_chips = [(1 - x, y), (x, 1 - y), (1 - x, 1 - y)]
        out = []
        for w in range(self.nw):
            r, cc = self.shapes[w]
            for p, ((k_of, of), chips, own, when) in enumerate(self.pieces):
                sub = (k_of * (r // of), r // of)
                s0 = 7 * (w * len(self.pieces) + p)
                mine = src[w].at[pl.ds(*sub), :]

                def place(dev, w=w, r=r, cc=cc, sub=sub):
                    return _block_of(dst[w], self.kinds[w], _flat(*dev), r, cc, sub)

                def copy(k, block, to, from_src=False, place=place, mine=mine, s0=s0):
                    return pltpu.make_async_remote_copy(
                        src_ref=mine if from_src else place(block), dst_ref=place(block),
                        send_sem=send_sems.at[s0 + k], recv_sem=recv_sems.at[s0 + k], device_id=to, device_id_type=MESH)

                relay = functools.partial(copy, 1 + CHIP_DIAGONAL, (x ^ (1 - c), y ^ c, c), (x ^ c, y ^ (1 - c), c))
                local = functools.partial(pltpu.make_async_copy, mine, place((x, y, c)), local_sems.at[s0])
                near = [(j, all_chips[j]) for j in chips if j != CHIP_DIAGONAL]
                out.append((near, CHIP_DIAGONAL in chips, own, copy, relay, local, when))
        return (x, y, c), (x, y, 1 - c), all_chips[CHIP_DIAGONAL], out

    def _start(self, *refs):
        me, sibling, diagonal, each = self._each(*refs)
        for near, far, own, copy, relay, local, when in each:
            if own:
                local().start()
                copy(0, me, sibling, from_src=True).start()
            for j, chip in near:
                copy(1 + j, me, (*chip, me[2]), from_src=True).start()
            if far and not near:
                relay().start()

    def _landed(self, near, copy, me, sibling):
        for j, chip in near:
            copy(1 + j, (*chip, me[2]), me).wait_recv()
            copy(4 + j, (*chip, me[2]), sibling).start()

    def _relay(self, *refs):
        me, sibling, diagonal, each = self._each(*refs)
        for near, far, own, copy, relay, local, when in each:
            if far and near:
                self._landed(near, copy, me, sibling)
                relay().start()

    def _pass_on(self, frac, *refs):
        me, sibling, diagonal, each = self._each(*refs)
        for near, far, own, copy, relay, local, when in each:
            if when == frac:
                self._landed([(CHIP_DIAGONAL, diagonal)] if far else near, copy, me, sibling)

    def _finish(self, *refs):
        me, sibling, diagonal, each = self._each(*refs)
        for near, far, own, copy, relay, local, when in each:
            if own:
                copy(0, sibling, me).wait_recv()
            for j, chip in near + [(CHIP_DIAGONAL, diagonal)] * int(far):
                copy(4 + j, (*chip, 1 - me[2]), me).wait_recv()
        for near, far, own, copy, relay, local, when in each:
            if own:
                copy(0, me, sibling, from_src=True).wait_send()
                local().wait()
            for j, chip in near:
                copy(1 + j, me, (*chip, me[2]), from_src=True).wait_send()
                copy(4 + j, (*chip, me[2]), sibling).wait_send()
            if far:
                relay().wait_send()
                copy(4 + CHIP_DIAGONAL, (*diagonal, me[2]), sibling).wait_send()


REL_ALL = [(1, 0, 0), (1, 0, 1), (0, 1, 0), (0, 1, 1), (1, 1, 0), (1, 1, 1), (0, 0, 1)]
N_CHIP = 4


class _PairJob:
    def __init__(self, grads, kinds, shard_shapes):
        self.ins = list(grads)
        self.kinds, self.shard_shapes, self.nw = list(kinds), list(shard_shapes), len(grads)
        self.out_shape = [jax.ShapeDtypeStruct((N_CHIP,) + tuple(s), g.dtype) for g, s in zip(grads, shard_shapes)]
        n = N_CHIP * self.nw
        self.sems = [pltpu.SemaphoreType.DMA((n,)), pltpu.SemaphoreType.DMA((n,))]
        self.phases = [(0.0, self._start), (1.0, self._finish)]

    def _copies(self, src, theirs, send_sems, recv_sems):
        x, y, c = _my_place()
        remote = []
        for w in range(self.nw):
            r, cc = self.shard_shapes[w]
            for q in range(N_CHIP):
                s = N_CHIP * w + q
                remote.append(pltpu.make_async_remote_copy(
                    src_ref=_block_of(src[w], self.kinds[w], 2 * q + (1 - c), r, cc), dst_ref=theirs[w].at[q],
                    send_sem=send_sems.at[s], recv_sem=recv_sems.at[s], device_id=(x, y, 1 - c), device_id_type=MESH))
        return remote

    def _start(self, *refs):
        for cp in self._copies(*refs):
            cp.start()

    def _finish(self, *refs):
        for cp in self._copies(*refs):
            cp.wait()


CHIP_RELS = [(1, 0), (0, 1), (1, 1)]


class _ChipJob:
    def __init__(self, chip_sums, shard_shapes, rels=CHIP_RELS, by_source=None, piece=(0, 1), base=None):
        nw = len(chip_sums)
        self.ins = list(chip_sums) + ([by_source] if by_source is not None else []) + (list(base) if base is not None else [])
        self.nw, self.has_src, self.rels, self.piece = nw, by_source is not None, list(rels), piece
        self.rows = [s[0] for s in shard_shapes]
        self.alias = [(nw + int(self.has_src) + w, w) for w in range(nw)] if base is not None else []
        self.out_shape = [jax.ShapeDtypeStruct((len(rels),) + tuple(s), a.dtype) for a, s in zip(chip_sums, shard_shapes)]
        if self.has_src:
            self.out_shape.append(jax.ShapeDtypeStruct((N_DEV,) + by_source.shape, by_source.dtype))
        n = len(rels) * self.nw + len(REL_ALL) * int(self.has_src)
        self.sems = [pltpu.SemaphoreType.DMA((n,)), pltpu.SemaphoreType.DMA((n,)), pltpu.SemaphoreType.DMA((1,))]
        self.phases = [(0.0, self._start), (1.0, self._finish)]

    def _copies(self, src, land, send_sems, recv_sems, local_sems):
        x, y, c = _my_place()
        nw = self.nw

        def remote_copy(s, src_ref, dst_ref, peer):
            return pltpu.make_async_remote_copy(src_ref=src_ref, dst_ref=dst_ref, send_sem=send_sems.at[s],
                                                recv_sem=recv_sems.at[s], device_id=peer, device_id_type=MESH)

        def rows(w):
            n = self.rows[w] // self.piece[1]
            return pl.ds(self.piece[0] * n, n)

        local, remote = [], []
        for k, (fx, fy) in enumerate(self.rels):
            px, py = x ^ fx, y ^ fy
            remote += [remote_copy(nw * k + w, src[w].at[2 * px + py, rows(w), :], land[w].at[k, rows(w), :], (px, py, c))
                       for w in range(nw)]
        if self.has_src:
            my_idx = _flat(x, y, c)
            local.append(pltpu.make_async_copy(src[nw], land[nw].at[my_idx], local_sems.at[0]))
            remote += [remote_copy(nw * len(self.rels) + k, src[nw], land[nw].at[my_idx], (x ^ fx, y ^ fy, c ^ fc))
                       for k, (fx, fy, fc) in enumerate(REL_ALL)]
        return local, remote

    def _start(self, *refs):
        local, remote = self._copies(*refs)
        for cp in local + remote:
            cp.start()

    def _finish(self, *refs):
        local, remote = self._copies(*refs)
        for cp in remote + local:
            cp.wait()


def _row_tile(R, C, elems=ADAMW_TILE_ELEMS):
    tr = 16
    while tr * 2 * C <= elems and R % (tr * 2) == 0:
        tr *= 2
    return tr if R % tr == 0 else R


BF16_ROWS = 16


def _tile_2d(R, C, elems):
    if R % BF16_ROWS == 0:
        tr = _row_tile(R, C, elems)
        return tr, C, R // tr
    tc = C
    while tc % (2 * LANE) == 0 and R * tc > elems:
        tc //= 2
    return R, tc, C // tc


def _tile_index(tr, tc, C):
    return (lambda i: (i, 0)) if tc == C else (lambda i: (0, i))


def _part_spec(kind, tr, tc, R, C, n_lead):
    at = _tile_index(tr, tc, C)

    def dev(ids, pref):
        return 2 * ids[0] + pref[0] if n_lead == 2 else pref[0]

    def index(*args):
        ids, pref = args[:n_lead], args[n_lead]
        ri, ci = at(ids[-1])
        if kind == "blk":
            return dev(ids, pref), ri, ci
        if kind == "rows":
            return dev(ids, pref) * (R // tr) + ri, ci
        return ri, dev(ids, pref) * (C // tc) + ci

    return pl.BlockSpec((None, tr, tc) if kind == "blk" else (tr, tc), index)


def _pair_sum(name, grad, kind, shard_shape, theirs, c):
    R, C = shard_shape
    tr, tc, tiles = _tile_2d(R, C, PAIR_TILE_ELEMS)
    at = _tile_index(tr, tc, C)

    def body(c_ref, g_ref, t_ref, o_ref):
        o_ref[...] = (g_ref[...].astype(F32) + t_ref[...].astype(F32)).astype(o_ref.dtype)

    slot = pl.BlockSpec((None, tr, tc), lambda q, i, c_ref: (q, *at(i)))
    return pl.pallas_call(
        body, name=name,
        grid_spec=pltpu.PrefetchScalarGridSpec(
            num_scalar_prefetch=1, grid=(N_CHIP, tiles),
            in_specs=[_part_spec(kind, tr, tc, R, C, 2), slot], out_specs=slot),
        out_shape=jax.ShapeDtypeStruct(theirs.shape, theirs.dtype), compiler_params=_params(2))(c, grad, theirs)


class _Carried:
    def __init__(self, jobs, grid):
        self.jobs = list(jobs)
        self.steps = int(np.prod(grid))
        self.grid = tuple(grid)
        self.operands = [a for j in self.jobs for a in j.ins]
        self.out_shape = [o for j in self.jobs for o in j.out_shape]
        self.sems = [s for j in self.jobs for s in j.sems]
        self.in_specs = [ANY] * len(self.operands)
        self.out_specs = [ANY] * len(self.out_shape)

    def aliases(self, n_in, n_out):
        out, i0, o0 = {}, n_in, n_out
        for job in self.jobs:
            for ji, jo in getattr(job, "alias", []):
                out[i0 + ji] = o0 + jo
            i0, o0 = i0 + len(job.ins), o0 + len(job.out_shape)
        return out

    def _step(self):
        lin = pl.program_id(0)
        for a in range(1, len(self.grid)):
            lin = lin * self.grid[a] + pl.program_id(a)
        return lin

    def _run(self, in_refs, out_refs, sem_refs, last):
        lin = self._step()
        for job in self.jobs:
            ins, in_refs = in_refs[:len(job.ins)], in_refs[len(job.ins):]
            outs, out_refs = out_refs[:len(job.out_shape)], out_refs[len(job.out_shape):]
            sems, sem_refs = sem_refs[:len(job.sems)], sem_refs[len(job.sems):]
            for frac, fn in job.phases:
                if (frac >= 1.0) != last:
                    continue
                at = min(self.steps - 1, int(frac * (self.steps - 1)))
                pl.when(lin == at)(functools.partial(fn, ins, outs, *sems))

    def before(self, in_refs, out_refs, sem_refs):
        self._run(in_refs, out_refs, sem_refs, last=False)

    def after(self, in_refs, out_refs, sem_refs):
        self._run(in_refs, out_refs, sem_refs, last=True)


def _split_refs(refs, n_in, n_job_in, n_out, n_job_out, n_scratch):
    cuts = np.cumsum([0, n_in, n_job_in, n_out, n_job_out, n_scratch])
    return [refs[cuts[i]:cuts[i + 1]] for i in range(5)] + [refs[cuts[5]:]]


def _matmul(name, a, b, *, ta=False, tb=False, out_dtypes=(F32,), epilogue=None, extras=(), tm=1024, tn=1024, tk=2048, jobs=(),
            n=None):
    if ta:
        K, M = a.shape
    else:
        M, K = a.shape
    N = n if n is not None else (b.shape[0] if tb else b.shape[1])
    if ta:
        tk = max(tk, DW_TK)
    tm, tn, tk = _tile(M, tm), _tile(N, tn), _tile(K, tk)
    nk = K // tk
    grid = (M // tm, N // tn, nk)
    dims = TN if ta else (NT if tb else NN)
    n_ex, n_out = len(extras), len(out_dtypes)
    carried = _Carried(jobs, grid)

    def body(*refs):
        ins, job_ins, out_refs, job_outs, scratch, job_sems = _split_refs(
            refs, 2 + n_ex, len(carried.operands), n_out, len(carried.out_shape), int(nk > 1))
        a_ref, b_ref, ex_refs = ins[0], ins[1], ins[2:]
        k = pl.program_id(2)
        carried.before(job_ins, job_outs, job_sems)

        def finish(acc):
            res = (acc,) if epilogue is None else epilogue(acc, *[e[...] for e in ex_refs])
            for o_ref, r in zip(out_refs, res):
                o_ref[...] = r.astype(o_ref.dtype)

        if nk == 1:
            finish(_dot(a_ref[...], b_ref[...], dims))
        else:
            acc_ref = scratch[0]

            @pl.when(k == 0)
            def _():
                acc_ref[...] = _dot(a_ref[...], b_ref[...], dims)

            @pl.when(k > 0)
            def _():
                acc_ref[...] += _dot(a_ref[...], b_ref[...], dims)

            pl.when(k == nk - 1)(lambda: finish(acc_ref[...]))

        carried.after(job_ins, job_outs, job_sems)

    a_spec = pl.BlockSpec((tk, tm), lambda i, j, k: (k, i)) if ta else pl.BlockSpec((tm, tk), lambda i, j, k: (i, k))
    b_spec = pl.BlockSpec((tn, tk), lambda i, j, k: (j, k)) if tb else pl.BlockSpec((tk, tn), lambda i, j, k: (k, j))
    tile_spec = pl.BlockSpec((tm, tn), lambda i, j, k: (i, j))
    outs = pl.pallas_call(
        body,
        name=name,
        grid=grid,
        in_specs=[a_spec, b_spec] + [tile_spec] * n_ex + carried.in_specs,
        out_specs=[tile_spec] * n_out + carried.out_specs,
        out_shape=[jax.ShapeDtypeStruct((M, N), d) for d in out_dtypes] + carried.out_shape,
        scratch_shapes=[pltpu.VMEM((tm, tn), F32)] * int(nk > 1) + carried.sems,
        input_output_aliases=carried.aliases(2 + n_ex, n_out),
        compiler_params=_params(3),
    )(a, b, *extras, *carried.operands)
    return outs[0] if len(outs) == 1 else outs


def _rowwise(name, fn, row_in, const_in, row_out, acc_out, tm, jobs=()):
    T = row_in[0][0].shape[0]
    n_r, n_c, n_o, n_a = len(row_in), len(const_in), len(row_out), len(acc_out)
    carried = _Carried(jobs, (T // tm,))

    def body(*refs):
        in_refs, job_ins, own_outs, job_outs, _, job_sems = _split_refs(
            refs, n_r + n_c, len(carried.operands), n_o + n_a, len(carried.out_shape), 0)
        out_refs, acc_refs = own_outs[:n_o], own_outs[n_o:]
        carried.before(job_ins, job_outs, job_sems)
        res = fn(*[r[...] for r in in_refs])
        for o_ref, r in zip(out_refs, res[:n_o]):
            o_ref[...] = r.astype(o_ref.dtype)

        @pl.when(pl.program_id(0) == 0)
        def _():
            for a_ref in acc_refs:
                a_ref[...] = jnp.zeros_like(a_ref)

        for a_ref, r in zip(acc_refs, res[n_o:]):
            a_ref[...] += jnp.sum(r.reshape(tm // SUBLANE, SUBLANE, r.shape[-1]), axis=0)
        carried.after(job_ins, job_outs, job_sems)

    in_specs = [pl.BlockSpec((tm, w), functools.partial(lambda i, c: (i, c), c=cb)) for (_, w, cb) in row_in]
    in_specs += [pl.BlockSpec(c.shape, functools.partial(lambda i, nd: (0,) * nd, nd=c.ndim)) for c in const_in]
    out_specs = [pl.BlockSpec((tm, c), lambda i: (i, 0)) for (c, _) in row_out]
    out_specs += [pl.BlockSpec((SUBLANE, c), lambda i: (0, 0)) for c in acc_out]
    out_shape = [jax.ShapeDtypeStruct((T, c), d) for (c, d) in row_out]
    out_shape += [jax.ShapeDtypeStruct((SUBLANE, c), F32) for c in acc_out]
    return pl.pallas_call(
        body, name=name, grid=(T // tm,), in_specs=in_specs + carried.in_specs, out_specs=out_specs + carried.out_specs,
        out_shape=out_shape + carried.out_shape, scratch_shapes=carried.sems,
        input_output_aliases=carried.aliases(n_r + n_c, n_o + n_a), compiler_params=_params(1),
    )(*[r[0] for r in row_in], *const_in, *carried.operands)


def _ln_stats(z):
    mu = jnp.mean(z, axis=-1, keepdims=True)
    var = jnp.mean(jnp.square(z - mu), axis=-1, keepdims=True)
    rstd = lax.rsqrt(var + EPS)
    return (z - mu) * rstd, rstd


def _ln_bwd(dy, xhat, rstd, g):
    dxh = dy * g
    return rstd * (dxh - jnp.mean(dxh, axis=-1, keepdims=True) - xhat * jnp.mean(dxh * xhat, axis=-1, keepdims=True))


def _rms(x, g):
    return x * lax.rsqrt(jnp.mean(jnp.square(x), axis=-1, keepdims=True) + EPS) * g


def _rope128(x, cos2, sin2):
    return x * cos2 + pltpu.roll(x, RET_DK // 2, 1) * sin2


def _unrope128(dy, cos2, sin2):
    return dy * cos2 + pltpu.roll(dy * sin2, RET_DK // 2, 1)


def _rope64(x, cosp, sina, sinb):
    h = MLA_ROPE // 2
    return x * cosp + pltpu.roll(x, h, 1) * sina + pltpu.roll(x, LANE - h, 1) * sinb


def _unrope64(dy, cosp, sina, sinb):
    h = MLA_ROPE // 2
    return dy * cosp + pltpu.roll(dy * sina, LANE - h, 1) + pltpu.roll(dy * sinb, h, 1)


RET_BLOCK = 512
RET_HEADS_PER_STEP = 4


def _ret_tables(H, B):
    log_g = np.log1p(-np.exp2(-5.0 - np.arange(H, dtype=np.float32))).astype(np.float32)
    idx = np.arange(B, dtype=np.float32)
    dist = np.abs(idx[:, None] - idx[None, :])
    visible = (np.arange(B)[None, :] // CHUNK) <= (np.arange(B)[:, None] // CHUNK)
    dmat = np.exp(log_g[:, None, None] * dist).astype(np.float32) * visible[None].astype(np.float32)
    qd = np.exp(log_g[:, None] * (idx + 1.0)).astype(np.float32)
    kd = np.exp(log_g[:, None] * (B - 1 - idx)).astype(np.float32)
    sd = np.exp(log_g * B).astype(np.float32)
    bc = lambda v: np.ascontiguousarray(np.broadcast_to(v[:, :, None], (H, v.shape[1], LANE)))
    sdb = np.ascontiguousarray(np.broadcast_to(sd[:, None, None], (H, SUBLANE, LANE)))
    return jnp.asarray(dmat), jnp.asarray(bc(qd)), jnp.asarray(bc(kd)), jnp.asarray(sdb)


def _ret_specs(H, B, reverse, nb):
    blk = (lambda b: nb - 1 - b) if reverse else (lambda b: b)
    G = min(RET_HEADS_PER_STEP, H)
    W = G * LANE
    col = lambda off: pl.BlockSpec((B, W), functools.partial(lambda h, b, off: (blk(b), off // G + h), off=off))
    row = pl.BlockSpec((B, LANE), lambda h, b: (blk(b), 0))
    per_head = lambda r: pl.BlockSpec((G, r, LANE), lambda h, b: (h, 0, 0))
    dmat = pl.BlockSpec((G, B, B), lambda h, b: (h, 0, 0))
    return G, col, row, per_head, dmat


def _ret_call(name, body, n_in, n_out, operands, in_specs, out_specs, out_shape, grid, G, jobs, own_alias=None):
    carried = _Carried(jobs, grid)

    def wrapped(*refs):
        ins, job_ins, outs, job_outs, (state,), job_sems = _split_refs(
            refs, n_in, len(carried.operands), n_out, len(carried.out_shape), 1)
        carried.before(job_ins, job_outs, job_sems)

        @pl.when(pl.program_id(1) == 0)
        def _():
            state[...] = jnp.zeros_like(state)

        body(*ins, *outs, state)
        carried.after(job_ins, job_outs, job_sems)

    return pl.pallas_call(
        wrapped, name=name, grid=grid,
        in_specs=in_specs + carried.in_specs, out_specs=out_specs + carried.out_specs,
        out_shape=out_shape + carried.out_shape,
        scratch_shapes=[pltpu.VMEM((G, RET_DK, RET_DV), F32)] + carried.sems,
        input_output_aliases={**(own_alias or {}), **carried.aliases(n_in, n_out)},
        compiler_params=_params(2),
    )(*operands, *carried.operands)


def _retention_fwd(p1, cos2, sin2, gn_g, mix_in, H, jobs=()):
    T = p1.shape[0]
    B = min(RET_BLOCK, T)
    nb = T // B
    scale = RET_DK ** -0.5
    dmat, qd, kd, sd = _ret_tables(H, B)
    G, col, row, per_head, dspec = _ret_specs(H, B, False, nb)

    def body(q_ref, k_ref, v_ref, g_ref, cos_ref, sin_ref, d_ref, qd_ref, kd_ref, sd_ref, gn_ref, _, pre_ref, out_ref, s_ref):
        cos, sin = cos_ref[...], sin_ref[...]
        for g in range(G):
            hl = pl.ds(g * LANE, LANE)
            q = _rope128(q_ref[:, hl], cos, sin)
            k = _rope128(k_ref[:, hl], cos, sin) * scale
            v = v_ref[:, hl]
            s = s_ref[g]
            a = _dot(q, k, NT) * d_ref[g]
            ret = _dot(a, v) + _dot(q * qd_ref[g], s)
            s_ref[g] = sd_ref[g, 0:1, :] * s + _dot(k * kd_ref[g], v, TN)
            pre_ref[:, hl] = ret
            nrm, _ = _ln_stats(ret)
            out_ref[:, hl] = (jax.nn.silu(g_ref[:, hl]) * (nrm * gn_ref[:, hl])).astype(out_ref.dtype)

    out_spec = pl.BlockSpec((B, G * LANE), lambda h, b: (b, h))
    return _ret_call(
        "retention_fwd", body, 12, 2, (p1, p1, p1, p1, cos2, sin2, dmat, qd, kd, sd, gn_g, mix_in),
        [col(0), col(H), col(2 * H), col(3 * H), row, row, dspec, per_head(B), per_head(B), per_head(SUBLANE),
         pl.BlockSpec((1, G * LANE), lambda h, b: (0, h)), ANY],
        [out_spec, out_spec],
        [jax.ShapeDtypeStruct((T, H * RET_DV), F32), jax.ShapeDtypeStruct(mix_in.shape, mix_in.dtype)],
        (H // G, nb), G, jobs, own_alias={11: 1})


def _retention_bwd(p1, dpre, cos2, sin2, H, dq_jobs=(), dkv_jobs=()):
    T = p1.shape[0]
    B = min(RET_BLOCK, T)
    nb = T // B
    scale = RET_DK ** -0.5
    dmat, qd, kd, sd = _ret_tables(H, B)

    def dq_body(k_ref, v_ref, do_ref, cos_ref, sin_ref, d_ref, qd_ref, kd_ref, sd_ref, dq_ref, s_ref):
        cos, sin = cos_ref[...], sin_ref[...]
        for g in range(G):
            hl = pl.ds(g * LANE, LANE)
            k = _rope128(k_ref[:, hl], cos, sin) * scale
            v, do, s = v_ref[:, hl], do_ref[:, hl], s_ref[g]
            da = _dot(do, v, NT) * d_ref[g]
            dq = _dot(da, k) + _dot(do, s, NT) * qd_ref[g]
            dq_ref[:, hl] = _unrope128(dq, cos, sin).astype(dq_ref.dtype)
            s_ref[g] = sd_ref[g, 0:1, :] * s + _dot(k * kd_ref[g], v, TN)

    G, col, row, per_head, dspec = _ret_specs(H, B, False, nb)
    dq = _ret_call(
        "retention_bwd_dq", dq_body, 9, 1, (p1, p1, dpre, cos2, sin2, dmat, qd, kd, sd),
        [col(H), col(2 * H), col(0), row, row, dspec, per_head(B), per_head(B), per_head(SUBLANE)],
        [pl.BlockSpec((B, G * LANE), lambda h, b: (b, h))], [jax.ShapeDtypeStruct((T, H * RET_DK), BF16)],
        (H // G, nb), G, dq_jobs)

    def dkv_body(q_ref, k_ref, v_ref, do_ref, cos_ref, sin_ref, d_ref, qd_ref, kd_ref, sd_ref, dk_ref, dv_ref, g_ref):
        cos, sin = cos_ref[...], sin_ref[...]
        for g in range(G):
            hl = pl.ds(g * LANE, LANE)
            q = _rope128(q_ref[:, hl], cos, sin)
            k = _rope128(k_ref[:, hl], cos, sin) * scale
            v, do, st = v_ref[:, hl], do_ref[:, hl], g_ref[g]
            a = _dot(q, k, NT) * d_ref[g]
            da = _dot(do, v, NT) * d_ref[g]
            dv = _dot(a, do, TN) + _dot(k * kd_ref[g], st)
            dk = _dot(da, q, TN) + _dot(v, st, NT) * kd_ref[g]
            dv_ref[:, hl] = dv.astype(dv_ref.dtype)
            dk_ref[:, hl] = _unrope128(dk * scale, cos, sin).astype(dk_ref.dtype)
            g_ref[g] = sd_ref[g, 0:1, :] * st + _dot(q * qd_ref[g], do, TN)

    G, col, row, per_head, dspec = _ret_specs(H, B, True, nb)
    out_spec = pl.BlockSpec((B, G * LANE), lambda h, b: (nb - 1 - b, h))
    dkv = _ret_call(
        "retention_bwd_dkv", dkv_body, 10, 2, (p1, p1, p1, dpre, cos2, sin2, dmat, qd, kd, sd),
        [col(0), col(H), col(2 * H), col(0), row, row, dspec, per_head(B), per_head(B), per_head(SUBLANE)],
        [out_spec, out_spec],
        [jax.ShapeDtypeStruct((T, H * RET_DK), BF16), jax.ShapeDtypeStruct((T, H * RET_DV), BF16)],
        (H // G, nb), G, dkv_jobs)
    return dq, dkv


ATT_BLOCK = 512


ATT_ROW_CHUNKS = 2
ATT_BWD_ROW_CHUNKS = 1
ATT_HEADS_PER_STEP = 8
ATT_BWD_HEADS_PER_STEP = 2
LOG2E = 1.4426950408889634


def _chunk_mask(rows, row0, cols):
    qi = (lax.broadcasted_iota(jnp.int32, (rows, cols), 0) + row0) // CHUNK
    kj = lax.broadcasted_iota(jnp.int32, (rows, cols), 1) // CHUNK
    return kj <= qi


def _attention_fwd(qh, kh, vx, H, jobs=()):
    T = qh.shape[0]
    t = min(ATT_BLOCK, T)
    n = T // t
    r = t // ATT_ROW_CHUNKS
    G = min(ATT_HEADS_PER_STEP, H)
    P = MLA_HEAD_PAD
    scale = (MLA_NOPE + MLA_ROPE) ** -0.5
    c2 = scale * LOG2E
    carried = _Carried(jobs, (H // G, n, n))

    def body(*refs):
        (q_ref, k_ref, v_ref), job_ins, (o_ref, ob_ref, lse_ref), job_outs, (m_ref, acc_ref), job_sems = _split_refs(
            refs, 3, len(carried.operands), 3, len(carried.out_shape), 2)
        i, j = pl.program_id(1), pl.program_id(2)
        carried.before(job_ins, job_outs, job_sems)

        @pl.when(j == 0)
        def _():
            m_ref[...] = jnp.full_like(m_ref, -jnp.inf)
            acc_ref[...] = jnp.zeros_like(acc_ref)

        def update(masked):
            for g in range(G):
                hc = pl.ds(g * P, P)
                for ch in range(ATT_ROW_CHUNKS):
                    rs = pl.ds(ch * r, r)
                    kc = (ch + 1) * r if masked else t
                    ks = pl.ds(0, kc)
                    s = _dot(q_ref[rs, hc], k_ref[ks, hc], NT)
                    if masked:
                        s = jnp.where(_chunk_mask(r, ch * r, kc), s, -jnp.inf)
                    m_prev = m_ref[rs, pl.ds(g * LANE, LANE)]
                    m_new = jnp.maximum(m_prev, jnp.max(s, axis=-1, keepdims=True))
                    alpha = jnp.exp2((m_prev - m_new) * c2)
                    p = jnp.exp2((s - jnp.tile(m_new, (1, kc // LANE))) * c2)
                    acc_ref[rs, hc] = jnp.tile(alpha, (1, 2)) * acc_ref[rs, hc] + _dot(p, v_ref[ks, hc])
                    m_ref[rs, pl.ds(g * LANE, LANE)] = m_new

        pl.when(j < i)(functools.partial(update, False))
        pl.when(j == i)(functools.partial(update, True))

        @pl.when(j == n - 1)
        def _():
            for g in range(G):
                acc = acc_ref[:, pl.ds(g * P, P)]
                l = acc[:, MLA_DV:]
                o = acc[:, :MLA_DV] / l
                o_ref[:, pl.ds(g * MLA_DV, MLA_DV)] = o
                ob_ref[:, pl.ds(g * MLA_DV, MLA_DV)] = o.astype(ob_ref.dtype)
                lse_ref[g] = m_ref[:, pl.ds(g * LANE, LANE)] * scale + jnp.log(l)

        carried.after(job_ins, job_outs, job_sems)

    kclamp = lambda i, j: jnp.minimum(j, i)
    o_spec = pl.BlockSpec((t, G * MLA_DV), lambda h, i, j: (i, h))
    mix_spec = pl.BlockSpec((t, G * MLA_DV), lambda h, i, j: (i, H // G + h))
    return pl.pallas_call(
        body, name="attention_fwd", grid=(H // G, n, n),
        in_specs=[pl.BlockSpec((t, G * P), lambda h, i, j: (i, h)),
                  pl.BlockSpec((t, G * P), lambda h, i, j: (kclamp(i, j), h)),
                  pl.BlockSpec((t, G * P), lambda h, i, j: (kclamp(i, j), h))] + carried.in_specs,
        out_specs=[o_spec, mix_spec, pl.BlockSpec((G, t, LANE), lambda h, i, j: (h, i, 0))] + carried.out_specs,
        out_shape=[jax.ShapeDtypeStruct((T, H * MLA_DV), F32), jax.ShapeDtypeStruct((T, 2 * H * MLA_DV), BF16),
                   jax.ShapeDtypeStruct((H, T, LANE), F32)] + carried.out_shape,
        scratch_shapes=[pltpu.VMEM((t, G * LANE), F32), pltpu.VMEM((t, G * P), F32)] + carried.sems,
        input_output_aliases=carried.aliases(3, 3),
        compiler_params=_params(3),
    )(qh, kh, vx, *carried.operands)


def _attention_bwd(qh, kh, vx, o, lse, dmix, H, jobs=()):
    T = qh.shape[0]
    t = min(ATT_BLOCK, T)
    n = T // t
    r = t // ATT_BWD_ROW_CHUNKS
    G = min(ATT_BWD_HEADS_PER_STEP, H)
    P = MLA_HEAD_PAD
    scale = (MLA_NOPE + MLA_ROPE) ** -0.5
    c2 = scale * LOG2E
    carried = _Carried(jobs, (H // G, n, n))

    def body(*refs):
        ((q_ref, k_ref, v_ref, do_ref, o_ref, lse_ref), job_ins, (dq_ref, dkn_ref, dkr_ref, dv_ref), job_outs,
         (dk_acc, dv_acc), job_sems) = _split_refs(refs, 6, len(carried.operands), 4, len(carried.out_shape), 2)
        j, i = pl.program_id(1), pl.program_id(2)
        carried.before(job_ins, job_outs, job_sems)

        @pl.when(jnp.logical_and(j == 0, i == 0))
        def _():
            dq_ref[...] = jnp.zeros_like(dq_ref)

        @pl.when(i == j)
        def _():
            dk_acc[...] = jnp.zeros_like(dk_acc)
            dv_acc[...] = jnp.zeros_like(dv_acc)

        def update(masked):
            for g in range(G):
                hc, hv = pl.ds(g * P, P), pl.ds(g * MLA_DV, MLA_DV)
                for ch in range(ATT_BWD_ROW_CHUNKS):
                    rs = pl.ds(ch * r, r)
                    kc = (ch + 1) * r if masked else t
                    ks = pl.ds(0, kc)
                    k, v = k_ref[ks, hc], v_ref[ks, pl.ds(g * P, MLA_DV)]
                    q, do = q_ref[rs, hc], do_ref[rs, hv]
                    s = _dot(q, k, NT)
                    if masked:
                        s = jnp.where(_chunk_mask(r, ch * r, kc), s, -jnp.inf)
                    p = jnp.exp2(s * c2 - jnp.tile(lse_ref[g, rs, :] * LOG2E, (1, kc // LANE)))
                    dv_acc[ks, hv] += _dot(p, do, TN)
                    dp = _dot(do, v, NT)
                    delta = jnp.sum(do * o_ref[rs, hv], axis=-1, keepdims=True)
                    ds = p * (dp - delta)
                    rows = pl.ds(pl.multiple_of(i * t, t) + ch * r, r)
                    dq_ref[rows, hc] += _dot(ds, k)
                    dk_acc[ks, hc] += _dot(ds, q, TN)

        pl.when(i > j)(functools.partial(update, False))
        pl.when(i == j)(functools.partial(update, True))

        @pl.when(i == n - 1)
        def _():
            for g in range(G):
                dk = dk_acc[:, pl.ds(g * P, P)] * scale
                dkn_ref[:, pl.ds(g * LANE, LANE)] = dk[:, :MLA_NOPE].astype(dkn_ref.dtype)
                dkr_ref[:, pl.ds(g * LANE, LANE)] = dk[:, MLA_NOPE:]
            dv_ref[...] = dv_acc[...].astype(dv_ref.dtype)

        carried.after(job_ins, job_outs, job_sems)

    qclamp = lambda j, i: jnp.maximum(i, j)
    kv_out = pl.BlockSpec((t, G * LANE), lambda h, j, i: (j, h))
    q_in = pl.BlockSpec((t, G * MLA_DV), lambda h, j, i: (qclamp(j, i), h))
    return pl.pallas_call(
        body, name="attention_bwd", grid=(H // G, n, n),
        in_specs=[pl.BlockSpec((t, G * P), lambda h, j, i: (qclamp(j, i), h)),
                  pl.BlockSpec((t, G * P), lambda h, j, i: (j, h)),
                  pl.BlockSpec((t, G * P), lambda h, j, i: (j, h)),
                  pl.BlockSpec((t, G * MLA_DV), lambda h, j, i: (qclamp(j, i), H // G + h)),
                  q_in,
                  pl.BlockSpec((G, t, LANE), lambda h, j, i: (h, qclamp(j, i), 0))] + carried.in_specs,
        out_specs=[pl.BlockSpec((T, G * P), lambda h, j, i: (0, h)), kv_out, kv_out, kv_out] + carried.out_specs,
        out_shape=[jax.ShapeDtypeStruct((T, H * P), F32), jax.ShapeDtypeStruct((T, H * LANE), BF16),
                   jax.ShapeDtypeStruct((T, H * LANE), F32), jax.ShapeDtypeStruct((T, H * LANE), BF16)] + carried.out_shape,
        scratch_shapes=[pltpu.VMEM((t, G * P), F32), pltpu.VMEM((t, G * MLA_DV), F32)] + carried.sems,
        input_output_aliases=carried.aliases(6, 4),
        compiler_params=_params(3),
    )(qh, kh, vx, dmix, o, lse, *carried.operands)


def _adamw_math(w, g, m, v):
    m = ADAM_B1 * m + (1.0 - ADAM_B1) * g
    v = ADAM_B2 * v + (1.0 - ADAM_B2) * jnp.square(g)
    m_hat = m / (1.0 - ADAM_B1 ** ADAM_STEP)
    v_hat = v / (1.0 - ADAM_B2 ** ADAM_STEP)
    delta = -ADAM_LR * (m_hat / (jnp.sqrt(v_hat) + ADAM_EPS) + ADAM_WD * w)
    return delta, m, v


def _adamw(name, chip_sums, my_chip, landed, w, m, v, jobs=()):
    R, C = w.shape
    tr, tc, tiles = _tile_2d(R, C, ADAMW_TILE_ELEMS)
    at = _tile_index(tr, tc, C)
    nl = len(landed)
    assert sum(l.shape[0] for l in landed) == N_CHIP - 1
    carried = _Carried(jobs, (tiles,))

    def body(q_ref, *refs):
        (own_ref, *rest), job_ins, (g_out, d_out, m_out, v_out), job_outs, _, job_sems = _split_refs(
            refs, nl + 4, len(carried.operands), 4, len(carried.out_shape), 0)
        l_refs, (w_ref, m_ref, v_ref) = rest[:nl], rest[nl:]
        carried.before(job_ins, job_outs, job_sems)
        g = own_ref[...].astype(F32)
        for l_ref in l_refs:
            for s in range(l_ref.shape[0]):
                g = g + l_ref[s].astype(F32)
        delta, mn, vn = _adamw_math(w_ref[...], g, m_ref[...], v_ref[...])
        g_out[...] = g
        d_out[...] = delta
        m_out[...] = mn
        v_out[...] = vn
        carried.after(job_ins, job_outs, job_sems)

    spec = pl.BlockSpec((tr, tc), lambda i, q_ref: at(i))
    own_spec = pl.BlockSpec((None, tr, tc), lambda i, q_ref: (q_ref[0], *at(i)))
    l_specs = [pl.BlockSpec((l.shape[0], tr, tc), lambda i, q_ref: (0, *at(i))) for l in landed]
    return pl.pallas_call(
        body, name=name,
        grid_spec=pltpu.PrefetchScalarGridSpec(
            num_scalar_prefetch=1, grid=(tiles,),
            in_specs=[own_spec] + l_specs + [spec, spec, spec] + carried.in_specs,
            out_specs=[spec] * 4 + carried.out_specs, scratch_shapes=carried.sems),
        out_shape=[jax.ShapeDtypeStruct((R, C), F32)] * 4 + carried.out_shape,
        input_output_aliases={1 + i: o for i, o in carried.aliases(nl + 4, 4).items()},
        compiler_params=_params(1),
    )(my_chip, chip_sums, *landed, w, m, v, *carried.operands)


def _adamw_small(landed, w, m, v):
    P = w.shape[1]

    def body(l_ref, w_ref, m_ref, v_ref, g_out, d_out, m_out, v_out, loss_out):
        acc = l_ref[0]
        for s in range(1, N_DEV):
            acc = acc + l_ref[s]
        total = jnp.sum(acc, axis=0, keepdims=True)
        g = total[:, :P]
        delta, mn, vn = _adamw_math(w_ref[...], g, m_ref[...], v_ref[...])
        g_out[...] = g
        d_out[...] = delta
        m_out[...] = mn
        v_out[...] = vn
        loss_out[...] = total[:, P:]

    return pl.pallas_call(
        body, name="adamw_replicated",
        out_shape=[jax.ShapeDtypeStruct((1, P), F32)] * 4 + [jax.ShapeDtypeStruct((1, LANE), F32)],
    )(landed, w, m, v)


def _rope_rows():
    inv_r = ROPE_BASE ** (-jnp.arange(0, RET_DK, 2, dtype=F32) / RET_DK)
    inv_m = ROPE_BASE ** (-jnp.arange(0, MLA_ROPE, 2, dtype=F32) / MLA_ROPE)
    h = MLA_ROPE // 2
    one, zero = jnp.ones((h,), F32), jnp.zeros((h,), F32)
    pad = jnp.zeros((LANE - MLA_ROPE,), F32)
    rows = [jnp.concatenate([inv_r, inv_r]), jnp.concatenate([-jnp.ones((RET_DK // 2,), F32), jnp.ones((RET_DK // 2,), F32)]),
            jnp.concatenate([inv_m, inv_m, pad]), jnp.concatenate([one, one, pad]),
            jnp.concatenate([zero, one, pad]), jnp.concatenate([-one, zero, pad])]
    return jnp.stack(rows + [jnp.zeros((LANE,), F32)] * (SUBLANE - len(rows)))


def _prep_tile(xt, p, rows):
    ang2, angp = p * rows[0:1], p * rows[2:3]
    sp = jnp.sin(angp)
    return (xt, jnp.cos(ang2), jnp.sin(ang2) * rows[1:2], jnp.cos(angp) * rows[3:4], sp * rows[4:5], sp * rows[5:6])


def kernel(x, positions, w_in, q_norm_g, w_uq, kv_norm_g, w_uk, w_uv, ret_gn_g, w_out, ln1_g, ln1_b, w_up, w_down, ln2_g, ln2_b, loss_target, m_w_in, m_q_norm_g, m_w_uq, m_kv_norm_g, m_w_uk, m_w_uv, m_ret_gn_g, m_w_out, m_ln1_g, m_ln1_b, m_w_up, m_w_down, m_ln2_g, m_ln2_b, v_w_in, v_q_norm_g, v_w_uq, v_kv_norm_g, v_w_uk, v_w_uv, v_ret_gn_g, v_w_out, v_ln1_g, v_ln1_b, v_w_up, v_w_down, v_ln2_g, v_ln2_b):
    T, D = x.shape[1], x.shape[2]
    H = D // 256
    RW = H * RET_DV
    x = x[0]
    tgt = loss_target[0]
    alpha = DEEPNORM_ALPHA
    in_width = w_in.shape[2] * N_DEV
    mla_in = in_width - 4 * RW
    mla_in_pad = -(-mla_in // 512) * 512
    f_uq = MLA_NOPE + MLA_ROPE

    names = ["w_in", "w_uq", "w_uk", "w_uv", "w_out", "w_up", "w_down"]
    big = dict(zip(names, [w_in[0].T, w_uq[0], w_uk[0], w_uv[0], w_out[0], w_up[0], w_down[0]]))
    kind = dict(zip(names, ["blk", "blk", "cols", "cols", "rows", "cols", "rows"]))
    half = {n: w.astype(BF16) for n, w in big.items()}

    def gather(ns, **parts):
        return _GatherJob([half[n] for n in ns], [kind[n] for n in ns], 0.55, 0.9, **parts)

    tr = min(256, T)
    xb, cos2, sin2, cosp, sina, sinb, g_in = _rowwise(
        "prep", _prep_tile, [(x, D, 0), (positions[0].astype(F32)[:, None], 1, 0)], [_rope_rows()],
        [(D, BF16)] + [(LANE, F32)] * 5, [], tr, jobs=[gather(["w_in"])])
    win = g_in.reshape(in_width, D)
    w1, n1 = win, 4 * RW
    w2 = jnp.pad(win[4 * RW:], ((0, mla_in_pad - mla_in), (0, 0)))

    first, second, near, far = (0, 2), (1, 2), (CHIP_X, CHIP_Y), (CHIP_DIAGONAL,)
    p1, g_uq, wuk, wuv, wout = _matmul("proj_ret", xb, w1, tb=True, n=n1,
                                       jobs=[gather(["w_uq", "w_uk", "w_uv", "w_out"])])
    wuq = jnp.transpose(g_uq, (1, 0, 2)).reshape(MLA_Q_RANK, H, f_uq)
    wuq = jnp.pad(wuq, ((0, 0), (0, 0), (0, MLA_HEAD_PAD - f_uq))).reshape(MLA_Q_RANK, H * MLA_HEAD_PAD)
    wukv = jnp.concatenate([wuk, wuv], axis=1)
    p2 = _matmul("proj_mla", xb, w2, tb=True)
    c_kv, c_kr = MLA_Q_RANK, MLA_Q_RANK + MLA_KV_RANK

    def mla_prep(p, cp, sa, sb, gq, gkv):
        cqn = _rms(p[:, :c_kv], gq)
        ckvn = _rms(p[:, c_kv:c_kr], gkv)
        return cqn, ckvn, _rope64(p[:, c_kr:c_kr + LANE], cp, sa, sb)

    cqn, ckvn, krope = _rowwise(
        "mla_prep", mla_prep, [(p2, mla_in_pad, 0), (cosp, LANE, 0), (sina, LANE, 0), (sinb, LANE, 0)],
        [q_norm_g, kv_norm_g], [(MLA_Q_RANK, BF16), (MLA_KV_RANK, BF16), (LANE, BF16)], [], tr)
    qf = _matmul("mla_q", cqn, wuq)
    kv = _matmul("mla_kv", ckvn, wukv, out_dtypes=(BF16,))

    def mla_heads(q, kn, vv, kr, cp, sa, sb):
        qs, ks, vs = [], [], []
        ones = jnp.ones((q.shape[0], MLA_DV), vv.dtype)
        for h in range(H):
            o = h * MLA_HEAD_PAD
            qs += [q[:, o:o + MLA_NOPE], _rope64(q[:, o + MLA_NOPE:o + MLA_HEAD_PAD], cp, sa, sb)]
            ks += [kn[:, h * MLA_NOPE:(h + 1) * MLA_NOPE], kr]
            vs += [vv[:, h * MLA_DV:(h + 1) * MLA_DV], ones]
        return jnp.concatenate(qs, axis=1), jnp.concatenate(ks, axis=1), jnp.concatenate(vs, axis=1)

    qh, kh, vx = _rowwise(
        "mla_heads", mla_heads,
        [(qf, H * MLA_HEAD_PAD, 0), (kv, RW, 0), (kv, RW, 1), (krope, LANE, 0), (cosp, LANE, 0), (sina, LANE, 0), (sinb, LANE, 0)],
        [], [(H * MLA_HEAD_PAD, BF16)] * 3, [], tr)
    whole = (0, 1)
    att_o, mix_half, lse, wup_a = _attention_fwd(
        qh, kh, vx, H, jobs=[gather(["w_up"], pieces=[(first, near, True, 0.45), (second, near, True, 0.75)])])
    ret_pre, mixin, wup = _retention_fwd(
        p1, cos2, sin2, ret_gn_g, mix_half, H,
        jobs=[gather(["w_up"], pieces=[(first, far, False, 0.55), (second, far, False, 0.9)], base=[wup_a])])
    mix, wdown_a = _matmul("mix_out", mixin, wout,
                           jobs=[gather(["w_down"], pieces=[((0, 4), near, True, 0.5), ((1, 4), near, True, 0.9)])])

    def ln1_fwd(xt, mt, g, b):
        z = alpha * xt + mt
        xhat, _ = _ln_stats(z)
        y = xhat * g + b
        return z, y, y

    z1, x1, x1b, wdown_b = _rowwise("ln1", ln1_fwd, [(x, D, 0), (mix, D, 0)], [ln1_g, ln1_b],
                                    [(D, F32), (D, F32), (D, BF16)], [], tr,
                                    jobs=[gather(["w_down"], pieces=[((2, 4), near, True, 0.5), ((3, 4), near, True, 0.9)],
                                                 base=[wdown_a])])

    def relu2(acc):
        r = jnp.maximum(acc, 0.0)
        return r * r, r

    a2b, rb, wdown = _matmul("mlp_up", x1b, wup, out_dtypes=(BF16, BF16), epilogue=relu2,
                             jobs=[gather(["w_down"], pieces=[(whole, far, False, 0.6)], base=[wdown_b])])
    hmlp = _matmul("mlp_down", a2b, wdown)

    def ln2_loss(x1t, ht, tt, g, b):
        xhat, rstd = _ln_stats(alpha * x1t + ht)
        err = xhat * g + b - tt
        dy = err / D
        dz = _ln_bwd(dy, xhat, rstd, g)
        lrow = 0.5 * jnp.mean(jnp.square(err), axis=-1, keepdims=True)
        return dz, dz, dy * xhat, dy, jnp.broadcast_to(lrow, (lrow.shape[0], LANE))

    dz2, dz2b, dg2, db2, lpart = _rowwise("ln2_loss", ln2_loss, [(x1, D, 0), (hmlp, D, 0), (tgt, D, 0)], [ln2_g, ln2_b],
                                          [(D, F32), (D, BF16)], [D, D, LANE], tr)

    def pair(grads):
        ns = list(grads)
        return _PairJob([grads[n] for n in ns], [kind[n] for n in ns], [big[n].shape for n in ns])

    my_core = lax.axis_index("c").astype(jnp.int32).reshape(1)
    my_chip = (2 * lax.axis_index("x") + lax.axis_index("y")).astype(jnp.int32).reshape(1)
    chip_sums = {}

    def pair_sums(grads, theirs):
        for n, t in zip(grads, theirs):
            chip_sums[n] = _pair_sum("pair_sum_" + n, grads[n], kind[n], big[n].shape, t, my_core)
        return {n: chip_sums[n] for n in grads}

    def chip(sums, rels=CHIP_RELS, by_source=None):
        ns = list(sums)
        return _ChipJob([sums[n] for n in ns], [big[n].shape for n in ns], rels, by_source)

    landed = {}
    da = _matmul("mlp_down_dx", dz2b, wdown, tb=True, out_dtypes=(BF16,), extras=(rb,),
                 epilogue=lambda acc, r: (acc * (2.0 * r.astype(F32)),))
    g_down = {"w_down": _matmul("mlp_down_dw", a2b, dz2b, ta=True, out_dtypes=(BF16,))}
    g_wup, *pr = _matmul("mlp_up_dw", x1b, da, ta=True, out_dtypes=(BF16,), jobs=[pair(g_down)])
    sums_down = pair_sums(g_down, pr)
    g_up = {"w_up": g_wup}
    dx1m, l_a, *pr = _matmul("mlp_up_dx", da, wup, tb=True, jobs=[chip(sums_down, CHIP_RELS[:2]), pair(g_up)])
    sums_up = pair_sums(g_up, pr)

    def ln1_bwd(dm, dzz, z, g):
        xhat, rstd = _ln_stats(z)
        dy = dm + alpha * dzz
        dz = _ln_bwd(dy, xhat, rstd, g)
        return dz, dz, dy * xhat, dy

    dz1, dz1b, dg1, db1 = _rowwise("ln1_bwd", ln1_bwd, [(dx1m, D, 0), (dz2, D, 0), (z1, D, 0)], [ln1_g],
                                   [(D, F32), (D, BF16)], [D, D], tr)
    g_wout = _matmul("mix_out_dw", mixin, dz1b, ta=True, out_dtypes=(BF16,))
    dmix = _matmul("mix_out_dx", dz1b, wout, tb=True)

    dqh, dkn, dkr, dvv, l_b, l_c = _attention_bwd(qh, kh, vx, att_o, lse, dmix, H,
                                                  jobs=[chip(sums_down, CHIP_RELS[2:]), chip(sums_up)])
    landed["w_down"], landed["w_up"] = [l_a, l_b], [l_c]
    att_scale = (MLA_NOPE + MLA_ROPE) ** -0.5

    def mla_heads_bwd(dq, dkrh, cp, sa, sb):
        parts, acc = [], dkrh[:, :LANE]
        dq = dq * att_scale
        for h in range(H):
            o = h * MLA_HEAD_PAD
            parts += [dq[:, o:o + MLA_NOPE], _unrope64(dq[:, o + MLA_NOPE:o + MLA_HEAD_PAD], cp, sa, sb)]
            if h:
                acc = acc + dkrh[:, h * LANE:(h + 1) * LANE]
        return jnp.concatenate(parts, axis=1), _unrope64(acc, cp, sa, sb)

    dqb, dkr128 = _rowwise(
        "mla_heads_bwd", mla_heads_bwd,
        [(dqh, H * MLA_HEAD_PAD, 0), (dkr, H * LANE, 0), (cosp, LANE, 0), (sina, LANE, 0), (sinb, LANE, 0)],
        [], [(H * MLA_HEAD_PAD, BF16), (LANE, F32)], [], tr)
    dcqn = _matmul("mla_q_dx", dqb, wuq, tb=True)
    g_wuq = _matmul("mla_q_dw", cqn, dqb, ta=True, out_dtypes=(BF16,))
    dkvb = jnp.concatenate([dkn, dvv], axis=1)
    dckvn = _matmul("mla_kv_dx", dkvb, wukv, tb=True)
    g_wukv = _matmul("mla_kv_dw", ckvn, dkvb, ta=True, out_dtypes=(BF16,))

    def rms_bwd(c, dy, g):
        rstd = lax.rsqrt(jnp.mean(jnp.square(c), axis=-1, keepdims=True) + EPS)
        dyg = dy * g
        dc = rstd * (dyg - c * (rstd * rstd) * jnp.mean(dyg * c, axis=-1, keepdims=True))
        return dc, dy * c * rstd

    def mla_prep_bwd(p, dq_, dkv_, dkr_, gq, gkv):
        dcq, dgq_ = rms_bwd(p[:, :c_kv], dq_, gq)
        dckv, dgkv_ = rms_bwd(p[:, c_kv:c_kr], dkv_, gkv)
        pad = jnp.zeros((p.shape[0], mla_in_pad - c_kr - LANE), F32)
        return jnp.concatenate([dcq, dckv, dkr_, pad], axis=1), dgq_, dgkv_

    d_p2, dgq, dgkv = _rowwise(
        "mla_prep_bwd", mla_prep_bwd, [(p2, mla_in_pad, 0), (dcqn, MLA_Q_RANK, 0), (dckvn, MLA_KV_RANK, 0), (dkr128, LANE, 0)],
        [q_norm_g, kv_norm_g], [(mla_in_pad, BF16)], [MLA_Q_RANK, MLA_KV_RANK], tr)

    def gate_bwd(pre, gate, dout, gn):
        dpre, dgate, dgn = [], [], []
        for h in range(H):
            sl = slice(h * RET_DV, (h + 1) * RET_DV)
            nrm, rstd = _ln_stats(pre[:, sl])
            gate_h, do_h, gn_h = gate[:, sl], dout[:, sl], gn[:, sl]
            sg = jax.nn.sigmoid(gate_h)
            silu = gate_h * sg
            dgate.append(do_h * (nrm * gn_h) * (sg * (1.0 + gate_h * (1.0 - sg))))
            dn = do_h * silu
            dgn.append(dn * nrm)
            dpre.append(_ln_bwd(dn, nrm, rstd, gn_h))
        return jnp.concatenate(dpre, axis=1), jnp.concatenate(dgate, axis=1), jnp.concatenate(dgn, axis=1)

    dpre, drg, dgn = _rowwise("ret_gate_bwd", gate_bwd, [(ret_pre, RW, 0), (p1, RW, 3), (dmix, RW, 0)], [ret_gn_g],
                              [(RW, F32), (RW, BF16)], [RW], tr)
    g_uq = g_wuq.reshape(MLA_Q_RANK, H, MLA_HEAD_PAD)[:, :, :f_uq].reshape(MLA_Q_RANK, N_DEV, H * f_uq // N_DEV)
    mid_grads = {"w_uq": jnp.transpose(g_uq, (1, 0, 2)), "w_uk": g_wukv[:, :RW], "w_uv": g_wukv[:, RW:], "w_out": g_wout}
    (drq, *pr), (drk, drv) = _retention_bwd(p1, dpre, cos2, sin2, H, dq_jobs=[pair(mid_grads)])
    sums_mid = pair_sums(mid_grads, pr)
    d_p1 = jnp.concatenate([drq, drk, drv, drg], axis=1)

    g_w1, *l_a = _matmul("proj_ret_dw", d_p1, xb, ta=True, out_dtypes=(BF16,), jobs=[chip(sums_mid)])
    for n, a_ in zip(mid_grads, l_a):
        landed[n] = [a_]
    g_w2 = _matmul("proj_mla_dw", d_p2, xb, ta=True, out_dtypes=(BF16,))

    g_win = {"w_in": jnp.concatenate([g_w1, g_w2[:mla_in]], axis=0).reshape(N_DEV, in_width // N_DEV, D)}
    small = jnp.concatenate([dgq, dgkv, dgn, dg1, db1, dg2, db2, lpart], axis=1)
    dx_a, *pr = _matmul("proj_mla_dx", d_p2, w2, extras=(dz1,), epilogue=lambda acc, d: (acc + alpha * d,), jobs=[pair(g_win)])
    sums_in = pair_sums(g_win, pr)
    grad_x, l_a, small_landed = _matmul("proj_ret_dx", d_p1, w1, extras=(dx_a,), epilogue=lambda acc, d: (acc + d,),
                                        jobs=[chip(sums_in, CHIP_RELS[:2], by_source=small)])

    moments = dict(zip(names, zip([m_w_in, m_w_uq, m_w_uk, m_w_uv, m_w_out, m_w_up, m_w_down],
                                  [v_w_in, v_w_uq, v_w_uk, v_w_uv, v_w_out, v_w_up, v_w_down])))
    res = {}

    def adamw(n, jobs=()):
        as_held = (lambda a: a[0].T) if n == "w_in" else (lambda a: a[0])
        as_given = (lambda r: r.T[None]) if n == "w_in" else (lambda r: r[None])
        out = _adamw("adamw_" + n, chip_sums[n], my_chip, landed[n], big[n], as_held(moments[n][0]), as_held(moments[n][1]),
                     jobs=jobs)
        res[n] = [as_given(r) for r in out[:4]]
        return out[4:]

    (l_b,) = adamw("w_out", jobs=[chip(sums_in, CHIP_RELS[2:])])
    landed["w_in"] = [l_a, l_b]
    for n in names:
        if n != "w_out":
            adamw(n)

    small_names = ["q_norm_g", "kv_norm_g", "ret_gn_g", "ln1_g", "ln1_b", "ln2_g", "ln2_b"]
    small_w = [q_norm_g, kv_norm_g, ret_gn_g, ln1_g, ln1_b, ln2_g, ln2_b]
    small_m = [m_q_norm_g, m_kv_norm_g, m_ret_gn_g, m_ln1_g, m_ln1_b, m_ln2_g, m_ln2_b]
    small_v = [v_q_norm_g, v_kv_norm_g, v_ret_gn_g, v_ln1_g, v_ln1_b, v_ln2_g, v_ln2_b]
    cat = lambda arrs: jnp.concatenate(arrs, axis=1)
    *sres, loss_lanes = _adamw_small(small_landed, cat(small_w), cat(small_m), cat(small_v))
    loss = loss_lanes[0, 0]
    off = 0
    for n, w in zip(small_names, small_w):
        res[n] = [r[:, off:off + w.shape[1]] for r in sres]
        off += w.shape[1]

    order = ["w_in", "q_norm_g", "w_uq", "kv_norm_g", "w_uk", "w_uv", "ret_gn_g", "w_out", "ln1_g", "ln1_b",
             "w_up", "w_down", "ln2_g", "ln2_b"]
    outs = [loss, grad_x[None]]
    for k in range(4):
        outs += [res[n][k] for n in order]
    return tuple(outs)
```

```python
import functools

import numpy as np

import jax
import jax.numpy as jnp
from jax import lax
from jax.experimental import pallas as pl
from jax.experimental.pallas import tpu as pltpu

F32 = jnp.float32
BF16 = jnp.bfloat16

CHUNK = 64
RET_DK = 128
RET_DV = 128
MLA_NOPE = 128
MLA_ROPE = 64
MLA_DV = 128
MLA_Q_RANK = 768
MLA_KV_RANK = 512
MLA_HEAD_PAD = 256
ROPE_BASE = 10000.0
EPS = 1e-5
DEPTH = 1
DEEPNORM_ALPHA = (2.0 * DEPTH) ** 0.25
ADAM_LR = 0.001
ADAM_B1 = 0.9
ADAM_B2 = 0.999
ADAM_EPS = 1e-08
ADAM_WD = 0.01
ADAM_STEP = 10

N_DEV = 8
LANE = 128
SUBLANE = 8
VMEM_LIMIT = 48 * 1024 * 1024
ADAMW_TILE_ELEMS = 128 * 1024
PAIR_TILE_ELEMS = 1024 * 1024
DW_TK = 4096

MESH = pl.DeviceIdType.MESH
ANY = pl.BlockSpec(memory_space=pl.ANY)

NN = (((1,), (0,)), ((), ()))
NT = (((1,), (1,)), ((), ()))
TN = (((0,), (0,)), ((), ()))


def _dot(a, b, dims=NN):
    return lax.dot_general(a.astype(BF16), b.astype(BF16), dims, preferred_element_type=F32)


def _params(n_axes):
    return pltpu.CompilerParams(dimension_semantics=("arbitrary",) * n_axes, vmem_limit_bytes=VMEM_LIMIT)


def _tile(n, want):
    if n <= want:
        return n
    t = want
    while t >= LANE:
        if n % t == 0:
            return t
        t -= LANE
    return n


def _my_place():
    return lax.axis_index("x"), lax.axis_index("y"), lax.axis_index("c")


def _flat(px, py, pc):
    return 4 * px + 2 * py + pc


def _block_of(ref, kind, idx, rows, cols, sub=None):
    lo, n = (0, rows) if sub is None else sub
    if kind == "blk":
        return ref.at[idx, pl.ds(lo, n), :]
    if kind == "rows":
        return ref.at[pl.ds(pl.multiple_of(idx * rows + lo, n), n), :]
    return ref.at[pl.ds(lo, n), pl.ds(pl.multiple_of(idx * cols, cols), cols)]


CHIP_X, CHIP_Y, CHIP_DIAGONAL = 0, 1, 2
ALL_CHIPS = (CHIP_X, CHIP_Y, CHIP_DIAGONAL)


class _GatherJob:
    def __init__(self, shards, kinds, mid, late, pieces=(((0, 1), ALL_CHIPS, True),), base=None):
        nw = len(shards)
        self.nw, self.kinds = nw, list(kinds)
        self.pieces = [tuple(p) if len(p) == 4 else (*p, late) for p in pieces]
        for _, chips, _, _ in self.pieces:
            assert CHIP_DIAGONAL not in chips or len(chips) in (1, 3)
        self.shapes = [s.shape for s in shards]
        self.ins = list(shards) + (list(base) if base is not None else [])
        self.alias = [(nw + w, w) for w in range(nw)] if base is not None else []
        self.out_shape = []
        for s, kind in zip(shards, kinds):
            r, c = s.shape
            shape = {"blk": (N_DEV, r, c), "rows": (N_DEV * r, c), "cols": (r, N_DEV * c)}[kind]
            self.out_shape.append(jax.ShapeDtypeStruct(shape, s.dtype))
        n = 7 * nw * len(self.pieces)
        self.sems = [pltpu.SemaphoreType.DMA((n,)), pltpu.SemaphoreType.DMA((n,)), pltpu.SemaphoreType.DMA((n,))]
        whens = sorted({p[3] for p in self.pieces})
        self.phases = ([(0.0, self._start), (mid, self._relay)] + [(f, functools.partial(self._pass_on, f)) for f in whens]
                       + [(1.0, self._finish)])

    def _each(self, src, dst, send_sems, recv_sems, local_sems):
        x, y, c = _my_place()
        all_chips = [(1 - x, y), (x, 1 - y), (1 - x, 1 - y)]
        out = []
        for w in range(self.nw):
            r, cc = self.shapes[w]
            for p, ((k_of, of), chips, own, when) in enumerate(self.pieces):
                sub = (k_of * (r // of), r // of)
                s0 = 7 * (w * len(self.pieces) + p)
                mine = src[w].at[pl.ds(*sub), :]

                def place(dev, w=w, r=r, cc=cc, sub=sub):
                    return _block_of(dst[w], self.kinds[w], _flat(*dev), r, cc, sub)

                def copy(k, block, to, from_src=False, place=place, mine=mine, s0=s0):
                    return pltpu.make_async_remote_copy(
                        src_ref=mine if from_src else place(block), dst_ref=place(block),
                        send_sem=send_sems.at[s0 + k], recv_sem=recv_sems.at[s0 + k], device_id=to, device_id_type=MESH)

                relay = functools.partial(copy, 1 + CHIP_DIAGONAL, (x ^ (1 - c), y ^ c, c), (x ^ c, y ^ (1 - c), c))
                local = functools.partial(pltpu.make_async_copy, mine, place((x, y, c)), local_sems.at[s0])
                near = [(j, all_chips[j]) for j in chips if j != CHIP_DIAGONAL]
                out.append((near, CHIP_DIAGONAL in chips, own, copy, relay, local, when))
        return (x, y, c), (x, y, 1 - c), all_chips[CHIP_DIAGONAL], out

    def _start(self, *refs):
        me, sibling, diagonal, each = self._each(*refs)
        for near, far, own, copy, relay, local, when in each:
            if own:
                local().start()
                copy(0, me, sibling, from_src=True).start()
            for j, chip in near:
                copy(1 + j, me, (*chip, me[2]), from_src=True).start()
            if far and not near:
                relay().start()

    def _landed(self, near, copy, me, sibling):
        for j, chip in near:
            copy(1 + j, (*chip, me[2]), me).wait_recv()
            copy(4 + j, (*chip, me[2]), sibling).start()

    def _relay(self, *refs):
        me, sibling, diagonal, each = self._each(*refs)
        for near, far, own, copy, relay, local, when in each:
            if far and near:
                self._landed(near, copy, me, sibling)
                relay().start()

    def _pass_on(self, frac, *refs):
        me, sibling, diagonal, each = self._each(*refs)
        for near, far, own, copy, relay, local, when in each:
            if when == frac:
                self._landed([(CHIP_DIAGONAL, diagonal)] if far else near, copy, me, sibling)

    def _finish(self, *refs):
        me, sibling, diagonal, each = self._each(*refs)
        for near, far, own, copy, relay, local, when in each:
            if own:
                copy(0, sibling, me).wait_recv()
            for j, chip in near + [(CHIP_DIAGONAL, diagonal)] * int(far):
                copy(4 + j, (*chip, 1 - me[2]), me).wait_recv()
        for near, far, own, copy, relay, local, when in each:
            if own:
                copy(0, me, sibling, from_src=True).wait_send()
                local().wait()
            for j, chip in near:
                copy(1 + j, me, (*chip, me[2]), from_src=True).wait_send()
                copy(4 + j, (*chip, me[2]), sibling).wait_send()
            if far:
                relay().wait_send()
                copy(4 + CHIP_DIAGONAL, (*diagonal, me[2]), sibling).wait_send()


REL_ALL = [(1, 0, 0), (1, 0, 1), (0, 1, 0), (0, 1, 1), (1, 1, 0), (1, 1, 1), (0, 0, 1)]
N_CHIP = 4


class _PairJob:
    def __init__(self, grads, kinds, shard_shapes):
        self.ins = list(grads)
        self.kinds, self.shard_shapes, self.nw = list(kinds), list(shard_shapes), len(grads)
        self.out_shape = [jax.ShapeDtypeStruct((N_CHIP,) + tuple(s), g.dtype) for g, s in zip(grads, shard_shapes)]
        n = N_CHIP * self.nw
        self.sems = [pltpu.SemaphoreType.DMA((n,)), pltpu.SemaphoreType.DMA((n,))]
        self.phases = [(0.0, self._start), (1.0, self._finish)]

    def _copies(self, src, theirs, send_sems, recv_sems):
        x, y, c = _my_place()
        remote = []
        for w in range(self.nw):
            r, cc = self.shard_shapes[w]
            for q in range(N_CHIP):
                s = N_CHIP * w + q
                remote.append(pltpu.make_async_remote_copy(
                    src_ref=_block_of(src[w], self.kinds[w], 2 * q + (1 - c), r, cc), dst_ref=theirs[w].at[q],
                    send_sem=send_sems.at[s], recv_sem=recv_sems.at[s], device_id=(x, y, 1 - c), device_id_type=MESH))
        return remote

    def _start(self, *refs):
        for cp in self._copies(*refs):
            cp.start()

    def _finish(self, *refs):
        for cp in self._copies(*refs):
            cp.wait()


CHIP_RELS = [(1, 0), (0, 1), (1, 1)]


class _ChipJob:
    def __init__(self, chip_sums, shard_shapes, rels=CHIP_RELS, by_source=None, piece=(0, 1), base=None):
        nw = len(chip_sums)
        self.ins = list(chip_sums) + ([by_source] if by_source is not None else []) + (list(base) if base is not None else [])
        self.nw, self.has_src, self.rels, self.piece = nw, by_source is not None, list(rels), piece
        self.rows = [s[0] for s in shard_shapes]
        self.alias = [(nw + int(self.has_src) + w, w) for w in range(nw)] if base is not None else []
        self.out_shape = [jax.ShapeDtypeStruct((len(rels),) + tuple(s), a.dtype) for a, s in zip(chip_sums, shard_shapes)]
        if self.has_src:
            self.out_shape.append(jax.ShapeDtypeStruct((N_DEV,) + by_source.shape, by_source.dtype))
        n = len(rels) * self.nw + len(REL_ALL) * int(self.has_src)
        self.sems = [pltpu.SemaphoreType.DMA((n,)), pltpu.SemaphoreType.DMA((n,)), pltpu.SemaphoreType.DMA((1,))]
        self.phases = [(0.0, self._start), (1.0, self._finish)]

    def _copies(self, src, land, send_sems, recv_sems, local_sems):
        x, y, c = _my_place()
        nw = self.nw

        def remote_copy(s, src_ref, dst_ref, peer):
            return pltpu.make_async_remote_copy(src_ref=src_ref, dst_ref=dst_ref, send_sem=send_sems.at[s],
                                                recv_sem=recv_sems.at[s], device_id=peer, device_id_type=MESH)

        def rows(w):
            n = self.rows[w] // self.piece[1]
            return pl.ds(self.piece[0] * n, n)

        local, remote = [], []
        for k, (fx, fy) in enumerate(self.rels):
            px, py = x ^ fx, y ^ fy
            remote += [remote_copy(nw * k + w, src[w].at[2 * px + py, rows(w), :], land[w].at[k, rows(w), :], (px, py, c))
                       for w in range(nw)]
        if self.has_src:
            my_idx = _flat(x, y, c)
            local.append(pltpu.make_async_copy(src[nw], land[nw].at[my_idx], local_sems.at[0]))
            remote += [remote_copy(nw * len(self.rels) + k, src[nw], land[nw].at[my_idx], (x ^ fx, y ^ fy, c ^ fc))
                       for k, (fx, fy, fc) in enumerate(REL_ALL)]
        return local, remote

    def _start(self, *refs):
        local, remote = self._copies(*refs)
        for cp in local + remote:
            cp.start()

    def _finish(self, *refs):
        local, remote = self._copies(*refs)
        for cp in remote + local:
            cp.wait()


def _row_tile(R, C, elems=ADAMW_TILE_ELEMS):
    tr = 16
    while tr * 2 * C <= elems and R % (tr * 2) == 0:
        tr *= 2
    return tr if R % tr == 0 else R


BF16_ROWS = 16


def _tile_2d(R, C, elems):
    if R % BF16_ROWS == 0:
        tr = _row_tile(R, C, elems)
        return tr, C, R // tr
    tc = C
    while tc % (2 * LANE) == 0 and R * tc > elems:
        tc //= 2
    return R, tc, C // tc


def _tile_index(tr, tc, C):
    return (lambda i: (i, 0)) if tc == C else (lambda i: (0, i))


def _part_spec(kind, tr, tc, R, C, n_lead):
    at = _tile_index(tr, tc, C)

    def dev(ids, pref):
        return 2 * ids[0] + pref[0] if n_lead == 2 else pref[0]

    def index(*args):
        ids, pref = args[:n_lead], args[n_lead]
        ri, ci = at(ids[-1])
        if kind == "blk":
            return dev(ids, pref), ri, ci
        if kind == "rows":
            return dev(ids, pref) * (R // tr) + ri, ci
        return ri, dev(ids, pref) * (C // tc) + ci

    return pl.BlockSpec((None, tr, tc) if kind == "blk" else (tr, tc), index)


def _pair_sum(name, grad, kind, shard_shape, theirs, c):
    R, C = shard_shape
    tr, tc, tiles = _tile_2d(R, C, PAIR_TILE_ELEMS)
    at = _tile_index(tr, tc, C)

    def body(c_ref, g_ref, t_ref, o_ref):
        o_ref[...] = (g_ref[...].astype(F32) + t_ref[...].astype(F32)).astype(o_ref.dtype)

    slot = pl.BlockSpec((None, tr, tc), lambda q, i, c_ref: (q, *at(i)))
    return pl.pallas_call(
        body, name=name,
        grid_spec=pltpu.PrefetchScalarGridSpec(
            num_scalar_prefetch=1, grid=(N_CHIP, tiles),
            in_specs=[_part_spec(kind, tr, tc, R, C, 2), slot], out_specs=slot),
        out_shape=jax.ShapeDtypeStruct(theirs.shape, theirs.dtype), compiler_params=_params(2))(c, grad, theirs)


class _Carried:
    def __init__(self, jobs, grid):
        self.jobs = list(jobs)
        self.steps = int(np.prod(grid))
        self.grid = tuple(grid)
        self.operands = [a for j in self.jobs for a in j.ins]
        self.out_shape = [o for j in self.jobs for o in j.out_shape]
        self.sems = [s for j in self.jobs for s in j.sems]
        self.in_specs = [ANY] * len(self.operands)
        self.out_specs = [ANY] * len(self.out_shape)

    def aliases(self, n_in, n_out):
        out, i0, o0 = {}, n_in, n_out
        for job in self.jobs:
            for ji, jo in getattr(job, "alias", []):
                out[i0 + ji] = o0 + jo
            i0, o0 = i0 + len(job.ins), o0 + len(job.out_shape)
        return out

    def _step(self):
        lin = pl.program_id(0)
        for a in range(1, len(self.grid)):
            lin = lin * self.grid[a] + pl.program_id(a)
        return lin

    def _run(self, in_refs, out_refs, sem_refs, last):
        lin = self._step()
        for job in self.jobs:
            ins, in_refs = in_refs[:len(job.ins)], in_refs[len(job.ins):]
            outs, out_refs = out_refs[:len(job.out_shape)], out_refs[len(job.out_shape):]
            sems, sem_refs = sem_refs[:len(job.sems)], sem_refs[len(job.sems):]
            for frac, fn in job.phases:
                if (frac >= 1.0) != last:
                    continue
                at = min(self.steps - 1, int(frac * (self.steps - 1)))
                pl.when(lin == at)(functools.partial(fn, ins, outs, *sems))

    def before(self, in_refs, out_refs, sem_refs):
        self._run(in_refs, out_refs, sem_refs, last=False)

    def after(self, in_refs, out_refs, sem_refs):
        self._run(in_refs, out_refs, sem_refs, last=True)


def _split_refs(refs, n_in, n_job_in, n_out, n_job_out, n_scratch):
    cuts = np.cumsum([0, n_in, n_job_in, n_out, n_job_out, n_scratch])
    return [refs[cuts[i]:cuts[i + 1]] for i in range(5)] + [refs[cuts[5]:]]


def _matmul(name, a, b, *, ta=False, tb=False, out_dtypes=(F32,), epilogue=None, extras=(), tm=1024, tn=1024, tk=2048, jobs=(),
            n=None, row_extras=(), out_widths=None):
    if ta:
        K, M = a.shape
    else:
        M, K = a.shape
    N = n if n is not None else (b.shape[0] if tb else b.shape[1])
    if ta:
        tk = max(tk, DW_TK)
    tm, tn, tk = _tile(M, tm), _tile(N, tn), _tile(K, tk)
    nk = K // tk
    grid = (M // tm, N // tn, nk)
    dims = TN if ta else (NT if tb else NN)
    n_ex, n_out = len(extras) + len(row_extras), len(out_dtypes)
    widths = list(out_widths) if out_widths is not None else [tn] * n_out
    carried = _Carried(jobs, grid)

    def body(*refs):
        ins, job_ins, out_refs, job_outs, scratch, job_sems = _split_refs(
            refs, 2 + n_ex, len(carried.operands), n_out, len(carried.out_shape), int(nk > 1))
        a_ref, b_ref, ex_refs = ins[0], ins[1], ins[2:]
        k = pl.program_id(2)
        carried.before(job_ins, job_outs, job_sems)

        def finish(acc):
            res = (acc,) if epilogue is None else epilogue(acc, *[e[...] for e in ex_refs])
            for o_ref, r in zip(out_refs, res):
                o_ref[...] = r.astype(o_ref.dtype)

        if nk == 1:
            finish(_dot(a_ref[...], b_ref[...], dims))
        else:
            acc_ref = scratch[0]

            @pl.when(k == 0)
            def _():
                acc_ref[...] = _dot(a_ref[...], b_ref[...], dims)

            @pl.when(k > 0)
            def _():
                acc_ref[...] += _dot(a_ref[...], b_ref[...], dims)

            pl.when(k == nk - 1)(lambda: finish(acc_ref[...]))

        carried.after(job_ins, job_outs, job_sems)

    a_spec = pl.BlockSpec((tk, tm), lambda i, j, k: (k, i)) if ta else pl.BlockSpec((tm, tk), lambda i, j, k: (i, k))
    b_spec = pl.BlockSpec((tn, tk), lambda i, j, k: (j, k)) if tb else pl.BlockSpec((tk, tn), lambda i, j, k: (k, j))
    tile_spec = pl.BlockSpec((tm, tn), lambda i, j, k: (i, j))
    row_specs = [pl.BlockSpec((tm, r.shape[1]), lambda i, j, k: (i, 0)) for r in row_extras]
    outs = pl.pallas_call(
        body,
        name=name,
        grid=grid,
        in_specs=[a_spec, b_spec] + [tile_spec] * len(extras) + row_specs + carried.in_specs,
        out_specs=[pl.BlockSpec((tm, w), lambda i, j, k: (i, j)) for w in widths] + carried.out_specs,
        out_shape=[jax.ShapeDtypeStruct((M, N // tn * w), d) for d, w in zip(out_dtypes, widths)] + carried.out_shape,
        scratch_shapes=[pltpu.VMEM((tm, tn), F32)] * int(nk > 1) + carried.sems,
        input_output_aliases=carried.aliases(2 + n_ex, n_out),
        compiler_params=_params(3),
    )(a, b, *extras, *row_extras, *carried.operands)
    return outs[0] if len(outs) == 1 else outs


def _rowwise(name, fn, row_in, const_in, row_out, acc_out, tm, jobs=()):
    T = row_in[0][0].shape[0]
    n_r, n_c, n_o, n_a = len(row_in), len(const_in), len(row_out), len(acc_out)
    carried = _Carried(jobs, (T // tm,))

    def body(*refs):
        in_refs, job_ins, own_outs, job_outs, _, job_sems = _split_refs(
            refs, n_r + n_c, len(carried.operands), n_o + n_a, len(carried.out_shape), 0)
        out_refs, acc_refs = own_outs[:n_o], own_outs[n_o:]
        carried.before(job_ins, job_outs, job_sems)
        res = fn(*[r[...] for r in in_refs])
        for o_ref, r in zip(out_refs, res[:n_o]):
            o_ref[...] = r.astype(o_ref.dtype)

        @pl.when(pl.program_id(0) == 0)
        def _():
            for a_ref in acc_refs:
                a_ref[...] = jnp.zeros_like(a_ref)

        for a_ref, r in zip(acc_refs, res[n_o:]):
            a_ref[...] += jnp.sum(r.reshape(tm // SUBLANE, SUBLANE, r.shape[-1]), axis=0)
        carried.after(job_ins, job_outs, job_sems)

    in_specs = [pl.BlockSpec((tm, w), functools.partial(lambda i, c: (i, c), c=cb)) for (_, w, cb) in row_in]
    in_specs += [pl.BlockSpec(c.shape, functools.partial(lambda i, nd: (0,) * nd, nd=c.ndim)) for c in const_in]
    out_specs = [pl.BlockSpec((tm, c), lambda i: (i, 0)) for (c, _) in row_out]
    out_specs += [pl.BlockSpec((SUBLANE, c), lambda i: (0, 0)) for c in acc_out]
    out_shape = [jax.ShapeDtypeStruct((T, c), d) for (c, d) in row_out]
    out_shape += [jax.ShapeDtypeStruct((SUBLANE, c), F32) for c in acc_out]
    return pl.pallas_call(
        body, name=name, grid=(T // tm,), in_specs=in_specs + carried.in_specs, out_specs=out_specs + carried.out_specs,
        out_shape=out_shape + carried.out_shape, scratch_shapes=carried.sems,
        input_output_aliases=carried.aliases(n_r + n_c, n_o + n_a), compiler_params=_params(1),
    )(*[r[0] for r in row_in], *const_in, *carried.operands)


def _ln_stats(z):
    mu = jnp.mean(z, axis=-1, keepdims=True)
    var = jnp.mean(jnp.square(z - mu), axis=-1, keepdims=True)
    rstd = lax.rsqrt(var + EPS)
    return (z - mu) * rstd, rstd


def _ln_bwd(dy, xhat, rstd, g):
    dxh = dy * g
    return rstd * (dxh - jnp.mean(dxh, axis=-1, keepdims=True) - xhat * jnp.mean(dxh * xhat, axis=-1, keepdims=True))


def _rms(x, g):
    return x * lax.rsqrt(jnp.mean(jnp.square(x), axis=-1, keepdims=True) + EPS) * g


def _rope128(x, cos2, sin2):
    return x * cos2 + pltpu.roll(x, RET_DK // 2, 1) * sin2


def _unrope128(dy, cos2, sin2):
    return dy * cos2 + pltpu.roll(dy * sin2, RET_DK // 2, 1)


def _rope64(x, cosp, sina, sinb):
    h = MLA_ROPE // 2
    return x * cosp + pltpu.roll(x, h, 1) * sina + pltpu.roll(x, LANE - h, 1) * sinb


def _unrope64(dy, cosp, sina, sinb):
    h = MLA_ROPE // 2
    return dy * cosp + pltpu.roll(dy * sina, LANE - h, 1) + pltpu.roll(dy * sinb, h, 1)


RET_BLOCK = 256
RET_HEADS_PER_STEP = 4


def _ret_tables(H, B):
    log_g = np.log1p(-np.exp2(-5.0 - np.arange(H, dtype=np.float32))).astype(np.float32)
    idx = np.arange(B, dtype=np.float32)
    dist = np.abs(idx[:, None] - idx[None, :])
    visible = (np.arange(B)[None, :] // CHUNK) <= (np.arange(B)[:, None] // CHUNK)
    dmat = np.exp(log_g[:, None, None] * dist).astype(np.float32) * visible[None].astype(np.float32)
    qd = np.exp(log_g[:, None] * (idx + 1.0)).astype(np.float32)
    kd = np.exp(log_g[:, None] * (B - 1 - idx)).astype(np.float32)
    sd = np.exp(log_g * B).astype(np.float32)
    bc = lambda v: np.ascontiguousarray(np.broadcast_to(v[:, :, None], (H, v.shape[1], LANE)))
    sdb = np.ascontiguousarray(np.broadcast_to(sd[:, None, None], (H, SUBLANE, LANE)))
    return jnp.asarray(dmat), jnp.asarray(bc(qd)), jnp.asarray(bc(kd)), jnp.asarray(sdb)


def _ret_specs(H, B, reverse, nb):
    blk = (lambda b: nb - 1 - b) if reverse else (lambda b: b)
    G = min(RET_HEADS_PER_STEP, H)
    W = G * LANE
    col = lambda off: pl.BlockSpec((B, W), functools.partial(lambda h, b, off: (blk(b), off // G + h), off=off))
    row = pl.BlockSpec((B, LANE), lambda h, b: (blk(b), 0))
    per_head = lambda r: pl.BlockSpec((G, r, LANE), lambda h, b: (h, 0, 0))
    dmat = pl.BlockSpec((G, B, B), lambda h, b: (h, 0, 0))
    return G, col, row, per_head, dmat


def _ret_call(name, body, n_in, n_out, operands, in_specs, out_specs, out_shape, grid, G, jobs, own_alias=None):
    carried = _Carried(jobs, grid)

    def wrapped(*refs):
        ins, job_ins, outs, job_outs, (state,), job_sems = _split_refs(
            refs, n_in, len(carried.operands), n_out, len(carried.out_shape), 1)
        carried.before(job_ins, job_outs, job_sems)

        @pl.when(pl.program_id(1) == 0)
        def _():
            state[...] = jnp.zeros_like(state)

        body(*ins, *outs, state)
        carried.after(job_ins, job_outs, job_sems)

    return pl.pallas_call(
        wrapped, name=name, grid=grid,
        in_specs=in_specs + carried.in_specs, out_specs=out_specs + carried.out_specs,
        out_shape=out_shape + carried.out_shape,
        scratch_shapes=[pltpu.VMEM((G, RET_DK, RET_DV), F32)] + carried.sems,
        input_output_aliases={**(own_alias or {}), **carried.aliases(n_in, n_out)},
        compiler_params=_params(2),
    )(*operands, *carried.operands)


def _retention_fwd(p1, cos2, sin2, gn_g, mix_in, H, jobs=()):
    T = p1.shape[0]
    B = min(RET_BLOCK, T)
    nb = T // B
    scale = RET_DK ** -0.5
    dmat, qd, kd, sd = _ret_tables(H, B)
    G, col, row, per_head, dspec = _ret_specs(H, B, False, nb)

    def body(q_ref, k_ref, v_ref, g_ref, cos_ref, sin_ref, d_ref, qd_ref, kd_ref, sd_ref, gn_ref, _, pre_ref, out_ref, s_ref):
        cos, sin = cos_ref[...], sin_ref[...]
        for g in range(G):
            hl = pl.ds(g * LANE, LANE)
            q = _rope128(q_ref[:, hl], cos, sin)
            k = _rope128(k_ref[:, hl], cos, sin) * scale
            v = v_ref[:, hl]
            s = s_ref[g]
            a = _dot(q, k, NT) * d_ref[g]
            ret = _dot(a, v) + _dot(q * qd_ref[g], s)
            s_ref[g] = sd_ref[g, 0:1, :] * s + _dot(k * kd_ref[g], v, TN)
            pre_ref[:, hl] = ret
            nrm, _ = _ln_stats(ret)
            out_ref[:, hl] = (jax.nn.silu(g_ref[:, hl]) * (nrm * gn_ref[:, hl])).astype(out_ref.dtype)

    out_spec = pl.BlockSpec((B, G * LANE), lambda h, b: (b, h))
    return _ret_call(
        "retention_fwd", body, 12, 2, (p1, p1, p1, p1, cos2, sin2, dmat, qd, kd, sd, gn_g, mix_in),
        [col(0), col(H), col(2 * H), col(3 * H), row, row, dspec, per_head(B), per_head(B), per_head(SUBLANE),
         pl.BlockSpec((1, G * LANE), lambda h, b: (0, h)), ANY],
        [out_spec, out_spec],
        [jax.ShapeDtypeStruct((T, H * RET_DV), F32), jax.ShapeDtypeStruct(mix_in.shape, mix_in.dtype)],
        (H // G, nb), G, jobs, own_alias={11: 1})


def _retention_bwd(p1, dpre, cos2, sin2, H, dq_jobs=(), dkv_jobs=()):
    T = p1.shape[0]
    B = min(RET_BLOCK, T)
    nb = T // B
    scale = RET_DK ** -0.5
    dmat, qd, kd, sd = _ret_tables(H, B)

    def dq_body(k_ref, v_ref, do_ref, cos_ref, sin_ref, d_ref, qd_ref, kd_ref, sd_ref, dq_ref, s_ref):
        cos, sin = cos_ref[...], sin_ref[...]
        for g in range(G):
            hl = pl.ds(g * LANE, LANE)
            k = _rope128(k_ref[:, hl], cos, sin) * scale
            v, do, s = v_ref[:, hl], do_ref[:, hl], s_ref[g]
            da = _dot(do, v, NT) * d_ref[g]
            dq = _dot(da, k) + _dot(do, s, NT) * qd_ref[g]
            dq_ref[:, hl] = _unrope128(dq, cos, sin).astype(dq_ref.dtype)
            s_ref[g] = sd_ref[g, 0:1, :] * s + _dot(k * kd_ref[g], v, TN)

    G, col, row, per_head, dspec = _ret_specs(H, B, False, nb)
    dq = _ret_call(
        "retention_bwd_dq", dq_body, 9, 1, (p1, p1, dpre, cos2, sin2, dmat, qd, kd, sd),
        [col(H), col(2 * H), col(0), row, row, dspec, per_head(B), per_head(B), per_head(SUBLANE)],
        [pl.BlockSpec((B, G * LANE), lambda h, b: (b, h))], [jax.ShapeDtypeStruct((T, H * RET_DK), BF16)],
        (H // G, nb), G, dq_jobs)

    def dkv_body(q_ref, k_ref, v_ref, do_ref, cos_ref, sin_ref, d_ref, qd_ref, kd_ref, sd_ref, dk_ref, dv_ref, g_ref):
        cos, sin = cos_ref[...], sin_ref[...]
        for g in range(G):
            hl = pl.ds(g * LANE, LANE)
            q = _rope128(q_ref[:, hl], cos, sin)
            k = _rope128(k_ref[:, hl], cos, sin) * scale
            v, do, st = v_ref[:, hl], do_ref[:, hl], g_ref[g]
            a = _dot(q, k, NT) * d_ref[g]
            da = _dot(do, v, NT) * d_ref[g]
            dv = _dot(a, do, TN) + _dot(k * kd_ref[g], st)
            dk = _dot(da, q, TN) + _dot(v, st, NT) * kd_ref[g]
            dv_ref[:, hl] = dv.astype(dv_ref.dtype)
            dk_ref[:, hl] = _unrope128(dk * scale, cos, sin).astype(dk_ref.dtype)
            g_ref[g] = sd_ref[g, 0:1, :] * st + _dot(q * qd_ref[g], do, TN)

    G, col, row, per_head, dspec = _ret_specs(H, B, True, nb)
    out_spec = pl.BlockSpec((B, G * LANE), lambda h, b: (nb - 1 - b, h))
    dkv = _ret_call(
        "retention_bwd_dkv", dkv_body, 10, 2, (p1, p1, p1, dpre, cos2, sin2, dmat, qd, kd, sd),
        [col(0), col(H), col(2 * H), col(0), row, row, dspec, per_head(B), per_head(B), per_head(SUBLANE)],
        [out_spec, out_spec],
        [jax.ShapeDtypeStruct((T, H * RET_DK), BF16), jax.ShapeDtypeStruct((T, H * RET_DV), BF16)],
        (H // G, nb), G, dkv_jobs)
    return dq, dkv


ATT_BLOCK = 512


ATT_ROW_CHUNKS = 2
ATT_BWD_ROW_CHUNKS = 1
ATT_HEADS_PER_STEP = 8
ATT_BWD_HEADS_PER_STEP = 2
LOG2E = 1.4426950408889634


def _chunk_mask(rows, row0, cols):
    qi = (lax.broadcasted_iota(jnp.int32, (rows, cols), 0) + row0) // CHUNK
    kj = lax.broadcasted_iota(jnp.int32, (rows, cols), 1) // CHUNK
    return kj <= qi


def _attention_fwd(qh, kh, vx, H, jobs=()):
    T = qh.shape[0]
    t = min(ATT_BLOCK, T)
    n = T // t
    r = t // ATT_ROW_CHUNKS
    G = min(ATT_HEADS_PER_STEP, H)
    P = MLA_HEAD_PAD
    scale = (MLA_NOPE + MLA_ROPE) ** -0.5
    c2 = scale * LOG2E
    carried = _Carried(jobs, (H // G, n, n))

    def body(*refs):
        (q_ref, k_ref, v_ref), job_ins, (o_ref, ob_ref, lse_ref), job_outs, (m_ref, acc_ref), job_sems = _split_refs(
            refs, 3, len(carried.operands), 3, len(carried.out_shape), 2)
        i, j = pl.program_id(1), pl.program_id(2)
        carried.before(job_ins, job_outs, job_sems)

        @pl.when(j == 0)
        def _():
            m_ref[...] = jnp.full_like(m_ref, -jnp.inf)
            acc_ref[...] = jnp.zeros_like(acc_ref)

        def update(masked):
            for g in range(G):
                hc = pl.ds(g * P, P)
                for ch in range(ATT_ROW_CHUNKS):
                    rs = pl.ds(ch * r, r)
                    kc = (ch + 1) * r if masked else t
                    ks = pl.ds(0, kc)
                    s = _dot(q_ref[rs, hc], k_ref[ks, hc], NT)
                    if masked:
                        s = jnp.where(_chunk_mask(r, ch * r, kc), s, -jnp.inf)
                    m_prev = m_ref[rs, pl.ds(g * LANE, LANE)]
                    m_new = jnp.maximum(m_prev, jnp.max(s, axis=-1, keepdims=True))
                    alpha = jnp.exp2((m_prev - m_new) * c2)
                    p = jnp.exp2((s - jnp.tile(m_new, (1, kc // LANE))) * c2)
                    acc_ref[rs, hc] = jnp.tile(alpha, (1, 2)) * acc_ref[rs, hc] + _dot(p, v_ref[ks, hc])
                    m_ref[rs, pl.ds(g * LANE, LANE)] = m_new

        pl.when(j < i)(functools.partial(update, False))
        pl.when(j == i)(functools.partial(update, True))

        @pl.when(j == n - 1)
        def _():
            for g in range(G):
                acc = acc_ref[:, pl.ds(g * P, P)]
                l = acc[:, MLA_DV:]
                o = acc[:, :MLA_DV] / l
                o_ref[:, pl.ds(g * MLA_DV, MLA_DV)] = o
                ob_ref[:, pl.ds(g * MLA_DV, MLA_DV)] = o.astype(ob_ref.dtype)
                lse_ref[g] = m_ref[:, pl.ds(g * LANE, LANE)] * scale + jnp.log(l)

        carried.after(job_ins, job_outs, job_sems)

    kclamp = lambda i, j: jnp.minimum(j, i)
    o_spec = pl.BlockSpec((t, G * MLA_DV), lambda h, i, j: (i, h))
    mix_spec = pl.BlockSpec((t, G * MLA_DV), lambda h, i, j: (i, H // G + h))
    return pl.pallas_call(
        body, name="attention_fwd", grid=(H // G, n, n),
        in_specs=[pl.BlockSpec((t, G * P), lambda h, i, j: (i, h)),
                  pl.BlockSpec((t, G * P), lambda h, i, j: (kclamp(i, j), h)),
                  pl.BlockSpec((t, G * P), lambda h, i, j: (kclamp(i, j), h))] + carried.in_specs,
        out_specs=[o_spec, mix_spec, pl.BlockSpec((G, t, LANE), lambda h, i, j: (h, i, 0))] + carried.out_specs,
        out_shape=[jax.ShapeDtypeStruct((T, H * MLA_DV), F32), jax.ShapeDtypeStruct((T, 2 * H * MLA_DV), BF16),
                   jax.ShapeDtypeStruct((H, T, LANE), F32)] + carried.out_shape,
        scratch_shapes=[pltpu.VMEM((t, G * LANE), F32), pltpu.VMEM((t, G * P), F32)] + carried.sems,
        input_output_aliases=carried.aliases(3, 3),
        compiler_params=_params(3),
    )(qh, kh, vx, *carried.operands)


def _attention_bwd(qh, kh, vx, o, lse, dmix, H, jobs=()):
    T = qh.shape[0]
    t = min(ATT_BLOCK, T)
    n = T // t
    r = t // ATT_BWD_ROW_CHUNKS
    G = min(ATT_BWD_HEADS_PER_STEP, H)
    P = MLA_HEAD_PAD
    scale = (MLA_NOPE + MLA_ROPE) ** -0.5
    c2 = scale * LOG2E
    carried = _Carried(jobs, (H // G, n, n))

    def body(*refs):
        ((q_ref, k_ref, v_ref, do_ref, o_ref, lse_ref), job_ins, (dq_ref, dkn_ref, dkr_ref, dv_ref), job_outs,
         (dk_acc, dv_acc), job_sems) = _split_refs(refs, 6, len(carried.operands), 4, len(carried.out_shape), 2)
        j, i = pl.program_id(1), pl.program_id(2)
        carried.before(job_ins, job_outs, job_sems)

        @pl.when(jnp.logical_and(j == 0, i == 0))
        def _():
            dq_ref[...] = jnp.zeros_like(dq_ref)

        @pl.when(i == j)
        def _():
            dk_acc[...] = jnp.zeros_like(dk_acc)
            dv_acc[...] = jnp.zeros_like(dv_acc)

        def update(masked):
            for g in range(G):
                hc, hv = pl.ds(g * P, P), pl.ds(g * MLA_DV, MLA_DV)
                for ch in range(ATT_BWD_ROW_CHUNKS):
                    rs = pl.ds(ch * r, r)
                    kc = (ch + 1) * r if masked else t
                    ks = pl.ds(0, kc)
                    k, v = k_ref[ks, hc], v_ref[ks, pl.ds(g * P, MLA_DV)]
                    q, do = q_ref[rs, hc], do_ref[rs, hv]
                    s = _dot(q, k, NT)
                    if masked:
                        s = jnp.where(_chunk_mask(r, ch * r, kc), s, -jnp.inf)
                    p = jnp.exp2(s * c2 - jnp.tile(lse_ref[g, rs, :] * LOG2E, (1, kc // LANE)))
                    dv_acc[ks, hv] += _dot(p, do, TN)
                    dp = _dot(do, v, NT)
                    delta = jnp.sum(do * o_ref[rs, hv], axis=-1, keepdims=True)
                    ds = p * (dp - delta)
                    rows = pl.ds(pl.multiple_of(i * t, t) + ch * r, r)
                    dq_ref[rows, hc] += _dot(ds, k)
                    dk_acc[ks, hc] += _dot(ds, q, TN)

        pl.when(i > j)(functools.partial(update, False))
        pl.when(i == j)(functools.partial(update, True))

        @pl.when(i == n - 1)
        def _():
            for g in range(G):
                dk = dk_acc[:, pl.ds(g * P, P)] * scale
                dkn_ref[:, pl.ds(g * LANE, LANE)] = dk[:, :MLA_NOPE].astype(dkn_ref.dtype)
                dkr_ref[:, pl.ds(g * LANE, LANE)] = dk[:, MLA_NOPE:]
            dv_ref[...] = dv_acc[...].astype(dv_ref.dtype)

        carried.after(job_ins, job_outs, job_sems)

    qclamp = lambda j, i: jnp.maximum(i, j)
    kv_out = pl.BlockSpec((t, G * LANE), lambda h, j, i: (j, h))
    q_in = pl.BlockSpec((t, G * MLA_DV), lambda h, j, i: (qclamp(j, i), h))
    return pl.pallas_call(
        body, name="attention_bwd", grid=(H // G, n, n),
        in_specs=[pl.BlockSpec((t, G * P), lambda h, j, i: (qclamp(j, i), h)),
                  pl.BlockSpec((t, G * P), lambda h, j, i: (j, h)),
                  pl.BlockSpec((t, G * P), lambda h, j, i: (j, h)),
                  pl.BlockSpec((t, G * MLA_DV), lambda h, j, i: (qclamp(j, i), H // G + h)),
                  q_in,
                  pl.BlockSpec((G, t, LANE), lambda h, j, i: (h, qclamp(j, i), 0))] + carried.in_specs,
        out_specs=[pl.BlockSpec((T, G * P), lambda h, j, i: (0, h)), kv_out, kv_out, kv_out] + carried.out_specs,
        out_shape=[jax.ShapeDtypeStruct((T, H * P), F32), jax.ShapeDtypeStruct((T, H * LANE), BF16),
                   jax.ShapeDtypeStruct((T, H * LANE), F32), jax.ShapeDtypeStruct((T, H * LANE), BF16)] + carried.out_shape,
        scratch_shapes=[pltpu.VMEM((t, G * P), F32), pltpu.VMEM((t, G * MLA_DV), F32)] + carried.sems,
        input_output_aliases=carried.aliases(6, 4),
        compiler_params=_params(3),
    )(qh, kh, vx, dmix, o, lse, *carried.operands)


def _adamw_math(w, g, m, v):
    m = ADAM_B1 * m + (1.0 - ADAM_B1) * g
    v = ADAM_B2 * v + (1.0 - ADAM_B2) * jnp.square(g)
    m_hat = m / (1.0 - ADAM_B1 ** ADAM_STEP)
    v_hat = v / (1.0 - ADAM_B2 ** ADAM_STEP)
    delta = -ADAM_LR * (m_hat / (jnp.sqrt(v_hat) + ADAM_EPS) + ADAM_WD * w)
    return delta, m, v


def _adamw(name, chip_sums, my_chip, landed, w, m, v, jobs=()):
    R, C = w.shape
    tr, tc, tiles = _tile_2d(R, C, ADAMW_TILE_ELEMS)
    at = _tile_index(tr, tc, C)
    nl = len(landed)
    assert sum(l.shape[0] for l in landed) == N_CHIP - 1
    carried = _Carried(jobs, (tiles,))

    def body(q_ref, *refs):
        (own_ref, *rest), job_ins, (g_out, d_out, m_out, v_out), job_outs, _, job_sems = _split_refs(
            refs, nl + 4, len(carried.operands), 4, len(carried.out_shape), 0)
        l_refs, (w_ref, m_ref, v_ref) = rest[:nl], rest[nl:]
        carried.before(job_ins, job_outs, job_sems)
        g = own_ref[...].astype(F32)
        for l_ref in l_refs:
            for s in range(l_ref.shape[0]):
                g = g + l_ref[s].astype(F32)
        delta, mn, vn = _adamw_math(w_ref[...], g, m_ref[...], v_ref[...])
        g_out[...] = g
        d_out[...] = delta
        m_out[...] = mn
        v_out[...] = vn
        carried.after(job_ins, job_outs, job_sems)

    spec = pl.BlockSpec((tr, tc), lambda i, q_ref: at(i))
    own_spec = pl.BlockSpec((None, tr, tc), lambda i, q_ref: (q_ref[0], *at(i)))
    l_specs = [pl.BlockSpec((l.shape[0], tr, tc), lambda i, q_ref: (0, *at(i))) for l in landed]
    return pl.pallas_call(
        body, name=name,
        grid_spec=pltpu.PrefetchScalarGridSpec(
            num_scalar_prefetch=1, grid=(tiles,),
            in_specs=[own_spec] + l_specs + [spec, spec, spec] + carried.in_specs,
            out_specs=[spec] * 4 + carried.out_specs, scratch_shapes=carried.sems),
        out_shape=[jax.ShapeDtypeStruct((R, C), F32)] * 4 + carried.out_shape,
        input_output_aliases={1 + i: o for i, o in carried.aliases(nl + 4, 4).items()},
        compiler_params=_params(1),
    )(my_chip, chip_sums, *landed, w, m, v, *carried.operands)


def _adamw_small(landed, w, m, v):
    P = w.shape[1]

    def body(l_ref, w_ref, m_ref, v_ref, g_out, d_out, m_out, v_out, loss_out):
        acc = l_ref[0]
        for s in range(1, N_DEV):
            acc = acc + l_ref[s]
        total = jnp.sum(acc, axis=0, keepdims=True)
        g = total[:, :P]
        delta, mn, vn = _adamw_math(w_ref[...], g, m_ref[...], v_ref[...])
        g_out[...] = g
        d_out[...] = delta
        m_out[...] = mn
        v_out[...] = vn
        loss_out[...] = total[:, P:]

    return pl.pallas_call(
        body, name="adamw_replicated",
        out_shape=[jax.ShapeDtypeStruct((1, P), F32)] * 4 + [jax.ShapeDtypeStruct((1, LANE), F32)],
    )(landed, w, m, v)


def _rope_rows():
    inv_r = ROPE_BASE ** (-jnp.arange(0, RET_DK, 2, dtype=F32) / RET_DK)
    inv_m = ROPE_BASE ** (-jnp.arange(0, MLA_ROPE, 2, dtype=F32) / MLA_ROPE)
    h = MLA_ROPE // 2
    one, zero = jnp.ones((h,), F32), jnp.zeros((h,), F32)
    pad = jnp.zeros((LANE - MLA_ROPE,), F32)
    rows = [jnp.concatenate([inv_r, inv_r]), jnp.concatenate([-jnp.ones((RET_DK // 2,), F32), jnp.ones((RET_DK // 2,), F32)]),
            jnp.concatenate([inv_m, inv_m, pad]), jnp.concatenate([one, one, pad]),
            jnp.concatenate([zero, one, pad]), jnp.concatenate([-one, zero, pad])]
    return jnp.stack(rows + [jnp.zeros((LANE,), F32)] * (SUBLANE - len(rows)))


def _prep_tile(xt, p, rows):
    ang2, angp = p * rows[0:1], p * rows[2:3]
    sp = jnp.sin(angp)
    return (xt, jnp.cos(ang2), jnp.sin(ang2) * rows[1:2], jnp.cos(angp) * rows[3:4], sp * rows[4:5], sp * rows[5:6])


def kernel(x, positions, w_in, q_norm_g, w_uq, kv_norm_g, w_uk, w_uv, ret_gn_g, w_out, ln1_g, ln1_b, w_up, w_down, ln2_g, ln2_b, loss_target, m_w_in, m_q_norm_g, m_w_uq, m_kv_norm_g, m_w_uk, m_w_uv, m_ret_gn_g, m_w_out, m_ln1_g, m_ln1_b, m_w_up, m_w_down, m_ln2_g, m_ln2_b, v_w_in, v_q_norm_g, v_w_uq, v_kv_norm_g, v_w_uk, v_w_uv, v_ret_gn_g, v_w_out, v_ln1_g, v_ln1_b, v_w_up, v_w_down, v_ln2_g, v_ln2_b):
    T, D = x.shape[1], x.shape[2]
    H = D // 256
    RW = H * RET_DV
    x = x[0]
    tgt = loss_target[0]
    alpha = DEEPNORM_ALPHA
    in_width = w_in.shape[2] * N_DEV
    mla_in = in_width - 4 * RW
    mla_in_pad = -(-mla_in // 512) * 512
    f_uq = MLA_NOPE + MLA_ROPE

    names = ["w_in", "w_uq", "w_uk", "w_uv", "w_out", "w_up", "w_down"]
    big = dict(zip(names, [w_in[0].T, w_uq[0], w_uk[0], w_uv[0], w_out[0], w_up[0], w_down[0]]))
    kind = dict(zip(names, ["blk", "blk", "cols", "cols", "rows", "cols", "rows"]))
    half = {n: w.astype(BF16) for n, w in big.items()}

    def gather(ns, **parts):
        return _GatherJob([half[n] for n in ns], [kind[n] for n in ns], 0.55, 0.9, **parts)

    tr = min(256, T)
    xb, cos2, sin2, cosp, sina, sinb, g_in = _rowwise(
        "prep", _prep_tile, [(x, D, 0), (positions[0].astype(F32)[:, None], 1, 0)], [_rope_rows()],
        [(D, BF16)] + [(LANE, F32)] * 5, [], tr, jobs=[gather(["w_in"])])
    win = g_in.reshape(in_width, D)
    w1, n1 = win, 4 * RW
    w2 = jnp.pad(win[4 * RW:], ((0, mla_in_pad - mla_in), (0, 0)))

    first, second, near, far = (0, 2), (1, 2), (CHIP_X, CHIP_Y), (CHIP_DIAGONAL,)
    p1, g_uq, wuk, wuv, wout = _matmul("proj_ret", xb, w1, tb=True, n=n1,
                                       jobs=[gather(["w_uq", "w_uk", "w_uv", "w_out"])])
    wuq = jnp.transpose(g_uq, (1, 0, 2)).reshape(MLA_Q_RANK, H, f_uq)
    wuq = jnp.pad(wuq, ((0, 0), (0, 0), (0, MLA_HEAD_PAD - f_uq))).reshape(MLA_Q_RANK, H * MLA_HEAD_PAD)
    wukv = jnp.concatenate([wuk, wuv], axis=1)
    p2 = _matmul("proj_mla", xb, w2, tb=True)
    c_kv, c_kr = MLA_Q_RANK, MLA_Q_RANK + MLA_KV_RANK

    def mla_prep(p, cp, sa, sb, gq, gkv):
        cqn = _rms(p[:, :c_kv], gq)
        ckvn = _rms(p[:, c_kv:c_kr], gkv)
        return cqn, ckvn, _rope64(p[:, c_kr:c_kr + LANE], cp, sa, sb)

    cqn, ckvn, krope = _rowwise(
        "mla_prep", mla_prep, [(p2, mla_in_pad, 0), (cosp, LANE, 0), (sina, LANE, 0), (sinb, LANE, 0)],
        [q_norm_g, kv_norm_g], [(MLA_Q_RANK, BF16), (MLA_KV_RANK, BF16), (LANE, BF16)], [], tr)
    def q_heads(acc, cp, sa, sb):
        parts = []
        for o in range(0, acc.shape[1], MLA_HEAD_PAD):
            parts += [acc[:, o:o + MLA_NOPE], _rope64(acc[:, o + MLA_NOPE:o + MLA_HEAD_PAD], cp, sa, sb)]
        return (jnp.concatenate(parts, axis=1),)

    def paired(acc, second):
        parts = []
        for o in range(0, acc.shape[1], LANE):
            parts += [acc[:, o:o + LANE], second.astype(acc.dtype)]
        return (jnp.concatenate(parts, axis=1),)

    qh = _matmul("mla_q", cqn, wuq, out_dtypes=(BF16,), epilogue=q_heads, row_extras=(cosp, sina, sinb))
    heads_tile = min(RW, 1024)
    kh = _matmul("mla_k", ckvn, wuk, out_dtypes=(BF16,), epilogue=paired, row_extras=(krope,), tn=heads_tile,
                 out_widths=[2 * heads_tile])
    vx = _matmul("mla_v", ckvn, wuv, out_dtypes=(BF16,), tn=heads_tile, out_widths=[2 * heads_tile],
                 epilogue=lambda acc: paired(acc, jnp.ones((acc.shape[0], MLA_DV), F32)))
    whole = (0, 1)
    att_o, mix_half, lse, wup_a = _attention_fwd(
        qh, kh, vx, H, jobs=[gather(["w_up"], pieces=[(first, near, True, 0.45), (second, near, True, 0.75)])])
    ret_pre, mixin, wup = _retention_fwd(
        p1, cos2, sin2, ret_gn_g, mix_half, H,
        jobs=[gather(["w_up"], pieces=[(first, far, False, 0.55), (second, far, False, 0.9)], base=[wup_a])])
    mix, wdown_a = _matmul("mix_out", mixin, wout,
                           jobs=[gather(["w_down"], pieces=[((0, 4), near, True, 0.5), ((1, 4), near, True, 0.9)])])

    def ln1_fwd(xt, mt, g, b):
        z = alpha * xt + mt
        xhat, _ = _ln_stats(z)
        y = xhat * g + b
        return z, y, y

    z1, x1, x1b, wdown_b = _rowwise("ln1", ln1_fwd, [(x, D, 0), (mix, D, 0)], [ln1_g, ln1_b],
                                    [(D, F32), (D, F32), (D, BF16)], [], tr,
                                    jobs=[gather(["w_down"], pieces=[((2, 4), near, True, 0.5), ((3, 4), near, True, 0.9)],
                                                 base=[wdown_a])])

    def relu2(acc):
        r = jnp.maximum(acc, 0.0)
        return r * r, r

    a2b, rb, wdown = _matmul("mlp_up", x1b, wup, out_dtypes=(BF16, BF16), epilogue=relu2,
                             jobs=[gather(["w_down"], pieces=[(whole, far, False, 0.6)], base=[wdown_b])])
    hmlp = _matmul("mlp_down", a2b, wdown)

    def ln2_loss(x1t, ht, tt, g, b):
        xhat, rstd = _ln_stats(alpha * x1t + ht)
        err = xhat * g + b - tt
        dy = err / D
        dz = _ln_bwd(dy, xhat, rstd, g)
        lrow = 0.5 * jnp.mean(jnp.square(err), axis=-1, keepdims=True)
        return dz, dz, dy * xhat, dy, jnp.broadcast_to(lrow, (lrow.shape[0], LANE))

    dz2, dz2b, dg2, db2, lpart = _rowwise("ln2_loss", ln2_loss, [(x1, D, 0), (hmlp, D, 0), (tgt, D, 0)], [ln2_g, ln2_b],
                                          [(D, F32), (D, BF16)], [D, D, LANE], tr)

    def pair(grads):
        ns = list(grads)
        return _PairJob([grads[n] for n in ns], [kind[n] for n in ns], [big[n].shape for n in ns])

    my_core = lax.axis_index("c").astype(jnp.int32).reshape(1)
    my_chip = (2 * lax.axis_index("x") + lax.axis_index("y")).astype(jnp.int32).reshape(1)
    chip_sums = {}

    def pair_sums(grads, theirs):
        for n, t in zip(grads, theirs):
            chip_sums[n] = _pair_sum("pair_sum_" + n, grads[n], kind[n], big[n].shape, t, my_core)
        return {n: chip_sums[n] for n in grads}

    def chip(sums, rels=CHIP_RELS, by_source=None):
        ns = list(sums)
        return _ChipJob([sums[n] for n in ns], [big[n].shape for n in ns], rels, by_source)

    landed = {}
    da = _matmul("mlp_down_dx", dz2b, wdown, tb=True, out_dtypes=(BF16,), extras=(rb,),
                 epilogue=lambda acc, r: (acc * (2.0 * r.astype(F32)),))
    g_down = {"w_down": _matmul("mlp_down_dw", a2b, dz2b, ta=True, out_dtypes=(BF16,))}
    g_wup, *pr = _matmul("mlp_up_dw", x1b, da, ta=True, out_dtypes=(BF16,), jobs=[pair(g_down)])
    sums_down = pair_sums(g_down, pr)
    g_up = {"w_up": g_wup}
    dx1m, l_a, *pr = _matmul("mlp_up_dx", da, wup, tb=True, jobs=[chip(sums_down, CHIP_RELS[:2]), pair(g_up)])
    sums_up = pair_sums(g_up, pr)

    def ln1_bwd(dm, dzz, z, g):
        xhat, rstd = _ln_stats(z)
        dy = dm + alpha * dzz
        dz = _ln_bwd(dy, xhat, rstd, g)
        return dz, dz, dy * xhat, dy

    dz1, dz1b, dg1, db1 = _rowwise("ln1_bwd", ln1_bwd, [(dx1m, D, 0), (dz2, D, 0), (z1, D, 0)], [ln1_g],
                                   [(D, F32), (D, BF16)], [D, D], tr)
    g_wout = _matmul("mix_out_dw", mixin, dz1b, ta=True, out_dtypes=(BF16,))
    dmix = _matmul("mix_out_dx", dz1b, wout, tb=True)

    dqh, dkn, dkr, dvv, l_b, l_c = _attention_bwd(qh, kh, vx, att_o, lse, dmix, H,
                                                  jobs=[chip(sums_down, CHIP_RELS[2:]), chip(sums_up)])
    landed["w_down"], landed["w_up"] = [l_a, l_b], [l_c]
    att_scale = (MLA_NOPE + MLA_ROPE) ** -0.5

    def mla_heads_bwd(dq, dkrh, cp, sa, sb):
        parts, acc = [], dkrh[:, :LANE]
        dq = dq * att_scale
        for h in range(H):
            o = h * MLA_HEAD_PAD
            parts += [dq[:, o:o + MLA_NOPE], _unrope64(dq[:, o + MLA_NOPE:o + MLA_HEAD_PAD], cp, sa, sb)]
            if h:
                acc = acc + dkrh[:, h * LANE:(h + 1) * LANE]
        return jnp.concatenate(parts, axis=1), _unrope64(acc, cp, sa, sb)

    dqb, dkr128 = _rowwise(
        "mla_heads_bwd", mla_heads_bwd,
        [(dqh, H * MLA_HEAD_PAD, 0), (dkr, H * LANE, 0), (cosp, LANE, 0), (sina, LANE, 0), (sinb, LANE, 0)],
        [], [(H * MLA_HEAD_PAD, BF16), (LANE, F32)], [], tr)
    dcqn = _matmul("mla_q_dx", dqb, wuq, tb=True)
    g_wuq = _matmul("mla_q_dw", cqn, dqb, ta=True, out_dtypes=(BF16,))
    dkvb = jnp.concatenate([dkn, dvv], axis=1)
    dckvn = _matmul("mla_kv_dx", dkvb, wukv, tb=True)
    g_wukv = _matmul("mla_kv_dw", ckvn, dkvb, ta=True, out_dtypes=(BF16,))

    def rms_bwd(c, dy, g):
        rstd = lax.rsqrt(jnp.mean(jnp.square(c), axis=-1, keepdims=True) + EPS)
        dyg = dy * g
        dc = rstd * (dyg - c * (rstd * rstd) * jnp.mean(dyg * c, axis=-1, keepdims=True))
        return dc, dy * c * rstd

    def mla_prep_bwd(p, dq_, dkv_, dkr_, gq, gkv):
        dcq, dgq_ = rms_bwd(p[:, :c_kv], dq_, gq)
        dckv, dgkv_ = rms_bwd(p[:, c_kv:c_kr], dkv_, gkv)
        pad = jnp.zeros((p.shape[0], mla_in_pad - c_kr - LANE), F32)
        return jnp.concatenate([dcq, dckv, dkr_, pad], axis=1), dgq_, dgkv_

    d_p2, dgq, dgkv = _rowwise(
        "mla_prep_bwd", mla_prep_bwd, [(p2, mla_in_pad, 0), (dcqn, MLA_Q_RANK, 0), (dckvn, MLA_KV_RANK, 0), (dkr128, LANE, 0)],
        [q_norm_g, kv_norm_g], [(mla_in_pad, BF16)], [MLA_Q_RANK, MLA_KV_RANK], tr)

    def gate_bwd(pre, gate, dout, gn):
        dpre, dgate, dgn = [], [], []
        for h in range(H):
            sl = slice(h * RET_DV, (h + 1) * RET_DV)
            nrm, rstd = _ln_stats(pre[:, sl])
            gate_h, do_h, gn_h = gate[:, sl], dout[:, sl], gn[:, sl]
            sg = jax.nn.sigmoid(gate_h)
            silu = gate_h * sg
            dgate.append(do_h * (nrm * gn_h) * (sg * (1.0 + gate_h * (1.0 - sg))))
            dn = do_h * silu
            dgn.append(dn * nrm)
            dpre.append(_ln_bwd(dn, nrm, rstd, gn_h))
        return jnp.concatenate(dpre, axis=1), jnp.concatenate(dgate, axis=1), jnp.concatenate(dgn, axis=1)

    dpre, drg, dgn = _rowwise("ret_gate_bwd", gate_bwd, [(ret_pre, RW, 0), (p1, RW, 3), (dmix, RW, 0)], [ret_gn_g],
                              [(RW, F32), (RW, BF16)], [RW], tr)
    g_uq = g_wuq.reshape(MLA_Q_RANK, H, MLA_HEAD_PAD)[:, :, :f_uq].reshape(MLA_Q_RANK, N_DEV, H * f_uq // N_DEV)
    mid_grads = {"w_uq": jnp.transpose(g_uq, (1, 0, 2)), "w_uk": g_wukv[:, :RW], "w_uv": g_wukv[:, RW:], "w_out": g_wout}
    (drq, *pr), (drk, drv) = _retention_bwd(p1, dpre, cos2, sin2, H, dq_jobs=[pair(mid_grads)])
    sums_mid = pair_sums(mid_grads, pr)
    d_p1 = jnp.concatenate([drq, drk, drv, drg], axis=1)

    g_w1, *l_a = _matmul("proj_ret_dw", d_p1, xb, ta=True, out_dtypes=(BF16,), jobs=[chip(sums_mid)])
    for n, a_ in zip(mid_grads, l_a):
        landed[n] = [a_]
    g_w2 = _matmul("proj_mla_dw", d_p2, xb, ta=True, out_dtypes=(BF16,))

    g_win = {"w_in": jnp.concatenate([g_w1, g_w2[:mla_in]], axis=0).reshape(N_DEV, in_width // N_DEV, D)}
    small = jnp.concatenate([dgq, dgkv, dgn, dg1, db1, dg2, db2, lpart], axis=1)
    dx_a, *pr = _matmul("proj_mla_dx", d_p2, w2, extras=(dz1,), epilogue=lambda acc, d: (acc + alpha * d,), jobs=[pair(g_win)])
    sums_in = pair_sums(g_win, pr)
    grad_x, l_a, small_landed = _matmul("proj_ret_dx", d_p1, w1, extras=(dx_a,), epilogue=lambda acc, d: (acc + d,),
                                        jobs=[chip(sums_in, CHIP_RELS[:2], by_source=small)])

    moments = dict(zip(names, zip([m_w_in, m_w_uq, m_w_uk, m_w_uv, m_w_out, m_w_up, m_w_down],
                                  [v_w_in, v_w_uq, v_w_uk, v_w_uv, v_w_out, v_w_up, v_w_down])))
    res = {}

    def adamw(n, jobs=()):
        as_held = (lambda a: a[0].T) if n == "w_in" else (lambda a: a[0])
        as_given = (lambda r: r.T[None]) if n == "w_in" else (lambda r: r[None])
        out = _adamw("adamw_" + n, chip_sums[n], my_chip, landed[n], big[n], as_held(moments[n][0]), as_held(moments[n][1]),
                     jobs=jobs)
        res[n] = [as_given(r) for r in out[:4]]
        return out[4:]

    (l_b,) = adamw("w_out", jobs=[chip(sums_in, CHIP_RELS[2:])])
    landed["w_in"] = [l_a, l_b]
    for n in names:
        if n != "w_out":
            adamw(n)

    small_names = ["q_norm_g", "kv_norm_g", "ret_gn_g", "ln1_g", "ln1_b", "ln2_g", "ln2_b"]
    small_w = [q_norm_g, kv_norm_g, ret_gn_g, ln1_g, ln1_b, ln2_g, ln2_b]
    small_m = [m_q_norm_g, m_kv_norm_g, m_ret_gn_g, m_ln1_g, m_ln1_b, m_ln2_g, m_ln2_b]
    small_v = [v_q_norm_g, v_kv_norm_g, v_ret_gn_g, v_ln1_g, v_ln1_b, v_ln2_g, v_ln2_b]
    cat = lambda arrs: jnp.concatenate(arrs, axis=1)
    *sres, loss_lanes = _adamw_small(small_landed, cat(small_w), cat(small_m), cat(small_v))
    loss = loss_lanes[0, 0]
    off = 0
    for n, w in zip(small_names, small_w):
        res[n] = [r[:, off:off + w.shape[1]] for r in sres]
        off += w.shape[1]

    order = ["w_in", "q_norm_g", "w_uq", "kv_norm_g", "w_uk", "w_uv", "ret_gn_g", "w_out", "ln1_g", "ln1_b",
             "w_up", "w_down", "ln2_g", "ln2_b"]
    outs = [loss, grad_x[None]]
    for k in range(4):
        outs += [res[n][k] for n in order]
    return tuple(outs)
```

```python
import functools

import numpy as np

import jax
import jax.numpy as jnp
from jax import lax
from jax.experimental import pallas as pl
from jax.experimental.pallas import tpu as pltpu

F32 = jnp.float32
BF16 = jnp.bfloat16

CHUNK = 64
RET_DK = 128
RET_DV = 128
MLA_NOPE = 128
MLA_ROPE = 64
MLA_DV = 128
MLA_Q_RANK = 768
MLA_KV_RANK = 512
MLA_HEAD_PAD = 256
ROPE_BASE = 10000.0
EPS = 1e-5
DEPTH = 1
DEEPNORM_ALPHA = (2.0 * DEPTH) ** 0.25
ADAM_LR = 0.001
ADAM_B1 = 0.9
ADAM_B2 = 0.999
ADAM_EPS = 1e-08
ADAM_WD = 0.01
ADAM_STEP = 10

N_DEV = 8
LANE = 128
SUBLANE = 8
VMEM_LIMIT = 48 * 1024 * 1024
ADAMW_TILE_ELEMS = 128 * 1024
PAIR_TILE_ELEMS = 1024 * 1024
DW_TK = 4096

MESH = pl.DeviceIdType.MESH
ANY = pl.BlockSpec(memory_space=pl.ANY)

NN = (((1,), (0,)), ((), ()))
NT = (((1,), (1,)), ((), ()))
TN = (((0,), (0,)), ((), ()))


def _dot(a, b, dims=NN):
    return lax.dot_general(a.astype(BF16), b.astype(BF16), dims, preferred_element_type=F32)


def _params(n_axes):
    return pltpu.CompilerParams(dimension_semantics=("arbitrary",) * n_axes, vmem_limit_bytes=VMEM_LIMIT)


def _tile(n, want):
    if n <= want:
        return n
    t = want
    while t >= LANE:
        if n % t == 0:
            return t
        t -= LANE
    return n


def _my_place():
    return lax.axis_index("x"), lax.axis_index("y"), lax.axis_index("c")


def _flat(px, py, pc):
    return 4 * px + 2 * py + pc


def _block_of(ref, kind, idx, rows, cols, sub=None):
    lo, n = (0, rows) if sub is None else sub
    if kind == "blk":
        return ref.at[idx, pl.ds(lo, n), :]
    if kind == "rows":
        return ref.at[pl.ds(pl.multiple_of(idx * rows + lo, n), n), :]
    return ref.at[pl.ds(lo, n), pl.ds(pl.multiple_of(idx * cols, cols), cols)]


CHIP_X, CHIP_Y, CHIP_DIAGONAL = 0, 1, 2
ALL_CHIPS = (CHIP_X, CHIP_Y, CHIP_DIAGONAL)


class _GatherJob:
    def __init__(self, shards, kinds, mid, late, pieces=(((0, 1), ALL_CHIPS, True),), base=None):
        nw = len(shards)
        self.nw, self.kinds = nw, list(kinds)
        self.pieces = [tuple(p) if len(p) == 4 else (*p, late) for p in pieces]
        for _, chips, _, _ in self.pieces:
            assert CHIP_DIAGONAL not in chips or len(chips) in (1, 3)
        self.shapes = [s.shape for s in shards]
        self.ins = list(shards) + (list(base) if base is not None else [])
        self.alias = [(nw + w, w) for w in range(nw)] if base is not None else []
        self.out_shape = []
        for s, kind in zip(shards, kinds):
            r, c = s.shape
            shape = {"blk": (N_DEV, r, c), "rows": (N_DEV * r, c), "cols": (r, N_DEV * c)}[kind]
            self.out_shape.append(jax.ShapeDtypeStruct(shape, s.dtype))
        n = 7 * nw * len(self.pieces)
        self.sems = [pltpu.SemaphoreType.DMA((n,)), pltpu.SemaphoreType.DMA((n,)), pltpu.SemaphoreType.DMA((n,))]
        whens = sorted({p[3] for p in self.pieces})
        self.phases = ([(0.0, self._start), (mid, self._relay)] + [(f, functools.partial(self._pass_on, f)) for f in whens]
                       + [(1.0, self._finish)])

    def _each(self, src, dst, send_sems, recv_sems, local_sems):
        x, y, c = _my_place()
        all_chips = [(1 - x, y), (x, 1 - y), (1 - x, 1 - y)]
        out = []
        for w in range(self.nw):
            r, cc = self.shapes[w]
            for p, ((k_of, of), chips, own, when) in enumerate(self.pieces):
                sub = (k_of * (r // of), r // of)
                s0 = 7 * (w * len(self.pieces) + p)
                mine = src[w].at[pl.ds(*sub), :]

                def place(dev, w=w, r=r, cc=cc, sub=sub):
                    return _block_of(dst[w], self.kinds[w], _flat(*dev), r, cc, sub)

                def copy(k, block, to, from_src=False, place=place, mine=mine, s0=s0):
                    return pltpu.make_async_remote_copy(
                        src_ref=mine if from_src else place(block), dst_ref=place(block),
                        send_sem=send_sems.at[s0 + k], recv_sem=recv_sems.at[s0 + k], device_id=to, device_id_type=MESH)

                relay = functools.partial(copy, 1 + CHIP_DIAGONAL, (x ^ (1 - c), y ^ c, c), (x ^ c, y ^ (1 - c), c))
                local = functools.partial(pltpu.make_async_copy, mine, place((x, y, c)), local_sems.at[s0])
                near = [(j, all_chips[j]) for j in chips if j != CHIP_DIAGONAL]
                out.append((near, CHIP_DIAGONAL in chips, own, copy, relay, local, when))
        return (x, y, c), (x, y, 1 - c), all_chips[CHIP_DIAGONAL], out

    def _start(self, *refs):
        me, sibling, diagonal, each = self._each(*refs)
        for near, far, own, copy, relay, local, when in each:
            if own:
                local().start()
                copy(0, me, sibling, from_src=True).start()
            for j, chip in near:
                copy(1 + j, me, (*chip, me[2]), from_src=True).start()
            if far and not near:
                relay().start()

    def _landed(self, near, copy, me, sibling):
        for j, chip in near:
            copy(1 + j, (*chip, me[2]), me).wait_recv()
            copy(4 + j, (*chip, me[2]), sibling).start()

    def _relay(self, *refs):
        me, sibling, diagonal, each = self._each(*refs)
        for near, far, own, copy, relay, local, when in each:
            if far and near:
                self._landed(near, copy, me, sibling)
                relay().start()

    def _pass_on(self, frac, *refs):
        me, sibling, diagonal, each = self._each(*refs)
        for near, far, own, copy, relay, local, when in each:
            if when == frac:
                self._landed([(CHIP_DIAGONAL, diagonal)] if far else near, copy, me, sibling)

    def _finish(self, *refs):
        me, sibling, diagonal, each = self._each(*refs)
        for near, far, own, copy, relay, local, when in each:
            if own:
                copy(0, sibling, me).wait_recv()
            for j, chip in near + [(CHIP_DIAGONAL, diagonal)] * int(far):
                copy(4 + j, (*chip, 1 - me[2]), me).wait_recv()
        for near, far, own, copy, relay, local, when in each:
            if own:
                copy(0, me, sibling, from_src=True).wait_send()
                local().wait()
            for j, chip in near:
                copy(1 + j, me, (*chip, me[2]), from_src=True).wait_send()
                copy(4 + j, (*chip, me[2]), sibling).wait_send()
            if far:
                relay().wait_send()
                copy(4 + CHIP_DIAGONAL, (*diagonal, me[2]), sibling).wait_send()


REL_ALL = [(1, 0, 0), (1, 0, 1), (0, 1, 0), (0, 1, 1), (1, 1, 0), (1, 1, 1), (0, 0, 1)]
N_CHIP = 4


class _PairJob:
    def __init__(self, grads, kinds, shard_shapes):
        self.ins = list(grads)
        self.kinds, self.shard_shapes, self.nw = list(kinds), list(shard_shapes), len(grads)
        self.out_shape = [jax.ShapeDtypeStruct((N_CHIP,) + tuple(s), g.dtype) for g, s in zip(grads, shard_shapes)]
        n = N_CHIP * self.nw
        self.sems = [pltpu.SemaphoreType.DMA((n,)), pltpu.SemaphoreType.DMA((n,))]
        self.phases = [(0.0, self._start), (1.0, self._finish)]

    def _copies(self, src, theirs, send_sems, recv_sems):
        x, y, c = _my_place()
        remote = []
        for w in range(self.nw):
            r, cc = self.shard_shapes[w]
            for q in range(N_CHIP):
                s = N_CHIP * w + q
                remote.append(pltpu.make_async_remote_copy(
                    src_ref=_block_of(src[w], self.kinds[w], 2 * q + (1 - c), r, cc), dst_ref=theirs[w].at[q],
                    send_sem=send_sems.at[s], recv_sem=recv_sems.at[s], device_id=(x, y, 1 - c), device_id_type=MESH))
        return remote

    def _start(self, *refs):
        for cp in self._copies(*refs):
            cp.start()

    def _finish(self, *refs):
        for cp in self._copies(*refs):
            cp.wait()


CHIP_RELS = [(1, 0), (0, 1), (1, 1)]


class _ChipJob:
    def __init__(self, chip_sums, shard_shapes, rels=CHIP_RELS, by_source=None, piece=(0, 1), base=None):
        nw = len(chip_sums)
        self.ins = list(chip_sums) + ([by_source] if by_source is not None else []) + (list(base) if base is not None else [])
        self.nw, self.has_src, self.rels, self.piece = nw, by_source is not None, list(rels), piece
        self.rows = [s[0] for s in shard_shapes]
        self.alias = [(nw + int(self.has_src) + w, w) for w in range(nw)] if base is not None else []
        self.out_shape = [jax.ShapeDtypeStruct((len(rels),) + tuple(s), a.dtype) for a, s in zip(chip_sums, shard_shapes)]
        if self.has_src:
            self.out_shape.append(jax.ShapeDtypeStruct((N_DEV,) + by_source.shape, by_source.dtype))
        n = len(rels) * self.nw + len(REL_ALL) * int(self.has_src)
        self.sems = [pltpu.SemaphoreType.DMA((n,)), pltpu.SemaphoreType.DMA((n,)), pltpu.SemaphoreType.DMA((1,))]
        self.phases = [(0.0, self._start), (1.0, self._finish)]

    def _copies(self, src, land, send_sems, recv_sems, local_sems):
        x, y, c = _my_place()
        nw = self.nw

        def remote_copy(s, src_ref, dst_ref, peer):
            return pltpu.make_async_remote_copy(src_ref=src_ref, dst_ref=dst_ref, send_sem=send_sems.at[s],
                                                recv_sem=recv_sems.at[s], device_id=peer, device_id_type=MESH)

        def rows(w):
            n = self.rows[w] // self.piece[1]
            return pl.ds(self.piece[0] * n, n)

        local, remote = [], []
        for k, (fx, fy) in enumerate(self.rels):
            px, py = x ^ fx, y ^ fy
            remote += [remote_copy(nw * k + w, src[w].at[2 * px + py, rows(w), :], land[w].at[k, rows(w), :], (px, py, c))
                       for w in range(nw)]
        if self.has_src:
            my_idx = _flat(x, y, c)
            local.append(pltpu.make_async_copy(src[nw], land[nw].at[my_idx], local_sems.at[0]))
            remote += [remote_copy(nw * len(self.rels) + k, src[nw], land[nw].at[my_idx], (x ^ fx, y ^ fy, c ^ fc))
                       for k, (fx, fy, fc) in enumerate(REL_ALL)]
        return local, remote

    def _start(self, *refs):
        local, remote = self._copies(*refs)
        for cp in local + remote:
            cp.start()

    def _finish(self, *refs):
        local, remote = self._copies(*refs)
        for cp in remote + local:
            cp.wait()


def _row_tile(R, C, elems=ADAMW_TILE_ELEMS):
    tr = 16
    while tr * 2 * C <= elems and R % (tr * 2) == 0:
        tr *= 2
    return tr if R % tr == 0 else R


BF16_ROWS = 16


def _tile_2d(R, C, elems):
    if R % BF16_ROWS == 0:
        tr = _row_tile(R, C, elems)
        return tr, C, R // tr
    tc = C
    while tc % (2 * LANE) == 0 and R * tc > elems:
        tc //= 2
    return R, tc, C // tc


def _tile_index(tr, tc, C):
    return (lambda i: (i, 0)) if tc == C else (lambda i: (0, i))


def _part_spec(kind, tr, tc, R, C, n_lead):
    at = _tile_index(tr, tc, C)

    def dev(ids, pref):
        return 2 * ids[0] + pref[0] if n_lead == 2 else pref[0]

    def index(*args):
        ids, pref = args[:n_lead], args[n_lead]
        ri, ci = at(ids[-1])
        if kind == "blk":
            return dev(ids, pref), ri, ci
        if kind == "rows":
            return dev(ids, pref) * (R // tr) + ri, ci
        return ri, dev(ids, pref) * (C // tc) + ci

    return pl.BlockSpec((None, tr, tc) if kind == "blk" else (tr, tc), index)


def _pair_sum(name, grad, kind, shard_shape, theirs, c):
    R, C = shard_shape
    tr, tc, tiles = _tile_2d(R, C, PAIR_TILE_ELEMS)
    at = _tile_index(tr, tc, C)

    def body(c_ref, g_ref, t_ref, o_ref):
        o_ref[...] = (g_ref[...].astype(F32) + t_ref[...].astype(F32)).astype(o_ref.dtype)

    slot = pl.BlockSpec((None, tr, tc), lambda q, i, c_ref: (q, *at(i)))
    return pl.pallas_call(
        body, name=name,
        grid_spec=pltpu.PrefetchScalarGridSpec(
            num_scalar_prefetch=1, grid=(N_CHIP, tiles),
            in_specs=[_part_spec(kind, tr, tc, R, C, 2), slot], out_specs=slot),
        out_shape=jax.ShapeDtypeStruct(theirs.shape, theirs.dtype), compiler_params=_params(2))(c, grad, theirs)


class _Carried:
    def __init__(self, jobs, grid):
        self.jobs = list(jobs)
        self.steps = int(np.prod(grid))
        self.grid = tuple(grid)
        self.operands = [a for j in self.jobs for a in j.ins]
        self.out_shape = [o for j in self.jobs for o in j.out_shape]
        self.sems = [s for j in self.jobs for s in j.sems]
        self.in_specs = [ANY] * len(self.operands)
        self.out_specs = [ANY] * len(self.out_shape)

    def aliases(self, n_in, n_out):
        out, i0, o0 = {}, n_in, n_out
        for job in self.jobs:
            for ji, jo in getattr(job, "alias", []):
                out[i0 + ji] = o0 + jo
            i0, o0 = i0 + len(job.ins), o0 + len(job.out_shape)
        return out

    def _step(self):
        lin = pl.program_id(0)
        for a in range(1, len(self.grid)):
            lin = lin * self.grid[a] + pl.program_id(a)
        return lin

    def _run(self, in_refs, out_refs, sem_refs, last):
        lin = self._step()
        for job in self.jobs:
            ins, in_refs = in_refs[:len(job.ins)], in_refs[len(job.ins):]
            outs, out_refs = out_refs[:len(job.out_shape)], out_refs[len(job.out_shape):]
            sems, sem_refs = sem_refs[:len(job.sems)], sem_refs[len(job.sems):]
            for frac, fn in job.phases:
                if (frac >= 1.0) != last:
                    continue
                at = min(self.steps - 1, int(frac * (self.steps - 1)))
                pl.when(lin == at)(functools.partial(fn, ins, outs, *sems))

    def before(self, in_refs, out_refs, sem_refs):
        self._run(in_refs, out_refs, sem_refs, last=False)

    def after(self, in_refs, out_refs, sem_refs):
        self._run(in_refs, out_refs, sem_refs, last=True)


def _split_refs(refs, n_in, n_job_in, n_out, n_job_out, n_scratch):
    cuts = np.cumsum([0, n_in, n_job_in, n_out, n_job_out, n_scratch])
    return [refs[cuts[i]:cuts[i + 1]] for i in range(5)] + [refs[cuts[5]:]]


def _matmul(name, a, b, *, ta=False, tb=False, out_dtypes=(F32,), epilogue=None, extras=(), tm=1024, tn=1024, tk=2048, jobs=(),
            n=None, row_extras=(), const_extras=(), out_widths=None):
    if ta:
        K, M = a.shape
    else:
        M, K = a.shape
    N = n if n is not None else (b.shape[0] if tb else b.shape[1])
    if ta:
        tk = max(tk, DW_TK)
    tm, tn, tk = _tile(M, tm), _tile(N, tn), _tile(K, tk)
    nk = K // tk
    grid = (M // tm, N // tn, nk)
    dims = TN if ta else (NT if tb else NN)
    n_ex, n_out = len(extras) + len(row_extras) + len(const_extras), len(out_dtypes)
    widths = list(out_widths) if out_widths is not None else [tn] * n_out
    carried = _Carried(jobs, grid)

    def body(*refs):
        ins, job_ins, out_refs, job_outs, scratch, job_sems = _split_refs(
            refs, 2 + n_ex, len(carried.operands), n_out, len(carried.out_shape), int(nk > 1))
        a_ref, b_ref, ex_refs = ins[0], ins[1], ins[2:]
        k = pl.program_id(2)
        carried.before(job_ins, job_outs, job_sems)

        def finish(acc):
            res = (acc,) if epilogue is None else epilogue(acc, *[e[...] for e in ex_refs])
            for o_ref, r in zip(out_refs, res):
                o_ref[...] = r.astype(o_ref.dtype)

        if nk == 1:
            finish(_dot(a_ref[...], b_ref[...], dims))
        else:
            acc_ref = scratch[0]

            @pl.when(k == 0)
            def _():
                acc_ref[...] = _dot(a_ref[...], b_ref[...], dims)

            @pl.when(k > 0)
            def _():
                acc_ref[...] += _dot(a_ref[...], b_ref[...], dims)

            pl.when(k == nk - 1)(lambda: finish(acc_ref[...]))

        carried.after(job_ins, job_outs, job_sems)

    a_spec = pl.BlockSpec((tk, tm), lambda i, j, k: (k, i)) if ta else pl.BlockSpec((tm, tk), lambda i, j, k: (i, k))
    b_spec = pl.BlockSpec((tn, tk), lambda i, j, k: (j, k)) if tb else pl.BlockSpec((tk, tn), lambda i, j, k: (k, j))
    tile_spec = pl.BlockSpec((tm, tn), lambda i, j, k: (i, j))
    row_specs = [pl.BlockSpec((tm, r.shape[1]), lambda i, j, k: (i, 0)) for r in row_extras]
    row_specs += [pl.BlockSpec(c.shape, functools.partial(lambda i, j, k, nd: (0,) * nd, nd=c.ndim)) for c in const_extras]
    outs = pl.pallas_call(
        body,
        name=name,
        grid=grid,
        in_specs=[a_spec, b_spec] + [tile_spec] * len(extras) + row_specs + carried.in_specs,
        out_specs=[pl.BlockSpec((tm, w), lambda i, j, k: (i, j)) for w in widths] + carried.out_specs,
        out_shape=[jax.ShapeDtypeStruct((M, N // tn * w), d) for d, w in zip(out_dtypes, widths)] + carried.out_shape,
        scratch_shapes=[pltpu.VMEM((tm, tn), F32)] * int(nk > 1) + carried.sems,
        input_output_aliases=carried.aliases(2 + n_ex, n_out),
        compiler_params=_params(3),
    )(a, b, *extras, *row_extras, *const_extras, *carried.operands)
    return outs[0] if len(outs) == 1 else outs


def _rowwise(name, fn, row_in, const_in, row_out, acc_out, tm, jobs=()):
    T = row_in[0][0].shape[0]
    n_r, n_c, n_o, n_a = len(row_in), len(const_in), len(row_out), len(acc_out)
    carried = _Carried(jobs, (T // tm,))

    def body(*refs):
        in_refs, job_ins, own_outs, job_outs, _, job_sems = _split_refs(
            refs, n_r + n_c, len(carried.operands), n_o + n_a, len(carried.out_shape), 0)
        out_refs, acc_refs = own_outs[:n_o], own_outs[n_o:]
        carried.before(job_ins, job_outs, job_sems)
        res = fn(*[r[...] for r in in_refs])
        for o_ref, r in zip(out_refs, res[:n_o]):
            o_ref[...] = r.astype(o_ref.dtype)

        @pl.when(pl.program_id(0) == 0)
        def _():
            for a_ref in acc_refs:
                a_ref[...] = jnp.zeros_like(a_ref)

        for a_ref, r in zip(acc_refs, res[n_o:]):
            a_ref[...] += jnp.sum(r.reshape(tm // SUBLANE, SUBLANE, r.shape[-1]), axis=0)
        carried.after(job_ins, job_outs, job_sems)

    in_specs = [pl.BlockSpec((tm, w), functools.partial(lambda i, c: (i, c), c=cb)) for (_, w, cb) in row_in]
    in_specs += [pl.BlockSpec(c.shape, functools.partial(lambda i, nd: (0,) * nd, nd=c.ndim)) for c in const_in]
    out_specs = [pl.BlockSpec((tm, c), lambda i: (i, 0)) for (c, _) in row_out]
    out_specs += [pl.BlockSpec((SUBLANE, c), lambda i: (0, 0)) for c in acc_out]
    out_shape = [jax.ShapeDtypeStruct((T, c), d) for (c, d) in row_out]
    out_shape += [jax.ShapeDtypeStruct((SUBLANE, c), F32) for c in acc_out]
    return pl.pallas_call(
        body, name=name, grid=(T // tm,), in_specs=in_specs + carried.in_specs, out_specs=out_specs + carried.out_specs,
        out_shape=out_shape + carried.out_shape, scratch_shapes=carried.sems,
        input_output_aliases=carried.aliases(n_r + n_c, n_o + n_a), compiler_params=_params(1),
    )(*[r[0] for r in row_in], *const_in, *carried.operands)


def _ln_stats(z):
    mu = jnp.mean(z, axis=-1, keepdims=True)
    var = jnp.mean(jnp.square(z - mu), axis=-1, keepdims=True)
    rstd = lax.rsqrt(var + EPS)
    return (z - mu) * rstd, rstd


def _ln_bwd(dy, xhat, rstd, g):
    dxh = dy * g
    return rstd * (dxh - jnp.mean(dxh, axis=-1, keepdims=True) - xhat * jnp.mean(dxh * xhat, axis=-1, keepdims=True))


def _rms(x, g):
    return x * lax.rsqrt(jnp.mean(jnp.square(x), axis=-1, keepdims=True) + EPS) * g


def _rope128(x, cos2, sin2):
    return x * cos2 + pltpu.roll(x, RET_DK // 2, 1) * sin2


def _unrope128(dy, cos2, sin2):
    return dy * cos2 + pltpu.roll(dy * sin2, RET_DK // 2, 1)


def _rope64(x, cosp, sina, sinb):
    h = MLA_ROPE // 2
    return x * cosp + pltpu.roll(x, h, 1) * sina + pltpu.roll(x, LANE - h, 1) * sinb


def _unrope64(dy, cosp, sina, sinb):
    h = MLA_ROPE // 2
    return dy * cosp + pltpu.roll(dy * sina, LANE - h, 1) + pltpu.roll(dy * sinb, h, 1)


RET_BLOCK = 256
RET_HEADS_PER_STEP = 4


def _ret_tables(H, B):
    log_g = np.log1p(-np.exp2(-5.0 - np.arange(H, dtype=np.float32))).astype(np.float32)
    idx = np.arange(B, dtype=np.float32)
    dist = np.abs(idx[:, None] - idx[None, :])
    visible = (np.arange(B)[None, :] // CHUNK) <= (np.arange(B)[:, None] // CHUNK)
    dmat = np.exp(log_g[:, None, None] * dist).astype(np.float32) * visible[None].astype(np.float32)
    qd = np.exp(log_g[:, None] * (idx + 1.0)).astype(np.float32)
    kd = np.exp(log_g[:, None] * (B - 1 - idx)).astype(np.float32)
    sd = np.exp(log_g * B).astype(np.float32)
    bc = lambda v: np.ascontiguousarray(np.broadcast_to(v[:, :, None], (H, v.shape[1], LANE)))
    sdb = np.ascontiguousarray(np.broadcast_to(sd[:, None, None], (H, SUBLANE, LANE)))
    return jnp.asarray(dmat), jnp.asarray(bc(qd)), jnp.asarray(bc(kd)), jnp.asarray(sdb)


def _ret_specs(H, B, reverse, nb):
    blk = (lambda b: nb - 1 - b) if reverse else (lambda b: b)
    G = min(RET_HEADS_PER_STEP, H)
    W = G * LANE
    col = lambda off: pl.BlockSpec((B, W), functools.partial(lambda h, b, off: (blk(b), off // G + h), off=off))
    row = pl.BlockSpec((B, LANE), lambda h, b: (blk(b), 0))
    per_head = lambda r: pl.BlockSpec((G, r, LANE), lambda h, b: (h, 0, 0))
    dmat = pl.BlockSpec((G, B, B), lambda h, b: (h, 0, 0))
    return G, col, row, per_head, dmat


def _ret_call(name, body, n_in, n_out, operands, in_specs, out_specs, out_shape, grid, G, jobs, own_alias=None):
    carried = _Carried(jobs, grid)

    def wrapped(*refs):
        ins, job_ins, outs, job_outs, (state,), job_sems = _split_refs(
            refs, n_in, len(carried.operands), n_out, len(carried.out_shape), 1)
        carried.before(job_ins, job_outs, job_sems)

        @pl.when(pl.program_id(1) == 0)
        def _():
            state[...] = jnp.zeros_like(state)

        body(*ins, *outs, state)
        carried.after(job_ins, job_outs, job_sems)

    return pl.pallas_call(
        wrapped, name=name, grid=grid,
        in_specs=in_specs + carried.in_specs, out_specs=out_specs + carried.out_specs,
        out_shape=out_shape + carried.out_shape,
        scratch_shapes=[pltpu.VMEM((G, RET_DK, RET_DV), F32)] + carried.sems,
        input_output_aliases={**(own_alias or {}), **carried.aliases(n_in, n_out)},
        compiler_params=_params(2),
    )(*operands, *carried.operands)


def _retention_fwd(p1, cos2, sin2, gn_g, mix_in, H, jobs=()):
    T = p1.shape[0]
    B = min(RET_BLOCK, T)
    nb = T // B
    scale = RET_DK ** -0.5
    dmat, qd, kd, sd = _ret_tables(H, B)
    G, col, row, per_head, dspec = _ret_specs(H, B, False, nb)

    def body(q_ref, k_ref, v_ref, g_ref, cos_ref, sin_ref, d_ref, qd_ref, kd_ref, sd_ref, gn_ref, _, pre_ref, out_ref, s_ref):
        cos, sin = cos_ref[...], sin_ref[...]
        for g in range(G):
            hl = pl.ds(g * LANE, LANE)
            q = _rope128(q_ref[:, hl], cos, sin)
            k = _rope128(k_ref[:, hl], cos, sin) * scale
            v = v_ref[:, hl]
            s = s_ref[g]
            a = _dot(q, k, NT) * d_ref[g]
            ret = _dot(a, v) + _dot(q * qd_ref[g], s)
            s_ref[g] = sd_ref[g, 0:1, :] * s + _dot(k * kd_ref[g], v, TN)
            pre_ref[:, hl] = ret
            nrm, _ = _ln_stats(ret)
            out_ref[:, hl] = (jax.nn.silu(g_ref[:, hl]) * (nrm * gn_ref[:, hl])).astype(out_ref.dtype)

    out_spec = pl.BlockSpec((B, G * LANE), lambda h, b: (b, h))
    return _ret_call(
        "retention_fwd", body, 12, 2, (p1, p1, p1, p1, cos2, sin2, dmat, qd, kd, sd, gn_g, mix_in),
        [col(0), col(H), col(2 * H), col(3 * H), row, row, dspec, per_head(B), per_head(B), per_head(SUBLANE),
         pl.BlockSpec((1, G * LANE), lambda h, b: (0, h)), ANY],
        [out_spec, out_spec],
        [jax.ShapeDtypeStruct((T, H * RET_DV), F32), jax.ShapeDtypeStruct(mix_in.shape, mix_in.dtype)],
        (H // G, nb), G, jobs, own_alias={11: 1})


def _retention_bwd(p1, dpre, cos2, sin2, H, dq_jobs=(), dkv_jobs=()):
    T = p1.shape[0]
    B = min(RET_BLOCK, T)
    nb = T // B
    scale = RET_DK ** -0.5
    dmat, qd, kd, sd = _ret_tables(H, B)

    def dq_body(k_ref, v_ref, do_ref, cos_ref, sin_ref, d_ref, qd_ref, kd_ref, sd_ref, dq_ref, s_ref):
        cos, sin = cos_ref[...], sin_ref[...]
        for g in range(G):
            hl = pl.ds(g * LANE, LANE)
            k = _rope128(k_ref[:, hl], cos, sin) * scale
            v, do, s = v_ref[:, hl], do_ref[:, hl], s_ref[g]
            da = _dot(do, v, NT) * d_ref[g]
            dq = _dot(da, k) + _dot(do, s, NT) * qd_ref[g]
            dq_ref[:, hl] = _unrope128(dq, cos, sin).astype(dq_ref.dtype)
            s_ref[g] = sd_ref[g, 0:1, :] * s + _dot(k * kd_ref[g], v, TN)

    G, col, row, per_head, dspec = _ret_specs(H, B, False, nb)
    dq = _ret_call(
        "retention_bwd_dq", dq_body, 9, 1, (p1, p1, dpre, cos2, sin2, dmat, qd, kd, sd),
        [col(H), col(2 * H), col(0), row, row, dspec, per_head(B), per_head(B), per_head(SUBLANE)],
        [pl.BlockSpec((B, G * LANE), lambda h, b: (b, h))], [jax.ShapeDtypeStruct((T, H * RET_DK), BF16)],
        (H // G, nb), G, dq_jobs)

    def dkv_body(q_ref, k_ref, v_ref, do_ref, cos_ref, sin_ref, d_ref, qd_ref, kd_ref, sd_ref, dk_ref, dv_ref, g_ref):
        cos, sin = cos_ref[...], sin_ref[...]
        for g in range(G):
            hl = pl.ds(g * LANE, LANE)
            q = _rope128(q_ref[:, hl], cos, sin)
            k = _rope128(k_ref[:, hl], cos, sin) * scale
            v, do, st = v_ref[:, hl], do_ref[:, hl], g_ref[g]
            a = _dot(q, k, NT) * d_ref[g]
            da = _dot(do, v, NT) * d_ref[g]
            dv = _dot(a, do, TN) + _dot(k * kd_ref[g], st)
            dk = _dot(da, q, TN) + _dot(v, st, NT) * kd_ref[g]
            dv_ref[:, hl] = dv.astype(dv_ref.dtype)
            dk_ref[:, hl] = _unrope128(dk * scale, cos, sin).astype(dk_ref.dtype)
            g_ref[g] = sd_ref[g, 0:1, :] * st + _dot(q * qd_ref[g], do, TN)

    G, col, row, per_head, dspec = _ret_specs(H, B, True, nb)
    out_spec = pl.BlockSpec((B, G * LANE), lambda h, b: (nb - 1 - b, h))
    dkv = _ret_call(
        "retention_bwd_dkv", dkv_body, 10, 2, (p1, p1, p1, dpre, cos2, sin2, dmat, qd, kd, sd),
        [col(0), col(H), col(2 * H), col(0), row, row, dspec, per_head(B), per_head(B), per_head(SUBLANE)],
        [out_spec, out_spec],
        [jax.ShapeDtypeStruct((T, H * RET_DK), BF16), jax.ShapeDtypeStruct((T, H * RET_DV), BF16)],
        (H // G, nb), G, dkv_jobs)
    return dq, dkv


ATT_BLOCK = 512


ATT_ROW_CHUNKS = 2
ATT_BWD_ROW_CHUNKS = 1
ATT_HEADS_PER_STEP = 8
ATT_BWD_HEADS_PER_STEP = 2
LOG2E = 1.4426950408889634


def _chunk_mask(rows, row0, cols):
    qi = (lax.broadcasted_iota(jnp.int32, (rows, cols), 0) + row0) // CHUNK
    kj = lax.broadcasted_iota(jnp.int32, (rows, cols), 1) // CHUNK
    return kj <= qi


def _attention_fwd(qh, kh, vx, H, jobs=()):
    T = qh.shape[0]
    t = min(ATT_BLOCK, T)
    n = T // t
    r = t // ATT_ROW_CHUNKS
    G = min(ATT_HEADS_PER_STEP, H)
    P = MLA_HEAD_PAD
    scale = (MLA_NOPE + MLA_ROPE) ** -0.5
    c2 = scale * LOG2E
    carried = _Carried(jobs, (H // G, n, n))

    def body(*refs):
        (q_ref, k_ref, v_ref), job_ins, (o_ref, ob_ref, lse_ref), job_outs, (m_ref, acc_ref), job_sems = _split_refs(
            refs, 3, len(carried.operands), 3, len(carried.out_shape), 2)
        i, j = pl.program_id(1), pl.program_id(2)
        carried.before(job_ins, job_outs, job_sems)

        @pl.when(j == 0)
        def _():
            m_ref[...] = jnp.full_like(m_ref, -jnp.inf)
            acc_ref[...] = jnp.zeros_like(acc_ref)

        def update(masked):
            for g in range(G):
                hc = pl.ds(g * P, P)
                for ch in range(ATT_ROW_CHUNKS):
                    rs = pl.ds(ch * r, r)
                    kc = (ch + 1) * r if masked else t
                    ks = pl.ds(0, kc)
                    s = _dot(q_ref[rs, hc], k_ref[ks, hc], NT)
                    if masked:
                        s = jnp.where(_chunk_mask(r, ch * r, kc), s, -jnp.inf)
                    m_prev = m_ref[rs, pl.ds(g * LANE, LANE)]
                    m_new = jnp.maximum(m_prev, jnp.max(s, axis=-1, keepdims=True))
                    alpha = jnp.exp2((m_prev - m_new) * c2)
                    p = jnp.exp2((s - jnp.tile(m_new, (1, kc // LANE))) * c2)
                    acc_ref[rs, hc] = jnp.tile(alpha, (1, 2)) * acc_ref[rs, hc] + _dot(p, v_ref[ks, hc])
                    m_ref[rs, pl.ds(g * LANE, LANE)] = m_new

        pl.when(j < i)(functools.partial(update, False))
        pl.when(j == i)(functools.partial(update, True))

        @pl.when(j == n - 1)
        def _():
            for g in range(G):
                acc = acc_ref[:, pl.ds(g * P, P)]
                l = acc[:, MLA_DV:]
                o = acc[:, :MLA_DV] / l
                o_ref[:, pl.ds(g * MLA_DV, MLA_DV)] = o
                ob_ref[:, pl.ds(g * MLA_DV, MLA_DV)] = o.astype(ob_ref.dtype)
                lse_ref[g] = m_ref[:, pl.ds(g * LANE, LANE)] * scale + jnp.log(l)

        carried.after(job_ins, job_outs, job_sems)

    kclamp = lambda i, j: jnp.minimum(j, i)
    o_spec = pl.BlockSpec((t, G * MLA_DV), lambda h, i, j: (i, h))
    mix_spec = pl.BlockSpec((t, G * MLA_DV), lambda h, i, j: (i, H // G + h))
    return pl.pallas_call(
        body, name="attention_fwd", grid=(H // G, n, n),
        in_specs=[pl.BlockSpec((t, G * P), lambda h, i, j: (i, h)),
                  pl.BlockSpec((t, G * P), lambda h, i, j: (kclamp(i, j), h)),
                  pl.BlockSpec((t, G * P), lambda h, i, j: (kclamp(i, j), h))] + carried.in_specs,
        out_specs=[o_spec, mix_spec, pl.BlockSpec((G, t, LANE), lambda h, i, j: (h, i, 0))] + carried.out_specs,
        out_shape=[jax.ShapeDtypeStruct((T, H * MLA_DV), F32), jax.ShapeDtypeStruct((T, 2 * H * MLA_DV), BF16),
                   jax.ShapeDtypeStruct((H, T, LANE), F32)] + carried.out_shape,
        scratch_shapes=[pltpu.VMEM((t, G * LANE), F32), pltpu.VMEM((t, G * P), F32)] + carried.sems,
        input_output_aliases=carried.aliases(3, 3),
        compiler_params=_params(3),
    )(qh, kh, vx, *carried.operands)


def _attention_bwd(qh, kh, vx, o, lse, dmix, H, jobs=()):
    T = qh.shape[0]
    t = min(ATT_BLOCK, T)
    n = T // t
    r = t // ATT_BWD_ROW_CHUNKS
    G = min(ATT_BWD_HEADS_PER_STEP, H)
    P = MLA_HEAD_PAD
    scale = (MLA_NOPE + MLA_ROPE) ** -0.5
    c2 = scale * LOG2E
    carried = _Carried(jobs, (H // G, n, n))

    def body(*refs):
        ((q_ref, k_ref, v_ref, do_ref, o_ref, lse_ref), job_ins, (dq_ref, dkn_ref, dkr_ref, dv_ref), job_outs,
         (dk_acc, dv_acc), job_sems) = _split_refs(refs, 6, len(carried.operands), 4, len(carried.out_shape), 2)
        j, i = pl.program_id(1), pl.program_id(2)
        carried.before(job_ins, job_outs, job_sems)

        @pl.when(jnp.logical_and(j == 0, i == 0))
        def _():
            dq_ref[...] = jnp.zeros_like(dq_ref)

        @pl.when(i == j)
        def _():
            dk_acc[...] = jnp.zeros_like(dk_acc)
            dv_acc[...] = jnp.zeros_like(dv_acc)

        def update(masked):
            for g in range(G):
                hc, hv = pl.ds(g * P, P), pl.ds(g * MLA_DV, MLA_DV)
                for ch in range(ATT_BWD_ROW_CHUNKS):
                    rs = pl.ds(ch * r, r)
                    kc = (ch + 1) * r if masked else t
                    ks = pl.ds(0, kc)
                    k, v = k_ref[ks, hc], v_ref[ks, pl.ds(g * P, MLA_DV)]
                    q, do = q_ref[rs, hc], do_ref[rs, hv]
                    s = _dot(q, k, NT)
                    if masked:
                        s = jnp.where(_chunk_mask(r, ch * r, kc), s, -jnp.inf)
                    p = jnp.exp2(s * c2 - jnp.tile(lse_ref[g, rs, :] * LOG2E, (1, kc // LANE)))
                    dv_acc[ks, hv] += _dot(p, do, TN)
                    dp = _dot(do, v, NT)
                    delta = jnp.sum(do * o_ref[rs, hv], axis=-1, keepdims=True)
                    ds = p * (dp - delta)
                    rows = pl.ds(pl.multiple_of(i * t, t) + ch * r, r)
                    dq_ref[rows, hc] += _dot(ds, k)
                    dk_acc[ks, hc] += _dot(ds, q, TN)

        pl.when(i > j)(functools.partial(update, False))
        pl.when(i == j)(functools.partial(update, True))

        @pl.when(i == n - 1)
        def _():
            for g in range(G):
                dk = dk_acc[:, pl.ds(g * P, P)] * scale
                dkn_ref[:, pl.ds(g * LANE, LANE)] = dk[:, :MLA_NOPE].astype(dkn_ref.dtype)
                dkr_ref[:, pl.ds(g * LANE, LANE)] = dk[:, MLA_NOPE:]
            dv_ref[...] = dv_acc[...].astype(dv_ref.dtype)

        carried.after(job_ins, job_outs, job_sems)

    qclamp = lambda j, i: jnp.maximum(i, j)
    kv_out = pl.BlockSpec((t, G * LANE), lambda h, j, i: (j, h))
    q_in = pl.BlockSpec((t, G * MLA_DV), lambda h, j, i: (qclamp(j, i), h))
    return pl.pallas_call(
        body, name="attention_bwd", grid=(H // G, n, n),
        in_specs=[pl.BlockSpec((t, G * P), lambda h, j, i: (qclamp(j, i), h)),
                  pl.BlockSpec((t, G * P), lambda h, j, i: (j, h)),
                  pl.BlockSpec((t, G * P), lambda h, j, i: (j, h)),
                  pl.BlockSpec((t, G * MLA_DV), lambda h, j, i: (qclamp(j, i), H // G + h)),
                  q_in,
                  pl.BlockSpec((G, t, LANE), lambda h, j, i: (h, qclamp(j, i), 0))] + carried.in_specs,
        out_specs=[pl.BlockSpec((T, G * P), lambda h, j, i: (0, h)), kv_out, kv_out, kv_out] + carried.out_specs,
        out_shape=[jax.ShapeDtypeStruct((T, H * P), F32), jax.ShapeDtypeStruct((T, H * LANE), BF16),
                   jax.ShapeDtypeStruct((T, H * LANE), F32), jax.ShapeDtypeStruct((T, H * LANE), BF16)] + carried.out_shape,
        scratch_shapes=[pltpu.VMEM((t, G * P), F32), pltpu.VMEM((t, G * MLA_DV), F32)] + carried.sems,
        input_output_aliases=carried.aliases(6, 4),
        compiler_params=_params(3),
    )(qh, kh, vx, dmix, o, lse, *carried.operands)


def _adamw_math(w, g, m, v):
    m = ADAM_B1 * m + (1.0 - ADAM_B1) * g
    v = ADAM_B2 * v + (1.0 - ADAM_B2) * jnp.square(g)
    m_hat = m / (1.0 - ADAM_B1 ** ADAM_STEP)
    v_hat = v / (1.0 - ADAM_B2 ** ADAM_STEP)
    delta = -ADAM_LR * (m_hat / (jnp.sqrt(v_hat) + ADAM_EPS) + ADAM_WD * w)
    return delta, m, v


def _adamw(name, chip_sums, my_chip, landed, w, m, v, jobs=()):
    R, C = w.shape
    tr, tc, tiles = _tile_2d(R, C, ADAMW_TILE_ELEMS)
    at = _tile_index(tr, tc, C)
    nl = len(landed)
    assert sum(l.shape[0] for l in landed) == N_CHIP - 1
    carried = _Carried(jobs, (tiles,))

    def body(q_ref, *refs):
        (own_ref, *rest), job_ins, (g_out, d_out, m_out, v_out), job_outs, _, job_sems = _split_refs(
            refs, nl + 4, len(carried.operands), 4, len(carried.out_shape), 0)
        l_refs, (w_ref, m_ref, v_ref) = rest[:nl], rest[nl:]
        carried.before(job_ins, job_outs, job_sems)
        g = own_ref[...].astype(F32)
        for l_ref in l_refs:
            for s in range(l_ref.shape[0]):
                g = g + l_ref[s].astype(F32)
        delta, mn, vn = _adamw_math(w_ref[...], g, m_ref[...], v_ref[...])
        g_out[...] = g
        d_out[...] = delta
        m_out[...] = mn
        v_out[...] = vn
        carried.after(job_ins, job_outs, job_sems)

    spec = pl.BlockSpec((tr, tc), lambda i, q_ref: at(i))
    own_spec = pl.BlockSpec((None, tr, tc), lambda i, q_ref: (q_ref[0], *at(i)))
    l_specs = [pl.BlockSpec((l.shape[0], tr, tc), lambda i, q_ref: (0, *at(i))) for l in landed]
    return pl.pallas_call(
        body, name=name,
        grid_spec=pltpu.PrefetchScalarGridSpec(
            num_scalar_prefetch=1, grid=(tiles,),
            in_specs=[own_spec] + l_specs + [spec, spec, spec] + carried.in_specs,
            out_specs=[spec] * 4 + carried.out_specs, scratch_shapes=carried.sems),
        out_shape=[jax.ShapeDtypeStruct((R, C), F32)] * 4 + carried.out_shape,
        input_output_aliases={1 + i: o for i, o in carried.aliases(nl + 4, 4).items()},
        compiler_params=_params(1),
    )(my_chip, chip_sums, *landed, w, m, v, *carried.operands)


def _adamw_small(landed, w, m, v):
    P = w.shape[1]

    def body(l_ref, w_ref, m_ref, v_ref, g_out, d_out, m_out, v_out, loss_out):
        acc = l_ref[0]
        for s in range(1, N_DEV):
            acc = acc + l_ref[s]
        total = jnp.sum(acc, axis=0, keepdims=True)
        g = total[:, :P]
        delta, mn, vn = _adamw_math(w_ref[...], g, m_ref[...], v_ref[...])
        g_out[...] = g
        d_out[...] = delta
        m_out[...] = mn
        v_out[...] = vn
        loss_out[...] = total[:, P:]

    return pl.pallas_call(
        body, name="adamw_replicated",
        out_shape=[jax.ShapeDtypeStruct((1, P), F32)] * 4 + [jax.ShapeDtypeStruct((1, LANE), F32)],
    )(landed, w, m, v)


def _rope_rows():
    inv_r = ROPE_BASE ** (-jnp.arange(0, RET_DK, 2, dtype=F32) / RET_DK)
    inv_m = ROPE_BASE ** (-jnp.arange(0, MLA_ROPE, 2, dtype=F32) / MLA_ROPE)
    h = MLA_ROPE // 2
    one, zero = jnp.ones((h,), F32), jnp.zeros((h,), F32)
    pad = jnp.zeros((LANE - MLA_ROPE,), F32)
    rows = [jnp.concatenate([inv_r, inv_r]), jnp.concatenate([-jnp.ones((RET_DK // 2,), F32), jnp.ones((RET_DK // 2,), F32)]),
            jnp.concatenate([inv_m, inv_m, pad]), jnp.concatenate([one, one, pad]),
            jnp.concatenate([zero, one, pad]), jnp.concatenate([-one, zero, pad])]
    return jnp.stack(rows + [jnp.zeros((LANE,), F32)] * (SUBLANE - len(rows)))


def _prep_tile(xt, p, rows):
    ang2, angp = p * rows[0:1], p * rows[2:3]
    sp = jnp.sin(angp)
    return (xt, jnp.cos(ang2), jnp.sin(ang2) * rows[1:2], jnp.cos(angp) * rows[3:4], sp * rows[4:5], sp * rows[5:6])


def kernel(x, positions, w_in, q_norm_g, w_uq, kv_norm_g, w_uk, w_uv, ret_gn_g, w_out, ln1_g, ln1_b, w_up, w_down, ln2_g, ln2_b, loss_target, m_w_in, m_q_norm_g, m_w_uq, m_kv_norm_g, m_w_uk, m_w_uv, m_ret_gn_g, m_w_out, m_ln1_g, m_ln1_b, m_w_up, m_w_down, m_ln2_g, m_ln2_b, v_w_in, v_q_norm_g, v_w_uq, v_kv_norm_g, v_w_uk, v_w_uv, v_ret_gn_g, v_w_out, v_ln1_g, v_ln1_b, v_w_up, v_w_down, v_ln2_g, v_ln2_b):
    T, D = x.shape[1], x.shape[2]
    H = D // 256
    RW = H * RET_DV
    x = x[0]
    tgt = loss_target[0]
    alpha = DEEPNORM_ALPHA
    in_width = w_in.shape[2] * N_DEV
    mla_in = in_width - 4 * RW
    mla_in_pad = -(-mla_in // 512) * 512
    f_uq = MLA_NOPE + MLA_ROPE

    names = ["w_in", "w_uq", "w_uk", "w_uv", "w_out", "w_up", "w_down"]
    big = dict(zip(names, [w_in[0].T, w_uq[0], w_uk[0], w_uv[0], w_out[0], w_up[0], w_down[0]]))
    kind = dict(zip(names, ["blk", "blk", "cols", "cols", "rows", "cols", "rows"]))
    half = {n: w.astype(BF16) for n, w in big.items()}

    def gather(ns, **parts):
        return _GatherJob([half[n] for n in ns], [kind[n] for n in ns], 0.55, 0.9, **parts)

    tr = min(256, T)
    xb, cos2, sin2, cosp, sina, sinb, g_in = _rowwise(
        "prep", _prep_tile, [(x, D, 0), (positions[0].astype(F32)[:, None], 1, 0)], [_rope_rows()],
        [(D, BF16)] + [(LANE, F32)] * 5, [], tr, jobs=[gather(["w_in"])])
    win = g_in.reshape(in_width, D)
    w1, n1 = win, 4 * RW
    w2 = jnp.pad(win[4 * RW:], ((0, mla_in_pad - mla_in), (0, 0)))

    first, second, near, far = (0, 2), (1, 2), (CHIP_X, CHIP_Y), (CHIP_DIAGONAL,)
    p1, g_uq, wuk, wuv, wout = _matmul("proj_ret", xb, w1, tb=True, n=n1,
                                       jobs=[gather(["w_uq", "w_uk", "w_uv", "w_out"])])
    wuq = jnp.transpose(g_uq, (1, 0, 2)).reshape(MLA_Q_RANK, H, f_uq)
    wuq = jnp.pad(wuq, ((0, 0), (0, 0), (0, MLA_HEAD_PAD - f_uq))).reshape(MLA_Q_RANK, H * MLA_HEAD_PAD)
    wukv = jnp.concatenate([wuk, wuv], axis=1)
    c_kv, c_kr = MLA_Q_RANK, MLA_Q_RANK + MLA_KV_RANK

    def mla_prep(p, cp, sa, sb, gq, gkv):
        return p, _rms(p[:, :c_kv], gq), _rms(p[:, c_kv:c_kr], gkv), _rope64(p[:, c_kr:c_kr + LANE], cp, sa, sb)

    p2, cqn, ckvn, krope = _matmul(
        "proj_mla", xb, w2, tb=True, tn=mla_in_pad, out_dtypes=(F32, BF16, BF16, BF16), epilogue=mla_prep,
        row_extras=(cosp, sina, sinb), const_extras=(q_norm_g, kv_norm_g),
        out_widths=[mla_in_pad, MLA_Q_RANK, MLA_KV_RANK, LANE])
    def q_heads(acc, cp, sa, sb):
        parts = []
        for o in range(0, acc.shape[1], MLA_HEAD_PAD):
            parts += [acc[:, o:o + MLA_NOPE], _rope64(acc[:, o + MLA_NOPE:o + MLA_HEAD_PAD], cp, sa, sb)]
        return (jnp.concatenate(parts, axis=1),)

    def paired(acc, second):
        parts = []
        for o in range(0, acc.shape[1], LANE):
            parts += [acc[:, o:o + LANE], second.astype(acc.dtype)]
        return (jnp.concatenate(parts, axis=1),)

    qh = _matmul("mla_q", cqn, wuq, out_dtypes=(BF16,), epilogue=q_heads, row_extras=(cosp, sina, sinb))
    heads_tile = min(RW, 1024)
    kh = _matmul("mla_k", ckvn, wuk, out_dtypes=(BF16,), epilogue=paired, row_extras=(krope,), tn=heads_tile,
                 out_widths=[2 * heads_tile])
    vx = _matmul("mla_v", ckvn, wuv, out_dtypes=(BF16,), tn=heads_tile, out_widths=[2 * heads_tile],
                 epilogue=lambda acc: paired(acc, jnp.ones((acc.shape[0], MLA_DV), F32)))
    whole = (0, 1)
    att_o, mix_half, lse, wup_a = _attention_fwd(
        qh, kh, vx, H, jobs=[gather(["w_up"], pieces=[(first, near, True, 0.45), (second, near, True, 0.75)])])
    ret_pre, mixin, wup = _retention_fwd(
        p1, cos2, sin2, ret_gn_g, mix_half, H,
        jobs=[gather(["w_up"], pieces=[(first, far, False, 0.55), (second, far, False, 0.9)], base=[wup_a])])
    mix, wdown_a = _matmul("mix_out", mixin, wout,
                           jobs=[gather(["w_down"], pieces=[((0, 4), near, True, 0.5), ((1, 4), near, True, 0.9)])])

    def ln1_fwd(xt, mt, g, b):
        z = alpha * xt + mt
        xhat, _ = _ln_stats(z)
        y = xhat * g + b
        return z, y, y

    z1, x1, x1b, wdown_b = _rowwise("ln1", ln1_fwd, [(x, D, 0), (mix, D, 0)], [ln1_g, ln1_b],
                                    [(D, F32), (D, F32), (D, BF16)], [], tr,
                                    jobs=[gather(["w_down"], pieces=[((2, 4), near, True, 0.5), ((3, 4), near, True, 0.9)],
                                                 base=[wdown_a])])

    def relu2(acc):
        r = jnp.maximum(acc, 0.0)
        return r * r, r

    a2b, rb, wdown = _matmul("mlp_up", x1b, wup, out_dtypes=(BF16, BF16), epilogue=relu2,
                             jobs=[gather(["w_down"], pieces=[(whole, far, False, 0.6)], base=[wdown_b])])
    hmlp = _matmul("mlp_down", a2b, wdown)

    def ln2_loss(x1t, ht, tt, g, b):
        xhat, rstd = _ln_stats(alpha * x1t + ht)
        err = xhat * g + b - tt
        dy = err / D
        dz = _ln_bwd(dy, xhat, rstd, g)
        lrow = 0.5 * jnp.mean(jnp.square(err), axis=-1, keepdims=True)
        return dz, dz, dy * xhat, dy, jnp.broadcast_to(lrow, (lrow.shape[0], LANE))

    dz2, dz2b, dg2, db2, lpart = _rowwise("ln2_loss", ln2_loss, [(x1, D, 0), (hmlp, D, 0), (tgt, D, 0)], [ln2_g, ln2_b],
                                          [(D, F32), (D, BF16)], [D, D, LANE], tr)

    def pair(grads):
        ns = list(grads)
        return _PairJob([grads[n] for n in ns], [kind[n] for n in ns], [big[n].shape for n in ns])

    my_core = lax.axis_index("c").astype(jnp.int32).reshape(1)
    my_chip = (2 * lax.axis_index("x") + lax.axis_index("y")).astype(jnp.int32).reshape(1)
    chip_sums = {}

    def pair_sums(grads, theirs):
        for n, t in zip(grads, theirs):
            chip_sums[n] = _pair_sum("pair_sum_" + n, grads[n], kind[n], big[n].shape, t, my_core)
        return {n: chip_sums[n] for n in grads}

    def chip(sums, rels=CHIP_RELS, by_source=None):
        ns = list(sums)
        return _ChipJob([sums[n] for n in ns], [big[n].shape for n in ns], rels, by_source)

    landed = {}
    da = _matmul("mlp_down_dx", dz2b, wdown, tb=True, out_dtypes=(BF16,), extras=(rb,),
                 epilogue=lambda acc, r: (acc * (2.0 * r.astype(F32)),))
    g_down = {"w_down": _matmul("mlp_down_dw", a2b, dz2b, ta=True, out_dtypes=(BF16,))}
    g_wup, *pr = _matmul("mlp_up_dw", x1b, da, ta=True, out_dtypes=(BF16,), jobs=[pair(g_down)])
    sums_down = pair_sums(g_down, pr)
    g_up = {"w_up": g_wup}
    dx1m, l_a, *pr = _matmul("mlp_up_dx", da, wup, tb=True, jobs=[chip(sums_down, CHIP_RELS[:2]), pair(g_up)])
    sums_up = pair_sums(g_up, pr)

    def ln1_bwd(dm, dzz, z, g):
        xhat, rstd = _ln_stats(z)
        dy = dm + alpha * dzz
        dz = _ln_bwd(dy, xhat, rstd, g)
        return dz, dz, dy * xhat, dy

    dz1, dz1b, dg1, db1 = _rowwise("ln1_bwd", ln1_bwd, [(dx1m, D, 0), (dz2, D, 0), (z1, D, 0)], [ln1_g],
                                   [(D, F32), (D, BF16)], [D, D], tr)
    g_wout = _matmul("mix_out_dw", mixin, dz1b, ta=True, out_dtypes=(BF16,))
    dmix = _matmul("mix_out_dx", dz1b, wout, tb=True)

    dqh, dkn, dkr, dvv, l_b, l_c = _attention_bwd(qh, kh, vx, att_o, lse, dmix, H,
                                                  jobs=[chip(sums_down, CHIP_RELS[2:]), chip(sums_up)])
    landed["w_down"], landed["w_up"] = [l_a, l_b], [l_c]
    att_scale = (MLA_NOPE + MLA_ROPE) ** -0.5

    def mla_heads_bwd(dq, dkrh, cp, sa, sb):
        parts, acc = [], dkrh[:, :LANE]
        dq = dq * att_scale
        for h in range(H):
            o = h * MLA_HEAD_PAD
            parts += [dq[:, o:o + MLA_NOPE], _unrope64(dq[:, o + MLA_NOPE:o + MLA_HEAD_PAD], cp, sa, sb)]
            if h:
                acc = acc + dkrh[:, h * LANE:(h + 1) * LANE]
        return jnp.concatenate(parts, axis=1), _unrope64(acc, cp, sa, sb)

    dqb, dkr128 = _rowwise(
        "mla_heads_bwd", mla_heads_bwd,
        [(dqh, H * MLA_HEAD_PAD, 0), (dkr, H * LANE, 0), (cosp, LANE, 0), (sina, LANE, 0), (sinb, LANE, 0)],
        [], [(H * MLA_HEAD_PAD, BF16), (LANE, F32)], [], tr)
    dcqn = _matmul("mla_q_dx", dqb, wuq, tb=True)
    g_wuq = _matmul("mla_q_dw", cqn, dqb, ta=True, out_dtypes=(BF16,))
    dkvb = jnp.concatenate([dkn, dvv], axis=1)
    dckvn = _matmul("mla_kv_dx", dkvb, wukv, tb=True)
    g_wukv = _matmul("mla_kv_dw", ckvn, dkvb, ta=True, out_dtypes=(BF16,))

    def rms_bwd(c, dy, g):
        rstd = lax.rsqrt(jnp.mean(jnp.square(c), axis=-1, keepdims=True) + EPS)
        dyg = dy * g
        dc = rstd * (dyg - c * (rstd * rstd) * jnp.mean(dyg * c, axis=-1, keepdims=True))
        return dc, dy * c * rstd

    def mla_prep_bwd(p, dq_, dkv_, dkr_, gq, gkv):
        dcq, dgq_ = rms_bwd(p[:, :c_kv], dq_, gq)
        dckv, dgkv_ = rms_bwd(p[:, c_kv:c_kr], dkv_, gkv)
        pad = jnp.zeros((p.shape[0], mla_in_pad - c_kr - LANE), F32)
        return jnp.concatenate([dcq, dckv, dkr_, pad], axis=1), dgq_, dgkv_

    d_p2, dgq, dgkv = _rowwise(
        "mla_prep_bwd", mla_prep_bwd, [(p2, mla_in_pad, 0), (dcqn, MLA_Q_RANK, 0), (dckvn, MLA_KV_RANK, 0), (dkr128, LANE, 0)],
        [q_norm_g, kv_norm_g], [(mla_in_pad, BF16)], [MLA_Q_RANK, MLA_KV_RANK], tr)

    def gate_bwd(pre, gate, dout, gn):
        dpre, dgate, dgn = [], [], []
        for h in range(H):
            sl = slice(h * RET_DV, (h + 1) * RET_DV)
            nrm, rstd = _ln_stats(pre[:, sl])
            gate_h, do_h, gn_h = gate[:, sl], dout[:, sl], gn[:, sl]
            sg = jax.nn.sigmoid(gate_h)
            silu = gate_h * sg
            dgate.append(do_h * (nrm * gn_h) * (sg * (1.0 + gate_h * (1.0 - sg))))
            dn = do_h * silu
            dgn.append(dn * nrm)
            dpre.append(_ln_bwd(dn, nrm, rstd, gn_h))
        return jnp.concatenate(dpre, axis=1), jnp.concatenate(dgate, axis=1), jnp.concatenate(dgn, axis=1)

    dpre, drg, dgn = _rowwise("ret_gate_bwd", gate_bwd, [(ret_pre, RW, 0), (p1, RW, 3), (dmix, RW, 0)], [ret_gn_g],
                              [(RW, F32), (RW, BF16)], [RW], tr)
    g_uq = g_wuq.reshape(MLA_Q_RANK, H, MLA_HEAD_PAD)[:, :, :f_uq].reshape(MLA_Q_RANK, N_DEV, H * f_uq // N_DEV)
    mid_grads = {"w_uq": jnp.transpose(g_uq, (1, 0, 2)), "w_uk": g_wukv[:, :RW], "w_uv": g_wukv[:, RW:], "w_out": g_wout}
    (drq, *pr), (drk, drv) = _retention_bwd(p1, dpre, cos2, sin2, H, dq_jobs=[pair(mid_grads)])
    sums_mid = pair_sums(mid_grads, pr)
    d_p1 = jnp.concatenate([drq, drk, drv, drg], axis=1)

    g_w1, *l_a = _matmul("proj_ret_dw", d_p1, xb, ta=True, out_dtypes=(BF16,), jobs=[chip(sums_mid)])
    for n, a_ in zip(mid_grads, l_a):
        landed[n] = [a_]
    g_w2 = _matmul("proj_mla_dw", d_p2, xb, ta=True, out_dtypes=(BF16,))

    g_win = {"w_in": jnp.concatenate([g_w1, g_w2[:mla_in]], axis=0).reshape(N_DEV, in_width // N_DEV, D)}
    small = jnp.concatenate([dgq, dgkv, dgn, dg1, db1, dg2, db2, lpart], axis=1)
    dx_a, *pr = _matmul("proj_mla_dx", d_p2, w2, extras=(dz1,), epilogue=lambda acc, d: (acc + alpha * d,), jobs=[pair(g_win)])
    sums_in = pair_sums(g_win, pr)
    grad_x, l_a, small_landed = _matmul("proj_ret_dx", d_p1, w1, extras=(dx_a,), epilogue=lambda acc, d: (acc + d,),
                                        jobs=[chip(sums_in, CHIP_RELS[:2], by_source=small)])

    moments = dict(zip(names, zip([m_w_in, m_w_uq, m_w_uk, m_w_uv, m_w_out, m_w_up, m_w_down],
                                  [v_w_in, v_w_uq, v_w_uk, v_w_uv, v_w_out, v_w_up, v_w_down])))
    res = {}

    def adamw(n, jobs=()):
        as_held = (lambda a: a[0].T) if n == "w_in" else (lambda a: a[0])
        as_given = (lambda r: r.T[None]) if n == "w_in" else (lambda r: r[None])
        out = _adamw("adamw_" + n, chip_sums[n], my_chip, landed[n], big[n], as_held(moments[n][0]), as_held(moments[n][1]),
                     jobs=jobs)
        res[n] = [as_given(r) for r in out[:4]]
        return out[4:]

    (l_b,) = adamw("w_out", jobs=[chip(sums_in, CHIP_RELS[2:])])
    landed["w_in"] = [l_a, l_b]
    for n in names:
        if n != "w_out":
            adamw(n)

    small_names = ["q_norm_g", "kv_norm_g", "ret_gn_g", "ln1_g", "ln1_b", "ln2_g", "ln2_b"]
    small_w = [q_norm_g, kv_norm_g, ret_gn_g, ln1_g, ln1_b, ln2_g, ln2_b]
    small_m = [m_q_norm_g, m_kv_norm_g, m_ret_gn_g, m_ln1_g, m_ln1_b, m_ln2_g, m_ln2_b]
    small_v = [v_q_norm_g, v_kv_norm_g, v_ret_gn_g, v_ln1_g, v_ln1_b, v_ln2_g, v_ln2_b]
    cat = lambda arrs: jnp.concatenate(arrs, axis=1)
    *sres, loss_lanes = _adamw_small(small_landed, cat(small_w), cat(small_m), cat(small_v))
    loss = loss_lanes[0, 0]
    off = 0
    for n, w in zip(small_names, small_w):
        res[n] = [r[:, off:off + w.shape[1]] for r in sres]
        off += w.shape[1]

    order = ["w_in", "q_norm_g", "w_uq", "kv_norm_g", "w_uk", "w_uv", "ret_gn_g", "w_out", "ln1_g", "ln1_b",
             "w_up", "w_down", "ln2_g", "ln2_b"]
    outs = [loss, grad_x[None]]
    for k in range(4):
        outs += [res[n][k] for n in order]
    return tuple(outs)
```

```python
import functools

import numpy as np

import jax
import jax.numpy as jnp
from jax import lax
from jax.experimental import pallas as pl
from jax.experimental.pallas import tpu as pltpu

F32 = jnp.float32
BF16 = jnp.bfloat16

CHUNK = 64
RET_DK = 128
RET_DV = 128
MLA_NOPE = 128
MLA_ROPE = 64
MLA_DV = 128
MLA_Q_RANK = 768
MLA_KV_RANK = 512
MLA_HEAD_PAD = 256
ROPE_BASE = 10000.0
EPS = 1e-5
DEPTH = 1
DEEPNORM_ALPHA = (2.0 * DEPTH) ** 0.25
ADAM_LR = 0.001
ADAM_B1 = 0.9
ADAM_B2 = 0.999
ADAM_EPS = 1e-08
ADAM_WD = 0.01
ADAM_STEP = 10

N_DEV = 8
LANE = 128
SUBLANE = 8
VMEM_LIMIT = 48 * 1024 * 1024
ADAMW_TILE_ELEMS = 128 * 1024
PAIR_TILE_ELEMS = 1024 * 1024
DW_TK = 4096

MESH = pl.DeviceIdType.MESH
ANY = pl.BlockSpec(memory_space=pl.ANY)

NN = (((1,), (0,)), ((), ()))
NT = (((1,), (1,)), ((), ()))
TN = (((0,), (0,)), ((), ()))


def _dot(a, b, dims=NN):
    return lax.dot_general(a.astype(BF16), b.astype(BF16), dims, preferred_element_type=F32)


def _params(n_axes):
    return pltpu.CompilerParams(dimension_semantics=("arbitrary",) * n_axes, vmem_limit_bytes=VMEM_LIMIT)


def _tile(n, want):
    if n <= want:
        return n
    t = want
    while t >= LANE:
        if n % t == 0:
            return t
        t -= LANE
    return n


def _my_place():
    return lax.axis_index("x"), lax.axis_index("y"), lax.axis_index("c")


def _flat(px, py, pc):
    return 4 * px + 2 * py + pc


def _block_of(ref, kind, idx, rows, cols, sub=None):
    lo, n = (0, rows) if sub is None else sub
    if kind == "blk":
        return ref.at[idx, pl.ds(lo, n), :]
    if kind == "rows":
        return ref.at[pl.ds(pl.multiple_of(idx * rows + lo, n), n), :]
    return ref.at[pl.ds(lo, n), pl.ds(pl.multiple_of(idx * cols, cols), cols)]


CHIP_X, CHIP_Y, CHIP_DIAGONAL = 0, 1, 2
ALL_CHIPS = (CHIP_X, CHIP_Y, CHIP_DIAGONAL)


class _GatherJob:
    def __init__(self, shards, kinds, mid, late, pieces=(((0, 1), ALL_CHIPS, True),), base=None):
        nw = len(shards)
        self.nw, self.kinds = nw, list(kinds)
        self.pieces = [tuple(p) if len(p) == 4 else (*p, late) for p in pieces]
        for _, chips, _, _ in self.pieces:
            assert CHIP_DIAGONAL not in chips or len(chips) in (1, 3)
        self.shapes = [s.shape for s in shards]
        self.ins = list(shards) + (list(base) if base is not None else [])
        self.alias = [(nw + w, w) for w in range(nw)] if base is not None else []
        self.out_shape = []
        for s, kind in zip(shards, kinds):
            r, c = s.shape
            shape = {"blk": (N_DEV, r, c), "rows": (N_DEV * r, c), "cols": (r, N_DEV * c)}[kind]
            self.out_shape.append(jax.ShapeDtypeStruct(shape, s.dtype))
        n = 7 * nw * len(self.pieces)
        self.sems = [pltpu.SemaphoreType.DMA((n,)), pltpu.SemaphoreType.DMA((n,)), pltpu.SemaphoreType.DMA((n,))]
        whens = sorted({p[3] for p in self.pieces})
        self.phases = ([(0.0, self._start), (mid, self._relay)] + [(f, functools.partial(self._pass_on, f)) for f in whens]
                       + [(1.0, self._finish)])

    def _each(self, src, dst, send_sems, recv_sems, local_sems):
        x, y, c = _my_place()
        all_chips = [(1 - x, y), (x, 1 - y), (1 - x, 1 - y)]
        out = []
        for w in range(self.nw):
            r, cc = self.shapes[w]
            for p, ((k_of, of), chips, own, when) in enumerate(self.pieces):
                sub = (k_of * (r // of), r // of)
                s0 = 7 * (w * len(self.pieces) + p)
                mine = src[w].at[pl.ds(*sub), :]

                def place(dev, w=w, r=r, cc=cc, sub=sub):
                    return _block_of(dst[w], self.kinds[w], _flat(*dev), r, cc, sub)

                def copy(k, block, to, from_src=False, place=place, mine=mine, s0=s0):
                    return pltpu.make_async_remote_copy(
                        src_ref=mine if from_src else place(block), dst_ref=place(block),
                        send_sem=send_sems.at[s0 + k], recv_sem=recv_sems.at[s0 + k], device_id=to, device_id_type=MESH)

                relay = functools.partial(copy, 1 + CHIP_DIAGONAL, (x ^ (1 - c), y ^ c, c), (x ^ c, y ^ (1 - c), c))
                local = functools.partial(pltpu.make_async_copy, mine, place((x, y, c)), local_sems.at[s0])
                near = [(j, all_chips[j]) for j in chips if j != CHIP_DIAGONAL]
                out.append((near, CHIP_DIAGONAL in chips, own, copy, relay, local, when))
        return (x, y, c), (x, y, 1 - c), all_chips[CHIP_DIAGONAL], out

    def _start(self, *refs):
        me, sibling, diagonal, each = self._each(*refs)
        for near, far, own, copy, relay, local, when in each:
            if own:
                local().start()
                copy(0, me, sibling, from_src=True).start()
            for j, chip in near:
                copy(1 + j, me, (*chip, me[2]), from_src=True).start()
            if far and not near:
                relay().start()

    def _landed(self, near, copy, me, sibling):
        for j, chip in near:
            copy(1 + j, (*chip, me[2]), me).wait_recv()
            copy(4 + j, (*chip, me[2]), sibling).start()

    def _relay(self, *refs):
        me, sibling, diagonal, each = self._each(*refs)
        for near, far, own, copy, relay, local, when in each:
            if far and near:
                self._landed(near, copy, me, sibling)
                relay().start()

    def _pass_on(self, frac, *refs):
        me, sibling, diagonal, each = self._each(*refs)
        for near, far, own, copy, relay, local, when in each:
            if when == frac:
                self._landed([(CHIP_DIAGONAL, diagonal)] if far else near, copy, me, sibling)

    def _finish(self, *refs):
        me, sibling, diagonal, each = self._each(*refs)
        for near, far, own, copy, relay, local, when in each:
            if own:
                copy(0, sibling, me).wait_recv()
            for j, chip in near + [(CHIP_DIAGONAL, diagonal)] * int(far):
                copy(4 + j, (*chip, 1 - me[2]), me).wait_recv()
        for near, far, own, copy, relay, local, when in each:
            if own:
                copy(0, me, sibling, from_src=True).wait_send()
                local().wait()
            for j, chip in near:
                copy(1 + j, me, (*chip, me[2]), from_src=True).wait_send()
                copy(4 + j, (*chip, me[2]), sibling).wait_send()
            if far:
                relay().wait_send()
                copy(4 + CHIP_DIAGONAL, (*diagonal, me[2]), sibling).wait_send()


REL_ALL = [(1, 0, 0), (1, 0, 1), (0, 1, 0), (0, 1, 1), (1, 1, 0), (1, 1, 1), (0, 0, 1)]
N_CHIP = 4


class _PairJob:
    def __init__(self, grads, kinds, shard_shapes):
        self.ins = list(grads)
        self.kinds, self.shard_shapes, self.nw = list(kinds), list(shard_shapes), len(grads)
        self.out_shape = [jax.ShapeDtypeStruct((N_CHIP,) + tuple(s), g.dtype) for g, s in zip(grads, shard_shapes)]
        n = N_CHIP * self.nw
        self.sems = [pltpu.SemaphoreType.DMA((n,)), pltpu.SemaphoreType.DMA((n,))]
        self.phases = [(0.0, self._start), (1.0, self._finish)]

    def _copies(self, src, theirs, send_sems, recv_sems):
        x, y, c = _my_place()
        remote = []
        for w in range(self.nw):
            r, cc = self.shard_shapes[w]
            for q in range(N_CHIP):
                s = N_CHIP * w + q
                remote.append(pltpu.make_async_remote_copy(
                    src_ref=_block_of(src[w], self.kinds[w], 2 * q + (1 - c), r, cc), dst_ref=theirs[w].at[q],
                    send_sem=send_sems.at[s], recv_sem=recv_sems.at[s], device_id=(x, y, 1 - c), device_id_type=MESH))
        return remote

    def _start(self, *refs):
        for cp in self._copies(*refs):
            cp.start()

    def _finish(self, *refs):
        for cp in self._copies(*refs):
            cp.wait()


CHIP_RELS = [(1, 0), (0, 1), (1, 1)]


class _ChipJob:
    def __init__(self, chip_sums, shard_shapes, rels=CHIP_RELS, by_source=None, piece=(0, 1), base=None):
        nw = len(chip_sums)
        self.ins = list(chip_sums) + ([by_source] if by_source is not None else []) + (list(base) if base is not None else [])
        self.nw, self.has_src, self.rels, self.piece = nw, by_source is not None, list(rels), piece
        self.rows = [s[0] for s in shard_shapes]
        self.alias = [(nw + int(self.has_src) + w, w) for w in range(nw)] if base is not None else []
        self.out_shape = [jax.ShapeDtypeStruct((len(rels),) + tuple(s), a.dtype) for a, s in zip(chip_sums, shard_shapes)]
        if self.has_src:
            self.out_shape.append(jax.ShapeDtypeStruct((N_DEV,) + by_source.shape, by_source.dtype))
        n = len(rels) * self.nw + len(REL_ALL) * int(self.has_src)
        self.sems = [pltpu.SemaphoreType.DMA((n,)), pltpu.SemaphoreType.DMA((n,)), pltpu.SemaphoreType.DMA((1,))]
        self.phases = [(0.0, self._start), (1.0, self._finish)]

    def _copies(self, src, land, send_sems, recv_sems, local_sems):
        x, y, c = _my_place()
        nw = self.nw

        def remote_copy(s, src_ref, dst_ref, peer):
            return pltpu.make_async_remote_copy(src_ref=src_ref, dst_ref=dst_ref, send_sem=send_sems.at[s],
                                                recv_sem=recv_sems.at[s], device_id=peer, device_id_type=MESH)

        def rows(w):
            n = self.rows[w] // self.piece[1]
            return pl.ds(self.piece[0] * n, n)

        local, remote = [], []
        for k, (fx, fy) in enumerate(self.rels):
            px, py = x ^ fx, y ^ fy
            remote += [remote_copy(nw * k + w, src[w].at[2 * px + py, rows(w), :], land[w].at[k, rows(w), :], (px, py, c))
                       for w in range(nw)]
        if self.has_src:
            my_idx = _flat(x, y, c)
            local.append(pltpu.make_async_copy(src[nw], land[nw].at[my_idx], local_sems.at[0]))
            remote += [remote_copy(nw * len(self.rels) + k, src[nw], land[nw].at[my_idx], (x ^ fx, y ^ fy, c ^ fc))
                       for k, (fx, fy, fc) in enumerate(REL_ALL)]
        return local, remote

    def _start(self, *refs):
        local, remote = self._copies(*refs)
        for cp in local + remote:
            cp.start()

    def _finish(self, *refs):
        local, remote = self._copies(*refs)
        for cp in remote + local:
            cp.wait()


def _row_tile(R, C, elems=ADAMW_TILE_ELEMS):
    tr = 16
    while tr * 2 * C <= elems and R % (tr * 2) == 0:
        tr *= 2
    return tr if R % tr == 0 else R


BF16_ROWS = 16


def _tile_2d(R, C, elems):
    if R % BF16_ROWS == 0:
        tr = _row_tile(R, C, elems)
        return tr, C, R // tr
    tc = C
    while tc % (2 * LANE) == 0 and R * tc > elems:
        tc //= 2
    return R, tc, C // tc


def _tile_index(tr, tc, C):
    return (lambda i: (i, 0)) if tc == C else (lambda i: (0, i))


def _part_spec(kind, tr, tc, R, C, n_lead):
    at = _tile_index(tr, tc, C)

    def dev(ids, pref):
        return 2 * ids[0] + pref[0] if n_lead == 2 else pref[0]

    def index(*args):
        ids, pref = args[:n_lead], args[n_lead]
        ri, ci = at(ids[-1])
        if kind == "blk":
            return dev(ids, pref), ri, ci
        if kind == "rows":
            return dev(ids, pref) * (R // tr) + ri, ci
        return ri, dev(ids, pref) * (C // tc) + ci

    return pl.BlockSpec((None, tr, tc) if kind == "blk" else (tr, tc), index)


def _pair_sum(name, grad, kind, shard_shape, theirs, c):
    R, C = shard_shape
    tr, tc, tiles = _tile_2d(R, C, PAIR_TILE_ELEMS)
    at = _tile_index(tr, tc, C)

    def body(c_ref, g_ref, t_ref, o_ref):
        o_ref[...] = (g_ref[...].astype(F32) + t_ref[...].astype(F32)).astype(o_ref.dtype)

    slot = pl.BlockSpec((None, tr, tc), lambda q, i, c_ref: (q, *at(i)))
    return pl.pallas_call(
        body, name=name,
        grid_spec=pltpu.PrefetchScalarGridSpec(
            num_scalar_prefetch=1, grid=(N_CHIP, tiles),
            in_specs=[_part_spec(kind, tr, tc, R, C, 2), slot], out_specs=slot),
        out_shape=jax.ShapeDtypeStruct(theirs.shape, theirs.dtype), compiler_params=_params(2))(c, grad, theirs)


class _Carried:
    def __init__(self, jobs, grid):
        self.jobs = list(jobs)
        self.steps = int(np.prod(grid))
        self.grid = tuple(grid)
        self.operands = [a for j in self.jobs for a in j.ins]
        self.out_shape = [o for j in self.jobs for o in j.out_shape]
        self.sems = [s for j in self.jobs for s in j.sems]
        self.in_specs = [ANY] * len(self.operands)
        self.out_specs = [ANY] * len(self.out_shape)

    def aliases(self, n_in, n_out):
        out, i0, o0 = {}, n_in, n_out
        for job in self.jobs:
            for ji, jo in getattr(job, "alias", []):
                out[i0 + ji] = o0 + jo
            i0, o0 = i0 + len(job.ins), o0 + len(job.out_shape)
        return out

    def _step(self):
        lin = pl.program_id(0)
        for a in range(1, len(self.grid)):
            lin = lin * self.grid[a] + pl.program_id(a)
        return lin

    def _run(self, in_refs, out_refs, sem_refs, last):
        lin = self._step()
        for job in self.jobs:
            ins, in_refs = in_refs[:len(job.ins)], in_refs[len(job.ins):]
            outs, out_refs = out_refs[:len(job.out_shape)], out_refs[len(job.out_shape):]
            sems, sem_refs = sem_refs[:len(job.sems)], sem_refs[len(job.sems):]
            for frac, fn in job.phases:
                if (frac >= 1.0) != last:
                    continue
                at = min(self.steps - 1, int(frac * (self.steps - 1)))
                pl.when(lin == at)(functools.partial(fn, ins, outs, *sems))

    def before(self, in_refs, out_refs, sem_refs):
        self._run(in_refs, out_refs, sem_refs, last=False)

    def after(self, in_refs, out_refs, sem_refs):
        self._run(in_refs, out_refs, sem_refs, last=True)


def _split_refs(refs, n_in, n_job_in, n_out, n_job_out, n_scratch):
    cuts = np.cumsum([0, n_in, n_job_in, n_out, n_job_out, n_scratch])
    return [refs[cuts[i]:cuts[i + 1]] for i in range(5)] + [refs[cuts[5]:]]


def _matmul(name, a, b, *, ta=False, tb=False, out_dtypes=(F32,), epilogue=None, extras=(), tm=1024, tn=1024, tk=2048, jobs=(),
            n=None, row_extras=(), out_widths=None):
    if ta:
        K, M = a.shape
    else:
        M, K = a.shape
    N = n if n is not None else (b.shape[0] if tb else b.shape[1])
    if ta:
        tk = max(tk, DW_TK)
    tm, tn, tk = _tile(M, tm), _tile(N, tn), _tile(K, tk)
    nk = K // tk
    grid = (M // tm, N // tn, nk)
    dims = TN if ta else (NT if tb else NN)
    n_ex, n_out = len(extras) + len(row_extras), len(out_dtypes)
    widths = list(out_widths) if out_widths is not None else [tn] * n_out
    carried = _Carried(jobs, grid)

    def body(*refs):
        ins, job_ins, out_refs, job_outs, scratch, job_sems = _split_refs(
            refs, 2 + n_ex, len(carried.operands), n_out, len(carried.out_shape), int(nk > 1))
        a_ref, b_ref, ex_refs = ins[0], ins[1], ins[2:]
        k = pl.program_id(2)
        carried.before(job_ins, job_outs, job_sems)

        def finish(acc):
            res = (acc,) if epilogue is None else epilogue(acc, *[e[...] for e in ex_refs])
            for o_ref, r in zip(out_refs, res):
                o_ref[...] = r.astype(o_ref.dtype)

        if nk == 1:
            finish(_dot(a_ref[...], b_ref[...], dims))
        else:
            acc_ref = scratch[0]

            @pl.when(k == 0)
            def _():
                acc_ref[...] = _dot(a_ref[...], b_ref[...], dims)

            @pl.when(k > 0)
            def _():
                acc_ref[...] += _dot(a_ref[...], b_ref[...], dims)

            pl.when(k == nk - 1)(lambda: finish(acc_ref[...]))

        carried.after(job_ins, job_outs, job_sems)

    a_spec = pl.BlockSpec((tk, tm), lambda i, j, k: (k, i)) if ta else pl.BlockSpec((tm, tk), lambda i, j, k: (i, k))
    b_spec = pl.BlockSpec((tn, tk), lambda i, j, k: (j, k)) if tb else pl.BlockSpec((tk, tn), lambda i, j, k: (k, j))
    tile_spec = pl.BlockSpec((tm, tn), lambda i, j, k: (i, j))
    row_specs = [pl.BlockSpec((tm, r.shape[1]), lambda i, j, k: (i, 0)) for r in row_extras]
    outs = pl.pallas_call(
        body,
        name=name,
        grid=grid,
        in_specs=[a_spec, b_spec] + [tile_spec] * len(extras) + row_specs + carried.in_specs,
        out_specs=[pl.BlockSpec((tm, w), lambda i, j, k: (i, j)) for w in widths] + carried.out_specs,
        out_shape=[jax.ShapeDtypeStruct((M, N // tn * w), d) for d, w in zip(out_dtypes, widths)] + carried.out_shape,
        scratch_shapes=[pltpu.VMEM((tm, tn), F32)] * int(nk > 1) + carried.sems,
        input_output_aliases=carried.aliases(2 + n_ex, n_out),
        compiler_params=_params(3),
    )(a, b, *extras, *row_extras, *carried.operands)
    return outs[0] if len(outs) == 1 else outs


def _rowwise(name, fn, row_in, const_in, row_out, acc_out, tm, jobs=()):
    T = row_in[0][0].shape[0]
    n_r, n_c, n_o, n_a = len(row_in), len(const_in), len(row_out), len(acc_out)
    carried = _Carried(jobs, (T // tm,))

    def body(*refs):
        in_refs, job_ins, own_outs, job_outs, _, job_sems = _split_refs(
            refs, n_r + n_c, len(carried.operands), n_o + n_a, len(carried.out_shape), 0)
        out_refs, acc_refs = own_outs[:n_o], own_outs[n_o:]
        carried.before(job_ins, job_outs, job_sems)
        res = fn(*[r[...] for r in in_refs])
        for o_ref, r in zip(out_refs, res[:n_o]):
            o_ref[...] = r.astype(o_ref.dtype)

        @pl.when(pl.program_id(0) == 0)
        def _():
            for a_ref in acc_refs:
                a_ref[...] = jnp.zeros_like(a_ref)

        for a_ref, r in zip(acc_refs, res[n_o:]):
            a_ref[...] += jnp.sum(r.reshape(tm // SUBLANE, SUBLANE, r.shape[-1]), axis=0)
        carried.after(job_ins, job_outs, job_sems)

    in_specs = [pl.BlockSpec((tm, w), functools.partial(lambda i, c: (i, c), c=cb)) for (_, w, cb) in row_in]
    in_specs += [pl.BlockSpec(c.shape, functools.partial(lambda i, nd: (0,) * nd, nd=c.ndim)) for c in const_in]
    row_out = [tuple(ro) + (ro[0], 0)[len(ro) - 2:] for ro in row_out]
    out_specs = [pl.BlockSpec((tm, c), functools.partial(lambda i, cb: (i, cb), cb=cb)) for (c, _, _, cb) in row_out]
    out_specs += [pl.BlockSpec((SUBLANE, c), lambda i: (0, 0)) for c in acc_out]
    out_shape = [jax.ShapeDtypeStruct((T, full), d) for (_, d, full, _) in row_out]
    out_shape += [jax.ShapeDtypeStruct((SUBLANE, c), F32) for c in acc_out]
    return pl.pallas_call(
        body, name=name, grid=(T // tm,), in_specs=in_specs + carried.in_specs, out_specs=out_specs + carried.out_specs,
        out_shape=out_shape + carried.out_shape, scratch_shapes=carried.sems,
        input_output_aliases=carried.aliases(n_r + n_c, n_o + n_a), compiler_params=_params(1),
    )(*[r[0] for r in row_in], *const_in, *carried.operands)


def _ln_stats(z):
    mu = jnp.mean(z, axis=-1, keepdims=True)
    var = jnp.mean(jnp.square(z - mu), axis=-1, keepdims=True)
    rstd = lax.rsqrt(var + EPS)
    return (z - mu) * rstd, rstd


def _ln_bwd(dy, xhat, rstd, g):
    dxh = dy * g
    return rstd * (dxh - jnp.mean(dxh, axis=-1, keepdims=True) - xhat * jnp.mean(dxh * xhat, axis=-1, keepdims=True))


def _rms(x, g):
    return x * lax.rsqrt(jnp.mean(jnp.square(x), axis=-1, keepdims=True) + EPS) * g


def _rope128(x, cos2, sin2):
    return x * cos2 + pltpu.roll(x, RET_DK // 2, 1) * sin2


def _unrope128(dy, cos2, sin2):
    return dy * cos2 + pltpu.roll(dy * sin2, RET_DK // 2, 1)


def _rope64(x, cosp, sina, sinb):
    h = MLA_ROPE // 2
    return x * cosp + pltpu.roll(x, h, 1) * sina + pltpu.roll(x, LANE - h, 1) * sinb


def _unrope64(dy, cosp, sina, sinb):
    h = MLA_ROPE // 2
    return dy * cosp + pltpu.roll(dy * sina, LANE - h, 1) + pltpu.roll(dy * sinb, h, 1)


RET_BLOCK = 256
RET_HEADS_PER_STEP = 4


def _ret_tables(H, B):
    log_g = np.log1p(-np.exp2(-5.0 - np.arange(H, dtype=np.float32))).astype(np.float32)
    idx = np.arange(B, dtype=np.float32)
    dist = np.abs(idx[:, None] - idx[None, :])
    visible = (np.arange(B)[None, :] // CHUNK) <= (np.arange(B)[:, None] // CHUNK)
    dmat = np.exp(log_g[:, None, None] * dist).astype(np.float32) * visible[None].astype(np.float32)
    qd = np.exp(log_g[:, None] * (idx + 1.0)).astype(np.float32)
    kd = np.exp(log_g[:, None] * (B - 1 - idx)).astype(np.float32)
    sd = np.exp(log_g * B).astype(np.float32)
    bc = lambda v: np.ascontiguousarray(np.broadcast_to(v[:, :, None], (H, v.shape[1], LANE)))
    sdb = np.ascontiguousarray(np.broadcast_to(sd[:, None, None], (H, SUBLANE, LANE)))
    return jnp.asarray(dmat), jnp.asarray(bc(qd)), jnp.asarray(bc(kd)), jnp.asarray(sdb)


def _ret_specs(H, B, reverse, nb):
    blk = (lambda b: nb - 1 - b) if reverse else (lambda b: b)
    G = min(RET_HEADS_PER_STEP, H)
    W = G * LANE
    col = lambda off: pl.BlockSpec((B, W), functools.partial(lambda h, b, off: (blk(b), off // G + h), off=off))
    row = pl.BlockSpec((B, LANE), lambda h, b: (blk(b), 0))
    per_head = lambda r: pl.BlockSpec((G, r, LANE), lambda h, b: (h, 0, 0))
    dmat = pl.BlockSpec((G, B, B), lambda h, b: (h, 0, 0))
    return G, col, row, per_head, dmat


def _ret_call(name, body, n_in, n_out, operands, in_specs, out_specs, out_shape, grid, G, jobs, own_alias=None, scratch=()):
    carried = _Carried(jobs, grid)

    def wrapped(*refs):
        ins, job_ins, outs, job_outs, own_scratch, job_sems = _split_refs(
            refs, n_in, len(carried.operands), n_out, len(carried.out_shape), 1 + len(scratch))
        state = own_scratch[0]
        carried.before(job_ins, job_outs, job_sems)

        @pl.when(pl.program_id(1) == 0)
        def _():
            state[...] = jnp.zeros_like(state)

        body(*ins, *outs, *own_scratch)
        carried.after(job_ins, job_outs, job_sems)

    return pl.pallas_call(
        wrapped, name=name, grid=grid,
        in_specs=in_specs + carried.in_specs, out_specs=out_specs + carried.out_specs,
        out_shape=out_shape + carried.out_shape,
        scratch_shapes=[pltpu.VMEM((G, RET_DK, RET_DV), F32)] + list(scratch) + carried.sems,
        input_output_aliases={**(own_alias or {}), **carried.aliases(n_in, n_out)},
        compiler_params=_params(2),
    )(*operands, *carried.operands)


def _retention_fwd(p1, cos2, sin2, gn_g, mix_in, H, jobs=()):
    T = p1.shape[0]
    B = min(RET_BLOCK, T)
    nb = T // B
    scale = RET_DK ** -0.5
    dmat, qd, kd, sd = _ret_tables(H, B)
    G, col, row, per_head, dspec = _ret_specs(H, B, False, nb)

    def body(q_ref, k_ref, v_ref, g_ref, cos_ref, sin_ref, d_ref, qd_ref, kd_ref, sd_ref, gn_ref, _, pre_ref, out_ref, s_ref):
        cos, sin = cos_ref[...], sin_ref[...]
        for g in range(G):
            hl = pl.ds(g * LANE, LANE)
            q = _rope128(q_ref[:, hl], cos, sin)
            k = _rope128(k_ref[:, hl], cos, sin) * scale
            v = v_ref[:, hl]
            s = s_ref[g]
            a = _dot(q, k, NT) * d_ref[g]
            ret = _dot(a, v) + _dot(q * qd_ref[g], s)
            s_ref[g] = sd_ref[g, 0:1, :] * s + _dot(k * kd_ref[g], v, TN)
            pre_ref[:, hl] = ret
            nrm, _ = _ln_stats(ret)
            out_ref[:, hl] = (jax.nn.silu(g_ref[:, hl]) * (nrm * gn_ref[:, hl])).astype(out_ref.dtype)

    out_spec = pl.BlockSpec((B, G * LANE), lambda h, b: (b, h))
    return _ret_call(
        "retention_fwd", body, 12, 2, (p1, p1, p1, p1, cos2, sin2, dmat, qd, kd, sd, gn_g, mix_in),
        [col(0), col(H), col(2 * H), col(3 * H), row, row, dspec, per_head(B), per_head(B), per_head(SUBLANE),
         pl.BlockSpec((1, G * LANE), lambda h, b: (0, h)), ANY],
        [out_spec, out_spec],
        [jax.ShapeDtypeStruct((T, H * RET_DV), F32), jax.ShapeDtypeStruct(mix_in.shape, mix_in.dtype)],
        (H // G, nb), G, jobs, own_alias={11: 1})


def _retention_bwd(p1, dpre, d_p1, cos2, sin2, H, dq_jobs=(), dkv_jobs=()):
    T = p1.shape[0]
    B = min(RET_BLOCK, T)
    nb = T // B
    scale = RET_DK ** -0.5
    dmat, qd, kd, sd = _ret_tables(H, B)

    def dq_body(k_ref, v_ref, do_ref, cos_ref, sin_ref, d_ref, qd_ref, kd_ref, sd_ref, _, dq_ref, s_ref):
        cos, sin = cos_ref[...], sin_ref[...]
        for g in range(G):
            hl = pl.ds(g * LANE, LANE)
            k = _rope128(k_ref[:, hl], cos, sin) * scale
            v, do, s = v_ref[:, hl], do_ref[:, hl], s_ref[g]
            da = _dot(do, v, NT) * d_ref[g]
            dq = _dot(da, k) + _dot(do, s, NT) * qd_ref[g]
            dq_ref[:, hl] = _unrope128(dq, cos, sin).astype(dq_ref.dtype)
            s_ref[g] = sd_ref[g, 0:1, :] * s + _dot(k * kd_ref[g], v, TN)

    G, col, row, per_head, dspec = _ret_specs(H, B, False, nb)
    dq = _ret_call(
        "retention_bwd_dq", dq_body, 10, 1, (p1, p1, dpre, cos2, sin2, dmat, qd, kd, sd, d_p1),
        [col(H), col(2 * H), col(0), row, row, dspec, per_head(B), per_head(B), per_head(SUBLANE), ANY],
        [pl.BlockSpec((B, G * LANE), lambda h, b: (b, h))], [jax.ShapeDtypeStruct(d_p1.shape, d_p1.dtype)],
        (H // G, nb), G, dq_jobs, own_alias={9: 0})

    W = G * LANE

    def dkv_body(q_ref, k_ref, v_ref, do_ref, cos_ref, sin_ref, d_ref, qd_ref, kd_ref, sd_ref, _, out_ref, g_ref, buf, sems):
        step = pl.program_id(0) * nb + pl.program_id(1)
        slot = step % 2

        def writes(slot, step):
            h, b = step // nb, step % nb
            rows = pl.ds((nb - 1 - b) * B, B)
            return [pltpu.make_async_copy(buf.at[slot, j], out_ref.at[rows, pl.ds(((j + 1) * H + h * G) * LANE, W)],
                                          sems.at[slot, j]) for j in range(2)]

        @pl.when(step >= 2)
        def _():
            for cp in writes(slot, step - 2):
                cp.wait()

        dk_ref, dv_ref = buf.at[slot, 0], buf.at[slot, 1]
        cos, sin = cos_ref[...], sin_ref[...]
        for g in range(G):
            hl = pl.ds(g * LANE, LANE)
            q = _rope128(q_ref[:, hl], cos, sin)
            k = _rope128(k_ref[:, hl], cos, sin) * scale
            v, do, st = v_ref[:, hl], do_ref[:, hl], g_ref[g]
            a = _dot(q, k, NT) * d_ref[g]
            da = _dot(do, v, NT) * d_ref[g]
            dv = _dot(a, do, TN) + _dot(k * kd_ref[g], st)
            dk = _dot(da, q, TN) + _dot(v, st, NT) * kd_ref[g]
            dv_ref[:, hl] = dv.astype(dv_ref.dtype)
            dk_ref[:, hl] = _unrope128(dk * scale, cos, sin).astype(dk_ref.dtype)
            g_ref[g] = sd_ref[g, 0:1, :] * st + _dot(q * qd_ref[g], do, TN)

        for cp in writes(slot, step):
            cp.start()

        @pl.when(step == (H // G) * nb - 1)
        def _():
            for cp in writes(1 - slot, step - 1) + writes(slot, step):
                cp.wait()

    G, col, row, per_head, dspec = _ret_specs(H, B, True, nb)
    dkv = _ret_call(
        "retention_bwd_dkv", dkv_body, 11, 1, (p1, p1, p1, dpre, cos2, sin2, dmat, qd, kd, sd, dq[0]),
        [col(0), col(H), col(2 * H), col(0), row, row, dspec, per_head(B), per_head(B), per_head(SUBLANE), ANY],
        [ANY], [jax.ShapeDtypeStruct(d_p1.shape, d_p1.dtype)],
        (H // G, nb), G, dkv_jobs, own_alias={10: 0},
        scratch=[pltpu.VMEM((2, 2, B, W), BF16), pltpu.SemaphoreType.DMA((2, 2))])
    return dq, dkv


ATT_BLOCK = 512


ATT_ROW_CHUNKS = 2
ATT_BWD_ROW_CHUNKS = 1
ATT_HEADS_PER_STEP = 8
ATT_BWD_HEADS_PER_STEP = 2
LOG2E = 1.4426950408889634


def _chunk_mask(rows, row0, cols):
    qi = (lax.broadcasted_iota(jnp.int32, (rows, cols), 0) + row0) // CHUNK
    kj = lax.broadcasted_iota(jnp.int32, (rows, cols), 1) // CHUNK
    return kj <= qi


def _attention_fwd(qh, kh, vx, H, jobs=()):
    T = qh.shape[0]
    t = min(ATT_BLOCK, T)
    n = T // t
    r = t // ATT_ROW_CHUNKS
    G = min(ATT_HEADS_PER_STEP, H)
    P = MLA_HEAD_PAD
    scale = (MLA_NOPE + MLA_ROPE) ** -0.5
    c2 = scale * LOG2E
    carried = _Carried(jobs, (H // G, n, n))

    def body(*refs):
        (q_ref, k_ref, v_ref), job_ins, (o_ref, ob_ref, lse_ref), job_outs, (m_ref, acc_ref), job_sems = _split_refs(
            refs, 3, len(carried.operands), 3, len(carried.out_shape), 2)
        i, j = pl.program_id(1), pl.program_id(2)
        carried.before(job_ins, job_outs, job_sems)

        @pl.when(j == 0)
        def _():
            m_ref[...] = jnp.full_like(m_ref, -jnp.inf)
            acc_ref[...] = jnp.zeros_like(acc_ref)

        def update(masked):
            for g in range(G):
                hc = pl.ds(g * P, P)
                for ch in range(ATT_ROW_CHUNKS):
                    rs = pl.ds(ch * r, r)
                    kc = (ch + 1) * r if masked else t
                    ks = pl.ds(0, kc)
                    s = _dot(q_ref[rs, hc], k_ref[ks, hc], NT)
                    if masked:
                        s = jnp.where(_chunk_mask(r, ch * r, kc), s, -jnp.inf)
                    m_prev = m_ref[rs, pl.ds(g * LANE, LANE)]
                    m_new = jnp.maximum(m_prev, jnp.max(s, axis=-1, keepdims=True))
                    alpha = jnp.exp2((m_prev - m_new) * c2)
                    p = jnp.exp2((s - jnp.tile(m_new, (1, kc // LANE))) * c2)
                    acc_ref[rs, hc] = jnp.tile(alpha, (1, 2)) * acc_ref[rs, hc] + _dot(p, v_ref[ks, hc])
                    m_ref[rs, pl.ds(g * LANE, LANE)] = m_new

        pl.when(j < i)(functools.partial(update, False))
        pl.when(j == i)(functools.partial(update, True))

        @pl.when(j == n - 1)
        def _():
            for g in range(G):
                acc = acc_ref[:, pl.ds(g * P, P)]
                l = acc[:, MLA_DV:]
                o = acc[:, :MLA_DV] / l
                o_ref[:, pl.ds(g * MLA_DV, MLA_DV)] = o
                ob_ref[:, pl.ds(g * MLA_DV, MLA_DV)] = o.astype(ob_ref.dtype)
                lse_ref[g] = m_ref[:, pl.ds(g * LANE, LANE)] * scale + jnp.log(l)

        carried.after(job_ins, job_outs, job_sems)

    kclamp = lambda i, j: jnp.minimum(j, i)
    o_spec = pl.BlockSpec((t, G * MLA_DV), lambda h, i, j: (i, h))
    mix_spec = pl.BlockSpec((t, G * MLA_DV), lambda h, i, j: (i, H // G + h))
    return pl.pallas_call(
        body, name="attention_fwd", grid=(H // G, n, n),
        in_specs=[pl.BlockSpec((t, G * P), lambda h, i, j: (i, h)),
                  pl.BlockSpec((t, G * P), lambda h, i, j: (kclamp(i, j), h)),
                  pl.BlockSpec((t, G * P), lambda h, i, j: (kclamp(i, j), h))] + carried.in_specs,
        out_specs=[o_spec, mix_spec, pl.BlockSpec((G, t, LANE), lambda h, i, j: (h, i, 0))] + carried.out_specs,
        out_shape=[jax.ShapeDtypeStruct((T, H * MLA_DV), F32), jax.ShapeDtypeStruct((T, 2 * H * MLA_DV), BF16),
                   jax.ShapeDtypeStruct((H, T, LANE), F32)] + carried.out_shape,
        scratch_shapes=[pltpu.VMEM((t, G * LANE), F32), pltpu.VMEM((t, G * P), F32)] + carried.sems,
        input_output_aliases=carried.aliases(3, 3),
        compiler_params=_params(3),
    )(qh, kh, vx, *carried.operands)


def _attention_bwd(qh, kh, vx, o, lse, dmix, H, jobs=()):
    T = qh.shape[0]
    t = min(ATT_BLOCK, T)
    n = T // t
    r = t // ATT_BWD_ROW_CHUNKS
    G = min(ATT_BWD_HEADS_PER_STEP, H)
    P = MLA_HEAD_PAD
    scale = (MLA_NOPE + MLA_ROPE) ** -0.5
    c2 = scale * LOG2E
    carried = _Carried(jobs, (H // G, n, n))

    def body(*refs):
        ((q_ref, k_ref, v_ref, do_ref, o_ref, lse_ref), job_ins, (dq_ref, dkn_ref, dkr_ref, dv_ref), job_outs,
         (dk_acc, dv_acc), job_sems) = _split_refs(refs, 6, len(carried.operands), 4, len(carried.out_shape), 2)
        j, i = pl.program_id(1), pl.program_id(2)
        carried.before(job_ins, job_outs, job_sems)

        @pl.when(jnp.logical_and(j == 0, i == 0))
        def _():
            dq_ref[...] = jnp.zeros_like(dq_ref)

        @pl.when(i == j)
        def _():
            dk_acc[...] = jnp.zeros_like(dk_acc)
            dv_acc[...] = jnp.zeros_like(dv_acc)

        def update(masked):
            for g in range(G):
                hc, hv = pl.ds(g * P, P), pl.ds(g * MLA_DV, MLA_DV)
                for ch in range(ATT_BWD_ROW_CHUNKS):
                    rs = pl.ds(ch * r, r)
                    kc = (ch + 1) * r if masked else t
                    ks = pl.ds(0, kc)
                    k, v = k_ref[ks, hc], v_ref[ks, pl.ds(g * P, MLA_DV)]
                    q, do = q_ref[rs, hc], do_ref[rs, hv]
                    s = _dot(q, k, NT)
                    if masked:
                        s = jnp.where(_chunk_mask(r, ch * r, kc), s, -jnp.inf)
                    p = jnp.exp2(s * c2 - jnp.tile(lse_ref[g, rs, :] * LOG2E, (1, kc // LANE)))
                    dv_acc[ks, hv] += _dot(p, do, TN)
                    dp = _dot(do, v, NT)
                    delta = jnp.sum(do * o_ref[rs, hv], axis=-1, keepdims=True)
                    ds = p * (dp - delta)
                    rows = pl.ds(pl.multiple_of(i * t, t) + ch * r, r)
                    dq_ref[rows, hc] += _dot(ds, k)
                    dk_acc[ks, hc] += _dot(ds, q, TN)

        pl.when(i > j)(functools.partial(update, False))
        pl.when(i == j)(functools.partial(update, True))

        @pl.when(i == n - 1)
        def _():
            for g in range(G):
                dk = dk_acc[:, pl.ds(g * P, P)] * scale
                dkn_ref[:, pl.ds(g * LANE, LANE)] = dk[:, :MLA_NOPE].astype(dkn_ref.dtype)
                dkr_ref[:, pl.ds(g * LANE, LANE)] = dk[:, MLA_NOPE:]
            dv_ref[...] = dv_acc[...].astype(dv_ref.dtype)

        carried.after(job_ins, job_outs, job_sems)

    qclamp = lambda j, i: jnp.maximum(i, j)
    kv_out = pl.BlockSpec((t, G * LANE), lambda h, j, i: (j, h))
    q_in = pl.BlockSpec((t, G * MLA_DV), lambda h, j, i: (qclamp(j, i), h))
    return pl.pallas_call(
        body, name="attention_bwd", grid=(H // G, n, n),
        in_specs=[pl.BlockSpec((t, G * P), lambda h, j, i: (qclamp(j, i), h)),
                  pl.BlockSpec((t, G * P), lambda h, j, i: (j, h)),
                  pl.BlockSpec((t, G * P), lambda h, j, i: (j, h)),
                  pl.BlockSpec((t, G * MLA_DV), lambda h, j, i: (qclamp(j, i), H // G + h)),
                  q_in,
                  pl.BlockSpec((G, t, LANE), lambda h, j, i: (h, qclamp(j, i), 0))] + carried.in_specs,
        out_specs=[pl.BlockSpec((T, G * P), lambda h, j, i: (0, h)), kv_out, kv_out, kv_out] + carried.out_specs,
        out_shape=[jax.ShapeDtypeStruct((T, H * P), F32), jax.ShapeDtypeStruct((T, H * LANE), BF16),
                   jax.ShapeDtypeStruct((T, H * LANE), F32), jax.ShapeDtypeStruct((T, H * LANE), BF16)] + carried.out_shape,
        scratch_shapes=[pltpu.VMEM((t, G * P), F32), pltpu.VMEM((t, G * MLA_DV), F32)] + carried.sems,
        input_output_aliases=carried.aliases(6, 4),
        compiler_params=_params(3),
    )(qh, kh, vx, dmix, o, lse, *carried.operands)


def _adamw_math(w, g, m, v):
    m = ADAM_B1 * m + (1.0 - ADAM_B1) * g
    v = ADAM_B2 * v + (1.0 - ADAM_B2) * jnp.square(g)
    m_hat = m / (1.0 - ADAM_B1 ** ADAM_STEP)
    v_hat = v / (1.0 - ADAM_B2 ** ADAM_STEP)
    delta = -ADAM_LR * (m_hat / (jnp.sqrt(v_hat) + ADAM_EPS) + ADAM_WD * w)
    return delta, m, v


def _adamw(name, chip_sums, my_chip, landed, w, m, v, jobs=()):
    R, C = w.shape
    tr, tc, tiles = _tile_2d(R, C, ADAMW_TILE_ELEMS)
    at = _tile_index(tr, tc, C)
    nl = len(landed)
    assert sum(l.shape[0] for l in landed) == N_CHIP - 1
    carried = _Carried(jobs, (tiles,))

    def body(q_ref, *refs):
        (own_ref, *rest), job_ins, (g_out, d_out, m_out, v_out), job_outs, _, job_sems = _split_refs(
            refs, nl + 4, len(carried.operands), 4, len(carried.out_shape), 0)
        l_refs, (w_ref, m_ref, v_ref) = rest[:nl], rest[nl:]
        carried.before(job_ins, job_outs, job_sems)
        g = own_ref[...].astype(F32)
        for l_ref in l_refs:
            for s in range(l_ref.shape[0]):
                g = g + l_ref[s].astype(F32)
        delta, mn, vn = _adamw_math(w_ref[...], g, m_ref[...], v_ref[...])
        g_out[...] = g
        d_out[...] = delta
        m_out[...] = mn
        v_out[...] = vn
        carried.after(job_ins, job_outs, job_sems)

    spec = pl.BlockSpec((tr, tc), lambda i, q_ref: at(i))
    own_spec = pl.BlockSpec((None, tr, tc), lambda i, q_ref: (q_ref[0], *at(i)))
    l_specs = [pl.BlockSpec((l.shape[0], tr, tc), lambda i, q_ref: (0, *at(i))) for l in landed]
    return pl.pallas_call(
        body, name=name,
        grid_spec=pltpu.PrefetchScalarGridSpec(
            num_scalar_prefetch=1, grid=(tiles,),
            in_specs=[own_spec] + l_specs + [spec, spec, spec] + carried.in_specs,
            out_specs=[spec] * 4 + carried.out_specs, scratch_shapes=carried.sems),
        out_shape=[jax.ShapeDtypeStruct((R, C), F32)] * 4 + carried.out_shape,
        input_output_aliases={1 + i: o for i, o in carried.aliases(nl + 4, 4).items()},
        compiler_params=_params(1),
    )(my_chip, chip_sums, *landed, w, m, v, *carried.operands)


def _adamw_small(landed, w, m, v):
    P = w.shape[1]

    def body(l_ref, w_ref, m_ref, v_ref, g_out, d_out, m_out, v_out, loss_out):
        acc = l_ref[0]
        for s in range(1, N_DEV):
            acc = acc + l_ref[s]
        total = jnp.sum(acc, axis=0, keepdims=True)
        g = total[:, :P]
        delta, mn, vn = _adamw_math(w_ref[...], g, m_ref[...], v_ref[...])
        g_out[...] = g
        d_out[...] = delta
        m_out[...] = mn
        v_out[...] = vn
        loss_out[...] = total[:, P:]

    return pl.pallas_call(
        body, name="adamw_replicated",
        out_shape=[jax.ShapeDtypeStruct((1, P), F32)] * 4 + [jax.ShapeDtypeStruct((1, LANE), F32)],
    )(landed, w, m, v)


def _rope_rows():
    inv_r = ROPE_BASE ** (-jnp.arange(0, RET_DK, 2, dtype=F32) / RET_DK)
    inv_m = ROPE_BASE ** (-jnp.arange(0, MLA_ROPE, 2, dtype=F32) / MLA_ROPE)
    h = MLA_ROPE // 2
    one, zero = jnp.ones((h,), F32), jnp.zeros((h,), F32)
    pad = jnp.zeros((LANE - MLA_ROPE,), F32)
    rows = [jnp.concatenate([inv_r, inv_r]), jnp.concatenate([-jnp.ones((RET_DK // 2,), F32), jnp.ones((RET_DK // 2,), F32)]),
            jnp.concatenate([inv_m, inv_m, pad]), jnp.concatenate([one, one, pad]),
            jnp.concatenate([zero, one, pad]), jnp.concatenate([-one, zero, pad])]
    return jnp.stack(rows + [jnp.zeros((LANE,), F32)] * (SUBLANE - len(rows)))


def _prep_tile(xt, p, rows):
    ang2, angp = p * rows[0:1], p * rows[2:3]
    sp = jnp.sin(angp)
    return (xt, jnp.cos(ang2), jnp.sin(ang2) * rows[1:2], jnp.cos(angp) * rows[3:4], sp * rows[4:5], sp * rows[5:6])


def kernel(x, positions, w_in, q_norm_g, w_uq, kv_norm_g, w_uk, w_uv, ret_gn_g, w_out, ln1_g, ln1_b, w_up, w_down, ln2_g, ln2_b, loss_target, m_w_in, m_q_norm_g, m_w_uq, m_kv_norm_g, m_w_uk, m_w_uv, m_ret_gn_g, m_w_out, m_ln1_g, m_ln1_b, m_w_up, m_w_down, m_ln2_g, m_ln2_b, v_w_in, v_q_norm_g, v_w_uq, v_kv_norm_g, v_w_uk, v_w_uv, v_ret_gn_g, v_w_out, v_ln1_g, v_ln1_b, v_w_up, v_w_down, v_ln2_g, v_ln2_b):
    T, D = x.shape[1], x.shape[2]
    H = D // 256
    RW = H * RET_DV
    x = x[0]
    tgt = loss_target[0]
    alpha = DEEPNORM_ALPHA
    in_width = w_in.shape[2] * N_DEV
    mla_in = in_width - 4 * RW
    mla_in_pad = -(-mla_in // 512) * 512
    f_uq = MLA_NOPE + MLA_ROPE

    names = ["w_in", "w_uq", "w_uk", "w_uv", "w_out", "w_up", "w_down"]
    big = dict(zip(names, [w_in[0].T, w_uq[0], w_uk[0], w_uv[0], w_out[0], w_up[0], w_down[0]]))
    kind = dict(zip(names, ["blk", "blk", "cols", "cols", "rows", "cols", "rows"]))
    half = {n: w.astype(BF16) for n, w in big.items()}

    def gather(ns, **parts):
        return _GatherJob([half[n] for n in ns], [kind[n] for n in ns], 0.55, 0.9, **parts)

    tr = min(256, T)
    xb, cos2, sin2, cosp, sina, sinb, g_in = _rowwise(
        "prep", _prep_tile, [(x, D, 0), (positions[0].astype(F32)[:, None], 1, 0)], [_rope_rows()],
        [(D, BF16)] + [(LANE, F32)] * 5, [], tr, jobs=[gather(["w_in"])])
    win = g_in.reshape(in_width, D)
    w1, n1 = win, 4 * RW
    w2 = jnp.pad(win[4 * RW:], ((0, mla_in_pad - mla_in), (0, 0)))

    first, second, near, far = (0, 2), (1, 2), (CHIP_X, CHIP_Y), (CHIP_DIAGONAL,)
    p1, g_uq, wuk, wuv, wout = _matmul("proj_ret", xb, w1, tb=True, n=n1,
                                       jobs=[gather(["w_uq", "w_uk", "w_uv", "w_out"])])
    wuq = jnp.transpose(g_uq, (1, 0, 2)).reshape(MLA_Q_RANK, H, f_uq)
    wuq = jnp.pad(wuq, ((0, 0), (0, 0), (0, MLA_HEAD_PAD - f_uq))).reshape(MLA_Q_RANK, H * MLA_HEAD_PAD)
    wukv = jnp.concatenate([wuk, wuv], axis=1)
    p2 = _matmul("proj_mla", xb, w2, tb=True)
    c_kv, c_kr = MLA_Q_RANK, MLA_Q_RANK + MLA_KV_RANK

    def mla_prep(p, cp, sa, sb, gq, gkv):
        cqn = _rms(p[:, :c_kv], gq)
        ckvn = _rms(p[:, c_kv:c_kr], gkv)
        return cqn, ckvn, _rope64(p[:, c_kr:c_kr + LANE], cp, sa, sb)

    cqn, ckvn, krope = _rowwise(
        "mla_prep", mla_prep, [(p2, mla_in_pad, 0), (cosp, LANE, 0), (sina, LANE, 0), (sinb, LANE, 0)],
        [q_norm_g, kv_norm_g], [(MLA_Q_RANK, BF16), (MLA_KV_RANK, BF16), (LANE, BF16)], [], tr)
    def q_heads(acc, cp, sa, sb):
        parts = []
        for o in range(0, acc.shape[1], MLA_HEAD_PAD):
            parts += [acc[:, o:o + MLA_NOPE], _rope64(acc[:, o + MLA_NOPE:o + MLA_HEAD_PAD], cp, sa, sb)]
        return (jnp.concatenate(parts, axis=1),)

    def paired(acc, second):
        parts = []
        for o in range(0, acc.shape[1], LANE):
            parts += [acc[:, o:o + LANE], second.astype(acc.dtype)]
        return (jnp.concatenate(parts, axis=1),)

    qh = _matmul("mla_q", cqn, wuq, out_dtypes=(BF16,), epilogue=q_heads, row_extras=(cosp, sina, sinb))
    heads_tile = min(RW, 1024)
    kh = _matmul("mla_k", ckvn, wuk, out_dtypes=(BF16,), epilogue=paired, row_extras=(krope,), tn=heads_tile,
                 out_widths=[2 * heads_tile])
    vx = _matmul("mla_v", ckvn, wuv, out_dtypes=(BF16,), tn=heads_tile, out_widths=[2 * heads_tile],
                 epilogue=lambda acc: paired(acc, jnp.ones((acc.shape[0], MLA_DV), F32)))
    whole = (0, 1)
    att_o, mix_half, lse, wup_a = _attention_fwd(
        qh, kh, vx, H, jobs=[gather(["w_up"], pieces=[(first, near, True, 0.45), (second, near, True, 0.75)])])
    ret_pre, mixin, wup = _retention_fwd(
        p1, cos2, sin2, ret_gn_g, mix_half, H,
        jobs=[gather(["w_up"], pieces=[(first, far, False, 0.55), (second, far, False, 0.9)], base=[wup_a])])
    mix, wdown_a = _matmul("mix_out", mixin, wout,
                           jobs=[gather(["w_down"], pieces=[((0, 4), near, True, 0.5), ((1, 4), near, True, 0.9)])])

    def ln1_fwd(xt, mt, g, b):
        z = alpha * xt + mt
        xhat, _ = _ln_stats(z)
        y = xhat * g + b
        return z, y, y

    z1, x1, x1b, wdown_b = _rowwise("ln1", ln1_fwd, [(x, D, 0), (mix, D, 0)], [ln1_g, ln1_b],
                                    [(D, F32), (D, F32), (D, BF16)], [], tr,
                                    jobs=[gather(["w_down"], pieces=[((2, 4), near, True, 0.5), ((3, 4), near, True, 0.9)],
                                                 base=[wdown_a])])

    def relu2(acc):
        r = jnp.maximum(acc, 0.0)
        return r * r, r

    a2b, rb, wdown = _matmul("mlp_up", x1b, wup, out_dtypes=(BF16, BF16), epilogue=relu2,
                             jobs=[gather(["w_down"], pieces=[(whole, far, False, 0.6)], base=[wdown_b])])
    hmlp = _matmul("mlp_down", a2b, wdown)

    def ln2_loss(x1t, ht, tt, g, b):
        xhat, rstd = _ln_stats(alpha * x1t + ht)
        err = xhat * g + b - tt
        dy = err / D
        dz = _ln_bwd(dy, xhat, rstd, g)
        lrow = 0.5 * jnp.mean(jnp.square(err), axis=-1, keepdims=True)
        return dz, dz, dy * xhat, dy, jnp.broadcast_to(lrow, (lrow.shape[0], LANE))

    dz2, dz2b, dg2, db2, lpart = _rowwise("ln2_loss", ln2_loss, [(x1, D, 0), (hmlp, D, 0), (tgt, D, 0)], [ln2_g, ln2_b],
                                          [(D, F32), (D, BF16)], [D, D, LANE], tr)

    def pair(grads):
        ns = list(grads)
        return _PairJob([grads[n] for n in ns], [kind[n] for n in ns], [big[n].shape for n in ns])

    my_core = lax.axis_index("c").astype(jnp.int32).reshape(1)
    my_chip = (2 * lax.axis_index("x") + lax.axis_index("y")).astype(jnp.int32).reshape(1)
    chip_sums = {}

    def pair_sums(grads, theirs):
        for n, t in zip(grads, theirs):
            chip_sums[n] = _pair_sum("pair_sum_" + n, grads[n], kind[n], big[n].shape, t, my_core)
        return {n: chip_sums[n] for n in grads}

    def chip(sums, rels=CHIP_RELS, by_source=None):
        ns = list(sums)
        return _ChipJob([sums[n] for n in ns], [big[n].shape for n in ns], rels, by_source)

    landed = {}
    da = _matmul("mlp_down_dx", dz2b, wdown, tb=True, out_dtypes=(BF16,), extras=(rb,),
                 epilogue=lambda acc, r: (acc * (2.0 * r.astype(F32)),))
    g_down = {"w_down": _matmul("mlp_down_dw", a2b, dz2b, ta=True, out_dtypes=(BF16,))}
    g_wup, *pr = _matmul("mlp_up_dw", x1b, da, ta=True, out_dtypes=(BF16,), jobs=[pair(g_down)])
    sums_down = pair_sums(g_down, pr)
    g_up = {"w_up": g_wup}
    dx1m, l_a, *pr = _matmul("mlp_up_dx", da, wup, tb=True, jobs=[chip(sums_down, CHIP_RELS[:2]), pair(g_up)])
    sums_up = pair_sums(g_up, pr)

    def ln1_bwd(dm, dzz, z, g):
        xhat, rstd = _ln_stats(z)
        dy = dm + alpha * dzz
        dz = _ln_bwd(dy, xhat, rstd, g)
        return dz, dz, dy * xhat, dy

    dz1, dz1b, dg1, db1 = _rowwise("ln1_bwd", ln1_bwd, [(dx1m, D, 0), (dz2, D, 0), (z1, D, 0)], [ln1_g],
                                   [(D, F32), (D, BF16)], [D, D], tr)
    g_wout = _matmul("mix_out_dw", mixin, dz1b, ta=True, out_dtypes=(BF16,))
    dmix = _matmul("mix_out_dx", dz1b, wout, tb=True)

    dqh, dkn, dkr, dvv, l_b, l_c = _attention_bwd(qh, kh, vx, att_o, lse, dmix, H,
                                                  jobs=[chip(sums_down, CHIP_RELS[2:]), chip(sums_up)])
    landed["w_down"], landed["w_up"] = [l_a, l_b], [l_c]
    att_scale = (MLA_NOPE + MLA_ROPE) ** -0.5

    def mla_heads_bwd(dq, dkrh, cp, sa, sb):
        parts, acc = [], dkrh[:, :LANE]
        dq = dq * att_scale
        for h in range(H):
            o = h * MLA_HEAD_PAD
            parts += [dq[:, o:o + MLA_NOPE], _unrope64(dq[:, o + MLA_NOPE:o + MLA_HEAD_PAD], cp, sa, sb)]
            if h:
                acc = acc + dkrh[:, h * LANE:(h + 1) * LANE]
        return jnp.concatenate(parts, axis=1), _unrope64(acc, cp, sa, sb)

    dqb, dkr128 = _rowwise(
        "mla_heads_bwd", mla_heads_bwd,
        [(dqh, H * MLA_HEAD_PAD, 0), (dkr, H * LANE, 0), (cosp, LANE, 0), (sina, LANE, 0), (sinb, LANE, 0)],
        [], [(H * MLA_HEAD_PAD, BF16), (LANE, F32)], [], tr)
    dcqn = _matmul("mla_q_dx", dqb, wuq, tb=True)
    g_wuq = _matmul("mla_q_dw", cqn, dqb, ta=True, out_dtypes=(BF16,))
    dkvb = jnp.concatenate([dkn, dvv], axis=1)
    dckvn = _matmul("mla_kv_dx", dkvb, wukv, tb=True)
    g_wukv = _matmul("mla_kv_dw", ckvn, dkvb, ta=True, out_dtypes=(BF16,))

    def rms_bwd(c, dy, g):
        rstd = lax.rsqrt(jnp.mean(jnp.square(c), axis=-1, keepdims=True) + EPS)
        dyg = dy * g
        dc = rstd * (dyg - c * (rstd * rstd) * jnp.mean(dyg * c, axis=-1, keepdims=True))
        return dc, dy * c * rstd

    def mla_prep_bwd(p, dq_, dkv_, dkr_, gq, gkv):
        dcq, dgq_ = rms_bwd(p[:, :c_kv], dq_, gq)
        dckv, dgkv_ = rms_bwd(p[:, c_kv:c_kr], dkv_, gkv)
        pad = jnp.zeros((p.shape[0], mla_in_pad - c_kr - LANE), F32)
        return jnp.concatenate([dcq, dckv, dkr_, pad], axis=1), dgq_, dgkv_

    d_p2, dgq, dgkv = _rowwise(
        "mla_prep_bwd", mla_prep_bwd, [(p2, mla_in_pad, 0), (dcqn, MLA_Q_RANK, 0), (dckvn, MLA_KV_RANK, 0), (dkr128, LANE, 0)],
        [q_norm_g, kv_norm_g], [(mla_in_pad, BF16)], [MLA_Q_RANK, MLA_KV_RANK], tr)

    def gate_bwd(pre, gate, dout, gn):
        dpre, dgate, dgn = [], [], []
        for h in range(H):
            sl = slice(h * RET_DV, (h + 1) * RET_DV)
            nrm, rstd = _ln_stats(pre[:, sl])
            gate_h, do_h, gn_h = gate[:, sl], dout[:, sl], gn[:, sl]
            sg = jax.nn.sigmoid(gate_h)
            silu = gate_h * sg
            dgate.append(do_h * (nrm * gn_h) * (sg * (1.0 + gate_h * (1.0 - sg))))
            dn = do_h * silu
            dgn.append(dn * nrm)
            dpre.append(_ln_bwd(dn, nrm, rstd, gn_h))
        return jnp.concatenate(dpre, axis=1), jnp.concatenate(dgate, axis=1), jnp.concatenate(dgn, axis=1)

    dpre, d_p1, dgn = _rowwise("ret_gate_bwd", gate_bwd, [(ret_pre, RW, 0), (p1, RW, 3), (dmix, RW, 0)], [ret_gn_g],
                               [(RW, F32), (RW, BF16, 4 * RW, 3)], [RW], tr)
    g_uq = g_wuq.reshape(MLA_Q_RANK, H, MLA_HEAD_PAD)[:, :, :f_uq].reshape(MLA_Q_RANK, N_DEV, H * f_uq // N_DEV)
    mid_grads = {"w_uq": jnp.transpose(g_uq, (1, 0, 2)), "w_uk": g_wukv[:, :RW], "w_uv": g_wukv[:, RW:], "w_out": g_wout}
    (_, *pr), (d_p1,) = _retention_bwd(p1, dpre, d_p1, cos2, sin2, H, dq_jobs=[pair(mid_grads)])
    sums_mid = pair_sums(mid_grads, pr)

    g_w1, *l_a = _matmul("proj_ret_dw", d_p1, xb, ta=True, out_dtypes=(BF16,), jobs=[chip(sums_mid)])
    for n, a_ in zip(mid_grads, l_a):
        landed[n] = [a_]
    g_w2 = _matmul("proj_mla_dw", d_p2, xb, ta=True, out_dtypes=(BF16,))

    g_win = {"w_in": jnp.concatenate([g_w1, g_w2[:mla_in]], axis=0).reshape(N_DEV, in_width // N_DEV, D)}
    small = jnp.concatenate([dgq, dgkv, dgn, dg1, db1, dg2, db2, lpart], axis=1)
    dx_a, *pr = _matmul("proj_mla_dx", d_p2, w2, extras=(dz1,), epilogue=lambda acc, d: (acc + alpha * d,), jobs=[pair(g_win)])
    sums_in = pair_sums(g_win, pr)
    grad_x, l_a, small_landed = _matmul("proj_ret_dx", d_p1, w1, extras=(dx_a,), epilogue=lambda acc, d: (acc + d,),
                                        jobs=[chip(sums_in, CHIP_RELS[:2], by_source=small)])

    moments = dict(zip(names, zip([m_w_in, m_w_uq, m_w_uk, m_w_uv, m_w_out, m_w_up, m_w_down],
                                  [v_w_in, v_w_uq, v_w_uk, v_w_uv, v_w_out, v_w_up, v_w_down])))
    res = {}

    def adamw(n, jobs=()):
        as_held = (lambda a: a[0].T) if n == "w_in" else (lambda a: a[0])
        as_given = (lambda r: r.T[None]) if n == "w_in" else (lambda r: r[None])
        out = _adamw("adamw_" + n, chip_sums[n], my_chip, landed[n], big[n], as_held(moments[n][0]), as_held(moments[n][1]),
                     jobs=jobs)
        res[n] = [as_given(r) for r in out[:4]]
        return out[4:]

    (l_b,) = adamw("w_out", jobs=[chip(sums_in, CHIP_RELS[2:])])
    landed["w_in"] = [l_a, l_b]
    for n in names:
        if n != "w_out":
            adamw(n)

    small_names = ["q_norm_g", "kv_norm_g", "ret_gn_g", "ln1_g", "ln1_b", "ln2_g", "ln2_b"]
    small_w = [q_norm_g, kv_norm_g, ret_gn_g, ln1_g, ln1_b, ln2_g, ln2_b]
    small_m = [m_q_norm_g, m_kv_norm_g, m_ret_gn_g, m_ln1_g, m_ln1_b, m_ln2_g, m_ln2_b]
    small_v = [v_q_norm_g, v_kv_norm_g, v_ret_gn_g, v_ln1_g, v_ln1_b, v_ln2_g, v_ln2_b]
    cat = lambda arrs: jnp.concatenate(arrs, axis=1)
    *sres, loss_lanes = _adamw_small(small_landed, cat(small_w), cat(small_m), cat(small_v))
    loss = loss_lanes[0, 0]
    off = 0
    for n, w in zip(small_names, small_w):
        res[n] = [r[:, off:off + w.shape[1]] for r in sres]
        off += w.shape[1]

    order = ["w_in", "q_norm_g", "w_uq", "kv_norm_g", "w_uk", "w_uv", "ret_gn_g", "w_out", "ln1_g", "ln1_b",
             "w_up", "w_down", "ln2_g", "ln2_b"]
    outs = [loss, grad_x[None]]
    for k in range(4):
        outs += [res[n][k] for n in order]
    return tuple(outs)
```
